```python
import math
import jax, jax.numpy as jnp
from jax import lax
import numpy as np

D_MODEL = 2048
BATCH = 8
SEQ = 4096
DEPTH = 2

N_META = 16
CHUNK = 128
SUB = 16
EPS = 1e-6

RET_HEADS = 8
RET_DK = 128
RET_DV = 256
RET_QK = RET_HEADS * RET_DK
RET_W = RET_HEADS * RET_DV
ROPE_BASE = 10000.0

S5_W = 1024
S5_GH = 16
S5_G = S5_W // S5_GH
S5_P = 64
DT_MIN = 1e-3
DT_MAX = 1e-1

GLA_HEADS = 4
GLA_DK = 256
GLA_DV = 512
GLA_QK = GLA_HEADS * GLA_DK
GLA_W = GLA_HEADS * GLA_DV
GLA_RANK = 16
GLA_TAU = 16.0

IN_AB = 2 * RET_QK + 2 * RET_W + 2 * S5_W
OUT_AB = RET_W + S5_W
IN_C = 2 * GLA_QK + 2 * GLA_W + GLA_RANK
N_EVEN = (DEPTH + 1) // 2
N_ODD = DEPTH // 2

kernel_name = "hybrid_retention_s5_gla_meta"


def _rmsnorm(x, w):
    xf = x.astype(jnp.float32)
    y = xf * lax.rsqrt(jnp.mean(xf * xf, axis=-1, keepdims=True) + EPS)
    return (y * w.astype(jnp.float32)).astype(x.dtype)


def _head_rmsnorm(o, w):
    y = o * lax.rsqrt(jnp.mean(o * o, axis=-1, keepdims=True) + EPS)
    b, l, hh, d = o.shape
    return y.reshape(b, l, hh * d) * w.astype(jnp.float32)


def _rope(t, cos, sin):
    half = t.shape[-1] // 2
    t1, t2 = t[..., :half], t[..., half:]
    c, s = cos[None, :, None, :], sin[None, :, None, :]
    return jnp.concatenate([t1 * c - t2 * s, t1 * s + t2 * c], axis=-1)


def _to_chunks(t):
    t = jnp.pad(t, ((0, 0), (CHUNK - N_META, 0), (0, 0), (0, 0)))
    b, lp, hh, d = t.shape
    return t.reshape(b, lp // CHUNK, CHUNK, hh, d).transpose(0, 3, 1, 2, 4)


def _from_chunks(t):
    b, hh, n, c, d = t.shape
    return t.transpose(0, 2, 3, 1, 4).reshape(b, n * c, hh, d)[:, CHUNK - N_META:]


def _retention(q, k, v):
    qc, kc, vc = _to_chunks(q), _to_chunks(k), _to_chunks(v)
    bsz = qc.shape[0]
    log_g = jnp.log1p(-jnp.exp2(-5.0 - jnp.arange(RET_HEADS, dtype=jnp.float32)))
    idx = jnp.arange(CHUNK, dtype=jnp.float32)
    diff = idx[:, None] - idx[None, :]
    causal = diff >= 0
    decay = jnp.where(causal, jnp.exp(log_g[:, None, None] * jnp.maximum(diff, 0.0)), 0.0)
    scores = jnp.einsum('bhnid,bhnjd->bhnij', qc, kc) * decay[None, :, None]
    o_intra = jnp.einsum('bhnij,bhnjv->bhniv', scores, vc)
    k_w = kc * jnp.exp(log_g[:, None] * (CHUNK - 1 - idx))[None, :, None, :, None]
    kv = jnp.einsum('bhnjd,bhnjv->nbhdv', k_w, vc)
    g_chunk = jnp.exp(log_g * CHUNK)[None, :, None, None]

    def step(s, kv_n):
        return s * g_chunk + kv_n, s

    s0 = jnp.zeros((bsz, RET_HEADS, RET_DK, RET_DV), jnp.float32)
    _, s_prev = lax.scan(step, s0, kv)
    q_w = qc * jnp.exp(log_g[:, None] * (idx + 1.0))[None, :, None, :, None]
    o_inter = jnp.einsum('bhnid,nbhdv->bhniv', q_w, s_prev)
    return _from_chunks(o_intra + o_inter)


def _s5(u, lam_re, lam_im, log_dt, b_re, b_im, c_re, c_im, d, w_glu):
    bsz, L, _ = u.shape
    lam_re = lam_re.astype(jnp.float32); lam_im = lam_im.astype(jnp.float32)
    dt = jnp.exp(log_dt.astype(jnp.float32))[:, None]
    mag = jnp.exp(lam_re * dt)
    ab_re, ab_im = mag * jnp.cos(lam_im * dt), mag * jnp.sin(lam_im * dt)
    den = lam_re * lam_re + lam_im * lam_im
    nr, ni = ab_re - 1.0, ab_im
    f_re = (nr * lam_re + ni * lam_im) / den
    f_im = (ni * lam_re - nr * lam_im) / den
    b_re = b_re.astype(jnp.float32); b_im = b_im.astype(jnp.float32)
    bb_re = f_re[..., None] * b_re - f_im[..., None] * b_im
    bb_im = f_re[..., None] * b_im + f_im[..., None] * b_re
    ug = u.reshape(bsz, L, S5_G, S5_GH)
    bu_re = jnp.einsum('blgh,gph->lbgp', ug, bb_re)
    bu_im = jnp.einsum('blgh,gph->lbgp', ug, bb_im)
    a_re = jnp.broadcast_to(ab_re, bu_re.shape)
    a_im = jnp.broadcast_to(ab_im, bu_im.shape)

    def combine(e1, e2):
        a1r, a1i, b1r, b1i = e1
        a2r, a2i, b2r, b2i = e2
        return (a2r * a1r - a2i * a1i,
                a2r * a1i + a2i * a1r,
                a2r * b1r - a2i * b1i + b2r,
                a2r * b1i + a2i * b1r + b2i)

    _, _, x_re, x_im = lax.associative_scan(combine, (a_re, a_im, bu_re, bu_im), axis=0)
    y = (jnp.einsum('lbgp,ghp->blgh', x_re, c_re.astype(jnp.float32))
         - jnp.einsum('lbgp,ghp->blgh', x_im, c_im.astype(jnp.float32)))
    y = y.reshape(bsz, L, S5_W) + d.astype(jnp.float32) * u
    y = jax.nn.gelu(y)
    return y * jax.nn.sigmoid(y @ w_glu.astype(jnp.float32))


def _gla(q, k, v, log_a):
    qc, kc, vc, gc = _to_chunks(q), _to_chunks(k), _to_chunks(v), _to_chunks(log_a)
    bsz, hh, n, c, dk = qc.shape
    dv = vc.shape[-1]
    nsub = CHUNK // SUB
    b = jnp.cumsum(gc, axis=3)
    b_last = b[:, :, :, -1]
    kv = jnp.einsum('bhnjd,bhnjv->nbhdv', kc * jnp.exp(b_last[:, :, :, None] - b), vc)
    dec = jnp.exp(b_last).transpose(2, 0, 1, 3)

    def step(s, inp):
        kv_n, dec_n = inp
        return s * dec_n[..., None] + kv_n, s

    s0 = jnp.zeros((bsz, hh, dk, dv), jnp.float32)
    _, s_prev = lax.scan(step, s0, (kv, dec))
    o_inter = jnp.einsum('bhnid,nbhdv->bhniv', qc * jnp.exp(b), s_prev)
    bs = b.reshape(bsz, hh, n, nsub, SUB, dk)
    qs = qc.reshape(bsz, hh, n, nsub, SUB, dk)
    ks = kc.reshape(bsz, hh, n, nsub, SUB, dk)
    vs = vc.reshape(bsz, hh, n, nsub, SUB, dv)
    b_ref = jnp.concatenate([jnp.zeros_like(bs[:, :, :, :1, 0]), bs[:, :, :, :-1, -1]], axis=3)
    q_hat = qs * jnp.exp(bs - b_ref[:, :, :, :, None])
    j_pos = jnp.arange(CHUNK)
    before = j_pos[None, :] < (jnp.arange(nsub) * SUB)[:, None]
    expo = jnp.where(before[:, :, None], b_ref[:, :, :, :, None] - b[:, :, :, None], -jnp.inf)
    k_hat = kc[:, :, :, None] * jnp.exp(expo)
    s_cross = jnp.einsum('bhnsid,bhnsjd->bhnsij', q_hat, k_hat)
    o_cross = jnp.einsum('bhnsij,bhnjv->bhnsiv', s_cross, vc)
    tri = jnp.arange(SUB)[:, None] >= jnp.arange(SUB)[None, :]
    expo_d = jnp.where(tri[:, :, None], bs[:, :, :, :, :, None] - bs[:, :, :, :, None], -jnp.inf)
    s_diag = jnp.einsum('bhnsid,bhnsjd,bhnsijd->bhnsij', qs, ks, jnp.exp(expo_d))
    o_diag = jnp.einsum('bhnsij,bhnsjv->bhnsiv', s_diag, vs)
    o_intra = (o_cross + o_diag).reshape(bsz, hh, n, c, dv)
    return _from_chunks(o_intra + o_inter)


def _mixer_ab(h, w_in, ret_norm_w, lam_re, lam_im, log_dt, b_re, b_im, c_re, c_im, d, w_glu, w_out, cos, sin):
    bsz, L, _ = h.shape
    proj = (h @ w_in).astype(jnp.float32)
    q, k, v, z_a, u, z_b = jnp.split(
        proj, [RET_QK, 2 * RET_QK, 2 * RET_QK + RET_W, 2 * RET_QK + 2 * RET_W,
               2 * RET_QK + 2 * RET_W + S5_W], axis=-1)
    q = _rope(q.reshape(bsz, L, RET_HEADS, RET_DK), cos, sin)
    k = _rope(k.reshape(bsz, L, RET_HEADS, RET_DK), cos, sin) * (RET_DK ** -0.5)
    v = v.reshape(bsz, L, RET_HEADS, RET_DV)
    o_a = _head_rmsnorm(_retention(q, k, v), ret_norm_w) * jax.nn.silu(z_a)
    o_b = _s5(u, lam_re, lam_im, log_dt, b_re, b_im, c_re, c_im, d, w_glu) * jax.nn.silu(z_b)
    return jnp.concatenate([o_a, o_b], axis=-1).astype(h.dtype) @ w_out


def _mixer_c(h, w_in, w_gate, b_gate, norm_w, w_out):
    bsz, L, _ = h.shape
    proj = (h @ w_in).astype(jnp.float32)
    q, k, v, z, g_low = jnp.split(
        proj, [GLA_QK, 2 * GLA_QK, 2 * GLA_QK + GLA_W, 2 * GLA_QK + 2 * GLA_W], axis=-1)
    log_a = jax.nn.log_sigmoid(g_low @ w_gate.astype(jnp.float32) + b_gate.astype(jnp.float32)) / GLA_TAU
    o = _gla(q.reshape(bsz, L, GLA_HEADS, GLA_DK) * (GLA_DK ** -0.5),
             k.reshape(bsz, L, GLA_HEADS, GLA_DK),
             v.reshape(bsz, L, GLA_HEADS, GLA_DV),
             log_a.reshape(bsz, L, GLA_HEADS, GLA_DK))
    o = _head_rmsnorm(o, norm_w) * jax.nn.silu(z)
    return o.astype(h.dtype) @ w_out


def _fwd_setup_inputs(seed: int = 0) -> dict:
    key = jax.random.key(seed)
    ks = jax.random.split(key, 24)
    f32 = jnp.float32

    def nrm(k, shape, scale):
        return jax.random.normal(k, shape, f32) * scale

    return {
        "x": nrm(ks[0], (BATCH, SEQ, D_MODEL), 1.0),
        "meta": nrm(ks[1], (N_META, D_MODEL), 1.0),
        "norm_ab_w": 1.0 + nrm(ks[2], (N_EVEN, D_MODEL), 0.02),
        "w_in_ab": nrm(ks[3], (N_EVEN, D_MODEL, IN_AB), D_MODEL ** -0.5),
        "ret_norm_w": 1.0 + nrm(ks[4], (N_EVEN, RET_W), 0.02),
        "s5_lam_re": -0.5 + nrm(ks[5], (N_EVEN, S5_G, S5_P), 0.01),
        "s5_lam_im": math.pi * jnp.arange(S5_P, dtype=f32) + nrm(ks[6], (N_EVEN, S5_G, S5_P), 0.01),
        "s5_log_dt": jax.random.uniform(ks[7], (N_EVEN, S5_G), f32, math.log(DT_MIN), math.log(DT_MAX)),
        "s5_b_re": nrm(ks[8], (N_EVEN, S5_G, S5_P, S5_GH), (2 * S5_GH) ** -0.5),
        "s5_b_im": nrm(ks[9], (N_EVEN, S5_G, S5_P, S5_GH), (2 * S5_GH) ** -0.5),
        "s5_c_re": nrm(ks[10], (N_EVEN, S5_G, S5_GH, S5_P), S5_P ** -0.5),
        "s5_c_im": nrm(ks[11], (N_EVEN, S5_G, S5_GH, S5_P), S5_P ** -0.5),
        "s5_d": nrm(ks[12], (N_EVEN, S5_W), 1.0),
        "s5_w_glu": nrm(ks[13], (N_EVEN, S5_W, S5_W), S5_W ** -0.5),
        "w_out_ab": nrm(ks[14], (N_EVEN, OUT_AB, D_MODEL), OUT_AB ** -0.5),
        "norm_c_w": 1.0 + nrm(ks[15], (N_ODD, D_MODEL), 0.02),
        "w_in_c": nrm(ks[16], (N_ODD, D_MODEL, IN_C), D_MODEL ** -0.5),
        "gla_w_gate": nrm(ks[17], (N_ODD, GLA_RANK, GLA_QK), GLA_RANK ** -0.5),
        "gla_b_gate": nrm(ks[18], (N_ODD, GLA_QK), 0.1),
        "gla_norm_w": 1.0 + nrm(ks[19], (N_ODD, GLA_W), 0.02),
        "w_out_c": nrm(ks[20], (N_ODD, GLA_W, D_MODEL), GLA_W ** -0.5),
        "final_norm_w": 1.0 + nrm(ks[21], (D_MODEL,), 0.02),
    }


def _fwd_reference(x, meta, norm_ab_w, w_in_ab, ret_norm_w, s5_lam_re, s5_lam_im, s5_log_dt,
              s5_b_re, s5_b_im, s5_c_re, s5_c_im, s5_d, s5_w_glu, w_out_ab,
              norm_c_w, w_in_c, gla_w_gate, gla_b_gate, gla_norm_w, w_out_c, final_norm_w):
    bsz = x.shape[0]
    h = jnp.concatenate(
        [jnp.broadcast_to(meta.astype(x.dtype)[None], (bsz, N_META, D_MODEL)), x], axis=1)
    L = h.shape[1]
    pos = jnp.arange(L, dtype=jnp.float32)
    inv_freq = jnp.power(ROPE_BASE, -jnp.arange(0, RET_DK, 2, dtype=jnp.float32) / RET_DK)
    ang = pos[:, None] * inv_freq[None, :]
    cos, sin = jnp.cos(ang), jnp.sin(ang)
    for layer in range(DEPTH):
        i = layer // 2
        if layer % 2 == 0:
            h = h + _mixer_ab(_rmsnorm(h, norm_ab_w[i]), w_in_ab[i], ret_norm_w[i],
                              s5_lam_re[i], s5_lam_im[i], s5_log_dt[i], s5_b_re[i], s5_b_im[i],
                              s5_c_re[i], s5_c_im[i], s5_d[i], s5_w_glu[i], w_out_ab[i], cos, sin)
        else:
            h = h + _mixer_c(_rmsnorm(h, norm_c_w[i]), w_in_c[i], gla_w_gate[i], gla_b_gate[i],
                             gla_norm_w[i], w_out_c[i])
    return _rmsnorm(h, final_norm_w)[:, N_META:]


import jax as _jax
import jax.numpy as _jnp

TWIN_FORMAT = 'train_step'
FWD_PARAMS = ['x', 'meta', 'norm_ab_w', 'w_in_ab', 'ret_norm_w', 's5_lam_re', 's5_lam_im', 's5_log_dt', 's5_b_re', 's5_b_im', 's5_c_re', 's5_c_im', 's5_d', 's5_w_glu', 'w_out_ab', 'norm_c_w', 'w_in_c', 'gla_w_gate', 'gla_b_gate', 'gla_norm_w', 'w_out_c', 'final_norm_w']
TWIN_WEIGHTS = ['meta', 'norm_ab_w', 'w_in_ab', 'ret_norm_w', 's5_lam_re', 's5_lam_im', 's5_log_dt', 's5_b_re', 's5_b_im', 's5_c_re', 's5_c_im', 's5_d', 's5_w_glu', 'w_out_ab', 'norm_c_w', 'w_in_c', 'gla_w_gate', 'gla_b_gate', 'gla_norm_w', 'w_out_c', 'final_norm_w']
TWIN_DIFF_INPUT = 'x'
TWIN_INPUTS = ['x', 'meta', 'norm_ab_w', 'w_in_ab', 'ret_norm_w', 's5_lam_re', 's5_lam_im', 's5_log_dt', 's5_b_re', 's5_b_im', 's5_c_re', 's5_c_im', 's5_d', 's5_w_glu', 'w_out_ab', 'norm_c_w', 'w_in_c', 'gla_w_gate', 'gla_b_gate', 'gla_norm_w', 'w_out_c', 'final_norm_w', 'loss_target', 'm_meta', 'm_norm_ab_w', 'm_w_in_ab', 'm_ret_norm_w', 'm_s5_lam_re', 'm_s5_lam_im', 'm_s5_log_dt', 'm_s5_b_re', 'm_s5_b_im', 'm_s5_c_re', 'm_s5_c_im', 'm_s5_d', 'm_s5_w_glu', 'm_w_out_ab', 'm_norm_c_w', 'm_w_in_c', 'm_gla_w_gate', 'm_gla_b_gate', 'm_gla_norm_w', 'm_w_out_c', 'm_final_norm_w', 'v_meta', 'v_norm_ab_w', 'v_w_in_ab', 'v_ret_norm_w', 'v_s5_lam_re', 'v_s5_lam_im', 'v_s5_log_dt', 'v_s5_b_re', 'v_s5_b_im', 'v_s5_c_re', 'v_s5_c_im', 'v_s5_d', 'v_s5_w_glu', 'v_w_out_ab', 'v_norm_c_w', 'v_w_in_c', 'v_gla_w_gate', 'v_gla_b_gate', 'v_gla_norm_w', 'v_w_out_c', 'v_final_norm_w']
TWIN_OUTPUTS = ['loss', 'grad_x', 'grad_meta', 'grad_norm_ab_w', 'grad_w_in_ab', 'grad_ret_norm_w', 'grad_s5_lam_re', 'grad_s5_lam_im', 'grad_s5_log_dt', 'grad_s5_b_re', 'grad_s5_b_im', 'grad_s5_c_re', 'grad_s5_c_im', 'grad_s5_d', 'grad_s5_w_glu', 'grad_w_out_ab', 'grad_norm_c_w', 'grad_w_in_c', 'grad_gla_w_gate', 'grad_gla_b_gate', 'grad_gla_norm_w', 'grad_w_out_c', 'grad_final_norm_w', 'delta_meta', 'delta_norm_ab_w', 'delta_w_in_ab', 'delta_ret_norm_w', 'delta_s5_lam_re', 'delta_s5_lam_im', 'delta_s5_log_dt', 'delta_s5_b_re', 'delta_s5_b_im', 'delta_s5_c_re', 'delta_s5_c_im', 'delta_s5_d', 'delta_s5_w_glu', 'delta_w_out_ab', 'delta_norm_c_w', 'delta_w_in_c', 'delta_gla_w_gate', 'delta_gla_b_gate', 'delta_gla_norm_w', 'delta_w_out_c', 'delta_final_norm_w', 'new_m_meta', 'new_m_norm_ab_w', 'new_m_w_in_ab', 'new_m_ret_norm_w', 'new_m_s5_lam_re', 'new_m_s5_lam_im', 'new_m_s5_log_dt', 'new_m_s5_b_re', 'new_m_s5_b_im', 'new_m_s5_c_re', 'new_m_s5_c_im', 'new_m_s5_d', 'new_m_s5_w_glu', 'new_m_w_out_ab', 'new_m_norm_c_w', 'new_m_w_in_c', 'new_m_gla_w_gate', 'new_m_gla_b_gate', 'new_m_gla_norm_w', 'new_m_w_out_c', 'new_m_final_norm_w', 'new_v_meta', 'new_v_norm_ab_w', 'new_v_w_in_ab', 'new_v_ret_norm_w', 'new_v_s5_lam_re', 'new_v_s5_lam_im', 'new_v_s5_log_dt', 'new_v_s5_b_re', 'new_v_s5_b_im', 'new_v_s5_c_re', 'new_v_s5_c_im', 'new_v_s5_d', 'new_v_s5_w_glu', 'new_v_w_out_ab', 'new_v_norm_c_w', 'new_v_w_in_c', 'new_v_gla_w_gate', 'new_v_gla_b_gate', 'new_v_gla_norm_w', 'new_v_w_out_c', 'new_v_final_norm_w']
TWIN_LEAF_KINDS = {'loss': 'loss', 'grad_x': 'grad_x', 'grad_meta': 'grad_w', 'grad_norm_ab_w': 'grad_w', 'grad_w_in_ab': 'grad_w', 'grad_ret_norm_w': 'grad_w', 'grad_s5_lam_re': 'grad_w', 'grad_s5_lam_im': 'grad_w', 'grad_s5_log_dt': 'grad_w', 'grad_s5_b_re': 'grad_w', 'grad_s5_b_im': 'grad_w', 'grad_s5_c_re': 'grad_w', 'grad_s5_c_im': 'grad_w', 'grad_s5_d': 'grad_w', 'grad_s5_w_glu': 'grad_w', 'grad_w_out_ab': 'grad_w', 'grad_norm_c_w': 'grad_w', 'grad_w_in_c': 'grad_w', 'grad_gla_w_gate': 'grad_w', 'grad_gla_b_gate': 'grad_w', 'grad_gla_norm_w': 'grad_w', 'grad_w_out_c': 'grad_w', 'grad_final_norm_w': 'grad_w', 'delta_meta': 'delta_w', 'delta_norm_ab_w': 'delta_w', 'delta_w_in_ab': 'delta_w', 'delta_ret_norm_w': 'delta_w', 'delta_s5_lam_re': 'delta_w', 'delta_s5_lam_im': 'delta_w', 'delta_s5_log_dt': 'delta_w', 'delta_s5_b_re': 'delta_w', 'delta_s5_b_im': 'delta_w', 'delta_s5_c_re': 'delta_w', 'delta_s5_c_im': 'delta_w', 'delta_s5_d': 'delta_w', 'delta_s5_w_glu': 'delta_w', 'delta_w_out_ab': 'delta_w', 'delta_norm_c_w': 'delta_w', 'delta_w_in_c': 'delta_w', 'delta_gla_w_gate': 'delta_w', 'delta_gla_b_gate': 'delta_w', 'delta_gla_norm_w': 'delta_w', 'delta_w_out_c': 'delta_w', 'delta_final_norm_w': 'delta_w', 'new_m_meta': 'new_m', 'new_m_norm_ab_w': 'new_m', 'new_m_w_in_ab': 'new_m', 'new_m_ret_norm_w': 'new_m', 'new_m_s5_lam_re': 'new_m', 'new_m_s5_lam_im': 'new_m', 'new_m_s5_log_dt': 'new_m', 'new_m_s5_b_re': 'new_m', 'new_m_s5_b_im': 'new_m', 'new_m_s5_c_re': 'new_m', 'new_m_s5_c_im': 'new_m', 'new_m_s5_d': 'new_m', 'new_m_s5_w_glu': 'new_m', 'new_m_w_out_ab': 'new_m', 'new_m_norm_c_w': 'new_m', 'new_m_w_in_c': 'new_m', 'new_m_gla_w_gate': 'new_m', 'new_m_gla_b_gate': 'new_m', 'new_m_gla_norm_w': 'new_m', 'new_m_w_out_c': 'new_m', 'new_m_final_norm_w': 'new_m', 'new_v_meta': 'new_v', 'new_v_norm_ab_w': 'new_v', 'new_v_w_in_ab': 'new_v', 'new_v_ret_norm_w': 'new_v', 'new_v_s5_lam_re': 'new_v', 'new_v_s5_lam_im': 'new_v', 'new_v_s5_log_dt': 'new_v', 'new_v_s5_b_re': 'new_v', 'new_v_s5_b_im': 'new_v', 'new_v_s5_c_re': 'new_v', 'new_v_s5_c_im': 'new_v', 'new_v_s5_d': 'new_v', 'new_v_s5_w_glu': 'new_v', 'new_v_w_out_ab': 'new_v', 'new_v_norm_c_w': 'new_v', 'new_v_w_in_c': 'new_v', 'new_v_gla_w_gate': 'new_v', 'new_v_gla_b_gate': 'new_v', 'new_v_gla_norm_w': 'new_v', 'new_v_w_out_c': 'new_v', 'new_v_final_norm_w': 'new_v'}


def _forward(args):
    return _fwd_reference(*[args[k] for k in FWD_PARAMS])


def _output_shape():
    def fwd():
        inp = _fwd_setup_inputs(0)
        return _fwd_reference(*[inp[k] for k in FWD_PARAMS])
    out = _jax.eval_shape(fwd)
    return out.shape, out.dtype

N_MICROBATCH = 1
ADAM_LR = 0.001
ADAM_B1 = 0.9
ADAM_B2 = 0.999
ADAM_EPS = 1e-08
ADAM_WD = 0.01
ADAM_STEP = 10
PER_EXAMPLE_BATCH_AXIS = {'x': 0, 'loss_target': 0}
SHARED_INPUTS = []
_WEIGHT_DTYPES = {'meta': _jnp.float32, 'norm_ab_w': _jnp.float32, 'w_in_ab': _jnp.float32, 'ret_norm_w': _jnp.float32, 's5_lam_re': _jnp.float32, 's5_lam_im': _jnp.float32, 's5_log_dt': _jnp.float32, 's5_b_re': _jnp.float32, 's5_b_im': _jnp.float32, 's5_c_re': _jnp.float32, 's5_c_im': _jnp.float32, 's5_d': _jnp.float32, 's5_w_glu': _jnp.float32, 'w_out_ab': _jnp.float32, 'norm_c_w': _jnp.float32, 'w_in_c': _jnp.float32, 'gla_w_gate': _jnp.float32, 'gla_b_gate': _jnp.float32, 'gla_norm_w': _jnp.float32, 'w_out_c': _jnp.float32, 'final_norm_w': _jnp.float32}
MOMENT_SCALE = {'meta': 7.192528e-03, 'norm_ab_w': 1.085847e-01, 'w_in_ab': 5.191774e-02, 'ret_norm_w': 5.001435e-02, 's5_lam_re': 1.165062e-03, 's5_lam_im': 1.265253e-03, 's5_log_dt': 9.641567e-01, 's5_b_re': 8.234971e-04, 's5_b_im': 8.379468e-04, 's5_c_re': 1.148221e-03, 's5_c_im': 1.148329e-03, 's5_d': 1.997156e-02, 's5_w_glu': 5.220381e-03, 'w_out_ab': 5.183367e-02, 'norm_c_w': 8.624825e-02, 'w_in_c': 4.923490e-02, 'gla_w_gate': 6.860659e-03, 'gla_b_gate': 2.987468e-02, 'gla_norm_w': 4.210013e-02, 'w_out_c': 4.175106e-02, 'final_norm_w': 1.600771e+01}


def _to_microbatches(a, axis):
    t = _jnp.moveaxis(a, axis, 0)
    t = t.reshape((N_MICROBATCH, t.shape[0] // N_MICROBATCH) + t.shape[1:])
    return _jnp.moveaxis(t, 1, axis + 1)


def setup_inputs(seed: int = 0) -> dict:
    inp = _fwd_setup_inputs(seed)
    key = _jax.random.fold_in(_jax.random.key(seed), 7919)
    shape, _ = _output_shape()
    out = dict(inp)
    out["loss_target"] = _jax.random.normal(_jax.random.fold_in(key, 0), shape, _jnp.float32)
    for i, name in enumerate(TWIN_WEIGHTS):
        w = inp[name].astype(_jnp.float32)
        if MOMENT_SCALE is None:
            s = _jnp.sqrt(_jnp.mean(_jnp.square(w)) + 1e-30)
        else:
            s = MOMENT_SCALE[name]
        km, kv = _jax.random.split(_jax.random.fold_in(key, i + 1))
        out[name] = w
        out["m_" + name] = s * _jax.random.normal(km, w.shape, _jnp.float32)
        out["v_" + name] = (s * s) * _jax.random.uniform(kv, w.shape, _jnp.float32, 0.5, 1.5)
    if N_MICROBATCH > 1:
        for name, axis in PER_EXAMPLE_BATCH_AXIS.items():
            out[name] = _to_microbatches(out[name], axis)
    return {'x': out['x'], 'meta': out['meta'], 'norm_ab_w': out['norm_ab_w'], 'w_in_ab': out['w_in_ab'], 'ret_norm_w': out['ret_norm_w'], 's5_lam_re': out['s5_lam_re'], 's5_lam_im': out['s5_lam_im'], 's5_log_dt': out['s5_log_dt'], 's5_b_re': out['s5_b_re'], 's5_b_im': out['s5_b_im'], 's5_c_re': out['s5_c_re'], 's5_c_im': out['s5_c_im'], 's5_d': out['s5_d'], 's5_w_glu': out['s5_w_glu'], 'w_out_ab': out['w_out_ab'], 'norm_c_w': out['norm_c_w'], 'w_in_c': out['w_in_c'], 'gla_w_gate': out['gla_w_gate'], 'gla_b_gate': out['gla_b_gate'], 'gla_norm_w': out['gla_norm_w'], 'w_out_c': out['w_out_c'], 'final_norm_w': out['final_norm_w'], 'loss_target': out['loss_target'], 'm_meta': out['m_meta'], 'm_norm_ab_w': out['m_norm_ab_w'], 'm_w_in_ab': out['m_w_in_ab'], 'm_ret_norm_w': out['m_ret_norm_w'], 'm_s5_lam_re': out['m_s5_lam_re'], 'm_s5_lam_im': out['m_s5_lam_im'], 'm_s5_log_dt': out['m_s5_log_dt'], 'm_s5_b_re': out['m_s5_b_re'], 'm_s5_b_im': out['m_s5_b_im'], 'm_s5_c_re': out['m_s5_c_re'], 'm_s5_c_im': out['m_s5_c_im'], 'm_s5_d': out['m_s5_d'], 'm_s5_w_glu': out['m_s5_w_glu'], 'm_w_out_ab': out['m_w_out_ab'], 'm_norm_c_w': out['m_norm_c_w'], 'm_w_in_c': out['m_w_in_c'], 'm_gla_w_gate': out['m_gla_w_gate'], 'm_gla_b_gate': out['m_gla_b_gate'], 'm_gla_norm_w': out['m_gla_norm_w'], 'm_w_out_c': out['m_w_out_c'], 'm_final_norm_w': out['m_final_norm_w'], 'v_meta': out['v_meta'], 'v_norm_ab_w': out['v_norm_ab_w'], 'v_w_in_ab': out['v_w_in_ab'], 'v_ret_norm_w': out['v_ret_norm_w'], 'v_s5_lam_re': out['v_s5_lam_re'], 'v_s5_lam_im': out['v_s5_lam_im'], 'v_s5_log_dt': out['v_s5_log_dt'], 'v_s5_b_re': out['v_s5_b_re'], 'v_s5_b_im': out['v_s5_b_im'], 'v_s5_c_re': out['v_s5_c_re'], 'v_s5_c_im': out['v_s5_c_im'], 'v_s5_d': out['v_s5_d'], 'v_s5_w_glu': out['v_s5_w_glu'], 'v_w_out_ab': out['v_w_out_ab'], 'v_norm_c_w': out['v_norm_c_w'], 'v_w_in_c': out['v_w_in_c'], 'v_gla_w_gate': out['v_gla_w_gate'], 'v_gla_b_gate': out['v_gla_b_gate'], 'v_gla_norm_w': out['v_gla_norm_w'], 'v_w_out_c': out['v_w_out_c'], 'v_final_norm_w': out['v_final_norm_w']}


def _loss(weights, diff, rest, loss_target):
    with _jax.named_scope("forward"):
        args = {**rest, TWIN_DIFF_INPUT: diff, **{k: w.astype(_WEIGHT_DTYPES[k]) for k, w in weights.items()}}
        y = _forward(args)
    with _jax.named_scope("loss_head"):
        err = _jnp.square(y.astype(_jnp.float32) - loss_target)
        return 0.5 * _jnp.sum(_jnp.mean(err, axis=-1)) if err.ndim else 0.5 * err


def _adamw(w, g, m, v):
    m = ADAM_B1 * m + (1.0 - ADAM_B1) * g
    v = ADAM_B2 * v + (1.0 - ADAM_B2) * _jnp.square(g)
    m_hat = m / (1.0 - ADAM_B1 ** ADAM_STEP)
    v_hat = v / (1.0 - ADAM_B2 ** ADAM_STEP)
    delta = -ADAM_LR * (m_hat / (_jnp.sqrt(v_hat) + ADAM_EPS) + ADAM_WD * w)
    return delta, m, v


def reference(x, meta, norm_ab_w, w_in_ab, ret_norm_w, s5_lam_re, s5_lam_im, s5_log_dt, s5_b_re, s5_b_im, s5_c_re, s5_c_im, s5_d, s5_w_glu, w_out_ab, norm_c_w, w_in_c, gla_w_gate, gla_b_gate, gla_norm_w, w_out_c, final_norm_w, loss_target, m_meta, m_norm_ab_w, m_w_in_ab, m_ret_norm_w, m_s5_lam_re, m_s5_lam_im, m_s5_log_dt, m_s5_b_re, m_s5_b_im, m_s5_c_re, m_s5_c_im, m_s5_d, m_s5_w_glu, m_w_out_ab, m_norm_c_w, m_w_in_c, m_gla_w_gate, m_gla_b_gate, m_gla_norm_w, m_w_out_c, m_final_norm_w, v_meta, v_norm_ab_w, v_w_in_ab, v_ret_norm_w, v_s5_lam_re, v_s5_lam_im, v_s5_log_dt, v_s5_b_re, v_s5_b_im, v_s5_c_re, v_s5_c_im, v_s5_d, v_s5_w_glu, v_w_out_ab, v_norm_c_w, v_w_in_c, v_gla_w_gate, v_gla_b_gate, v_gla_norm_w, v_w_out_c, v_final_norm_w):
    given = dict(x=x, meta=meta, norm_ab_w=norm_ab_w, w_in_ab=w_in_ab, ret_norm_w=ret_norm_w, s5_lam_re=s5_lam_re, s5_lam_im=s5_lam_im, s5_log_dt=s5_log_dt, s5_b_re=s5_b_re, s5_b_im=s5_b_im, s5_c_re=s5_c_re, s5_c_im=s5_c_im, s5_d=s5_d, s5_w_glu=s5_w_glu, w_out_ab=w_out_ab, norm_c_w=norm_c_w, w_in_c=w_in_c, gla_w_gate=gla_w_gate, gla_b_gate=gla_b_gate, gla_norm_w=gla_norm_w, w_out_c=w_out_c, final_norm_w=final_norm_w, loss_target=loss_target, m_meta=m_meta, m_norm_ab_w=m_norm_ab_w, m_w_in_ab=m_w_in_ab, m_ret_norm_w=m_ret_norm_w, m_s5_lam_re=m_s5_lam_re, m_s5_lam_im=m_s5_lam_im, m_s5_log_dt=m_s5_log_dt, m_s5_b_re=m_s5_b_re, m_s5_b_im=m_s5_b_im, m_s5_c_re=m_s5_c_re, m_s5_c_im=m_s5_c_im, m_s5_d=m_s5_d, m_s5_w_glu=m_s5_w_glu, m_w_out_ab=m_w_out_ab, m_norm_c_w=m_norm_c_w, m_w_in_c=m_w_in_c, m_gla_w_gate=m_gla_w_gate, m_gla_b_gate=m_gla_b_gate, m_gla_norm_w=m_gla_norm_w, m_w_out_c=m_w_out_c, m_final_norm_w=m_final_norm_w, v_meta=v_meta, v_norm_ab_w=v_norm_ab_w, v_w_in_ab=v_w_in_ab, v_ret_norm_w=v_ret_norm_w, v_s5_lam_re=v_s5_lam_re, v_s5_lam_im=v_s5_lam_im, v_s5_log_dt=v_s5_log_dt, v_s5_b_re=v_s5_b_re, v_s5_b_im=v_s5_b_im, v_s5_c_re=v_s5_c_re, v_s5_c_im=v_s5_c_im, v_s5_d=v_s5_d, v_s5_w_glu=v_s5_w_glu, v_w_out_ab=v_w_out_ab, v_norm_c_w=v_norm_c_w, v_w_in_c=v_w_in_c, v_gla_w_gate=v_gla_w_gate, v_gla_b_gate=v_gla_b_gate, v_gla_norm_w=v_gla_norm_w, v_w_out_c=v_w_out_c, v_final_norm_w=v_final_norm_w)
    weights = {n: given[n] for n in TWIN_WEIGHTS}
    shared = {n: given[n] for n in SHARED_INPUTS}
    per_example = {n: given[n] for n in ['x']}
    grad_fn = _jax.value_and_grad(_loss, argnums=(0, 1))

    def one_microbatch(ex, loss_target):
        ex = dict(ex)
        diff = ex.pop(TWIN_DIFF_INPUT)
        return grad_fn(weights, diff, {**shared, **ex}, loss_target)

    if N_MICROBATCH == 1:
        loss, (grad_w, grad_x) = one_microbatch(per_example, given["loss_target"])
    else:
        def body(carry, xs):
            loss_sum, grad_sum = carry
            l_k, (gw_k, gx_k) = one_microbatch(xs[0], xs[1])
            with _jax.named_scope("update"):
                return (loss_sum + l_k, _jax.tree.map(_jnp.add, grad_sum, gw_k)), gx_k

        init = (_jnp.zeros((), _jnp.float32), _jax.tree.map(_jnp.zeros_like, weights))
        (loss, grad_w), grad_x = _jax.lax.scan(body, init, (per_example, given["loss_target"]))
    with _jax.named_scope("update"):
        delta_w, new_m, new_v = {}, {}, {}
        for n in TWIN_WEIGHTS:
            delta_w[n], new_m[n], new_v[n] = _adamw(weights[n], grad_w[n], given["m_" + n], given["v_" + n])
    return (loss, grad_x, *[grad_w[n] for n in TWIN_WEIGHTS], *[delta_w[n] for n in TWIN_WEIGHTS],
            *[new_m[n] for n in TWIN_WEIGHTS], *[new_v[n] for n in TWIN_WEIGHTS])
```

```python
import functools
import math

import jax
import jax.numpy as jnp
from jax import lax
from jax.experimental import pallas as pl
from jax.experimental.pallas import tpu as pltpu

F32, BF16 = jnp.float32, jnp.bfloat16
MESH = pl.DeviceIdType.MESH
N_DEV = 8

D_MODEL = 2048
CHUNK = 128
N_META = 16
PAD = CHUNK - N_META
SUB = 16
EPS = 1e-6
RET_H, RET_DK, RET_DV = 8, 128, 256
RET_QK, RET_W = RET_H * RET_DK, RET_H * RET_DV
ROPE_BASE = 10000.0
S5_W, S5_G, S5_P, S5_GH = 1024, 64, 64, 16
S5_N = S5_G * S5_P
GLA_H, GLA_DK, GLA_DV, GLA_RANK, GLA_TAU = 4, 256, 512, 16, 16.0
GLA_QK, GLA_W = GLA_H * GLA_DK, GLA_H * GLA_DV
IN_AB = 2 * RET_QK + 2 * RET_W + 2 * S5_W
OUT_AB = RET_W + S5_W
IN_C = 2 * GLA_QK + 2 * GLA_W + GLA_RANK
IN_C_PAD = 2 * GLA_QK + 2 * GLA_W + 128
ADAM_LR, ADAM_B1, ADAM_B2, ADAM_EPS, ADAM_WD, ADAM_STEP = 0.001, 0.9, 0.999, 1e-08, 0.01, 10

VMEM_LIMIT_BYTES = 48 * 2 ** 20
PACK_COLS = 1024
PACK_ROW_MULT = 512


def _params(sem):
    return pltpu.CompilerParams(dimension_semantics=sem, vmem_limit_bytes=VMEM_LIMIT_BYTES)


def _tile(n, cap, mult):
    best = None
    for t in range(mult, min(n, cap) + 1, mult):
        if n % t == 0:
            best = t
    assert best is not None, (n, cap, mult)
    return best


def _dg(a, b, ca, cb):
    return lax.dot_general(a.astype(BF16), b.astype(BF16), (((ca,), (cb,)), ((), ())),
                           preferred_element_type=F32)


@functools.partial(jax.custom_vjp, nondiff_argnums=(2, 3))
def _bdot(a, b, ca, cb):
    return _dg(a, b, ca, cb)


def _bdot_fwd(a, b, ca, cb):
    return _dg(a, b, ca, cb), (a, b)


def _bdot_bwd(ca, cb, res, g):
    a, b = res
    da = _dg(g, b, 1, 1 - cb) if ca == 1 else _dg(b, g, 1 - cb, 1)
    db = _dg(a, g, 1 - ca, 0) if cb == 0 else _dg(g, a, 0, 1 - ca)
    return da.astype(a.dtype), db.astype(b.dtype)


_bdot.defvjp(_bdot_fwd, _bdot_bwd)


def _sigmoid(x):
    return 1.0 / (1.0 + jnp.exp(-x))


def _silu(x):
    return x * _sigmoid(x)


def _log_sigmoid(x):
    return jnp.minimum(x, 0.0) - jnp.log(1.0 + jnp.exp(-jnp.abs(x)))


def _gelu(x):
    return 0.5 * x * (1.0 + jnp.tanh(math.sqrt(2.0 / math.pi) * (x + 0.044715 * (x * x * x))))


def _rms(x, w):
    return x * lax.rsqrt(jnp.mean(x * x, axis=-1, keepdims=True) + EPS) * w


def _mm(a, b, mode, *, name, out_dtype=F32, a_win=None, add=None, bias=None):
    if mode == "tn":
        kdim, n = a.shape[0], b.shape[1]
        m = a.shape[1] if a_win is None else a_win[1]
        tm, tn, tk = _tile(m, 512, 128), _tile(n, 1024, 128), _tile(kdim, 1408, 8)
        off = 0 if a_win is None else a_win[0] // tm
        a_spec = pl.BlockSpec((tk, tm), lambda i, j, k: (k, i + off))
        b_spec = pl.BlockSpec((tk, tn), lambda i, j, k: (k, j))
        dims = (((0,), (0,)), ((), ()))
    else:
        m = a.shape[0]
        kdim = a.shape[1] if a_win is None else a_win[1]
        n = b.shape[1] if mode == "nn" else b.shape[0]
        tm, tn, tk = _tile(m, 1408, 8), _tile(n, 512, 128), _tile(kdim, 1024, 128)
        off = 0 if a_win is None else a_win[0] // tk
        a_spec = pl.BlockSpec((tm, tk), lambda i, j, k: (i, k + off))
        if mode == "nn":
            b_spec = pl.BlockSpec((tk, tn), lambda i, j, k: (k, j))
            dims = (((1,), (0,)), ((), ()))
        else:
            b_spec = pl.BlockSpec((tn, tk), lambda i, j, k: (j, k))
            dims = (((1,), (1,)), ((), ()))
    if a_win is not None:
        assert a_win[0] % (tm if mode == "tn" else tk) == 0
    nk = kdim // tk
    has_add, has_bias = add is not None, bias is not None

    def body(*refs):
        a_ref, b_ref = refs[0], refs[1]
        o_ref, acc = refs[-2], refs[-1]
        k = pl.program_id(2)

        @pl.when(k == 0)
        def _():
            acc[...] = jnp.zeros_like(acc)

        acc[...] += lax.dot_general(a_ref[...].astype(BF16), b_ref[...].astype(BF16), dims,
                                    preferred_element_type=F32)

        @pl.when(k == nk - 1)
        def _():
            r = acc[...]
            if has_add:
                r = r + refs[2][...].astype(F32)
            if has_bias:
                r = r + refs[2 + has_add][...]
            o_ref[...] = r.astype(out_dtype)

    in_specs, operands = [a_spec, b_spec], [a, b]
    if has_add:
        in_specs.append(pl.BlockSpec((tm, tn), lambda i, j, k: (i, j)))
        operands.append(add)
    if has_bias:
        in_specs.append(pl.BlockSpec((1, tn), lambda i, j, k: (0, j)))
        operands.append(bias)
    return pl.pallas_call(
        body, name=name, grid=(m // tm, n // tn, nk),
        in_specs=in_specs, out_specs=pl.BlockSpec((tm, tn), lambda i, j, k: (i, j)),
        out_shape=jax.ShapeDtypeStruct((m, n), out_dtype),
        scratch_shapes=[pltpu.VMEM((tm, tn), F32)],
        compiler_params=_params(("arbitrary", "arbitrary", "arbitrary")),
    )(*operands)


def _win(arr, col0=0, width=None, roff=0):
    return (arr, col0, arr.shape[1] if width is None else width, roff)


def _rows(fn, rows, consts, outs, accs, *, name, nrow, tr=CHUNK):
    nr, nc, no = len(rows), len(consts), len(outs)

    def body(*refs):
        i = pl.program_id(0)
        ins = [r[...] for r in refs[:nr + nc]]
        o_refs = refs[nr + nc:nr + nc + no]
        a_refs = refs[nr + nc + no:]
        res_o, res_a = fn(i, *ins)
        for r, v in zip(o_refs, res_o):
            r[...] = v.astype(r.dtype)
        if a_refs:
            @pl.when(i == 0)
            def _():
                for r in a_refs:
                    r[...] = jnp.zeros_like(r)

            for r, v in zip(a_refs, res_a):
                r[...] += v

    in_specs = []
    for (arr, col0, width, roff) in rows:
        assert col0 % width == 0 and arr.shape[0] % tr == 0
        in_specs.append(pl.BlockSpec((tr, width), lambda i, c=col0 // width, ro=roff: (jnp.maximum(i - ro, 0), c)))
    for c in consts:
        in_specs.append(pl.BlockSpec(c.shape, lambda i, nd=c.ndim: (0,) * nd))
    out_specs = [pl.BlockSpec((tr, w), lambda i: (i, 0)) for (w, _) in outs]
    out_specs += [pl.BlockSpec(s, lambda i, nd=len(s): (0,) * nd) for s in accs]
    out_shape = [jax.ShapeDtypeStruct((nrow, w), dt) for (w, dt) in outs]
    out_shape += [jax.ShapeDtypeStruct(s, F32) for s in accs]
    res = pl.pallas_call(
        body, name=name, grid=(nrow // tr,), in_specs=in_specs, out_specs=out_specs, out_shape=out_shape,
        compiler_params=_params(("arbitrary",)),
    )(*[r[0] for r in rows], *consts)
    return res[:no], res[no:]


def _scan_specs(xs, cs, ws, ks, chunk_of):
    specs = []
    for (arr, width, colfn) in list(xs) + list(cs):
        specs.append(pl.BlockSpec((CHUNK, width), lambda h, n, f=colfn: (chunk_of(n), f(h))))
    for arr in list(ws) + list(ks):
        specs.append(pl.BlockSpec((1, 1, arr.shape[2]), lambda h, n: (h, 0, 0)))
    return specs


def _scan_fwd(fn, xs, cs, ws, ks, *, heads, nchunk, s_shape, out_w, name, pre=None):
    nx, ncs, nw = len(xs), len(cs), len(ws)

    def body(*refs):
        n = pl.program_id(1)
        nin = nx + ncs + nw + len(ks)
        y_ref, sp_ref, s_scr = refs[nin:]

        @pl.when(n == 0)
        def _():
            s_scr[...] = jnp.zeros_like(s_scr)

        state = s_scr[...]
        sp_ref[0, 0] = state
        xv = [r[...] for r in refs[:nx]]
        cv = [r[...] for r in refs[nx:nx + ncs]]
        wv = [r[0] for r in refs[nx + ncs:nx + ncs + nw]]
        kv = [r[0] for r in refs[nx + ncs + nw:nin]]
        if pre is not None:
            xv = pre(xv, cv)
        y, s_new = fn(n, xv, state, cv, wv, kv)
        y_ref[...] = y.astype(y_ref.dtype)
        s_scr[...] = s_new

    lp = nchunk * CHUNK
    return pl.pallas_call(
        body, name=name, grid=(heads, nchunk),
        in_specs=_scan_specs(xs, cs, ws, ks, lambda n: n),
        out_specs=[pl.BlockSpec((CHUNK, out_w), lambda h, n: (n, h)),
                   pl.BlockSpec((1, 1) + s_shape, lambda h, n: (h, n, 0, 0))],
        out_shape=[jax.ShapeDtypeStruct((lp, heads * out_w), BF16),
                   jax.ShapeDtypeStruct((heads, nchunk) + s_shape, F32)],
        scratch_shapes=[pltpu.VMEM(s_shape, F32)],
        compiler_params=_params(("arbitrary", "arbitrary")),
    )(*[t[0] for t in xs], *[t[0] for t in cs], *ws, *ks)


def _scan_bwd(fn, xs, cs, ws, ks, dy, sprev, *, heads, nchunk, s_shape, out_w, name, pre=None, post=None):
    nx, ncs, nw = len(xs), len(cs), len(ws)
    nin = nx + ncs + nw + len(ks)

    def body(*refs):
        step = pl.program_id(1)
        n = nchunk - 1 - step
        dy_ref, sp_ref = refs[nin], refs[nin + 1]
        dx_refs = refs[nin + 2:nin + 2 + nx]
        dw_refs = refs[nin + 2 + nx:nin + 2 + nx + nw]
        ds_scr = refs[-1]

        @pl.when(step == 0)
        def _():
            ds_scr[...] = jnp.zeros_like(ds_scr)
            for r in dw_refs:
                r[...] = jnp.zeros_like(r)

        xv = [r[...] for r in refs[:nx]]
        cv = [r[...] for r in refs[nx:nx + ncs]]
        wv = [r[0] for r in refs[nx + ncs:nx + ncs + nw]]
        kv = [r[0] for r in refs[nx + ncs + nw:nin]]
        if pre is not None:
            xv = pre(xv, cv)
        _, vjp = jax.vjp(lambda xs_, s_, ws_: fn(n, xs_, s_, cv, ws_, kv), xv, sp_ref[0, 0], wv)
        dxs, ds_prev, dws = vjp((dy_ref[...].astype(F32), ds_scr[...]))
        if post is not None:
            dxs = post(dxs, cv)
        for r, v in zip(dx_refs, dxs):
            r[...] = v.astype(r.dtype)
        for r, v in zip(dw_refs, dws):
            r[0] += v
        ds_scr[...] = ds_prev

    lp = nchunk * CHUNK
    rev = lambda n: nchunk - 1 - n
    in_specs = _scan_specs(xs, cs, ws, ks, rev)
    in_specs.append(pl.BlockSpec((CHUNK, out_w), lambda h, n: (rev(n), h)))
    in_specs.append(pl.BlockSpec((1, 1) + s_shape, lambda h, n: (h, rev(n), 0, 0)))
    out_specs = [pl.BlockSpec((CHUNK, w), lambda h, n: (rev(n), h)) for (_, w, _) in xs]
    out_specs += [pl.BlockSpec((1, 1, w.shape[2]), lambda h, n: (h, 0, 0)) for w in ws]
    out_shape = [jax.ShapeDtypeStruct((lp, heads * w), BF16) for (_, w, _) in xs]
    out_shape += [jax.ShapeDtypeStruct(w.shape, F32) for w in ws]
    res = pl.pallas_call(
        body, name=name, grid=(heads, nchunk), in_specs=in_specs, out_specs=out_specs, out_shape=out_shape,
        scratch_shapes=[pltpu.VMEM(s_shape, F32)],
        compiler_params=_params(("arbitrary", "arbitrary")),
    )(*[t[0] for t in xs], *[t[0] for t in cs], *ws, *ks, dy, sprev)
    return res[:nx], res[nx:]


def _iota2(shape, dim):
    return lax.broadcasted_iota(jnp.int32, shape, dim)


def _ret_chunk(n, xs, state, cs, ws, ks):
    q, k, v, z = xs
    (w,), (lg,) = ws, ks
    lgc = lg[:, :1]
    row, col = _iota2((CHUNK, CHUNK), 0), _iota2((CHUNK, CHUNK), 1)
    diff = jnp.maximum(row - col, 0).astype(F32)
    decay = jnp.where(row >= col, jnp.exp(lg * diff), 0.0)
    scores = _bdot(q, k, 1, 1) * decay
    o_intra = _bdot(scores, v, 1, 0)
    idx = _iota2((CHUNK, 1), 0).astype(F32)
    k_w = k * jnp.exp(lgc * (CHUNK - 1.0 - idx))
    kv = _bdot(k_w, v, 0, 0)
    s_new = state * jnp.exp(lgc * float(CHUNK)) + kv
    q_w = q * jnp.exp(lgc * (idx + 1.0))
    o = o_intra + _bdot(q_w, state, 1, 0)
    return _rms(o, w) * _silu(z), s_new


def _rope(t, cos2, sin2):
    return t * cos2 + pltpu.roll(t, RET_DK // 2, 1) * sin2


def _rope_t(g, cos2, sin2):
    return g * cos2 - pltpu.roll(g, RET_DK // 2, 1) * sin2


def _ret_pre(xv, cv):
    q, k, v, z = xv
    cos2, sin2 = cv
    return [_rope(q, cos2, sin2), _rope(k, cos2, sin2) * (RET_DK ** -0.5), v, z]


def _ret_post(dxs, cv):
    dq, dk, dv, dz = dxs
    cos2, sin2 = cv
    return [_rope_t(dq, cos2, sin2), _rope_t(dk, cos2, sin2) * (RET_DK ** -0.5), dv, dz]


def _gla_chunk(n, xs, state_t, cs, ws, ks):
    q, k, v, z, pre = xs
    (w,) = ws
    q = q * (GLA_DK ** -0.5)
    rowc = _iota2((CHUNK, 1), 0)
    valid = jnp.logical_or(n > 0, rowc >= PAD)
    log_a = jnp.where(valid, _log_sigmoid(pre) / GLA_TAU, 0.0)
    row, col = _iota2((CHUNK, CHUNK), 0), _iota2((CHUNK, CHUNK), 1)
    tri = (row >= col).astype(F32)
    b = jnp.dot(tri, log_a, precision=lax.Precision.HIGHEST, preferred_element_type=F32)
    b_last = b[CHUNK - 1:CHUNK, :]
    kv_t = _bdot(v, k * jnp.exp(b_last - b), 0, 0)
    s_new = state_t * jnp.exp(b_last) + kv_t
    o_inter = _bdot(q * jnp.exp(b), state_t, 1, 1)
    outs = []
    for s in range(CHUNK // SUB):
        lo, hi = s * SUB, (s + 1) * SUB
        b_ref = jnp.zeros_like(b_last) if s == 0 else b[lo - 1:lo, :]
        q_hat = q[lo:hi] * jnp.exp(b[lo:hi] - b_ref)
        k_hat = k[:hi] * jnp.exp(b_ref - b[:hi])
        sc = _bdot(q_hat, k_hat, 1, 1)
        causal = _iota2((SUB, hi), 0) + lo >= _iota2((SUB, hi), 1)
        outs.append(_bdot(jnp.where(causal, sc, 0.0), v[:hi], 1, 0))
    o = jnp.concatenate(outs, axis=0) + o_inter
    return _rms(o, w) * _silu(z), s_new


def _s5_disc(lam_re, lam_im, log_dt, b_re, b_im, expand):
    dt = jnp.exp(log_dt)
    mag = jnp.exp(lam_re * dt)
    ab_re, ab_im = mag * jnp.cos(lam_im * dt), mag * jnp.sin(lam_im * dt)
    den = lam_re * lam_re + lam_im * lam_im
    nr, ni = ab_re - 1.0, ab_im
    f_re = (nr * lam_re + ni * lam_im) / den
    f_im = (ni * lam_re - nr * lam_im) / den
    hp = lax.Precision.HIGHEST
    f_re = jnp.dot(f_re, expand, precision=hp, preferred_element_type=F32)
    f_im = jnp.dot(f_im, expand, precision=hp, preferred_element_type=F32)
    return ab_re, ab_im, f_re * b_re - f_im * b_im, f_re * b_im + f_im * b_re


def _s5_disc_fwd(args):
    def body(*refs):
        outs = _s5_disc(*[r[...] for r in refs[:6]])
        for r, v in zip(refs[6:], outs):
            r[...] = v

    g, p = args[0].shape
    return pl.pallas_call(
        body, name="s5_disc_fwd",
        out_shape=[jax.ShapeDtypeStruct((g, p), F32)] * 2 + [jax.ShapeDtypeStruct(args[3].shape, F32)] * 2,
    )(*args)


def _s5_disc_bwd(args, cts):
    def body(*refs):
        prim = [r[...] for r in refs[:5]]
        expand = refs[5][...]
        ct = tuple(r[...] for r in refs[6:10])
        _, vjp = jax.vjp(lambda *a: _s5_disc(*a, expand), *prim)
        for r, v in zip(refs[10:], vjp(ct)):
            r[...] = v

    return pl.pallas_call(
        body, name="s5_disc_bwd", out_shape=[jax.ShapeDtypeStruct(a.shape, F32) for a in args[:5]],
    )(*args, *cts)


S5_SUBL = 8
S5_LANES = S5_N // S5_SUBL
S5_TB = 64


def _s5_scan_fwd(bu, a_re, a_im):
    lp = bu.shape[0]

    def body(bu_ref, ar_ref, ai_ref, x_ref, st):
        @pl.when(pl.program_id(0) == 0)
        def _():
            st[...] = jnp.zeros_like(st)

        ar, ai = ar_ref[...], ai_ref[...]

        def step(t, carry):
            xr, xi = carry
            nr = ar * xr - ai * xi + bu_ref[t, 0:S5_SUBL, :]
            ni = ar * xi + ai * xr + bu_ref[t, S5_SUBL:2 * S5_SUBL, :]
            x_ref[t, 0:S5_SUBL, :] = nr
            x_ref[t, S5_SUBL:2 * S5_SUBL, :] = ni
            return nr, ni

        xr, xi = lax.fori_loop(0, S5_TB, step, (st[0], st[1]))
        st[0] = xr
        st[1] = xi

    blk = pl.BlockSpec((S5_TB, 2 * S5_SUBL, S5_LANES), lambda i: (i, 0, 0))
    cst = pl.BlockSpec((S5_SUBL, S5_LANES), lambda i: (0, 0))
    return pl.pallas_call(
        body, name="s5_scan_fwd", grid=(lp // S5_TB,), in_specs=[blk, cst, cst], out_specs=blk,
        out_shape=jax.ShapeDtypeStruct(bu.shape, F32),
        scratch_shapes=[pltpu.VMEM((2, S5_SUBL, S5_LANES), F32)],
        compiler_params=_params(("arbitrary",)),
    )(bu, a_re, a_im)


def _s5_scan_bwd(gx, x, a_re, a_im):
    lp = gx.shape[0]
    nb = lp // S5_TB

    def body(gx_ref, x_ref, xp_ref, ar_ref, ai_ref, g_ref, da_ref, st):
        i = pl.program_id(0)

        @pl.when(i == 0)
        def _():
            st[...] = jnp.zeros_like(st)
            da_ref[...] = jnp.zeros_like(da_ref)

        ar, ai = ar_ref[...], ai_ref[...]
        first = (i == nb - 1).astype(F32)

        def step(s, carry):
            gr, gi, dar, dai = carry
            t = S5_TB - 1 - s
            ngr = gx_ref[t, 0:S5_SUBL, :] + ar * gr + ai * gi
            ngi = gx_ref[t, S5_SUBL:2 * S5_SUBL, :] + ar * gi - ai * gr
            g_ref[t, 0:S5_SUBL, :] = ngr
            g_ref[t, S5_SUBL:2 * S5_SUBL, :] = ngi
            tp = jnp.maximum(t - 1, 0)
            at0 = (t == 0).astype(F32)
            keep = 1.0 - at0
            pr = keep * x_ref[tp, 0:S5_SUBL, :] + at0 * (1.0 - first) * xp_ref[0, 0:S5_SUBL, :]
            pi = keep * x_ref[tp, S5_SUBL:2 * S5_SUBL, :] + at0 * (1.0 - first) * xp_ref[0, S5_SUBL:2 * S5_SUBL, :]
            return ngr, ngi, dar + ngr * pr + ngi * pi, dai + ngi * pr - ngr * pi

        zero = jnp.zeros((S5_SUBL, S5_LANES), F32)
        gr, gi, dar, dai = lax.fori_loop(0, S5_TB, step, (st[0], st[1], zero, zero))
        st[0] = gr
        st[1] = gi
        da_ref[0] += dar
        da_ref[1] += dai

    rev = lambda i: nb - 1 - i
    blk = pl.BlockSpec((S5_TB, 2 * S5_SUBL, S5_LANES), lambda i: (rev(i), 0, 0))
    prev = pl.BlockSpec((1, 2 * S5_SUBL, S5_LANES), lambda i: (jnp.maximum(rev(i) * S5_TB - 1, 0), 0, 0))
    cst = pl.BlockSpec((S5_SUBL, S5_LANES), lambda i: (0, 0))
    return pl.pallas_call(
        body, name="s5_scan_bwd", grid=(nb,), in_specs=[blk, blk, prev, cst, cst],
        out_specs=[blk, pl.BlockSpec((2, S5_SUBL, S5_LANES), lambda i: (0, 0, 0))],
        out_shape=[jax.ShapeDtypeStruct(gx.shape, F32), jax.ShapeDtypeStruct((2, S5_SUBL, S5_LANES), F32)],
        scratch_shapes=[pltpu.VMEM((2, S5_SUBL, S5_LANES), F32)],
        compiler_params=_params(("arbitrary",)),
    )(gx, x, x, a_re, a_im)


def _place():
    x, y, c = lax.axis_index("x"), lax.axis_index("y"), lax.axis_index("c")
    return x, y, c, [(1 - x, y), (x, 1 - y), (1 - x, 1 - y)]


def _all_gather(shard, name):
    def body(x_ref, out_ref, send_sems, recv_sems, local_sem):
        x, y, c, chips = _place()
        me, sibling = (x, y, c), (x, y, 1 - c)

        def rows(px, py, pc):
            return out_ref.at[4 * px + 2 * py + pc]

        def copy(k, block, to, src=None):
            return pltpu.make_async_remote_copy(
                src_ref=rows(*block) if src is None else src, dst_ref=rows(*block),
                send_sem=send_sems.at[k], recv_sem=recv_sems.at[k], device_id=to, device_id_type=MESH)

        mine = pltpu.make_async_copy(x_ref, rows(*me), local_sem)
        mine.start()
        first = [copy(0, me, sibling, src=x_ref)]
        first += [copy(1 + j, me, (*chip, c), src=x_ref) for j, chip in enumerate(chips)]
        for cp in first:
            cp.start()
        passed = [copy(4 + j, (*chip, c), sibling) for j, chip in enumerate(chips)]
        for j, chip in enumerate(chips):
            copy(1 + j, (*chip, c), me).wait_recv()
            passed[j].start()
        copy(0, sibling, me).wait_recv()
        for j, chip in enumerate(chips):
            copy(4 + j, (*chip, 1 - c), me).wait_recv()
        for cp in first + passed:
            cp.wait_send()
        mine.wait()

    return pl.pallas_call(
        body, name=name, out_shape=jax.ShapeDtypeStruct((N_DEV,) + shard.shape, shard.dtype),
        in_specs=[pl.BlockSpec(memory_space=pl.ANY)], out_specs=pl.BlockSpec(memory_space=pl.ANY),
        scratch_shapes=[pltpu.SemaphoreType.DMA((7,)), pltpu.SemaphoreType.DMA((7,)), pltpu.SemaphoreType.DMA],
    )(shard)


def _swap_with_sibling(parts):
    def body(p_ref, out_ref, send_sems, recv_sems):
        x, y, c, _ = _place()
        copies = [pltpu.make_async_remote_copy(
            src_ref=p_ref.at[2 * chip + (1 - c)], dst_ref=out_ref.at[chip],
            send_sem=send_sems.at[chip], recv_sem=recv_sems.at[chip],
            device_id=(x, y, 1 - c), device_id_type=MESH) for chip in range(4)]
        for cp in copies:
            cp.start()
        for cp in copies:
            cp.wait()

    return pl.pallas_call(
        body, name="rs_sibling", out_shape=jax.ShapeDtypeStruct((4,) + parts.shape[1:], parts.dtype),
        in_specs=[pl.BlockSpec(memory_space=pl.ANY)], out_specs=pl.BlockSpec(memory_space=pl.ANY),
        scratch_shapes=[pltpu.SemaphoreType.DMA((4,)), pltpu.SemaphoreType.DMA((4,))],
    )(parts)


def _exchange_chips(parts):
    def body(p_ref, out_ref, send_sems, recv_sems):
        x, y, c, chips = _place()
        copies = [pltpu.make_async_remote_copy(
            src_ref=p_ref.at[2 * px + py], dst_ref=out_ref.at[j],
            send_sem=send_sems.at[j], recv_sem=recv_sems.at[j],
            device_id=(px, py, c), device_id_type=MESH) for j, (px, py) in enumerate(chips)]
        for cp in copies:
            cp.start()
        for cp in copies:
            cp.wait()

    return pl.pallas_call(
        body, name="rs_chips", out_shape=jax.ShapeDtypeStruct((3,) + parts.shape[1:], parts.dtype),
        in_specs=[pl.BlockSpec(memory_space=pl.ANY)], out_specs=pl.BlockSpec(memory_space=pl.ANY),
        scratch_shapes=[pltpu.SemaphoreType.DMA((3,)), pltpu.SemaphoreType.DMA((3,))],
    )(parts)


def _pack_rows(n_elem, row_mult=PACK_ROW_MULT):
    rows = -(-n_elem // PACK_COLS)
    return -(-rows // row_mult) * row_mult


def _pack(flats, dtype, row_mult=PACK_ROW_MULT):
    flat = jnp.concatenate([f.reshape(-1).astype(dtype) for f in flats])
    rows = _pack_rows(flat.shape[0], row_mult)
    return jnp.pad(flat, (0, rows * PACK_COLS - flat.shape[0])).reshape(rows, PACK_COLS)


def _unpack(buf, shapes):
    lead = buf.shape[:-2]
    flat = buf.reshape(lead + (-1,))
    outs, o = [], 0
    for s in shapes:
        n = math.prod(s)
        outs.append(flat[..., o:o + n].reshape(lead + tuple(s)))
        o += n
    return outs


def _adamw(w, g, m, v, name):
    rows, cols = w.shape
    tr = _tile(rows, 256, 8) if rows % 8 == 0 else rows

    def fn(i, w_, g_, m_, v_):
        m_new = ADAM_B1 * m_ + (1.0 - ADAM_B1) * g_
        v_new = ADAM_B2 * v_ + (1.0 - ADAM_B2) * (g_ * g_)
        m_hat = m_new / (1.0 - ADAM_B1 ** ADAM_STEP)
        v_hat = v_new / (1.0 - ADAM_B2 ** ADAM_STEP)
        delta = -ADAM_LR * (m_hat / (jnp.sqrt(v_hat) + ADAM_EPS) + ADAM_WD * w_)
        return (delta, m_new, v_new), ()

    outs, _ = _rows(fn, [_win(w), _win(g), _win(m), _win(v)], [], [(cols, F32)] * 3, [], name=name,
                    nrow=rows, tr=tr)
    return outs


def _as2d(a):
    if a.ndim == 1:
        return a.reshape(1, -1)
    if a.ndim == 2:
        return a
    a = a.reshape(a.shape[1:])
    return a if a.ndim == 2 else a.reshape(a.shape[0], -1)


def kernel(x, meta, norm_ab_w, w_in_ab, ret_norm_w, s5_lam_re, s5_lam_im, s5_log_dt, s5_b_re, s5_b_im, s5_c_re, s5_c_im, s5_d, s5_w_glu, w_out_ab, norm_c_w, w_in_c, gla_w_gate, gla_b_gate, gla_norm_w, w_out_c, final_norm_w, loss_target, m_meta, m_norm_ab_w, m_w_in_ab, m_ret_norm_w, m_s5_lam_re, m_s5_lam_im, m_s5_log_dt, m_s5_b_re, m_s5_b_im, m_s5_c_re, m_s5_c_im, m_s5_d, m_s5_w_glu, m_w_out_ab, m_norm_c_w, m_w_in_c, m_gla_w_gate, m_gla_b_gate, m_gla_norm_w, m_w_out_c, m_final_norm_w, v_meta, v_norm_ab_w, v_w_in_ab, v_ret_norm_w, v_s5_lam_re, v_s5_lam_im, v_s5_log_dt, v_s5_b_re, v_s5_b_im, v_s5_c_re, v_s5_c_im, v_s5_d, v_s5_w_glu, v_w_out_ab, v_norm_c_w, v_w_in_c, v_gla_w_gate, v_gla_b_gate, v_gla_norm_w, v_w_out_c, v_final_norm_w):
    weights = dict(meta=meta, norm_ab_w=norm_ab_w, w_in_ab=w_in_ab, ret_norm_w=ret_norm_w, s5_lam_re=s5_lam_re,
                   s5_lam_im=s5_lam_im, s5_log_dt=s5_log_dt, s5_b_re=s5_b_re, s5_b_im=s5_b_im, s5_c_re=s5_c_re,
                   s5_c_im=s5_c_im, s5_d=s5_d, s5_w_glu=s5_w_glu, w_out_ab=w_out_ab, norm_c_w=norm_c_w,
                   w_in_c=w_in_c, gla_w_gate=gla_w_gate, gla_b_gate=gla_b_gate, gla_norm_w=gla_norm_w,
                   w_out_c=w_out_c, final_norm_w=final_norm_w)
    mom_m = dict(meta=m_meta, norm_ab_w=m_norm_ab_w, w_in_ab=m_w_in_ab, ret_norm_w=m_ret_norm_w,
                 s5_lam_re=m_s5_lam_re, s5_lam_im=m_s5_lam_im, s5_log_dt=m_s5_log_dt, s5_b_re=m_s5_b_re,
                 s5_b_im=m_s5_b_im, s5_c_re=m_s5_c_re, s5_c_im=m_s5_c_im, s5_d=m_s5_d, s5_w_glu=m_s5_w_glu,
                 w_out_ab=m_w_out_ab, norm_c_w=m_norm_c_w, w_in_c=m_w_in_c, gla_w_gate=m_gla_w_gate,
                 gla_b_gate=m_gla_b_gate, gla_norm_w=m_gla_norm_w, w_out_c=m_w_out_c, final_norm_w=m_final_norm_w)
    mom_v = dict(meta=v_meta, norm_ab_w=v_norm_ab_w, w_in_ab=v_w_in_ab, ret_norm_w=v_ret_norm_w,
                 s5_lam_re=v_s5_lam_re, s5_lam_im=v_s5_lam_im, s5_log_dt=v_s5_log_dt, s5_b_re=v_s5_b_re,
                 s5_b_im=v_s5_b_im, s5_c_re=v_s5_c_re, s5_c_im=v_s5_c_im, s5_d=v_s5_d, s5_w_glu=v_s5_w_glu,
                 w_out_ab=v_w_out_ab, norm_c_w=v_norm_c_w, w_in_c=v_w_in_c, gla_w_gate=v_gla_w_gate,
                 gla_b_gate=v_gla_b_gate, gla_norm_w=v_gla_norm_w, w_out_c=v_w_out_c, final_norm_w=v_final_norm_w)
    order = list(weights)

    seq = x.shape[1]
    lp = CHUNK + seq
    nchunk = lp // CHUNK
    dev = 4 * lax.axis_index("x") + 2 * lax.axis_index("y") + lax.axis_index("c")
    core = lax.axis_index("c")
    chip = 2 * lax.axis_index("x") + lax.axis_index("y")

    big_names = ["w_in_ab", "s5_w_glu", "w_out_ab", "w_in_c", "w_out_c"]
    big_shapes = [weights[k][0].shape for k in big_names]
    small_names = ["meta", "norm_c_w", "gla_w_gate", "gla_b_gate", "gla_norm_w"]
    small_shards = [meta, norm_c_w[0], gla_w_gate[0], gla_b_gate[0], gla_norm_w[0]]
    small_shapes = [s.shape for s in small_shards]
    gb = _all_gather(_pack([weights[k][0] for k in big_names], BF16), "gather_big")
    gs = _all_gather(_pack(small_shards, F32, 8), "gather_small")
    g_in_ab, g_glu, g_out_ab, g_in_c, g_out_c = _unpack(gb, big_shapes)
    w_in_ab_f = g_in_ab.transpose(1, 0, 2).reshape(D_MODEL, IN_AB)
    w_glu_f = g_glu.reshape(S5_W, S5_W)
    w_out_ab_f = g_out_ab.reshape(OUT_AB, D_MODEL)
    w_in_c_f = jnp.pad(g_in_c.transpose(1, 0, 2).reshape(D_MODEL, IN_C), ((0, 0), (0, IN_C_PAD - IN_C)))
    w_out_c_f = g_out_c.reshape(GLA_W, D_MODEL)
    s_meta, s_norm_c, s_wgate, s_bgate, s_gnorm = _unpack(gs, small_shapes)
    meta_f = s_meta.transpose(1, 0, 2).reshape(N_META, D_MODEL)
    norm_c_f = s_norm_c.reshape(1, D_MODEL)
    w_gate_f = jnp.pad(s_wgate.transpose(1, 0, 2).reshape(GLA_RANK, GLA_QK), ((0, 128 - GLA_RANK), (0, 0)))
    b_gate_f = s_bgate.reshape(1, GLA_QK)
    gla_norm_f = s_gnorm.reshape(GLA_H, 1, GLA_DV)

    pos = jnp.maximum(jnp.arange(lp, dtype=F32) - float(PAD), 0.0)
    inv_freq = jnp.power(ROPE_BASE, -jnp.arange(0, RET_DK, 2, dtype=F32) / RET_DK)
    ang = pos[:, None] * inv_freq[None, :]
    cos2 = jnp.concatenate([jnp.cos(ang), jnp.cos(ang)], axis=1)
    sin2 = jnp.concatenate([-jnp.sin(ang), jnp.sin(ang)], axis=1)
    log_g = jnp.log1p(-jnp.exp2(-5.0 - jnp.arange(RET_H, dtype=F32)))
    lg = jnp.broadcast_to(log_g[:, None, None], (RET_H, 1, 128))
    ret_norm_h = ret_norm_w.reshape(RET_H, 1, RET_DV)

    h0 = jnp.concatenate([jnp.zeros((PAD, D_MODEL), F32), meta_f, x[0]], axis=0)

    def rowmask(i):
        return (_iota2((CHUNK, 1), 0) + i * CHUNK) >= PAD

    (hn0,), _ = _rows(lambda i, h, w: ((_rms(h, w),), ()), [_win(h0)], [norm_ab_w], [(D_MODEL, BF16)], [],
                      name="norm_ab_fwd", nrow=lp)
    proj_ab = _mm(hn0, w_in_ab_f, "nn", name="in_ab_fwd")

    q_off, k_off, v_off, za_off = 0, RET_QK, 2 * RET_QK, 2 * RET_QK + RET_W
    u_off, zb_off = 2 * RET_QK + 2 * RET_W, 2 * RET_QK + 2 * RET_W + S5_W
    ret_xs = [(proj_ab, RET_DK, lambda h: q_off // RET_DK + h), (proj_ab, RET_DK, lambda h: k_off // RET_DK + h),
              (proj_ab, RET_DV, lambda h: v_off // RET_DV + h), (proj_ab, RET_DV, lambda h: za_off // RET_DV + h)]
    ret_cs = [(cos2, RET_DK, lambda h: 0), (sin2, RET_DK, lambda h: 0)]
    ret_kw = dict(heads=RET_H, nchunk=nchunk, s_shape=(RET_DK, RET_DV), out_w=RET_DV, pre=_ret_pre)
    o_a, ret_sprev = _scan_fwd(_ret_chunk, ret_xs, ret_cs, [ret_norm_h], [lg], name="ret_fwd", **ret_kw)

    expand = jnp.repeat(jnp.eye(S5_P, dtype=F32), S5_GH, axis=1)
    disc_args = (s5_lam_re[0], s5_lam_im[0], s5_log_dt[0].reshape(S5_G, 1),
                 s5_b_re[0].reshape(S5_G, S5_P * S5_GH), s5_b_im[0].reshape(S5_G, S5_P * S5_GH), expand)
    ab_re, ab_im, bb_re, bb_im = _s5_disc_fwd(disc_args)
    eye_g = jnp.eye(S5_G, dtype=F32)

    def blockdiag_in(bb):
        return jnp.einsum("gph,gk->ghkp", bb.reshape(S5_G, S5_P, S5_GH), eye_g).reshape(S5_W, S5_N)

    def blockdiag_out(cc):
        return jnp.einsum("ghp,gk->gpkh", cc, eye_g).reshape(S5_N, S5_W)

    wb_cat = jnp.concatenate([blockdiag_in(bb_re), blockdiag_in(bb_im)], axis=1).astype(BF16)
    wc_cat = jnp.concatenate([blockdiag_out(s5_c_re[0]), -blockdiag_out(s5_c_im[0])], axis=0).astype(BF16)
    a_re, a_im = ab_re.reshape(S5_SUBL, S5_LANES), ab_im.reshape(S5_SUBL, S5_LANES)
    bu = _mm(proj_ab, wb_cat, "nn", name="s5_bu", a_win=(u_off, S5_W))
    xs5 = _s5_scan_fwd(bu.reshape(lp, 2 * S5_SUBL, S5_LANES), a_re, a_im)
    xs5_2d = xs5.reshape(lp, 2 * S5_N)
    y_pre = _mm(xs5_2d, wc_cat, "nn", name="s5_cx")
    (y_s5, yg_bf), _ = _rows(
        lambda i, yp, u, d: ((yp + d * u, _gelu(yp + d * u)), ()),
        [_win(y_pre), _win(proj_ab, u_off, S5_W)], [s5_d], [(S5_W, F32), (S5_W, BF16)], [], name="s5_gelu_fwd", nrow=lp)
    t_glu = _mm(yg_bf, w_glu_f, "nn", name="s5_glu_fwd")

    def s5_gate(y, t, zb):
        return _gelu(y) * _sigmoid(t) * _silu(zb)

    (o_b,), _ = _rows(lambda i, y, t, zb: ((s5_gate(y, t, zb),), ()),
                      [_win(y_s5), _win(t_glu), _win(proj_ab, zb_off, S5_W)], [], [(S5_W, BF16)], [],
                      name="s5_gate_fwd", nrow=lp)
    o_ab = jnp.concatenate([o_a, o_b], axis=1)
    h1 = _mm(o_ab, w_out_ab_f, "nn", name="out_ab_fwd", add=h0)

    (hn1,), _ = _rows(lambda i, h, w: ((_rms(h, w),), ()), [_win(h1)], [norm_c_f], [(D_MODEL, BF16)], [],
                      name="norm_c_fwd", nrow=lp)
    proj_c = _mm(hn1, w_in_c_f, "nn", name="in_c_fwd")
    gl_off = 2 * GLA_QK + 2 * GLA_W
    pre_gate = _mm(proj_c, w_gate_f, "nn", name="gate_fwd", a_win=(gl_off, 128), bias=b_gate_f)
    gla_xs = [(proj_c, GLA_DK, lambda h: h), (proj_c, GLA_DK, lambda h: GLA_QK // GLA_DK + h),
              (proj_c, GLA_DV, lambda h: 2 * GLA_QK // GLA_DV + h),
              (proj_c, GLA_DV, lambda h: (2 * GLA_QK + GLA_W) // GLA_DV + h),
              (pre_gate, GLA_DK, lambda h: h)]
    gla_kw = dict(heads=GLA_H, nchunk=nchunk, s_shape=(GLA_DV, GLA_DK), out_w=GLA_DV)
    o_c, gla_sprev = _scan_fwd(_gla_chunk, gla_xs, [], [gla_norm_f], [], name="gla_fwd", **gla_kw)
    h2 = _mm(o_c, w_out_c_f, "nn", name="out_c_fwd", add=h1)

    fnw = final_norm_w.reshape(1, D_MODEL)

    def final_fn(i, h, tgt, w):
        def loss_of(h_, w_):
            err = _rms(h_, w_) - tgt
            return 0.5 * jnp.sum(jnp.mean(err * err, axis=-1))

        real = (i > 0).astype(F32)
        loss_i, (dh, dw) = jax.value_and_grad(loss_of, argnums=(0, 1))(h, w)
        return (dh * real,), (jnp.full((1, 128), loss_i * real, F32), dw * real)

    (dh2,), (loss_acc, g_final) = _rows(final_fn, [_win(h2), _win(loss_target[0], roff=1)], [fnw],
                                        [(D_MODEL, F32)], [(1, 128), (1, D_MODEL)], name="final_loss", nrow=lp)

    dh2_bf = dh2.astype(BF16)
    do_c = _mm(dh2_bf, w_out_c_f, "nt", name="out_c_dx", out_dtype=BF16)
    gw_out_c = _mm(o_c, dh2_bf, "tn", name="out_c_dw")
    (dq_c, dk_c, dv_c, dz_c, dpre), (g_gla_norm,) = _scan_bwd(
        _gla_chunk, gla_xs, [], [gla_norm_f], [], do_c, gla_sprev, name="gla_bwd", **gla_kw)
    dglow = _mm(dpre, w_gate_f, "nt", name="gate_dx", out_dtype=BF16)
    g_wgate = _mm(proj_c, dpre, "tn", name="gate_dw", a_win=(gl_off, 128))[:GLA_RANK]
    (), (g_bgate,) = _rows(lambda i, d: ((), (jnp.sum(d.astype(F32), axis=0, keepdims=True),)), [_win(dpre)], [], [],
                           [(1, GLA_QK)], name="gate_db", nrow=lp)
    dproj_c = jnp.concatenate([dq_c, dk_c, dv_c, dz_c, dglow], axis=1)
    dhn1 = _mm(dproj_c, w_in_c_f, "nt", name="in_c_dx")
    gw_in_c = _mm(hn1, dproj_c, "tn", name="in_c_dw")[:, :IN_C]

    def norm_bwd(i, h, dhn, dres, w):
        _, vjp = jax.vjp(_rms, h, w)
        dh, dw = vjp(dhn)
        return (jnp.where(rowmask(i), dh + dres, 0.0),), (dw,)

    (dh1,), (g_norm_c,) = _rows(norm_bwd, [_win(h1), _win(dhn1), _win(dh2)], [norm_c_f], [(D_MODEL, F32)],
                                [(1, D_MODEL)], name="norm_c_bwd", nrow=lp)

    dh1_bf = dh1.astype(BF16)
    do_ab = _mm(dh1_bf, w_out_ab_f, "nt", name="out_ab_dx", out_dtype=BF16)
    gw_out_ab = _mm(o_ab, dh1_bf, "tn", name="out_ab_dw")

    def s5_gate_bwd(i, dob, y, t, zb):
        _, vjp = jax.vjp(s5_gate, y, t, zb)
        dy, dt, dzb = vjp(dob.astype(F32))
        return (dy, dt, dzb), ()

    (dy_a, dt_glu, dzb), _ = _rows(
        s5_gate_bwd, [_win(do_ab, RET_W, S5_W), _win(y_s5), _win(t_glu), _win(proj_ab, zb_off, S5_W)], [],
        [(S5_W, F32), (S5_W, BF16), (S5_W, BF16)], [], name="s5_gate_bwd", nrow=lp)
    dyg2 = _mm(dt_glu, w_glu_f, "nt", name="s5_glu_dx")
    gw_glu = _mm(yg_bf, dt_glu, "tn", name="s5_glu_dw")

    def s5_y_bwd(i, dya, dyg, y, u, d):
        _, vjp = jax.vjp(_gelu, y)
        (dy_g,) = vjp(dyg)
        dy = dya + dy_g
        return (dy, d * dy), (jnp.sum(dy * u, axis=0, keepdims=True),)

    (dy_s5, du1), (g_d,) = _rows(
        s5_y_bwd, [_win(dy_a), _win(dyg2), _win(y_s5), _win(proj_ab, u_off, S5_W)], [s5_d],
        [(S5_W, BF16), (S5_W, F32)], [(1, S5_W)], name="s5_y_bwd", nrow=lp)
    gx = _mm(dy_s5, wc_cat, "nt", name="s5_cx_dx")
    gwc = _mm(xs5_2d, dy_s5, "tn", name="s5_cx_dw")
    g_s5, da = _s5_scan_bwd(gx.reshape(lp, 2 * S5_SUBL, S5_LANES), xs5, a_re, a_im)
    g_s5_2d = g_s5.reshape(lp, 2 * S5_N)
    du = _mm(g_s5_2d, wb_cat, "nt", name="s5_bu_dx", add=du1, out_dtype=BF16)
    gwb = _mm(proj_ab, g_s5_2d, "tn", name="s5_bu_dw", a_win=(u_off, S5_W))
    gwc5 = gwc.reshape(2, S5_G, S5_P, S5_G, S5_GH)
    g_c_re = jnp.einsum("gpgh->ghp", gwc5[0])
    g_c_im = -jnp.einsum("gpgh->ghp", gwc5[1])
    gwb5 = gwb.reshape(S5_G, S5_GH, 2, S5_G, S5_P)
    d_bb_re = jnp.einsum("ghgp->gph", gwb5[:, :, 0]).reshape(S5_G, S5_P * S5_GH)
    d_bb_im = jnp.einsum("ghgp->gph", gwb5[:, :, 1]).reshape(S5_G, S5_P * S5_GH)
    g_lam_re, g_lam_im, g_log_dt, g_b_re, g_b_im = _s5_disc_bwd(
        disc_args, (da[0].reshape(S5_G, S5_P), da[1].reshape(S5_G, S5_P), d_bb_re, d_bb_im))

    (dq_a, dk_a, dv_a, dz_a), (g_ret_norm,) = _scan_bwd(
        _ret_chunk, ret_xs, ret_cs, [ret_norm_h], [lg], do_ab, ret_sprev, name="ret_bwd", post=_ret_post, **ret_kw)
    dproj_ab = jnp.concatenate([dq_a, dk_a, dv_a, dz_a, du, dzb], axis=1)
    dhn0 = _mm(dproj_ab, w_in_ab_f, "nt", name="in_ab_dx")
    gw_in_ab = _mm(hn0, dproj_ab, "tn", name="in_ab_dw")
    (dh0,), (g_norm_ab,) = _rows(norm_bwd, [_win(h0), _win(dhn0), _win(dh1)], [norm_ab_w], [(D_MODEL, F32)],
                                 [(1, D_MODEL)], name="norm_ab_bwd", nrow=lp)
    grad_x = dh0[CHUNK:][None]
    g_meta_part = dh0[PAD:CHUNK]

    def by_dest(g, axis, n):
        if axis == 1:
            return g.reshape(g.shape[0], N_DEV, n).transpose(1, 0, 2)
        return g.reshape(N_DEV, n, g.shape[1])

    dest_parts = [by_dest(gw_in_ab, 1, IN_AB // N_DEV), by_dest(gw_glu, 0, S5_W // N_DEV),
                  by_dest(gw_out_ab, 0, OUT_AB // N_DEV), by_dest(gw_in_c, 1, IN_C // N_DEV),
                  by_dest(gw_out_c, 0, GLA_W // N_DEV)]
    flat_dest = jnp.concatenate([p.reshape(N_DEV, -1) for p in dest_parts], axis=1)
    prow = _pack_rows(flat_dest.shape[1])
    g_full = jnp.pad(flat_dest, ((0, 0), (0, prow * PACK_COLS - flat_dest.shape[1]))).reshape(N_DEV, prow, PACK_COLS)
    from_sibling = _swap_with_sibling(g_full.astype(BF16))
    mine_by_chip = lax.dynamic_index_in_dim(g_full.reshape(4, 2, prow, PACK_COLS), core, axis=1, keepdims=False)
    (p1, p1_bf), _ = _rows(
        lambda i, a, b: ((a + b.astype(F32), a + b.astype(F32)), ()),
        [_win(mine_by_chip.reshape(4 * prow, PACK_COLS)), _win(from_sibling.reshape(4 * prow, PACK_COLS))], [],
        [(PACK_COLS, F32), (PACK_COLS, BF16)], [], name="rs_sum_sibling", nrow=4 * prow, tr=PACK_ROW_MULT)
    from_chips = _exchange_chips(p1_bf.reshape(4, prow, PACK_COLS))
    own = lax.dynamic_index_in_dim(p1.reshape(4, prow, PACK_COLS), chip, axis=0, keepdims=False)
    nblk = prow // PACK_ROW_MULT
    fc2 = from_chips.reshape(3 * prow, PACK_COLS)
    (g_shard,), _ = _rows(
        lambda i, a, b0, b1, b2: ((((a + b0.astype(F32)) + b1.astype(F32)) + b2.astype(F32),), ()),
        [_win(own), _win(fc2), _win(fc2, roff=-nblk), _win(fc2, roff=-2 * nblk)], [], [(PACK_COLS, F32)], [],
        name="rs_sum_chips", nrow=prow, tr=PACK_ROW_MULT)
    big_grads = dict(zip(big_names, _unpack(g_shard, big_shapes)))

    small_parts = dict(
        norm_ab_w=g_norm_ab, ret_norm_w=g_ret_norm, s5_lam_re=g_lam_re, s5_lam_im=g_lam_im, s5_log_dt=g_log_dt,
        s5_b_re=g_b_re, s5_b_im=g_b_im, s5_c_re=g_c_re, s5_c_im=g_c_im, s5_d=g_d, final_norm_w=g_final,
        meta=g_meta_part, norm_c_w=g_norm_c, gla_w_gate=g_wgate, gla_b_gate=g_bgate, gla_norm_w=g_gla_norm,
        loss=loss_acc[:, :1])
    sp_names = list(small_parts)
    sp_shapes = [small_parts[k].shape for k in sp_names]
    small_tr = 64
    sp_pack = _pack([small_parts[k] for k in sp_names], F32, small_tr)
    srow = sp_pack.shape[0]
    sp_all = _all_gather(sp_pack, "gather_grads").reshape(N_DEV * srow, PACK_COLS)
    snb = srow // small_tr

    def sum8(i, *blocks):
        acc = blocks[0]
        for b in blocks[1:]:
            acc = acc + b
        return (acc,), ()

    (sp_sum,), _ = _rows(sum8, [_win(sp_all, roff=-d * snb) for d in range(N_DEV)], [], [(PACK_COLS, F32)], [],
                         name="sum_small", nrow=srow, tr=small_tr)
    small = dict(zip(sp_names, _unpack(sp_sum, sp_shapes)))
    loss = small["loss"].reshape(())

    def my_cols(g, n):
        return lax.dynamic_slice_in_dim(g, dev * n, n, axis=g.ndim - 1)

    grads = dict(
        meta=my_cols(small["meta"], D_MODEL // N_DEV),
        norm_ab_w=small["norm_ab_w"], w_in_ab=big_grads["w_in_ab"][None], ret_norm_w=small["ret_norm_w"].reshape(1, RET_W),
        s5_lam_re=small["s5_lam_re"][None], s5_lam_im=small["s5_lam_im"][None],
        s5_log_dt=small["s5_log_dt"].reshape(1, S5_G),
        s5_b_re=small["s5_b_re"].reshape(1, S5_G, S5_P, S5_GH), s5_b_im=small["s5_b_im"].reshape(1, S5_G, S5_P, S5_GH),
        s5_c_re=small["s5_c_re"][None], s5_c_im=small["s5_c_im"][None], s5_d=small["s5_d"],
        s5_w_glu=big_grads["s5_w_glu"][None], w_out_ab=big_grads["w_out_ab"][None],
        norm_c_w=my_cols(small["norm_c_w"], D_MODEL // N_DEV), w_in_c=big_grads["w_in_c"][None],
        gla_w_gate=my_cols(small["gla_w_gate"], GLA_QK // N_DEV)[None],
        gla_b_gate=my_cols(small["gla_b_gate"], GLA_QK // N_DEV),
        gla_norm_w=my_cols(small["gla_norm_w"].reshape(1, GLA_W), GLA_W // N_DEV),
        w_out_c=big_grads["w_out_c"][None], final_norm_w=small["final_norm_w"].reshape(D_MODEL))

    deltas, new_m, new_v = {}, {}, {}
    for k in order:
        w = weights[k]
        d2, m2, v2 = _adamw(_as2d(w), _as2d(grads[k].reshape(w.shape)), _as2d(mom_m[k]), _as2d(mom_v[k]), "adamw_" + k)
        deltas[k], new_m[k], new_v[k] = d2.reshape(w.shape), m2.reshape(w.shape), v2.reshape(w.shape)
        grads[k] = grads[k].reshape(w.shape)

    return (loss, grad_x, *[grads[k] for k in order], *[deltas[k] for k in order],
            *[new_m[k] for k in order], *[new_v[k] for k in order])
```

```python
import functools
import math

import jax
import jax.numpy as jnp
from jax import lax
from jax.experimental import pallas as pl
from jax.experimental.pallas import tpu as pltpu

F32, BF16 = jnp.float32, jnp.bfloat16
MESH = pl.DeviceIdType.MESH
N_DEV = 8

D_MODEL = 2048
CHUNK = 128
N_META = 16
PAD = CHUNK - N_META
SUB = 16
EPS = 1e-6
RET_H, RET_DK, RET_DV = 8, 128, 256
RET_QK, RET_W = RET_H * RET_DK, RET_H * RET_DV
ROPE_BASE = 10000.0
S5_W, S5_G, S5_P, S5_GH = 1024, 64, 64, 16
S5_N = S5_G * S5_P
GLA_H, GLA_DK, GLA_DV, GLA_RANK, GLA_TAU = 4, 256, 512, 16, 16.0
GLA_QK, GLA_W = GLA_H * GLA_DK, GLA_H * GLA_DV
IN_AB = 2 * RET_QK + 2 * RET_W + 2 * S5_W
OUT_AB = RET_W + S5_W
IN_C = 2 * GLA_QK + 2 * GLA_W + GLA_RANK
GATE_PAD = 256
IN_C_PAD = 2 * GLA_QK + 2 * GLA_W + GATE_PAD
ADAM_LR, ADAM_B1, ADAM_B2, ADAM_EPS, ADAM_WD, ADAM_STEP = 0.001, 0.9, 0.999, 1e-08, 0.01, 10

VMEM_LIMIT_BYTES = 48 * 2 ** 20
PACK_COLS = 1024
PACK_ROW_MULT = 8
SHARD_C = IN_C // N_DEV
WIN_STEP = 768
WIN_COLS = 1024


def _params(sem):
    return pltpu.CompilerParams(dimension_semantics=sem, vmem_limit_bytes=VMEM_LIMIT_BYTES)


def _tile(n, cap, mult):
    best = None
    for t in range(mult, min(n, cap) + 1, mult):
        if n % t == 0:
            best = t
    assert best is not None, (n, cap, mult)
    return best


def _dg(a, b, ca, cb):
    return lax.dot_general(a.astype(BF16), b.astype(BF16), (((ca,), (cb,)), ((), ())),
                           preferred_element_type=F32)


@functools.partial(jax.custom_vjp, nondiff_argnums=(2, 3))
def _bdot(a, b, ca, cb):
    return _dg(a, b, ca, cb)


def _bdot_fwd(a, b, ca, cb):
    return _dg(a, b, ca, cb), (a, b)


def _bdot_bwd(ca, cb, res, g):
    a, b = res
    da = _dg(g, b, 1, 1 - cb) if ca == 1 else _dg(b, g, 1 - cb, 1)
    db = _dg(a, g, 1 - ca, 0) if cb == 0 else _dg(g, a, 0, 1 - ca)
    return da.astype(a.dtype), db.astype(b.dtype)


_bdot.defvjp(_bdot_fwd, _bdot_bwd)


def _sigmoid(x):
    return 1.0 / (1.0 + jnp.exp(-x))


def _silu(x):
    return x * _sigmoid(x)


def _log_sigmoid(x):
    return jnp.minimum(x, 0.0) - jnp.log(1.0 + jnp.exp(-jnp.abs(x)))


def _gelu(x):
    return 0.5 * x * (1.0 + jnp.tanh(math.sqrt(2.0 / math.pi) * (x + 0.044715 * (x * x * x))))


def _rms(x, w):
    return x * lax.rsqrt(jnp.mean(x * x, axis=-1, keepdims=True) + EPS) * w


def _mm(a, b, mode, *, name, out_dtype=F32, a_win=None, add=None, bias=None):
    if mode == "tn":
        kdim, n = a.shape[0], b.shape[1]
        m = a.shape[1] if a_win is None else a_win[1]
        tm, tn, tk = _tile(m, 512, 128), _tile(n, 1024, 128), _tile(kdim, 1408, 8)
        off = 0 if a_win is None else a_win[0] // tm
        a_spec = pl.BlockSpec((tk, tm), lambda i, j, k: (k, i + off))
        b_spec = pl.BlockSpec((tk, tn), lambda i, j, k: (k, j))
        dims = (((0,), (0,)), ((), ()))
    else:
        m = a.shape[0]
        kdim = a.shape[1] if a_win is None else a_win[1]
        n = b.shape[1] if mode == "nn" else b.shape[0]
        tm, tn, tk = _tile(m, 1408, 8), _tile(n, 640, 128), _tile(kdim, 1024, 128)
        off = 0 if a_win is None else a_win[0] // tk
        a_spec = pl.BlockSpec((tm, tk), lambda i, j, k: (i, k + off))
        if mode == "nn":
            b_spec = pl.BlockSpec((tk, tn), lambda i, j, k: (k, j))
            dims = (((1,), (0,)), ((), ()))
        else:
            b_spec = pl.BlockSpec((tn, tk), lambda i, j, k: (j, k))
            dims = (((1,), (1,)), ((), ()))
    if a_win is not None:
        assert a_win[0] % (tm if mode == "tn" else tk) == 0
    nk = kdim // tk
    has_add, has_bias = add is not None, bias is not None

    def body(*refs):
        a_ref, b_ref = refs[0], refs[1]
        o_ref, acc = refs[-2], refs[-1]
        k = pl.program_id(2)

        @pl.when(k == 0)
        def _():
            acc[...] = jnp.zeros_like(acc)

        acc[...] += lax.dot_general(a_ref[...].astype(BF16), b_ref[...].astype(BF16), dims,
                                    preferred_element_type=F32)

        @pl.when(k == nk - 1)
        def _():
            r = acc[...]
            if has_add:
                r = r + refs[2][...].astype(F32)
            if has_bias:
                r = r + refs[2 + has_add][...]
            o_ref[...] = r.astype(out_dtype)

    in_specs, operands = [a_spec, b_spec], [a, b]
    if has_add:
        in_specs.append(pl.BlockSpec((tm, tn), lambda i, j, k: (i, j)))
        operands.append(add)
    if has_bias:
        in_specs.append(pl.BlockSpec((1, tn), lambda i, j, k: (0, j)))
        operands.append(bias)
    return pl.pallas_call(
        body, name=name, grid=(m // tm, n // tn, nk),
        in_specs=in_specs, out_specs=pl.BlockSpec((tm, tn), lambda i, j, k: (i, j)),
        out_shape=jax.ShapeDtypeStruct((m, n), out_dtype),
        scratch_shapes=[pltpu.VMEM((tm, tn), F32)],
        compiler_params=_params(("arbitrary", "arbitrary", "arbitrary")),
    )(*operands)


def _win(arr, col0=0, width=None, roff=0):
    return (arr, col0, arr.shape[1] if width is None else width, roff)


def _rows(fn, rows, consts, outs, accs, *, name, nrow, tr=CHUNK):
    nr, nc, no = len(rows), len(consts), len(outs)

    def body(*refs):
        i = pl.program_id(0)
        ins = [r[...] for r in refs[:nr + nc]]
        o_refs = refs[nr + nc:nr + nc + no]
        a_refs = refs[nr + nc + no:]
        res_o, res_a = fn(i, *ins)
        for r, v in zip(o_refs, res_o):
            r[...] = v.astype(r.dtype)
        if a_refs:
            @pl.when(i == 0)
            def _():
                for r in a_refs:
                    r[...] = jnp.zeros_like(r)

            for r, v in zip(a_refs, res_a):
                r[...] += v

    in_specs = []
    for (arr, col0, width, roff) in rows:
        assert col0 % width == 0 and arr.shape[0] % tr == 0
        in_specs.append(pl.BlockSpec((tr, width), lambda i, c=col0 // width, ro=roff: (jnp.maximum(i - ro, 0), c)))
    for c in consts:
        in_specs.append(pl.BlockSpec(c.shape, lambda i, nd=c.ndim: (0,) * nd))
    out_specs = [pl.BlockSpec((tr, w), lambda i: (i, 0)) for (w, _) in outs]
    out_specs += [pl.BlockSpec(s, lambda i, nd=len(s): (0,) * nd) for s in accs]
    out_shape = [jax.ShapeDtypeStruct((nrow, w), dt) for (w, dt) in outs]
    out_shape += [jax.ShapeDtypeStruct(s, F32) for s in accs]
    res = pl.pallas_call(
        body, name=name, grid=(nrow // tr,), in_specs=in_specs, out_specs=out_specs, out_shape=out_shape,
        compiler_params=_params(("arbitrary",)),
    )(*[r[0] for r in rows], *consts)
    return res[:no], res[no:]


def _scan_specs(xs, cs, ws, ks, chunk_of):
    specs = []
    for (arr, width, colfn) in list(xs) + list(cs):
        specs.append(pl.BlockSpec((CHUNK, width), lambda h, n, f=colfn: (chunk_of(n), f(h))))
    for arr in list(ws) + list(ks):
        specs.append(pl.BlockSpec((1, 1, arr.shape[2]), lambda h, n: (h, 0, 0)))
    return specs


def _scan_fwd(fn, xs, cs, ws, ks, *, heads, nchunk, s_shape, out_w, name, pre=None):
    nx, ncs, nw = len(xs), len(cs), len(ws)

    def body(*refs):
        n = pl.program_id(1)
        nin = nx + ncs + nw + len(ks)
        y_ref, sp_ref, s_scr = refs[nin:]

        @pl.when(n == 0)
        def _():
            s_scr[...] = jnp.zeros_like(s_scr)

        state = s_scr[...]
        sp_ref[0, 0] = state
        xv = [r[...] for r in refs[:nx]]
        cv = [r[...] for r in refs[nx:nx + ncs]]
        wv = [r[0] for r in refs[nx + ncs:nx + ncs + nw]]
        kv = [r[0] for r in refs[nx + ncs + nw:nin]]
        if pre is not None:
            xv = pre(xv, cv)
        y, s_new = fn(n, xv, state, cv, wv, kv)
        y_ref[...] = y.astype(y_ref.dtype)
        s_scr[...] = s_new

    lp = nchunk * CHUNK
    return pl.pallas_call(
        body, name=name, grid=(heads, nchunk),
        in_specs=_scan_specs(xs, cs, ws, ks, lambda n: n),
        out_specs=[pl.BlockSpec((CHUNK, out_w), lambda h, n: (n, h)),
                   pl.BlockSpec((1, 1) + s_shape, lambda h, n: (h, n, 0, 0))],
        out_shape=[jax.ShapeDtypeStruct((lp, heads * out_w), BF16),
                   jax.ShapeDtypeStruct((heads, nchunk) + s_shape, F32)],
        scratch_shapes=[pltpu.VMEM(s_shape, F32)],
        compiler_params=_params(("arbitrary", "arbitrary")),
    )(*[t[0] for t in xs], *[t[0] for t in cs], *ws, *ks)


def _scan_bwd(fn, xs, cs, ws, ks, dy, sprev, *, heads, nchunk, s_shape, out_w, name, pre=None, post=None):
    nx, ncs, nw = len(xs), len(cs), len(ws)
    nin = nx + ncs + nw + len(ks)

    def body(*refs):
        step = pl.program_id(1)
        n = nchunk - 1 - step
        dy_ref, sp_ref = refs[nin], refs[nin + 1]
        dx_refs = refs[nin + 2:nin + 2 + nx]
        dw_refs = refs[nin + 2 + nx:nin + 2 + nx + nw]
        ds_scr = refs[-1]

        @pl.when(step == 0)
        def _():
            ds_scr[...] = jnp.zeros_like(ds_scr)
            for r in dw_refs:
                r[...] = jnp.zeros_like(r)

        xv = [r[...] for r in refs[:nx]]
        cv = [r[...] for r in refs[nx:nx + ncs]]
        wv = [r[0] for r in refs[nx + ncs:nx + ncs + nw]]
        kv = [r[0] for r in refs[nx + ncs + nw:nin]]
        if pre is not None:
            xv = pre(xv, cv)
        _, vjp = jax.vjp(lambda xs_, s_, ws_: fn(n, xs_, s_, cv, ws_, kv), xv, sp_ref[0, 0], wv)
        dxs, ds_prev, dws = vjp((dy_ref[...].astype(F32), ds_scr[...]))
        if post is not None:
            dxs = post(dxs, cv)
        for r, v in zip(dx_refs, dxs):
            r[...] = v.astype(r.dtype)
        for r, v in zip(dw_refs, dws):
            r[0] += v
        ds_scr[...] = ds_prev

    lp = nchunk * CHUNK
    rev = lambda n: nchunk - 1 - n
    in_specs = _scan_specs(xs, cs, ws, ks, rev)
    in_specs.append(pl.BlockSpec((CHUNK, out_w), lambda h, n: (rev(n), h)))
    in_specs.append(pl.BlockSpec((1, 1) + s_shape, lambda h, n: (h, rev(n), 0, 0)))
    out_specs = [pl.BlockSpec((CHUNK, w), lambda h, n: (rev(n), h)) for (_, w, _) in xs]
    out_specs += [pl.BlockSpec((1, 1, w.shape[2]), lambda h, n: (h, 0, 0)) for w in ws]
    out_shape = [jax.ShapeDtypeStruct((lp, heads * w), BF16) for (_, w, _) in xs]
    out_shape += [jax.ShapeDtypeStruct(w.shape, F32) for w in ws]
    res = pl.pallas_call(
        body, name=name, grid=(heads, nchunk), in_specs=in_specs, out_specs=out_specs, out_shape=out_shape,
        scratch_shapes=[pltpu.VMEM(s_shape, F32)],
        compiler_params=_params(("arbitrary", "arbitrary")),
    )(*[t[0] for t in xs], *[t[0] for t in cs], *ws, *ks, dy, sprev)
    return res[:nx], res[nx:]


def _iota2(shape, dim):
    return lax.broadcasted_iota(jnp.int32, shape, dim)


def _ret_chunk(n, xs, state, cs, ws, ks):
    q, k, v, z = xs
    (w,), (lg,) = ws, ks
    lgc = lg[:, :1]
    row, col = _iota2((CHUNK, CHUNK), 0), _iota2((CHUNK, CHUNK), 1)
    diff = jnp.maximum(row - col, 0).astype(F32)
    decay = jnp.where(row >= col, jnp.exp(lg * diff), 0.0)
    scores = _bdot(q, k, 1, 1) * decay
    o_intra = _bdot(scores, v, 1, 0)
    idx = _iota2((CHUNK, 1), 0).astype(F32)
    k_w = k * jnp.exp(lgc * (CHUNK - 1.0 - idx))
    kv = _bdot(k_w, v, 0, 0)
    s_new = state * jnp.exp(lgc * float(CHUNK)) + kv
    q_w = q * jnp.exp(lgc * (idx + 1.0))
    o = o_intra + _bdot(q_w, state, 1, 0)
    return _rms(o, w) * _silu(z), s_new


def _rope(t, cos2, sin2):
    return t * cos2 + pltpu.roll(t, RET_DK // 2, 1) * sin2


def _rope_t(g, cos2, sin2):
    return g * cos2 - pltpu.roll(g, RET_DK // 2, 1) * sin2


def _ret_pre(xv, cv):
    q, k, v, z = xv
    cos2, sin2 = cv
    return [_rope(q, cos2, sin2), _rope(k, cos2, sin2) * (RET_DK ** -0.5), v, z]


def _ret_post(dxs, cv):
    dq, dk, dv, dz = dxs
    cos2, sin2 = cv
    return [_rope_t(dq, cos2, sin2), _rope_t(dk, cos2, sin2) * (RET_DK ** -0.5), dv, dz]


def _gla_chunk(n, xs, state_t, cs, ws, ks):
    q, k, v, z, pre = xs
    (w,) = ws
    q = q * (GLA_DK ** -0.5)
    rowc = _iota2((CHUNK, 1), 0)
    valid = jnp.logical_or(n > 0, rowc >= PAD)
    log_a = jnp.where(valid, _log_sigmoid(pre) / GLA_TAU, 0.0)
    row, col = _iota2((CHUNK, CHUNK), 0), _iota2((CHUNK, CHUNK), 1)
    tri = (row >= col).astype(F32)
    b = jnp.dot(tri, log_a, precision=lax.Precision.HIGHEST, preferred_element_type=F32)
    b_last = b[CHUNK - 1:CHUNK, :]
    kv_t = _bdot(v, k * jnp.exp(b_last - b), 0, 0)
    s_new = state_t * jnp.exp(b_last) + kv_t
    o_inter = _bdot(q * jnp.exp(b), state_t, 1, 1)
    outs = []
    for s in range(CHUNK // SUB):
        lo, hi = s * SUB, (s + 1) * SUB
        b_ref = jnp.zeros_like(b_last) if s == 0 else b[lo - 1:lo, :]
        q_hat = q[lo:hi] * jnp.exp(b[lo:hi] - b_ref)
        k_hat = k[:hi] * jnp.exp(b_ref - b[:hi])
        sc = _bdot(q_hat, k_hat, 1, 1)
        causal = _iota2((SUB, hi), 0) + lo >= _iota2((SUB, hi), 1)
        outs.append(_bdot(jnp.where(causal, sc, 0.0), v[:hi], 1, 0))
    o = jnp.concatenate(outs, axis=0) + o_inter
    return _rms(o, w) * _silu(z), s_new


def _s5_disc(lam_re, lam_im, log_dt, b_re, b_im, expand):
    dt = jnp.exp(log_dt)
    mag = jnp.exp(lam_re * dt)
    ab_re, ab_im = mag * jnp.cos(lam_im * dt), mag * jnp.sin(lam_im * dt)
    den = lam_re * lam_re + lam_im * lam_im
    nr, ni = ab_re - 1.0, ab_im
    f_re = (nr * lam_re + ni * lam_im) / den
    f_im = (ni * lam_re - nr * lam_im) / den
    hp = lax.Precision.HIGHEST
    f_re = jnp.dot(f_re, expand, precision=hp, preferred_element_type=F32)
    f_im = jnp.dot(f_im, expand, precision=hp, preferred_element_type=F32)
    return ab_re, ab_im, f_re * b_re - f_im * b_im, f_re * b_im + f_im * b_re


def _s5_disc_fwd(args):
    def body(*refs):
        outs = _s5_disc(*[r[...] for r in refs[:6]])
        for r, v in zip(refs[6:], outs):
            r[...] = v

    g, p = args[0].shape
    return pl.pallas_call(
        body, name="s5_disc_fwd",
        out_shape=[jax.ShapeDtypeStruct((g, p), F32)] * 2 + [jax.ShapeDtypeStruct(args[3].shape, F32)] * 2,
    )(*args)


def _s5_disc_bwd(args, cts):
    def body(*refs):
        prim = [r[...] for r in refs[:5]]
        expand = refs[5][...]
        ct = tuple(r[...] for r in refs[6:10])
        _, vjp = jax.vjp(lambda *a: _s5_disc(*a, expand), *prim)
        for r, v in zip(refs[10:], vjp(ct)):
            r[...] = v

    return pl.pallas_call(
        body, name="s5_disc_bwd", out_shape=[jax.ShapeDtypeStruct(a.shape, F32) for a in args[:5]],
    )(*args, *cts)


S5_SUBL = 8
S5_LANES = S5_N // S5_SUBL
S5_TB = 64


def _s5_scan_fwd(bu, a_re, a_im):
    lp = bu.shape[0]

    def body(bu_ref, ar_ref, ai_ref, x_ref, st):
        @pl.when(pl.program_id(0) == 0)
        def _():
            st[...] = jnp.zeros_like(st)

        ar, ai = ar_ref[...], ai_ref[...]

        def step(t, carry):
            xr, xi = carry
            nr = ar * xr - ai * xi + bu_ref[t, 0:S5_SUBL, :]
            ni = ar * xi + ai * xr + bu_ref[t, S5_SUBL:2 * S5_SUBL, :]
            x_ref[t, 0:S5_SUBL, :] = nr
            x_ref[t, S5_SUBL:2 * S5_SUBL, :] = ni
            return nr, ni

        xr, xi = lax.fori_loop(0, S5_TB, step, (st[0], st[1]))
        st[0] = xr
        st[1] = xi

    blk = pl.BlockSpec((S5_TB, 2 * S5_SUBL, S5_LANES), lambda i: (i, 0, 0))
    cst = pl.BlockSpec((S5_SUBL, S5_LANES), lambda i: (0, 0))
    return pl.pallas_call(
        body, name="s5_scan_fwd", grid=(lp // S5_TB,), in_specs=[blk, cst, cst], out_specs=blk,
        out_shape=jax.ShapeDtypeStruct(bu.shape, F32),
        scratch_shapes=[pltpu.VMEM((2, S5_SUBL, S5_LANES), F32)],
        compiler_params=_params(("arbitrary",)),
    )(bu, a_re, a_im)


def _s5_scan_bwd(gx, x, a_re, a_im):
    lp = gx.shape[0]
    nb = lp // S5_TB

    def body(gx_ref, x_ref, xp_ref, ar_ref, ai_ref, g_ref, da_ref, st):
        i = pl.program_id(0)

        @pl.when(i == 0)
        def _():
            st[...] = jnp.zeros_like(st)
            da_ref[...] = jnp.zeros_like(da_ref)

        ar, ai = ar_ref[...], ai_ref[...]
        first = (i == nb - 1).astype(F32)

        def step(s, carry):
            gr, gi, dar, dai = carry
            t = S5_TB - 1 - s
            ngr = gx_ref[t, 0:S5_SUBL, :] + ar * gr + ai * gi
            ngi = gx_ref[t, S5_SUBL:2 * S5_SUBL, :] + ar * gi - ai * gr
            g_ref[t, 0:S5_SUBL, :] = ngr
            g_ref[t, S5_SUBL:2 * S5_SUBL, :] = ngi
            tp = jnp.maximum(t - 1, 0)
            at0 = (t == 0).astype(F32)
            keep = 1.0 - at0
            pr = keep * x_ref[tp, 0:S5_SUBL, :] + at0 * (1.0 - first) * xp_ref[0, 0:S5_SUBL, :]
            pi = keep * x_ref[tp, S5_SUBL:2 * S5_SUBL, :] + at0 * (1.0 - first) * xp_ref[0, S5_SUBL:2 * S5_SUBL, :]
            return ngr, ngi, dar + ngr * pr + ngi * pi, dai + ngi * pr - ngr * pi

        zero = jnp.zeros((S5_SUBL, S5_LANES), F32)
        gr, gi, dar, dai = lax.fori_loop(0, S5_TB, step, (st[0], st[1], zero, zero))
        st[0] = gr
        st[1] = gi
        da_ref[0] += dar
        da_ref[1] += dai

    rev = lambda i: nb - 1 - i
    blk = pl.BlockSpec((S5_TB, 2 * S5_SUBL, S5_LANES), lambda i: (rev(i), 0, 0))
    prev = pl.BlockSpec((1, 2 * S5_SUBL, S5_LANES), lambda i: (jnp.maximum(rev(i) * S5_TB - 1, 0), 0, 0))
    cst = pl.BlockSpec((S5_SUBL, S5_LANES), lambda i: (0, 0))
    return pl.pallas_call(
        body, name="s5_scan_bwd", grid=(nb,), in_specs=[blk, blk, prev, cst, cst],
        out_specs=[blk, pl.BlockSpec((2, S5_SUBL, S5_LANES), lambda i: (0, 0, 0))],
        out_shape=[jax.ShapeDtypeStruct(gx.shape, F32), jax.ShapeDtypeStruct((2, S5_SUBL, S5_LANES), F32)],
        scratch_shapes=[pltpu.VMEM((2, S5_SUBL, S5_LANES), F32)],
        compiler_params=_params(("arbitrary",)),
    )(gx, x, x, a_re, a_im)


def _place():
    x, y, c = lax.axis_index("x"), lax.axis_index("y"), lax.axis_index("c")
    return x, y, c, [(1 - x, y), (x, 1 - y), (1 - x, 1 - y)]


def _all_gather(shard, name):
    def body(x_ref, out_ref, send_sems, recv_sems, local_sem):
        x, y, c, chips = _place()
        me, sibling = (x, y, c), (x, y, 1 - c)

        def rows(px, py, pc):
            return out_ref.at[4 * px + 2 * py + pc]

        def copy(k, block, to, src=None):
            return pltpu.make_async_remote_copy(
                src_ref=rows(*block) if src is None else src, dst_ref=rows(*block),
                send_sem=send_sems.at[k], recv_sem=recv_sems.at[k], device_id=to, device_id_type=MESH)

        mine = pltpu.make_async_copy(x_ref, rows(*me), local_sem)
        mine.start()
        first = [copy(0, me, sibling, src=x_ref)]
        first += [copy(1 + j, me, (*chip, c), src=x_ref) for j, chip in enumerate(chips)]
        for cp in first:
            cp.start()
        passed = [copy(4 + j, (*chip, c), sibling) for j, chip in enumerate(chips)]
        for j, chip in enumerate(chips):
            copy(1 + j, (*chip, c), me).wait_recv()
            passed[j].start()
        copy(0, sibling, me).wait_recv()
        for j, chip in enumerate(chips):
            copy(4 + j, (*chip, 1 - c), me).wait_recv()
        for cp in first + passed:
            cp.wait_send()
        mine.wait()

    return pl.pallas_call(
        body, name=name, out_shape=jax.ShapeDtypeStruct((N_DEV,) + shard.shape, shard.dtype),
        in_specs=[pl.BlockSpec(memory_space=pl.ANY)], out_specs=pl.BlockSpec(memory_space=pl.ANY),
        scratch_shapes=[pltpu.SemaphoreType.DMA((7,)), pltpu.SemaphoreType.DMA((7,)), pltpu.SemaphoreType.DMA],
    )(shard)


def _swap_with_sibling(parts):
    def body(p_ref, out_ref, send_sems, recv_sems):
        x, y, c, _ = _place()
        copies = [pltpu.make_async_remote_copy(
            src_ref=p_ref.at[2 * chip + (1 - c)], dst_ref=out_ref.at[chip],
            send_sem=send_sems.at[chip], recv_sem=recv_sems.at[chip],
            device_id=(x, y, 1 - c), device_id_type=MESH) for chip in range(4)]
        for cp in copies:
            cp.start()
        for cp in copies:
            cp.wait()

    return pl.pallas_call(
        body, name="rs_sibling", out_shape=jax.ShapeDtypeStruct((4,) + parts.shape[1:], parts.dtype),
        in_specs=[pl.BlockSpec(memory_space=pl.ANY)], out_specs=pl.BlockSpec(memory_space=pl.ANY),
        scratch_shapes=[pltpu.SemaphoreType.DMA((4,)), pltpu.SemaphoreType.DMA((4,))],
    )(parts)


def _exchange_chips(parts):
    def body(p_ref, out_ref, send_sems, recv_sems):
        x, y, c, chips = _place()
        copies = [pltpu.make_async_remote_copy(
            src_ref=p_ref.at[2 * px + py], dst_ref=out_ref.at[j],
            send_sem=send_sems.at[j], recv_sem=recv_sems.at[j],
            device_id=(px, py, c), device_id_type=MESH) for j, (px, py) in enumerate(chips)]
        for cp in copies:
            cp.start()
        for cp in copies:
            cp.wait()

    return pl.pallas_call(
        body, name="rs_chips", out_shape=jax.ShapeDtypeStruct((3,) + parts.shape[1:], parts.dtype),
        in_specs=[pl.BlockSpec(memory_space=pl.ANY)], out_specs=pl.BlockSpec(memory_space=pl.ANY),
        scratch_shapes=[pltpu.SemaphoreType.DMA((3,)), pltpu.SemaphoreType.DMA((3,))],
    )(parts)


def _pack_rows(n_elem, row_mult=PACK_ROW_MULT):
    rows = -(-n_elem // PACK_COLS)
    return -(-rows // row_mult) * row_mult


def _pack(flats, dtype, row_mult=PACK_ROW_MULT):
    flat = jnp.concatenate([f.reshape(-1).astype(dtype) for f in flats])
    rows = _pack_rows(flat.shape[0], row_mult)
    return jnp.pad(flat, (0, rows * PACK_COLS - flat.shape[0])).reshape(rows, PACK_COLS)


def _unpack(buf, shapes):
    lead = buf.shape[:-2]
    flat = buf.reshape(lead + (-1,))
    outs, o = [], 0
    for s in shapes:
        n = math.prod(s)
        outs.append(flat[..., o:o + n].reshape(lead + tuple(s)))
        o += n
    return outs


BIG_LAYOUT = (("w_in_ab", D_MODEL, PACK_COLS), ("s5_w_glu", S5_W // N_DEV, PACK_COLS),
              ("w_out_ab", OUT_AB // N_DEV, 2 * PACK_COLS), ("w_in_c", D_MODEL, PACK_COLS),
              ("w_out_c", GLA_W // N_DEV, 2 * PACK_COLS))


def _to_rows(a):
    if a.shape[-1] == PACK_COLS:
        return a
    assert a.shape[-1] == 2 * PACK_COLS
    return jnp.concatenate([a[..., :PACK_COLS], a[..., PACK_COLS:]], axis=-2)


def _from_rows(p, cols):
    if cols == PACK_COLS:
        return p
    r = p.shape[-2] // 2
    return jnp.concatenate([p[..., :r, :], p[..., r:, :]], axis=-1)


def _pack_big(pieces):
    return jnp.concatenate([_to_rows(pieces[name]) for name, _, _ in BIG_LAYOUT], axis=-2)


def _unpack_big(buf):
    out, o = {}, 0
    for name, rows, cols in BIG_LAYOUT:
        r = rows * cols // PACK_COLS
        out[name] = _from_rows(buf[..., o:o + r, :], cols)
        o += r
    return out


def _lane_select(a, off, sign, n_out, out_dtype, exact, name):
    rows, n_in = a.shape
    tr = _tile(rows, 256, 16)

    def body(off_ref, a_ref, o_ref):
        sel = _iota2((n_in, n_out), 0) + off_ref[0] * sign == _iota2((n_in, n_out), 1)
        if exact:
            r = jnp.dot(a_ref[...], sel.astype(F32), precision=lax.Precision.HIGHEST, preferred_element_type=F32)
        else:
            r = _dg(a_ref[...], sel.astype(BF16), 1, 0)
        o_ref[...] = r.astype(out_dtype)

    return pl.pallas_call(
        body, name=name, grid=(rows // tr,),
        in_specs=[pl.BlockSpec(memory_space=pltpu.SMEM), pl.BlockSpec((tr, n_in), lambda i: (i, 0))],
        out_specs=pl.BlockSpec((tr, n_out), lambda i: (i, 0)),
        out_shape=jax.ShapeDtypeStruct((rows, n_out), out_dtype),
        compiler_params=_params(("arbitrary",)),
    )(off, a)


def _adamw(w, g, m, v, name):
    rows, cols = w.shape
    tr = _tile(rows, 256, 8) if rows % 8 == 0 else rows

    def fn(i, w_, g_, m_, v_):
        m_new = ADAM_B1 * m_ + (1.0 - ADAM_B1) * g_
        v_new = ADAM_B2 * v_ + (1.0 - ADAM_B2) * (g_ * g_)
        m_hat = m_new / (1.0 - ADAM_B1 ** ADAM_STEP)
        v_hat = v_new / (1.0 - ADAM_B2 ** ADAM_STEP)
        delta = -ADAM_LR * (m_hat / (jnp.sqrt(v_hat) + ADAM_EPS) + ADAM_WD * w_)
        return (delta, m_new, v_new), ()

    outs, _ = _rows(fn, [_win(w), _win(g), _win(m), _win(v)], [], [(cols, F32)] * 3, [], name=name,
                    nrow=rows, tr=tr)
    return outs


def _as2d(a):
    if a.ndim == 1:
        return a.reshape(1, -1)
    if a.ndim == 2:
        return a
    a = a.reshape(a.shape[1:])
    return a if a.ndim == 2 else a.reshape(a.shape[0], -1)


def kernel(x, meta, norm_ab_w, w_in_ab, ret_norm_w, s5_lam_re, s5_lam_im, s5_log_dt, s5_b_re, s5_b_im, s5_c_re, s5_c_im, s5_d, s5_w_glu, w_out_ab, norm_c_w, w_in_c, gla_w_gate, gla_b_gate, gla_norm_w, w_out_c, final_norm_w, loss_target, m_meta, m_norm_ab_w, m_w_in_ab, m_ret_norm_w, m_s5_lam_re, m_s5_lam_im, m_s5_log_dt, m_s5_b_re, m_s5_b_im, m_s5_c_re, m_s5_c_im, m_s5_d, m_s5_w_glu, m_w_out_ab, m_norm_c_w, m_w_in_c, m_gla_w_gate, m_gla_b_gate, m_gla_norm_w, m_w_out_c, m_final_norm_w, v_meta, v_norm_ab_w, v_w_in_ab, v_ret_norm_w, v_s5_lam_re, v_s5_lam_im, v_s5_log_dt, v_s5_b_re, v_s5_b_im, v_s5_c_re, v_s5_c_im, v_s5_d, v_s5_w_glu, v_w_out_ab, v_norm_c_w, v_w_in_c, v_gla_w_gate, v_gla_b_gate, v_gla_norm_w, v_w_out_c, v_final_norm_w):
    weights = dict(meta=meta, norm_ab_w=norm_ab_w, w_in_ab=w_in_ab, ret_norm_w=ret_norm_w, s5_lam_re=s5_lam_re,
                   s5_lam_im=s5_lam_im, s5_log_dt=s5_log_dt, s5_b_re=s5_b_re, s5_b_im=s5_b_im, s5_c_re=s5_c_re,
                   s5_c_im=s5_c_im, s5_d=s5_d, s5_w_glu=s5_w_glu, w_out_ab=w_out_ab, norm_c_w=norm_c_w,
                   w_in_c=w_in_c, gla_w_gate=gla_w_gate, gla_b_gate=gla_b_gate, gla_norm_w=gla_norm_w,
                   w_out_c=w_out_c, final_norm_w=final_norm_w)
    mom_m = dict(meta=m_meta, norm_ab_w=m_norm_ab_w, w_in_ab=m_w_in_ab, ret_norm_w=m_ret_norm_w,
                 s5_lam_re=m_s5_lam_re, s5_lam_im=m_s5_lam_im, s5_log_dt=m_s5_log_dt, s5_b_re=m_s5_b_re,
                 s5_b_im=m_s5_b_im, s5_c_re=m_s5_c_re, s5_c_im=m_s5_c_im, s5_d=m_s5_d, s5_w_glu=m_s5_w_glu,
                 w_out_ab=m_w_out_ab, norm_c_w=m_norm_c_w, w_in_c=m_w_in_c, gla_w_gate=m_gla_w_gate,
                 gla_b_gate=m_gla_b_gate, gla_norm_w=m_gla_norm_w, w_out_c=m_w_out_c, final_norm_w=m_final_norm_w)
    mom_v = dict(meta=v_meta, norm_ab_w=v_norm_ab_w, w_in_ab=v_w_in_ab, ret_norm_w=v_ret_norm_w,
                 s5_lam_re=v_s5_lam_re, s5_lam_im=v_s5_lam_im, s5_log_dt=v_s5_log_dt, s5_b_re=v_s5_b_re,
                 s5_b_im=v_s5_b_im, s5_c_re=v_s5_c_re, s5_c_im=v_s5_c_im, s5_d=v_s5_d, s5_w_glu=v_s5_w_glu,
                 w_out_ab=v_w_out_ab, norm_c_w=v_norm_c_w, w_in_c=v_w_in_c, gla_w_gate=v_gla_w_gate,
                 gla_b_gate=v_gla_b_gate, gla_norm_w=v_gla_norm_w, w_out_c=v_w_out_c, final_norm_w=v_final_norm_w)
    order = list(weights)

    seq = x.shape[1]
    lp = CHUNK + seq
    nchunk = lp // CHUNK
    dev = 4 * lax.axis_index("x") + 2 * lax.axis_index("y") + lax.axis_index("c")
    core = lax.axis_index("c")
    chip = 2 * lax.axis_index("x") + lax.axis_index("y")

    win_off = jnp.reshape(2 * dev, (1,)).astype(jnp.int32)
    shard_c = jnp.pad(w_in_c[0].astype(BF16), ((0, 0), (0, 896 - SHARD_C)))
    big_shards = dict(w_in_ab=w_in_ab[0].astype(BF16), s5_w_glu=s5_w_glu[0].astype(BF16),
                      w_out_ab=w_out_ab[0].astype(BF16), w_out_c=w_out_c[0].astype(BF16),
                      w_in_c=_lane_select(shard_c, win_off, 1, WIN_COLS, BF16, False, "w_in_c_to_window"))
    small_names = ["meta", "norm_c_w", "gla_w_gate", "gla_b_gate", "gla_norm_w"]
    small_shards = [meta, norm_c_w[0], gla_w_gate[0], gla_b_gate[0], gla_norm_w[0]]
    small_shapes = [s.shape for s in small_shards]
    gb = _unpack_big(_all_gather(_pack_big(big_shards), "gather_big"))
    gs = _all_gather(_pack(small_shards, F32, 8), "gather_small")
    w_in_ab_f = gb["w_in_ab"].transpose(1, 0, 2).reshape(D_MODEL, IN_AB)
    w_glu_f = gb["s5_w_glu"].reshape(S5_W, S5_W)
    w_out_ab_f = gb["w_out_ab"].reshape(OUT_AB, D_MODEL)
    w_in_c_f = sum(jnp.pad(gb["w_in_c"][d], ((0, 0), (WIN_STEP * d, IN_C_PAD - WIN_STEP * d - WIN_COLS)))
                   for d in range(N_DEV))
    w_out_c_f = gb["w_out_c"].reshape(GLA_W, D_MODEL)
    s_meta, s_norm_c, s_wgate, s_bgate, s_gnorm = _unpack(gs, small_shapes)
    meta_f = s_meta.transpose(1, 0, 2).reshape(N_META, D_MODEL)
    norm_c_f = s_norm_c.reshape(1, D_MODEL)
    w_gate_f = jnp.pad(s_wgate.transpose(1, 0, 2).reshape(GLA_RANK, GLA_QK), ((0, GATE_PAD - GLA_RANK), (0, 0)))
    b_gate_f = s_bgate.reshape(1, GLA_QK)
    gla_norm_f = s_gnorm.reshape(GLA_H, 1, GLA_DV)

    pos = jnp.maximum(jnp.arange(lp, dtype=F32) - float(PAD), 0.0)
    inv_freq = jnp.power(ROPE_BASE, -jnp.arange(0, RET_DK, 2, dtype=F32) / RET_DK)
    ang = pos[:, None] * inv_freq[None, :]
    cos2 = jnp.concatenate([jnp.cos(ang), jnp.cos(ang)], axis=1)
    sin2 = jnp.concatenate([-jnp.sin(ang), jnp.sin(ang)], axis=1)
    log_g = jnp.log1p(-jnp.exp2(-5.0 - jnp.arange(RET_H, dtype=F32)))
    lg = jnp.broadcast_to(log_g[:, None, None], (RET_H, 1, 128))
    ret_norm_h = ret_norm_w.reshape(RET_H, 1, RET_DV)

    h0 = jnp.concatenate([jnp.zeros((PAD, D_MODEL), F32), meta_f, x[0]], axis=0)

    def rowmask(i):
        return (_iota2((CHUNK, 1), 0) + i * CHUNK) >= PAD

    (hn0,), _ = _rows(lambda i, h, w: ((_rms(h, w),), ()), [_win(h0)], [norm_ab_w], [(D_MODEL, BF16)], [],
                      name="norm_ab_fwd", nrow=lp)
    proj_ab = _mm(hn0, w_in_ab_f, "nn", name="in_ab_fwd")

    q_off, k_off, v_off, za_off = 0, RET_QK, 2 * RET_QK, 2 * RET_QK + RET_W
    u_off, zb_off = 2 * RET_QK + 2 * RET_W, 2 * RET_QK + 2 * RET_W + S5_W
    ret_xs = [(proj_ab, RET_DK, lambda h: q_off // RET_DK + h), (proj_ab, RET_DK, lambda h: k_off // RET_DK + h),
              (proj_ab, RET_DV, lambda h: v_off // RET_DV + h), (proj_ab, RET_DV, lambda h: za_off // RET_DV + h)]
    ret_cs = [(cos2, RET_DK, lambda h: 0), (sin2, RET_DK, lambda h: 0)]
    ret_kw = dict(heads=RET_H, nchunk=nchunk, s_shape=(RET_DK, RET_DV), out_w=RET_DV, pre=_ret_pre)
    o_a, ret_sprev = _scan_fwd(_ret_chunk, ret_xs, ret_cs, [ret_norm_h], [lg], name="ret_fwd", **ret_kw)

    expand = jnp.repeat(jnp.eye(S5_P, dtype=F32), S5_GH, axis=1)
    disc_args = (s5_lam_re[0], s5_lam_im[0], s5_log_dt[0].reshape(S5_G, 1),
                 s5_b_re[0].reshape(S5_G, S5_P * S5_GH), s5_b_im[0].reshape(S5_G, S5_P * S5_GH), expand)
    ab_re, ab_im, bb_re, bb_im = _s5_disc_fwd(disc_args)
    eye_g = jnp.eye(S5_G, dtype=F32)

    def blockdiag_in(bb):
        return jnp.einsum("gph,gk->ghkp", bb.reshape(S5_G, S5_P, S5_GH), eye_g).reshape(S5_W, S5_N)

    def blockdiag_out(cc):
        return jnp.einsum("ghp,gk->gpkh", cc, eye_g).reshape(S5_N, S5_W)

    wb_cat = jnp.concatenate([blockdiag_in(bb_re), blockdiag_in(bb_im)], axis=1).astype(BF16)
    wc_cat = jnp.concatenate([blockdiag_out(s5_c_re[0]), -blockdiag_out(s5_c_im[0])], axis=0).astype(BF16)
    a_re, a_im = ab_re.reshape(S5_SUBL, S5_LANES), ab_im.reshape(S5_SUBL, S5_LANES)
    bu = _mm(proj_ab, wb_cat, "nn", name="s5_bu", a_win=(u_off, S5_W))
    xs5 = _s5_scan_fwd(bu.reshape(lp, 2 * S5_SUBL, S5_LANES), a_re, a_im)
    xs5_2d = xs5.reshape(lp, 2 * S5_N)
    y_pre = _mm(xs5_2d, wc_cat, "nn", name="s5_cx")
    (y_s5, yg_bf), _ = _rows(
        lambda i, yp, u, d: ((yp + d * u, _gelu(yp + d * u)), ()),
        [_win(y_pre), _win(proj_ab, u_off, S5_W)], [s5_d], [(S5_W, F32), (S5_W, BF16)], [], name="s5_gelu_fwd", nrow=lp)
    t_glu = _mm(yg_bf, w_glu_f, "nn", name="s5_glu_fwd")

    def s5_gate(y, t, zb):
        return _gelu(y) * _sigmoid(t) * _silu(zb)

    (o_b,), _ = _rows(lambda i, y, t, zb: ((s5_gate(y, t, zb),), ()),
                      [_win(y_s5), _win(t_glu), _win(proj_ab, zb_off, S5_W)], [], [(S5_W, BF16)], [],
                      name="s5_gate_fwd", nrow=lp)
    o_ab = jnp.concatenate([o_a, o_b], axis=1)
    h1 = _mm(o_ab, w_out_ab_f, "nn", name="out_ab_fwd", add=h0)

    (hn1,), _ = _rows(lambda i, h, w: ((_rms(h, w),), ()), [_win(h1)], [norm_c_f], [(D_MODEL, BF16)], [],
                      name="norm_c_fwd", nrow=lp)
    proj_c = _mm(hn1, w_in_c_f, "nn", name="in_c_fwd")
    gl_off = 2 * GLA_QK + 2 * GLA_W
    pre_gate = _mm(proj_c, w_gate_f, "nn", name="gate_fwd", a_win=(gl_off, GATE_PAD), bias=b_gate_f)
    gla_xs = [(proj_c, GLA_DK, lambda h: h), (proj_c, GLA_DK, lambda h: GLA_QK // GLA_DK + h),
              (proj_c, GLA_DV, lambda h: 2 * GLA_QK // GLA_DV + h),
              (proj_c, GLA_DV, lambda h: (2 * GLA_QK + GLA_W) // GLA_DV + h),
              (pre_gate, GLA_DK, lambda h: h)]
    gla_kw = dict(heads=GLA_H, nchunk=nchunk, s_shape=(GLA_DV, GLA_DK), out_w=GLA_DV)
    o_c, gla_sprev = _scan_fwd(_gla_chunk, gla_xs, [], [gla_norm_f], [], name="gla_fwd", **gla_kw)
    h2 = _mm(o_c, w_out_c_f, "nn", name="out_c_fwd", add=h1)

    fnw = final_norm_w.reshape(1, D_MODEL)

    def final_fn(i, h, tgt, w):
        def loss_of(h_, w_):
            err = _rms(h_, w_) - tgt
            return 0.5 * jnp.sum(jnp.mean(err * err, axis=-1))

        real = (i > 0).astype(F32)
        loss_i, (dh, dw) = jax.value_and_grad(loss_of, argnums=(0, 1))(h, w)
        return (dh * real,), (jnp.full((1, 128), loss_i * real, F32), dw * real)

    (dh2,), (loss_acc, g_final) = _rows(final_fn, [_win(h2), _win(loss_target[0], roff=1)], [fnw],
                                        [(D_MODEL, F32)], [(1, 128), (1, D_MODEL)], name="final_loss", nrow=lp)

    dh2_bf = dh2.astype(BF16)
    do_c = _mm(dh2_bf, w_out_c_f, "nt", name="out_c_dx", out_dtype=BF16)
    gw_out_c = _mm(o_c, dh2_bf, "tn", name="out_c_dw")
    (dq_c, dk_c, dv_c, dz_c, dpre), (g_gla_norm,) = _scan_bwd(
        _gla_chunk, gla_xs, [], [gla_norm_f], [], do_c, gla_sprev, name="gla_bwd", **gla_kw)
    dglow = _mm(dpre, w_gate_f, "nt", name="gate_dx", out_dtype=BF16)
    g_wgate = _mm(proj_c, dpre, "tn", name="gate_dw", a_win=(gl_off, GATE_PAD))[:GLA_RANK]
    (), (g_bgate,) = _rows(lambda i, d: ((), (jnp.sum(d.astype(F32), axis=0, keepdims=True),)), [_win(dpre)], [], [],
                           [(1, GLA_QK)], name="gate_db", nrow=lp)
    dproj_c = jnp.concatenate([dq_c, dk_c, dv_c, dz_c, dglow], axis=1)
    dhn1 = _mm(dproj_c, w_in_c_f, "nt", name="in_c_dx")
    gw_in_c = _mm(hn1, dproj_c, "tn", name="in_c_dw")

    def norm_bwd(i, h, dhn, dres, w):
        _, vjp = jax.vjp(_rms, h, w)
        dh, dw = vjp(dhn)
        return (jnp.where(rowmask(i), dh + dres, 0.0),), (dw,)

    (dh1,), (g_norm_c,) = _rows(norm_bwd, [_win(h1), _win(dhn1), _win(dh2)], [norm_c_f], [(D_MODEL, F32)],
                                [(1, D_MODEL)], name="norm_c_bwd", nrow=lp)

    dh1_bf = dh1.astype(BF16)
    do_ab = _mm(dh1_bf, w_out_ab_f, "nt", name="out_ab_dx", out_dtype=BF16)
    gw_out_ab = _mm(o_ab, dh1_bf, "tn", name="out_ab_dw")

    def s5_gate_bwd(i, dob, y, t, zb):
        _, vjp = jax.vjp(s5_gate, y, t, zb)
        dy, dt, dzb = vjp(dob.astype(F32))
        return (dy, dt, dzb), ()

    (dy_a, dt_glu, dzb), _ = _rows(
        s5_gate_bwd, [_win(do_ab, RET_W, S5_W), _win(y_s5), _win(t_glu), _win(proj_ab, zb_off, S5_W)], [],
        [(S5_W, F32), (S5_W, BF16), (S5_W, BF16)], [], name="s5_gate_bwd", nrow=lp)
    dyg2 = _mm(dt_glu, w_glu_f, "nt", name="s5_glu_dx")
    gw_glu = _mm(yg_bf, dt_glu, "tn", name="s5_glu_dw")

    def s5_y_bwd(i, dya, dyg, y, u, d):
        _, vjp = jax.vjp(_gelu, y)
        (dy_g,) = vjp(dyg)
        dy = dya + dy_g
        return (dy, d * dy), (jnp.sum(dy * u, axis=0, keepdims=True),)

    (dy_s5, du1), (g_d,) = _rows(
        s5_y_bwd, [_win(dy_a), _win(dyg2), _win(y_s5), _win(proj_ab, u_off, S5_W)], [s5_d],
        [(S5_W, BF16), (S5_W, F32)], [(1, S5_W)], name="s5_y_bwd", nrow=lp)
    gx = _mm(dy_s5, wc_cat, "nt", name="s5_cx_dx")
    gwc = _mm(xs5_2d, dy_s5, "tn", name="s5_cx_dw")
    g_s5, da = _s5_scan_bwd(gx.reshape(lp, 2 * S5_SUBL, S5_LANES), xs5, a_re, a_im)
    g_s5_2d = g_s5.reshape(lp, 2 * S5_N)
    du = _mm(g_s5_2d, wb_cat, "nt", name="s5_bu_dx", add=du1, out_dtype=BF16)
    gwb = _mm(proj_ab, g_s5_2d, "tn", name="s5_bu_dw", a_win=(u_off, S5_W))
    gwc5 = gwc.reshape(2, S5_G, S5_P, S5_G, S5_GH)
    g_c_re = jnp.einsum("gpgh->ghp", gwc5[0])
    g_c_im = -jnp.einsum("gpgh->ghp", gwc5[1])
    gwb5 = gwb.reshape(S5_G, S5_GH, 2, S5_G, S5_P)
    d_bb_re = jnp.einsum("ghgp->gph", gwb5[:, :, 0]).reshape(S5_G, S5_P * S5_GH)
    d_bb_im = jnp.einsum("ghgp->gph", gwb5[:, :, 1]).reshape(S5_G, S5_P * S5_GH)
    g_lam_re, g_lam_im, g_log_dt, g_b_re, g_b_im = _s5_disc_bwd(
        disc_args, (da[0].reshape(S5_G, S5_P), da[1].reshape(S5_G, S5_P), d_bb_re, d_bb_im))

    (dq_a, dk_a, dv_a, dz_a), (g_ret_norm,) = _scan_bwd(
        _ret_chunk, ret_xs, ret_cs, [ret_norm_h], [lg], do_ab, ret_sprev, name="ret_bwd", post=_ret_post, **ret_kw)
    dproj_ab = jnp.concatenate([dq_a, dk_a, dv_a, dz_a, du, dzb], axis=1)
    dhn0 = _mm(dproj_ab, w_in_ab_f, "nt", name="in_ab_dx")
    gw_in_ab = _mm(hn0, dproj_ab, "tn", name="in_ab_dw")
    (dh0,), (g_norm_ab,) = _rows(norm_bwd, [_win(h0), _win(dhn0), _win(dh1)], [norm_ab_w], [(D_MODEL, F32)],
                                 [(1, D_MODEL)], name="norm_ab_bwd", nrow=lp)
    grad_x = dh0[CHUNK:][None]
    g_meta_part = dh0[PAD:CHUNK]

    dest_pieces = dict(
        w_in_ab=gw_in_ab.reshape(D_MODEL, N_DEV, IN_AB // N_DEV).transpose(1, 0, 2),
        s5_w_glu=gw_glu.reshape(N_DEV, S5_W // N_DEV, S5_W),
        w_out_ab=gw_out_ab.reshape(N_DEV, OUT_AB // N_DEV, D_MODEL),
        w_in_c=jnp.stack([gw_in_c[:, WIN_STEP * d:WIN_STEP * d + WIN_COLS] for d in range(N_DEV)]),
        w_out_c=gw_out_c.reshape(N_DEV, GLA_W // N_DEV, D_MODEL))
    g_full = _pack_big(dest_pieces)
    prow = g_full.shape[1]
    sum_tr = _tile(prow, 512, 16)
    from_sibling = _swap_with_sibling(g_full.astype(BF16))
    mine_by_chip = lax.dynamic_index_in_dim(g_full.reshape(4, 2, prow, PACK_COLS), core, axis=1, keepdims=False)
    (p1, p1_bf), _ = _rows(
        lambda i, a, b: ((a + b.astype(F32), a + b.astype(F32)), ()),
        [_win(mine_by_chip.reshape(4 * prow, PACK_COLS)), _win(from_sibling.reshape(4 * prow, PACK_COLS))], [],
        [(PACK_COLS, F32), (PACK_COLS, BF16)], [], name="rs_sum_sibling", nrow=4 * prow, tr=sum_tr)
    from_chips = _exchange_chips(p1_bf.reshape(4, prow, PACK_COLS))
    own = lax.dynamic_index_in_dim(p1.reshape(4, prow, PACK_COLS), chip, axis=0, keepdims=False)
    nblk = prow // sum_tr
    fc2 = from_chips.reshape(3 * prow, PACK_COLS)
    (g_shard,), _ = _rows(
        lambda i, a, b0, b1, b2: ((((a + b0.astype(F32)) + b1.astype(F32)) + b2.astype(F32),), ()),
        [_win(own), _win(fc2), _win(fc2, roff=-nblk), _win(fc2, roff=-2 * nblk)], [], [(PACK_COLS, F32)], [],
        name="rs_sum_chips", nrow=prow, tr=sum_tr)
    big_grads = _unpack_big(g_shard)
    big_grads["w_in_c"] = _lane_select(big_grads["w_in_c"], win_off, -1, 896, F32, True,
                                       "w_in_c_from_window")[:, :SHARD_C]

    small_parts = dict(
        norm_ab_w=g_norm_ab, ret_norm_w=g_ret_norm, s5_lam_re=g_lam_re, s5_lam_im=g_lam_im, s5_log_dt=g_log_dt,
        s5_b_re=g_b_re, s5_b_im=g_b_im, s5_c_re=g_c_re, s5_c_im=g_c_im, s5_d=g_d, final_norm_w=g_final,
        meta=g_meta_part, norm_c_w=g_norm_c, gla_w_gate=g_wgate, gla_b_gate=g_bgate, gla_norm_w=g_gla_norm,
        loss=loss_acc[:, :1])
    sp_names = list(small_parts)
    sp_shapes = [small_parts[k].shape for k in sp_names]
    small_tr = 64
    sp_pack = _pack([small_parts[k] for k in sp_names], F32, small_tr)
    srow = sp_pack.shape[0]
    sp_all = _all_gather(sp_pack, "gather_grads").reshape(N_DEV * srow, PACK_COLS)
    snb = srow // small_tr

    def sum8(i, *blocks):
        acc = blocks[0]
        for b in blocks[1:]:
            acc = acc + b
        return (acc,), ()

    (sp_sum,), _ = _rows(sum8, [_win(sp_all, roff=-d * snb) for d in range(N_DEV)], [], [(PACK_COLS, F32)], [],
                         name="sum_small", nrow=srow, tr=small_tr)
    small = dict(zip(sp_names, _unpack(sp_sum, sp_shapes)))
    loss = small["loss"].reshape(())

    def my_cols(g, n):
        return lax.dynamic_slice_in_dim(g, dev * n, n, axis=g.ndim - 1)

    grads = dict(
        meta=my_cols(small["meta"], D_MODEL // N_DEV),
        norm_ab_w=small["norm_ab_w"], w_in_ab=big_grads["w_in_ab"][None], ret_norm_w=small["ret_norm_w"].reshape(1, RET_W),
        s5_lam_re=small["s5_lam_re"][None], s5_lam_im=small["s5_lam_im"][None],
        s5_log_dt=small["s5_log_dt"].reshape(1, S5_G),
        s5_b_re=small["s5_b_re"].reshape(1, S5_G, S5_P, S5_GH), s5_b_im=small["s5_b_im"].reshape(1, S5_G, S5_P, S5_GH),
        s5_c_re=small["s5_c_re"][None], s5_c_im=small["s5_c_im"][None], s5_d=small["s5_d"],
        s5_w_glu=big_grads["s5_w_glu"][None], w_out_ab=big_grads["w_out_ab"][None],
        norm_c_w=my_cols(small["norm_c_w"], D_MODEL // N_DEV), w_in_c=big_grads["w_in_c"][None],
        gla_w_gate=my_cols(small["gla_w_gate"], GLA_QK // N_DEV)[None],
        gla_b_gate=my_cols(small["gla_b_gate"], GLA_QK // N_DEV),
        gla_norm_w=my_cols(small["gla_norm_w"].reshape(1, GLA_W), GLA_W // N_DEV),
        w_out_c=big_grads["w_out_c"][None], final_norm_w=small["final_norm_w"].reshape(D_MODEL))

    deltas, new_m, new_v = {}, {}, {}
    for k in order:
        w = weights[k]
        d2, m2, v2 = _adamw(_as2d(w), _as2d(grads[k].reshape(w.shape)), _as2d(mom_m[k]), _as2d(mom_v[k]), "adamw_" + k)
        deltas[k], new_m[k], new_v[k] = d2.reshape(w.shape), m2.reshape(w.shape), v2.reshape(w.shape)
        grads[k] = grads[k].reshape(w.shape)

    return (loss, grad_x, *[grads[k] for k in order], *[deltas[k] for k in order],
            *[new_m[k] for k in order], *[new_v[k] for k in order])
```

```python
import functools
import math

import jax
import jax.numpy as jnp
from jax import lax
from jax.experimental import pallas as pl
from jax.experimental.pallas import tpu as pltpu

F32, BF16 = jnp.float32, jnp.bfloat16
MESH = pl.DeviceIdType.MESH
N_DEV = 8

D_MODEL = 2048
CHUNK = 128
N_META = 16
PAD = CHUNK - N_META
SUB = 16
EPS = 1e-6
RET_H, RET_DK, RET_DV = 8, 128, 256
RET_QK, RET_W = RET_H * RET_DK, RET_H * RET_DV
ROPE_BASE = 10000.0
S5_W, S5_G, S5_P, S5_GH = 1024, 64, 64, 16
S5_N = S5_G * S5_P
GLA_H, GLA_DK, GLA_DV, GLA_RANK, GLA_TAU = 4, 256, 512, 16, 16.0
GLA_QK, GLA_W = GLA_H * GLA_DK, GLA_H * GLA_DV
IN_AB = 2 * RET_QK + 2 * RET_W + 2 * S5_W
OUT_AB = RET_W + S5_W
IN_C = 2 * GLA_QK + 2 * GLA_W + GLA_RANK
GATE_PAD = 256
IN_C_PAD = 2 * GLA_QK + 2 * GLA_W + GATE_PAD
ADAM_LR, ADAM_B1, ADAM_B2, ADAM_EPS, ADAM_WD, ADAM_STEP = 0.001, 0.9, 0.999, 1e-08, 0.01, 10

VMEM_LIMIT_BYTES = 48 * 2 ** 20
PACK_COLS = 1024
PACK_ROW_MULT = 8
SHARD_C = IN_C // N_DEV
WIN_STEP = 768
WIN_COLS = 1024


def _params(sem):
    return pltpu.CompilerParams(dimension_semantics=sem, vmem_limit_bytes=VMEM_LIMIT_BYTES)


def _tile(n, cap, mult):
    best = None
    for t in range(mult, min(n, cap) + 1, mult):
        if n % t == 0:
            best = t
    assert best is not None, (n, cap, mult)
    return best


def _dg(a, b, ca, cb):
    return lax.dot_general(a.astype(BF16), b.astype(BF16), (((ca,), (cb,)), ((), ())),
                           preferred_element_type=F32)


@functools.partial(jax.custom_vjp, nondiff_argnums=(2, 3))
def _bdot(a, b, ca, cb):
    return _dg(a, b, ca, cb)


def _bdot_fwd(a, b, ca, cb):
    return _dg(a, b, ca, cb), (a, b)


def _bdot_bwd(ca, cb, res, g):
    a, b = res
    da = _dg(g, b, 1, 1 - cb) if ca == 1 else _dg(b, g, 1 - cb, 1)
    db = _dg(a, g, 1 - ca, 0) if cb == 0 else _dg(g, a, 0, 1 - ca)
    return da.astype(a.dtype), db.astype(b.dtype)


_bdot.defvjp(_bdot_fwd, _bdot_bwd)


def _sigmoid(x):
    return 1.0 / (1.0 + jnp.exp(-x))


def _silu(x):
    return x * _sigmoid(x)


def _log_sigmoid(x):
    return jnp.minimum(x, 0.0) - jnp.log(1.0 + jnp.exp(-jnp.abs(x)))


def _gelu(x):
    return 0.5 * x * (1.0 + jnp.tanh(math.sqrt(2.0 / math.pi) * (x + 0.044715 * (x * x * x))))


def _rms(x, w):
    return x * lax.rsqrt(jnp.mean(x * x, axis=-1, keepdims=True) + EPS) * w


class _Hook:
    def __init__(self, ins, outs, sems, phases):
        self.ins, self.outs, self.sems, self.phases = list(ins), list(outs), list(sems), list(phases)


_NO_HOOK = _Hook([], [], [], [])
_ANY = pl.BlockSpec(memory_space=pl.ANY)


def _run_hook(hook, lin, total, in_refs, out_refs, sem_refs):
    for frac, fn in hook.phases:
        at = min(int(frac * total), total - 1)

        @pl.when(lin == at)
        def _(fn=fn):
            fn(in_refs, out_refs, sem_refs)


def _mm_core(a, b, *, dims, grid, a_spec, b_spec, o_spec, out_shape, acc_shape, name, extra=(), hook=None):
    nk = grid[2]
    n_extra = len(extra)
    hook = _NO_HOOK if hook is None else hook
    hi, ho = len(hook.ins), len(hook.outs)

    def body(*refs):
        a_ref, b_ref = refs[0], refs[1]
        o_ref, acc = refs[2 + n_extra + hi], refs[3 + n_extra + hi + ho]
        k = pl.program_id(2)
        lin = (pl.program_id(0) * grid[1] + pl.program_id(1)) * nk + k
        _run_hook(hook, lin, grid[0] * grid[1] * nk, refs[2 + n_extra:2 + n_extra + hi],
                  refs[3 + n_extra + hi:3 + n_extra + hi + ho], refs[4 + n_extra + hi + ho:])

        @pl.when(k == 0)
        def _():
            acc[...] = jnp.zeros_like(acc)

        acc[...] += lax.dot_general(a_ref[...].astype(BF16), b_ref[...].astype(BF16), dims,
                                    preferred_element_type=F32)

        @pl.when(k == nk - 1)
        def _():
            r = acc[...]
            for e in range(n_extra):
                r = r + refs[2 + e][...].astype(F32)
            o_ref[...] = r.astype(o_ref.dtype)

    res = pl.pallas_call(
        body, name=name, grid=grid,
        in_specs=[a_spec, b_spec] + [sp for _, sp in extra] + [_ANY] * hi,
        out_specs=[o_spec] + [_ANY] * ho, out_shape=[out_shape] + hook.outs,
        scratch_shapes=[pltpu.VMEM(acc_shape, F32)] + hook.sems,
        compiler_params=_params(("arbitrary", "arbitrary", "arbitrary")),
    )(a, b, *[arr for arr, _ in extra], *hook.ins)
    return res[0] if hook is _NO_HOOK else (res[0], res[1:])


NN, NT, TN = (((1,), (0,)), ((), ())), (((1,), (1,)), ((), ())), (((0,), (0,)), ((), ()))


def _mm(a, b, mode, *, name, out_dtype=F32, a_win=None, add=None, bias=None, hook=None):
    if mode == "tn":
        kdim, n = a.shape[0], b.shape[1]
        m = a.shape[1] if a_win is None else a_win[1]
        tm, tn, tk = _tile(m, 512, 128), _tile(n, 1024, 128), _tile(kdim, 1408, 8)
        off = 0 if a_win is None else a_win[0] // tm
        a_spec = pl.BlockSpec((tk, tm), lambda i, j, k: (k, i + off))
        b_spec = pl.BlockSpec((tk, tn), lambda i, j, k: (k, j))
        dims = (((0,), (0,)), ((), ()))
    else:
        m = a.shape[0]
        kdim = a.shape[1] if a_win is None else a_win[1]
        n = b.shape[1] if mode == "nn" else b.shape[0]
        tm, tn, tk = _tile(m, 1408, 8), _tile(n, 640, 128), _tile(kdim, 1024, 128)
        off = 0 if a_win is None else a_win[0] // tk
        a_spec = pl.BlockSpec((tm, tk), lambda i, j, k: (i, k + off))
        if mode == "nn":
            b_spec = pl.BlockSpec((tk, tn), lambda i, j, k: (k, j))
            dims = (((1,), (0,)), ((), ()))
        else:
            b_spec = pl.BlockSpec((tn, tk), lambda i, j, k: (j, k))
            dims = (((1,), (1,)), ((), ()))
    if a_win is not None:
        assert a_win[0] % (tm if mode == "tn" else tk) == 0
    extra = []
    if add is not None:
        extra.append((add, pl.BlockSpec((tm, tn), lambda i, j, k: (i, j))))
    if bias is not None:
        extra.append((bias, pl.BlockSpec((1, tn), lambda i, j, k: (0, j))))
    return _mm_core(a, b, dims=dims, grid=(m // tm, n // tn, kdim // tk), a_spec=a_spec, b_spec=b_spec,
                    o_spec=pl.BlockSpec((tm, tn), lambda i, j, k: (i, j)),
                    out_shape=jax.ShapeDtypeStruct((m, n), out_dtype), acc_shape=(tm, tn), name=name, extra=extra,
                    hook=hook)


def _win(arr, col0=0, width=None, roff=0):
    return (arr, col0, arr.shape[1] if width is None else width, roff)


def _rows(fn, rows, consts, outs, accs, *, name, nrow, tr=CHUNK):
    nr, nc, no = len(rows), len(consts), len(outs)

    def body(*refs):
        i = pl.program_id(0)
        ins = [r[...] for r in refs[:nr + nc]]
        o_refs = refs[nr + nc:nr + nc + no]
        a_refs = refs[nr + nc + no:]
        res_o, res_a = fn(i, *ins)
        for r, v in zip(o_refs, res_o):
            r[...] = v.astype(r.dtype)
        if a_refs:
            @pl.when(i == 0)
            def _():
                for r in a_refs:
                    r[...] = jnp.zeros_like(r)

            for r, v in zip(a_refs, res_a):
                r[...] += v

    in_specs = []
    for (arr, col0, width, roff) in rows:
        assert col0 % width == 0 and arr.shape[0] % tr == 0
        in_specs.append(pl.BlockSpec((tr, width), lambda i, c=col0 // width, ro=roff: (jnp.maximum(i - ro, 0), c)))
    for c in consts:
        in_specs.append(pl.BlockSpec(c.shape, lambda i, nd=c.ndim: (0,) * nd))
    out_specs = [pl.BlockSpec((tr, w), lambda i: (i, 0)) for (w, _) in outs]
    out_specs += [pl.BlockSpec(s, lambda i, nd=len(s): (0,) * nd) for s in accs]
    out_shape = [jax.ShapeDtypeStruct((nrow, w), dt) for (w, dt) in outs]
    out_shape += [jax.ShapeDtypeStruct(s, F32) for s in accs]
    res = pl.pallas_call(
        body, name=name, grid=(nrow // tr,), in_specs=in_specs, out_specs=out_specs, out_shape=out_shape,
        compiler_params=_params(("arbitrary",)),
    )(*[r[0] for r in rows], *consts)
    return res[:no], res[no:]


def _scan_specs(xs, cs, ws, ks, chunk_of):
    specs = []
    for (arr, width, colfn) in list(xs) + list(cs):
        specs.append(pl.BlockSpec((CHUNK, width), lambda h, n, f=colfn: (chunk_of(n), f(h))))
    for arr in list(ws) + list(ks):
        specs.append(pl.BlockSpec((1, 1, arr.shape[2]), lambda h, n: (h, 0, 0)))
    return specs


def _scan_fwd(fn, xs, cs, ws, ks, *, heads, nchunk, s_shape, out_w, name, pre=None):
    nx, ncs, nw = len(xs), len(cs), len(ws)

    def body(*refs):
        n = pl.program_id(1)
        nin = nx + ncs + nw + len(ks)
        y_ref, sp_ref, s_scr = refs[nin:]

        @pl.when(n == 0)
        def _():
            s_scr[...] = jnp.zeros_like(s_scr)

        state = s_scr[...]
        sp_ref[0, 0] = state
        xv = [r[...] for r in refs[:nx]]
        cv = [r[...] for r in refs[nx:nx + ncs]]
        wv = [r[0] for r in refs[nx + ncs:nx + ncs + nw]]
        kv = [r[0] for r in refs[nx + ncs + nw:nin]]
        if pre is not None:
            xv = pre(xv, cv)
        y, s_new = fn(n, xv, state, cv, wv, kv)
        y_ref[...] = y.astype(y_ref.dtype)
        s_scr[...] = s_new

    lp = nchunk * CHUNK
    return pl.pallas_call(
        body, name=name, grid=(heads, nchunk),
        in_specs=_scan_specs(xs, cs, ws, ks, lambda n: n),
        out_specs=[pl.BlockSpec((CHUNK, out_w), lambda h, n: (n, h)),
                   pl.BlockSpec((1, 1) + s_shape, lambda h, n: (h, n, 0, 0))],
        out_shape=[jax.ShapeDtypeStruct((lp, heads * out_w), BF16),
                   jax.ShapeDtypeStruct((heads, nchunk) + s_shape, F32)],
        scratch_shapes=[pltpu.VMEM(s_shape, F32)],
        compiler_params=_params(("arbitrary", "arbitrary")),
    )(*[t[0] for t in xs], *[t[0] for t in cs], *ws, *ks)


def _scan_bwd(fn, xs, cs, ws, ks, dy, sprev, *, heads, nchunk, s_shape, out_w, name, pre=None, post=None,
              hook=None):
    nx, ncs, nw = len(xs), len(cs), len(ws)
    nin = nx + ncs + nw + len(ks)
    hook = _NO_HOOK if hook is None else hook
    hi, ho = len(hook.ins), len(hook.outs)

    def body(*refs):
        step = pl.program_id(1)
        n = nchunk - 1 - step
        dy_ref, sp_ref = refs[nin], refs[nin + 1]
        o0 = nin + 2 + hi
        dx_refs = refs[o0:o0 + nx]
        dw_refs = refs[o0 + nx:o0 + nx + nw]
        ds_scr = refs[o0 + nx + nw + ho]
        _run_hook(hook, pl.program_id(0) * nchunk + step, heads * nchunk, refs[nin + 2:o0],
                  refs[o0 + nx + nw:o0 + nx + nw + ho], refs[o0 + nx + nw + ho + 1:])

        @pl.when(step == 0)
        def _():
            ds_scr[...] = jnp.zeros_like(ds_scr)
            for r in dw_refs:
                r[...] = jnp.zeros_like(r)

        xv = [r[...] for r in refs[:nx]]
        cv = [r[...] for r in refs[nx:nx + ncs]]
        wv = [r[0] for r in refs[nx + ncs:nx + ncs + nw]]
        kv = [r[0] for r in refs[nx + ncs + nw:nin]]
        if pre is not None:
            xv = pre(xv, cv)
        _, vjp = jax.vjp(lambda xs_, s_, ws_: fn(n, xs_, s_, cv, ws_, kv), xv, sp_ref[0, 0], wv)
        dxs, ds_prev, dws = vjp((dy_ref[...].astype(F32), ds_scr[...]))
        if post is not None:
            dxs = post(dxs, cv)
        for r, v in zip(dx_refs, dxs):
            r[...] = v.astype(r.dtype)
        for r, v in zip(dw_refs, dws):
            r[0] += v
        ds_scr[...] = ds_prev

    lp = nchunk * CHUNK
    rev = lambda n: nchunk - 1 - n
    in_specs = _scan_specs(xs, cs, ws, ks, rev)
    in_specs.append(pl.BlockSpec((CHUNK, out_w), lambda h, n: (rev(n), h)))
    in_specs.append(pl.BlockSpec((1, 1) + s_shape, lambda h, n: (h, rev(n), 0, 0)))
    out_specs = [pl.BlockSpec((CHUNK, w), lambda h, n: (rev(n), h)) for (_, w, _) in xs]
    out_specs += [pl.BlockSpec((1, 1, w.shape[2]), lambda h, n: (h, 0, 0)) for w in ws]
    out_shape = [jax.ShapeDtypeStruct((lp, heads * w), BF16) for (_, w, _) in xs]
    out_shape += [jax.ShapeDtypeStruct(w.shape, F32) for w in ws]
    res = pl.pallas_call(
        body, name=name, grid=(heads, nchunk), in_specs=in_specs + [_ANY] * hi,
        out_specs=out_specs + [_ANY] * ho, out_shape=out_shape + hook.outs,
        scratch_shapes=[pltpu.VMEM(s_shape, F32)] + hook.sems,
        compiler_params=_params(("arbitrary", "arbitrary")),
    )(*[t[0] for t in xs], *[t[0] for t in cs], *ws, *ks, dy, sprev, *hook.ins)
    if hook is _NO_HOOK:
        return res[:nx], res[nx:]
    return res[:nx], res[nx:nx + nw], res[nx + nw:]


def _iota2(shape, dim):
    return lax.broadcasted_iota(jnp.int32, shape, dim)


def _ret_chunk(n, xs, state, cs, ws, ks):
    q, k, v, z = xs
    (w,), (lg,) = ws, ks
    lgc = lg[:, :1]
    row, col = _iota2((CHUNK, CHUNK), 0), _iota2((CHUNK, CHUNK), 1)
    diff = jnp.maximum(row - col, 0).astype(F32)
    decay = jnp.where(row >= col, jnp.exp(lg * diff), 0.0)
    scores = _bdot(q, k, 1, 1) * decay
    o_intra = _bdot(scores, v, 1, 0)
    idx = _iota2((CHUNK, 1), 0).astype(F32)
    k_w = k * jnp.exp(lgc * (CHUNK - 1.0 - idx))
    kv = _bdot(k_w, v, 0, 0)
    s_new = state * jnp.exp(lgc * float(CHUNK)) + kv
    q_w = q * jnp.exp(lgc * (idx + 1.0))
    o = o_intra + _bdot(q_w, state, 1, 0)
    return _rms(o, w) * _silu(z), s_new


def _rope(t, cos2, sin2):
    return t * cos2 + pltpu.roll(t, RET_DK // 2, 1) * sin2


def _rope_t(g, cos2, sin2):
    return g * cos2 - pltpu.roll(g, RET_DK // 2, 1) * sin2


def _ret_pre(xv, cv):
    q, k, v, z = xv
    cos2, sin2 = cv
    return [_rope(q, cos2, sin2), _rope(k, cos2, sin2) * (RET_DK ** -0.5), v, z]


def _ret_post(dxs, cv):
    dq, dk, dv, dz = dxs
    cos2, sin2 = cv
    return [_rope_t(dq, cos2, sin2), _rope_t(dk, cos2, sin2) * (RET_DK ** -0.5), dv, dz]


def _gla_chunk(n, xs, state_t, cs, ws, ks):
    q, k, v, z, pre = xs
    (w,) = ws
    q = q * (GLA_DK ** -0.5)
    rowc = _iota2((CHUNK, 1), 0)
    valid = jnp.logical_or(n > 0, rowc >= PAD)
    log_a = jnp.where(valid, _log_sigmoid(pre) / GLA_TAU, 0.0)
    row, col = _iota2((CHUNK, CHUNK), 0), _iota2((CHUNK, CHUNK), 1)
    tri = (row >= col).astype(F32)
    b = jnp.dot(tri, log_a, precision=lax.Precision.HIGHEST, preferred_element_type=F32)
    b_last = b[CHUNK - 1:CHUNK, :]
    kv_t = _bdot(v, k * jnp.exp(b_last - b), 0, 0)
    s_new = state_t * jnp.exp(b_last) + kv_t
    o_inter = _bdot(q * jnp.exp(b), state_t, 1, 1)
    outs = []
    for s in range(CHUNK // SUB):
        lo, hi = s * SUB, (s + 1) * SUB
        b_ref = jnp.zeros_like(b_last) if s == 0 else b[lo - 1:lo, :]
        q_hat = q[lo:hi] * jnp.exp(b[lo:hi] - b_ref)
        k_hat = k[:hi] * jnp.exp(b_ref - b[:hi])
        sc = _bdot(q_hat, k_hat, 1, 1)
        causal = _iota2((SUB, hi), 0) + lo >= _iota2((SUB, hi), 1)
        outs.append(_bdot(jnp.where(causal, sc, 0.0), v[:hi], 1, 0))
    o = jnp.concatenate(outs, axis=0) + o_inter
    return _rms(o, w) * _silu(z), s_new


def _s5_disc(lam_re, lam_im, log_dt, b_re, b_im, expand):
    dt = jnp.exp(log_dt)
    mag = jnp.exp(lam_re * dt)
    ab_re, ab_im = mag * jnp.cos(lam_im * dt), mag * jnp.sin(lam_im * dt)
    den = lam_re * lam_re + lam_im * lam_im
    nr, ni = ab_re - 1.0, ab_im
    f_re = (nr * lam_re + ni * lam_im) / den
    f_im = (ni * lam_re - nr * lam_im) / den
    hp = lax.Precision.HIGHEST
    f_re = jnp.dot(f_re, expand, precision=hp, preferred_element_type=F32)
    f_im = jnp.dot(f_im, expand, precision=hp, preferred_element_type=F32)
    return ab_re, ab_im, f_re * b_re - f_im * b_im, f_re * b_im + f_im * b_re


def _s5_disc_fwd(args):
    def body(*refs):
        outs = _s5_disc(*[r[...] for r in refs[:6]])
        for r, v in zip(refs[6:], outs):
            r[...] = v

    g, p = args[0].shape
    return pl.pallas_call(
        body, name="s5_disc_fwd",
        out_shape=[jax.ShapeDtypeStruct((g, p), F32)] * 2 + [jax.ShapeDtypeStruct(args[3].shape, F32)] * 2,
    )(*args)


def _s5_disc_bwd(args, cts):
    def body(*refs):
        prim = [r[...] for r in refs[:5]]
        expand = refs[5][...]
        ct = tuple(r[...] for r in refs[6:10])
        _, vjp = jax.vjp(lambda *a: _s5_disc(*a, expand), *prim)
        for r, v in zip(refs[10:], vjp(ct)):
            r[...] = v

    return pl.pallas_call(
        body, name="s5_disc_bwd", out_shape=[jax.ShapeDtypeStruct(a.shape, F32) for a in args[:5]],
    )(*args, *cts)


S5_SUBL = 8
S5_LANES = S5_N // S5_SUBL
S5_TB = 64


def _s5_scan_fwd(bu, a_re, a_im):
    lp = bu.shape[0]

    def body(bu_ref, ar_ref, ai_ref, x_ref, st):
        @pl.when(pl.program_id(0) == 0)
        def _():
            st[...] = jnp.zeros_like(st)

        ar, ai = ar_ref[...], ai_ref[...]

        def step(t, carry):
            xr, xi = carry
            nr = ar * xr - ai * xi + bu_ref[t, 0:S5_SUBL, :]
            ni = ar * xi + ai * xr + bu_ref[t, S5_SUBL:2 * S5_SUBL, :]
            x_ref[t, 0:S5_SUBL, :] = nr
            x_ref[t, S5_SUBL:2 * S5_SUBL, :] = ni
            return nr, ni

        xr, xi = lax.fori_loop(0, S5_TB, step, (st[0], st[1]))
        st[0] = xr
        st[1] = xi

    blk = pl.BlockSpec((S5_TB, 2 * S5_SUBL, S5_LANES), lambda i: (i, 0, 0))
    cst = pl.BlockSpec((S5_SUBL, S5_LANES), lambda i: (0, 0))
    return pl.pallas_call(
        body, name="s5_scan_fwd", grid=(lp // S5_TB,), in_specs=[blk, cst, cst], out_specs=blk,
        out_shape=jax.ShapeDtypeStruct(bu.shape, F32),
        scratch_shapes=[pltpu.VMEM((2, S5_SUBL, S5_LANES), F32)],
        compiler_params=_params(("arbitrary",)),
    )(bu, a_re, a_im)


def _s5_scan_bwd(gx, x, a_re, a_im):
    lp = gx.shape[0]
    nb = lp // S5_TB

    def body(gx_ref, x_ref, xp_ref, ar_ref, ai_ref, g_ref, da_ref, st):
        i = pl.program_id(0)

        @pl.when(i == 0)
        def _():
            st[...] = jnp.zeros_like(st)
            da_ref[...] = jnp.zeros_like(da_ref)

        ar, ai = ar_ref[...], ai_ref[...]
        first = (i == nb - 1).astype(F32)

        def step(s, carry):
            gr, gi, dar, dai = carry
            t = S5_TB - 1 - s
            ngr = gx_ref[t, 0:S5_SUBL, :] + ar * gr + ai * gi
            ngi = gx_ref[t, S5_SUBL:2 * S5_SUBL, :] + ar * gi - ai * gr
            g_ref[t, 0:S5_SUBL, :] = ngr
            g_ref[t, S5_SUBL:2 * S5_SUBL, :] = ngi
            tp = jnp.maximum(t - 1, 0)
            at0 = (t == 0).astype(F32)
            keep = 1.0 - at0
            pr = keep * x_ref[tp, 0:S5_SUBL, :] + at0 * (1.0 - first) * xp_ref[0, 0:S5_SUBL, :]
            pi = keep * x_ref[tp, S5_SUBL:2 * S5_SUBL, :] + at0 * (1.0 - first) * xp_ref[0, S5_SUBL:2 * S5_SUBL, :]
            return ngr, ngi, dar + ngr * pr + ngi * pi, dai + ngi * pr - ngr * pi

        zero = jnp.zeros((S5_SUBL, S5_LANES), F32)
        gr, gi, dar, dai = lax.fori_loop(0, S5_TB, step, (st[0], st[1], zero, zero))
        st[0] = gr
        st[1] = gi
        da_ref[0] += dar
        da_ref[1] += dai

    rev = lambda i: nb - 1 - i
    blk = pl.BlockSpec((S5_TB, 2 * S5_SUBL, S5_LANES), lambda i: (rev(i), 0, 0))
    prev = pl.BlockSpec((1, 2 * S5_SUBL, S5_LANES), lambda i: (jnp.maximum(rev(i) * S5_TB - 1, 0), 0, 0))
    cst = pl.BlockSpec((S5_SUBL, S5_LANES), lambda i: (0, 0))
    return pl.pallas_call(
        body, name="s5_scan_bwd", grid=(nb,), in_specs=[blk, blk, prev, cst, cst],
        out_specs=[blk, pl.BlockSpec((2, S5_SUBL, S5_LANES), lambda i: (0, 0, 0))],
        out_shape=[jax.ShapeDtypeStruct(gx.shape, F32), jax.ShapeDtypeStruct((2, S5_SUBL, S5_LANES), F32)],
        scratch_shapes=[pltpu.VMEM((2, S5_SUBL, S5_LANES), F32)],
        compiler_params=_params(("arbitrary",)),
    )(gx, x, x, a_re, a_im)


def _place():
    x, y, c = lax.axis_index("x"), lax.axis_index("y"), lax.axis_index("c")
    return x, y, c, [(1 - x, y), (x, 1 - y), (1 - x, 1 - y)]


def _gather_phases():
    def plan(x_ref, out_ref, send_sems, recv_sems, local_sem):
        x, y, c, chips = _place()
        me, sibling = (x, y, c), (x, y, 1 - c)

        def rows(px, py, pc):
            return out_ref.at[4 * px + 2 * py + pc]

        def copy(k, block, to, src=None):
            return pltpu.make_async_remote_copy(
                src_ref=rows(*block) if src is None else src, dst_ref=rows(*block),
                send_sem=send_sems.at[k], recv_sem=recv_sems.at[k], device_id=to, device_id_type=MESH)

        mine = pltpu.make_async_copy(x_ref, rows(*me), local_sem)
        first = [copy(0, me, sibling, src=x_ref)]
        first += [copy(1 + j, me, (*chip, c), src=x_ref) for j, chip in enumerate(chips)]
        passed = [copy(4 + j, (*chip, c), sibling) for j, chip in enumerate(chips)]
        return c, chips, me, sibling, copy, mine, first, passed

    def start(ins, outs, sems):
        _, _, _, _, _, mine, first, _ = plan(ins[0], outs[0], *sems)
        mine.start()
        for cp in first:
            cp.start()

    def middle(ins, outs, sems):
        c, chips, me, _, copy, _, _, passed = plan(ins[0], outs[0], *sems)
        for j, chip in enumerate(chips):
            copy(1 + j, (*chip, c), me).wait_recv()
            passed[j].start()

    def finish(ins, outs, sems):
        c, chips, me, sibling, copy, mine, first, passed = plan(ins[0], outs[0], *sems)
        copy(0, sibling, me).wait_recv()
        for j, chip in enumerate(chips):
            copy(4 + j, (*chip, 1 - c), me).wait_recv()
        for cp in first + passed:
            cp.wait_send()
        mine.wait()

    return start, middle, finish


_GATHER_SEMS = [pltpu.SemaphoreType.DMA((7,)), pltpu.SemaphoreType.DMA((7,)), pltpu.SemaphoreType.DMA]


def _all_gather(shard, name):
    phases = _gather_phases()

    def body(x_ref, out_ref, *sems):
        for phase in phases:
            phase([x_ref], [out_ref], sems)

    return pl.pallas_call(
        body, name=name, out_shape=jax.ShapeDtypeStruct((N_DEV,) + shard.shape, shard.dtype),
        in_specs=[_ANY], out_specs=_ANY, scratch_shapes=list(_GATHER_SEMS),
    )(shard)


def _gather_hook(shard):
    start, middle, finish = _gather_phases()
    return _Hook([shard], [jax.ShapeDtypeStruct((N_DEV,) + shard.shape, shard.dtype)], _GATHER_SEMS,
                 [(0.0, start), (0.6, middle), (1.0, finish)])


def _swap_with_sibling(parts, name):
    def body(p_ref, out_ref, send_sems, recv_sems):
        x, y, c, _ = _place()
        copies = [pltpu.make_async_remote_copy(
            src_ref=p_ref.at[2 * chip + (1 - c)], dst_ref=out_ref.at[chip],
            send_sem=send_sems.at[chip], recv_sem=recv_sems.at[chip],
            device_id=(x, y, 1 - c), device_id_type=MESH) for chip in range(4)]
        for cp in copies:
            cp.start()
        for cp in copies:
            cp.wait()

    return pl.pallas_call(
        body, name=name, out_shape=jax.ShapeDtypeStruct((4,) + parts.shape[1:], parts.dtype),
        in_specs=[pl.BlockSpec(memory_space=pl.ANY)], out_specs=pl.BlockSpec(memory_space=pl.ANY),
        scratch_shapes=[pltpu.SemaphoreType.DMA((4,)), pltpu.SemaphoreType.DMA((4,))],
    )(parts)


def _chips_phases():
    def copies(p_ref, out_ref, send_sems, recv_sems):
        x, y, c, chips = _place()
        return [pltpu.make_async_remote_copy(
            src_ref=p_ref.at[2 * px + py], dst_ref=out_ref.at[j],
            send_sem=send_sems.at[j], recv_sem=recv_sems.at[j],
            device_id=(px, py, c), device_id_type=MESH) for j, (px, py) in enumerate(chips)]

    def start(ins, outs, sems):
        for cp in copies(ins[0], outs[0], *sems):
            cp.start()

    def finish(ins, outs, sems):
        for cp in copies(ins[0], outs[0], *sems):
            cp.wait()

    return start, finish


def _chips_hook(parts):
    start, finish = _chips_phases()
    return _Hook([parts], [jax.ShapeDtypeStruct((3,) + parts.shape[1:], parts.dtype)],
                 [pltpu.SemaphoreType.DMA((3,)), pltpu.SemaphoreType.DMA((3,))], [(0.0, start), (1.0, finish)])


def _pack_rows(n_elem, row_mult=PACK_ROW_MULT):
    rows = -(-n_elem // PACK_COLS)
    return -(-rows // row_mult) * row_mult


def _pack(flats, dtype, row_mult=PACK_ROW_MULT):
    flat = jnp.concatenate([f.reshape(-1).astype(dtype) for f in flats])
    rows = _pack_rows(flat.shape[0], row_mult)
    return jnp.pad(flat, (0, rows * PACK_COLS - flat.shape[0])).reshape(rows, PACK_COLS)


def _unpack(buf, shapes):
    lead = buf.shape[:-2]
    flat = buf.reshape(lead + (-1,))
    outs, o = [], 0
    for s in shapes:
        n = math.prod(s)
        outs.append(flat[..., o:o + n].reshape(lead + tuple(s)))
        o += n
    return outs


BIG_LAYOUT = (("w_in_ab", D_MODEL, PACK_COLS), ("s5_w_glu", S5_W // N_DEV, PACK_COLS),
              ("w_out_ab", OUT_AB // N_DEV, 2 * PACK_COLS), ("w_in_c", D_MODEL, PACK_COLS),
              ("w_out_c", GLA_W // N_DEV, 2 * PACK_COLS))


def _to_rows(a):
    if a.shape[-1] == PACK_COLS:
        return a
    assert a.shape[-1] == 2 * PACK_COLS
    return jnp.concatenate([a[..., :PACK_COLS], a[..., PACK_COLS:]], axis=-2)


def _from_rows(p, cols):
    if cols == PACK_COLS:
        return p
    r = p.shape[-2] // 2
    return jnp.concatenate([p[..., :r, :], p[..., r:, :]], axis=-1)


FIRST_LAYOUT, REST_LAYOUT = BIG_LAYOUT[:1], BIG_LAYOUT[1:]


def _pack_big(pieces, layout):
    return jnp.concatenate([_to_rows(pieces[name]) for name, _, _ in layout], axis=-2)


def _unpack_big(buf, layout):
    out, o = {}, 0
    for name, rows, cols in layout:
        r = rows * cols // PACK_COLS
        out[name] = _from_rows(buf[..., o:o + r, :], cols)
        o += r
    return out


def _lane_select(a, off, sign, n_out, out_dtype, exact, name):
    rows, n_in = a.shape
    tr = _tile(rows, 256, 16)

    def body(off_ref, a_ref, o_ref):
        sel = _iota2((n_in, n_out), 0) + off_ref[0] * sign == _iota2((n_in, n_out), 1)
        if exact:
            r = jnp.dot(a_ref[...], sel.astype(F32), precision=lax.Precision.HIGHEST, preferred_element_type=F32)
        else:
            r = _dg(a_ref[...], sel.astype(BF16), 1, 0)
        o_ref[...] = r.astype(out_dtype)

    return pl.pallas_call(
        body, name=name, grid=(rows // tr,),
        in_specs=[pl.BlockSpec(memory_space=pltpu.SMEM), pl.BlockSpec((tr, n_in), lambda i: (i, 0))],
        out_specs=pl.BlockSpec((tr, n_out), lambda i: (i, 0)),
        out_shape=jax.ShapeDtypeStruct((rows, n_out), out_dtype),
        compiler_params=_params(("arbitrary",)),
    )(off, a)


def _adamw(w, g, m, v, name):
    rows, cols = w.shape
    tr = _tile(rows, 256, 8) if rows % 8 == 0 else rows

    def fn(i, w_, g_, m_, v_):
        m_new = ADAM_B1 * m_ + (1.0 - ADAM_B1) * g_
        v_new = ADAM_B2 * v_ + (1.0 - ADAM_B2) * (g_ * g_)
        m_hat = m_new / (1.0 - ADAM_B1 ** ADAM_STEP)
        v_hat = v_new / (1.0 - ADAM_B2 ** ADAM_STEP)
        delta = -ADAM_LR * (m_hat / (jnp.sqrt(v_hat) + ADAM_EPS) + ADAM_WD * w_)
        return (delta, m_new, v_new), ()

    outs, _ = _rows(fn, [_win(w), _win(g), _win(m), _win(v)], [], [(cols, F32)] * 3, [], name=name,
                    nrow=rows, tr=tr)
    return outs


def _as2d(a):
    if a.ndim == 1:
        return a.reshape(1, -1)
    if a.ndim == 2:
        return a
    a = a.reshape(a.shape[1:])
    return a if a.ndim == 2 else a.reshape(a.shape[0], -1)


def kernel(x, meta, norm_ab_w, w_in_ab, ret_norm_w, s5_lam_re, s5_lam_im, s5_log_dt, s5_b_re, s5_b_im, s5_c_re, s5_c_im, s5_d, s5_w_glu, w_out_ab, norm_c_w, w_in_c, gla_w_gate, gla_b_gate, gla_norm_w, w_out_c, final_norm_w, loss_target, m_meta, m_norm_ab_w, m_w_in_ab, m_ret_norm_w, m_s5_lam_re, m_s5_lam_im, m_s5_log_dt, m_s5_b_re, m_s5_b_im, m_s5_c_re, m_s5_c_im, m_s5_d, m_s5_w_glu, m_w_out_ab, m_norm_c_w, m_w_in_c, m_gla_w_gate, m_gla_b_gate, m_gla_norm_w, m_w_out_c, m_final_norm_w, v_meta, v_norm_ab_w, v_w_in_ab, v_ret_norm_w, v_s5_lam_re, v_s5_lam_im, v_s5_log_dt, v_s5_b_re, v_s5_b_im, v_s5_c_re, v_s5_c_im, v_s5_d, v_s5_w_glu, v_w_out_ab, v_norm_c_w, v_w_in_c, v_gla_w_gate, v_gla_b_gate, v_gla_norm_w, v_w_out_c, v_final_norm_w):
    weights = dict(meta=meta, norm_ab_w=norm_ab_w, w_in_ab=w_in_ab, ret_norm_w=ret_norm_w, s5_lam_re=s5_lam_re,
                   s5_lam_im=s5_lam_im, s5_log_dt=s5_log_dt, s5_b_re=s5_b_re, s5_b_im=s5_b_im, s5_c_re=s5_c_re,
                   s5_c_im=s5_c_im, s5_d=s5_d, s5_w_glu=s5_w_glu, w_out_ab=w_out_ab, norm_c_w=norm_c_w,
                   w_in_c=w_in_c, gla_w_gate=gla_w_gate, gla_b_gate=gla_b_gate, gla_norm_w=gla_norm_w,
                   w_out_c=w_out_c, final_norm_w=final_norm_w)
    mom_m = dict(meta=m_meta, norm_ab_w=m_norm_ab_w, w_in_ab=m_w_in_ab, ret_norm_w=m_ret_norm_w,
                 s5_lam_re=m_s5_lam_re, s5_lam_im=m_s5_lam_im, s5_log_dt=m_s5_log_dt, s5_b_re=m_s5_b_re,
                 s5_b_im=m_s5_b_im, s5_c_re=m_s5_c_re, s5_c_im=m_s5_c_im, s5_d=m_s5_d, s5_w_glu=m_s5_w_glu,
                 w_out_ab=m_w_out_ab, norm_c_w=m_norm_c_w, w_in_c=m_w_in_c, gla_w_gate=m_gla_w_gate,
                 gla_b_gate=m_gla_b_gate, gla_norm_w=m_gla_norm_w, w_out_c=m_w_out_c, final_norm_w=m_final_norm_w)
    mom_v = dict(meta=v_meta, norm_ab_w=v_norm_ab_w, w_in_ab=v_w_in_ab, ret_norm_w=v_ret_norm_w,
                 s5_lam_re=v_s5_lam_re, s5_lam_im=v_s5_lam_im, s5_log_dt=v_s5_log_dt, s5_b_re=v_s5_b_re,
                 s5_b_im=v_s5_b_im, s5_c_re=v_s5_c_re, s5_c_im=v_s5_c_im, s5_d=v_s5_d, s5_w_glu=v_s5_w_glu,
                 w_out_ab=v_w_out_ab, norm_c_w=v_norm_c_w, w_in_c=v_w_in_c, gla_w_gate=v_gla_w_gate,
                 gla_b_gate=v_gla_b_gate, gla_norm_w=v_gla_norm_w, w_out_c=v_w_out_c, final_norm_w=v_final_norm_w)
    order = list(weights)

    seq = x.shape[1]
    lp = CHUNK + seq
    nchunk = lp // CHUNK
    dev = 4 * lax.axis_index("x") + 2 * lax.axis_index("y") + lax.axis_index("c")
    core = lax.axis_index("c")
    chip = 2 * lax.axis_index("x") + lax.axis_index("y")

    win_off = jnp.reshape(2 * dev, (1,)).astype(jnp.int32)
    shard_c = jnp.pad(w_in_c[0].astype(BF16), ((0, 0), (0, 896 - SHARD_C)))
    big_shards = dict(w_in_ab=w_in_ab[0].astype(BF16), s5_w_glu=s5_w_glu[0].astype(BF16),
                      w_out_ab=w_out_ab[0].astype(BF16), w_out_c=w_out_c[0].astype(BF16),
                      w_in_c=_lane_select(shard_c, win_off, 1, WIN_COLS, BF16, False, "w_in_c_to_window"))
    small_names = ["meta", "norm_c_w", "gla_w_gate", "gla_b_gate", "gla_norm_w"]
    small_shards = [meta, norm_c_w[0], gla_w_gate[0], gla_b_gate[0], gla_norm_w[0]]
    small_shapes = [s.shape for s in small_shards]
    gb = _unpack_big(_all_gather(_pack_big(big_shards, FIRST_LAYOUT), "gather_first"), FIRST_LAYOUT)
    rest_hook = _gather_hook(_pack_big(big_shards, REST_LAYOUT))
    gs = _all_gather(_pack(small_shards, F32, 8), "gather_small")
    w_in_ab_f = gb["w_in_ab"].transpose(1, 0, 2).reshape(D_MODEL, IN_AB)
    s_meta, s_norm_c, s_wgate, s_bgate, s_gnorm = _unpack(gs, small_shapes)
    meta_f = s_meta.transpose(1, 0, 2).reshape(N_META, D_MODEL)
    norm_c_f = s_norm_c.reshape(1, D_MODEL)
    w_gate_f = jnp.pad(s_wgate.transpose(1, 0, 2).reshape(GLA_RANK, GLA_QK), ((0, GATE_PAD - GLA_RANK), (0, 0)))
    b_gate_f = s_bgate.reshape(1, GLA_QK)
    gla_norm_f = s_gnorm.reshape(GLA_H, 1, GLA_DV)

    pos = jnp.maximum(jnp.arange(lp, dtype=F32) - float(PAD), 0.0)
    inv_freq = jnp.power(ROPE_BASE, -jnp.arange(0, RET_DK, 2, dtype=F32) / RET_DK)
    ang = pos[:, None] * inv_freq[None, :]
    cos2 = jnp.concatenate([jnp.cos(ang), jnp.cos(ang)], axis=1)
    sin2 = jnp.concatenate([-jnp.sin(ang), jnp.sin(ang)], axis=1)
    log_g = jnp.log1p(-jnp.exp2(-5.0 - jnp.arange(RET_H, dtype=F32)))
    lg = jnp.broadcast_to(log_g[:, None, None], (RET_H, 1, 128))
    ret_norm_h = ret_norm_w.reshape(RET_H, 1, RET_DV)

    h0 = jnp.concatenate([jnp.zeros((PAD, D_MODEL), F32), meta_f, x[0]], axis=0)

    def rowmask(i):
        return (_iota2((CHUNK, 1), 0) + i * CHUNK) >= PAD

    (hn0,), _ = _rows(lambda i, h, w: ((_rms(h, w),), ()), [_win(h0)], [norm_ab_w], [(D_MODEL, BF16)], [],
                      name="norm_ab_fwd", nrow=lp)
    proj_ab, (gathered_rest,) = _mm(hn0, w_in_ab_f, "nn", name="in_ab_fwd", hook=rest_hook)
    gb = _unpack_big(gathered_rest, REST_LAYOUT)
    w_glu_f = gb["s5_w_glu"].reshape(S5_W, S5_W)
    w_out_ab_f = gb["w_out_ab"].reshape(OUT_AB, D_MODEL)
    w_in_c_f = sum(jnp.pad(gb["w_in_c"][d], ((0, 0), (WIN_STEP * d, IN_C_PAD - WIN_STEP * d - WIN_COLS)))
                   for d in range(N_DEV))
    w_out_c_f = gb["w_out_c"].reshape(GLA_W, D_MODEL)

    q_off, k_off, v_off, za_off = 0, RET_QK, 2 * RET_QK, 2 * RET_QK + RET_W
    u_off, zb_off = 2 * RET_QK + 2 * RET_W, 2 * RET_QK + 2 * RET_W + S5_W
    ret_xs = [(proj_ab, RET_DK, lambda h: q_off // RET_DK + h), (proj_ab, RET_DK, lambda h: k_off // RET_DK + h),
              (proj_ab, RET_DV, lambda h: v_off // RET_DV + h), (proj_ab, RET_DV, lambda h: za_off // RET_DV + h)]
    ret_cs = [(cos2, RET_DK, lambda h: 0), (sin2, RET_DK, lambda h: 0)]
    ret_kw = dict(heads=RET_H, nchunk=nchunk, s_shape=(RET_DK, RET_DV), out_w=RET_DV, pre=_ret_pre)
    o_a, ret_sprev = _scan_fwd(_ret_chunk, ret_xs, ret_cs, [ret_norm_h], [lg], name="ret_fwd", **ret_kw)

    expand = jnp.repeat(jnp.eye(S5_P, dtype=F32), S5_GH, axis=1)
    disc_args = (s5_lam_re[0], s5_lam_im[0], s5_log_dt[0].reshape(S5_G, 1),
                 s5_b_re[0].reshape(S5_G, S5_P * S5_GH), s5_b_im[0].reshape(S5_G, S5_P * S5_GH), expand)
    ab_re, ab_im, bb_re, bb_im = _s5_disc_fwd(disc_args)
    gt = S5_SUBL
    eye_t = jnp.eye(gt, dtype=F32)

    def tiles_in(bb):
        return jnp.einsum("sgph,gk->sghkp", bb.reshape(gt, gt, S5_P, S5_GH), eye_t).reshape(gt, 128, S5_LANES)

    def tiles_out(cc):
        return jnp.einsum("sghp,gk->sgpkh", cc.reshape(gt, gt, S5_GH, S5_P), eye_t).reshape(gt, S5_LANES, 128)

    wb_t = jnp.concatenate([tiles_in(bb_re), tiles_in(bb_im)], axis=0).astype(BF16)
    wc_t = jnp.concatenate([tiles_out(s5_c_re[0]), -tiles_out(s5_c_im[0])], axis=0).astype(BF16)
    a_re, a_im = ab_re.reshape(S5_SUBL, S5_LANES), ab_im.reshape(S5_SUBL, S5_LANES)
    tm5, tk5, nt5 = _tile(lp, 1408, 8), _tile(lp, 1408, 8), 2 * gt
    u_blk = u_off // 128
    wide = pl.BlockSpec((tm5, S5_LANES), lambda i, j, k: (i, j))
    wide_k = pl.BlockSpec((tm5, S5_LANES), lambda i, j, k: (i, k * gt + j))
    narrow = pl.BlockSpec((tm5, 128), lambda i, j, k: (i, j))
    wb_j = pl.BlockSpec((None, 128, S5_LANES), lambda i, j, k: (j, 0, 0))
    wc_j = pl.BlockSpec((None, S5_LANES, 128), lambda i, j, k: (j, 0, 0))
    wb_k = pl.BlockSpec((None, 128, S5_LANES), lambda i, j, k: (k * gt + j, 0, 0))
    wc_k = pl.BlockSpec((None, S5_LANES, 128), lambda i, j, k: (k * gt + j, 0, 0))
    wide_shape = jax.ShapeDtypeStruct((lp, 2 * S5_N), F32)
    bu = _mm_core(proj_ab, wb_t, dims=NN, grid=(lp // tm5, nt5, 1), name="s5_bu",
                  a_spec=pl.BlockSpec((tm5, 128), lambda i, j, k: (i, u_blk + j % gt)), b_spec=wb_j,
                  o_spec=wide, out_shape=wide_shape, acc_shape=(tm5, S5_LANES))
    xs5 = _s5_scan_fwd(bu.reshape(lp, 2 * S5_SUBL, S5_LANES), a_re, a_im)
    xs5_2d = xs5.reshape(lp, 2 * S5_N)
    y_pre = _mm_core(xs5_2d, wc_t, dims=NN, grid=(lp // tm5, gt, 2), name="s5_cx", a_spec=wide_k, b_spec=wc_k,
                     o_spec=narrow, out_shape=jax.ShapeDtypeStruct((lp, S5_W), F32), acc_shape=(tm5, 128))
    (y_s5, yg_bf), _ = _rows(
        lambda i, yp, u, d: ((yp + d * u, _gelu(yp + d * u)), ()),
        [_win(y_pre), _win(proj_ab, u_off, S5_W)], [s5_d], [(S5_W, F32), (S5_W, BF16)], [], name="s5_gelu_fwd", nrow=lp)
    t_glu = _mm(yg_bf, w_glu_f, "nn", name="s5_glu_fwd")

    def s5_gate(y, t, zb):
        return _gelu(y) * _sigmoid(t) * _silu(zb)

    (o_b,), _ = _rows(lambda i, y, t, zb: ((s5_gate(y, t, zb),), ()),
                      [_win(y_s5), _win(t_glu), _win(proj_ab, zb_off, S5_W)], [], [(S5_W, BF16)], [],
                      name="s5_gate_fwd", nrow=lp)
    o_ab = jnp.concatenate([o_a, o_b], axis=1)
    h1 = _mm(o_ab, w_out_ab_f, "nn", name="out_ab_fwd", add=h0)

    (hn1,), _ = _rows(lambda i, h, w: ((_rms(h, w),), ()), [_win(h1)], [norm_c_f], [(D_MODEL, BF16)], [],
                      name="norm_c_fwd", nrow=lp)
    proj_c = _mm(hn1, w_in_c_f, "nn", name="in_c_fwd")
    gl_off = 2 * GLA_QK + 2 * GLA_W
    pre_gate = _mm(proj_c, w_gate_f, "nn", name="gate_fwd", a_win=(gl_off, GATE_PAD), bias=b_gate_f)
    gla_xs = [(proj_c, GLA_DK, lambda h: h), (proj_c, GLA_DK, lambda h: GLA_QK // GLA_DK + h),
              (proj_c, GLA_DV, lambda h: 2 * GLA_QK // GLA_DV + h),
              (proj_c, GLA_DV, lambda h: (2 * GLA_QK + GLA_W) // GLA_DV + h),
              (pre_gate, GLA_DK, lambda h: h)]
    gla_kw = dict(heads=GLA_H, nchunk=nchunk, s_shape=(GLA_DV, GLA_DK), out_w=GLA_DV)
    o_c, gla_sprev = _scan_fwd(_gla_chunk, gla_xs, [], [gla_norm_f], [], name="gla_fwd", **gla_kw)
    h2 = _mm(o_c, w_out_c_f, "nn", name="out_c_fwd", add=h1)

    fnw = final_norm_w.reshape(1, D_MODEL)

    def final_fn(i, h, tgt, w):
        def loss_of(h_, w_):
            err = _rms(h_, w_) - tgt
            return 0.5 * jnp.sum(jnp.mean(err * err, axis=-1))

        real = (i > 0).astype(F32)
        loss_i, (dh, dw) = jax.value_and_grad(loss_of, argnums=(0, 1))(h, w)
        return (dh * real,), (jnp.full((1, 128), loss_i * real, F32), dw * real)

    (dh2,), (loss_acc, g_final) = _rows(final_fn, [_win(h2), _win(loss_target[0], roff=1)], [fnw],
                                        [(D_MODEL, F32)], [(1, 128), (1, D_MODEL)], name="final_loss", nrow=lp)

    def rs_front(pieces, layout, tag):
        g_full = _pack_big(pieces, layout)
        prow = g_full.shape[1]
        from_sibling = _swap_with_sibling(g_full.astype(BF16), "rs_sibling_" + tag)
        mine_by_chip = lax.dynamic_index_in_dim(g_full.reshape(4, 2, prow, PACK_COLS), core, axis=1, keepdims=False)
        (p1, p1_bf), _ = _rows(
            lambda i, a, b: ((a + b.astype(F32), a + b.astype(F32)), ()),
            [_win(mine_by_chip.reshape(4 * prow, PACK_COLS)), _win(from_sibling.reshape(4 * prow, PACK_COLS))], [],
            [(PACK_COLS, F32), (PACK_COLS, BF16)], [], name="rs_sum_sibling_" + tag, nrow=4 * prow,
            tr=_tile(prow, 512, 16))
        return p1.reshape(4, prow, PACK_COLS), p1_bf.reshape(4, prow, PACK_COLS)

    def rs_back(p1, from_chips, layout, tag):
        prow = p1.shape[1]
        tr = _tile(prow, 512, 16)
        own = lax.dynamic_index_in_dim(p1, chip, axis=0, keepdims=False)
        fc2 = from_chips.reshape(3 * prow, PACK_COLS)
        nblk = prow // tr
        (g_shard,), _ = _rows(
            lambda i, a, b0, b1, b2: ((((a + b0.astype(F32)) + b1.astype(F32)) + b2.astype(F32),), ()),
            [_win(own), _win(fc2), _win(fc2, roff=-nblk), _win(fc2, roff=-2 * nblk)], [], [(PACK_COLS, F32)], [],
            name="rs_sum_chips_" + tag, nrow=prow, tr=tr)
        return _unpack_big(g_shard, layout)

    dh2_bf = dh2.astype(BF16)
    do_c = _mm(dh2_bf, w_out_c_f, "nt", name="out_c_dx", out_dtype=BF16)
    gw_out_c = _mm(o_c, dh2_bf, "tn", name="out_c_dw")
    (dq_c, dk_c, dv_c, dz_c, dpre), (g_gla_norm,) = _scan_bwd(
        _gla_chunk, gla_xs, [], [gla_norm_f], [], do_c, gla_sprev, name="gla_bwd", **gla_kw)
    dglow = _mm(dpre, w_gate_f, "nt", name="gate_dx", out_dtype=BF16)
    g_wgate = _mm(proj_c, dpre, "tn", name="gate_dw", a_win=(gl_off, GATE_PAD))[:GLA_RANK]
    (), (g_bgate,) = _rows(lambda i, d: ((), (jnp.sum(d.astype(F32), axis=0, keepdims=True),)), [_win(dpre)], [], [],
                           [(1, GLA_QK)], name="gate_db", nrow=lp)
    dproj_c = jnp.concatenate([dq_c, dk_c, dv_c, dz_c, dglow], axis=1)
    dhn1 = _mm(dproj_c, w_in_c_f, "nt", name="in_c_dx")
    gw_in_c = _mm(hn1, dproj_c, "tn", name="in_c_dw")

    def norm_bwd(i, h, dhn, dres, w):
        _, vjp = jax.vjp(_rms, h, w)
        dh, dw = vjp(dhn)
        return (jnp.where(rowmask(i), dh + dres, 0.0),), (dw,)

    (dh1,), (g_norm_c,) = _rows(norm_bwd, [_win(h1), _win(dhn1), _win(dh2)], [norm_c_f], [(D_MODEL, F32)],
                                [(1, D_MODEL)], name="norm_c_bwd", nrow=lp)

    dh1_bf = dh1.astype(BF16)
    do_ab = _mm(dh1_bf, w_out_ab_f, "nt", name="out_ab_dx", out_dtype=BF16)
    gw_out_ab = _mm(o_ab, dh1_bf, "tn", name="out_ab_dw")

    def s5_gate_bwd(i, dob, y, t, zb):
        _, vjp = jax.vjp(s5_gate, y, t, zb)
        dy, dt, dzb = vjp(dob.astype(F32))
        return (dy, dt, dzb), ()

    (dy_a, dt_glu, dzb), _ = _rows(
        s5_gate_bwd, [_win(do_ab, RET_W, S5_W), _win(y_s5), _win(t_glu), _win(proj_ab, zb_off, S5_W)], [],
        [(S5_W, F32), (S5_W, BF16), (S5_W, BF16)], [], name="s5_gate_bwd", nrow=lp)
    dyg2 = _mm(dt_glu, w_glu_f, "nt", name="s5_glu_dx")
    gw_glu = _mm(yg_bf, dt_glu, "tn", name="s5_glu_dw")

    def s5_y_bwd(i, dya, dyg, y, u, d):
        _, vjp = jax.vjp(_gelu, y)
        (dy_g,) = vjp(dyg)
        dy = dya + dy_g
        return (dy, d * dy), (jnp.sum(dy * u, axis=0, keepdims=True),)

    (dy_s5, du1), (g_d,) = _rows(
        s5_y_bwd, [_win(dy_a), _win(dyg2), _win(y_s5), _win(proj_ab, u_off, S5_W)], [s5_d],
        [(S5_W, BF16), (S5_W, F32)], [(1, S5_W)], name="s5_y_bwd", nrow=lp)
    gx = _mm_core(dy_s5, wc_t, dims=NT, grid=(lp // tm5, nt5, 1), name="s5_cx_dx",
                  a_spec=pl.BlockSpec((tm5, 128), lambda i, j, k: (i, j % gt)), b_spec=wc_j,
                  o_spec=wide, out_shape=wide_shape, acc_shape=(tm5, S5_LANES))
    rows_k = lambda col: pl.BlockSpec((tk5, col), lambda i, j, k: (k, i))
    gwc = _mm_core(xs5_2d, dy_s5, dims=TN, grid=(nt5, 1, lp // tk5), name="s5_cx_dw", a_spec=rows_k(S5_LANES),
                   b_spec=pl.BlockSpec((tk5, 128), lambda i, j, k: (k, i % gt)),
                   o_spec=pl.BlockSpec((None, S5_LANES, 128), lambda i, j, k: (i, 0, 0)),
                   out_shape=jax.ShapeDtypeStruct((nt5, S5_LANES, 128), F32), acc_shape=(S5_LANES, 128))
    g_s5, da = _s5_scan_bwd(gx.reshape(lp, 2 * S5_SUBL, S5_LANES), xs5, a_re, a_im)
    g_s5_2d = g_s5.reshape(lp, 2 * S5_N)
    du = _mm_core(g_s5_2d, wb_t, dims=NT, grid=(lp // tm5, gt, 2), name="s5_bu_dx", a_spec=wide_k, b_spec=wb_k,
                  o_spec=narrow, out_shape=jax.ShapeDtypeStruct((lp, S5_W), BF16), acc_shape=(tm5, 128),
                  extra=[(du1, narrow)])
    gwb = _mm_core(proj_ab, g_s5_2d, dims=TN, grid=(nt5, 1, lp // tk5), name="s5_bu_dw",
                   a_spec=pl.BlockSpec((tk5, 128), lambda i, j, k: (k, u_blk + i % gt)), b_spec=rows_k(S5_LANES),
                   o_spec=pl.BlockSpec((None, 128, S5_LANES), lambda i, j, k: (i, 0, 0)),
                   out_shape=jax.ShapeDtypeStruct((nt5, 128, S5_LANES), F32), acc_shape=(128, S5_LANES))
    gwc6 = gwc.reshape(2, gt, gt, S5_P, gt, S5_GH)
    g_c = jnp.einsum("rsgpgh->rsghp", gwc6).reshape(2, S5_G, S5_GH, S5_P)
    g_c_re, g_c_im = g_c[0], -g_c[1]
    gwb6 = gwb.reshape(2, gt, gt, S5_GH, gt, S5_P)
    d_bb = jnp.einsum("rsghgp->rsgph", gwb6).reshape(2, S5_G, S5_P * S5_GH)
    d_bb_re, d_bb_im = d_bb[0], d_bb[1]
    g_lam_re, g_lam_im, g_log_dt, g_b_re, g_b_im = _s5_disc_bwd(
        disc_args, (da[0].reshape(S5_G, S5_P), da[1].reshape(S5_G, S5_P), d_bb_re, d_bb_im))

    p1_rest, p1_rest_bf = rs_front(dict(
        s5_w_glu=gw_glu.reshape(N_DEV, S5_W // N_DEV, S5_W),
        w_out_ab=gw_out_ab.reshape(N_DEV, OUT_AB // N_DEV, D_MODEL),
        w_in_c=jnp.stack([gw_in_c[:, WIN_STEP * d:WIN_STEP * d + WIN_COLS] for d in range(N_DEV)]),
        w_out_c=gw_out_c.reshape(N_DEV, GLA_W // N_DEV, D_MODEL)), REST_LAYOUT, "rest")
    (dq_a, dk_a, dv_a, dz_a), (g_ret_norm,), (from_chips_rest,) = _scan_bwd(
        _ret_chunk, ret_xs, ret_cs, [ret_norm_h], [lg], do_ab, ret_sprev, name="ret_bwd", post=_ret_post,
        hook=_chips_hook(p1_rest_bf), **ret_kw)
    dproj_ab = jnp.concatenate([dq_a, dk_a, dv_a, dz_a, du, dzb], axis=1)
    gw_in_ab = _mm(hn0, dproj_ab, "tn", name="in_ab_dw")
    p1_first, p1_first_bf = rs_front(
        dict(w_in_ab=gw_in_ab.reshape(D_MODEL, N_DEV, IN_AB // N_DEV).transpose(1, 0, 2)), FIRST_LAYOUT, "first")
    dhn0, (from_chips_first,) = _mm(dproj_ab, w_in_ab_f, "nt", name="in_ab_dx", hook=_chips_hook(p1_first_bf))
    (dh0,), (g_norm_ab,) = _rows(norm_bwd, [_win(h0), _win(dhn0), _win(dh1)], [norm_ab_w], [(D_MODEL, F32)],
                                 [(1, D_MODEL)], name="norm_ab_bwd", nrow=lp)
    grad_x = dh0[CHUNK:][None]
    g_meta_part = dh0[PAD:CHUNK]

    big_grads = {**rs_back(p1_rest, from_chips_rest, REST_LAYOUT, "rest"),
                 **rs_back(p1_first, from_chips_first, FIRST_LAYOUT, "first")}
    big_grads["w_in_c"] = _lane_select(big_grads["w_in_c"], win_off, -1, 896, F32, True,
                                       "w_in_c_from_window")[:, :SHARD_C]

    small_parts = dict(
        norm_ab_w=g_norm_ab, ret_norm_w=g_ret_norm, s5_lam_re=g_lam_re, s5_lam_im=g_lam_im, s5_log_dt=g_log_dt,
        s5_b_re=g_b_re, s5_b_im=g_b_im, s5_c_re=g_c_re, s5_c_im=g_c_im, s5_d=g_d, final_norm_w=g_final,
        meta=g_meta_part, norm_c_w=g_norm_c, gla_w_gate=g_wgate, gla_b_gate=g_bgate, gla_norm_w=g_gla_norm,
        loss=loss_acc[:, :1])
    sp_names = list(small_parts)
    sp_shapes = [small_parts[k].shape for k in sp_names]
    small_tr = 64
    sp_pack = _pack([small_parts[k] for k in sp_names], F32, small_tr)
    srow = sp_pack.shape[0]
    sp_all = _all_gather(sp_pack, "gather_grads").reshape(N_DEV * srow, PACK_COLS)
    snb = srow // small_tr

    def sum8(i, *blocks):
        acc = blocks[0]
        for b in blocks[1:]:
            acc = acc + b
        return (acc,), ()

    (sp_sum,), _ = _rows(sum8, [_win(sp_all, roff=-d * snb) for d in range(N_DEV)], [], [(PACK_COLS, F32)], [],
                         name="sum_small", nrow=srow, tr=small_tr)
    small = dict(zip(sp_names, _unpack(sp_sum, sp_shapes)))
    loss = small["loss"].reshape(())

    def my_cols(g, n):
        return lax.dynamic_slice_in_dim(g, dev * n, n, axis=g.ndim - 1)

    grads = dict(
        meta=my_cols(small["meta"], D_MODEL // N_DEV),
        norm_ab_w=small["norm_ab_w"], w_in_ab=big_grads["w_in_ab"][None], ret_norm_w=small["ret_norm_w"].reshape(1, RET_W),
        s5_lam_re=small["s5_lam_re"][None], s5_lam_im=small["s5_lam_im"][None],
        s5_log_dt=small["s5_log_dt"].reshape(1, S5_G),
        s5_b_re=small["s5_b_re"].reshape(1, S5_G, S5_P, S5_GH), s5_b_im=small["s5_b_im"].reshape(1, S5_G, S5_P, S5_GH),
        s5_c_re=small["s5_c_re"][None], s5_c_im=small["s5_c_im"][None], s5_d=small["s5_d"],
        s5_w_glu=big_grads["s5_w_glu"][None], w_out_ab=big_grads["w_out_ab"][None],
        norm_c_w=my_cols(small["norm_c_w"], D_MODEL // N_DEV), w_in_c=big_grads["w_in_c"][None],
        gla_w_gate=my_cols(small["gla_w_gate"], GLA_QK // N_DEV)[None],
        gla_b_gate=my_cols(small["gla_b_gate"], GLA_QK // N_DEV),
        gla_norm_w=my_cols(small["gla_norm_w"].reshape(1, GLA_W), GLA_W // N_DEV),
        w_out_c=big_grads["w_out_c"][None], final_norm_w=small["final_norm_w"].reshape(D_MODEL))

    deltas, new_m, new_v = {}, {}, {}
    for k in order:
        w = weights[k]
        d2, m2, v2 = _adamw(_as2d(w), _as2d(grads[k].reshape(w.shape)), _as2d(mom_m[k]), _as2d(mom_v[k]), "adamw_" + k)
        deltas[k], new_m[k], new_v[k] = d2.reshape(w.shape), m2.reshape(w.shape), v2.reshape(w.shape)
        grads[k] = grads[k].reshape(w.shape)

    return (loss, grad_x, *[grads[k] for k in order], *[deltas[k] for k in order],
            *[new_m[k] for k in order], *[new_v[k] for k in order])
```

```python
import functools
import math

import jax
import jax.numpy as jnp
from jax import lax
from jax.experimental import pallas as pl
from jax.experimental.pallas import tpu as pltpu

F32, BF16 = jnp.float32, jnp.bfloat16
MESH = pl.DeviceIdType.MESH
N_DEV = 8

D_MODEL = 2048
CHUNK = 128
N_META = 16
PAD = CHUNK - N_META
SUB = 16
EPS = 1e-6
RET_H, RET_DK, RET_DV = 8, 128, 256
RET_QK, RET_W = RET_H * RET_DK, RET_H * RET_DV
ROPE_BASE = 10000.0
S5_W, S5_G, S5_P, S5_GH = 1024, 64, 64, 16
S5_N = S5_G * S5_P
GLA_H, GLA_DK, GLA_DV, GLA_RANK, GLA_TAU = 4, 256, 512, 16, 16.0
GLA_QK, GLA_W = GLA_H * GLA_DK, GLA_H * GLA_DV
IN_AB = 2 * RET_QK + 2 * RET_W + 2 * S5_W
OUT_AB = RET_W + S5_W
IN_C = 2 * GLA_QK + 2 * GLA_W + GLA_RANK
GATE_PAD = 256
IN_C_PAD = 2 * GLA_QK + 2 * GLA_W + GATE_PAD
ADAM_LR, ADAM_B1, ADAM_B2, ADAM_EPS, ADAM_WD, ADAM_STEP = 0.001, 0.9, 0.999, 1e-08, 0.01, 10

VMEM_LIMIT_BYTES = 48 * 2 ** 20
PACK_COLS = 1024
PACK_ROW_MULT = 8
SHARD_C = IN_C // N_DEV
WIN_STEP = 768
WIN_COLS = 1024


def _params(sem):
    return pltpu.CompilerParams(dimension_semantics=sem, vmem_limit_bytes=VMEM_LIMIT_BYTES)


def _tile(n, cap, mult):
    best = None
    for t in range(mult, min(n, cap) + 1, mult):
        if n % t == 0:
            best = t
    assert best is not None, (n, cap, mult)
    return best


def _dg(a, b, ca, cb):
    return lax.dot_general(a.astype(BF16), b.astype(BF16), (((ca,), (cb,)), ((), ())),
                           preferred_element_type=F32)


@functools.partial(jax.custom_vjp, nondiff_argnums=(2, 3))
def _bdot(a, b, ca, cb):
    return _dg(a, b, ca, cb)


def _bdot_fwd(a, b, ca, cb):
    return _dg(a, b, ca, cb), (a, b)


def _bdot_bwd(ca, cb, res, g):
    a, b = res
    da = _dg(g, b, 1, 1 - cb) if ca == 1 else _dg(b, g, 1 - cb, 1)
    db = _dg(a, g, 1 - ca, 0) if cb == 0 else _dg(g, a, 0, 1 - ca)
    return da.astype(a.dtype), db.astype(b.dtype)


_bdot.defvjp(_bdot_fwd, _bdot_bwd)


def _sigmoid(x):
    return 1.0 / (1.0 + jnp.exp(-x))


def _silu(x):
    return x * _sigmoid(x)


def _log_sigmoid(x):
    return jnp.minimum(x, 0.0) - jnp.log(1.0 + jnp.exp(-jnp.abs(x)))


def _gelu(x):
    return 0.5 * x * (1.0 + jnp.tanh(math.sqrt(2.0 / math.pi) * (x + 0.044715 * (x * x * x))))


def _rms(x, w):
    return x * lax.rsqrt(jnp.mean(x * x, axis=-1, keepdims=True) + EPS) * w


class _Hook:
    def __init__(self, ins, outs, sems, phases):
        self.ins, self.outs, self.sems, self.phases = list(ins), list(outs), list(sems), list(phases)


_NO_HOOK = _Hook([], [], [], [])
_ANY = pl.BlockSpec(memory_space=pl.ANY)


def _run_hook(hook, lin, total, in_refs, out_refs, sem_refs):
    for frac, fn in hook.phases:
        at = min(int(frac * total), total - 1)

        @pl.when(lin == at)
        def _(fn=fn):
            fn(in_refs, out_refs, sem_refs)


def _mm_core(a, b, *, dims, grid, a_spec, b_spec, o_spec, out_shape, acc_shape, name, extra=(), hook=None):
    nk = grid[2]
    n_extra = len(extra)
    hook = _NO_HOOK if hook is None else hook
    hi, ho = len(hook.ins), len(hook.outs)

    def body(*refs):
        a_ref, b_ref = refs[0], refs[1]
        o_ref, acc = refs[2 + n_extra + hi], refs[3 + n_extra + hi + ho]
        k = pl.program_id(2)
        lin = (pl.program_id(0) * grid[1] + pl.program_id(1)) * nk + k
        _run_hook(hook, lin, grid[0] * grid[1] * nk, refs[2 + n_extra:2 + n_extra + hi],
                  refs[3 + n_extra + hi:3 + n_extra + hi + ho], refs[4 + n_extra + hi + ho:])

        part = lax.dot_general(a_ref[...].astype(BF16), b_ref[...].astype(BF16), dims, preferred_element_type=F32)

        def finish(r):
            for e in range(n_extra):
                r = r + refs[2 + e][...].astype(F32)
            o_ref[...] = r.astype(o_ref.dtype)

        if nk == 1:
            finish(part)
        else:
            @pl.when(k == 0)
            def _():
                acc[...] = part

            @pl.when(k > 0)
            def _():
                acc[...] += part

            @pl.when(k == nk - 1)
            def _():
                finish(acc[...])

    res = pl.pallas_call(
        body, name=name, grid=grid,
        in_specs=[a_spec, b_spec] + [sp for _, sp in extra] + [_ANY] * hi,
        out_specs=[o_spec] + [_ANY] * ho, out_shape=[out_shape] + hook.outs,
        scratch_shapes=[pltpu.VMEM(acc_shape if nk > 1 else (8, 128), F32)] + hook.sems,
        compiler_params=_params(("arbitrary", "arbitrary", "arbitrary")),
    )(a, b, *[arr for arr, _ in extra], *hook.ins)
    return res[0] if hook is _NO_HOOK else (res[0], res[1:])


NN, NT, TN = (((1,), (0,)), ((), ())), (((1,), (1,)), ((), ())), (((0,), (0,)), ((), ()))


FULL_K = 2048


def _mm(a, b, mode, *, name, out_dtype=F32, a_win=None, add=None, bias=None, hook=None, b_dev=False,
        out_dest=False):
    if mode == "tn":
        kdim, n = a.shape[0], b.shape[1]
        m = a.shape[1] if a_win is None else a_win[1]
        tm, tn, tk = _tile(m, 1024, 128), _tile(n, 1024, 128), _tile(kdim, 1408, 8)
        off = 0 if a_win is None else a_win[0] // tm
        a_spec = pl.BlockSpec((tk, tm), lambda i, j, k: (k, i + off))
        b_spec = pl.BlockSpec((tk, tn), lambda i, j, k: (k, j))
        dims = TN
    else:
        m = a.shape[0]
        kdim = a.shape[1] if a_win is None else a_win[1]
        if b_dev:
            n = b.shape[0] * b.shape[2] if mode == "nn" else b.shape[1]
        else:
            n = b.shape[1] if mode == "nn" else b.shape[0]
        tm = _tile(m, 1408, 8)
        if kdim <= FULL_K:
            tn, tk = _tile(n, 640, 128), kdim
        else:
            tn, tk = _tile(n, 1024, 128), _tile(kdim, 1024, 128)
        off = 0 if a_win is None else a_win[0] // tk
        a_spec = pl.BlockSpec((tm, tk), lambda i, j, k: (i, k + off))
        if mode == "nn":
            dims = NN
            if b_dev:
                per = PACK_COLS // tn
                b_spec = pl.BlockSpec((None, tk, tn), lambda i, j, k: (j // per, k, j % per))
            else:
                b_spec = pl.BlockSpec((tk, tn), lambda i, j, k: (k, j))
        else:
            dims = NT
            if b_dev:
                per = PACK_COLS // tk
                b_spec = pl.BlockSpec((None, tn, tk), lambda i, j, k: (k // per, j, k % per))
            else:
                b_spec = pl.BlockSpec((tn, tk), lambda i, j, k: (j, k))
    if a_win is not None:
        assert a_win[0] % (tm if mode == "tn" else tk) == 0
    extra = []
    if add is not None:
        extra.append((add, pl.BlockSpec((tm, tn), lambda i, j, k: (i, j))))
    if bias is not None:
        extra.append((bias, pl.BlockSpec((1, tn), lambda i, j, k: (0, j))))
    if out_dest:
        per = PACK_COLS // tn
        o_spec = pl.BlockSpec((None, tm, tn), lambda i, j, k: (j // per, i, j % per))
        out_shape = jax.ShapeDtypeStruct((n // PACK_COLS, m, PACK_COLS), out_dtype)
    else:
        o_spec = pl.BlockSpec((tm, tn), lambda i, j, k: (i, j))
        out_shape = jax.ShapeDtypeStruct((m, n), out_dtype)
    return _mm_core(a, b, dims=dims, grid=(m // tm, n // tn, kdim // tk), a_spec=a_spec, b_spec=b_spec,
                    o_spec=o_spec, out_shape=out_shape, acc_shape=(tm, tn), name=name, extra=extra, hook=hook)


def _win(arr, col0=0, width=None, roff=0):
    return (arr, col0, arr.shape[1] if width is None else width, roff)


def _rows(fn, rows, consts, outs, accs, *, name, nrow, tr=CHUNK):
    nr, nc, no = len(rows), len(consts), len(outs)

    def body(*refs):
        i = pl.program_id(0)
        ins = [r[...] for r in refs[:nr + nc]]
        o_refs = refs[nr + nc:nr + nc + no]
        a_refs = refs[nr + nc + no:]
        res_o, res_a = fn(i, *ins)
        for r, v in zip(o_refs, res_o):
            r[...] = v.astype(r.dtype)
        if a_refs:
            @pl.when(i == 0)
            def _():
                for r in a_refs:
                    r[...] = jnp.zeros_like(r)

            for r, v in zip(a_refs, res_a):
                r[...] += v

    in_specs = []
    for (arr, col0, width, roff) in rows:
        assert col0 % width == 0 and arr.shape[0] % tr == 0
        in_specs.append(pl.BlockSpec((tr, width), lambda i, c=col0 // width, ro=roff: (jnp.maximum(i - ro, 0), c)))
    for c in consts:
        in_specs.append(pl.BlockSpec(c.shape, lambda i, nd=c.ndim: (0,) * nd))
    out_specs = [pl.BlockSpec((tr, w), lambda i: (i, 0)) for (w, _) in outs]
    out_specs += [pl.BlockSpec(s, lambda i, nd=len(s): (0,) * nd) for s in accs]
    out_shape = [jax.ShapeDtypeStruct((nrow, w), dt) for (w, dt) in outs]
    out_shape += [jax.ShapeDtypeStruct(s, F32) for s in accs]
    res = pl.pallas_call(
        body, name=name, grid=(nrow // tr,), in_specs=in_specs, out_specs=out_specs, out_shape=out_shape,
        compiler_params=_params(("arbitrary",)),
    )(*[r[0] for r in rows], *consts)
    return res[:no], res[no:]


def _scan_specs(xs, cs, ws, ks, chunk_of):
    specs = []
    for (arr, width, colfn) in list(xs) + list(cs):
        specs.append(pl.BlockSpec((CHUNK, width), lambda h, n, f=colfn: (chunk_of(n), f(h))))
    for arr in list(ws) + list(ks):
        specs.append(pl.BlockSpec((1, 1, arr.shape[2]), lambda h, n: (h, 0, 0)))
    return specs


def _scan_fwd(fn, xs, cs, ws, ks, *, heads, nchunk, s_shape, out_w, name, pre=None, hook=None):
    nx, ncs, nw = len(xs), len(cs), len(ws)
    hook = _NO_HOOK if hook is None else hook
    hi, ho = len(hook.ins), len(hook.outs)

    def body(*refs):
        n = pl.program_id(1)
        nin = nx + ncs + nw + len(ks)
        y_ref, sp_ref = refs[nin + hi], refs[nin + hi + 1]
        s_scr = refs[nin + hi + 2 + ho]
        _run_hook(hook, pl.program_id(0) * nchunk + n, heads * nchunk, refs[nin:nin + hi],
                  refs[nin + hi + 2:nin + hi + 2 + ho], refs[nin + hi + 3 + ho:])

        @pl.when(n == 0)
        def _():
            s_scr[...] = jnp.zeros_like(s_scr)

        state = s_scr[...]
        sp_ref[0, 0] = state
        xv = [r[...] for r in refs[:nx]]
        cv = [r[...] for r in refs[nx:nx + ncs]]
        wv = [r[0] for r in refs[nx + ncs:nx + ncs + nw]]
        kv = [r[0] for r in refs[nx + ncs + nw:nin]]
        if pre is not None:
            xv = pre(xv, cv)
        y, s_new = fn(n, xv, state, cv, wv, kv)
        y_ref[...] = y.astype(y_ref.dtype)
        s_scr[...] = s_new

    lp = nchunk * CHUNK
    res = pl.pallas_call(
        body, name=name, grid=(heads, nchunk),
        in_specs=_scan_specs(xs, cs, ws, ks, lambda n: n) + [_ANY] * hi,
        out_specs=[pl.BlockSpec((CHUNK, out_w), lambda h, n: (n, h)),
                   pl.BlockSpec((1, 1) + s_shape, lambda h, n: (h, n, 0, 0))] + [_ANY] * ho,
        out_shape=[jax.ShapeDtypeStruct((lp, heads * out_w), BF16),
                   jax.ShapeDtypeStruct((heads, nchunk) + s_shape, F32)] + hook.outs,
        scratch_shapes=[pltpu.VMEM(s_shape, F32)] + hook.sems,
        compiler_params=_params(("arbitrary", "arbitrary")),
    )(*[t[0] for t in xs], *[t[0] for t in cs], *ws, *ks, *hook.ins)
    return (res[0], res[1]) if hook is _NO_HOOK else (res[0], res[1], res[2:])


def _scan_bwd(fn, xs, cs, ws, ks, dy, sprev, *, heads, nchunk, s_shape, out_w, name, pre=None, post=None,
              hook=None):
    nx, ncs, nw = len(xs), len(cs), len(ws)
    nin = nx + ncs + nw + len(ks)
    hook = _NO_HOOK if hook is None else hook
    hi, ho = len(hook.ins), len(hook.outs)

    def body(*refs):
        step = pl.program_id(1)
        n = nchunk - 1 - step
        dy_ref, sp_ref = refs[nin], refs[nin + 1]
        o0 = nin + 2 + hi
        dx_refs = refs[o0:o0 + nx]
        dw_refs = refs[o0 + nx:o0 + nx + nw]
        ds_scr = refs[o0 + nx + nw + ho]
        _run_hook(hook, pl.program_id(0) * nchunk + step, heads * nchunk, refs[nin + 2:o0],
                  refs[o0 + nx + nw:o0 + nx + nw + ho], refs[o0 + nx + nw + ho + 1:])

        @pl.when(step == 0)
        def _():
            ds_scr[...] = jnp.zeros_like(ds_scr)
            for r in dw_refs:
                r[...] = jnp.zeros_like(r)

        xv = [r[...] for r in refs[:nx]]
        cv = [r[...] for r in refs[nx:nx + ncs]]
        wv = [r[0] for r in refs[nx + ncs:nx + ncs + nw]]
        kv = [r[0] for r in refs[nx + ncs + nw:nin]]
        if pre is not None:
            xv = pre(xv, cv)
        _, vjp = jax.vjp(lambda xs_, s_, ws_: fn(n, xs_, s_, cv, ws_, kv), xv, sp_ref[0, 0], wv)
        dxs, ds_prev, dws = vjp((dy_ref[...].astype(F32), ds_scr[...]))
        if post is not None:
            dxs = post(dxs, cv)
        for r, v in zip(dx_refs, dxs):
            r[...] = v.astype(r.dtype)
        for r, v in zip(dw_refs, dws):
            r[0] += v
        ds_scr[...] = ds_prev

    lp = nchunk * CHUNK
    rev = lambda n: nchunk - 1 - n
    in_specs = _scan_specs(xs, cs, ws, ks, rev)
    in_specs.append(pl.BlockSpec((CHUNK, out_w), lambda h, n: (rev(n), h)))
    in_specs.append(pl.BlockSpec((1, 1) + s_shape, lambda h, n: (h, rev(n), 0, 0)))
    out_specs = [pl.BlockSpec((CHUNK, w), lambda h, n: (rev(n), h)) for (_, w, _) in xs]
    out_specs += [pl.BlockSpec((1, 1, w.shape[2]), lambda h, n: (h, 0, 0)) for w in ws]
    out_shape = [jax.ShapeDtypeStruct((lp, heads * w), BF16) for (_, w, _) in xs]
    out_shape += [jax.ShapeDtypeStruct(w.shape, F32) for w in ws]
    res = pl.pallas_call(
        body, name=name, grid=(heads, nchunk), in_specs=in_specs + [_ANY] * hi,
        out_specs=out_specs + [_ANY] * ho, out_shape=out_shape + hook.outs,
        scratch_shapes=[pltpu.VMEM(s_shape, F32)] + hook.sems,
        compiler_params=_params(("arbitrary", "arbitrary")),
    )(*[t[0] for t in xs], *[t[0] for t in cs], *ws, *ks, dy, sprev, *hook.ins)
    if hook is _NO_HOOK:
        return res[:nx], res[nx:]
    return res[:nx], res[nx:nx + nw], res[nx + nw:]


def _iota2(shape, dim):
    return lax.broadcasted_iota(jnp.int32, shape, dim)


def _ret_chunk(n, xs, state, cs, ws, ks):
    q, k, v, z = xs
    (w,), (lg,) = ws, ks
    lgc = lg[:, :1]
    row, col = _iota2((CHUNK, CHUNK), 0), _iota2((CHUNK, CHUNK), 1)
    diff = jnp.maximum(row - col, 0).astype(F32)
    decay = jnp.where(row >= col, jnp.exp(lg * diff), 0.0)
    scores = _bdot(q, k, 1, 1) * decay
    o_intra = _bdot(scores, v, 1, 0)
    idx = _iota2((CHUNK, 1), 0).astype(F32)
    k_w = k * jnp.exp(lgc * (CHUNK - 1.0 - idx))
    kv = _bdot(k_w, v, 0, 0)
    s_new = state * jnp.exp(lgc * float(CHUNK)) + kv
    q_w = q * jnp.exp(lgc * (idx + 1.0))
    o = o_intra + _bdot(q_w, state, 1, 0)
    return _rms(o, w) * _silu(z), s_new


def _rope(t, cos2, sin2):
    return t * cos2 + pltpu.roll(t, RET_DK // 2, 1) * sin2


def _rope_t(g, cos2, sin2):
    return g * cos2 - pltpu.roll(g, RET_DK // 2, 1) * sin2


def _ret_pre(xv, cv):
    q, k, v, z = xv
    cos2, sin2 = cv
    return [_rope(q, cos2, sin2), _rope(k, cos2, sin2) * (RET_DK ** -0.5), v, z]


def _ret_post(dxs, cv):
    dq, dk, dv, dz = dxs
    cos2, sin2 = cv
    return [_rope_t(dq, cos2, sin2), _rope_t(dk, cos2, sin2) * (RET_DK ** -0.5), dv, dz]


def _gla_chunk(n, xs, state_t, cs, ws, ks):
    q, k, v, z, pre = xs
    (w,) = ws
    q = q * (GLA_DK ** -0.5)
    rowc = _iota2((CHUNK, 1), 0)
    valid = jnp.logical_or(n > 0, rowc >= PAD)
    log_a = jnp.where(valid, _log_sigmoid(pre) / GLA_TAU, 0.0)
    row, col = _iota2((CHUNK, CHUNK), 0), _iota2((CHUNK, CHUNK), 1)
    tri = (row >= col).astype(F32)
    b = jnp.dot(tri, log_a, precision=lax.Precision.HIGHEST, preferred_element_type=F32)
    b_last = b[CHUNK - 1:CHUNK, :]
    kv_t = _bdot(v, k * jnp.exp(b_last - b), 0, 0)
    s_new = state_t * jnp.exp(b_last) + kv_t
    o_inter = _bdot(q * jnp.exp(b), state_t, 1, 1)
    outs = []
    for s in range(CHUNK // SUB):
        lo, hi = s * SUB, (s + 1) * SUB
        b_ref = jnp.zeros_like(b_last) if s == 0 else b[lo - 1:lo, :]
        q_hat = q[lo:hi] * jnp.exp(b[lo:hi] - b_ref)
        k_hat = k[:hi] * jnp.exp(b_ref - b[:hi])
        sc = _bdot(q_hat, k_hat, 1, 1)
        causal = _iota2((SUB, hi), 0) + lo >= _iota2((SUB, hi), 1)
        outs.append(_bdot(jnp.where(causal, sc, 0.0), v[:hi], 1, 0))
    o = jnp.concatenate(outs, axis=0) + o_inter
    return _rms(o, w) * _silu(z), s_new


def _s5_disc(lam_re, lam_im, log_dt, b_re, b_im, expand):
    dt = jnp.exp(log_dt)
    mag = jnp.exp(lam_re * dt)
    ab_re, ab_im = mag * jnp.cos(lam_im * dt), mag * jnp.sin(lam_im * dt)
    den = lam_re * lam_re + lam_im * lam_im
    nr, ni = ab_re - 1.0, ab_im
    f_re = (nr * lam_re + ni * lam_im) / den
    f_im = (ni * lam_re - nr * lam_im) / den
    hp = lax.Precision.HIGHEST
    f_re = jnp.dot(f_re, expand, precision=hp, preferred_element_type=F32)
    f_im = jnp.dot(f_im, expand, precision=hp, preferred_element_type=F32)
    return ab_re, ab_im, f_re * b_re - f_im * b_im, f_re * b_im + f_im * b_re


def _s5_disc_fwd(args):
    def body(*refs):
        outs = _s5_disc(*[r[...] for r in refs[:6]])
        for r, v in zip(refs[6:], outs):
            r[...] = v

    g, p = args[0].shape
    return pl.pallas_call(
        body, name="s5_disc_fwd",
        out_shape=[jax.ShapeDtypeStruct((g, p), F32)] * 2 + [jax.ShapeDtypeStruct(args[3].shape, F32)] * 2,
    )(*args)


def _s5_disc_bwd(args, cts):
    def body(*refs):
        prim = [r[...] for r in refs[:5]]
        expand = refs[5][...]
        ct = tuple(r[...] for r in refs[6:10])
        _, vjp = jax.vjp(lambda *a: _s5_disc(*a, expand), *prim)
        for r, v in zip(refs[10:], vjp(ct)):
            r[...] = v

    return pl.pallas_call(
        body, name="s5_disc_bwd", out_shape=[jax.ShapeDtypeStruct(a.shape, F32) for a in args[:5]],
    )(*args, *cts)


S5_SUBL = 8
S5_LANES = S5_N // S5_SUBL
S5_TB = 64


def _s5_scan_fwd(bu, a_re, a_im):
    lp = bu.shape[0]

    def body(bu_ref, ar_ref, ai_ref, x_ref, st):
        @pl.when(pl.program_id(0) == 0)
        def _():
            st[...] = jnp.zeros_like(st)

        ar, ai = ar_ref[...], ai_ref[...]

        def step(t, carry):
            xr, xi = carry
            nr = ar * xr - ai * xi + bu_ref[t, 0:S5_SUBL, :]
            ni = ar * xi + ai * xr + bu_ref[t, S5_SUBL:2 * S5_SUBL, :]
            x_ref[t, 0:S5_SUBL, :] = nr
            x_ref[t, S5_SUBL:2 * S5_SUBL, :] = ni
            return nr, ni

        xr, xi = lax.fori_loop(0, S5_TB, step, (st[0], st[1]))
        st[0] = xr
        st[1] = xi

    blk = pl.BlockSpec((S5_TB, 2 * S5_SUBL, S5_LANES), lambda i: (i, 0, 0))
    cst = pl.BlockSpec((S5_SUBL, S5_LANES), lambda i: (0, 0))
    return pl.pallas_call(
        body, name="s5_scan_fwd", grid=(lp // S5_TB,), in_specs=[blk, cst, cst], out_specs=blk,
        out_shape=jax.ShapeDtypeStruct(bu.shape, F32),
        scratch_shapes=[pltpu.VMEM((2, S5_SUBL, S5_LANES), F32)],
        compiler_params=_params(("arbitrary",)),
    )(bu, a_re, a_im)


def _s5_scan_bwd(gx, x, a_re, a_im):
    lp = gx.shape[0]
    nb = lp // S5_TB

    def body(gx_ref, x_ref, xp_ref, ar_ref, ai_ref, g_ref, da_ref, st):
        i = pl.program_id(0)

        @pl.when(i == 0)
        def _():
            st[...] = jnp.zeros_like(st)
            da_ref[...] = jnp.zeros_like(da_ref)

        ar, ai = ar_ref[...], ai_ref[...]
        first = (i == nb - 1).astype(F32)

        def step(s, carry):
            gr, gi, dar, dai = carry
            t = S5_TB - 1 - s
            ngr = gx_ref[t, 0:S5_SUBL, :] + ar * gr + ai * gi
            ngi = gx_ref[t, S5_SUBL:2 * S5_SUBL, :] + ar * gi - ai * gr
            g_ref[t, 0:S5_SUBL, :] = ngr
            g_ref[t, S5_SUBL:2 * S5_SUBL, :] = ngi
            tp = jnp.maximum(t - 1, 0)
            at0 = (t == 0).astype(F32)
            keep = 1.0 - at0
            pr = keep * x_ref[tp, 0:S5_SUBL, :] + at0 * (1.0 - first) * xp_ref[0, 0:S5_SUBL, :]
            pi = keep * x_ref[tp, S5_SUBL:2 * S5_SUBL, :] + at0 * (1.0 - first) * xp_ref[0, S5_SUBL:2 * S5_SUBL, :]
            return ngr, ngi, dar + ngr * pr + ngi * pi, dai + ngi * pr - ngr * pi

        zero = jnp.zeros((S5_SUBL, S5_LANES), F32)
        gr, gi, dar, dai = lax.fori_loop(0, S5_TB, step, (st[0], st[1], zero, zero))
        st[0] = gr
        st[1] = gi
        da_ref[0] += dar
        da_ref[1] += dai

    rev = lambda i: nb - 1 - i
    blk = pl.BlockSpec((S5_TB, 2 * S5_SUBL, S5_LANES), lambda i: (rev(i), 0, 0))
    prev = pl.BlockSpec((1, 2 * S5_SUBL, S5_LANES), lambda i: (jnp.maximum(rev(i) * S5_TB - 1, 0), 0, 0))
    cst = pl.BlockSpec((S5_SUBL, S5_LANES), lambda i: (0, 0))
    return pl.pallas_call(
        body, name="s5_scan_bwd", grid=(nb,), in_specs=[blk, blk, prev, cst, cst],
        out_specs=[blk, pl.BlockSpec((2, S5_SUBL, S5_LANES), lambda i: (0, 0, 0))],
        out_shape=[jax.ShapeDtypeStruct(gx.shape, F32), jax.ShapeDtypeStruct((2, S5_SUBL, S5_LANES), F32)],
        scratch_shapes=[pltpu.VMEM((2, S5_SUBL, S5_LANES), F32)],
        compiler_params=_params(("arbitrary",)),
    )(gx, x, x, a_re, a_im)


def _place():
    x, y, c = lax.axis_index("x"), lax.axis_index("y"), lax.axis_index("c")
    return x, y, c, [(1 - x, y), (x, 1 - y), (1 - x, 1 - y)]


def _gather_phases():
    def plan(x_ref, out_ref, send_sems, recv_sems, local_sem):
        x, y, c, chips = _place()
        me, sibling = (x, y, c), (x, y, 1 - c)

        def rows(px, py, pc):
            return out_ref.at[4 * px + 2 * py + pc]

        def copy(k, block, to, src=None):
            return pltpu.make_async_remote_copy(
                src_ref=rows(*block) if src is None else src, dst_ref=rows(*block),
                send_sem=send_sems.at[k], recv_sem=recv_sems.at[k], device_id=to, device_id_type=MESH)

        mine = pltpu.make_async_copy(x_ref, rows(*me), local_sem)
        first = [copy(0, me, sibling, src=x_ref)]
        first += [copy(1 + j, me, (*chip, c), src=x_ref) for j, chip in enumerate(chips)]
        passed = [copy(4 + j, (*chip, c), sibling) for j, chip in enumerate(chips)]
        return c, chips, me, sibling, copy, mine, first, passed

    def start(ins, outs, sems):
        _, _, _, _, _, mine, first, _ = plan(ins[0], outs[0], *sems)
        mine.start()
        for cp in first:
            cp.start()

    def middle(ins, outs, sems):
        c, chips, me, _, copy, _, _, passed = plan(ins[0], outs[0], *sems)
        for j, chip in enumerate(chips):
            copy(1 + j, (*chip, c), me).wait_recv()
            passed[j].start()

    def finish(ins, outs, sems):
        c, chips, me, sibling, copy, mine, first, passed = plan(ins[0], outs[0], *sems)
        copy(0, sibling, me).wait_recv()
        for j, chip in enumerate(chips):
            copy(4 + j, (*chip, 1 - c), me).wait_recv()
        for cp in first + passed:
            cp.wait_send()
        mine.wait()

    return start, middle, finish


_GATHER_SEMS = [pltpu.SemaphoreType.DMA((7,)), pltpu.SemaphoreType.DMA((7,)), pltpu.SemaphoreType.DMA]


def _all_gather(shard, name):
    phases = _gather_phases()

    def body(x_ref, out_ref, *sems):
        for phase in phases:
            phase([x_ref], [out_ref], sems)

    return pl.pallas_call(
        body, name=name, out_shape=jax.ShapeDtypeStruct((N_DEV,) + shard.shape, shard.dtype),
        in_specs=[_ANY], out_specs=_ANY, scratch_shapes=list(_GATHER_SEMS),
    )(shard)


def _gather_hook(shard):
    start, middle, finish = _gather_phases()
    return _Hook([shard], [jax.ShapeDtypeStruct((N_DEV,) + shard.shape, shard.dtype)], _GATHER_SEMS,
                 [(0.0, start), (0.85, middle), (1.0, finish)])


def _swap_with_sibling(parts, name):
    def body(p_ref, out_ref, send_sems, recv_sems):
        x, y, c, _ = _place()
        copies = [pltpu.make_async_remote_copy(
            src_ref=p_ref.at[2 * chip + (1 - c)], dst_ref=out_ref.at[chip],
            send_sem=send_sems.at[chip], recv_sem=recv_sems.at[chip],
            device_id=(x, y, 1 - c), device_id_type=MESH) for chip in range(4)]
        for cp in copies:
            cp.start()
        for cp in copies:
            cp.wait()

    return pl.pallas_call(
        body, name=name, out_shape=jax.ShapeDtypeStruct((4,) + parts.shape[1:], parts.dtype),
        in_specs=[pl.BlockSpec(memory_space=pl.ANY)], out_specs=pl.BlockSpec(memory_space=pl.ANY),
        scratch_shapes=[pltpu.SemaphoreType.DMA((4,)), pltpu.SemaphoreType.DMA((4,))],
    )(parts)


def _chips_phases():
    def copies(p_ref, out_ref, send_sems, recv_sems):
        x, y, c, chips = _place()
        return [pltpu.make_async_remote_copy(
            src_ref=p_ref.at[2 * px + py], dst_ref=out_ref.at[j],
            send_sem=send_sems.at[j], recv_sem=recv_sems.at[j],
            device_id=(px, py, c), device_id_type=MESH) for j, (px, py) in enumerate(chips)]

    def start(ins, outs, sems):
        for cp in copies(ins[0], outs[0], *sems):
            cp.start()

    def finish(ins, outs, sems):
        for cp in copies(ins[0], outs[0], *sems):
            cp.wait()

    return start, finish


def _chips_hook(parts):
    start, finish = _chips_phases()
    return _Hook([parts], [jax.ShapeDtypeStruct((3,) + parts.shape[1:], parts.dtype)],
                 [pltpu.SemaphoreType.DMA((3,)), pltpu.SemaphoreType.DMA((3,))], [(0.0, start), (1.0, finish)])


def _pack_rows(n_elem, row_mult=PACK_ROW_MULT):
    rows = -(-n_elem // PACK_COLS)
    return -(-rows // row_mult) * row_mult


def _pack(flats, dtype, row_mult=PACK_ROW_MULT):
    flat = jnp.concatenate([f.reshape(-1).astype(dtype) for f in flats])
    rows = _pack_rows(flat.shape[0], row_mult)
    return jnp.pad(flat, (0, rows * PACK_COLS - flat.shape[0])).reshape(rows, PACK_COLS)


def _unpack(buf, shapes):
    lead = buf.shape[:-2]
    flat = buf.reshape(lead + (-1,))
    outs, o = [], 0
    for s in shapes:
        n = math.prod(s)
        outs.append(flat[..., o:o + n].reshape(lead + tuple(s)))
        o += n
    return outs


BIG_LAYOUT = (("w_in_ab", D_MODEL, PACK_COLS), ("s5_w_glu", S5_W // N_DEV, PACK_COLS),
              ("w_out_ab", OUT_AB // N_DEV, 2 * PACK_COLS), ("w_in_c", D_MODEL, PACK_COLS),
              ("w_out_c", GLA_W // N_DEV, 2 * PACK_COLS))


def _to_rows(a):
    if a.shape[-1] == PACK_COLS:
        return a
    assert a.shape[-1] == 2 * PACK_COLS
    return jnp.concatenate([a[..., :PACK_COLS], a[..., PACK_COLS:]], axis=-2)


def _from_rows(p, cols):
    if cols == PACK_COLS:
        return p
    r = p.shape[-2] // 2
    return jnp.concatenate([p[..., :r, :], p[..., r:, :]], axis=-1)


FIRST_LAYOUT, REST_LAYOUT = BIG_LAYOUT[:1], BIG_LAYOUT[1:]
MID_LAYOUT, LAST_LAYOUT = BIG_LAYOUT[1:3], BIG_LAYOUT[3:]


def _pack_big(pieces, layout):
    return jnp.concatenate([_to_rows(pieces[name]) for name, _, _ in layout], axis=-2)


def _unpack_big(buf, layout):
    out, o = {}, 0
    for name, rows, cols in layout:
        r = rows * cols // PACK_COLS
        out[name] = _from_rows(buf[..., o:o + r, :], cols)
        o += r
    return out


def _rows1024(a):
    r, c = a.shape
    if c > PACK_COLS:
        a = jnp.concatenate([a[:, i * PACK_COLS:(i + 1) * PACK_COLS] for i in range(c // PACK_COLS)], axis=0)
    elif c < PACK_COLS:
        a = jnp.pad(a, ((0, 0), (0, PACK_COLS - c)))
    return jnp.pad(a, ((0, -a.shape[0] % 8), (0, 0)))


def _unrows1024(p, r, c):
    if c > PACK_COLS:
        return jnp.concatenate([p[i * r:(i + 1) * r] for i in range(c // PACK_COLS)], axis=1)
    return p[:r, :c]


def _lane_select(a, off, sign, n_out, out_dtype, exact, name):
    rows, n_in = a.shape
    tr = _tile(rows, 256, 16)

    def body(off_ref, a_ref, o_ref):
        sel = _iota2((n_in, n_out), 0) + off_ref[0] * sign == _iota2((n_in, n_out), 1)
        if exact:
            r = jnp.dot(a_ref[...], sel.astype(F32), precision=lax.Precision.HIGHEST, preferred_element_type=F32)
        else:
            r = _dg(a_ref[...], sel.astype(BF16), 1, 0)
        o_ref[...] = r.astype(out_dtype)

    return pl.pallas_call(
        body, name=name, grid=(rows // tr,),
        in_specs=[pl.BlockSpec(memory_space=pltpu.SMEM), pl.BlockSpec((tr, n_in), lambda i: (i, 0))],
        out_specs=pl.BlockSpec((tr, n_out), lambda i: (i, 0)),
        out_shape=jax.ShapeDtypeStruct((rows, n_out), out_dtype),
        compiler_params=_params(("arbitrary",)),
    )(off, a)


def _adamw(w, g, m, v, name):
    rows, cols = w.shape
    tr = _tile(rows, 256, 8) if rows % 8 == 0 else rows

    def fn(i, w_, g_, m_, v_):
        m_new = ADAM_B1 * m_ + (1.0 - ADAM_B1) * g_
        v_new = ADAM_B2 * v_ + (1.0 - ADAM_B2) * (g_ * g_)
        m_hat = m_new / (1.0 - ADAM_B1 ** ADAM_STEP)
        v_hat = v_new / (1.0 - ADAM_B2 ** ADAM_STEP)
        delta = -ADAM_LR * (m_hat / (jnp.sqrt(v_hat) + ADAM_EPS) + ADAM_WD * w_)
        return (delta, m_new, v_new), ()

    outs, _ = _rows(fn, [_win(w), _win(g), _win(m), _win(v)], [], [(cols, F32)] * 3, [], name=name,
                    nrow=rows, tr=tr)
    return outs


def _as2d(a):
    if a.ndim == 1:
        return a.reshape(1, -1)
    if a.ndim == 2:
        return a
    a = a.reshape(a.shape[1:])
    return a if a.ndim == 2 else a.reshape(a.shape[0], -1)


def kernel(x, meta, norm_ab_w, w_in_ab, ret_norm_w, s5_lam_re, s5_lam_im, s5_log_dt, s5_b_re, s5_b_im, s5_c_re, s5_c_im, s5_d, s5_w_glu, w_out_ab, norm_c_w, w_in_c, gla_w_gate, gla_b_gate, gla_norm_w, w_out_c, final_norm_w, loss_target, m_meta, m_norm_ab_w, m_w_in_ab, m_ret_norm_w, m_s5_lam_re, m_s5_lam_im, m_s5_log_dt, m_s5_b_re, m_s5_b_im, m_s5_c_re, m_s5_c_im, m_s5_d, m_s5_w_glu, m_w_out_ab, m_norm_c_w, m_w_in_c, m_gla_w_gate, m_gla_b_gate, m_gla_norm_w, m_w_out_c, m_final_norm_w, v_meta, v_norm_ab_w, v_w_in_ab, v_ret_norm_w, v_s5_lam_re, v_s5_lam_im, v_s5_log_dt, v_s5_b_re, v_s5_b_im, v_s5_c_re, v_s5_c_im, v_s5_d, v_s5_w_glu, v_w_out_ab, v_norm_c_w, v_w_in_c, v_gla_w_gate, v_gla_b_gate, v_gla_norm_w, v_w_out_c, v_final_norm_w):
    weights = dict(meta=meta, norm_ab_w=norm_ab_w, w_in_ab=w_in_ab, ret_norm_w=ret_norm_w, s5_lam_re=s5_lam_re,
                   s5_lam_im=s5_lam_im, s5_log_dt=s5_log_dt, s5_b_re=s5_b_re, s5_b_im=s5_b_im, s5_c_re=s5_c_re,
                   s5_c_im=s5_c_im, s5_d=s5_d, s5_w_glu=s5_w_glu, w_out_ab=w_out_ab, norm_c_w=norm_c_w,
                   w_in_c=w_in_c, gla_w_gate=gla_w_gate, gla_b_gate=gla_b_gate, gla_norm_w=gla_norm_w,
                   w_out_c=w_out_c, final_norm_w=final_norm_w)
    mom_m = dict(meta=m_meta, norm_ab_w=m_norm_ab_w, w_in_ab=m_w_in_ab, ret_norm_w=m_ret_norm_w,
                 s5_lam_re=m_s5_lam_re, s5_lam_im=m_s5_lam_im, s5_log_dt=m_s5_log_dt, s5_b_re=m_s5_b_re,
                 s5_b_im=m_s5_b_im, s5_c_re=m_s5_c_re, s5_c_im=m_s5_c_im, s5_d=m_s5_d, s5_w_glu=m_s5_w_glu,
                 w_out_ab=m_w_out_ab, norm_c_w=m_norm_c_w, w_in_c=m_w_in_c, gla_w_gate=m_gla_w_gate,
                 gla_b_gate=m_gla_b_gate, gla_norm_w=m_gla_norm_w, w_out_c=m_w_out_c, final_norm_w=m_final_norm_w)
    mom_v = dict(meta=v_meta, norm_ab_w=v_norm_ab_w, w_in_ab=v_w_in_ab, ret_norm_w=v_ret_norm_w,
                 s5_lam_re=v_s5_lam_re, s5_lam_im=v_s5_lam_im, s5_log_dt=v_s5_log_dt, s5_b_re=v_s5_b_re,
                 s5_b_im=v_s5_b_im, s5_c_re=v_s5_c_re, s5_c_im=v_s5_c_im, s5_d=v_s5_d, s5_w_glu=v_s5_w_glu,
                 w_out_ab=v_w_out_ab, norm_c_w=v_norm_c_w, w_in_c=v_w_in_c, gla_w_gate=v_gla_w_gate,
                 gla_b_gate=v_gla_b_gate, gla_norm_w=v_gla_norm_w, w_out_c=v_w_out_c, final_norm_w=v_final_norm_w)
    order = list(weights)

    seq = x.shape[1]
    lp = CHUNK + seq
    nchunk = lp // CHUNK
    dev = 4 * lax.axis_index("x") + 2 * lax.axis_index("y") + lax.axis_index("c")
    core = lax.axis_index("c")
    chip = 2 * lax.axis_index("x") + lax.axis_index("y")

    win_off = jnp.reshape(2 * dev, (1,)).astype(jnp.int32)
    shard_c = jnp.pad(w_in_c[0].astype(BF16), ((0, 0), (0, 896 - SHARD_C)))
    big_shards = dict(w_in_ab=w_in_ab[0].astype(BF16), s5_w_glu=s5_w_glu[0].astype(BF16),
                      w_out_ab=w_out_ab[0].astype(BF16), w_out_c=w_out_c[0].astype(BF16),
                      w_in_c=_lane_select(shard_c, win_off, 1, WIN_COLS, BF16, False, "w_in_c_to_window"))
    def pad_to(a, rows, cols):
        return jnp.pad(a, ((0, rows - a.shape[0]), (0, cols - a.shape[1])))

    shard_w = D_MODEL // N_DEV
    small_pack = jnp.concatenate([meta, pad_to(norm_c_w, 8, shard_w), pad_to(gla_w_gate[0], GLA_RANK, shard_w),
                                  pad_to(gla_b_gate, 8, shard_w), pad_to(gla_norm_w, 8, shard_w)], axis=0)
    w_in_ab_g = _all_gather(big_shards["w_in_ab"], "gather_first")
    mid_hook = _gather_hook(_pack_big(big_shards, MID_LAYOUT))
    last_hook = _gather_hook(_pack_big(big_shards, LAST_LAYOUT))
    gs = _all_gather(small_pack, "gather_small")
    gate_w = GLA_QK // N_DEV
    s_meta, s_norm_c = gs[:, :N_META], gs[:, N_META]
    s_wgate, s_bgate, s_gnorm = gs[:, 24:24 + GLA_RANK, :gate_w], gs[:, 40, :gate_w], gs[:, 48]
    meta_f = s_meta.transpose(1, 0, 2).reshape(N_META, D_MODEL)
    norm_c_f = s_norm_c.reshape(1, D_MODEL)
    w_gate_f = jnp.pad(s_wgate.transpose(1, 0, 2).reshape(GLA_RANK, GLA_QK), ((0, GATE_PAD - GLA_RANK), (0, 0)))
    b_gate_f = s_bgate.reshape(1, GLA_QK)
    gla_norm_f = s_gnorm.reshape(GLA_H, 1, GLA_DV)

    pos = jnp.maximum(jnp.arange(lp, dtype=F32) - float(PAD), 0.0)
    inv_freq = jnp.power(ROPE_BASE, -jnp.arange(0, RET_DK, 2, dtype=F32) / RET_DK)
    ang = pos[:, None] * inv_freq[None, :]
    cos2 = jnp.concatenate([jnp.cos(ang), jnp.cos(ang)], axis=1)
    sin2 = jnp.concatenate([-jnp.sin(ang), jnp.sin(ang)], axis=1)
    log_g = jnp.log1p(-jnp.exp2(-5.0 - jnp.arange(RET_H, dtype=F32)))
    lg = jnp.broadcast_to(log_g[:, None, None], (RET_H, 1, 128))
    ret_norm_h = ret_norm_w.reshape(RET_H, 1, RET_DV)

    h0 = jnp.concatenate([jnp.zeros((PAD, D_MODEL), F32), meta_f, x[0]], axis=0)

    def rowmask(i):
        return (_iota2((CHUNK, 1), 0) + i * CHUNK) >= PAD

    (hn0,), _ = _rows(lambda i, h, w: ((_rms(h, w),), ()), [_win(h0)], [norm_ab_w], [(D_MODEL, BF16)], [],
                      name="norm_ab_fwd", nrow=lp)
    proj_ab, (gathered_mid,) = _mm(hn0, w_in_ab_g, "nn", name="in_ab_fwd", hook=mid_hook, b_dev=True)
    gb = _unpack_big(gathered_mid, MID_LAYOUT)
    w_glu_f = gb["s5_w_glu"].reshape(S5_W, S5_W)
    w_out_ab_f = gb["w_out_ab"].reshape(OUT_AB, D_MODEL)

    q_off, k_off, v_off, za_off = 0, RET_QK, 2 * RET_QK, 2 * RET_QK + RET_W
    u_off, zb_off = 2 * RET_QK + 2 * RET_W, 2 * RET_QK + 2 * RET_W + S5_W
    ret_xs = [(proj_ab, RET_DK, lambda h: q_off // RET_DK + h), (proj_ab, RET_DK, lambda h: k_off // RET_DK + h),
              (proj_ab, RET_DV, lambda h: v_off // RET_DV + h), (proj_ab, RET_DV, lambda h: za_off // RET_DV + h)]
    ret_cs = [(cos2, RET_DK, lambda h: 0), (sin2, RET_DK, lambda h: 0)]
    ret_kw = dict(heads=RET_H, nchunk=nchunk, s_shape=(RET_DK, RET_DV), out_w=RET_DV, pre=_ret_pre)
    o_a, ret_sprev, (gathered_last,) = _scan_fwd(_ret_chunk, ret_xs, ret_cs, [ret_norm_h], [lg], name="ret_fwd",
                                                 hook=last_hook, **ret_kw)
    gb = _unpack_big(gathered_last, LAST_LAYOUT)
    w_in_c_f = sum(jnp.pad(gb["w_in_c"][d], ((0, 0), (WIN_STEP * d, IN_C_PAD - WIN_STEP * d - WIN_COLS)))
                   for d in range(N_DEV))
    w_out_c_f = gb["w_out_c"].reshape(GLA_W, D_MODEL)

    expand = jnp.repeat(jnp.eye(S5_P, dtype=F32), S5_GH, axis=1)
    disc_args = (s5_lam_re[0], s5_lam_im[0], s5_log_dt[0].reshape(S5_G, 1),
                 s5_b_re[0].reshape(S5_G, S5_P * S5_GH), s5_b_im[0].reshape(S5_G, S5_P * S5_GH), expand)
    ab_re, ab_im, bb_re, bb_im = _s5_disc_fwd(disc_args)
    gt = S5_SUBL
    eye_t = jnp.eye(gt, dtype=F32)

    def tiles_in(bb):
        return jnp.einsum("sgph,gk->sghkp", bb.reshape(gt, gt, S5_P, S5_GH), eye_t).reshape(gt, 128, S5_LANES)

    def tiles_out(cc):
        return jnp.einsum("sghp,gk->sgpkh", cc.reshape(gt, gt, S5_GH, S5_P), eye_t).reshape(gt, S5_LANES, 128)

    wb_t = jnp.concatenate([tiles_in(bb_re), tiles_in(bb_im)], axis=0).astype(BF16)
    wc_t = jnp.concatenate([tiles_out(s5_c_re[0]), -tiles_out(s5_c_im[0])], axis=0).astype(BF16)
    a_re, a_im = ab_re.reshape(S5_SUBL, S5_LANES), ab_im.reshape(S5_SUBL, S5_LANES)
    tm5, tk5, nt5 = _tile(lp, 1408, 8), _tile(lp, 1408, 8), 2 * gt
    u_blk = u_off // 128
    wide = pl.BlockSpec((tm5, S5_LANES), lambda i, j, k: (i, j))
    wide_k = pl.BlockSpec((tm5, S5_LANES), lambda i, j, k: (i, k * gt + j))
    narrow = pl.BlockSpec((tm5, 128), lambda i, j, k: (i, j))
    wb_j = pl.BlockSpec((None, 128, S5_LANES), lambda i, j, k: (j, 0, 0))
    wc_j = pl.BlockSpec((None, S5_LANES, 128), lambda i, j, k: (j, 0, 0))
    wb_k = pl.BlockSpec((None, 128, S5_LANES), lambda i, j, k: (k * gt + j, 0, 0))
    wc_k = pl.BlockSpec((None, S5_LANES, 128), lambda i, j, k: (k * gt + j, 0, 0))
    wide_shape = jax.ShapeDtypeStruct((lp, 2 * S5_N), F32)
    bu = _mm_core(proj_ab, wb_t, dims=NN, grid=(lp // tm5, nt5, 1), name="s5_bu",
                  a_spec=pl.BlockSpec((tm5, 128), lambda i, j, k: (i, u_blk + j % gt)), b_spec=wb_j,
                  o_spec=wide, out_shape=wide_shape, acc_shape=(tm5, S5_LANES))
    xs5 = _s5_scan_fwd(bu.reshape(lp, 2 * S5_SUBL, S5_LANES), a_re, a_im)
    xs5_2d = xs5.reshape(lp, 2 * S5_N)
    y_pre = _mm_core(xs5_2d, wc_t, dims=NN, grid=(lp // tm5, gt, 2), name="s5_cx", a_spec=wide_k, b_spec=wc_k,
                     o_spec=narrow, out_shape=jax.ShapeDtypeStruct((lp, S5_W), F32), acc_shape=(tm5, 128))
    (y_s5, yg_bf), _ = _rows(
        lambda i, yp, u, d: ((yp + d * u, _gelu(yp + d * u)), ()),
        [_win(y_pre), _win(proj_ab, u_off, S5_W)], [s5_d], [(S5_W, F32), (S5_W, BF16)], [], name="s5_gelu_fwd", nrow=lp)
    t_glu = _mm(yg_bf, w_glu_f, "nn", name="s5_glu_fwd")

    def s5_gate(y, t, zb):
        return _gelu(y) * _sigmoid(t) * _silu(zb)

    (o_b,), _ = _rows(lambda i, y, t, zb: ((s5_gate(y, t, zb),), ()),
                      [_win(y_s5), _win(t_glu), _win(proj_ab, zb_off, S5_W)], [], [(S5_W, BF16)], [],
                      name="s5_gate_fwd", nrow=lp)
    o_ab = jnp.concatenate([o_a, o_b], axis=1)
    h1 = _mm(o_ab, w_out_ab_f, "nn", name="out_ab_fwd", add=h0)

    (hn1,), _ = _rows(lambda i, h, w: ((_rms(h, w),), ()), [_win(h1)], [norm_c_f], [(D_MODEL, BF16)], [],
                      name="norm_c_fwd", nrow=lp)
    proj_c = _mm(hn1, w_in_c_f, "nn", name="in_c_fwd")
    gl_off = 2 * GLA_QK + 2 * GLA_W
    pre_gate = _mm(proj_c, w_gate_f, "nn", name="gate_fwd", a_win=(gl_off, GATE_PAD), bias=b_gate_f)
    gla_xs = [(proj_c, GLA_DK, lambda h: h), (proj_c, GLA_DK, lambda h: GLA_QK // GLA_DK + h),
              (proj_c, GLA_DV, lambda h: 2 * GLA_QK // GLA_DV + h),
              (proj_c, GLA_DV, lambda h: (2 * GLA_QK + GLA_W) // GLA_DV + h),
              (pre_gate, GLA_DK, lambda h: h)]
    gla_kw = dict(heads=GLA_H, nchunk=nchunk, s_shape=(GLA_DV, GLA_DK), out_w=GLA_DV)
    o_c, gla_sprev = _scan_fwd(_gla_chunk, gla_xs, [], [gla_norm_f], [], name="gla_fwd", **gla_kw)
    h2 = _mm(o_c, w_out_c_f, "nn", name="out_c_fwd", add=h1)

    fnw = final_norm_w.reshape(1, D_MODEL)

    def final_fn(i, h, tgt, w):
        def loss_of(h_, w_):
            err = _rms(h_, w_) - tgt
            return 0.5 * jnp.sum(jnp.mean(err * err, axis=-1))

        real = (i > 0).astype(F32)
        loss_i, (dh, dw) = jax.value_and_grad(loss_of, argnums=(0, 1))(h, w)
        return (dh * real,), (jnp.full((1, 128), loss_i * real, F32), dw * real)

    (dh2,), (loss_acc, g_final) = _rows(final_fn, [_win(h2), _win(loss_target[0], roff=1)], [fnw],
                                        [(D_MODEL, F32)], [(1, 128), (1, D_MODEL)], name="final_loss", nrow=lp)

    def rs_front(pieces, layout, tag):
        g_full = _pack_big(pieces, layout)
        prow = g_full.shape[1]
        from_sibling = _swap_with_sibling(g_full.astype(BF16), "rs_sibling_" + tag)
        mine_by_chip = lax.dynamic_index_in_dim(g_full.reshape(4, 2, prow, PACK_COLS), core, axis=1, keepdims=False)
        (p1, p1_bf), _ = _rows(
            lambda i, a, b: ((a + b.astype(F32), a + b.astype(F32)), ()),
            [_win(mine_by_chip.reshape(4 * prow, PACK_COLS)), _win(from_sibling.reshape(4 * prow, PACK_COLS))], [],
            [(PACK_COLS, F32), (PACK_COLS, BF16)], [], name="rs_sum_sibling_" + tag, nrow=4 * prow,
            tr=_tile(prow, 512, 16))
        return p1.reshape(4, prow, PACK_COLS), p1_bf.reshape(4, prow, PACK_COLS)

    def rs_back(p1, from_chips, layout, tag):
        prow = p1.shape[1]
        tr = _tile(prow, 512, 16)
        own = lax.dynamic_index_in_dim(p1, chip, axis=0, keepdims=False)
        fc2 = from_chips.reshape(3 * prow, PACK_COLS)
        nblk = prow // tr
        (g_shard,), _ = _rows(
            lambda i, a, b0, b1, b2: ((((a + b0.astype(F32)) + b1.astype(F32)) + b2.astype(F32),), ()),
            [_win(own), _win(fc2), _win(fc2, roff=-nblk), _win(fc2, roff=-2 * nblk)], [], [(PACK_COLS, F32)], [],
            name="rs_sum_chips_" + tag, nrow=prow, tr=tr)
        return _unpack_big(g_shard, layout)

    dh2_bf = dh2.astype(BF16)
    do_c = _mm(dh2_bf, w_out_c_f, "nt", name="out_c_dx", out_dtype=BF16)
    gw_out_c = _mm(o_c, dh2_bf, "tn", name="out_c_dw")
    (dq_c, dk_c, dv_c, dz_c, dpre), (g_gla_norm,) = _scan_bwd(
        _gla_chunk, gla_xs, [], [gla_norm_f], [], do_c, gla_sprev, name="gla_bwd", **gla_kw)
    dglow = _mm(dpre, w_gate_f, "nt", name="gate_dx", out_dtype=BF16)
    g_wgate = _mm(proj_c, dpre, "tn", name="gate_dw", a_win=(gl_off, GATE_PAD))[:GLA_RANK]
    (), (g_bgate,) = _rows(lambda i, d: ((), (jnp.sum(d.astype(F32), axis=0, keepdims=True),)), [_win(dpre)], [], [],
                           [(1, GLA_QK)], name="gate_db", nrow=lp)
    dproj_c = jnp.concatenate([dq_c, dk_c, dv_c, dz_c, dglow], axis=1)
    dhn1 = _mm(dproj_c, w_in_c_f, "nt", name="in_c_dx")
    gw_in_c = _mm(hn1, dproj_c, "tn", name="in_c_dw")

    def norm_bwd(i, h, dhn, dres, w):
        _, vjp = jax.vjp(_rms, h, w)
        dh, dw = vjp(dhn)
        return (jnp.where(rowmask(i), dh + dres, 0.0),), (dw,)

    (dh1,), (g_norm_c,) = _rows(norm_bwd, [_win(h1), _win(dhn1), _win(dh2)], [norm_c_f], [(D_MODEL, F32)],
                                [(1, D_MODEL)], name="norm_c_bwd", nrow=lp)

    dh1_bf = dh1.astype(BF16)
    do_ab = _mm(dh1_bf, w_out_ab_f, "nt", name="out_ab_dx", out_dtype=BF16)
    gw_out_ab = _mm(o_ab, dh1_bf, "tn", name="out_ab_dw")

    def s5_gate_bwd(i, dob, y, t, zb):
        _, vjp = jax.vjp(s5_gate, y, t, zb)
        dy, dt, dzb = vjp(dob.astype(F32))
        return (dy, dt, dzb), ()

    (dy_a, dt_glu, dzb), _ = _rows(
        s5_gate_bwd, [_win(do_ab, RET_W, S5_W), _win(y_s5), _win(t_glu), _win(proj_ab, zb_off, S5_W)], [],
        [(S5_W, F32), (S5_W, BF16), (S5_W, BF16)], [], name="s5_gate_bwd", nrow=lp)
    dyg2 = _mm(dt_glu, w_glu_f, "nt", name="s5_glu_dx")
    gw_glu = _mm(yg_bf, dt_glu, "tn", name="s5_glu_dw")

    def s5_y_bwd(i, dya, dyg, y, u, d):
        _, vjp = jax.vjp(_gelu, y)
        (dy_g,) = vjp(dyg)
        dy = dya + dy_g
        return (dy, d * dy), (jnp.sum(dy * u, axis=0, keepdims=True),)

    (dy_s5, du1), (g_d,) = _rows(
        s5_y_bwd, [_win(dy_a), _win(dyg2), _win(y_s5), _win(proj_ab, u_off, S5_W)], [s5_d],
        [(S5_W, BF16), (S5_W, F32)], [(1, S5_W)], name="s5_y_bwd", nrow=lp)
    gx = _mm_core(dy_s5, wc_t, dims=NT, grid=(lp // tm5, nt5, 1), name="s5_cx_dx",
                  a_spec=pl.BlockSpec((tm5, 128), lambda i, j, k: (i, j % gt)), b_spec=wc_j,
                  o_spec=wide, out_shape=wide_shape, acc_shape=(tm5, S5_LANES))
    rows_k = lambda col: pl.BlockSpec((tk5, col), lambda i, j, k: (k, i))
    gwc = _mm_core(xs5_2d, dy_s5, dims=TN, grid=(nt5, 1, lp // tk5), name="s5_cx_dw", a_spec=rows_k(S5_LANES),
                   b_spec=pl.BlockSpec((tk5, 128), lambda i, j, k: (k, i % gt)),
                   o_spec=pl.BlockSpec((None, S5_LANES, 128), lambda i, j, k: (i, 0, 0)),
                   out_shape=jax.ShapeDtypeStruct((nt5, S5_LANES, 128), F32), acc_shape=(S5_LANES, 128))
    g_s5, da = _s5_scan_bwd(gx.reshape(lp, 2 * S5_SUBL, S5_LANES), xs5, a_re, a_im)
    g_s5_2d = g_s5.reshape(lp, 2 * S5_N)
    du = _mm_core(g_s5_2d, wb_t, dims=NT, grid=(lp // tm5, gt, 2), name="s5_bu_dx", a_spec=wide_k, b_spec=wb_k,
                  o_spec=narrow, out_shape=jax.ShapeDtypeStruct((lp, S5_W), BF16), acc_shape=(tm5, 128),
                  extra=[(du1, narrow)])
    gwb = _mm_core(proj_ab, g_s5_2d, dims=TN, grid=(nt5, 1, lp // tk5), name="s5_bu_dw",
                   a_spec=pl.BlockSpec((tk5, 128), lambda i, j, k: (k, u_blk + i % gt)), b_spec=rows_k(S5_LANES),
                   o_spec=pl.BlockSpec((None, 128, S5_LANES), lambda i, j, k: (i, 0, 0)),
                   out_shape=jax.ShapeDtypeStruct((nt5, 128, S5_LANES), F32), acc_shape=(128, S5_LANES))
    gwc6 = gwc.reshape(2, gt, gt, S5_P, gt, S5_GH)
    g_c = jnp.einsum("rsgpgh->rsghp", gwc6).reshape(2, S5_G, S5_GH, S5_P)
    g_c_re, g_c_im = g_c[0], -g_c[1]
    gwb6 = gwb.reshape(2, gt, gt, S5_GH, gt, S5_P)
    d_bb = jnp.einsum("rsghgp->rsgph", gwb6).reshape(2, S5_G, S5_P * S5_GH)
    d_bb_re, d_bb_im = d_bb[0], d_bb[1]
    g_lam_re, g_lam_im, g_log_dt, g_b_re, g_b_im = _s5_disc_bwd(
        disc_args, (da[0].reshape(S5_G, S5_P), da[1].reshape(S5_G, S5_P), d_bb_re, d_bb_im))

    p1_rest, p1_rest_bf = rs_front(dict(
        s5_w_glu=gw_glu.reshape(N_DEV, S5_W // N_DEV, S5_W),
        w_out_ab=gw_out_ab.reshape(N_DEV, OUT_AB // N_DEV, D_MODEL),
        w_in_c=jnp.stack([gw_in_c[:, WIN_STEP * d:WIN_STEP * d + WIN_COLS] for d in range(N_DEV)]),
        w_out_c=gw_out_c.reshape(N_DEV, GLA_W // N_DEV, D_MODEL)), REST_LAYOUT, "rest")
    (dq_a, dk_a, dv_a, dz_a), (g_ret_norm,), (from_chips_rest,) = _scan_bwd(
        _ret_chunk, ret_xs, ret_cs, [ret_norm_h], [lg], do_ab, ret_sprev, name="ret_bwd", post=_ret_post,
        hook=_chips_hook(p1_rest_bf), **ret_kw)
    dproj_ab = jnp.concatenate([dq_a, dk_a, dv_a, dz_a, du, dzb], axis=1)
    gw_in_ab = _mm(hn0, dproj_ab, "tn", name="in_ab_dw", out_dest=True)
    p1_first, p1_first_bf = rs_front(dict(w_in_ab=gw_in_ab), FIRST_LAYOUT, "first")
    dhn0, (from_chips_first,) = _mm(dproj_ab, w_in_ab_g, "nt", name="in_ab_dx", hook=_chips_hook(p1_first_bf),
                                    b_dev=True)
    (dh0,), (g_norm_ab,) = _rows(norm_bwd, [_win(h0), _win(dhn0), _win(dh1)], [norm_ab_w], [(D_MODEL, F32)],
                                 [(1, D_MODEL)], name="norm_ab_bwd", nrow=lp)
    grad_x = dh0[CHUNK:][None]
    g_meta_part = dh0[PAD:CHUNK]

    big_grads = {**rs_back(p1_rest, from_chips_rest, REST_LAYOUT, "rest"),
                 **rs_back(p1_first, from_chips_first, FIRST_LAYOUT, "first")}
    big_grads["w_in_c"] = _lane_select(big_grads["w_in_c"], win_off, -1, 896, F32, True,
                                       "w_in_c_from_window")[:, :SHARD_C]

    lane = lambda a_: pad_to(a_, a_.shape[0], 128)
    small_pieces = [
        ("vec2048", jnp.concatenate([g_norm_ab, g_final, g_norm_c], axis=0)),
        ("vec1024", jnp.concatenate([g_d, g_bgate, pad_to(loss_acc[:, :1], 1, PACK_COLS)], axis=0)),
        ("lam3", jnp.concatenate([lane(g_lam_re), lane(g_lam_im), lane(g_log_dt)], axis=1)),
        ("s5_b_re", g_b_re), ("s5_b_im", g_b_im),
        ("s5_c_re", g_c_re.reshape(S5_G, S5_GH * S5_P)), ("s5_c_im", g_c_im.reshape(S5_G, S5_GH * S5_P)),
        ("ret_norm_w", g_ret_norm.reshape(RET_H, RET_DV)), ("gla_norm_w", g_gla_norm.reshape(GLA_H, GLA_DV)),
        ("meta", g_meta_part), ("gla_w_gate", g_wgate)]
    sp_pack = jnp.concatenate([_rows1024(p) for _, p in small_pieces], axis=0)
    srow = sp_pack.shape[0]
    small_tr = _tile(srow, 128, 8)
    sp_all = _all_gather(sp_pack, "gather_grads").reshape(N_DEV * srow, PACK_COLS)
    snb = srow // small_tr

    def sum8(i, *blocks):
        acc = blocks[0]
        for b in blocks[1:]:
            acc = acc + b
        return (acc,), ()

    (sp_sum,), _ = _rows(sum8, [_win(sp_all, roff=-d * snb) for d in range(N_DEV)], [], [(PACK_COLS, F32)], [],
                         name="sum_small", nrow=srow, tr=small_tr)
    small, o = {}, 0
    for name_, p in small_pieces:
        r8 = _rows1024(p).shape[0]
        small[name_] = _unrows1024(sp_sum[o:o + r8], *p.shape)
        o += r8
    small["norm_ab_w"], small["final_norm_w"], small["norm_c_w"] = (small["vec2048"][i:i + 1] for i in range(3))
    small["s5_d"], small["gla_b_gate"] = small["vec1024"][0:1], small["vec1024"][1:2]
    loss = small["vec1024"][2, 0]
    small["s5_lam_re"], small["s5_lam_im"] = small["lam3"][:, :S5_P], small["lam3"][:, 128:128 + S5_P]
    small["s5_log_dt"] = small["lam3"][:, 256:257]

    def my_cols(g, n):
        return lax.dynamic_slice_in_dim(g, dev * n, n, axis=g.ndim - 1)

    grads = dict(
        meta=my_cols(small["meta"], D_MODEL // N_DEV),
        norm_ab_w=small["norm_ab_w"], w_in_ab=big_grads["w_in_ab"][None], ret_norm_w=small["ret_norm_w"].reshape(1, RET_W),
        s5_lam_re=small["s5_lam_re"][None], s5_lam_im=small["s5_lam_im"][None],
        s5_log_dt=small["s5_log_dt"].reshape(1, S5_G),
        s5_b_re=small["s5_b_re"].reshape(1, S5_G, S5_P, S5_GH), s5_b_im=small["s5_b_im"].reshape(1, S5_G, S5_P, S5_GH),
        s5_c_re=small["s5_c_re"][None], s5_c_im=small["s5_c_im"][None], s5_d=small["s5_d"],
        s5_w_glu=big_grads["s5_w_glu"][None], w_out_ab=big_grads["w_out_ab"][None],
        norm_c_w=my_cols(small["norm_c_w"], D_MODEL // N_DEV), w_in_c=big_grads["w_in_c"][None],
        gla_w_gate=my_cols(small["gla_w_gate"], GLA_QK // N_DEV)[None],
        gla_b_gate=my_cols(small["gla_b_gate"], GLA_QK // N_DEV),
        gla_norm_w=my_cols(small["gla_norm_w"].reshape(1, GLA_W), GLA_W // N_DEV),
        w_out_c=big_grads["w_out_c"][None], final_norm_w=small["final_norm_w"].reshape(D_MODEL))

    deltas, new_m, new_v = {}, {}, {}
    for k in order:
        w = weights[k]
        d2, m2, v2 = _adamw(_as2d(w), _as2d(grads[k].reshape(w.shape)), _as2d(mom_m[k]), _as2d(mom_v[k]), "adamw_" + k)
        deltas[k], new_m[k], new_v[k] = d2.reshape(w.shape), m2.reshape(w.shape), v2.reshape(w.shape)
        grads[k] = grads[k].reshape(w.shape)

    return (loss, grad_x, *[grads[k] for k in order], *[deltas[k] for k in order],
            *[new_m[k] for k in order], *[new_v[k] for k in order])
```

```python
import functools
import math

import jax
import jax.numpy as jnp
from jax import lax
from jax.experimental import pallas as pl
from jax.experimental.pallas import tpu as pltpu

F32, BF16 = jnp.float32, jnp.bfloat16
MESH = pl.DeviceIdType.MESH
N_DEV = 8

D_MODEL = 2048
CHUNK = 128
N_META = 16
PAD = CHUNK - N_META
SUB = 16
EPS = 1e-6
RET_H, RET_DK, RET_DV = 8, 128, 256
RET_QK, RET_W = RET_H * RET_DK, RET_H * RET_DV
ROPE_BASE = 10000.0
S5_W, S5_G, S5_P, S5_GH = 1024, 64, 64, 16
S5_N = S5_G * S5_P
GLA_H, GLA_DK, GLA_DV, GLA_RANK, GLA_TAU = 4, 256, 512, 16, 16.0
GLA_QK, GLA_W = GLA_H * GLA_DK, GLA_H * GLA_DV
IN_AB = 2 * RET_QK + 2 * RET_W + 2 * S5_W
OUT_AB = RET_W + S5_W
IN_C = 2 * GLA_QK + 2 * GLA_W + GLA_RANK
GATE_PAD = 256
IN_C_PAD = 2 * GLA_QK + 2 * GLA_W + GATE_PAD
ADAM_LR, ADAM_B1, ADAM_B2, ADAM_EPS, ADAM_WD, ADAM_STEP = 0.001, 0.9, 0.999, 1e-08, 0.01, 10

VMEM_LIMIT_BYTES = 48 * 2 ** 20
PACK_COLS = 1024
PACK_ROW_MULT = 8
SHARD_C = IN_C // N_DEV
WIN_STEP = 768
WIN_COLS = 1024


def _params(sem):
    return pltpu.CompilerParams(dimension_semantics=sem, vmem_limit_bytes=VMEM_LIMIT_BYTES)


def _tile(n, cap, mult):
    best = None
    for t in range(mult, min(n, cap) + 1, mult):
        if n % t == 0:
            best = t
    assert best is not None, (n, cap, mult)
    return best


def _dg(a, b, ca, cb):
    return lax.dot_general(a.astype(BF16), b.astype(BF16), (((ca,), (cb,)), ((), ())),
                           preferred_element_type=F32)


@functools.partial(jax.custom_vjp, nondiff_argnums=(2, 3))
def _bdot(a, b, ca, cb):
    return _dg(a, b, ca, cb)


def _bdot_fwd(a, b, ca, cb):
    return _dg(a, b, ca, cb), (a, b)


def _bdot_bwd(ca, cb, res, g):
    a, b = res
    da = _dg(g, b, 1, 1 - cb) if ca == 1 else _dg(b, g, 1 - cb, 1)
    db = _dg(a, g, 1 - ca, 0) if cb == 0 else _dg(g, a, 0, 1 - ca)
    return da.astype(a.dtype), db.astype(b.dtype)


_bdot.defvjp(_bdot_fwd, _bdot_bwd)


def _sigmoid(x):
    return 1.0 / (1.0 + jnp.exp(-x))


def _silu(x):
    return x * _sigmoid(x)


def _log_sigmoid(x):
    return jnp.minimum(x, 0.0) - jnp.log(1.0 + jnp.exp(-jnp.abs(x)))


def _gelu(x):
    return 0.5 * x * (1.0 + jnp.tanh(math.sqrt(2.0 / math.pi) * (x + 0.044715 * (x * x * x))))


def _rms(x, w):
    return x * lax.rsqrt(jnp.mean(x * x, axis=-1, keepdims=True) + EPS) * w


class _Hook:
    def __init__(self, ins, outs, sems, phases):
        self.ins, self.outs, self.sems, self.phases = list(ins), list(outs), list(sems), list(phases)


_NO_HOOK = _Hook([], [], [], [])
_ANY = pl.BlockSpec(memory_space=pl.ANY)


def _run_hook(hook, lin, total, in_refs, out_refs, sem_refs):
    for frac, fn in hook.phases:
        at = min(int(frac * total), total - 1)

        @pl.when(lin == at)
        def _(fn=fn):
            fn(in_refs, out_refs, sem_refs)


def _mm_core(a, b, *, dims, grid, a_spec, b_spec, o_spec, out_shape, acc_shape, name, extra=(), hook=None):
    nk = grid[2]
    n_extra = len(extra)
    hook = _NO_HOOK if hook is None else hook
    hi, ho = len(hook.ins), len(hook.outs)

    def body(*refs):
        a_ref, b_ref = refs[0], refs[1]
        o_ref, acc = refs[2 + n_extra + hi], refs[3 + n_extra + hi + ho]
        k = pl.program_id(2)
        lin = (pl.program_id(0) * grid[1] + pl.program_id(1)) * nk + k
        _run_hook(hook, lin, grid[0] * grid[1] * nk, refs[2 + n_extra:2 + n_extra + hi],
                  refs[3 + n_extra + hi:3 + n_extra + hi + ho], refs[4 + n_extra + hi + ho:])

        part = lax.dot_general(a_ref[...].astype(BF16), b_ref[...].astype(BF16), dims, preferred_element_type=F32)

        def finish(r):
            for e in range(n_extra):
                r = r + refs[2 + e][...].astype(F32)
            o_ref[...] = r.astype(o_ref.dtype)

        if nk == 1:
            finish(part)
        else:
            @pl.when(k == 0)
            def _():
                acc[...] = part

            @pl.when(k > 0)
            def _():
                acc[...] += part

            @pl.when(k == nk - 1)
            def _():
                finish(acc[...])

    res = pl.pallas_call(
        body, name=name, grid=grid,
        in_specs=[a_spec, b_spec] + [sp for _, sp in extra] + [_ANY] * hi,
        out_specs=[o_spec] + [_ANY] * ho, out_shape=[out_shape] + hook.outs,
        scratch_shapes=[pltpu.VMEM(acc_shape if nk > 1 else (8, 128), F32)] + hook.sems,
        compiler_params=_params(("arbitrary", "arbitrary", "arbitrary")),
    )(a, b, *[arr for arr, _ in extra], *hook.ins)
    return res[0] if hook is _NO_HOOK else (res[0], res[1:])


NN, NT, TN = (((1,), (0,)), ((), ())), (((1,), (1,)), ((), ())), (((0,), (0,)), ((), ()))


FULL_K = 2048


def _mm(a, b, mode, *, name, out_dtype=F32, a_win=None, add=None, bias=None, hook=None, b_dev=False,
        out_dest=False):
    if mode == "tn":
        kdim, n = a.shape[0], b.shape[1]
        m = a.shape[1] if a_win is None else a_win[1]
        tm, tn, tk = _tile(m, 1024, 128), _tile(n, 1024, 128), _tile(kdim, 1408, 8)
        off = 0 if a_win is None else a_win[0] // tm
        a_spec = pl.BlockSpec((tk, tm), lambda i, j, k: (k, i + off))
        b_spec = pl.BlockSpec((tk, tn), lambda i, j, k: (k, j))
        dims = TN
    else:
        m = a.shape[0]
        kdim = a.shape[1] if a_win is None else a_win[1]
        if b_dev:
            n = b.shape[0] * b.shape[2] if mode == "nn" else b.shape[1]
        else:
            n = b.shape[1] if mode == "nn" else b.shape[0]
        tm = _tile(m, 1408, 8)
        if kdim <= FULL_K:
            tn, tk = _tile(n, 640, 128), kdim
        else:
            tn, tk = _tile(n, 1024, 128), _tile(kdim, 1024, 128)
        off = 0 if a_win is None else a_win[0] // tk
        a_spec = pl.BlockSpec((tm, tk), lambda i, j, k: (i, k + off))
        if mode == "nn":
            dims = NN
            if b_dev:
                per = PACK_COLS // tn
                b_spec = pl.BlockSpec((None, tk, tn), lambda i, j, k: (j // per, k, j % per))
            else:
                b_spec = pl.BlockSpec((tk, tn), lambda i, j, k: (k, j))
        else:
            dims = NT
            if b_dev:
                per = PACK_COLS // tk
                b_spec = pl.BlockSpec((None, tn, tk), lambda i, j, k: (k // per, j, k % per))
            else:
                b_spec = pl.BlockSpec((tn, tk), lambda i, j, k: (j, k))
    if a_win is not None:
        assert a_win[0] % (tm if mode == "tn" else tk) == 0
    extra = []
    if add is not None:
        extra.append((add, pl.BlockSpec((tm, tn), lambda i, j, k: (i, j))))
    if bias is not None:
        extra.append((bias, pl.BlockSpec((1, tn), lambda i, j, k: (0, j))))
    if out_dest:
        per = PACK_COLS // tn
        o_spec = pl.BlockSpec((None, tm, tn), lambda i, j, k: (j // per, i, j % per))
        out_shape = jax.ShapeDtypeStruct((n // PACK_COLS, m, PACK_COLS), out_dtype)
    else:
        o_spec = pl.BlockSpec((tm, tn), lambda i, j, k: (i, j))
        out_shape = jax.ShapeDtypeStruct((m, n), out_dtype)
    return _mm_core(a, b, dims=dims, grid=(m // tm, n // tn, kdim // tk), a_spec=a_spec, b_spec=b_spec,
                    o_spec=o_spec, out_shape=out_shape, acc_shape=(tm, tn), name=name, extra=extra, hook=hook)


def _win(arr, col0=0, width=None, roff=0):
    return (arr, col0, arr.shape[1] if width is None else width, roff)


def _rows(fn, rows, consts, outs, accs, *, name, nrow, tr=CHUNK):
    nr, nc, no = len(rows), len(consts), len(outs)

    def body(*refs):
        i = pl.program_id(0)
        ins = [r[...] for r in refs[:nr + nc]]
        o_refs = refs[nr + nc:nr + nc + no]
        a_refs = refs[nr + nc + no:]
        res_o, res_a = fn(i, *ins)
        for r, v in zip(o_refs, res_o):
            r[...] = v.astype(r.dtype)
        if a_refs:
            @pl.when(i == 0)
            def _():
                for r in a_refs:
                    r[...] = jnp.zeros_like(r)

            for r, v in zip(a_refs, res_a):
                r[...] += v

    in_specs = []
    for (arr, col0, width, roff) in rows:
        assert col0 % width == 0 and arr.shape[0] % tr == 0
        in_specs.append(pl.BlockSpec((tr, width), lambda i, c=col0 // width, ro=roff: (jnp.maximum(i - ro, 0), c)))
    for c in consts:
        in_specs.append(pl.BlockSpec(c.shape, lambda i, nd=c.ndim: (0,) * nd))
    out_specs = [pl.BlockSpec((tr, w), lambda i: (i, 0)) for (w, _) in outs]
    out_specs += [pl.BlockSpec(s, lambda i, nd=len(s): (0,) * nd) for s in accs]
    out_shape = [jax.ShapeDtypeStruct((nrow, w), dt) for (w, dt) in outs]
    out_shape += [jax.ShapeDtypeStruct(s, F32) for s in accs]
    res = pl.pallas_call(
        body, name=name, grid=(nrow // tr,), in_specs=in_specs, out_specs=out_specs, out_shape=out_shape,
        compiler_params=_params(("arbitrary",)),
    )(*[r[0] for r in rows], *consts)
    return res[:no], res[no:]


HEADS_PER_STEP = 2


def _scan_specs(xs, cs, ws, ks, chunk_of, hpb):
    specs = []
    for (arr, width, colfn) in xs:
        specs.append(pl.BlockSpec((CHUNK, width * hpb), lambda h, n, f=colfn: (chunk_of(n), f(h * hpb) // hpb)))
    for (arr, width, colfn) in cs:
        specs.append(pl.BlockSpec((CHUNK, width), lambda h, n, f=colfn: (chunk_of(n), f(h))))
    for arr in list(ws) + list(ks):
        specs.append(pl.BlockSpec((hpb, 1, arr.shape[2]), lambda h, n: (h, 0, 0)))
    return specs


def _scan_fwd(fn, xs, cs, ws, ks, *, heads, nchunk, s_shape, out_w, name, pre=None, hook=None):
    nx, ncs, nw = len(xs), len(cs), len(ws)
    hook = _NO_HOOK if hook is None else hook
    hi, ho = len(hook.ins), len(hook.outs)
    hpb = HEADS_PER_STEP
    hblocks = heads // hpb

    def body(*refs):
        n = pl.program_id(1)
        nin = nx + ncs + nw + len(ks)
        y_ref, sp_ref = refs[nin + hi], refs[nin + hi + 1]
        s_scr = refs[nin + hi + 2 + ho]
        _run_hook(hook, pl.program_id(0) * nchunk + n, hblocks * nchunk, refs[nin:nin + hi],
                  refs[nin + hi + 2:nin + hi + 2 + ho], refs[nin + hi + 3 + ho:])

        @pl.when(n == 0)
        def _():
            s_scr[...] = jnp.zeros_like(s_scr)

        cv = [r[...] for r in refs[nx:nx + ncs]]
        for e in range(hpb):
            state = s_scr[e]
            sp_ref[e, 0] = state
            xv = [r[:, e * w:(e + 1) * w] for r, (_, w, _) in zip(refs[:nx], xs)]
            wv = [r[e] for r in refs[nx + ncs:nx + ncs + nw]]
            kv = [r[e] for r in refs[nx + ncs + nw:nin]]
            if pre is not None:
                xv = pre(xv, cv)
            y, s_new = fn(n, xv, state, cv, wv, kv)
            y_ref[:, e * out_w:(e + 1) * out_w] = y.astype(y_ref.dtype)
            s_scr[e] = s_new

    lp = nchunk * CHUNK
    res = pl.pallas_call(
        body, name=name, grid=(hblocks, nchunk),
        in_specs=_scan_specs(xs, cs, ws, ks, lambda n: n, hpb) + [_ANY] * hi,
        out_specs=[pl.BlockSpec((CHUNK, out_w * hpb), lambda h, n: (n, h)),
                   pl.BlockSpec((hpb, 1) + s_shape, lambda h, n: (h, n, 0, 0))] + [_ANY] * ho,
        out_shape=[jax.ShapeDtypeStruct((lp, heads * out_w), BF16),
                   jax.ShapeDtypeStruct((heads, nchunk) + s_shape, F32)] + hook.outs,
        scratch_shapes=[pltpu.VMEM((hpb,) + s_shape, F32)] + hook.sems,
        compiler_params=_params(("arbitrary", "arbitrary")),
    )(*[t[0] for t in xs], *[t[0] for t in cs], *ws, *ks, *hook.ins)
    return (res[0], res[1]) if hook is _NO_HOOK else (res[0], res[1], res[2:])


def _scan_bwd(fn, xs, cs, ws, ks, dy, sprev, *, heads, nchunk, s_shape, out_w, name, pre=None, post=None,
              hook=None):
    nx, ncs, nw = len(xs), len(cs), len(ws)
    nin = nx + ncs + nw + len(ks)
    hook = _NO_HOOK if hook is None else hook
    hi, ho = len(hook.ins), len(hook.outs)
    hpb = HEADS_PER_STEP
    hblocks = heads // hpb

    def body(*refs):
        step = pl.program_id(1)
        n = nchunk - 1 - step
        dy_ref, sp_ref = refs[nin], refs[nin + 1]
        o0 = nin + 2 + hi
        dx_refs = refs[o0:o0 + nx]
        dw_refs = refs[o0 + nx:o0 + nx + nw]
        ds_scr = refs[o0 + nx + nw + ho]
        _run_hook(hook, pl.program_id(0) * nchunk + step, hblocks * nchunk, refs[nin + 2:o0],
                  refs[o0 + nx + nw:o0 + nx + nw + ho], refs[o0 + nx + nw + ho + 1:])

        @pl.when(step == 0)
        def _():
            ds_scr[...] = jnp.zeros_like(ds_scr)
            for r in dw_refs:
                r[...] = jnp.zeros_like(r)

        cv = [r[...] for r in refs[nx:nx + ncs]]
        for e in range(hpb):
            xv = [r[:, e * w:(e + 1) * w] for r, (_, w, _) in zip(refs[:nx], xs)]
            wv = [r[e] for r in refs[nx + ncs:nx + ncs + nw]]
            kv = [r[e] for r in refs[nx + ncs + nw:nin]]
            if pre is not None:
                xv = pre(xv, cv)
            _, vjp = jax.vjp(lambda xs_, s_, ws_, kv=kv: fn(n, xs_, s_, cv, ws_, kv), xv, sp_ref[e, 0], wv)
            dxs, ds_prev, dws = vjp((dy_ref[:, e * out_w:(e + 1) * out_w].astype(F32), ds_scr[e]))
            if post is not None:
                dxs = post(dxs, cv)
            for r, v, (_, w, _) in zip(dx_refs, dxs, xs):
                r[:, e * w:(e + 1) * w] = v.astype(r.dtype)
            for r, v in zip(dw_refs, dws):
                r[e] += v
            ds_scr[e] = ds_prev

    lp = nchunk * CHUNK
    rev = lambda n: nchunk - 1 - n
    in_specs = _scan_specs(xs, cs, ws, ks, rev, hpb)
    in_specs.append(pl.BlockSpec((CHUNK, out_w * hpb), lambda h, n: (rev(n), h)))
    in_specs.append(pl.BlockSpec((hpb, 1) + s_shape, lambda h, n: (h, rev(n), 0, 0)))
    out_specs = [pl.BlockSpec((CHUNK, w * hpb), lambda h, n: (rev(n), h)) for (_, w, _) in xs]
    out_specs += [pl.BlockSpec((hpb, 1, w.shape[2]), lambda h, n: (h, 0, 0)) for w in ws]
    out_shape = [jax.ShapeDtypeStruct((lp, heads * w), BF16) for (_, w, _) in xs]
    out_shape += [jax.ShapeDtypeStruct(w.shape, F32) for w in ws]
    res = pl.pallas_call(
        body, name=name, grid=(hblocks, nchunk), in_specs=in_specs + [_ANY] * hi,
        out_specs=out_specs + [_ANY] * ho, out_shape=out_shape + hook.outs,
        scratch_shapes=[pltpu.VMEM((hpb,) + s_shape, F32)] + hook.sems,
        compiler_params=_params(("arbitrary", "arbitrary")),
    )(*[t[0] for t in xs], *[t[0] for t in cs], *ws, *ks, dy, sprev, *hook.ins)
    if hook is _NO_HOOK:
        return res[:nx], res[nx:]
    return res[:nx], res[nx:nx + nw], res[nx + nw:]


def _iota2(shape, dim):
    return lax.broadcasted_iota(jnp.int32, shape, dim)


def _ret_chunk(n, xs, state, cs, ws, ks):
    q, k, v, z = xs
    (w,), (lg,) = ws, ks
    lgc = lg[:, :1]
    row, col = _iota2((CHUNK, CHUNK), 0), _iota2((CHUNK, CHUNK), 1)
    diff = jnp.maximum(row - col, 0).astype(F32)
    decay = jnp.where(row >= col, jnp.exp(lg * diff), 0.0)
    scores = _bdot(q, k, 1, 1) * decay
    o_intra = _bdot(scores, v, 1, 0)
    idx = _iota2((CHUNK, 1), 0).astype(F32)
    k_w = k * jnp.exp(lgc * (CHUNK - 1.0 - idx))
    kv = _bdot(k_w, v, 0, 0)
    s_new = state * jnp.exp(lgc * float(CHUNK)) + kv
    q_w = q * jnp.exp(lgc * (idx + 1.0))
    o = o_intra + _bdot(q_w, state, 1, 0)
    return _rms(o, w) * _silu(z), s_new


def _rope(t, cos2, sin2):
    return t * cos2 + pltpu.roll(t, RET_DK // 2, 1) * sin2


def _rope_t(g, cos2, sin2):
    return g * cos2 - pltpu.roll(g, RET_DK // 2, 1) * sin2


def _ret_pre(xv, cv):
    q, k, v, z = xv
    cos2, sin2 = cv
    return [_rope(q, cos2, sin2), _rope(k, cos2, sin2) * (RET_DK ** -0.5), v, z]


def _ret_post(dxs, cv):
    dq, dk, dv, dz = dxs
    cos2, sin2 = cv
    return [_rope_t(dq, cos2, sin2), _rope_t(dk, cos2, sin2) * (RET_DK ** -0.5), dv, dz]


def _gla_chunk(n, xs, state_t, cs, ws, ks):
    q, k, v, z, pre = xs
    (w,) = ws
    q = q * (GLA_DK ** -0.5)
    rowc = _iota2((CHUNK, 1), 0)
    valid = jnp.logical_or(n > 0, rowc >= PAD)
    log_a = jnp.where(valid, _log_sigmoid(pre) / GLA_TAU, 0.0)
    row, col = _iota2((CHUNK, CHUNK), 0), _iota2((CHUNK, CHUNK), 1)
    tri = (row >= col).astype(F32)
    b = jnp.dot(tri, log_a, precision=lax.Precision.HIGHEST, preferred_element_type=F32)
    b_last = b[CHUNK - 1:CHUNK, :]
    kv_t = _bdot(v, k * jnp.exp(b_last - b), 0, 0)
    s_new = state_t * jnp.exp(b_last) + kv_t
    o_inter = _bdot(q * jnp.exp(b), state_t, 1, 1)
    outs = []
    for s in range(CHUNK // SUB):
        lo, hi = s * SUB, (s + 1) * SUB
        b_ref = jnp.zeros_like(b_last) if s == 0 else b[lo - 1:lo, :]
        q_hat = q[lo:hi] * jnp.exp(b[lo:hi] - b_ref)
        k_hat = k[:hi] * jnp.exp(b_ref - b[:hi])
        sc = _bdot(q_hat, k_hat, 1, 1)
        causal = _iota2((SUB, hi), 0) + lo >= _iota2((SUB, hi), 1)
        outs.append(_bdot(jnp.where(causal, sc, 0.0), v[:hi], 1, 0))
    o = jnp.concatenate(outs, axis=0) + o_inter
    return _rms(o, w) * _silu(z), s_new


def _s5_disc(lam_re, lam_im, log_dt, b_re, b_im, expand):
    dt = jnp.exp(log_dt)
    mag = jnp.exp(lam_re * dt)
    ab_re, ab_im = mag * jnp.cos(lam_im * dt), mag * jnp.sin(lam_im * dt)
    den = lam_re * lam_re + lam_im * lam_im
    nr, ni = ab_re - 1.0, ab_im
    f_re = (nr * lam_re + ni * lam_im) / den
    f_im = (ni * lam_re - nr * lam_im) / den
    hp = lax.Precision.HIGHEST
    f_re = jnp.dot(f_re, expand, precision=hp, preferred_element_type=F32)
    f_im = jnp.dot(f_im, expand, precision=hp, preferred_element_type=F32)
    return ab_re, ab_im, f_re * b_re - f_im * b_im, f_re * b_im + f_im * b_re


def _s5_disc_fwd(args):
    def body(*refs):
        outs = _s5_disc(*[r[...] for r in refs[:6]])
        for r, v in zip(refs[6:], outs):
            r[...] = v

    g, p = args[0].shape
    return pl.pallas_call(
        body, name="s5_disc_fwd",
        out_shape=[jax.ShapeDtypeStruct((g, p), F32)] * 2 + [jax.ShapeDtypeStruct(args[3].shape, F32)] * 2,
    )(*args)


def _s5_disc_bwd(args, cts):
    def body(*refs):
        prim = [r[...] for r in refs[:5]]
        expand = refs[5][...]
        ct = tuple(r[...] for r in refs[6:10])
        _, vjp = jax.vjp(lambda *a: _s5_disc(*a, expand), *prim)
        for r, v in zip(refs[10:], vjp(ct)):
            r[...] = v

    return pl.pallas_call(
        body, name="s5_disc_bwd", out_shape=[jax.ShapeDtypeStruct(a.shape, F32) for a in args[:5]],
    )(*args, *cts)


S5_SUBL = 8
S5_LANES = S5_N // S5_SUBL
S5_TB = 64


def _s5_scan_fwd(bu, a_re, a_im):
    lp = bu.shape[0]

    def body(bu_ref, ar_ref, ai_ref, x_ref, st):
        @pl.when(pl.program_id(0) == 0)
        def _():
            st[...] = jnp.zeros_like(st)

        ar, ai = ar_ref[...], ai_ref[...]

        def step(t, carry):
            xr, xi = carry
            nr = ar * xr - ai * xi + bu_ref[t, 0:S5_SUBL, :]
            ni = ar * xi + ai * xr + bu_ref[t, S5_SUBL:2 * S5_SUBL, :]
            x_ref[t, 0:S5_SUBL, :] = nr
            x_ref[t, S5_SUBL:2 * S5_SUBL, :] = ni
            return nr, ni

        xr, xi = lax.fori_loop(0, S5_TB, step, (st[0], st[1]))
        st[0] = xr
        st[1] = xi

    blk = pl.BlockSpec((S5_TB, 2 * S5_SUBL, S5_LANES), lambda i: (i, 0, 0))
    cst = pl.BlockSpec((S5_SUBL, S5_LANES), lambda i: (0, 0))
    return pl.pallas_call(
        body, name="s5_scan_fwd", grid=(lp // S5_TB,), in_specs=[blk, cst, cst], out_specs=blk,
        out_shape=jax.ShapeDtypeStruct(bu.shape, F32),
        scratch_shapes=[pltpu.VMEM((2, S5_SUBL, S5_LANES), F32)],
        compiler_params=_params(("arbitrary",)),
    )(bu, a_re, a_im)


def _s5_scan_bwd(gx, x, a_re, a_im):
    lp = gx.shape[0]
    nb = lp // S5_TB

    def body(gx_ref, x_ref, xp_ref, ar_ref, ai_ref, g_ref, da_ref, st):
        i = pl.program_id(0)

        @pl.when(i == 0)
        def _():
            st[...] = jnp.zeros_like(st)
            da_ref[...] = jnp.zeros_like(da_ref)

        ar, ai = ar_ref[...], ai_ref[...]
        first = (i == nb - 1).astype(F32)

        def step(s, carry):
            gr, gi, dar, dai = carry
            t = S5_TB - 1 - s
            ngr = gx_ref[t, 0:S5_SUBL, :] + ar * gr + ai * gi
            ngi = gx_ref[t, S5_SUBL:2 * S5_SUBL, :] + ar * gi - ai * gr
            g_ref[t, 0:S5_SUBL, :] = ngr
            g_ref[t, S5_SUBL:2 * S5_SUBL, :] = ngi
            tp = jnp.maximum(t - 1, 0)
            at0 = (t == 0).astype(F32)
            keep = 1.0 - at0
            pr = keep * x_ref[tp, 0:S5_SUBL, :] + at0 * (1.0 - first) * xp_ref[0, 0:S5_SUBL, :]
            pi = keep * x_ref[tp, S5_SUBL:2 * S5_SUBL, :] + at0 * (1.0 - first) * xp_ref[0, S5_SUBL:2 * S5_SUBL, :]
            return ngr, ngi, dar + ngr * pr + ngi * pi, dai + ngi * pr - ngr * pi

        zero = jnp.zeros((S5_SUBL, S5_LANES), F32)
        gr, gi, dar, dai = lax.fori_loop(0, S5_TB, step, (st[0], st[1], zero, zero))
        st[0] = gr
        st[1] = gi
        da_ref[0] += dar
        da_ref[1] += dai

    rev = lambda i: nb - 1 - i
    blk = pl.BlockSpec((S5_TB, 2 * S5_SUBL, S5_LANES), lambda i: (rev(i), 0, 0))
    prev = pl.BlockSpec((1, 2 * S5_SUBL, S5_LANES), lambda i: (jnp.maximum(rev(i) * S5_TB - 1, 0), 0, 0))
    cst = pl.BlockSpec((S5_SUBL, S5_LANES), lambda i: (0, 0))
    return pl.pallas_call(
        body, name="s5_scan_bwd", grid=(nb,), in_specs=[blk, blk, prev, cst, cst],
        out_specs=[blk, pl.BlockSpec((2, S5_SUBL, S5_LANES), lambda i: (0, 0, 0))],
        out_shape=[jax.ShapeDtypeStruct(gx.shape, F32), jax.ShapeDtypeStruct((2, S5_SUBL, S5_LANES), F32)],
        scratch_shapes=[pltpu.VMEM((2, S5_SUBL, S5_LANES), F32)],
        compiler_params=_params(("arbitrary",)),
    )(gx, x, x, a_re, a_im)


def _place():
    x, y, c = lax.axis_index("x"), lax.axis_index("y"), lax.axis_index("c")
    return x, y, c, [(1 - x, y), (x, 1 - y), (1 - x, 1 - y)]


def _gather_phases():
    def plan(x_ref, out_ref, send_sems, recv_sems, local_sem):
        x, y, c, chips = _place()
        me, sibling = (x, y, c), (x, y, 1 - c)

        def rows(px, py, pc):
            return out_ref.at[4 * px + 2 * py + pc]

        def copy(k, block, to, src=None):
            return pltpu.make_async_remote_copy(
                src_ref=rows(*block) if src is None else src, dst_ref=rows(*block),
                send_sem=send_sems.at[k], recv_sem=recv_sems.at[k], device_id=to, device_id_type=MESH)

        mine = pltpu.make_async_copy(x_ref, rows(*me), local_sem)
        first = [copy(0, me, sibling, src=x_ref)]
        first += [copy(1 + j, me, (*chip, c), src=x_ref) for j, chip in enumerate(chips)]
        passed = [copy(4 + j, (*chip, c), sibling) for j, chip in enumerate(chips)]
        return c, chips, me, sibling, copy, mine, first, passed

    def start(ins, outs, sems):
        _, _, _, _, _, mine, first, _ = plan(ins[0], outs[0], *sems)
        mine.start()
        for cp in first:
            cp.start()

    def middle(ins, outs, sems):
        c, chips, me, _, copy, _, _, passed = plan(ins[0], outs[0], *sems)
        for j, chip in enumerate(chips):
            copy(1 + j, (*chip, c), me).wait_recv()
            passed[j].start()

    def finish(ins, outs, sems):
        c, chips, me, sibling, copy, mine, first, passed = plan(ins[0], outs[0], *sems)
        copy(0, sibling, me).wait_recv()
        for j, chip in enumerate(chips):
            copy(4 + j, (*chip, 1 - c), me).wait_recv()
        for cp in first + passed:
            cp.wait_send()
        mine.wait()

    return start, middle, finish


_GATHER_SEMS = [pltpu.SemaphoreType.DMA((7,)), pltpu.SemaphoreType.DMA((7,)), pltpu.SemaphoreType.DMA]


def _all_gather(shard, name):
    phases = _gather_phases()

    def body(x_ref, out_ref, *sems):
        for phase in phases:
            phase([x_ref], [out_ref], sems)

    return pl.pallas_call(
        body, name=name, out_shape=jax.ShapeDtypeStruct((N_DEV,) + shard.shape, shard.dtype),
        in_specs=[_ANY], out_specs=_ANY, scratch_shapes=list(_GATHER_SEMS),
    )(shard)


def _gather_hook(shard):
    start, middle, finish = _gather_phases()
    return _Hook([shard], [jax.ShapeDtypeStruct((N_DEV,) + shard.shape, shard.dtype)], _GATHER_SEMS,
                 [(0.0, start), (0.85, middle), (1.0, finish)])


def _swap_with_sibling(parts, name):
    def body(p_ref, out_ref, send_sems, recv_sems):
        x, y, c, _ = _place()
        copies = [pltpu.make_async_remote_copy(
            src_ref=p_ref.at[2 * chip + (1 - c)], dst_ref=out_ref.at[chip],
            send_sem=send_sems.at[chip], recv_sem=recv_sems.at[chip],
            device_id=(x, y, 1 - c), device_id_type=MESH) for chip in range(4)]
        for cp in copies:
            cp.start()
        for cp in copies:
            cp.wait()

    return pl.pallas_call(
        body, name=name, out_shape=jax.ShapeDtypeStruct((4,) + parts.shape[1:], parts.dtype),
        in_specs=[pl.BlockSpec(memory_space=pl.ANY)], out_specs=pl.BlockSpec(memory_space=pl.ANY),
        scratch_shapes=[pltpu.SemaphoreType.DMA((4,)), pltpu.SemaphoreType.DMA((4,))],
    )(parts)


def _chips_phases():
    def copies(p_ref, out_ref, send_sems, recv_sems):
        x, y, c, chips = _place()
        return [pltpu.make_async_remote_copy(
            src_ref=p_ref.at[2 * px + py], dst_ref=out_ref.at[j],
            send_sem=send_sems.at[j], recv_sem=recv_sems.at[j],
            device_id=(px, py, c), device_id_type=MESH) for j, (px, py) in enumerate(chips)]

    def start(ins, outs, sems):
        for cp in copies(ins[0], outs[0], *sems):
            cp.start()

    def finish(ins, outs, sems):
        for cp in copies(ins[0], outs[0], *sems):
            cp.wait()

    return start, finish


def _chips_hook(parts):
    start, finish = _chips_phases()
    return _Hook([parts], [jax.ShapeDtypeStruct((3,) + parts.shape[1:], parts.dtype)],
                 [pltpu.SemaphoreType.DMA((3,)), pltpu.SemaphoreType.DMA((3,))], [(0.0, start), (1.0, finish)])


def _pack_rows(n_elem, row_mult=PACK_ROW_MULT):
    rows = -(-n_elem // PACK_COLS)
    return -(-rows // row_mult) * row_mult


def _pack(flats, dtype, row_mult=PACK_ROW_MULT):
    flat = jnp.concatenate([f.reshape(-1).astype(dtype) for f in flats])
    rows = _pack_rows(flat.shape[0], row_mult)
    return jnp.pad(flat, (0, rows * PACK_COLS - flat.shape[0])).reshape(rows, PACK_COLS)


def _unpack(buf, shapes):
    lead = buf.shape[:-2]
    flat = buf.reshape(lead + (-1,))
    outs, o = [], 0
    for s in shapes:
        n = math.prod(s)
        outs.append(flat[..., o:o + n].reshape(lead + tuple(s)))
        o += n
    return outs


BIG_LAYOUT = (("w_in_ab", D_MODEL, PACK_COLS), ("s5_w_glu", S5_W // N_DEV, PACK_COLS),
              ("w_out_ab", OUT_AB // N_DEV, 2 * PACK_COLS), ("w_in_c", D_MODEL, PACK_COLS),
              ("w_out_c", GLA_W // N_DEV, 2 * PACK_COLS))


def _to_rows(a):
    if a.shape[-1] == PACK_COLS:
        return a
    assert a.shape[-1] == 2 * PACK_COLS
    return jnp.concatenate([a[..., :PACK_COLS], a[..., PACK_COLS:]], axis=-2)


def _from_rows(p, cols):
    if cols == PACK_COLS:
        return p
    r = p.shape[-2] // 2
    return jnp.concatenate([p[..., :r, :], p[..., r:, :]], axis=-1)


FIRST_LAYOUT, REST_LAYOUT = BIG_LAYOUT[:1], BIG_LAYOUT[1:]
MID_LAYOUT, LAST_LAYOUT = BIG_LAYOUT[1:3], BIG_LAYOUT[3:]


def _pack_big(pieces, layout):
    return jnp.concatenate([_to_rows(pieces[name]) for name, _, _ in layout], axis=-2)


def _unpack_big(buf, layout):
    out, o = {}, 0
    for name, rows, cols in layout:
        r = rows * cols // PACK_COLS
        out[name] = _from_rows(buf[..., o:o + r, :], cols)
        o += r
    return out


def _rows1024(a):
    r, c = a.shape
    if c > PACK_COLS:
        a = jnp.concatenate([a[:, i * PACK_COLS:(i + 1) * PACK_COLS] for i in range(c // PACK_COLS)], axis=0)
    elif c < PACK_COLS:
        a = jnp.pad(a, ((0, 0), (0, PACK_COLS - c)))
    return jnp.pad(a, ((0, -a.shape[0] % 8), (0, 0)))


def _unrows1024(p, r, c):
    if c > PACK_COLS:
        return jnp.concatenate([p[i * r:(i + 1) * r] for i in range(c // PACK_COLS)], axis=1)
    return p[:r, :c]


def _lane_select(a, off, sign, n_out, out_dtype, exact, name):
    rows, n_in = a.shape
    tr = _tile(rows, 256, 16)

    def body(off_ref, a_ref, o_ref):
        sel = _iota2((n_in, n_out), 0) + off_ref[0] * sign == _iota2((n_in, n_out), 1)
        if exact:
            r = jnp.dot(a_ref[...], sel.astype(F32), precision=lax.Precision.HIGHEST, preferred_element_type=F32)
        else:
            r = _dg(a_ref[...], sel.astype(BF16), 1, 0)
        o_ref[...] = r.astype(out_dtype)

    return pl.pallas_call(
        body, name=name, grid=(rows // tr,),
        in_specs=[pl.BlockSpec(memory_space=pltpu.SMEM), pl.BlockSpec((tr, n_in), lambda i: (i, 0))],
        out_specs=pl.BlockSpec((tr, n_out), lambda i: (i, 0)),
        out_shape=jax.ShapeDtypeStruct((rows, n_out), out_dtype),
        compiler_params=_params(("arbitrary",)),
    )(off, a)


def _adamw(w, g, m, v, name):
    rows, cols = w.shape
    tr = _tile(rows, 256, 8) if rows % 8 == 0 else rows

    def fn(i, w_, g_, m_, v_):
        m_new = ADAM_B1 * m_ + (1.0 - ADAM_B1) * g_
        v_new = ADAM_B2 * v_ + (1.0 - ADAM_B2) * (g_ * g_)
        m_hat = m_new / (1.0 - ADAM_B1 ** ADAM_STEP)
        v_hat = v_new / (1.0 - ADAM_B2 ** ADAM_STEP)
        delta = -ADAM_LR * (m_hat / (jnp.sqrt(v_hat) + ADAM_EPS) + ADAM_WD * w_)
        return (delta, m_new, v_new), ()

    outs, _ = _rows(fn, [_win(w), _win(g), _win(m), _win(v)], [], [(cols, F32)] * 3, [], name=name,
                    nrow=rows, tr=tr)
    return outs


def _as2d(a):
    if a.ndim == 1:
        return a.reshape(1, -1)
    if a.ndim == 2:
        return a
    a = a.reshape(a.shape[1:])
    return a if a.ndim == 2 else a.reshape(a.shape[0], -1)


def kernel(x, meta, norm_ab_w, w_in_ab, ret_norm_w, s5_lam_re, s5_lam_im, s5_log_dt, s5_b_re, s5_b_im, s5_c_re, s5_c_im, s5_d, s5_w_glu, w_out_ab, norm_c_w, w_in_c, gla_w_gate, gla_b_gate, gla_norm_w, w_out_c, final_norm_w, loss_target, m_meta, m_norm_ab_w, m_w_in_ab, m_ret_norm_w, m_s5_lam_re, m_s5_lam_im, m_s5_log_dt, m_s5_b_re, m_s5_b_im, m_s5_c_re, m_s5_c_im, m_s5_d, m_s5_w_glu, m_w_out_ab, m_norm_c_w, m_w_in_c, m_gla_w_gate, m_gla_b_gate, m_gla_norm_w, m_w_out_c, m_final_norm_w, v_meta, v_norm_ab_w, v_w_in_ab, v_ret_norm_w, v_s5_lam_re, v_s5_lam_im, v_s5_log_dt, v_s5_b_re, v_s5_b_im, v_s5_c_re, v_s5_c_im, v_s5_d, v_s5_w_glu, v_w_out_ab, v_norm_c_w, v_w_in_c, v_gla_w_gate, v_gla_b_gate, v_gla_norm_w, v_w_out_c, v_final_norm_w):
    weights = dict(meta=meta, norm_ab_w=norm_ab_w, w_in_ab=w_in_ab, ret_norm_w=ret_norm_w, s5_lam_re=s5_lam_re,
                   s5_lam_im=s5_lam_im, s5_log_dt=s5_log_dt, s5_b_re=s5_b_re, s5_b_im=s5_b_im, s5_c_re=s5_c_re,
                   s5_c_im=s5_c_im, s5_d=s5_d, s5_w_glu=s5_w_glu, w_out_ab=w_out_ab, norm_c_w=norm_c_w,
                   w_in_c=w_in_c, gla_w_gate=gla_w_gate, gla_b_gate=gla_b_gate, gla_norm_w=gla_norm_w,
                   w_out_c=w_out_c, final_norm_w=final_norm_w)
    mom_m = dict(meta=m_meta, norm_ab_w=m_norm_ab_w, w_in_ab=m_w_in_ab, ret_norm_w=m_ret_norm_w,
                 s5_lam_re=m_s5_lam_re, s5_lam_im=m_s5_lam_im, s5_log_dt=m_s5_log_dt, s5_b_re=m_s5_b_re,
                 s5_b_im=m_s5_b_im, s5_c_re=m_s5_c_re, s5_c_im=m_s5_c_im, s5_d=m_s5_d, s5_w_glu=m_s5_w_glu,
                 w_out_ab=m_w_out_ab, norm_c_w=m_norm_c_w, w_in_c=m_w_in_c, gla_w_gate=m_gla_w_gate,
                 gla_b_gate=m_gla_b_gate, gla_norm_w=m_gla_norm_w, w_out_c=m_w_out_c, final_norm_w=m_final_norm_w)
    mom_v = dict(meta=v_meta, norm_ab_w=v_norm_ab_w, w_in_ab=v_w_in_ab, ret_norm_w=v_ret_norm_w,
                 s5_lam_re=v_s5_lam_re, s5_lam_im=v_s5_lam_im, s5_log_dt=v_s5_log_dt, s5_b_re=v_s5_b_re,
                 s5_b_im=v_s5_b_im, s5_c_re=v_s5_c_re, s5_c_im=v_s5_c_im, s5_d=v_s5_d, s5_w_glu=v_s5_w_glu,
                 w_out_ab=v_w_out_ab, norm_c_w=v_norm_c_w, w_in_c=v_w_in_c, gla_w_gate=v_gla_w_gate,
                 gla_b_gate=v_gla_b_gate, gla_norm_w=v_gla_norm_w, w_out_c=v_w_out_c, final_norm_w=v_final_norm_w)
    order = list(weights)

    seq = x.shape[1]
    lp = CHUNK + seq
    nchunk = lp // CHUNK
    dev = 4 * lax.axis_index("x") + 2 * lax.axis_index("y") + lax.axis_index("c")
    core = lax.axis_index("c")
    chip = 2 * lax.axis_index("x") + lax.axis_index("y")

    win_off = jnp.reshape(2 * dev, (1,)).astype(jnp.int32)
    shard_c = jnp.pad(w_in_c[0].astype(BF16), ((0, 0), (0, 896 - SHARD_C)))
    big_shards = dict(w_in_ab=w_in_ab[0].astype(BF16), s5_w_glu=s5_w_glu[0].astype(BF16),
                      w_out_ab=w_out_ab[0].astype(BF16), w_out_c=w_out_c[0].astype(BF16),
                      w_in_c=_lane_select(shard_c, win_off, 1, WIN_COLS, BF16, False, "w_in_c_to_window"))
    def pad_to(a, rows, cols):
        return jnp.pad(a, ((0, rows - a.shape[0]), (0, cols - a.shape[1])))

    shard_w = D_MODEL // N_DEV
    small_pack = jnp.concatenate([meta, pad_to(norm_c_w, 8, shard_w), pad_to(gla_w_gate[0], GLA_RANK, shard_w),
                                  pad_to(gla_b_gate, 8, shard_w), pad_to(gla_norm_w, 8, shard_w)], axis=0)
    w_in_ab_g = _all_gather(big_shards["w_in_ab"], "gather_first")
    mid_hook = _gather_hook(_pack_big(big_shards, MID_LAYOUT))
    last_hook = _gather_hook(_pack_big(big_shards, LAST_LAYOUT))
    gs = _all_gather(small_pack, "gather_small")
    gate_w = GLA_QK // N_DEV
    s_meta, s_norm_c = gs[:, :N_META], gs[:, N_META]
    s_wgate, s_bgate, s_gnorm = gs[:, 24:24 + GLA_RANK, :gate_w], gs[:, 40, :gate_w], gs[:, 48]
    meta_f = s_meta.transpose(1, 0, 2).reshape(N_META, D_MODEL)
    norm_c_f = s_norm_c.reshape(1, D_MODEL)
    w_gate_f = jnp.pad(s_wgate.transpose(1, 0, 2).reshape(GLA_RANK, GLA_QK), ((0, GATE_PAD - GLA_RANK), (0, 0)))
    b_gate_f = s_bgate.reshape(1, GLA_QK)
    gla_norm_f = s_gnorm.reshape(GLA_H, 1, GLA_DV)

    pos = jnp.maximum(jnp.arange(lp, dtype=F32) - float(PAD), 0.0)
    inv_freq = jnp.power(ROPE_BASE, -jnp.arange(0, RET_DK, 2, dtype=F32) / RET_DK)
    ang = pos[:, None] * inv_freq[None, :]
    cos2 = jnp.concatenate([jnp.cos(ang), jnp.cos(ang)], axis=1)
    sin2 = jnp.concatenate([-jnp.sin(ang), jnp.sin(ang)], axis=1)
    log_g = jnp.log1p(-jnp.exp2(-5.0 - jnp.arange(RET_H, dtype=F32)))
    lg = jnp.broadcast_to(log_g[:, None, None], (RET_H, 1, 128))
    ret_norm_h = ret_norm_w.reshape(RET_H, 1, RET_DV)

    h0 = jnp.concatenate([jnp.zeros((PAD, D_MODEL), F32), meta_f, x[0]], axis=0)

    def rowmask(i):
        return (_iota2((CHUNK, 1), 0) + i * CHUNK) >= PAD

    (hn0,), _ = _rows(lambda i, h, w: ((_rms(h, w),), ()), [_win(h0)], [norm_ab_w], [(D_MODEL, BF16)], [],
                      name="norm_ab_fwd", nrow=lp)
    proj_ab, (gathered_mid,) = _mm(hn0, w_in_ab_g, "nn", name="in_ab_fwd", hook=mid_hook, b_dev=True)
    gb = _unpack_big(gathered_mid, MID_LAYOUT)
    w_glu_f = gb["s5_w_glu"].reshape(S5_W, S5_W)
    w_out_ab_f = gb["w_out_ab"].reshape(OUT_AB, D_MODEL)

    q_off, k_off, v_off, za_off = 0, RET_QK, 2 * RET_QK, 2 * RET_QK + RET_W
    u_off, zb_off = 2 * RET_QK + 2 * RET_W, 2 * RET_QK + 2 * RET_W + S5_W
    ret_xs = [(proj_ab, RET_DK, lambda h: q_off // RET_DK + h), (proj_ab, RET_DK, lambda h: k_off // RET_DK + h),
              (proj_ab, RET_DV, lambda h: v_off // RET_DV + h), (proj_ab, RET_DV, lambda h: za_off // RET_DV + h)]
    ret_cs = [(cos2, RET_DK, lambda h: 0), (sin2, RET_DK, lambda h: 0)]
    ret_kw = dict(heads=RET_H, nchunk=nchunk, s_shape=(RET_DK, RET_DV), out_w=RET_DV, pre=_ret_pre)
    o_a, ret_sprev, (gathered_last,) = _scan_fwd(_ret_chunk, ret_xs, ret_cs, [ret_norm_h], [lg], name="ret_fwd",
                                                 hook=last_hook, **ret_kw)
    gb = _unpack_big(gathered_last, LAST_LAYOUT)
    w_in_c_f = sum(jnp.pad(gb["w_in_c"][d], ((0, 0), (WIN_STEP * d, IN_C_PAD - WIN_STEP * d - WIN_COLS)))
                   for d in range(N_DEV))
    w_out_c_f = gb["w_out_c"].reshape(GLA_W, D_MODEL)

    expand = jnp.repeat(jnp.eye(S5_P, dtype=F32), S5_GH, axis=1)
    disc_args = (s5_lam_re[0], s5_lam_im[0], s5_log_dt[0].reshape(S5_G, 1),
                 s5_b_re[0].reshape(S5_G, S5_P * S5_GH), s5_b_im[0].reshape(S5_G, S5_P * S5_GH), expand)
    ab_re, ab_im, bb_re, bb_im = _s5_disc_fwd(disc_args)
    gt = S5_SUBL
    eye_t = jnp.eye(gt, dtype=F32)

    def tiles_in(bb):
        return jnp.einsum("sgph,gk->sghkp", bb.reshape(gt, gt, S5_P, S5_GH), eye_t).reshape(gt, 128, S5_LANES)

    def tiles_out(cc):
        return jnp.einsum("sghp,gk->sgpkh", cc.reshape(gt, gt, S5_GH, S5_P), eye_t).reshape(gt, S5_LANES, 128)

    wb_t = jnp.concatenate([tiles_in(bb_re), tiles_in(bb_im)], axis=0).astype(BF16)
    wc_t = jnp.concatenate([tiles_out(s5_c_re[0]), -tiles_out(s5_c_im[0])], axis=0).astype(BF16)
    a_re, a_im = ab_re.reshape(S5_SUBL, S5_LANES), ab_im.reshape(S5_SUBL, S5_LANES)
    tm5, tk5, nt5 = _tile(lp, 1408, 8), _tile(lp, 1408, 8), 2 * gt
    u_blk = u_off // 128
    wide = pl.BlockSpec((tm5, S5_LANES), lambda i, j, k: (i, j))
    wide_k = pl.BlockSpec((tm5, S5_LANES), lambda i, j, k: (i, k * gt + j))
    narrow = pl.BlockSpec((tm5, 128), lambda i, j, k: (i, j))
    wb_j = pl.BlockSpec((None, 128, S5_LANES), lambda i, j, k: (j, 0, 0))
    wc_j = pl.BlockSpec((None, S5_LANES, 128), lambda i, j, k: (j, 0, 0))
    wb_k = pl.BlockSpec((None, 128, S5_LANES), lambda i, j, k: (k * gt + j, 0, 0))
    wc_k = pl.BlockSpec((None, S5_LANES, 128), lambda i, j, k: (k * gt + j, 0, 0))
    wide_shape = jax.ShapeDtypeStruct((lp, 2 * S5_N), F32)
    bu = _mm_core(proj_ab, wb_t, dims=NN, grid=(lp // tm5, nt5, 1), name="s5_bu",
                  a_spec=pl.BlockSpec((tm5, 128), lambda i, j, k: (i, u_blk + j % gt)), b_spec=wb_j,
                  o_spec=wide, out_shape=wide_shape, acc_shape=(tm5, S5_LANES))
    xs5 = _s5_scan_fwd(bu.reshape(lp, 2 * S5_SUBL, S5_LANES), a_re, a_im)
    xs5_2d = xs5.reshape(lp, 2 * S5_N)
    y_pre = _mm_core(xs5_2d, wc_t, dims=NN, grid=(lp // tm5, gt, 2), name="s5_cx", a_spec=wide_k, b_spec=wc_k,
                     o_spec=narrow, out_shape=jax.ShapeDtypeStruct((lp, S5_W), F32), acc_shape=(tm5, 128))
    (y_s5, yg_bf), _ = _rows(
        lambda i, yp, u, d: ((yp + d * u, _gelu(yp + d * u)), ()),
        [_win(y_pre), _win(proj_ab, u_off, S5_W)], [s5_d], [(S5_W, F32), (S5_W, BF16)], [], name="s5_gelu_fwd", nrow=lp)
    t_glu = _mm(yg_bf, w_glu_f, "nn", name="s5_glu_fwd")

    def s5_gate(y, t, zb):
        return _gelu(y) * _sigmoid(t) * _silu(zb)

    (o_b,), _ = _rows(lambda i, y, t, zb: ((s5_gate(y, t, zb),), ()),
                      [_win(y_s5), _win(t_glu), _win(proj_ab, zb_off, S5_W)], [], [(S5_W, BF16)], [],
                      name="s5_gate_fwd", nrow=lp)
    o_ab = jnp.concatenate([o_a, o_b], axis=1)
    h1 = _mm(o_ab, w_out_ab_f, "nn", name="out_ab_fwd", add=h0)

    (hn1,), _ = _rows(lambda i, h, w: ((_rms(h, w),), ()), [_win(h1)], [norm_c_f], [(D_MODEL, BF16)], [],
                      name="norm_c_fwd", nrow=lp)
    proj_c = _mm(hn1, w_in_c_f, "nn", name="in_c_fwd")
    gl_off = 2 * GLA_QK + 2 * GLA_W
    pre_gate = _mm(proj_c, w_gate_f, "nn", name="gate_fwd", a_win=(gl_off, GATE_PAD), bias=b_gate_f)
    gla_xs = [(proj_c, GLA_DK, lambda h: h), (proj_c, GLA_DK, lambda h: GLA_QK // GLA_DK + h),
              (proj_c, GLA_DV, lambda h: 2 * GLA_QK // GLA_DV + h),
              (proj_c, GLA_DV, lambda h: (2 * GLA_QK + GLA_W) // GLA_DV + h),
              (pre_gate, GLA_DK, lambda h: h)]
    gla_kw = dict(heads=GLA_H, nchunk=nchunk, s_shape=(GLA_DV, GLA_DK), out_w=GLA_DV)
    o_c, gla_sprev = _scan_fwd(_gla_chunk, gla_xs, [], [gla_norm_f], [], name="gla_fwd", **gla_kw)
    h2 = _mm(o_c, w_out_c_f, "nn", name="out_c_fwd", add=h1)

    fnw = final_norm_w.reshape(1, D_MODEL)

    def final_fn(i, h, tgt, w):
        def loss_of(h_, w_):
            err = _rms(h_, w_) - tgt
            return 0.5 * jnp.sum(jnp.mean(err * err, axis=-1))

        real = (i > 0).astype(F32)
        loss_i, (dh, dw) = jax.value_and_grad(loss_of, argnums=(0, 1))(h, w)
        return (dh * real,), (jnp.full((1, 128), loss_i * real, F32), dw * real)

    (dh2,), (loss_acc, g_final) = _rows(final_fn, [_win(h2), _win(loss_target[0], roff=1)], [fnw],
                                        [(D_MODEL, F32)], [(1, 128), (1, D_MODEL)], name="final_loss", nrow=lp)

    def rs_front(pieces, layout, tag):
        g_full = _pack_big(pieces, layout)
        prow = g_full.shape[1]
        from_sibling = _swap_with_sibling(g_full, "rs_sibling_" + tag)
        mine_by_chip = lax.dynamic_index_in_dim(g_full.reshape(4, 2, prow, PACK_COLS), core, axis=1, keepdims=False)
        (p1, p1_bf), _ = _rows(
            lambda i, a, b: ((a.astype(F32) + b.astype(F32), a.astype(F32) + b.astype(F32)), ()),
            [_win(mine_by_chip.reshape(4 * prow, PACK_COLS)), _win(from_sibling.reshape(4 * prow, PACK_COLS))], [],
            [(PACK_COLS, F32), (PACK_COLS, BF16)], [], name="rs_sum_sibling_" + tag, nrow=4 * prow,
            tr=_tile(prow, 512, 16))
        return p1.reshape(4, prow, PACK_COLS), p1_bf.reshape(4, prow, PACK_COLS)

    def rs_back(p1, from_chips, layout, tag):
        prow = p1.shape[1]
        tr = _tile(prow, 512, 16)
        own = lax.dynamic_index_in_dim(p1, chip, axis=0, keepdims=False)
        fc2 = from_chips.reshape(3 * prow, PACK_COLS)
        nblk = prow // tr
        (g_shard,), _ = _rows(
            lambda i, a, b0, b1, b2: ((((a + b0.astype(F32)) + b1.astype(F32)) + b2.astype(F32),), ()),
            [_win(own), _win(fc2), _win(fc2, roff=-nblk), _win(fc2, roff=-2 * nblk)], [], [(PACK_COLS, F32)], [],
            name="rs_sum_chips_" + tag, nrow=prow, tr=tr)
        return _unpack_big(g_shard, layout)

    dh2_bf = dh2.astype(BF16)
    do_c = _mm(dh2_bf, w_out_c_f, "nt", name="out_c_dx", out_dtype=BF16)
    gw_out_c = _mm(o_c, dh2_bf, "tn", name="out_c_dw", out_dtype=BF16)
    (dq_c, dk_c, dv_c, dz_c, dpre), (g_gla_norm,) = _scan_bwd(
        _gla_chunk, gla_xs, [], [gla_norm_f], [], do_c, gla_sprev, name="gla_bwd", **gla_kw)
    dglow = _mm(dpre, w_gate_f, "nt", name="gate_dx", out_dtype=BF16)
    g_wgate = _mm(proj_c, dpre, "tn", name="gate_dw", a_win=(gl_off, GATE_PAD))[:GLA_RANK]
    (), (g_bgate,) = _rows(lambda i, d: ((), (jnp.sum(d.astype(F32), axis=0, keepdims=True),)), [_win(dpre)], [], [],
                           [(1, GLA_QK)], name="gate_db", nrow=lp)
    dproj_c = jnp.concatenate([dq_c, dk_c, dv_c, dz_c, dglow], axis=1)
    dhn1 = _mm(dproj_c, w_in_c_f, "nt", name="in_c_dx")
    gw_in_c = _mm(hn1, dproj_c, "tn", name="in_c_dw", out_dtype=BF16)

    def norm_bwd(i, h, dhn, dres, w):
        _, vjp = jax.vjp(_rms, h, w)
        dh, dw = vjp(dhn)
        return (jnp.where(rowmask(i), dh + dres, 0.0),), (dw,)

    (dh1,), (g_norm_c,) = _rows(norm_bwd, [_win(h1), _win(dhn1), _win(dh2)], [norm_c_f], [(D_MODEL, F32)],
                                [(1, D_MODEL)], name="norm_c_bwd", nrow=lp)

    dh1_bf = dh1.astype(BF16)
    do_ab = _mm(dh1_bf, w_out_ab_f, "nt", name="out_ab_dx", out_dtype=BF16)
    gw_out_ab = _mm(o_ab, dh1_bf, "tn", name="out_ab_dw", out_dtype=BF16)

    def s5_gate_bwd(i, dob, y, t, zb):
        _, vjp = jax.vjp(s5_gate, y, t, zb)
        dy, dt, dzb = vjp(dob.astype(F32))
        return (dy, dt, dzb), ()

    (dy_a, dt_glu, dzb), _ = _rows(
        s5_gate_bwd, [_win(do_ab, RET_W, S5_W), _win(y_s5), _win(t_glu), _win(proj_ab, zb_off, S5_W)], [],
        [(S5_W, F32), (S5_W, BF16), (S5_W, BF16)], [], name="s5_gate_bwd", nrow=lp)
    dyg2 = _mm(dt_glu, w_glu_f, "nt", name="s5_glu_dx")
    gw_glu = _mm(yg_bf, dt_glu, "tn", name="s5_glu_dw", out_dtype=BF16)

    def s5_y_bwd(i, dya, dyg, y, u, d):
        _, vjp = jax.vjp(_gelu, y)
        (dy_g,) = vjp(dyg)
        dy = dya + dy_g
        return (dy, d * dy), (jnp.sum(dy * u, axis=0, keepdims=True),)

    (dy_s5, du1), (g_d,) = _rows(
        s5_y_bwd, [_win(dy_a), _win(dyg2), _win(y_s5), _win(proj_ab, u_off, S5_W)], [s5_d],
        [(S5_W, BF16), (S5_W, F32)], [(1, S5_W)], name="s5_y_bwd", nrow=lp)
    gx = _mm_core(dy_s5, wc_t, dims=NT, grid=(lp // tm5, nt5, 1), name="s5_cx_dx",
                  a_spec=pl.BlockSpec((tm5, 128), lambda i, j, k: (i, j % gt)), b_spec=wc_j,
                  o_spec=wide, out_shape=wide_shape, acc_shape=(tm5, S5_LANES))
    rows_k = lambda col: pl.BlockSpec((tk5, col), lambda i, j, k: (k, i))
    gwc = _mm_core(xs5_2d, dy_s5, dims=TN, grid=(nt5, 1, lp // tk5), name="s5_cx_dw", a_spec=rows_k(S5_LANES),
                   b_spec=pl.BlockSpec((tk5, 128), lambda i, j, k: (k, i % gt)),
                   o_spec=pl.BlockSpec((None, S5_LANES, 128), lambda i, j, k: (i, 0, 0)),
                   out_shape=jax.ShapeDtypeStruct((nt5, S5_LANES, 128), F32), acc_shape=(S5_LANES, 128))
    g_s5, da = _s5_scan_bwd(gx.reshape(lp, 2 * S5_SUBL, S5_LANES), xs5, a_re, a_im)
    g_s5_2d = g_s5.reshape(lp, 2 * S5_N)
    du = _mm_core(g_s5_2d, wb_t, dims=NT, grid=(lp // tm5, gt, 2), name="s5_bu_dx", a_spec=wide_k, b_spec=wb_k,
                  o_spec=narrow, out_shape=jax.ShapeDtypeStruct((lp, S5_W), BF16), acc_shape=(tm5, 128),
                  extra=[(du1, narrow)])
    gwb = _mm_core(proj_ab, g_s5_2d, dims=TN, grid=(nt5, 1, lp // tk5), name="s5_bu_dw",
                   a_spec=pl.BlockSpec((tk5, 128), lambda i, j, k: (k, u_blk + i % gt)), b_spec=rows_k(S5_LANES),
                   o_spec=pl.BlockSpec((None, 128, S5_LANES), lambda i, j, k: (i, 0, 0)),
                   out_shape=jax.ShapeDtypeStruct((nt5, 128, S5_LANES), F32), acc_shape=(128, S5_LANES))
    gwc6 = gwc.reshape(2, gt, gt, S5_P, gt, S5_GH)
    g_c = jnp.einsum("rsgpgh->rsghp", gwc6).reshape(2, S5_G, S5_GH, S5_P)
    g_c_re, g_c_im = g_c[0], -g_c[1]
    gwb6 = gwb.reshape(2, gt, gt, S5_GH, gt, S5_P)
    d_bb = jnp.einsum("rsghgp->rsgph", gwb6).reshape(2, S5_G, S5_P * S5_GH)
    d_bb_re, d_bb_im = d_bb[0], d_bb[1]
    g_lam_re, g_lam_im, g_log_dt, g_b_re, g_b_im = _s5_disc_bwd(
        disc_args, (da[0].reshape(S5_G, S5_P), da[1].reshape(S5_G, S5_P), d_bb_re, d_bb_im))

    p1_rest, p1_rest_bf = rs_front(dict(
        s5_w_glu=gw_glu.reshape(N_DEV, S5_W // N_DEV, S5_W),
        w_out_ab=gw_out_ab.reshape(N_DEV, OUT_AB // N_DEV, D_MODEL),
        w_in_c=jnp.stack([gw_in_c[:, WIN_STEP * d:WIN_STEP * d + WIN_COLS] for d in range(N_DEV)]),
        w_out_c=gw_out_c.reshape(N_DEV, GLA_W // N_DEV, D_MODEL)), REST_LAYOUT, "rest")
    (dq_a, dk_a, dv_a, dz_a), (g_ret_norm,), (from_chips_rest,) = _scan_bwd(
        _ret_chunk, ret_xs, ret_cs, [ret_norm_h], [lg], do_ab, ret_sprev, name="ret_bwd", post=_ret_post,
        hook=_chips_hook(p1_rest_bf), **ret_kw)
    dproj_ab = jnp.concatenate([dq_a, dk_a, dv_a, dz_a, du, dzb], axis=1)
    gw_in_ab = _mm(hn0, dproj_ab, "tn", name="in_ab_dw", out_dest=True, out_dtype=BF16)
    p1_first, p1_first_bf = rs_front(dict(w_in_ab=gw_in_ab), FIRST_LAYOUT, "first")
    dhn0, (from_chips_first,) = _mm(dproj_ab, w_in_ab_g, "nt", name="in_ab_dx", hook=_chips_hook(p1_first_bf),
                                    b_dev=True)
    (dh0,), (g_norm_ab,) = _rows(norm_bwd, [_win(h0), _win(dhn0), _win(dh1)], [norm_ab_w], [(D_MODEL, F32)],
                                 [(1, D_MODEL)], name="norm_ab_bwd", nrow=lp)
    grad_x = dh0[CHUNK:][None]
    g_meta_part = dh0[PAD:CHUNK]

    big_grads = {**rs_back(p1_rest, from_chips_rest, REST_LAYOUT, "rest"),
                 **rs_back(p1_first, from_chips_first, FIRST_LAYOUT, "first")}
    big_grads["w_in_c"] = _lane_select(big_grads["w_in_c"], win_off, -1, 896, F32, True,
                                       "w_in_c_from_window")[:, :SHARD_C]

    lane = lambda a_: pad_to(a_, a_.shape[0], 128)
    small_pieces = [
        ("vec2048", jnp.concatenate([g_norm_ab, g_final, g_norm_c], axis=0)),
        ("vec1024", jnp.concatenate([g_d, g_bgate, pad_to(loss_acc[:, :1], 1, PACK_COLS)], axis=0)),
        ("lam3", jnp.concatenate([lane(g_lam_re), lane(g_lam_im), lane(g_log_dt)], axis=1)),
        ("s5_b_re", g_b_re), ("s5_b_im", g_b_im),
        ("s5_c_re", g_c_re.reshape(S5_G, S5_GH * S5_P)), ("s5_c_im", g_c_im.reshape(S5_G, S5_GH * S5_P)),
        ("ret_norm_w", g_ret_norm.reshape(RET_H, RET_DV)), ("gla_norm_w", g_gla_norm.reshape(GLA_H, GLA_DV)),
        ("meta", g_meta_part), ("gla_w_gate", g_wgate)]
    sp_pack = jnp.concatenate([_rows1024(p) for _, p in small_pieces], axis=0)
    srow = sp_pack.shape[0]
    small_tr = _tile(srow, 128, 8)
    sp_all = _all_gather(sp_pack, "gather_grads").reshape(N_DEV * srow, PACK_COLS)
    snb = srow // small_tr

    def sum8(i, *blocks):
        acc = blocks[0]
        for b in blocks[1:]:
            acc = acc + b
        return (acc,), ()

    (sp_sum,), _ = _rows(sum8, [_win(sp_all, roff=-d * snb) for d in range(N_DEV)], [], [(PACK_COLS, F32)], [],
                         name="sum_small", nrow=srow, tr=small_tr)
    small, o = {}, 0
    for name_, p in small_pieces:
        r8 = _rows1024(p).shape[0]
        small[name_] = _unrows1024(sp_sum[o:o + r8], *p.shape)
        o += r8
    small["norm_ab_w"], small["final_norm_w"], small["norm_c_w"] = (small["vec2048"][i:i + 1] for i in range(3))
    small["s5_d"], small["gla_b_gate"] = small["vec1024"][0:1], small["vec1024"][1:2]
    loss = small["vec1024"][2, 0]
    small["s5_lam_re"], small["s5_lam_im"] = small["lam3"][:, :S5_P], small["lam3"][:, 128:128 + S5_P]
    small["s5_log_dt"] = small["lam3"][:, 256:257]

    def my_cols(g, n):
        return lax.dynamic_slice_in_dim(g, dev * n, n, axis=g.ndim - 1)

    grads = dict(
        meta=my_cols(small["meta"], D_MODEL // N_DEV),
        norm_ab_w=small["norm_ab_w"], w_in_ab=big_grads["w_in_ab"][None], ret_norm_w=small["ret_norm_w"].reshape(1, RET_W),
        s5_lam_re=small["s5_lam_re"][None], s5_lam_im=small["s5_lam_im"][None],
        s5_log_dt=small["s5_log_dt"].reshape(1, S5_G),
        s5_b_re=small["s5_b_re"].reshape(1, S5_G, S5_P, S5_GH), s5_b_im=small["s5_b_im"].reshape(1, S5_G, S5_P, S5_GH),
        s5_c_re=small["s5_c_re"][None], s5_c_im=small["s5_c_im"][None], s5_d=small["s5_d"],
        s5_w_glu=big_grads["s5_w_glu"][None], w_out_ab=big_grads["w_out_ab"][None],
        norm_c_w=my_cols(small["norm_c_w"], D_MODEL // N_DEV), w_in_c=big_grads["w_in_c"][None],
        gla_w_gate=my_cols(small["gla_w_gate"], GLA_QK // N_DEV)[None],
        gla_b_gate=my_cols(small["gla_b_gate"], GLA_QK // N_DEV),
        gla_norm_w=my_cols(small["gla_norm_w"].reshape(1, GLA_W), GLA_W // N_DEV),
        w_out_c=big_grads["w_out_c"][None], final_norm_w=small["final_norm_w"].reshape(D_MODEL))

    deltas, new_m, new_v = {}, {}, {}
    for k in order:
        w = weights[k]
        d2, m2, v2 = _adamw(_as2d(w), _as2d(grads[k].reshape(w.shape)), _as2d(mom_m[k]), _as2d(mom_v[k]), "adamw_" + k)
        deltas[k], new_m[k], new_v[k] = d2.reshape(w.shape), m2.reshape(w.shape), v2.reshape(w.shape)
        grads[k] = grads[k].reshape(w.shape)

    return (loss, grad_x, *[grads[k] for k in order], *[deltas[k] for k in order],
            *[new_m[k] for k in order], *[new_v[k] for k in order])
```

```python
import functools
import math

import jax
import jax.numpy as jnp
from jax import lax
from jax.experimental import pallas as pl
from jax.experimental.pallas import tpu as pltpu

F32, BF16 = jnp.float32, jnp.bfloat16
MESH = pl.DeviceIdType.MESH
N_DEV = 8

D_MODEL = 2048
CHUNK = 128
N_META = 16
PAD = CHUNK - N_META
SUB = 16
EPS = 1e-6
RET_H, RET_DK, RET_DV = 8, 128, 256
RET_QK, RET_W = RET_H * RET_DK, RET_H * RET_DV
ROPE_BASE = 10000.0
S5_W, S5_G, S5_P, S5_GH = 1024, 64, 64, 16
S5_N = S5_G * S5_P
GLA_H, GLA_DK, GLA_DV, GLA_RANK, GLA_TAU = 4, 256, 512, 16, 16.0
GLA_QK, GLA_W = GLA_H * GLA_DK, GLA_H * GLA_DV
IN_AB = 2 * RET_QK + 2 * RET_W + 2 * S5_W
OUT_AB = RET_W + S5_W
IN_C = 2 * GLA_QK + 2 * GLA_W + GLA_RANK
GATE_PAD = 256
IN_C_PAD = 2 * GLA_QK + 2 * GLA_W + GATE_PAD
ADAM_LR, ADAM_B1, ADAM_B2, ADAM_EPS, ADAM_WD, ADAM_STEP = 0.001, 0.9, 0.999, 1e-08, 0.01, 10

VMEM_LIMIT_BYTES = 48 * 2 ** 20
PACK_COLS = 1024
PACK_ROW_MULT = 8
SHARD_C = IN_C // N_DEV
WIN_STEP = 768
WIN_COLS = 1024


def _params(sem):
    return pltpu.CompilerParams(dimension_semantics=sem, vmem_limit_bytes=VMEM_LIMIT_BYTES)


def _tile(n, cap, mult):
    best = None
    for t in range(mult, min(n, cap) + 1, mult):
        if n % t == 0:
            best = t
    assert best is not None, (n, cap, mult)
    return best


def _dg(a, b, ca, cb):
    return lax.dot_general(a.astype(BF16), b.astype(BF16), (((ca,), (cb,)), ((), ())),
                           preferred_element_type=F32)


@functools.partial(jax.custom_vjp, nondiff_argnums=(2, 3))
def _bdot(a, b, ca, cb):
    return _dg(a, b, ca, cb)


def _bdot_fwd(a, b, ca, cb):
    return _dg(a, b, ca, cb), (a, b)


def _bdot_bwd(ca, cb, res, g):
    a, b = res
    da = _dg(g, b, 1, 1 - cb) if ca == 1 else _dg(b, g, 1 - cb, 1)
    db = _dg(a, g, 1 - ca, 0) if cb == 0 else _dg(g, a, 0, 1 - ca)
    return da.astype(a.dtype), db.astype(b.dtype)


_bdot.defvjp(_bdot_fwd, _bdot_bwd)


def _sigmoid(x):
    return 1.0 / (1.0 + jnp.exp(-x))


def _silu(x):
    return x * _sigmoid(x)


def _log_sigmoid(x):
    return jnp.minimum(x, 0.0) - jnp.log(1.0 + jnp.exp(-jnp.abs(x)))


def _gelu(x):
    return 0.5 * x * (1.0 + jnp.tanh(math.sqrt(2.0 / math.pi) * (x + 0.044715 * (x * x * x))))


def _rms(x, w):
    return x * lax.rsqrt(jnp.mean(x * x, axis=-1, keepdims=True) + EPS) * w


class _Hook:
    def __init__(self, ins, outs, sems, phases):
        self.ins, self.outs, self.sems, self.phases = list(ins), list(outs), list(sems), list(phases)


def _merge_hooks(first, second):
    ni, no, ns = len(first.ins), len(first.outs), len(first.sems)
    phases = [(f, lambda i, o, s, fn=fn: fn(i[:ni], o[:no], s[:ns])) for f, fn in first.phases]
    phases += [(f, lambda i, o, s, fn=fn: fn(i[ni:], o[no:], s[ns:])) for f, fn in second.phases]
    return _Hook(first.ins + second.ins, first.outs + second.outs, first.sems + second.sems,
                 sorted(phases, key=lambda p: p[0]))


_NO_HOOK = _Hook([], [], [], [])
_ANY = pl.BlockSpec(memory_space=pl.ANY)


def _run_hook(hook, lin, total, in_refs, out_refs, sem_refs):
    for frac, fn in hook.phases:
        at = min(int(frac * total), total - 1)

        @pl.when(lin == at)
        def _(fn=fn):
            fn(in_refs, out_refs, sem_refs)


def _mm_core(a, b, *, dims, grid, a_spec, b_spec, o_spec, out_shape, acc_shape, name, extra=(), hook=None):
    nk = grid[2]
    n_extra = len(extra)
    hook = _NO_HOOK if hook is None else hook
    hi, ho = len(hook.ins), len(hook.outs)

    def body(*refs):
        a_ref, b_ref = refs[0], refs[1]
        o_ref, acc = refs[2 + n_extra + hi], refs[3 + n_extra + hi + ho]
        k = pl.program_id(2)
        lin = (pl.program_id(0) * grid[1] + pl.program_id(1)) * nk + k
        _run_hook(hook, lin, grid[0] * grid[1] * nk, refs[2 + n_extra:2 + n_extra + hi],
                  refs[3 + n_extra + hi:3 + n_extra + hi + ho], refs[4 + n_extra + hi + ho:])

        part = lax.dot_general(a_ref[...].astype(BF16), b_ref[...].astype(BF16), dims, preferred_element_type=F32)

        def finish(r):
            for e in range(n_extra):
                r = r + refs[2 + e][...].astype(F32)
            o_ref[...] = r.astype(o_ref.dtype)

        if nk == 1:
            finish(part)
        else:
            @pl.when(k == 0)
            def _():
                acc[...] = part

            @pl.when(k > 0)
            def _():
                acc[...] += part

            @pl.when(k == nk - 1)
            def _():
                finish(acc[...])

    res = pl.pallas_call(
        body, name=name, grid=grid,
        in_specs=[a_spec, b_spec] + [sp for _, sp in extra] + [_ANY] * hi,
        out_specs=[o_spec] + [_ANY] * ho, out_shape=[out_shape] + hook.outs,
        scratch_shapes=[pltpu.VMEM(acc_shape if nk > 1 else (8, 128), F32)] + hook.sems,
        compiler_params=_params(("arbitrary", "arbitrary", "arbitrary")),
    )(a, b, *[arr for arr, _ in extra], *hook.ins)
    return res[0] if hook is _NO_HOOK else (res[0], res[1:])


NN, NT, TN = (((1,), (0,)), ((), ())), (((1,), (1,)), ((), ())), (((0,), (0,)), ((), ()))


FULL_K = 2048


def _mm(a, b, mode, *, name, out_dtype=F32, a_win=None, add=None, bias=None, hook=None, b_dev=False,
        out_dest=False):
    if mode == "tn":
        kdim, n = a.shape[0], b.shape[1]
        m = a.shape[1] if a_win is None else a_win[1]
        tm, tn, tk = _tile(m, 1024, 128), _tile(n, 1024, 128), _tile(kdim, 1408, 8)
        off = 0 if a_win is None else a_win[0] // tm
        a_spec = pl.BlockSpec((tk, tm), lambda i, j, k: (k, i + off))
        b_spec = pl.BlockSpec((tk, tn), lambda i, j, k: (k, j))
        dims = TN
    else:
        m = a.shape[0]
        kdim = a.shape[1] if a_win is None else a_win[1]
        if b_dev:
            n = b.shape[0] * b.shape[2] if mode == "nn" else b.shape[1]
        else:
            n = b.shape[1] if mode == "nn" else b.shape[0]
        tm = _tile(m, 1408, 8)
        if kdim <= FULL_K:
            tn, tk = _tile(n, 640, 128), kdim
        else:
            tn, tk = _tile(n, 1024, 128), _tile(kdim, 1024, 128)
        off = 0 if a_win is None else a_win[0] // tk
        a_spec = pl.BlockSpec((tm, tk), lambda i, j, k: (i, k + off))
        if mode == "nn":
            dims = NN
            if b_dev:
                per = PACK_COLS // tn
                b_spec = pl.BlockSpec((None, tk, tn), lambda i, j, k: (j // per, k, j % per))
            else:
                b_spec = pl.BlockSpec((tk, tn), lambda i, j, k: (k, j))
        else:
            dims = NT
            if b_dev:
                per = PACK_COLS // tk
                b_spec = pl.BlockSpec((None, tn, tk), lambda i, j, k: (k // per, j, k % per))
            else:
                b_spec = pl.BlockSpec((tn, tk), lambda i, j, k: (j, k))
    if a_win is not None:
        assert a_win[0] % (tm if mode == "tn" else tk) == 0
    extra = []
    if add is not None:
        extra.append((add, pl.BlockSpec((tm, tn), lambda i, j, k: (i, j))))
    if bias is not None:
        extra.append((bias, pl.BlockSpec((1, tn), lambda i, j, k: (0, j))))
    if out_dest:
        per = PACK_COLS // tn
        o_spec = pl.BlockSpec((None, tm, tn), lambda i, j, k: (j // per, i, j % per))
        out_shape = jax.ShapeDtypeStruct((n // PACK_COLS, m, PACK_COLS), out_dtype)
    else:
        o_spec = pl.BlockSpec((tm, tn), lambda i, j, k: (i, j))
        out_shape = jax.ShapeDtypeStruct((m, n), out_dtype)
    return _mm_core(a, b, dims=dims, grid=(m // tm, n // tn, kdim // tk), a_spec=a_spec, b_spec=b_spec,
                    o_spec=o_spec, out_shape=out_shape, acc_shape=(tm, tn), name=name, extra=extra, hook=hook)


def _win(arr, col0=0, width=None, roff=0):
    return (arr, col0, arr.shape[1] if width is None else width, roff)


def _rows(fn, rows, consts, outs, accs, *, name, nrow, tr=CHUNK):
    nr, nc, no = len(rows), len(consts), len(outs)

    def body(*refs):
        i = pl.program_id(0)
        ins = [r[...] for r in refs[:nr + nc]]
        o_refs = refs[nr + nc:nr + nc + no]
        a_refs = refs[nr + nc + no:]
        res_o, res_a = fn(i, *ins)
        for r, v in zip(o_refs, res_o):
            r[...] = v.astype(r.dtype)
        if a_refs:
            @pl.when(i == 0)
            def _():
                for r in a_refs:
                    r[...] = jnp.zeros_like(r)

            for r, v in zip(a_refs, res_a):
                r[...] += v

    in_specs = []
    for (arr, col0, width, roff) in rows:
        assert col0 % width == 0 and arr.shape[0] % tr == 0
        in_specs.append(pl.BlockSpec((tr, width), lambda i, c=col0 // width, ro=roff: (jnp.maximum(i - ro, 0), c)))
    for c in consts:
        in_specs.append(pl.BlockSpec(c.shape, lambda i, nd=c.ndim: (0,) * nd))
    out_specs = [pl.BlockSpec((tr, w), lambda i: (i, 0)) for (w, _) in outs]
    out_specs += [pl.BlockSpec(s, lambda i, nd=len(s): (0,) * nd) for s in accs]
    out_shape = [jax.ShapeDtypeStruct((nrow, w), dt) for (w, dt) in outs]
    out_shape += [jax.ShapeDtypeStruct(s, F32) for s in accs]
    res = pl.pallas_call(
        body, name=name, grid=(nrow // tr,), in_specs=in_specs, out_specs=out_specs, out_shape=out_shape,
        compiler_params=_params(("arbitrary",)),
    )(*[r[0] for r in rows], *consts)
    return res[:no], res[no:]


HEADS_PER_STEP = 2


def _scan_specs(xs, cs, ws, ks, chunk_of, hpb):
    specs = []
    for (arr, width, colfn) in xs:
        specs.append(pl.BlockSpec((CHUNK, width * hpb), lambda h, n, f=colfn: (chunk_of(n), f(h * hpb) // hpb)))
    for (arr, width, colfn) in cs:
        specs.append(pl.BlockSpec((CHUNK, width), lambda h, n, f=colfn: (chunk_of(n), f(h))))
    for arr in list(ws) + list(ks):
        specs.append(pl.BlockSpec((hpb, 1, arr.shape[2]), lambda h, n: (h, 0, 0)))
    return specs


def _scan_fwd(fn, xs, cs, ws, ks, *, heads, nchunk, s_shape, out_w, name, pre=None, hook=None):
    nx, ncs, nw = len(xs), len(cs), len(ws)
    hook = _NO_HOOK if hook is None else hook
    hi, ho = len(hook.ins), len(hook.outs)
    hpb = HEADS_PER_STEP
    hblocks = heads // hpb

    def body(*refs):
        n = pl.program_id(1)
        nin = nx + ncs + nw + len(ks)
        y_ref, sp_ref = refs[nin + hi], refs[nin + hi + 1]
        s_scr = refs[nin + hi + 2 + ho]
        _run_hook(hook, pl.program_id(0) * nchunk + n, hblocks * nchunk, refs[nin:nin + hi],
                  refs[nin + hi + 2:nin + hi + 2 + ho], refs[nin + hi + 3 + ho:])

        @pl.when(n == 0)
        def _():
            s_scr[...] = jnp.zeros_like(s_scr)

        cv = [r[...] for r in refs[nx:nx + ncs]]
        for e in range(hpb):
            state = s_scr[e]
            sp_ref[e, 0] = state
            xv = [r[:, e * w:(e + 1) * w] for r, (_, w, _) in zip(refs[:nx], xs)]
            wv = [r[e] for r in refs[nx + ncs:nx + ncs + nw]]
            kv = [r[e] for r in refs[nx + ncs + nw:nin]]
            if pre is not None:
                xv = pre(xv, cv)
            y, s_new = fn(n, xv, state, cv, wv, kv)
            y_ref[:, e * out_w:(e + 1) * out_w] = y.astype(y_ref.dtype)
            s_scr[e] = s_new

    lp = nchunk * CHUNK
    res = pl.pallas_call(
        body, name=name, grid=(hblocks, nchunk),
        in_specs=_scan_specs(xs, cs, ws, ks, lambda n: n, hpb) + [_ANY] * hi,
        out_specs=[pl.BlockSpec((CHUNK, out_w * hpb), lambda h, n: (n, h)),
                   pl.BlockSpec((hpb, 1) + s_shape, lambda h, n: (h, n, 0, 0))] + [_ANY] * ho,
        out_shape=[jax.ShapeDtypeStruct((lp, heads * out_w), BF16),
                   jax.ShapeDtypeStruct((heads, nchunk) + s_shape, F32)] + hook.outs,
        scratch_shapes=[pltpu.VMEM((hpb,) + s_shape, F32)] + hook.sems,
        compiler_params=_params(("arbitrary", "arbitrary")),
    )(*[t[0] for t in xs], *[t[0] for t in cs], *ws, *ks, *hook.ins)
    return (res[0], res[1]) if hook is _NO_HOOK else (res[0], res[1], res[2:])


def _scan_bwd(fn, xs, cs, ws, ks, dy, sprev, *, heads, nchunk, s_shape, out_w, name, pre=None, post=None,
              hook=None):
    nx, ncs, nw = len(xs), len(cs), len(ws)
    nin = nx + ncs + nw + len(ks)
    hook = _NO_HOOK if hook is None else hook
    hi, ho = len(hook.ins), len(hook.outs)
    hpb = HEADS_PER_STEP
    hblocks = heads // hpb

    def body(*refs):
        step = pl.program_id(1)
        n = nchunk - 1 - step
        dy_ref, sp_ref = refs[nin], refs[nin + 1]
        o0 = nin + 2 + hi
        dx_refs = refs[o0:o0 + nx]
        dw_refs = refs[o0 + nx:o0 + nx + nw]
        ds_scr = refs[o0 + nx + nw + ho]
        _run_hook(hook, pl.program_id(0) * nchunk + step, hblocks * nchunk, refs[nin + 2:o0],
                  refs[o0 + nx + nw:o0 + nx + nw + ho], refs[o0 + nx + nw + ho + 1:])

        @pl.when(step == 0)
        def _():
            ds_scr[...] = jnp.zeros_like(ds_scr)
            for r in dw_refs:
                r[...] = jnp.zeros_like(r)

        cv = [r[...] for r in refs[nx:nx + ncs]]
        for e in range(hpb):
            xv = [r[:, e * w:(e + 1) * w] for r, (_, w, _) in zip(refs[:nx], xs)]
            wv = [r[e] for r in refs[nx + ncs:nx + ncs + nw]]
            kv = [r[e] for r in refs[nx + ncs + nw:nin]]
            if pre is not None:
                xv = pre(xv, cv)
            _, vjp = jax.vjp(lambda xs_, s_, ws_, kv=kv: fn(n, xs_, s_, cv, ws_, kv), xv, sp_ref[e, 0], wv)
            dxs, ds_prev, dws = vjp((dy_ref[:, e * out_w:(e + 1) * out_w].astype(F32), ds_scr[e]))
            if post is not None:
                dxs = post(dxs, cv)
            for r, v, (_, w, _) in zip(dx_refs, dxs, xs):
                r[:, e * w:(e + 1) * w] = v.astype(r.dtype)
            for r, v in zip(dw_refs, dws):
                r[e] += v
            ds_scr[e] = ds_prev

    lp = nchunk * CHUNK
    rev = lambda n: nchunk - 1 - n
    in_specs = _scan_specs(xs, cs, ws, ks, rev, hpb)
    in_specs.append(pl.BlockSpec((CHUNK, out_w * hpb), lambda h, n: (rev(n), h)))
    in_specs.append(pl.BlockSpec((hpb, 1) + s_shape, lambda h, n: (h, rev(n), 0, 0)))
    out_specs = [pl.BlockSpec((CHUNK, w * hpb), lambda h, n: (rev(n), h)) for (_, w, _) in xs]
    out_specs += [pl.BlockSpec((hpb, 1, w.shape[2]), lambda h, n: (h, 0, 0)) for w in ws]
    out_shape = [jax.ShapeDtypeStruct((lp, heads * w), BF16) for (_, w, _) in xs]
    out_shape += [jax.ShapeDtypeStruct(w.shape, F32) for w in ws]
    res = pl.pallas_call(
        body, name=name, grid=(hblocks, nchunk), in_specs=in_specs + [_ANY] * hi,
        out_specs=out_specs + [_ANY] * ho, out_shape=out_shape + hook.outs,
        scratch_shapes=[pltpu.VMEM((hpb,) + s_shape, F32)] + hook.sems,
        compiler_params=_params(("arbitrary", "arbitrary")),
    )(*[t[0] for t in xs], *[t[0] for t in cs], *ws, *ks, dy, sprev, *hook.ins)
    if hook is _NO_HOOK:
        return res[:nx], res[nx:]
    return res[:nx], res[nx:nx + nw], res[nx + nw:]


def _iota2(shape, dim):
    return lax.broadcasted_iota(jnp.int32, shape, dim)


def _ret_chunk(n, xs, state, cs, ws, ks):
    q, k, v, z = xs
    (w,), (lg,) = ws, ks
    lgc = lg[:, :1]
    row, col = _iota2((CHUNK, CHUNK), 0), _iota2((CHUNK, CHUNK), 1)
    diff = jnp.maximum(row - col, 0).astype(F32)
    decay = jnp.where(row >= col, jnp.exp(lg * diff), 0.0)
    scores = _bdot(q, k, 1, 1) * decay
    o_intra = _bdot(scores, v, 1, 0)
    idx = _iota2((CHUNK, 1), 0).astype(F32)
    k_w = k * jnp.exp(lgc * (CHUNK - 1.0 - idx))
    kv = _bdot(k_w, v, 0, 0)
    s_new = state * jnp.exp(lgc * float(CHUNK)) + kv
    q_w = q * jnp.exp(lgc * (idx + 1.0))
    o = o_intra + _bdot(q_w, state, 1, 0)
    return _rms(o, w) * _silu(z), s_new


def _rope(t, cos2, sin2):
    return t * cos2 + pltpu.roll(t, RET_DK // 2, 1) * sin2


def _rope_t(g, cos2, sin2):
    return g * cos2 - pltpu.roll(g, RET_DK // 2, 1) * sin2


def _ret_pre(xv, cv):
    q, k, v, z = xv
    cos2, sin2 = cv
    return [_rope(q, cos2, sin2), _rope(k, cos2, sin2) * (RET_DK ** -0.5), v, z]


def _ret_post(dxs, cv):
    dq, dk, dv, dz = dxs
    cos2, sin2 = cv
    return [_rope_t(dq, cos2, sin2), _rope_t(dk, cos2, sin2) * (RET_DK ** -0.5), dv, dz]


def _gla_chunk(n, xs, state_t, cs, ws, ks):
    q, k, v, z, pre = xs
    (w,) = ws
    q = q * (GLA_DK ** -0.5)
    rowc = _iota2((CHUNK, 1), 0)
    valid = jnp.logical_or(n > 0, rowc >= PAD)
    log_a = jnp.where(valid, _log_sigmoid(pre) / GLA_TAU, 0.0)
    row, col = _iota2((CHUNK, CHUNK), 0), _iota2((CHUNK, CHUNK), 1)
    tri = (row >= col).astype(F32)
    b = jnp.dot(tri, log_a, precision=lax.Precision.HIGHEST, preferred_element_type=F32)
    b_last = b[CHUNK - 1:CHUNK, :]
    kv_t = _bdot(v, k * jnp.exp(b_last - b), 0, 0)
    s_new = state_t * jnp.exp(b_last) + kv_t
    o_inter = _bdot(q * jnp.exp(b), state_t, 1, 1)
    outs = []
    for s in range(CHUNK // SUB):
        lo, hi = s * SUB, (s + 1) * SUB
        b_ref = jnp.zeros_like(b_last) if s == 0 else b[lo - 1:lo, :]
        q_hat = q[lo:hi] * jnp.exp(b[lo:hi] - b_ref)
        k_hat = k[:hi] * jnp.exp(b_ref - b[:hi])
        sc = _bdot(q_hat, k_hat, 1, 1)
        causal = _iota2((SUB, hi), 0) + lo >= _iota2((SUB, hi), 1)
        outs.append(_bdot(jnp.where(causal, sc, 0.0), v[:hi], 1, 0))
    o = jnp.concatenate(outs, axis=0) + o_inter
    return _rms(o, w) * _silu(z), s_new


def _s5_disc(lam_re, lam_im, log_dt, b_re, b_im, expand):
    dt = jnp.exp(log_dt)
    mag = jnp.exp(lam_re * dt)
    ab_re, ab_im = mag * jnp.cos(lam_im * dt), mag * jnp.sin(lam_im * dt)
    den = lam_re * lam_re + lam_im * lam_im
    nr, ni = ab_re - 1.0, ab_im
    f_re = (nr * lam_re + ni * lam_im) / den
    f_im = (ni * lam_re - nr * lam_im) / den
    hp = lax.Precision.HIGHEST
    f_re = jnp.dot(f_re, expand, precision=hp, preferred_element_type=F32)
    f_im = jnp.dot(f_im, expand, precision=hp, preferred_element_type=F32)
    return ab_re, ab_im, f_re * b_re - f_im * b_im, f_re * b_im + f_im * b_re


def _s5_disc_fwd(args):
    def body(*refs):
        outs = _s5_disc(*[r[...] for r in refs[:6]])
        for r, v in zip(refs[6:], outs):
            r[...] = v

    g, p = args[0].shape
    return pl.pallas_call(
        body, name="s5_disc_fwd",
        out_shape=[jax.ShapeDtypeStruct((g, p), F32)] * 2 + [jax.ShapeDtypeStruct(args[3].shape, F32)] * 2,
    )(*args)


def _s5_disc_bwd(args, cts):
    def body(*refs):
        prim = [r[...] for r in refs[:5]]
        expand = refs[5][...]
        ct = tuple(r[...] for r in refs[6:10])
        _, vjp = jax.vjp(lambda *a: _s5_disc(*a, expand), *prim)
        for r, v in zip(refs[10:], vjp(ct)):
            r[...] = v

    return pl.pallas_call(
        body, name="s5_disc_bwd", out_shape=[jax.ShapeDtypeStruct(a.shape, F32) for a in args[:5]],
    )(*args, *cts)


S5_SUBL = 8
S5_LANES = S5_N // S5_SUBL
S5_TB = 64


def _s5_scan_fwd(bu, a_re, a_im):
    lp = bu.shape[0]

    def body(bu_ref, ar_ref, ai_ref, x_ref, st):
        @pl.when(pl.program_id(0) == 0)
        def _():
            st[...] = jnp.zeros_like(st)

        ar, ai = ar_ref[...], ai_ref[...]

        def step(t, carry):
            xr, xi = carry
            nr = ar * xr - ai * xi + bu_ref[t, 0:S5_SUBL, :]
            ni = ar * xi + ai * xr + bu_ref[t, S5_SUBL:2 * S5_SUBL, :]
            x_ref[t, 0:S5_SUBL, :] = nr
            x_ref[t, S5_SUBL:2 * S5_SUBL, :] = ni
            return nr, ni

        xr, xi = lax.fori_loop(0, S5_TB, step, (st[0], st[1]))
        st[0] = xr
        st[1] = xi

    blk = pl.BlockSpec((S5_TB, 2 * S5_SUBL, S5_LANES), lambda i: (i, 0, 0))
    cst = pl.BlockSpec((S5_SUBL, S5_LANES), lambda i: (0, 0))
    return pl.pallas_call(
        body, name="s5_scan_fwd", grid=(lp // S5_TB,), in_specs=[blk, cst, cst], out_specs=blk,
        out_shape=jax.ShapeDtypeStruct(bu.shape, F32),
        scratch_shapes=[pltpu.VMEM((2, S5_SUBL, S5_LANES), F32)],
        compiler_params=_params(("arbitrary",)),
    )(bu, a_re, a_im)


def _s5_scan_bwd(gx, x, a_re, a_im):
    lp = gx.shape[0]
    nb = lp // S5_TB

    def body(gx_ref, x_ref, xp_ref, ar_ref, ai_ref, g_ref, da_ref, st):
        i = pl.program_id(0)

        @pl.when(i == 0)
        def _():
            st[...] = jnp.zeros_like(st)
            da_ref[...] = jnp.zeros_like(da_ref)

        ar, ai = ar_ref[...], ai_ref[...]
        first = (i == nb - 1).astype(F32)

        def step(s, carry):
            gr, gi, dar, dai = carry
            t = S5_TB - 1 - s
            ngr = gx_ref[t, 0:S5_SUBL, :] + ar * gr + ai * gi
            ngi = gx_ref[t, S5_SUBL:2 * S5_SUBL, :] + ar * gi - ai * gr
            g_ref[t, 0:S5_SUBL, :] = ngr
            g_ref[t, S5_SUBL:2 * S5_SUBL, :] = ngi
            tp = jnp.maximum(t - 1, 0)
            at0 = (t == 0).astype(F32)
            keep = 1.0 - at0
            pr = keep * x_ref[tp, 0:S5_SUBL, :] + at0 * (1.0 - first) * xp_ref[0, 0:S5_SUBL, :]
            pi = keep * x_ref[tp, S5_SUBL:2 * S5_SUBL, :] + at0 * (1.0 - first) * xp_ref[0, S5_SUBL:2 * S5_SUBL, :]
            return ngr, ngi, dar + ngr * pr + ngi * pi, dai + ngi * pr - ngr * pi

        zero = jnp.zeros((S5_SUBL, S5_LANES), F32)
        gr, gi, dar, dai = lax.fori_loop(0, S5_TB, step, (st[0], st[1], zero, zero))
        st[0] = gr
        st[1] = gi
        da_ref[0] += dar
        da_ref[1] += dai

    rev = lambda i: nb - 1 - i
    blk = pl.BlockSpec((S5_TB, 2 * S5_SUBL, S5_LANES), lambda i: (rev(i), 0, 0))
    prev = pl.BlockSpec((1, 2 * S5_SUBL, S5_LANES), lambda i: (jnp.maximum(rev(i) * S5_TB - 1, 0), 0, 0))
    cst = pl.BlockSpec((S5_SUBL, S5_LANES), lambda i: (0, 0))
    return pl.pallas_call(
        body, name="s5_scan_bwd", grid=(nb,), in_specs=[blk, blk, prev, cst, cst],
        out_specs=[blk, pl.BlockSpec((2, S5_SUBL, S5_LANES), lambda i: (0, 0, 0))],
        out_shape=[jax.ShapeDtypeStruct(gx.shape, F32), jax.ShapeDtypeStruct((2, S5_SUBL, S5_LANES), F32)],
        scratch_shapes=[pltpu.VMEM((2, S5_SUBL, S5_LANES), F32)],
        compiler_params=_params(("arbitrary",)),
    )(gx, x, x, a_re, a_im)


def _place():
    x, y, c = lax.axis_index("x"), lax.axis_index("y"), lax.axis_index("c")
    return x, y, c, [(1 - x, y), (x, 1 - y), (1 - x, 1 - y)]


def _gather_phases():
    def plan(x_ref, out_ref, send_sems, recv_sems, local_sem):
        x, y, c, chips = _place()
        me, sibling = (x, y, c), (x, y, 1 - c)

        def rows(px, py, pc):
            return out_ref.at[4 * px + 2 * py + pc]

        def copy(k, block, to, src=None):
            return pltpu.make_async_remote_copy(
                src_ref=rows(*block) if src is None else src, dst_ref=rows(*block),
                send_sem=send_sems.at[k], recv_sem=recv_sems.at[k], device_id=to, device_id_type=MESH)

        mine = pltpu.make_async_copy(x_ref, rows(*me), local_sem)
        first = [copy(0, me, sibling, src=x_ref)]
        first += [copy(1 + j, me, (*chip, c), src=x_ref) for j, chip in enumerate(chips)]
        passed = [copy(4 + j, (*chip, c), sibling) for j, chip in enumerate(chips)]
        return c, chips, me, sibling, copy, mine, first, passed

    def start(ins, outs, sems):
        _, _, _, _, _, mine, first, _ = plan(ins[0], outs[0], *sems)
        mine.start()
        for cp in first:
            cp.start()

    def middle(ins, outs, sems):
        c, chips, me, _, copy, _, _, passed = plan(ins[0], outs[0], *sems)
        for j, chip in enumerate(chips):
            copy(1 + j, (*chip, c), me).wait_recv()
            passed[j].start()

    def finish(ins, outs, sems):
        c, chips, me, sibling, copy, mine, first, passed = plan(ins[0], outs[0], *sems)
        copy(0, sibling, me).wait_recv()
        for j, chip in enumerate(chips):
            copy(4 + j, (*chip, 1 - c), me).wait_recv()
        for cp in first + passed:
            cp.wait_send()
        mine.wait()

    return start, middle, finish


_GATHER_SEMS = [pltpu.SemaphoreType.DMA((7,)), pltpu.SemaphoreType.DMA((7,)), pltpu.SemaphoreType.DMA]


def _all_gather(shard, name):
    phases = _gather_phases()

    def body(x_ref, out_ref, *sems):
        for phase in phases:
            phase([x_ref], [out_ref], sems)

    return pl.pallas_call(
        body, name=name, out_shape=jax.ShapeDtypeStruct((N_DEV,) + shard.shape, shard.dtype),
        in_specs=[_ANY], out_specs=_ANY, scratch_shapes=list(_GATHER_SEMS),
    )(shard)


def _gather_hook(shard):
    start, middle, finish = _gather_phases()
    return _Hook([shard], [jax.ShapeDtypeStruct((N_DEV,) + shard.shape, shard.dtype)], _GATHER_SEMS,
                 [(0.0, start), (0.85, middle), (1.0, finish)])


def _swap_with_sibling(parts, name):
    def body(p_ref, out_ref, send_sems, recv_sems):
        x, y, c, _ = _place()
        copies = [pltpu.make_async_remote_copy(
            src_ref=p_ref.at[2 * chip + (1 - c)], dst_ref=out_ref.at[chip],
            send_sem=send_sems.at[chip], recv_sem=recv_sems.at[chip],
            device_id=(x, y, 1 - c), device_id_type=MESH) for chip in range(4)]
        for cp in copies:
            cp.start()
        for cp in copies:
            cp.wait()

    return pl.pallas_call(
        body, name=name, out_shape=jax.ShapeDtypeStruct((4,) + parts.shape[1:], parts.dtype),
        in_specs=[pl.BlockSpec(memory_space=pl.ANY)], out_specs=pl.BlockSpec(memory_space=pl.ANY),
        scratch_shapes=[pltpu.SemaphoreType.DMA((4,)), pltpu.SemaphoreType.DMA((4,))],
    )(parts)


def _chips_phases():
    def copies(p_ref, out_ref, send_sems, recv_sems):
        x, y, c, chips = _place()
        return [pltpu.make_async_remote_copy(
            src_ref=p_ref.at[2 * px + py], dst_ref=out_ref.at[j],
            send_sem=send_sems.at[j], recv_sem=recv_sems.at[j],
            device_id=(px, py, c), device_id_type=MESH) for j, (px, py) in enumerate(chips)]

    def start(ins, outs, sems):
        for cp in copies(ins[0], outs[0], *sems):
            cp.start()

    def finish(ins, outs, sems):
        for cp in copies(ins[0], outs[0], *sems):
            cp.wait()

    return start, finish


def _chips_hook(parts):
    start, finish = _chips_phases()
    return _Hook([parts], [jax.ShapeDtypeStruct((3,) + parts.shape[1:], parts.dtype)],
                 [pltpu.SemaphoreType.DMA((3,)), pltpu.SemaphoreType.DMA((3,))], [(0.0, start), (1.0, finish)])


def _pack_rows(n_elem, row_mult=PACK_ROW_MULT):
    rows = -(-n_elem // PACK_COLS)
    return -(-rows // row_mult) * row_mult


def _pack(flats, dtype, row_mult=PACK_ROW_MULT):
    flat = jnp.concatenate([f.reshape(-1).astype(dtype) for f in flats])
    rows = _pack_rows(flat.shape[0], row_mult)
    return jnp.pad(flat, (0, rows * PACK_COLS - flat.shape[0])).reshape(rows, PACK_COLS)


def _unpack(buf, shapes):
    lead = buf.shape[:-2]
    flat = buf.reshape(lead + (-1,))
    outs, o = [], 0
    for s in shapes:
        n = math.prod(s)
        outs.append(flat[..., o:o + n].reshape(lead + tuple(s)))
        o += n
    return outs


BIG_LAYOUT = (("w_in_ab", D_MODEL, PACK_COLS), ("s5_w_glu", S5_W // N_DEV, PACK_COLS),
              ("w_out_ab", OUT_AB // N_DEV, 2 * PACK_COLS), ("w_in_c", D_MODEL, PACK_COLS),
              ("w_out_c", GLA_W // N_DEV, 2 * PACK_COLS))


def _to_rows(a):
    if a.shape[-1] == PACK_COLS:
        return a
    assert a.shape[-1] == 2 * PACK_COLS
    return jnp.concatenate([a[..., :PACK_COLS], a[..., PACK_COLS:]], axis=-2)


def _from_rows(p, cols):
    if cols == PACK_COLS:
        return p
    r = p.shape[-2] // 2
    return jnp.concatenate([p[..., :r, :], p[..., r:, :]], axis=-1)


FIRST_LAYOUT = BIG_LAYOUT[:1]
OTHER_LAYOUT = BIG_LAYOUT[1:3] + BIG_LAYOUT[4:]
L0_LAYOUT, L1_LAYOUT = BIG_LAYOUT[1:3], BIG_LAYOUT[3:]


def _pack_big(pieces, layout):
    return jnp.concatenate([_to_rows(pieces[name]) for name, _, _ in layout], axis=-2)


def _unpack_big(buf, layout):
    out, o = {}, 0
    for name, rows, cols in layout:
        r = rows * cols // PACK_COLS
        out[name] = _from_rows(buf[..., o:o + r, :], cols)
        o += r
    return out


def _rows1024(a):
    r, c = a.shape
    if c > PACK_COLS:
        a = jnp.concatenate([a[:, i * PACK_COLS:(i + 1) * PACK_COLS] for i in range(c // PACK_COLS)], axis=0)
    elif c < PACK_COLS:
        a = jnp.pad(a, ((0, 0), (0, PACK_COLS - c)))
    return jnp.pad(a, ((0, -a.shape[0] % 8), (0, 0)))


def _unrows1024(p, r, c):
    if c > PACK_COLS:
        return jnp.concatenate([p[i * r:(i + 1) * r] for i in range(c // PACK_COLS)], axis=1)
    return p[:r, :c]


def _lane_select(a, off, sign, n_out, out_dtype, exact, name):
    rows, n_in = a.shape
    tr = _tile(rows, 256, 16)

    def body(off_ref, a_ref, o_ref):
        sel = _iota2((n_in, n_out), 0) + off_ref[0] * sign == _iota2((n_in, n_out), 1)
        if exact:
            r = jnp.dot(a_ref[...], sel.astype(F32), precision=lax.Precision.HIGHEST, preferred_element_type=F32)
        else:
            r = _dg(a_ref[...], sel.astype(BF16), 1, 0)
        o_ref[...] = r.astype(out_dtype)

    return pl.pallas_call(
        body, name=name, grid=(rows // tr,),
        in_specs=[pl.BlockSpec(memory_space=pltpu.SMEM), pl.BlockSpec((tr, n_in), lambda i: (i, 0))],
        out_specs=pl.BlockSpec((tr, n_out), lambda i: (i, 0)),
        out_shape=jax.ShapeDtypeStruct((rows, n_out), out_dtype),
        compiler_params=_params(("arbitrary",)),
    )(off, a)


def _adamw(w, g, m, v, name):
    rows, cols = w.shape
    tr = _tile(rows, 256, 8) if rows % 8 == 0 else rows

    def fn(i, w_, g_, m_, v_):
        m_new = ADAM_B1 * m_ + (1.0 - ADAM_B1) * g_
        v_new = ADAM_B2 * v_ + (1.0 - ADAM_B2) * (g_ * g_)
        m_hat = m_new / (1.0 - ADAM_B1 ** ADAM_STEP)
        v_hat = v_new / (1.0 - ADAM_B2 ** ADAM_STEP)
        delta = -ADAM_LR * (m_hat / (jnp.sqrt(v_hat) + ADAM_EPS) + ADAM_WD * w_)
        return (delta, m_new, v_new), ()

    outs, _ = _rows(fn, [_win(w), _win(g), _win(m), _win(v)], [], [(cols, F32)] * 3, [], name=name,
                    nrow=rows, tr=tr)
    return outs


def _as2d(a):
    if a.ndim == 1:
        return a.reshape(1, -1)
    if a.ndim == 2:
        return a
    a = a.reshape(a.shape[1:])
    return a if a.ndim == 2 else a.reshape(a.shape[0], -1)


def kernel(x, meta, norm_ab_w, w_in_ab, ret_norm_w, s5_lam_re, s5_lam_im, s5_log_dt, s5_b_re, s5_b_im, s5_c_re, s5_c_im, s5_d, s5_w_glu, w_out_ab, norm_c_w, w_in_c, gla_w_gate, gla_b_gate, gla_norm_w, w_out_c, final_norm_w, loss_target, m_meta, m_norm_ab_w, m_w_in_ab, m_ret_norm_w, m_s5_lam_re, m_s5_lam_im, m_s5_log_dt, m_s5_b_re, m_s5_b_im, m_s5_c_re, m_s5_c_im, m_s5_d, m_s5_w_glu, m_w_out_ab, m_norm_c_w, m_w_in_c, m_gla_w_gate, m_gla_b_gate, m_gla_norm_w, m_w_out_c, m_final_norm_w, v_meta, v_norm_ab_w, v_w_in_ab, v_ret_norm_w, v_s5_lam_re, v_s5_lam_im, v_s5_log_dt, v_s5_b_re, v_s5_b_im, v_s5_c_re, v_s5_c_im, v_s5_d, v_s5_w_glu, v_w_out_ab, v_norm_c_w, v_w_in_c, v_gla_w_gate, v_gla_b_gate, v_gla_norm_w, v_w_out_c, v_final_norm_w):
    weights = dict(meta=meta, norm_ab_w=norm_ab_w, w_in_ab=w_in_ab, ret_norm_w=ret_norm_w, s5_lam_re=s5_lam_re,
                   s5_lam_im=s5_lam_im, s5_log_dt=s5_log_dt, s5_b_re=s5_b_re, s5_b_im=s5_b_im, s5_c_re=s5_c_re,
                   s5_c_im=s5_c_im, s5_d=s5_d, s5_w_glu=s5_w_glu, w_out_ab=w_out_ab, norm_c_w=norm_c_w,
                   w_in_c=w_in_c, gla_w_gate=gla_w_gate, gla_b_gate=gla_b_gate, gla_norm_w=gla_norm_w,
                   w_out_c=w_out_c, final_norm_w=final_norm_w)
    mom_m = dict(meta=m_meta, norm_ab_w=m_norm_ab_w, w_in_ab=m_w_in_ab, ret_norm_w=m_ret_norm_w,
                 s5_lam_re=m_s5_lam_re, s5_lam_im=m_s5_lam_im, s5_log_dt=m_s5_log_dt, s5_b_re=m_s5_b_re,
                 s5_b_im=m_s5_b_im, s5_c_re=m_s5_c_re, s5_c_im=m_s5_c_im, s5_d=m_s5_d, s5_w_glu=m_s5_w_glu,
                 w_out_ab=m_w_out_ab, norm_c_w=m_norm_c_w, w_in_c=m_w_in_c, gla_w_gate=m_gla_w_gate,
                 gla_b_gate=m_gla_b_gate, gla_norm_w=m_gla_norm_w, w_out_c=m_w_out_c, final_norm_w=m_final_norm_w)
    mom_v = dict(meta=v_meta, norm_ab_w=v_norm_ab_w, w_in_ab=v_w_in_ab, ret_norm_w=v_ret_norm_w,
                 s5_lam_re=v_s5_lam_re, s5_lam_im=v_s5_lam_im, s5_log_dt=v_s5_log_dt, s5_b_re=v_s5_b_re,
                 s5_b_im=v_s5_b_im, s5_c_re=v_s5_c_re, s5_c_im=v_s5_c_im, s5_d=v_s5_d, s5_w_glu=v_s5_w_glu,
                 w_out_ab=v_w_out_ab, norm_c_w=v_norm_c_w, w_in_c=v_w_in_c, gla_w_gate=v_gla_w_gate,
                 gla_b_gate=v_gla_b_gate, gla_norm_w=v_gla_norm_w, w_out_c=v_w_out_c, final_norm_w=v_final_norm_w)
    order = list(weights)

    seq = x.shape[1]
    lp = CHUNK + seq
    nchunk = lp // CHUNK
    dev = 4 * lax.axis_index("x") + 2 * lax.axis_index("y") + lax.axis_index("c")
    core = lax.axis_index("c")
    chip = 2 * lax.axis_index("x") + lax.axis_index("y")

    win_off = jnp.reshape(2 * dev, (1,)).astype(jnp.int32)
    shard_c = jnp.pad(w_in_c[0].astype(BF16), ((0, 0), (0, 896 - SHARD_C)))
    big_shards = dict(w_in_ab=w_in_ab[0].astype(BF16), s5_w_glu=s5_w_glu[0].astype(BF16),
                      w_out_ab=w_out_ab[0].astype(BF16), w_out_c=w_out_c[0].astype(BF16),
                      w_in_c=_lane_select(shard_c, win_off, 1, WIN_COLS, BF16, False, "w_in_c_to_window"))
    def pad_to(a, rows, cols):
        return jnp.pad(a, ((0, rows - a.shape[0]), (0, cols - a.shape[1])))

    shard_w = D_MODEL // N_DEV
    small_pack = jnp.concatenate([meta, pad_to(norm_c_w, 8, shard_w), pad_to(gla_w_gate[0], GLA_RANK, shard_w),
                                  pad_to(gla_b_gate, 8, shard_w), pad_to(gla_norm_w, 8, shard_w)], axis=0)
    w_in_ab_g = _all_gather(big_shards["w_in_ab"], "gather_first")
    in_c_hook = _gather_hook(big_shards["w_in_c"])
    other_hook = _gather_hook(_pack_big(big_shards, OTHER_LAYOUT))
    gs = _all_gather(small_pack, "gather_small")
    gate_w = GLA_QK // N_DEV
    s_meta, s_norm_c = gs[:, :N_META], gs[:, N_META]
    s_wgate, s_bgate, s_gnorm = gs[:, 24:24 + GLA_RANK, :gate_w], gs[:, 40, :gate_w], gs[:, 48]
    meta_f = s_meta.transpose(1, 0, 2).reshape(N_META, D_MODEL)
    norm_c_f = s_norm_c.reshape(1, D_MODEL)
    w_gate_f = jnp.pad(s_wgate.transpose(1, 0, 2).reshape(GLA_RANK, GLA_QK), ((0, GATE_PAD - GLA_RANK), (0, 0)))
    b_gate_f = s_bgate.reshape(1, GLA_QK)
    gla_norm_f = s_gnorm.reshape(GLA_H, 1, GLA_DV)

    pos = jnp.maximum(jnp.arange(lp, dtype=F32) - float(PAD), 0.0)
    inv_freq = jnp.power(ROPE_BASE, -jnp.arange(0, RET_DK, 2, dtype=F32) / RET_DK)
    ang = pos[:, None] * inv_freq[None, :]
    cos2 = jnp.concatenate([jnp.cos(ang), jnp.cos(ang)], axis=1)
    sin2 = jnp.concatenate([-jnp.sin(ang), jnp.sin(ang)], axis=1)
    log_g = jnp.log1p(-jnp.exp2(-5.0 - jnp.arange(RET_H, dtype=F32)))
    lg = jnp.broadcast_to(log_g[:, None, None], (RET_H, 1, 128))
    ret_norm_h = ret_norm_w.reshape(RET_H, 1, RET_DV)

    h0 = jnp.concatenate([jnp.zeros((PAD, D_MODEL), F32), meta_f, x[0]], axis=0)

    def rowmask(i):
        return (_iota2((CHUNK, 1), 0) + i * CHUNK) >= PAD

    (hn0,), _ = _rows(lambda i, h, w: ((_rms(h, w),), ()), [_win(h0)], [norm_ab_w], [(D_MODEL, BF16)], [],
                      name="norm_ab_fwd", nrow=lp)
    proj_ab, (w_in_c_g,) = _mm(hn0, w_in_ab_g, "nn", name="in_ab_fwd", hook=in_c_hook, b_dev=True)
    w_in_c_f = sum(jnp.pad(w_in_c_g[d], ((0, 0), (WIN_STEP * d, IN_C_PAD - WIN_STEP * d - WIN_COLS)))
                   for d in range(N_DEV))

    q_off, k_off, v_off, za_off = 0, RET_QK, 2 * RET_QK, 2 * RET_QK + RET_W
    u_off, zb_off = 2 * RET_QK + 2 * RET_W, 2 * RET_QK + 2 * RET_W + S5_W
    ret_xs = [(proj_ab, RET_DK, lambda h: q_off // RET_DK + h), (proj_ab, RET_DK, lambda h: k_off // RET_DK + h),
              (proj_ab, RET_DV, lambda h: v_off // RET_DV + h), (proj_ab, RET_DV, lambda h: za_off // RET_DV + h)]
    ret_cs = [(cos2, RET_DK, lambda h: 0), (sin2, RET_DK, lambda h: 0)]
    ret_kw = dict(heads=RET_H, nchunk=nchunk, s_shape=(RET_DK, RET_DV), out_w=RET_DV, pre=_ret_pre)
    o_a, ret_sprev, (gathered_other,) = _scan_fwd(_ret_chunk, ret_xs, ret_cs, [ret_norm_h], [lg], name="ret_fwd",
                                                  hook=other_hook, **ret_kw)
    gb = _unpack_big(gathered_other, OTHER_LAYOUT)
    w_glu_f = gb["s5_w_glu"].reshape(S5_W, S5_W)
    w_out_ab_f = gb["w_out_ab"].reshape(OUT_AB, D_MODEL)
    w_out_c_f = gb["w_out_c"].reshape(GLA_W, D_MODEL)

    expand = jnp.repeat(jnp.eye(S5_P, dtype=F32), S5_GH, axis=1)
    disc_args = (s5_lam_re[0], s5_lam_im[0], s5_log_dt[0].reshape(S5_G, 1),
                 s5_b_re[0].reshape(S5_G, S5_P * S5_GH), s5_b_im[0].reshape(S5_G, S5_P * S5_GH), expand)
    ab_re, ab_im, bb_re, bb_im = _s5_disc_fwd(disc_args)
    gt = S5_SUBL
    eye_t = jnp.eye(gt, dtype=F32)

    def tiles_in(bb):
        return jnp.einsum("sgph,gk->sghkp", bb.reshape(gt, gt, S5_P, S5_GH), eye_t).reshape(gt, 128, S5_LANES)

    def tiles_out(cc):
        return jnp.einsum("sghp,gk->sgpkh", cc.reshape(gt, gt, S5_GH, S5_P), eye_t).reshape(gt, S5_LANES, 128)

    wb_t = jnp.concatenate([tiles_in(bb_re), tiles_in(bb_im)], axis=0).astype(BF16)
    wc_t = jnp.concatenate([tiles_out(s5_c_re[0]), -tiles_out(s5_c_im[0])], axis=0).astype(BF16)
    a_re, a_im = ab_re.reshape(S5_SUBL, S5_LANES), ab_im.reshape(S5_SUBL, S5_LANES)
    tm5, tk5, nt5 = _tile(lp, 1408, 8), _tile(lp, 1408, 8), 2 * gt
    u_blk = u_off // 128
    wide = pl.BlockSpec((tm5, S5_LANES), lambda i, j, k: (i, j))
    wide_k = pl.BlockSpec((tm5, S5_LANES), lambda i, j, k: (i, k * gt + j))
    narrow = pl.BlockSpec((tm5, 128), lambda i, j, k: (i, j))
    wb_j = pl.BlockSpec((None, 128, S5_LANES), lambda i, j, k: (j, 0, 0))
    wc_j = pl.BlockSpec((None, S5_LANES, 128), lambda i, j, k: (j, 0, 0))
    wb_k = pl.BlockSpec((None, 128, S5_LANES), lambda i, j, k: (k * gt + j, 0, 0))
    wc_k = pl.BlockSpec((None, S5_LANES, 128), lambda i, j, k: (k * gt + j, 0, 0))
    wide_shape = jax.ShapeDtypeStruct((lp, 2 * S5_N), F32)
    bu = _mm_core(proj_ab, wb_t, dims=NN, grid=(lp // tm5, nt5, 1), name="s5_bu",
                  a_spec=pl.BlockSpec((tm5, 128), lambda i, j, k: (i, u_blk + j % gt)), b_spec=wb_j,
                  o_spec=wide, out_shape=wide_shape, acc_shape=(tm5, S5_LANES))
    xs5 = _s5_scan_fwd(bu.reshape(lp, 2 * S5_SUBL, S5_LANES), a_re, a_im)
    xs5_2d = xs5.reshape(lp, 2 * S5_N)
    y_pre = _mm_core(xs5_2d, wc_t, dims=NN, grid=(lp // tm5, gt, 2), name="s5_cx", a_spec=wide_k, b_spec=wc_k,
                     o_spec=narrow, out_shape=jax.ShapeDtypeStruct((lp, S5_W), F32), acc_shape=(tm5, 128))
    (y_s5, yg_bf), _ = _rows(
        lambda i, yp, u, d: ((yp + d * u, _gelu(yp + d * u)), ()),
        [_win(y_pre), _win(proj_ab, u_off, S5_W)], [s5_d], [(S5_W, F32), (S5_W, BF16)], [], name="s5_gelu_fwd", nrow=lp)
    t_glu = _mm(yg_bf, w_glu_f, "nn", name="s5_glu_fwd")

    def s5_gate(y, t, zb):
        return _gelu(y) * _sigmoid(t) * _silu(zb)

    (o_b,), _ = _rows(lambda i, y, t, zb: ((s5_gate(y, t, zb),), ()),
                      [_win(y_s5), _win(t_glu), _win(proj_ab, zb_off, S5_W)], [], [(S5_W, BF16)], [],
                      name="s5_gate_fwd", nrow=lp)
    o_ab = jnp.concatenate([o_a, o_b], axis=1)
    h1 = _mm(o_ab, w_out_ab_f, "nn", name="out_ab_fwd", add=h0)

    (hn1,), _ = _rows(lambda i, h, w: ((_rms(h, w),), ()), [_win(h1)], [norm_c_f], [(D_MODEL, BF16)], [],
                      name="norm_c_fwd", nrow=lp)
    proj_c = _mm(hn1, w_in_c_f, "nn", name="in_c_fwd")
    gl_off = 2 * GLA_QK + 2 * GLA_W
    pre_gate = _mm(proj_c, w_gate_f, "nn", name="gate_fwd", a_win=(gl_off, GATE_PAD), bias=b_gate_f)
    gla_xs = [(proj_c, GLA_DK, lambda h: h), (proj_c, GLA_DK, lambda h: GLA_QK // GLA_DK + h),
              (proj_c, GLA_DV, lambda h: 2 * GLA_QK // GLA_DV + h),
              (proj_c, GLA_DV, lambda h: (2 * GLA_QK + GLA_W) // GLA_DV + h),
              (pre_gate, GLA_DK, lambda h: h)]
    gla_kw = dict(heads=GLA_H, nchunk=nchunk, s_shape=(GLA_DV, GLA_DK), out_w=GLA_DV)
    o_c, gla_sprev = _scan_fwd(_gla_chunk, gla_xs, [], [gla_norm_f], [], name="gla_fwd", **gla_kw)
    h2 = _mm(o_c, w_out_c_f, "nn", name="out_c_fwd", add=h1)

    fnw = final_norm_w.reshape(1, D_MODEL)

    def final_fn(i, h, tgt, w):
        def loss_of(h_, w_):
            err = _rms(h_, w_) - tgt
            return 0.5 * jnp.sum(jnp.mean(err * err, axis=-1))

        real = (i > 0).astype(F32)
        loss_i, (dh, dw) = jax.value_and_grad(loss_of, argnums=(0, 1))(h, w)
        return (dh * real,), (jnp.full((1, 128), loss_i * real, F32), dw * real)

    (dh2,), (loss_acc, g_final) = _rows(final_fn, [_win(h2), _win(loss_target[0], roff=1)], [fnw],
                                        [(D_MODEL, F32)], [(1, 128), (1, D_MODEL)], name="final_loss", nrow=lp)

    def rs_front(pieces, layout, tag):
        g_full = _pack_big(pieces, layout)
        prow = g_full.shape[1]
        from_sibling = _swap_with_sibling(g_full, "rs_sibling_" + tag)
        mine_by_chip = lax.dynamic_index_in_dim(g_full.reshape(4, 2, prow, PACK_COLS), core, axis=1, keepdims=False)
        (p1, p1_bf), _ = _rows(
            lambda i, a, b: ((a.astype(F32) + b.astype(F32), a.astype(F32) + b.astype(F32)), ()),
            [_win(mine_by_chip.reshape(4 * prow, PACK_COLS)), _win(from_sibling.reshape(4 * prow, PACK_COLS))], [],
            [(PACK_COLS, F32), (PACK_COLS, BF16)], [], name="rs_sum_sibling_" + tag, nrow=4 * prow,
            tr=_tile(prow, 512, 16))
        return p1.reshape(4, prow, PACK_COLS), p1_bf.reshape(4, prow, PACK_COLS)

    def rs_back(p1, from_chips, layout, tag):
        prow = p1.shape[1]
        tr = _tile(prow, 512, 16)
        own = lax.dynamic_index_in_dim(p1, chip, axis=0, keepdims=False)
        fc2 = from_chips.reshape(3 * prow, PACK_COLS)
        nblk = prow // tr
        (g_shard,), _ = _rows(
            lambda i, a, b0, b1, b2: ((((a + b0.astype(F32)) + b1.astype(F32)) + b2.astype(F32),), ()),
            [_win(own), _win(fc2), _win(fc2, roff=-nblk), _win(fc2, roff=-2 * nblk)], [], [(PACK_COLS, F32)], [],
            name="rs_sum_chips_" + tag, nrow=prow, tr=tr)
        return _unpack_big(g_shard, layout)

    dh2_bf = dh2.astype(BF16)
    do_c = _mm(dh2_bf, w_out_c_f, "nt", name="out_c_dx", out_dtype=BF16)
    gw_out_c = _mm(o_c, dh2_bf, "tn", name="out_c_dw", out_dtype=BF16)
    (dq_c, dk_c, dv_c, dz_c, dpre), (g_gla_norm,) = _scan_bwd(
        _gla_chunk, gla_xs, [], [gla_norm_f], [], do_c, gla_sprev, name="gla_bwd", **gla_kw)
    dglow = _mm(dpre, w_gate_f, "nt", name="gate_dx", out_dtype=BF16)
    g_wgate = _mm(proj_c, dpre, "tn", name="gate_dw", a_win=(gl_off, GATE_PAD))[:GLA_RANK]
    (), (g_bgate,) = _rows(lambda i, d: ((), (jnp.sum(d.astype(F32), axis=0, keepdims=True),)), [_win(dpre)], [], [],
                           [(1, GLA_QK)], name="gate_db", nrow=lp)
    dproj_c = jnp.concatenate([dq_c, dk_c, dv_c, dz_c, dglow], axis=1)
    dhn1 = _mm(dproj_c, w_in_c_f, "nt", name="in_c_dx")
    gw_in_c = _mm(hn1, dproj_c, "tn", name="in_c_dw", out_dtype=BF16)
    p1_l1, p1_l1_bf = rs_front(dict(
        w_in_c=jnp.stack([gw_in_c[:, WIN_STEP * d:WIN_STEP * d + WIN_COLS] for d in range(N_DEV)]),
        w_out_c=gw_out_c.reshape(N_DEV, GLA_W // N_DEV, D_MODEL)), L1_LAYOUT, "l1")

    def norm_bwd(i, h, dhn, dres, w):
        _, vjp = jax.vjp(_rms, h, w)
        dh, dw = vjp(dhn)
        return (jnp.where(rowmask(i), dh + dres, 0.0),), (dw,)

    (dh1,), (g_norm_c,) = _rows(norm_bwd, [_win(h1), _win(dhn1), _win(dh2)], [norm_c_f], [(D_MODEL, F32)],
                                [(1, D_MODEL)], name="norm_c_bwd", nrow=lp)

    dh1_bf = dh1.astype(BF16)
    do_ab = _mm(dh1_bf, w_out_ab_f, "nt", name="out_ab_dx", out_dtype=BF16)
    gw_out_ab = _mm(o_ab, dh1_bf, "tn", name="out_ab_dw", out_dtype=BF16)

    def s5_gate_bwd(i, dob, y, t, zb):
        _, vjp = jax.vjp(s5_gate, y, t, zb)
        dy, dt, dzb = vjp(dob.astype(F32))
        return (dy, dt, dzb), ()

    (dy_a, dt_glu, dzb), _ = _rows(
        s5_gate_bwd, [_win(do_ab, RET_W, S5_W), _win(y_s5), _win(t_glu), _win(proj_ab, zb_off, S5_W)], [],
        [(S5_W, F32), (S5_W, BF16), (S5_W, BF16)], [], name="s5_gate_bwd", nrow=lp)
    dyg2 = _mm(dt_glu, w_glu_f, "nt", name="s5_glu_dx")
    gw_glu = _mm(yg_bf, dt_glu, "tn", name="s5_glu_dw", out_dtype=BF16)

    def s5_y_bwd(i, dya, dyg, y, u, d):
        _, vjp = jax.vjp(_gelu, y)
        (dy_g,) = vjp(dyg)
        dy = dya + dy_g
        return (dy, d * dy), (jnp.sum(dy * u, axis=0, keepdims=True),)

    (dy_s5, du1), (g_d,) = _rows(
        s5_y_bwd, [_win(dy_a), _win(dyg2), _win(y_s5), _win(proj_ab, u_off, S5_W)], [s5_d],
        [(S5_W, BF16), (S5_W, F32)], [(1, S5_W)], name="s5_y_bwd", nrow=lp)
    gx = _mm_core(dy_s5, wc_t, dims=NT, grid=(lp // tm5, nt5, 1), name="s5_cx_dx",
                  a_spec=pl.BlockSpec((tm5, 128), lambda i, j, k: (i, j % gt)), b_spec=wc_j,
                  o_spec=wide, out_shape=wide_shape, acc_shape=(tm5, S5_LANES))
    rows_k = lambda col: pl.BlockSpec((tk5, col), lambda i, j, k: (k, i))
    gwc = _mm_core(xs5_2d, dy_s5, dims=TN, grid=(nt5, 1, lp // tk5), name="s5_cx_dw", a_spec=rows_k(S5_LANES),
                   b_spec=pl.BlockSpec((tk5, 128), lambda i, j, k: (k, i % gt)),
                   o_spec=pl.BlockSpec((None, S5_LANES, 128), lambda i, j, k: (i, 0, 0)),
                   out_shape=jax.ShapeDtypeStruct((nt5, S5_LANES, 128), F32), acc_shape=(S5_LANES, 128))
    g_s5, da = _s5_scan_bwd(gx.reshape(lp, 2 * S5_SUBL, S5_LANES), xs5, a_re, a_im)
    g_s5_2d = g_s5.reshape(lp, 2 * S5_N)
    du = _mm_core(g_s5_2d, wb_t, dims=NT, grid=(lp // tm5, gt, 2), name="s5_bu_dx", a_spec=wide_k, b_spec=wb_k,
                  o_spec=narrow, out_shape=jax.ShapeDtypeStruct((lp, S5_W), BF16), acc_shape=(tm5, 128),
                  extra=[(du1, narrow)])
    gwb = _mm_core(proj_ab, g_s5_2d, dims=TN, grid=(nt5, 1, lp // tk5), name="s5_bu_dw",
                   a_spec=pl.BlockSpec((tk5, 128), lambda i, j, k: (k, u_blk + i % gt)), b_spec=rows_k(S5_LANES),
                   o_spec=pl.BlockSpec((None, 128, S5_LANES), lambda i, j, k: (i, 0, 0)),
                   out_shape=jax.ShapeDtypeStruct((nt5, 128, S5_LANES), F32), acc_shape=(128, S5_LANES))
    gwc6 = gwc.reshape(2, gt, gt, S5_P, gt, S5_GH)
    g_c = jnp.einsum("rsgpgh->rsghp", gwc6).reshape(2, S5_G, S5_GH, S5_P)
    g_c_re, g_c_im = g_c[0], -g_c[1]
    gwb6 = gwb.reshape(2, gt, gt, S5_GH, gt, S5_P)
    d_bb = jnp.einsum("rsghgp->rsgph", gwb6).reshape(2, S5_G, S5_P * S5_GH)
    d_bb_re, d_bb_im = d_bb[0], d_bb[1]
    g_lam_re, g_lam_im, g_log_dt, g_b_re, g_b_im = _s5_disc_bwd(
        disc_args, (da[0].reshape(S5_G, S5_P), da[1].reshape(S5_G, S5_P), d_bb_re, d_bb_im))

    (dq_a, dk_a, dv_a, dz_a), (g_ret_norm,), (from_chips_l1,) = _scan_bwd(
        _ret_chunk, ret_xs, ret_cs, [ret_norm_h], [lg], do_ab, ret_sprev, name="ret_bwd", post=_ret_post,
        hook=_chips_hook(p1_l1_bf), **ret_kw)
    dproj_ab = jnp.concatenate([dq_a, dk_a, dv_a, dz_a, du, dzb], axis=1)
    p1_l0, p1_l0_bf = rs_front(dict(s5_w_glu=gw_glu.reshape(N_DEV, S5_W // N_DEV, S5_W),
                                    w_out_ab=gw_out_ab.reshape(N_DEV, OUT_AB // N_DEV, D_MODEL)), L0_LAYOUT, "l0")
    gw_in_ab, (from_chips_l0,) = _mm(hn0, dproj_ab, "tn", name="in_ab_dw", out_dest=True, out_dtype=BF16,
                                     hook=_chips_hook(p1_l0_bf))
    p1_first, p1_first_bf = rs_front(dict(w_in_ab=gw_in_ab), FIRST_LAYOUT, "first")

    lane = lambda a_: pad_to(a_, a_.shape[0], 128)

    def sum8(i, *blocks):
        acc = blocks[0]
        for b in blocks[1:]:
            acc = acc + b
        return (acc,), ()

    def pack_small(pieces):
        return jnp.concatenate([_rows1024(p) for _, p in pieces], axis=0)

    def sum_small(gathered, pieces, tag):
        srow = gathered.shape[1]
        tr = _tile(srow, 128, 8)
        flat = gathered.reshape(N_DEV * srow, PACK_COLS)
        (total,), _ = _rows(sum8, [_win(flat, roff=-d * (srow // tr)) for d in range(N_DEV)], [], [(PACK_COLS, F32)],
                            [], name="sum_small_" + tag, nrow=srow, tr=tr)
        out, o = {}, 0
        for name_, p in pieces:
            r8 = _rows1024(p).shape[0]
            out[name_] = _unrows1024(total[o:o + r8], *p.shape)
            o += r8
        return out

    early_pieces = [
        ("vec2048", jnp.concatenate([g_final, g_norm_c], axis=0)),
        ("vec1024", jnp.concatenate([g_d, g_bgate, pad_to(loss_acc[:, :1], 1, PACK_COLS)], axis=0)),
        ("lam3", jnp.concatenate([lane(g_lam_re), lane(g_lam_im), lane(g_log_dt)], axis=1)),
        ("s5_b_re", g_b_re), ("s5_b_im", g_b_im),
        ("s5_c_re", g_c_re.reshape(S5_G, S5_GH * S5_P)), ("s5_c_im", g_c_im.reshape(S5_G, S5_GH * S5_P)),
        ("ret_norm_w", g_ret_norm.reshape(RET_H, RET_DV)), ("gla_norm_w", g_gla_norm.reshape(GLA_H, GLA_DV)),
        ("gla_w_gate", g_wgate)]
    dhn0, (from_chips_first, early_all) = _mm(
        dproj_ab, w_in_ab_g, "nt", name="in_ab_dx", b_dev=True,
        hook=_merge_hooks(_chips_hook(p1_first_bf), _gather_hook(pack_small(early_pieces))))
    (dh0,), (g_norm_ab,) = _rows(norm_bwd, [_win(h0), _win(dhn0), _win(dh1)], [norm_ab_w], [(D_MODEL, F32)],
                                 [(1, D_MODEL)], name="norm_ab_bwd", nrow=lp)
    grad_x = dh0[CHUNK:][None]
    late_pieces = [("norm_ab_w", g_norm_ab), ("meta", dh0[PAD:CHUNK])]
    small = sum_small(early_all, early_pieces, "early")
    small.update(sum_small(_all_gather(pack_small(late_pieces), "gather_grads"), late_pieces, "late"))

    big_grads = {**rs_back(p1_l1, from_chips_l1, L1_LAYOUT, "l1"), **rs_back(p1_l0, from_chips_l0, L0_LAYOUT, "l0"),
                 **rs_back(p1_first, from_chips_first, FIRST_LAYOUT, "first")}
    big_grads["w_in_c"] = _lane_select(big_grads["w_in_c"], win_off, -1, 896, F32, True,
                                       "w_in_c_from_window")[:, :SHARD_C]
    small["final_norm_w"], small["norm_c_w"] = small["vec2048"][0:1], small["vec2048"][1:2]
    small["s5_d"], small["gla_b_gate"] = small["vec1024"][0:1], small["vec1024"][1:2]
    loss = small["vec1024"][2, 0]
    small["s5_lam_re"], small["s5_lam_im"] = small["lam3"][:, :S5_P], small["lam3"][:, 128:128 + S5_P]
    small["s5_log_dt"] = small["lam3"][:, 256:257]

    def my_cols(g, n):
        return lax.dynamic_slice_in_dim(g, dev * n, n, axis=g.ndim - 1)

    grads = dict(
        meta=my_cols(small["meta"], D_MODEL // N_DEV),
        norm_ab_w=small["norm_ab_w"], w_in_ab=big_grads["w_in_ab"][None], ret_norm_w=small["ret_norm_w"].reshape(1, RET_W),
        s5_lam_re=small["s5_lam_re"][None], s5_lam_im=small["s5_lam_im"][None],
        s5_log_dt=small["s5_log_dt"].reshape(1, S5_G),
        s5_b_re=small["s5_b_re"].reshape(1, S5_G, S5_P, S5_GH), s5_b_im=small["s5_b_im"].reshape(1, S5_G, S5_P, S5_GH),
        s5_c_re=small["s5_c_re"][None], s5_c_im=small["s5_c_im"][None], s5_d=small["s5_d"],
        s5_w_glu=big_grads["s5_w_glu"][None], w_out_ab=big_grads["w_out_ab"][None],
        norm_c_w=my_cols(small["norm_c_w"], D_MODEL // N_DEV), w_in_c=big_grads["w_in_c"][None],
        gla_w_gate=my_cols(small["gla_w_gate"], GLA_QK // N_DEV)[None],
        gla_b_gate=my_cols(small["gla_b_gate"], GLA_QK // N_DEV),
        gla_norm_w=my_cols(small["gla_norm_w"].reshape(1, GLA_W), GLA_W // N_DEV),
        w_out_c=big_grads["w_out_c"][None], final_norm_w=small["final_norm_w"].reshape(D_MODEL))

    deltas, new_m, new_v = {}, {}, {}
    for k in order:
        w = weights[k]
        d2, m2, v2 = _adamw(_as2d(w), _as2d(grads[k].reshape(w.shape)), _as2d(mom_m[k]), _as2d(mom_v[k]), "adamw_" + k)
        deltas[k], new_m[k], new_v[k] = d2.reshape(w.shape), m2.reshape(w.shape), v2.reshape(w.shape)
        grads[k] = grads[k].reshape(w.shape)

    return (loss, grad_x, *[grads[k] for k in order], *[deltas[k] for k in order],
            *[new_m[k] for k in order], *[new_v[k] for k in order])
```

```python
import functools
import math

import jax
import jax.numpy as jnp
from jax import lax
from jax.experimental import pallas as pl
from jax.experimental.pallas import tpu as pltpu

F32, BF16 = jnp.float32, jnp.bfloat16
MESH = pl.DeviceIdType.MESH
N_DEV = 8

D_MODEL = 2048
CHUNK = 128
N_META = 16
PAD = CHUNK - N_META
SUB = 16
EPS = 1e-6
RET_H, RET_DK, RET_DV = 8, 128, 256
RET_QK, RET_W = RET_H * RET_DK, RET_H * RET_DV
ROPE_BASE = 10000.0
S5_W, S5_G, S5_P, S5_GH = 1024, 64, 64, 16
S5_N = S5_G * S5_P
GLA_H, GLA_DK, GLA_DV, GLA_RANK, GLA_TAU = 4, 256, 512, 16, 16.0
GLA_QK, GLA_W = GLA_H * GLA_DK, GLA_H * GLA_DV
IN_AB = 2 * RET_QK + 2 * RET_W + 2 * S5_W
OUT_AB = RET_W + S5_W
IN_C = 2 * GLA_QK + 2 * GLA_W + GLA_RANK
GATE_PAD = 256
IN_C_PAD = 2 * GLA_QK + 2 * GLA_W + GATE_PAD
ADAM_LR, ADAM_B1, ADAM_B2, ADAM_EPS, ADAM_WD, ADAM_STEP = 0.001, 0.9, 0.999, 1e-08, 0.01, 10

VMEM_LIMIT_BYTES = 48 * 2 ** 20
PACK_COLS = 1024
PACK_ROW_MULT = 8
SHARD_C = IN_C // N_DEV
WIN_STEP = 768
WIN_COLS = 1024


def _params(sem):
    return pltpu.CompilerParams(dimension_semantics=sem, vmem_limit_bytes=VMEM_LIMIT_BYTES)


def _tile(n, cap, mult):
    best = None
    for t in range(mult, min(n, cap) + 1, mult):
        if n % t == 0:
            best = t
    assert best is not None, (n, cap, mult)
    return best


def _dg(a, b, ca, cb):
    return lax.dot_general(a.astype(BF16), b.astype(BF16), (((ca,), (cb,)), ((), ())),
                           preferred_element_type=F32)


@functools.partial(jax.custom_vjp, nondiff_argnums=(2, 3))
def _bdot(a, b, ca, cb):
    return _dg(a, b, ca, cb)


def _bdot_fwd(a, b, ca, cb):
    return _dg(a, b, ca, cb), (a, b)


def _bdot_bwd(ca, cb, res, g):
    a, b = res
    da = _dg(g, b, 1, 1 - cb) if ca == 1 else _dg(b, g, 1 - cb, 1)
    db = _dg(a, g, 1 - ca, 0) if cb == 0 else _dg(g, a, 0, 1 - ca)
    return da.astype(a.dtype), db.astype(b.dtype)


_bdot.defvjp(_bdot_fwd, _bdot_bwd)


def _sigmoid(x):
    return 1.0 / (1.0 + jnp.exp(-x))


def _silu(x):
    return x * _sigmoid(x)


def _log_sigmoid(x):
    return jnp.minimum(x, 0.0) - jnp.log(1.0 + jnp.exp(-jnp.abs(x)))


def _gelu(x):
    return 0.5 * x * (1.0 + jnp.tanh(math.sqrt(2.0 / math.pi) * (x + 0.044715 * (x * x * x))))


def _rms(x, w):
    return x * lax.rsqrt(jnp.mean(x * x, axis=-1, keepdims=True) + EPS) * w


class _Hook:
    def __init__(self, ins, outs, sems, phases):
        self.ins, self.outs, self.sems, self.phases = list(ins), list(outs), list(sems), list(phases)


def _merge_hooks(first, second):
    ni, no, ns = len(first.ins), len(first.outs), len(first.sems)
    phases = [(f, lambda i, o, s, fn=fn: fn(i[:ni], o[:no], s[:ns])) for f, fn in first.phases]
    phases += [(f, lambda i, o, s, fn=fn: fn(i[ni:], o[no:], s[ns:])) for f, fn in second.phases]
    return _Hook(first.ins + second.ins, first.outs + second.outs, first.sems + second.sems,
                 sorted(phases, key=lambda p: p[0]))


_NO_HOOK = _Hook([], [], [], [])
_ANY = pl.BlockSpec(memory_space=pl.ANY)


def _run_hook(hook, lin, total, in_refs, out_refs, sem_refs):
    for frac, fn in hook.phases:
        at = min(int(frac * total), total - 1)

        @pl.when(lin == at)
        def _(fn=fn):
            fn(in_refs, out_refs, sem_refs)


def _mm_core(a, b, *, dims, grid, a_spec, b_spec, o_spec, out_shape, acc_shape, name, extra=(), hook=None):
    nk = grid[2]
    n_extra = len(extra)
    hook = _NO_HOOK if hook is None else hook
    hi, ho = len(hook.ins), len(hook.outs)

    def body(*refs):
        a_ref, b_ref = refs[0], refs[1]
        o_ref, acc = refs[2 + n_extra + hi], refs[3 + n_extra + hi + ho]
        k = pl.program_id(2)
        lin = (pl.program_id(0) * grid[1] + pl.program_id(1)) * nk + k
        _run_hook(hook, lin, grid[0] * grid[1] * nk, refs[2 + n_extra:2 + n_extra + hi],
                  refs[3 + n_extra + hi:3 + n_extra + hi + ho], refs[4 + n_extra + hi + ho:])

        part = lax.dot_general(a_ref[...].astype(BF16), b_ref[...].astype(BF16), dims, preferred_element_type=F32)

        def finish(r):
            for e in range(n_extra):
                r = r + refs[2 + e][...].astype(F32)
            o_ref[...] = r.astype(o_ref.dtype)

        if nk == 1:
            finish(part)
        else:
            @pl.when(k == 0)
            def _():
                acc[...] = part

            @pl.when(k > 0)
            def _():
                acc[...] += part

            @pl.when(k == nk - 1)
            def _():
                finish(acc[...])

    res = pl.pallas_call(
        body, name=name, grid=grid,
        in_specs=[a_spec, b_spec] + [sp for _, sp in extra] + [_ANY] * hi,
        out_specs=[o_spec] + [_ANY] * ho, out_shape=[out_shape] + hook.outs,
        scratch_shapes=[pltpu.VMEM(acc_shape if nk > 1 else (8, 128), F32)] + hook.sems,
        compiler_params=_params(("arbitrary", "arbitrary", "arbitrary")),
    )(a, b, *[arr for arr, _ in extra], *hook.ins)
    return res[0] if hook is _NO_HOOK else (res[0], res[1:])


NN, NT, TN = (((1,), (0,)), ((), ())), (((1,), (1,)), ((), ())), (((0,), (0,)), ((), ()))


FULL_K = 2048


def _mm(a, b, mode, *, name, out_dtype=F32, a_win=None, add=None, bias=None, hook=None, b_dev=False,
        out_dest=False):
    if mode == "tn":
        kdim, n = a.shape[0], b.shape[1]
        m = a.shape[1] if a_win is None else a_win[1]
        tm, tn, tk = _tile(m, 1024, 128), _tile(n, 1024, 128), _tile(kdim, 1408, 8)
        off = 0 if a_win is None else a_win[0] // tm
        a_spec = pl.BlockSpec((tk, tm), lambda i, j, k: (k, i + off))
        b_spec = pl.BlockSpec((tk, tn), lambda i, j, k: (k, j))
        dims = TN
    else:
        m = a.shape[0]
        kdim = a.shape[1] if a_win is None else a_win[1]
        if b_dev:
            n = b.shape[0] * b.shape[2] if mode == "nn" else b.shape[1]
        else:
            n = b.shape[1] if mode == "nn" else b.shape[0]
        tm = _tile(m, 1408, 8)
        if kdim <= FULL_K:
            tn, tk = _tile(n, 640, 128), kdim
        else:
            tn, tk = _tile(n, 1024, 128), _tile(kdim, 1024, 128)
        off = 0 if a_win is None else a_win[0] // tk
        a_spec = pl.BlockSpec((tm, tk), lambda i, j, k: (i, k + off))
        if mode == "nn":
            dims = NN
            if b_dev:
                per = PACK_COLS // tn
                b_spec = pl.BlockSpec((None, tk, tn), lambda i, j, k: (j // per, k, j % per))
            else:
                b_spec = pl.BlockSpec((tk, tn), lambda i, j, k: (k, j))
        else:
            dims = NT
            if b_dev:
                per = PACK_COLS // tk
                b_spec = pl.BlockSpec((None, tn, tk), lambda i, j, k: (k // per, j, k % per))
            else:
                b_spec = pl.BlockSpec((tn, tk), lambda i, j, k: (j, k))
    if a_win is not None:
        assert a_win[0] % (tm if mode == "tn" else tk) == 0
    extra = []
    if add is not None:
        extra.append((add, pl.BlockSpec((tm, tn), lambda i, j, k: (i, j))))
    if bias is not None:
        extra.append((bias, pl.BlockSpec((1, tn), lambda i, j, k: (0, j))))
    if out_dest:
        per = PACK_COLS // tn
        o_spec = pl.BlockSpec((None, tm, tn), lambda i, j, k: (j // per, i, j % per))
        out_shape = jax.ShapeDtypeStruct((n // PACK_COLS, m, PACK_COLS), out_dtype)
    else:
        o_spec = pl.BlockSpec((tm, tn), lambda i, j, k: (i, j))
        out_shape = jax.ShapeDtypeStruct((m, n), out_dtype)
    return _mm_core(a, b, dims=dims, grid=(m // tm, n // tn, kdim // tk), a_spec=a_spec, b_spec=b_spec,
                    o_spec=o_spec, out_shape=out_shape, acc_shape=(tm, tn), name=name, extra=extra, hook=hook)


def _win(arr, col0=0, width=None, roff=0):
    return (arr, col0, arr.shape[1] if width is None else width, roff)


def _rows(fn, rows, consts, outs, accs, *, name, nrow, tr=CHUNK):
    nr, nc, no = len(rows), len(consts), len(outs)

    def body(*refs):
        i = pl.program_id(0)
        ins = [r[...] for r in refs[:nr + nc]]
        o_refs = refs[nr + nc:nr + nc + no]
        a_refs = refs[nr + nc + no:]
        res_o, res_a = fn(i, *ins)
        for r, v in zip(o_refs, res_o):
            r[...] = v.astype(r.dtype)
        if a_refs:
            @pl.when(i == 0)
            def _():
                for r in a_refs:
                    r[...] = jnp.zeros_like(r)

            for r, v in zip(a_refs, res_a):
                r[...] += v

    in_specs = []
    for (arr, col0, width, roff) in rows:
        assert col0 % width == 0 and arr.shape[0] % tr == 0
        in_specs.append(pl.BlockSpec((tr, width), lambda i, c=col0 // width, ro=roff: (jnp.maximum(i - ro, 0), c)))
    for c in consts:
        in_specs.append(pl.BlockSpec(c.shape, lambda i, nd=c.ndim: (0,) * nd))
    out_specs = [pl.BlockSpec((tr, w), lambda i: (i, 0)) for (w, _) in outs]
    out_specs += [pl.BlockSpec(s, lambda i, nd=len(s): (0,) * nd) for s in accs]
    out_shape = [jax.ShapeDtypeStruct((nrow, w), dt) for (w, dt) in outs]
    out_shape += [jax.ShapeDtypeStruct(s, F32) for s in accs]
    res = pl.pallas_call(
        body, name=name, grid=(nrow // tr,), in_specs=in_specs, out_specs=out_specs, out_shape=out_shape,
        compiler_params=_params(("arbitrary",)),
    )(*[r[0] for r in rows], *consts)
    return res[:no], res[no:]


HEADS_PER_STEP = 2


def _scan_specs(xs, cs, ws, ks, chunk_of, hpb):
    specs = []
    for (arr, width, colfn) in xs:
        specs.append(pl.BlockSpec((CHUNK, width * hpb), lambda h, n, f=colfn: (chunk_of(n), f(h * hpb) // hpb)))
    for (arr, width, colfn) in cs:
        specs.append(pl.BlockSpec((CHUNK, width), lambda h, n, f=colfn: (chunk_of(n), f(h))))
    for arr in list(ws) + list(ks):
        specs.append(pl.BlockSpec((hpb, 1, arr.shape[2]), lambda h, n: (h, 0, 0)))
    return specs


def _scan_fwd(fn, xs, cs, ws, ks, *, heads, nchunk, s_shape, out_w, name, pre=None, hook=None,
              hpb=HEADS_PER_STEP):
    nx, ncs, nw = len(xs), len(cs), len(ws)
    hook = _NO_HOOK if hook is None else hook
    hi, ho = len(hook.ins), len(hook.outs)
    hblocks = heads // hpb

    def body(*refs):
        n = pl.program_id(1)
        nin = nx + ncs + nw + len(ks)
        y_ref, sp_ref = refs[nin + hi], refs[nin + hi + 1]
        s_scr = refs[nin + hi + 2 + ho]
        _run_hook(hook, pl.program_id(0) * nchunk + n, hblocks * nchunk, refs[nin:nin + hi],
                  refs[nin + hi + 2:nin + hi + 2 + ho], refs[nin + hi + 3 + ho:])

        @pl.when(n == 0)
        def _():
            s_scr[...] = jnp.zeros_like(s_scr)

        cv = [r[...] for r in refs[nx:nx + ncs]]
        for e in range(hpb):
            state = s_scr[e]
            sp_ref[e, 0] = state
            xv = [r[:, e * w:(e + 1) * w] for r, (_, w, _) in zip(refs[:nx], xs)]
            wv = [r[e] for r in refs[nx + ncs:nx + ncs + nw]]
            kv = [r[e] for r in refs[nx + ncs + nw:nin]]
            if pre is not None:
                xv = pre(xv, cv)
            y, s_new = fn(n, xv, state, cv, wv, kv)
            y_ref[:, e * out_w:(e + 1) * out_w] = y.astype(y_ref.dtype)
            s_scr[e] = s_new

    lp = nchunk * CHUNK
    res = pl.pallas_call(
        body, name=name, grid=(hblocks, nchunk),
        in_specs=_scan_specs(xs, cs, ws, ks, lambda n: n, hpb) + [_ANY] * hi,
        out_specs=[pl.BlockSpec((CHUNK, out_w * hpb), lambda h, n: (n, h)),
                   pl.BlockSpec((hpb, 1) + s_shape, lambda h, n: (h, n, 0, 0))] + [_ANY] * ho,
        out_shape=[jax.ShapeDtypeStruct((lp, heads * out_w), BF16),
                   jax.ShapeDtypeStruct((heads, nchunk) + s_shape, F32)] + hook.outs,
        scratch_shapes=[pltpu.VMEM((hpb,) + s_shape, F32)] + hook.sems,
        compiler_params=_params(("arbitrary", "arbitrary")),
    )(*[t[0] for t in xs], *[t[0] for t in cs], *ws, *ks, *hook.ins)
    return (res[0], res[1]) if hook is _NO_HOOK else (res[0], res[1], res[2:])


def _scan_bwd(fn, xs, cs, ws, ks, dy, sprev, *, heads, nchunk, s_shape, out_w, name, pre=None, post=None,
              hook=None, hpb=HEADS_PER_STEP):
    nx, ncs, nw = len(xs), len(cs), len(ws)
    nin = nx + ncs + nw + len(ks)
    hook = _NO_HOOK if hook is None else hook
    hi, ho = len(hook.ins), len(hook.outs)
    hblocks = heads // hpb

    def body(*refs):
        step = pl.program_id(1)
        n = nchunk - 1 - step
        dy_ref, sp_ref = refs[nin], refs[nin + 1]
        o0 = nin + 2 + hi
        dx_refs = refs[o0:o0 + nx]
        dw_refs = refs[o0 + nx:o0 + nx + nw]
        ds_scr = refs[o0 + nx + nw + ho]
        _run_hook(hook, pl.program_id(0) * nchunk + step, hblocks * nchunk, refs[nin + 2:o0],
                  refs[o0 + nx + nw:o0 + nx + nw + ho], refs[o0 + nx + nw + ho + 1:])

        @pl.when(step == 0)
        def _():
            ds_scr[...] = jnp.zeros_like(ds_scr)
            for r in dw_refs:
                r[...] = jnp.zeros_like(r)

        cv = [r[...] for r in refs[nx:nx + ncs]]
        for e in range(hpb):
            xv = [r[:, e * w:(e + 1) * w] for r, (_, w, _) in zip(refs[:nx], xs)]
            wv = [r[e] for r in refs[nx + ncs:nx + ncs + nw]]
            kv = [r[e] for r in refs[nx + ncs + nw:nin]]
            if pre is not None:
                xv = pre(xv, cv)
            _, vjp = jax.vjp(lambda xs_, s_, ws_, kv=kv: fn(n, xs_, s_, cv, ws_, kv), xv, sp_ref[e, 0], wv)
            dxs, ds_prev, dws = vjp((dy_ref[:, e * out_w:(e + 1) * out_w].astype(F32), ds_scr[e]))
            if post is not None:
                dxs = post(dxs, cv)
            for r, v, (_, w, _) in zip(dx_refs, dxs, xs):
                r[:, e * w:(e + 1) * w] = v.astype(r.dtype)
            for r, v in zip(dw_refs, dws):
                r[e] += v
            ds_scr[e] = ds_prev

    lp = nchunk * CHUNK
    rev = lambda n: nchunk - 1 - n
    in_specs = _scan_specs(xs, cs, ws, ks, rev, hpb)
    in_specs.append(pl.BlockSpec((CHUNK, out_w * hpb), lambda h, n: (rev(n), h)))
    in_specs.append(pl.BlockSpec((hpb, 1) + s_shape, lambda h, n: (h, rev(n), 0, 0)))
    out_specs = [pl.BlockSpec((CHUNK, w * hpb), lambda h, n: (rev(n), h)) for (_, w, _) in xs]
    out_specs += [pl.BlockSpec((hpb, 1, w.shape[2]), lambda h, n: (h, 0, 0)) for w in ws]
    out_shape = [jax.ShapeDtypeStruct((lp, heads * w), BF16) for (_, w, _) in xs]
    out_shape += [jax.ShapeDtypeStruct(w.shape, F32) for w in ws]
    res = pl.pallas_call(
        body, name=name, grid=(hblocks, nchunk), in_specs=in_specs + [_ANY] * hi,
        out_specs=out_specs + [_ANY] * ho, out_shape=out_shape + hook.outs,
        scratch_shapes=[pltpu.VMEM((hpb,) + s_shape, F32)] + hook.sems,
        compiler_params=_params(("arbitrary", "arbitrary")),
    )(*[t[0] for t in xs], *[t[0] for t in cs], *ws, *ks, dy, sprev, *hook.ins)
    if hook is _NO_HOOK:
        return res[:nx], res[nx:]
    return res[:nx], res[nx:nx + nw], res[nx + nw:]


def _iota2(shape, dim):
    return lax.broadcasted_iota(jnp.int32, shape, dim)


def _ret_chunk(n, xs, state, cs, ws, ks):
    q, k, v, z = xs
    (w,), (lg,) = ws, ks
    lgc = lg[:, :1]
    row, col = _iota2((CHUNK, CHUNK), 0), _iota2((CHUNK, CHUNK), 1)
    diff = jnp.maximum(row - col, 0).astype(F32)
    decay = jnp.where(row >= col, jnp.exp(lg * diff), 0.0)
    scores = _bdot(q, k, 1, 1) * decay
    o_intra = _bdot(scores, v, 1, 0)
    idx = _iota2((CHUNK, 1), 0).astype(F32)
    k_w = k * jnp.exp(lgc * (CHUNK - 1.0 - idx))
    kv = _bdot(k_w, v, 0, 0)
    s_new = state * jnp.exp(lgc * float(CHUNK)) + kv
    q_w = q * jnp.exp(lgc * (idx + 1.0))
    o = o_intra + _bdot(q_w, state, 1, 0)
    return _rms(o, w) * _silu(z), s_new


def _rope(t, cos2, sin2):
    return t * cos2 + pltpu.roll(t, RET_DK // 2, 1) * sin2


def _rope_t(g, cos2, sin2):
    return g * cos2 - pltpu.roll(g, RET_DK // 2, 1) * sin2


def _ret_pre(xv, cv):
    q, k, v, z = xv
    cos2, sin2 = cv
    return [_rope(q, cos2, sin2), _rope(k, cos2, sin2) * (RET_DK ** -0.5), v, z]


def _ret_post(dxs, cv):
    dq, dk, dv, dz = dxs
    cos2, sin2 = cv
    return [_rope_t(dq, cos2, sin2), _rope_t(dk, cos2, sin2) * (RET_DK ** -0.5), dv, dz]


def _gla_chunk(n, xs, state_t, cs, ws, ks):
    q, k, v, z, pre = xs
    (w,) = ws
    q = q * (GLA_DK ** -0.5)
    rowc = _iota2((CHUNK, 1), 0)
    valid = jnp.logical_or(n > 0, rowc >= PAD)
    log_a = jnp.where(valid, _log_sigmoid(pre) / GLA_TAU, 0.0)
    row, col = _iota2((CHUNK, CHUNK), 0), _iota2((CHUNK, CHUNK), 1)
    tri = (row >= col).astype(F32)
    b = jnp.dot(tri, log_a, precision=lax.Precision.HIGHEST, preferred_element_type=F32)
    b_last = b[CHUNK - 1:CHUNK, :]
    kv_t = _bdot(v, k * jnp.exp(b_last - b), 0, 0)
    s_new = state_t * jnp.exp(b_last) + kv_t
    o_inter = _bdot(q * jnp.exp(b), state_t, 1, 1)
    outs = []
    for s in range(CHUNK // SUB):
        lo, hi = s * SUB, (s + 1) * SUB
        b_ref = jnp.zeros_like(b_last) if s == 0 else b[lo - 1:lo, :]
        q_hat = q[lo:hi] * jnp.exp(b[lo:hi] - b_ref)
        k_hat = k[:hi] * jnp.exp(b_ref - b[:hi])
        sc = _bdot(q_hat, k_hat, 1, 1)
        causal = _iota2((SUB, hi), 0) + lo >= _iota2((SUB, hi), 1)
        outs.append(_bdot(jnp.where(causal, sc, 0.0), v[:hi], 1, 0))
    o = jnp.concatenate(outs, axis=0) + o_inter
    return _rms(o, w) * _silu(z), s_new


def _s5_disc(lam_re, lam_im, log_dt, b_re, b_im, expand):
    dt = jnp.exp(log_dt)
    mag = jnp.exp(lam_re * dt)
    ab_re, ab_im = mag * jnp.cos(lam_im * dt), mag * jnp.sin(lam_im * dt)
    den = lam_re * lam_re + lam_im * lam_im
    nr, ni = ab_re - 1.0, ab_im
    f_re = (nr * lam_re + ni * lam_im) / den
    f_im = (ni * lam_re - nr * lam_im) / den
    hp = lax.Precision.HIGHEST
    f_re = jnp.dot(f_re, expand, precision=hp, preferred_element_type=F32)
    f_im = jnp.dot(f_im, expand, precision=hp, preferred_element_type=F32)
    return ab_re, ab_im, f_re * b_re - f_im * b_im, f_re * b_im + f_im * b_re


def _s5_disc_fwd(args):
    def body(*refs):
        outs = _s5_disc(*[r[...] for r in refs[:6]])
        for r, v in zip(refs[6:], outs):
            r[...] = v

    g, p = args[0].shape
    return pl.pallas_call(
        body, name="s5_disc_fwd",
        out_shape=[jax.ShapeDtypeStruct((g, p), F32)] * 2 + [jax.ShapeDtypeStruct(args[3].shape, F32)] * 2,
    )(*args)


def _s5_disc_bwd(args, cts):
    def body(*refs):
        prim = [r[...] for r in refs[:5]]
        expand = refs[5][...]
        ct = tuple(r[...] for r in refs[6:10])
        _, vjp = jax.vjp(lambda *a: _s5_disc(*a, expand), *prim)
        for r, v in zip(refs[10:], vjp(ct)):
            r[...] = v

    return pl.pallas_call(
        body, name="s5_disc_bwd", out_shape=[jax.ShapeDtypeStruct(a.shape, F32) for a in args[:5]],
    )(*args, *cts)


S5_SUBL = 8
S5_LANES = S5_N // S5_SUBL
S5_TB = 64


def _s5_scan_fwd(bu, a_re, a_im):
    lp = bu.shape[0]

    def body(bu_ref, ar_ref, ai_ref, x_ref, st):
        @pl.when(pl.program_id(0) == 0)
        def _():
            st[...] = jnp.zeros_like(st)

        ar, ai = ar_ref[...], ai_ref[...]

        def step(t, carry):
            xr, xi = carry
            nr = ar * xr - ai * xi + bu_ref[t, 0:S5_SUBL, :]
            ni = ar * xi + ai * xr + bu_ref[t, S5_SUBL:2 * S5_SUBL, :]
            x_ref[t, 0:S5_SUBL, :] = nr
            x_ref[t, S5_SUBL:2 * S5_SUBL, :] = ni
            return nr, ni

        xr, xi = lax.fori_loop(0, S5_TB, step, (st[0], st[1]))
        st[0] = xr
        st[1] = xi

    blk = pl.BlockSpec((S5_TB, 2 * S5_SUBL, S5_LANES), lambda i: (i, 0, 0))
    cst = pl.BlockSpec((S5_SUBL, S5_LANES), lambda i: (0, 0))
    return pl.pallas_call(
        body, name="s5_scan_fwd", grid=(lp // S5_TB,), in_specs=[blk, cst, cst], out_specs=blk,
        out_shape=jax.ShapeDtypeStruct(bu.shape, F32),
        scratch_shapes=[pltpu.VMEM((2, S5_SUBL, S5_LANES), F32)],
        compiler_params=_params(("arbitrary",)),
    )(bu, a_re, a_im)


def _s5_scan_bwd(gx, x, a_re, a_im):
    lp = gx.shape[0]
    nb = lp // S5_TB

    def body(gx_ref, x_ref, xp_ref, ar_ref, ai_ref, g_ref, da_ref, st):
        i = pl.program_id(0)

        @pl.when(i == 0)
        def _():
            st[...] = jnp.zeros_like(st)
            da_ref[...] = jnp.zeros_like(da_ref)

        ar, ai = ar_ref[...], ai_ref[...]
        first = (i == nb - 1).astype(F32)

        def step(s, carry):
            gr, gi, dar, dai = carry
            t = S5_TB - 1 - s
            ngr = gx_ref[t, 0:S5_SUBL, :] + ar * gr + ai * gi
            ngi = gx_ref[t, S5_SUBL:2 * S5_SUBL, :] + ar * gi - ai * gr
            g_ref[t, 0:S5_SUBL, :] = ngr
            g_ref[t, S5_SUBL:2 * S5_SUBL, :] = ngi
            tp = jnp.maximum(t - 1, 0)
            at0 = (t == 0).astype(F32)
            keep = 1.0 - at0
            pr = keep * x_ref[tp, 0:S5_SUBL, :] + at0 * (1.0 - first) * xp_ref[0, 0:S5_SUBL, :]
            pi = keep * x_ref[tp, S5_SUBL:2 * S5_SUBL, :] + at0 * (1.0 - first) * xp_ref[0, S5_SUBL:2 * S5_SUBL, :]
            return ngr, ngi, dar + ngr * pr + ngi * pi, dai + ngi * pr - ngr * pi

        zero = jnp.zeros((S5_SUBL, S5_LANES), F32)
        gr, gi, dar, dai = lax.fori_loop(0, S5_TB, step, (st[0], st[1], zero, zero))
        st[0] = gr
        st[1] = gi
        da_ref[0] += dar
        da_ref[1] += dai

    rev = lambda i: nb - 1 - i
    blk = pl.BlockSpec((S5_TB, 2 * S5_SUBL, S5_LANES), lambda i: (rev(i), 0, 0))
    prev = pl.BlockSpec((1, 2 * S5_SUBL, S5_LANES), lambda i: (jnp.maximum(rev(i) * S5_TB - 1, 0), 0, 0))
    cst = pl.BlockSpec((S5_SUBL, S5_LANES), lambda i: (0, 0))
    return pl.pallas_call(
        body, name="s5_scan_bwd", grid=(nb,), in_specs=[blk, blk, prev, cst, cst],
        out_specs=[blk, pl.BlockSpec((2, S5_SUBL, S5_LANES), lambda i: (0, 0, 0))],
        out_shape=[jax.ShapeDtypeStruct(gx.shape, F32), jax.ShapeDtypeStruct((2, S5_SUBL, S5_LANES), F32)],
        scratch_shapes=[pltpu.VMEM((2, S5_SUBL, S5_LANES), F32)],
        compiler_params=_params(("arbitrary",)),
    )(gx, x, x, a_re, a_im)


def _place():
    x, y, c = lax.axis_index("x"), lax.axis_index("y"), lax.axis_index("c")
    return x, y, c, [(1 - x, y), (x, 1 - y), (1 - x, 1 - y)]


def _gather_phases():
    def plan(x_ref, out_ref, send_sems, recv_sems, local_sem):
        x, y, c, chips = _place()
        me, sibling = (x, y, c), (x, y, 1 - c)

        def rows(px, py, pc):
            return out_ref.at[4 * px + 2 * py + pc]

        def copy(k, block, to, src=None):
            return pltpu.make_async_remote_copy(
                src_ref=rows(*block) if src is None else src, dst_ref=rows(*block),
                send_sem=send_sems.at[k], recv_sem=recv_sems.at[k], device_id=to, device_id_type=MESH)

        mine = pltpu.make_async_copy(x_ref, rows(*me), local_sem)
        first = [copy(0, me, sibling, src=x_ref)]
        first += [copy(1 + j, me, (*chip, c), src=x_ref) for j, chip in enumerate(chips)]
        passed = [copy(4 + j, (*chip, c), sibling) for j, chip in enumerate(chips)]
        return c, chips, me, sibling, copy, mine, first, passed

    def start(ins, outs, sems):
        _, _, _, _, _, mine, first, _ = plan(ins[0], outs[0], *sems)
        mine.start()
        for cp in first:
            cp.start()

    def middle(ins, outs, sems):
        c, chips, me, _, copy, _, _, passed = plan(ins[0], outs[0], *sems)
        for j, chip in enumerate(chips):
            copy(1 + j, (*chip, c), me).wait_recv()
            passed[j].start()

    def finish(ins, outs, sems):
        c, chips, me, sibling, copy, mine, first, passed = plan(ins[0], outs[0], *sems)
        copy(0, sibling, me).wait_recv()
        for j, chip in enumerate(chips):
            copy(4 + j, (*chip, 1 - c), me).wait_recv()
        for cp in first + passed:
            cp.wait_send()
        mine.wait()

    return start, middle, finish


_GATHER_SEMS = [pltpu.SemaphoreType.DMA((7,)), pltpu.SemaphoreType.DMA((7,)), pltpu.SemaphoreType.DMA]


def _all_gather(shard, name):
    phases = _gather_phases()

    def body(x_ref, out_ref, *sems):
        for phase in phases:
            phase([x_ref], [out_ref], sems)

    return pl.pallas_call(
        body, name=name, out_shape=jax.ShapeDtypeStruct((N_DEV,) + shard.shape, shard.dtype),
        in_specs=[_ANY], out_specs=_ANY, scratch_shapes=list(_GATHER_SEMS),
    )(shard)


def _gather_hook(shard):
    start, middle, finish = _gather_phases()
    return _Hook([shard], [jax.ShapeDtypeStruct((N_DEV,) + shard.shape, shard.dtype)], _GATHER_SEMS,
                 [(0.0, start), (0.85, middle), (1.0, finish)])


def _swap_with_sibling(parts, name):
    def body(p_ref, out_ref, send_sems, recv_sems):
        x, y, c, _ = _place()
        copies = [pltpu.make_async_remote_copy(
            src_ref=p_ref.at[2 * chip + (1 - c)], dst_ref=out_ref.at[chip],
            send_sem=send_sems.at[chip], recv_sem=recv_sems.at[chip],
            device_id=(x, y, 1 - c), device_id_type=MESH) for chip in range(4)]
        for cp in copies:
            cp.start()
        for cp in copies:
            cp.wait()

    return pl.pallas_call(
        body, name=name, out_shape=jax.ShapeDtypeStruct((4,) + parts.shape[1:], parts.dtype),
        in_specs=[pl.BlockSpec(memory_space=pl.ANY)], out_specs=pl.BlockSpec(memory_space=pl.ANY),
        scratch_shapes=[pltpu.SemaphoreType.DMA((4,)), pltpu.SemaphoreType.DMA((4,))],
    )(parts)


def _chips_phases():
    def copies(p_ref, out_ref, send_sems, recv_sems):
        x, y, c, chips = _place()
        return [pltpu.make_async_remote_copy(
            src_ref=p_ref.at[2 * px + py], dst_ref=out_ref.at[j],
            send_sem=send_sems.at[j], recv_sem=recv_sems.at[j],
            device_id=(px, py, c), device_id_type=MESH) for j, (px, py) in enumerate(chips)]

    def start(ins, outs, sems):
        for cp in copies(ins[0], outs[0], *sems):
            cp.start()

    def finish(ins, outs, sems):
        for cp in copies(ins[0], outs[0], *sems):
            cp.wait()

    return start, finish


def _chips_hook(parts):
    start, finish = _chips_phases()
    return _Hook([parts], [jax.ShapeDtypeStruct((3,) + parts.shape[1:], parts.dtype)],
                 [pltpu.SemaphoreType.DMA((3,)), pltpu.SemaphoreType.DMA((3,))], [(0.0, start), (1.0, finish)])


def _pack_rows(n_elem, row_mult=PACK_ROW_MULT):
    rows = -(-n_elem // PACK_COLS)
    return -(-rows // row_mult) * row_mult


def _pack(flats, dtype, row_mult=PACK_ROW_MULT):
    flat = jnp.concatenate([f.reshape(-1).astype(dtype) for f in flats])
    rows = _pack_rows(flat.shape[0], row_mult)
    return jnp.pad(flat, (0, rows * PACK_COLS - flat.shape[0])).reshape(rows, PACK_COLS)


def _unpack(buf, shapes):
    lead = buf.shape[:-2]
    flat = buf.reshape(lead + (-1,))
    outs, o = [], 0
    for s in shapes:
        n = math.prod(s)
        outs.append(flat[..., o:o + n].reshape(lead + tuple(s)))
        o += n
    return outs


BIG_LAYOUT = (("w_in_ab", D_MODEL, PACK_COLS), ("s5_w_glu", S5_W // N_DEV, PACK_COLS),
              ("w_out_ab", OUT_AB // N_DEV, 2 * PACK_COLS), ("w_in_c", D_MODEL, PACK_COLS),
              ("w_out_c", GLA_W // N_DEV, 2 * PACK_COLS))


def _to_rows(a):
    if a.shape[-1] == PACK_COLS:
        return a
    assert a.shape[-1] == 2 * PACK_COLS
    return jnp.concatenate([a[..., :PACK_COLS], a[..., PACK_COLS:]], axis=-2)


def _from_rows(p, cols):
    if cols == PACK_COLS:
        return p
    r = p.shape[-2] // 2
    return jnp.concatenate([p[..., :r, :], p[..., r:, :]], axis=-1)


FIRST_LAYOUT = BIG_LAYOUT[:1]
OTHER_LAYOUT = BIG_LAYOUT[1:3] + BIG_LAYOUT[4:]
IN_C_LAYOUT = BIG_LAYOUT[3:4]


def _pack_big(pieces, layout):
    return jnp.concatenate([_to_rows(pieces[name]) for name, _, _ in layout], axis=-2)


def _unpack_big(buf, layout):
    out, o = {}, 0
    for name, rows, cols in layout:
        r = rows * cols // PACK_COLS
        out[name] = _from_rows(buf[..., o:o + r, :], cols)
        o += r
    return out


def _rows1024(a):
    r, c = a.shape
    if c > PACK_COLS:
        a = jnp.concatenate([a[:, i * PACK_COLS:(i + 1) * PACK_COLS] for i in range(c // PACK_COLS)], axis=0)
    elif c < PACK_COLS:
        a = jnp.pad(a, ((0, 0), (0, PACK_COLS - c)))
    return jnp.pad(a, ((0, -a.shape[0] % 8), (0, 0)))


def _unrows1024(p, r, c):
    if c > PACK_COLS:
        return jnp.concatenate([p[i * r:(i + 1) * r] for i in range(c // PACK_COLS)], axis=1)
    return p[:r, :c]


def _lane_select(a, off, sign, n_out, out_dtype, exact, name):
    rows, n_in = a.shape
    tr = _tile(rows, 256, 16)

    def body(off_ref, a_ref, o_ref):
        sel = _iota2((n_in, n_out), 0) + off_ref[0] * sign == _iota2((n_in, n_out), 1)
        if exact:
            r = jnp.dot(a_ref[...], sel.astype(F32), precision=lax.Precision.HIGHEST, preferred_element_type=F32)
        else:
            r = _dg(a_ref[...], sel.astype(BF16), 1, 0)
        o_ref[...] = r.astype(out_dtype)

    return pl.pallas_call(
        body, name=name, grid=(rows // tr,),
        in_specs=[pl.BlockSpec(memory_space=pltpu.SMEM), pl.BlockSpec((tr, n_in), lambda i: (i, 0))],
        out_specs=pl.BlockSpec((tr, n_out), lambda i: (i, 0)),
        out_shape=jax.ShapeDtypeStruct((rows, n_out), out_dtype),
        compiler_params=_params(("arbitrary",)),
    )(off, a)


def _adamw(w, g, m, v, name):
    rows, cols = w.shape
    tr = _tile(rows, 256, 8) if rows % 8 == 0 else rows

    def fn(i, w_, g_, m_, v_):
        m_new = ADAM_B1 * m_ + (1.0 - ADAM_B1) * g_
        v_new = ADAM_B2 * v_ + (1.0 - ADAM_B2) * (g_ * g_)
        m_hat = m_new / (1.0 - ADAM_B1 ** ADAM_STEP)
        v_hat = v_new / (1.0 - ADAM_B2 ** ADAM_STEP)
        delta = -ADAM_LR * (m_hat / (jnp.sqrt(v_hat) + ADAM_EPS) + ADAM_WD * w_)
        return (delta, m_new, v_new), ()

    outs, _ = _rows(fn, [_win(w), _win(g), _win(m), _win(v)], [], [(cols, F32)] * 3, [], name=name,
                    nrow=rows, tr=tr)
    return outs


def _as2d(a):
    if a.ndim == 1:
        return a.reshape(1, -1)
    if a.ndim == 2:
        return a
    a = a.reshape(a.shape[1:])
    return a if a.ndim == 2 else a.reshape(a.shape[0], -1)


def kernel(x, meta, norm_ab_w, w_in_ab, ret_norm_w, s5_lam_re, s5_lam_im, s5_log_dt, s5_b_re, s5_b_im, s5_c_re, s5_c_im, s5_d, s5_w_glu, w_out_ab, norm_c_w, w_in_c, gla_w_gate, gla_b_gate, gla_norm_w, w_out_c, final_norm_w, loss_target, m_meta, m_norm_ab_w, m_w_in_ab, m_ret_norm_w, m_s5_lam_re, m_s5_lam_im, m_s5_log_dt, m_s5_b_re, m_s5_b_im, m_s5_c_re, m_s5_c_im, m_s5_d, m_s5_w_glu, m_w_out_ab, m_norm_c_w, m_w_in_c, m_gla_w_gate, m_gla_b_gate, m_gla_norm_w, m_w_out_c, m_final_norm_w, v_meta, v_norm_ab_w, v_w_in_ab, v_ret_norm_w, v_s5_lam_re, v_s5_lam_im, v_s5_log_dt, v_s5_b_re, v_s5_b_im, v_s5_c_re, v_s5_c_im, v_s5_d, v_s5_w_glu, v_w_out_ab, v_norm_c_w, v_w_in_c, v_gla_w_gate, v_gla_b_gate, v_gla_norm_w, v_w_out_c, v_final_norm_w):
    weights = dict(meta=meta, norm_ab_w=norm_ab_w, w_in_ab=w_in_ab, ret_norm_w=ret_norm_w, s5_lam_re=s5_lam_re,
                   s5_lam_im=s5_lam_im, s5_log_dt=s5_log_dt, s5_b_re=s5_b_re, s5_b_im=s5_b_im, s5_c_re=s5_c_re,
                   s5_c_im=s5_c_im, s5_d=s5_d, s5_w_glu=s5_w_glu, w_out_ab=w_out_ab, norm_c_w=norm_c_w,
                   w_in_c=w_in_c, gla_w_gate=gla_w_gate, gla_b_gate=gla_b_gate, gla_norm_w=gla_norm_w,
                   w_out_c=w_out_c, final_norm_w=final_norm_w)
    mom_m = dict(meta=m_meta, norm_ab_w=m_norm_ab_w, w_in_ab=m_w_in_ab, ret_norm_w=m_ret_norm_w,
                 s5_lam_re=m_s5_lam_re, s5_lam_im=m_s5_lam_im, s5_log_dt=m_s5_log_dt, s5_b_re=m_s5_b_re,
                 s5_b_im=m_s5_b_im, s5_c_re=m_s5_c_re, s5_c_im=m_s5_c_im, s5_d=m_s5_d, s5_w_glu=m_s5_w_glu,
                 w_out_ab=m_w_out_ab, norm_c_w=m_norm_c_w, w_in_c=m_w_in_c, gla_w_gate=m_gla_w_gate,
                 gla_b_gate=m_gla_b_gate, gla_norm_w=m_gla_norm_w, w_out_c=m_w_out_c, final_norm_w=m_final_norm_w)
    mom_v = dict(meta=v_meta, norm_ab_w=v_norm_ab_w, w_in_ab=v_w_in_ab, ret_norm_w=v_ret_norm_w,
                 s5_lam_re=v_s5_lam_re, s5_lam_im=v_s5_lam_im, s5_log_dt=v_s5_log_dt, s5_b_re=v_s5_b_re,
                 s5_b_im=v_s5_b_im, s5_c_re=v_s5_c_re, s5_c_im=v_s5_c_im, s5_d=v_s5_d, s5_w_glu=v_s5_w_glu,
                 w_out_ab=v_w_out_ab, norm_c_w=v_norm_c_w, w_in_c=v_w_in_c, gla_w_gate=v_gla_w_gate,
                 gla_b_gate=v_gla_b_gate, gla_norm_w=v_gla_norm_w, w_out_c=v_w_out_c, final_norm_w=v_final_norm_w)
    order = list(weights)

    seq = x.shape[1]
    lp = CHUNK + seq
    nchunk = lp // CHUNK
    dev = 4 * lax.axis_index("x") + 2 * lax.axis_index("y") + lax.axis_index("c")
    core = lax.axis_index("c")
    chip = 2 * lax.axis_index("x") + lax.axis_index("y")

    win_off = jnp.reshape(2 * dev, (1,)).astype(jnp.int32)
    shard_c = jnp.pad(w_in_c[0].astype(BF16), ((0, 0), (0, 896 - SHARD_C)))
    big_shards = dict(w_in_ab=w_in_ab[0].astype(BF16), s5_w_glu=s5_w_glu[0].astype(BF16),
                      w_out_ab=w_out_ab[0].astype(BF16), w_out_c=w_out_c[0].astype(BF16),
                      w_in_c=_lane_select(shard_c, win_off, 1, WIN_COLS, BF16, False, "w_in_c_to_window"))
    def pad_to(a, rows, cols):
        return jnp.pad(a, ((0, rows - a.shape[0]), (0, cols - a.shape[1])))

    shard_w = D_MODEL // N_DEV
    small_pack = jnp.concatenate([meta, pad_to(norm_c_w, 8, shard_w), pad_to(gla_w_gate[0], GLA_RANK, shard_w),
                                  pad_to(gla_b_gate, 8, shard_w), pad_to(gla_norm_w, 8, shard_w)], axis=0)
    w_in_ab_g = _all_gather(big_shards["w_in_ab"], "gather_first")
    half = D_MODEL // 2
    in_c_hook_a = _gather_hook(big_shards["w_in_c"][:half])
    in_c_hook_b = _gather_hook(big_shards["w_in_c"][half:])
    other_hook = _gather_hook(_pack_big(big_shards, OTHER_LAYOUT))
    gs = _all_gather(small_pack, "gather_small")
    gate_w = GLA_QK // N_DEV
    s_meta, s_norm_c = gs[:, :N_META], gs[:, N_META]
    s_wgate, s_bgate, s_gnorm = gs[:, 24:24 + GLA_RANK, :gate_w], gs[:, 40, :gate_w], gs[:, 48]
    meta_f = s_meta.transpose(1, 0, 2).reshape(N_META, D_MODEL)
    norm_c_f = s_norm_c.reshape(1, D_MODEL)
    w_gate_f = jnp.pad(s_wgate.transpose(1, 0, 2).reshape(GLA_RANK, GLA_QK), ((0, GATE_PAD - GLA_RANK), (0, 0)))
    b_gate_f = s_bgate.reshape(1, GLA_QK)
    gla_norm_f = s_gnorm.reshape(GLA_H, 1, GLA_DV)

    pos = jnp.maximum(jnp.arange(lp, dtype=F32) - float(PAD), 0.0)
    inv_freq = jnp.power(ROPE_BASE, -jnp.arange(0, RET_DK, 2, dtype=F32) / RET_DK)
    ang = pos[:, None] * inv_freq[None, :]
    cos2 = jnp.concatenate([jnp.cos(ang), jnp.cos(ang)], axis=1)
    sin2 = jnp.concatenate([-jnp.sin(ang), jnp.sin(ang)], axis=1)
    log_g = jnp.log1p(-jnp.exp2(-5.0 - jnp.arange(RET_H, dtype=F32)))
    lg = jnp.broadcast_to(log_g[:, None, None], (RET_H, 1, 128))
    ret_norm_h = ret_norm_w.reshape(RET_H, 1, RET_DV)

    h0 = jnp.concatenate([jnp.zeros((PAD, D_MODEL), F32), meta_f, x[0]], axis=0)

    def rowmask(i):
        return (_iota2((CHUNK, 1), 0) + i * CHUNK) >= PAD

    (hn0,), _ = _rows(lambda i, h, w: ((_rms(h, w),), ()), [_win(h0)], [norm_ab_w], [(D_MODEL, BF16)], [],
                      name="norm_ab_fwd", nrow=lp)
    proj_ab, (w_in_c_ga,) = _mm(hn0, w_in_ab_g, "nn", name="in_ab_fwd", hook=in_c_hook_a, b_dev=True)

    q_off, k_off, v_off, za_off = 0, RET_QK, 2 * RET_QK, 2 * RET_QK + RET_W
    u_off, zb_off = 2 * RET_QK + 2 * RET_W, 2 * RET_QK + 2 * RET_W + S5_W
    ret_xs = [(proj_ab, RET_DK, lambda h: q_off // RET_DK + h), (proj_ab, RET_DK, lambda h: k_off // RET_DK + h),
              (proj_ab, RET_DV, lambda h: v_off // RET_DV + h), (proj_ab, RET_DV, lambda h: za_off // RET_DV + h)]
    ret_cs = [(cos2, RET_DK, lambda h: 0), (sin2, RET_DK, lambda h: 0)]
    ret_kw = dict(heads=RET_H, nchunk=nchunk, s_shape=(RET_DK, RET_DV), out_w=RET_DV, pre=_ret_pre)
    o_a, ret_sprev, (gathered_other,) = _scan_fwd(_ret_chunk, ret_xs, ret_cs, [ret_norm_h], [lg], name="ret_fwd",
                                                  hook=other_hook, **ret_kw)
    gb = _unpack_big(gathered_other, OTHER_LAYOUT)
    w_glu_f = gb["s5_w_glu"].reshape(S5_W, S5_W)
    w_out_ab_f = gb["w_out_ab"].reshape(OUT_AB, D_MODEL)
    w_out_c_f = gb["w_out_c"].reshape(GLA_W, D_MODEL)

    expand = jnp.repeat(jnp.eye(S5_P, dtype=F32), S5_GH, axis=1)
    disc_args = (s5_lam_re[0], s5_lam_im[0], s5_log_dt[0].reshape(S5_G, 1),
                 s5_b_re[0].reshape(S5_G, S5_P * S5_GH), s5_b_im[0].reshape(S5_G, S5_P * S5_GH), expand)
    ab_re, ab_im, bb_re, bb_im = _s5_disc_fwd(disc_args)
    gt = S5_SUBL
    eye_t = jnp.eye(gt, dtype=F32)

    def tiles_in(bb):
        return jnp.einsum("sgph,gk->sghkp", bb.reshape(gt, gt, S5_P, S5_GH), eye_t).reshape(gt, 128, S5_LANES)

    def tiles_out(cc):
        return jnp.einsum("sghp,gk->sgpkh", cc.reshape(gt, gt, S5_GH, S5_P), eye_t).reshape(gt, S5_LANES, 128)

    wb_t = jnp.concatenate([tiles_in(bb_re), tiles_in(bb_im)], axis=0).astype(BF16)
    wc_t = jnp.concatenate([tiles_out(s5_c_re[0]), -tiles_out(s5_c_im[0])], axis=0).astype(BF16)
    a_re, a_im = ab_re.reshape(S5_SUBL, S5_LANES), ab_im.reshape(S5_SUBL, S5_LANES)
    tm5, tk5, nt5 = _tile(lp, 1408, 8), _tile(lp, 1408, 8), 2 * gt
    u_blk = u_off // 128
    wide = pl.BlockSpec((tm5, S5_LANES), lambda i, j, k: (i, j))
    wide_k = pl.BlockSpec((tm5, S5_LANES), lambda i, j, k: (i, k * gt + j))
    narrow = pl.BlockSpec((tm5, 128), lambda i, j, k: (i, j))
    wb_j = pl.BlockSpec((None, 128, S5_LANES), lambda i, j, k: (j, 0, 0))
    wc_j = pl.BlockSpec((None, S5_LANES, 128), lambda i, j, k: (j, 0, 0))
    wb_k = pl.BlockSpec((None, 128, S5_LANES), lambda i, j, k: (k * gt + j, 0, 0))
    wc_k = pl.BlockSpec((None, S5_LANES, 128), lambda i, j, k: (k * gt + j, 0, 0))
    wide_shape = jax.ShapeDtypeStruct((lp, 2 * S5_N), F32)
    bu = _mm_core(proj_ab, wb_t, dims=NN, grid=(lp // tm5, nt5, 1), name="s5_bu",
                  a_spec=pl.BlockSpec((tm5, 128), lambda i, j, k: (i, u_blk + j % gt)), b_spec=wb_j,
                  o_spec=wide, out_shape=wide_shape, acc_shape=(tm5, S5_LANES))
    xs5 = _s5_scan_fwd(bu.reshape(lp, 2 * S5_SUBL, S5_LANES), a_re, a_im)
    xs5_2d = xs5.reshape(lp, 2 * S5_N)
    y_pre = _mm_core(xs5_2d, wc_t, dims=NN, grid=(lp // tm5, gt, 2), name="s5_cx", a_spec=wide_k, b_spec=wc_k,
                     o_spec=narrow, out_shape=jax.ShapeDtypeStruct((lp, S5_W), F32), acc_shape=(tm5, 128))
    (y_s5, yg_bf), _ = _rows(
        lambda i, yp, u, d: ((yp + d * u, _gelu(yp + d * u)), ()),
        [_win(y_pre), _win(proj_ab, u_off, S5_W)], [s5_d], [(S5_W, F32), (S5_W, BF16)], [], name="s5_gelu_fwd", nrow=lp)
    t_glu = _mm(yg_bf, w_glu_f, "nn", name="s5_glu_fwd")

    def s5_gate(y, t, zb):
        return _gelu(y) * _sigmoid(t) * _silu(zb)

    (o_b,), _ = _rows(lambda i, y, t, zb: ((s5_gate(y, t, zb),), ()),
                      [_win(y_s5), _win(t_glu), _win(proj_ab, zb_off, S5_W)], [], [(S5_W, BF16)], [],
                      name="s5_gate_fwd", nrow=lp)
    o_ab = jnp.concatenate([o_a, o_b], axis=1)
    h1, (w_in_c_gb,) = _mm(o_ab, w_out_ab_f, "nn", name="out_ab_fwd", add=h0, hook=in_c_hook_b)
    w_in_c_g = jnp.concatenate([w_in_c_ga, w_in_c_gb], axis=1)
    w_in_c_f = sum(jnp.pad(w_in_c_g[d], ((0, 0), (WIN_STEP * d, IN_C_PAD - WIN_STEP * d - WIN_COLS)))
                   for d in range(N_DEV))

    (hn1,), _ = _rows(lambda i, h, w: ((_rms(h, w),), ()), [_win(h1)], [norm_c_f], [(D_MODEL, BF16)], [],
                      name="norm_c_fwd", nrow=lp)
    proj_c = _mm(hn1, w_in_c_f, "nn", name="in_c_fwd")
    gl_off = 2 * GLA_QK + 2 * GLA_W
    pre_gate = _mm(proj_c, w_gate_f, "nn", name="gate_fwd", a_win=(gl_off, GATE_PAD), bias=b_gate_f)
    gla_xs = [(proj_c, GLA_DK, lambda h: h), (proj_c, GLA_DK, lambda h: GLA_QK // GLA_DK + h),
              (proj_c, GLA_DV, lambda h: 2 * GLA_QK // GLA_DV + h),
              (proj_c, GLA_DV, lambda h: (2 * GLA_QK + GLA_W) // GLA_DV + h),
              (pre_gate, GLA_DK, lambda h: h)]
    gla_kw = dict(heads=GLA_H, nchunk=nchunk, s_shape=(GLA_DV, GLA_DK), out_w=GLA_DV, hpb=GLA_H)
    o_c, gla_sprev = _scan_fwd(_gla_chunk, gla_xs, [], [gla_norm_f], [], name="gla_fwd", **gla_kw)
    h2 = _mm(o_c, w_out_c_f, "nn", name="out_c_fwd", add=h1)

    fnw = final_norm_w.reshape(1, D_MODEL)

    def final_fn(i, h, tgt, w):
        def loss_of(h_, w_):
            err = _rms(h_, w_) - tgt
            return 0.5 * jnp.sum(jnp.mean(err * err, axis=-1))

        real = (i > 0).astype(F32)
        loss_i, (dh, dw) = jax.value_and_grad(loss_of, argnums=(0, 1))(h, w)
        return (dh * real,), (jnp.full((1, 128), loss_i * real, F32), dw * real)

    (dh2,), (loss_acc, g_final) = _rows(final_fn, [_win(h2), _win(loss_target[0], roff=1)], [fnw],
                                        [(D_MODEL, F32)], [(1, 128), (1, D_MODEL)], name="final_loss", nrow=lp)

    def rs_front(pieces, layout, tag):
        g_full = _pack_big(pieces, layout)
        prow = g_full.shape[1]
        from_sibling = _swap_with_sibling(g_full, "rs_sibling_" + tag)
        mine_by_chip = lax.dynamic_index_in_dim(g_full.reshape(4, 2, prow, PACK_COLS), core, axis=1, keepdims=False)
        (p1, p1_bf), _ = _rows(
            lambda i, a, b: ((a.astype(F32) + b.astype(F32), a.astype(F32) + b.astype(F32)), ()),
            [_win(mine_by_chip.reshape(4 * prow, PACK_COLS)), _win(from_sibling.reshape(4 * prow, PACK_COLS))], [],
            [(PACK_COLS, F32), (PACK_COLS, BF16)], [], name="rs_sum_sibling_" + tag, nrow=4 * prow,
            tr=_tile(prow, 512, 16))
        return p1.reshape(4, prow, PACK_COLS), p1_bf.reshape(4, prow, PACK_COLS)

    def rs_back(p1, from_chips, layout, tag):
        prow = p1.shape[1]
        tr = _tile(prow, 512, 16)
        own = lax.dynamic_index_in_dim(p1, chip, axis=0, keepdims=False)
        fc2 = from_chips.reshape(3 * prow, PACK_COLS)
        nblk = prow // tr
        (g_shard,), _ = _rows(
            lambda i, a, b0, b1, b2: ((((a + b0.astype(F32)) + b1.astype(F32)) + b2.astype(F32),), ()),
            [_win(own), _win(fc2), _win(fc2, roff=-nblk), _win(fc2, roff=-2 * nblk)], [], [(PACK_COLS, F32)], [],
            name="rs_sum_chips_" + tag, nrow=prow, tr=tr)
        return _unpack_big(g_shard, layout)

    dh2_bf = dh2.astype(BF16)
    do_c = _mm(dh2_bf, w_out_c_f, "nt", name="out_c_dx", out_dtype=BF16)
    gw_out_c = _mm(o_c, dh2_bf, "tn", name="out_c_dw", out_dtype=BF16)
    (dq_c, dk_c, dv_c, dz_c, dpre), (g_gla_norm,) = _scan_bwd(
        _gla_chunk, gla_xs, [], [gla_norm_f], [], do_c, gla_sprev, name="gla_bwd", **gla_kw)
    dglow = _mm(dpre, w_gate_f, "nt", name="gate_dx", out_dtype=BF16)
    g_wgate = _mm(proj_c, dpre, "tn", name="gate_dw", a_win=(gl_off, GATE_PAD))[:GLA_RANK]
    (), (g_bgate,) = _rows(lambda i, d: ((), (jnp.sum(d.astype(F32), axis=0, keepdims=True),)), [_win(dpre)], [], [],
                           [(1, GLA_QK)], name="gate_db", nrow=lp)
    dproj_c = jnp.concatenate([dq_c, dk_c, dv_c, dz_c, dglow], axis=1)
    dhn1 = _mm(dproj_c, w_in_c_f, "nt", name="in_c_dx")
    gw_in_c = _mm(hn1, dproj_c, "tn", name="in_c_dw", out_dtype=BF16)
    p1_c, p1_c_bf = rs_front(dict(
        w_in_c=jnp.stack([gw_in_c[:, WIN_STEP * d:WIN_STEP * d + WIN_COLS] for d in range(N_DEV)])),
        IN_C_LAYOUT, "in_c")

    def norm_bwd(i, h, dhn, dres, w):
        _, vjp = jax.vjp(_rms, h, w)
        dh, dw = vjp(dhn)
        return (jnp.where(rowmask(i), dh + dres, 0.0),), (dw,)

    (dh1,), (g_norm_c,) = _rows(norm_bwd, [_win(h1), _win(dhn1), _win(dh2)], [norm_c_f], [(D_MODEL, F32)],
                                [(1, D_MODEL)], name="norm_c_bwd", nrow=lp)

    dh1_bf = dh1.astype(BF16)
    do_ab = _mm(dh1_bf, w_out_ab_f, "nt", name="out_ab_dx", out_dtype=BF16)
    gw_out_ab = _mm(o_ab, dh1_bf, "tn", name="out_ab_dw", out_dtype=BF16)

    def s5_gate_bwd(i, dob, y, t, zb):
        _, vjp = jax.vjp(s5_gate, y, t, zb)
        dy, dt, dzb = vjp(dob.astype(F32))
        return (dy, dt, dzb), ()

    (dy_a, dt_glu, dzb), _ = _rows(
        s5_gate_bwd, [_win(do_ab, RET_W, S5_W), _win(y_s5), _win(t_glu), _win(proj_ab, zb_off, S5_W)], [],
        [(S5_W, F32), (S5_W, BF16), (S5_W, BF16)], [], name="s5_gate_bwd", nrow=lp)
    dyg2 = _mm(dt_glu, w_glu_f, "nt", name="s5_glu_dx")
    gw_glu = _mm(yg_bf, dt_glu, "tn", name="s5_glu_dw", out_dtype=BF16)

    def s5_y_bwd(i, dya, dyg, y, u, d):
        _, vjp = jax.vjp(_gelu, y)
        (dy_g,) = vjp(dyg)
        dy = dya + dy_g
        return (dy, d * dy), (jnp.sum(dy * u, axis=0, keepdims=True),)

    (dy_s5, du1), (g_d,) = _rows(
        s5_y_bwd, [_win(dy_a), _win(dyg2), _win(y_s5), _win(proj_ab, u_off, S5_W)], [s5_d],
        [(S5_W, BF16), (S5_W, F32)], [(1, S5_W)], name="s5_y_bwd", nrow=lp)
    gx = _mm_core(dy_s5, wc_t, dims=NT, grid=(lp // tm5, nt5, 1), name="s5_cx_dx",
                  a_spec=pl.BlockSpec((tm5, 128), lambda i, j, k: (i, j % gt)), b_spec=wc_j,
                  o_spec=wide, out_shape=wide_shape, acc_shape=(tm5, S5_LANES))
    rows_k = lambda col: pl.BlockSpec((tk5, col), lambda i, j, k: (k, i))
    gwc = _mm_core(xs5_2d, dy_s5, dims=TN, grid=(nt5, 1, lp // tk5), name="s5_cx_dw", a_spec=rows_k(S5_LANES),
                   b_spec=pl.BlockSpec((tk5, 128), lambda i, j, k: (k, i % gt)),
                   o_spec=pl.BlockSpec((None, S5_LANES, 128), lambda i, j, k: (i, 0, 0)),
                   out_shape=jax.ShapeDtypeStruct((nt5, S5_LANES, 128), F32), acc_shape=(S5_LANES, 128))
    g_s5, da = _s5_scan_bwd(gx.reshape(lp, 2 * S5_SUBL, S5_LANES), xs5, a_re, a_im)
    g_s5_2d = g_s5.reshape(lp, 2 * S5_N)
    du = _mm_core(g_s5_2d, wb_t, dims=NT, grid=(lp // tm5, gt, 2), name="s5_bu_dx", a_spec=wide_k, b_spec=wb_k,
                  o_spec=narrow, out_shape=jax.ShapeDtypeStruct((lp, S5_W), BF16), acc_shape=(tm5, 128),
                  extra=[(du1, narrow)])
    gwb = _mm_core(proj_ab, g_s5_2d, dims=TN, grid=(nt5, 1, lp // tk5), name="s5_bu_dw",
                   a_spec=pl.BlockSpec((tk5, 128), lambda i, j, k: (k, u_blk + i % gt)), b_spec=rows_k(S5_LANES),
                   o_spec=pl.BlockSpec((None, 128, S5_LANES), lambda i, j, k: (i, 0, 0)),
                   out_shape=jax.ShapeDtypeStruct((nt5, 128, S5_LANES), F32), acc_shape=(128, S5_LANES))
    gwc6 = gwc.reshape(2, gt, gt, S5_P, gt, S5_GH)
    g_c = jnp.einsum("rsgpgh->rsghp", gwc6).reshape(2, S5_G, S5_GH, S5_P)
    g_c_re, g_c_im = g_c[0], -g_c[1]
    gwb6 = gwb.reshape(2, gt, gt, S5_GH, gt, S5_P)
    d_bb = jnp.einsum("rsghgp->rsgph", gwb6).reshape(2, S5_G, S5_P * S5_GH)
    d_bb_re, d_bb_im = d_bb[0], d_bb[1]
    g_lam_re, g_lam_im, g_log_dt, g_b_re, g_b_im = _s5_disc_bwd(
        disc_args, (da[0].reshape(S5_G, S5_P), da[1].reshape(S5_G, S5_P), d_bb_re, d_bb_im))

    (dq_a, dk_a, dv_a, dz_a), (g_ret_norm,), (from_chips_c,) = _scan_bwd(
        _ret_chunk, ret_xs, ret_cs, [ret_norm_h], [lg], do_ab, ret_sprev, name="ret_bwd", post=_ret_post,
        hook=_chips_hook(p1_c_bf), **ret_kw)
    dproj_ab = jnp.concatenate([dq_a, dk_a, dv_a, dz_a, du, dzb], axis=1)
    p1_o, p1_o_bf = rs_front(dict(s5_w_glu=gw_glu.reshape(N_DEV, S5_W // N_DEV, S5_W),
                                  w_out_ab=gw_out_ab.reshape(N_DEV, OUT_AB // N_DEV, D_MODEL),
                                  w_out_c=gw_out_c.reshape(N_DEV, GLA_W // N_DEV, D_MODEL)), OTHER_LAYOUT, "other")

    lane = lambda a_: pad_to(a_, a_.shape[0], 128)

    def sum8(i, *blocks):
        acc = blocks[0]
        for b in blocks[1:]:
            acc = acc + b
        return (acc,), ()

    def pack_small(pieces):
        return jnp.concatenate([_rows1024(p) for _, p in pieces], axis=0)

    def sum_small(gathered, pieces, tag):
        srow = gathered.shape[1]
        tr = _tile(srow, 128, 8)
        flat = gathered.reshape(N_DEV * srow, PACK_COLS)
        (total,), _ = _rows(sum8, [_win(flat, roff=-d * (srow // tr)) for d in range(N_DEV)], [], [(PACK_COLS, F32)],
                            [], name="sum_small_" + tag, nrow=srow, tr=tr)
        out, o = {}, 0
        for name_, p in pieces:
            r8 = _rows1024(p).shape[0]
            out[name_] = _unrows1024(total[o:o + r8], *p.shape)
            o += r8
        return out

    early_pieces = [
        ("vec2048", jnp.concatenate([g_final, g_norm_c], axis=0)),
        ("vec1024", jnp.concatenate([g_d, g_bgate, pad_to(loss_acc[:, :1], 1, PACK_COLS)], axis=0)),
        ("lam3", jnp.concatenate([lane(g_lam_re), lane(g_lam_im), lane(g_log_dt)], axis=1)),
        ("s5_b_re", g_b_re), ("s5_b_im", g_b_im),
        ("s5_c_re", g_c_re.reshape(S5_G, S5_GH * S5_P)), ("s5_c_im", g_c_im.reshape(S5_G, S5_GH * S5_P)),
        ("ret_norm_w", g_ret_norm.reshape(RET_H, RET_DV)), ("gla_norm_w", g_gla_norm.reshape(GLA_H, GLA_DV)),
        ("gla_w_gate", g_wgate)]
    gw_in_ab, (from_chips_o, early_all) = _mm(
        hn0, dproj_ab, "tn", name="in_ab_dw", out_dest=True, out_dtype=BF16,
        hook=_merge_hooks(_chips_hook(p1_o_bf), _gather_hook(pack_small(early_pieces))))
    p1_first, p1_first_bf = rs_front(dict(w_in_ab=gw_in_ab), FIRST_LAYOUT, "first")
    dhn0, (from_chips_first,) = _mm(dproj_ab, w_in_ab_g, "nt", name="in_ab_dx", b_dev=True,
                                    hook=_chips_hook(p1_first_bf))
    (dh0,), (g_norm_ab,) = _rows(norm_bwd, [_win(h0), _win(dhn0), _win(dh1)], [norm_ab_w], [(D_MODEL, F32)],
                                 [(1, D_MODEL)], name="norm_ab_bwd", nrow=lp)
    grad_x = dh0[CHUNK:][None]
    late_pieces = [("norm_ab_w", g_norm_ab), ("meta", dh0[PAD:CHUNK])]
    small = sum_small(early_all, early_pieces, "early")
    small.update(sum_small(_all_gather(pack_small(late_pieces), "gather_grads"), late_pieces, "late"))

    big_grads = {**rs_back(p1_c, from_chips_c, IN_C_LAYOUT, "in_c"), **rs_back(p1_o, from_chips_o, OTHER_LAYOUT, "other"),
                 **rs_back(p1_first, from_chips_first, FIRST_LAYOUT, "first")}
    big_grads["w_in_c"] = _lane_select(big_grads["w_in_c"], win_off, -1, 896, F32, True,
                                       "w_in_c_from_window")[:, :SHARD_C]
    small["final_norm_w"], small["norm_c_w"] = small["vec2048"][0:1], small["vec2048"][1:2]
    small["s5_d"], small["gla_b_gate"] = small["vec1024"][0:1], small["vec1024"][1:2]
    loss = small["vec1024"][2, 0]
    small["s5_lam_re"], small["s5_lam_im"] = small["lam3"][:, :S5_P], small["lam3"][:, 128:128 + S5_P]
    small["s5_log_dt"] = small["lam3"][:, 256:257]

    def my_cols(g, n):
        return lax.dynamic_slice_in_dim(g, dev * n, n, axis=g.ndim - 1)

    grads = dict(
        meta=my_cols(small["meta"], D_MODEL // N_DEV),
        norm_ab_w=small["norm_ab_w"], w_in_ab=big_grads["w_in_ab"][None], ret_norm_w=small["ret_norm_w"].reshape(1, RET_W),
        s5_lam_re=small["s5_lam_re"][None], s5_lam_im=small["s5_lam_im"][None],
        s5_log_dt=small["s5_log_dt"].reshape(1, S5_G),
        s5_b_re=small["s5_b_re"].reshape(1, S5_G, S5_P, S5_GH), s5_b_im=small["s5_b_im"].reshape(1, S5_G, S5_P, S5_GH),
        s5_c_re=small["s5_c_re"][None], s5_c_im=small["s5_c_im"][None], s5_d=small["s5_d"],
        s5_w_glu=big_grads["s5_w_glu"][None], w_out_ab=big_grads["w_out_ab"][None],
        norm_c_w=my_cols(small["norm_c_w"], D_MODEL // N_DEV), w_in_c=big_grads["w_in_c"][None],
        gla_w_gate=my_cols(small["gla_w_gate"], GLA_QK // N_DEV)[None],
        gla_b_gate=my_cols(small["gla_b_gate"], GLA_QK // N_DEV),
        gla_norm_w=my_cols(small["gla_norm_w"].reshape(1, GLA_W), GLA_W // N_DEV),
        w_out_c=big_grads["w_out_c"][None], final_norm_w=small["final_norm_w"].reshape(D_MODEL))

    deltas, new_m, new_v = {}, {}, {}
    for k in order:
        w = weights[k]
        d2, m2, v2 = _adamw(_as2d(w), _as2d(grads[k].reshape(w.shape)), _as2d(mom_m[k]), _as2d(mom_v[k]), "adamw_" + k)
        deltas[k], new_m[k], new_v[k] = d2.reshape(w.shape), m2.reshape(w.shape), v2.reshape(w.shape)
        grads[k] = grads[k].reshape(w.shape)

    return (loss, grad_x, *[grads[k] for k in order], *[deltas[k] for k in order],
            *[new_m[k] for k in order], *[new_v[k] for k in order])
```

```python
import functools
import math

import jax
import jax.numpy as jnp
from jax import lax
from jax.experimental import pallas as pl
from jax.experimental.pallas import tpu as pltpu

F32, BF16 = jnp.float32, jnp.bfloat16
MESH = pl.DeviceIdType.MESH
N_DEV = 8

D_MODEL = 2048
CHUNK = 128
N_META = 16
PAD = CHUNK - N_META
SUB = 16
EPS = 1e-6
RET_H, RET_DK, RET_DV = 8, 128, 256
RET_QK, RET_W = RET_H * RET_DK, RET_H * RET_DV
ROPE_BASE = 10000.0
S5_W, S5_G, S5_P, S5_GH = 1024, 64, 64, 16
S5_N = S5_G * S5_P
GLA_H, GLA_DK, GLA_DV, GLA_RANK, GLA_TAU = 4, 256, 512, 16, 16.0
GLA_QK, GLA_W = GLA_H * GLA_DK, GLA_H * GLA_DV
IN_AB = 2 * RET_QK + 2 * RET_W + 2 * S5_W
OUT_AB = RET_W + S5_W
IN_C = 2 * GLA_QK + 2 * GLA_W + GLA_RANK
GATE_PAD = 256
IN_C_PAD = 2 * GLA_QK + 2 * GLA_W + GATE_PAD
ADAM_LR, ADAM_B1, ADAM_B2, ADAM_EPS, ADAM_WD, ADAM_STEP = 0.001, 0.9, 0.999, 1e-08, 0.01, 10

VMEM_LIMIT_BYTES = 48 * 2 ** 20
PACK_COLS = 1024
PACK_ROW_MULT = 8
SHARD_C = IN_C // N_DEV
WIN_STEP = 768
WIN_COLS = 1024


def _params(sem):
    return pltpu.CompilerParams(dimension_semantics=sem, vmem_limit_bytes=VMEM_LIMIT_BYTES)


def _tile(n, cap, mult):
    best = None
    for t in range(mult, min(n, cap) + 1, mult):
        if n % t == 0:
            best = t
    assert best is not None, (n, cap, mult)
    return best


def _dg(a, b, ca, cb):
    return lax.dot_general(a.astype(BF16), b.astype(BF16), (((ca,), (cb,)), ((), ())),
                           preferred_element_type=F32)


@functools.partial(jax.custom_vjp, nondiff_argnums=(2, 3))
def _bdot(a, b, ca, cb):
    return _dg(a, b, ca, cb)


def _bdot_fwd(a, b, ca, cb):
    return _dg(a, b, ca, cb), (a, b)


def _bdot_bwd(ca, cb, res, g):
    a, b = res
    da = _dg(g, b, 1, 1 - cb) if ca == 1 else _dg(b, g, 1 - cb, 1)
    db = _dg(a, g, 1 - ca, 0) if cb == 0 else _dg(g, a, 0, 1 - ca)
    return da.astype(a.dtype), db.astype(b.dtype)


_bdot.defvjp(_bdot_fwd, _bdot_bwd)


def _sigmoid(x):
    return 1.0 / (1.0 + jnp.exp(-x))


def _silu(x):
    return x * _sigmoid(x)


def _log_sigmoid(x):
    return jnp.minimum(x, 0.0) - jnp.log(1.0 + jnp.exp(-jnp.abs(x)))


def _gelu(x):
    return 0.5 * x * (1.0 + jnp.tanh(math.sqrt(2.0 / math.pi) * (x + 0.044715 * (x * x * x))))


def _rms(x, w):
    return x * lax.rsqrt(jnp.mean(x * x, axis=-1, keepdims=True) + EPS) * w


class _Hook:
    def __init__(self, ins, outs, sems, phases):
        self.ins, self.outs, self.sems, self.phases = list(ins), list(outs), list(sems), list(phases)


def _merge_hooks(first, second):
    ni, no, ns = len(first.ins), len(first.outs), len(first.sems)
    phases = [(f, lambda i, o, s, fn=fn: fn(i[:ni], o[:no], s[:ns])) for f, fn in first.phases]
    phases += [(f, lambda i, o, s, fn=fn: fn(i[ni:], o[no:], s[ns:])) for f, fn in second.phases]
    return _Hook(first.ins + second.ins, first.outs + second.outs, first.sems + second.sems,
                 sorted(phases, key=lambda p: p[0]))


_NO_HOOK = _Hook([], [], [], [])
_ANY = pl.BlockSpec(memory_space=pl.ANY)


def _run_hook(hook, lin, total, in_refs, out_refs, sem_refs):
    for frac, fn in hook.phases:
        at = min(int(frac * total), total - 1)

        @pl.when(lin == at)
        def _(fn=fn):
            fn(in_refs, out_refs, sem_refs)


def _mm_core(a, b, *, dims, grid, a_spec, b_spec, o_spec, out_shape, acc_shape, name, extra=(), hook=None,
             b_parts=0):
    nk = grid[2]
    n_extra = len(extra)
    hook = _NO_HOOK if hook is None else hook
    hi, ho = len(hook.ins), len(hook.outs)

    def body(*refs):
        a_ref, b_ref = refs[0], refs[1]
        o_ref, acc = refs[2 + n_extra + hi], refs[3 + n_extra + hi + ho]
        k = pl.program_id(2)
        lin = (pl.program_id(0) * grid[1] + pl.program_id(1)) * nk + k
        _run_hook(hook, lin, grid[0] * grid[1] * nk, refs[2 + n_extra:2 + n_extra + hi],
                  refs[3 + n_extra + hi:3 + n_extra + hi + ho], refs[4 + n_extra + hi + ho:])

        if b_parts:
            part = sum(lax.dot_general(a_ref[:, d * PACK_COLS:(d + 1) * PACK_COLS].astype(BF16), b_ref[d].astype(BF16),
                                       dims, preferred_element_type=F32) for d in range(b_parts))
        else:
            part = lax.dot_general(a_ref[...].astype(BF16), b_ref[...].astype(BF16), dims, preferred_element_type=F32)

        def finish(r):
            for e in range(n_extra):
                r = r + refs[2 + e][...].astype(F32)
            o_ref[...] = r.astype(o_ref.dtype)

        if nk == 1:
            finish(part)
        else:
            @pl.when(k == 0)
            def _():
                acc[...] = part

            @pl.when(k > 0)
            def _():
                acc[...] += part

            @pl.when(k == nk - 1)
            def _():
                finish(acc[...])

    res = pl.pallas_call(
        body, name=name, grid=grid,
        in_specs=[a_spec, b_spec] + [sp for _, sp in extra] + [_ANY] * hi,
        out_specs=[o_spec] + [_ANY] * ho, out_shape=[out_shape] + hook.outs,
        scratch_shapes=[pltpu.VMEM(acc_shape if nk > 1 else (8, 128), F32)] + hook.sems,
        compiler_params=_params(("arbitrary", "arbitrary", "arbitrary")),
    )(a, b, *[arr for arr, _ in extra], *hook.ins)
    return res[0] if hook is _NO_HOOK else (res[0], res[1:])


NN, NT, TN = (((1,), (0,)), ((), ())), (((1,), (1,)), ((), ())), (((0,), (0,)), ((), ()))


FULL_K = 2048


def _mm(a, b, mode, *, name, out_dtype=F32, a_win=None, add=None, bias=None, hook=None, b_dev=False,
        out_dest=False):
    b_parts = 0
    if mode == "tn":
        kdim, n = a.shape[0], b.shape[1]
        m = a.shape[1] if a_win is None else a_win[1]
        tm, tn, tk = _tile(m, 512, 128), _tile(n, 640, 128), kdim
        off = 0 if a_win is None else a_win[0] // tm
        a_spec = pl.BlockSpec((tk, tm), lambda i, j, k: (k, i + off))
        b_spec = pl.BlockSpec((tk, tn), lambda i, j, k: (k, j))
        dims = TN
    else:
        m = a.shape[0]
        kdim = a.shape[1] if a_win is None else a_win[1]
        if b_dev:
            n = b.shape[0] * b.shape[2] if mode == "nn" else b.shape[1]
        else:
            n = b.shape[1] if mode == "nn" else b.shape[0]
        tm = _tile(m, 1408 if kdim <= FULL_K else 352, 8)
        tn, tk = _tile(n, 640, 128), kdim
        off = 0 if a_win is None else a_win[0] // tk
        a_spec = pl.BlockSpec((tm, tk), lambda i, j, k: (i, k + off))
        if mode == "nn":
            dims = NN
            if b_dev:
                per = PACK_COLS // tn
                b_spec = pl.BlockSpec((None, tk, tn), lambda i, j, k: (j // per, k, j % per))
            else:
                b_spec = pl.BlockSpec((tk, tn), lambda i, j, k: (k, j))
        else:
            dims = NT
            if b_dev:
                b_parts = kdim // PACK_COLS
                b_spec = pl.BlockSpec((b_parts, tn, PACK_COLS), lambda i, j, k: (0, j, 0))
            else:
                b_spec = pl.BlockSpec((tn, tk), lambda i, j, k: (j, k))
    if a_win is not None:
        assert a_win[0] % (tm if mode == "tn" else tk) == 0
    extra = []
    if add is not None:
        extra.append((add, pl.BlockSpec((tm, tn), lambda i, j, k: (i, j))))
    if bias is not None:
        extra.append((bias, pl.BlockSpec((1, tn), lambda i, j, k: (0, j))))
    if out_dest:
        per = PACK_COLS // tn
        o_spec = pl.BlockSpec((None, tm, tn), lambda i, j, k: (j // per, i, j % per))
        out_shape = jax.ShapeDtypeStruct((n // PACK_COLS, m, PACK_COLS), out_dtype)
    else:
        o_spec = pl.BlockSpec((tm, tn), lambda i, j, k: (i, j))
        out_shape = jax.ShapeDtypeStruct((m, n), out_dtype)
    return _mm_core(a, b, dims=dims, grid=(m // tm, n // tn, kdim // tk), a_spec=a_spec, b_spec=b_spec,
                    o_spec=o_spec, out_shape=out_shape, acc_shape=(tm, tn), name=name, extra=extra, hook=hook,
                    b_parts=b_parts)


def _win(arr, col0=0, width=None, roff=0):
    return (arr, col0, arr.shape[1] if width is None else width, roff)


def _rows(fn, rows, consts, outs, accs, *, name, nrow, tr=CHUNK):
    nr, nc, no = len(rows), len(consts), len(outs)

    def body(*refs):
        i = pl.program_id(0)
        ins = [r[...] for r in refs[:nr + nc]]
        o_refs = refs[nr + nc:nr + nc + no]
        a_refs = refs[nr + nc + no:]
        res_o, res_a = fn(i, *ins)
        for r, v in zip(o_refs, res_o):
            r[...] = v.astype(r.dtype)
        if a_refs:
            @pl.when(i == 0)
            def _():
                for r in a_refs:
                    r[...] = jnp.zeros_like(r)

            for r, v in zip(a_refs, res_a):
                r[...] += v

    in_specs = []
    for (arr, col0, width, roff) in rows:
        assert col0 % width == 0 and arr.shape[0] % tr == 0
        in_specs.append(pl.BlockSpec((tr, width), lambda i, c=col0 // width, ro=roff: (jnp.maximum(i - ro, 0), c)))
    for c in consts:
        in_specs.append(pl.BlockSpec(c.shape, lambda i, nd=c.ndim: (0,) * nd))
    out_specs = [pl.BlockSpec((tr, w), lambda i: (i, 0)) for (w, _) in outs]
    out_specs += [pl.BlockSpec(s, lambda i, nd=len(s): (0,) * nd) for s in accs]
    out_shape = [jax.ShapeDtypeStruct((nrow, w), dt) for (w, dt) in outs]
    out_shape += [jax.ShapeDtypeStruct(s, F32) for s in accs]
    res = pl.pallas_call(
        body, name=name, grid=(nrow // tr,), in_specs=in_specs, out_specs=out_specs, out_shape=out_shape,
        compiler_params=_params(("arbitrary",)),
    )(*[r[0] for r in rows], *consts)
    return res[:no], res[no:]


HEADS_PER_STEP = 2


def _scan_specs(xs, cs, ws, ks, chunk_of, hpb):
    specs = []
    for (arr, width, colfn) in xs:
        specs.append(pl.BlockSpec((CHUNK, width * hpb), lambda h, n, f=colfn: (chunk_of(n), f(h * hpb) // hpb)))
    for (arr, width, colfn) in cs:
        specs.append(pl.BlockSpec((CHUNK, width), lambda h, n, f=colfn: (chunk_of(n), f(h))))
    for arr in list(ws) + list(ks):
        specs.append(pl.BlockSpec((hpb, 1, arr.shape[2]), lambda h, n: (h, 0, 0)))
    return specs


def _scan_fwd(fn, xs, cs, ws, ks, *, heads, nchunk, s_shape, out_w, name, pre=None, hook=None,
              hpb=HEADS_PER_STEP):
    nx, ncs, nw = len(xs), len(cs), len(ws)
    hook = _NO_HOOK if hook is None else hook
    hi, ho = len(hook.ins), len(hook.outs)
    hblocks = heads // hpb

    def body(*refs):
        n = pl.program_id(1)
        nin = nx + ncs + nw + len(ks)
        y_ref, sp_ref = refs[nin + hi], refs[nin + hi + 1]
        s_scr = refs[nin + hi + 2 + ho]
        _run_hook(hook, pl.program_id(0) * nchunk + n, hblocks * nchunk, refs[nin:nin + hi],
                  refs[nin + hi + 2:nin + hi + 2 + ho], refs[nin + hi + 3 + ho:])

        @pl.when(n == 0)
        def _():
            s_scr[...] = jnp.zeros_like(s_scr)

        cv = [r[...] for r in refs[nx:nx + ncs]]
        for e in range(hpb):
            state = s_scr[e]
            sp_ref[e, 0] = state
            xv = [r[:, e * w:(e + 1) * w] for r, (_, w, _) in zip(refs[:nx], xs)]
            wv = [r[e] for r in refs[nx + ncs:nx + ncs + nw]]
            kv = [r[e] for r in refs[nx + ncs + nw:nin]]
            if pre is not None:
                xv = pre(xv, cv)
            y, s_new = fn(n, xv, state, cv, wv, kv)
            y_ref[:, e * out_w:(e + 1) * out_w] = y.astype(y_ref.dtype)
            s_scr[e] = s_new

    lp = nchunk * CHUNK
    res = pl.pallas_call(
        body, name=name, grid=(hblocks, nchunk),
        in_specs=_scan_specs(xs, cs, ws, ks, lambda n: n, hpb) + [_ANY] * hi,
        out_specs=[pl.BlockSpec((CHUNK, out_w * hpb), lambda h, n: (n, h)),
                   pl.BlockSpec((hpb, 1) + s_shape, lambda h, n: (h, n, 0, 0))] + [_ANY] * ho,
        out_shape=[jax.ShapeDtypeStruct((lp, heads * out_w), BF16),
                   jax.ShapeDtypeStruct((heads, nchunk) + s_shape, F32)] + hook.outs,
        scratch_shapes=[pltpu.VMEM((hpb,) + s_shape, F32)] + hook.sems,
        compiler_params=_params(("arbitrary", "arbitrary")),
    )(*[t[0] for t in xs], *[t[0] for t in cs], *ws, *ks, *hook.ins)
    return (res[0], res[1]) if hook is _NO_HOOK else (res[0], res[1], res[2:])


def _scan_bwd(fn, xs, cs, ws, ks, dy, sprev, *, heads, nchunk, s_shape, out_w, name, pre=None, post=None,
              hook=None, hpb=HEADS_PER_STEP):
    nx, ncs, nw = len(xs), len(cs), len(ws)
    nin = nx + ncs + nw + len(ks)
    hook = _NO_HOOK if hook is None else hook
    hi, ho = len(hook.ins), len(hook.outs)
    hblocks = heads // hpb

    def body(*refs):
        step = pl.program_id(1)
        n = nchunk - 1 - step
        dy_ref, sp_ref = refs[nin], refs[nin + 1]
        o0 = nin + 2 + hi
        dx_refs = refs[o0:o0 + nx]
        dw_refs = refs[o0 + nx:o0 + nx + nw]
        ds_scr = refs[o0 + nx + nw + ho]
        _run_hook(hook, pl.program_id(0) * nchunk + step, hblocks * nchunk, refs[nin + 2:o0],
                  refs[o0 + nx + nw:o0 + nx + nw + ho], refs[o0 + nx + nw + ho + 1:])

        @pl.when(step == 0)
        def _():
            ds_scr[...] = jnp.zeros_like(ds_scr)
            for r in dw_refs:
                r[...] = jnp.zeros_like(r)

        cv = [r[...] for r in refs[nx:nx + ncs]]
        for e in range(hpb):
            xv = [r[:, e * w:(e + 1) * w] for r, (_, w, _) in zip(refs[:nx], xs)]
            wv = [r[e] for r in refs[nx + ncs:nx + ncs + nw]]
            kv = [r[e] for r in refs[nx + ncs + nw:nin]]
            if pre is not None:
                xv = pre(xv, cv)
            _, vjp = jax.vjp(lambda xs_, s_, ws_, kv=kv: fn(n, xs_, s_, cv, ws_, kv), xv, sp_ref[e, 0], wv)
            dxs, ds_prev, dws = vjp((dy_ref[:, e * out_w:(e + 1) * out_w].astype(F32), ds_scr[e]))
            if post is not None:
                dxs = post(dxs, cv)
            for r, v, (_, w, _) in zip(dx_refs, dxs, xs):
                r[:, e * w:(e + 1) * w] = v.astype(r.dtype)
            for r, v in zip(dw_refs, dws):
                r[e] += v
            ds_scr[e] = ds_prev

    lp = nchunk * CHUNK
    rev = lambda n: nchunk - 1 - n
    in_specs = _scan_specs(xs, cs, ws, ks, rev, hpb)
    in_specs.append(pl.BlockSpec((CHUNK, out_w * hpb), lambda h, n: (rev(n), h)))
    in_specs.append(pl.BlockSpec((hpb, 1) + s_shape, lambda h, n: (h, rev(n), 0, 0)))
    out_specs = [pl.BlockSpec((CHUNK, w * hpb), lambda h, n: (rev(n), h)) for (_, w, _) in xs]
    out_specs += [pl.BlockSpec((hpb, 1, w.shape[2]), lambda h, n: (h, 0, 0)) for w in ws]
    out_shape = [jax.ShapeDtypeStruct((lp, heads * w), BF16) for (_, w, _) in xs]
    out_shape += [jax.ShapeDtypeStruct(w.shape, F32) for w in ws]
    res = pl.pallas_call(
        body, name=name, grid=(hblocks, nchunk), in_specs=in_specs + [_ANY] * hi,
        out_specs=out_specs + [_ANY] * ho, out_shape=out_shape + hook.outs,
        scratch_shapes=[pltpu.VMEM((hpb,) + s_shape, F32)] + hook.sems,
        compiler_params=_params(("arbitrary", "arbitrary")),
    )(*[t[0] for t in xs], *[t[0] for t in cs], *ws, *ks, dy, sprev, *hook.ins)
    if hook is _NO_HOOK:
        return res[:nx], res[nx:]
    return res[:nx], res[nx:nx + nw], res[nx + nw:]


def _iota2(shape, dim):
    return lax.broadcasted_iota(jnp.int32, shape, dim)


def _ret_chunk(n, xs, state, cs, ws, ks):
    q, k, v, z = xs
    (w,), (lg,) = ws, ks
    lgc = lg[:, :1]
    row, col = _iota2((CHUNK, CHUNK), 0), _iota2((CHUNK, CHUNK), 1)
    diff = jnp.maximum(row - col, 0).astype(F32)
    decay = jnp.where(row >= col, jnp.exp(lg * diff), 0.0)
    scores = _bdot(q, k, 1, 1) * decay
    o_intra = _bdot(scores, v, 1, 0)
    idx = _iota2((CHUNK, 1), 0).astype(F32)
    k_w = k * jnp.exp(lgc * (CHUNK - 1.0 - idx))
    kv = _bdot(k_w, v, 0, 0)
    s_new = state * jnp.exp(lgc * float(CHUNK)) + kv
    q_w = q * jnp.exp(lgc * (idx + 1.0))
    o = o_intra + _bdot(q_w, state, 1, 0)
    return _rms(o, w) * _silu(z), s_new


def _rope(t, cos2, sin2):
    return t * cos2 + pltpu.roll(t, RET_DK // 2, 1) * sin2


def _rope_t(g, cos2, sin2):
    return g * cos2 - pltpu.roll(g, RET_DK // 2, 1) * sin2


def _ret_pre(xv, cv):
    q, k, v, z = xv
    cos2, sin2 = cv
    return [_rope(q, cos2, sin2), _rope(k, cos2, sin2) * (RET_DK ** -0.5), v, z]


def _ret_post(dxs, cv):
    dq, dk, dv, dz = dxs
    cos2, sin2 = cv
    return [_rope_t(dq, cos2, sin2), _rope_t(dk, cos2, sin2) * (RET_DK ** -0.5), dv, dz]


def _gla_chunk(n, xs, state_t, cs, ws, ks):
    q, k, v, z, pre = xs
    (w,) = ws
    q = q * (GLA_DK ** -0.5)
    rowc = _iota2((CHUNK, 1), 0)
    valid = jnp.logical_or(n > 0, rowc >= PAD)
    log_a = jnp.where(valid, _log_sigmoid(pre) / GLA_TAU, 0.0)
    row, col = _iota2((CHUNK, CHUNK), 0), _iota2((CHUNK, CHUNK), 1)
    tri = (row >= col).astype(F32)
    b = jnp.dot(tri, log_a, precision=lax.Precision.HIGHEST, preferred_element_type=F32)
    b_last = b[CHUNK - 1:CHUNK, :]
    kv_t = _bdot(v, k * jnp.exp(b_last - b), 0, 0)
    s_new = state_t * jnp.exp(b_last) + kv_t
    o_inter = _bdot(q * jnp.exp(b), state_t, 1, 1)
    outs = []
    for s in range(CHUNK // SUB):
        lo, hi = s * SUB, (s + 1) * SUB
        b_ref = jnp.zeros_like(b_last) if s == 0 else b[lo - 1:lo, :]
        q_hat = q[lo:hi] * jnp.exp(b[lo:hi] - b_ref)
        k_hat = k[:hi] * jnp.exp(b_ref - b[:hi])
        sc = _bdot(q_hat, k_hat, 1, 1)
        causal = _iota2((SUB, hi), 0) + lo >= _iota2((SUB, hi), 1)
        outs.append(_bdot(jnp.where(causal, sc, 0.0), v[:hi], 1, 0))
    o = jnp.concatenate(outs, axis=0) + o_inter
    return _rms(o, w) * _silu(z), s_new


def _s5_disc(lam_re, lam_im, log_dt, b_re, b_im, expand):
    dt = jnp.exp(log_dt)
    mag = jnp.exp(lam_re * dt)
    ab_re, ab_im = mag * jnp.cos(lam_im * dt), mag * jnp.sin(lam_im * dt)
    den = lam_re * lam_re + lam_im * lam_im
    nr, ni = ab_re - 1.0, ab_im
    f_re = (nr * lam_re + ni * lam_im) / den
    f_im = (ni * lam_re - nr * lam_im) / den
    hp = lax.Precision.HIGHEST
    f_re = jnp.dot(f_re, expand, precision=hp, preferred_element_type=F32)
    f_im = jnp.dot(f_im, expand, precision=hp, preferred_element_type=F32)
    return ab_re, ab_im, f_re * b_re - f_im * b_im, f_re * b_im + f_im * b_re


def _s5_disc_fwd(args):
    def body(*refs):
        outs = _s5_disc(*[r[...] for r in refs[:6]])
        for r, v in zip(refs[6:], outs):
            r[...] = v

    g, p = args[0].shape
    return pl.pallas_call(
        body, name="s5_disc_fwd",
        out_shape=[jax.ShapeDtypeStruct((g, p), F32)] * 2 + [jax.ShapeDtypeStruct(args[3].shape, F32)] * 2,
    )(*args)


def _s5_disc_bwd(args, cts):
    def body(*refs):
        prim = [r[...] for r in refs[:5]]
        expand = refs[5][...]
        ct = tuple(r[...] for r in refs[6:10])
        _, vjp = jax.vjp(lambda *a: _s5_disc(*a, expand), *prim)
        for r, v in zip(refs[10:], vjp(ct)):
            r[...] = v

    return pl.pallas_call(
        body, name="s5_disc_bwd", out_shape=[jax.ShapeDtypeStruct(a.shape, F32) for a in args[:5]],
    )(*args, *cts)


S5_SUBL = 8
S5_LANES = S5_N // S5_SUBL
S5_TB = 64


def _s5_scan_fwd(bu, a_re, a_im):
    lp = bu.shape[0]

    def body(bu_ref, ar_ref, ai_ref, x_ref, st):
        @pl.when(pl.program_id(0) == 0)
        def _():
            st[...] = jnp.zeros_like(st)

        ar, ai = ar_ref[...], ai_ref[...]

        def step(t, carry):
            xr, xi = carry
            nr = ar * xr - ai * xi + bu_ref[t, 0:S5_SUBL, :]
            ni = ar * xi + ai * xr + bu_ref[t, S5_SUBL:2 * S5_SUBL, :]
            x_ref[t, 0:S5_SUBL, :] = nr
            x_ref[t, S5_SUBL:2 * S5_SUBL, :] = ni
            return nr, ni

        xr, xi = lax.fori_loop(0, S5_TB, step, (st[0], st[1]))
        st[0] = xr
        st[1] = xi

    blk = pl.BlockSpec((S5_TB, 2 * S5_SUBL, S5_LANES), lambda i: (i, 0, 0))
    cst = pl.BlockSpec((S5_SUBL, S5_LANES), lambda i: (0, 0))
    return pl.pallas_call(
        body, name="s5_scan_fwd", grid=(lp // S5_TB,), in_specs=[blk, cst, cst], out_specs=blk,
        out_shape=jax.ShapeDtypeStruct(bu.shape, F32),
        scratch_shapes=[pltpu.VMEM((2, S5_SUBL, S5_LANES), F32)],
        compiler_params=_params(("arbitrary",)),
    )(bu, a_re, a_im)


def _s5_scan_bwd(gx, x, a_re, a_im):
    lp = gx.shape[0]
    nb = lp // S5_TB

    def body(gx_ref, x_ref, xp_ref, ar_ref, ai_ref, g_ref, da_ref, st):
        i = pl.program_id(0)

        @pl.when(i == 0)
        def _():
            st[...] = jnp.zeros_like(st)
            da_ref[...] = jnp.zeros_like(da_ref)

        ar, ai = ar_ref[...], ai_ref[...]
        first = (i == nb - 1).astype(F32)

        def step(s, carry):
            gr, gi, dar, dai = carry
            t = S5_TB - 1 - s
            ngr = gx_ref[t, 0:S5_SUBL, :] + ar * gr + ai * gi
            ngi = gx_ref[t, S5_SUBL:2 * S5_SUBL, :] + ar * gi - ai * gr
            g_ref[t, 0:S5_SUBL, :] = ngr
            g_ref[t, S5_SUBL:2 * S5_SUBL, :] = ngi
            tp = jnp.maximum(t - 1, 0)
            at0 = (t == 0).astype(F32)
            keep = 1.0 - at0
            pr = keep * x_ref[tp, 0:S5_SUBL, :] + at0 * (1.0 - first) * xp_ref[0, 0:S5_SUBL, :]
            pi = keep * x_ref[tp, S5_SUBL:2 * S5_SUBL, :] + at0 * (1.0 - first) * xp_ref[0, S5_SUBL:2 * S5_SUBL, :]
            return ngr, ngi, dar + ngr * pr + ngi * pi, dai + ngi * pr - ngr * pi

        zero = jnp.zeros((S5_SUBL, S5_LANES), F32)
        gr, gi, dar, dai = lax.fori_loop(0, S5_TB, step, (st[0], st[1], zero, zero))
        st[0] = gr
        st[1] = gi
        da_ref[0] += dar
        da_ref[1] += dai

    rev = lambda i: nb - 1 - i
    blk = pl.BlockSpec((S5_TB, 2 * S5_SUBL, S5_LANES), lambda i: (rev(i), 0, 0))
    prev = pl.BlockSpec((1, 2 * S5_SUBL, S5_LANES), lambda i: (jnp.maximum(rev(i) * S5_TB - 1, 0), 0, 0))
    cst = pl.BlockSpec((S5_SUBL, S5_LANES), lambda i: (0, 0))
    return pl.pallas_call(
        body, name="s5_scan_bwd", grid=(nb,), in_specs=[blk, blk, prev, cst, cst],
        out_specs=[blk, pl.BlockSpec((2, S5_SUBL, S5_LANES), lambda i: (0, 0, 0))],
        out_shape=[jax.ShapeDtypeStruct(gx.shape, F32), jax.ShapeDtypeStruct((2, S5_SUBL, S5_LANES), F32)],
        scratch_shapes=[pltpu.VMEM((2, S5_SUBL, S5_LANES), F32)],
        compiler_params=_params(("arbitrary",)),
    )(gx, x, x, a_re, a_im)


def _place():
    x, y, c = lax.axis_index("x"), lax.axis_index("y"), lax.axis_index("c")
    return x, y, c, [(1 - x, y), (x, 1 - y), (1 - x, 1 - y)]


def _gather_phases():
    def plan(x_ref, out_ref, send_sems, recv_sems, local_sem):
        x, y, c, chips = _place()
        me, sibling = (x, y, c), (x, y, 1 - c)

        def rows(px, py, pc):
            return out_ref.at[4 * px + 2 * py + pc]

        def copy(k, block, to, src=None):
            return pltpu.make_async_remote_copy(
                src_ref=rows(*block) if src is None else src, dst_ref=rows(*block),
                send_sem=send_sems.at[k], recv_sem=recv_sems.at[k], device_id=to, device_id_type=MESH)

        mine = pltpu.make_async_copy(x_ref, rows(*me), local_sem)
        first = [copy(0, me, sibling, src=x_ref)]
        first += [copy(1 + j, me, (*chip, c), src=x_ref) for j, chip in enumerate(chips)]
        passed = [copy(4 + j, (*chip, c), sibling) for j, chip in enumerate(chips)]
        return c, chips, me, sibling, copy, mine, first, passed

    def start(ins, outs, sems):
        _, _, _, _, _, mine, first, _ = plan(ins[0], outs[0], *sems)
        mine.start()
        for cp in first:
            cp.start()

    def middle(ins, outs, sems):
        c, chips, me, _, copy, _, _, passed = plan(ins[0], outs[0], *sems)
        for j, chip in enumerate(chips):
            copy(1 + j, (*chip, c), me).wait_recv()
            passed[j].start()

    def finish(ins, outs, sems):
        c, chips, me, sibling, copy, mine, first, passed = plan(ins[0], outs[0], *sems)
        copy(0, sibling, me).wait_recv()
        for j, chip in enumerate(chips):
            copy(4 + j, (*chip, 1 - c), me).wait_recv()
        for cp in first + passed:
            cp.wait_send()
        mine.wait()

    return start, middle, finish


_GATHER_SEMS = [pltpu.SemaphoreType.DMA((7,)), pltpu.SemaphoreType.DMA((7,)), pltpu.SemaphoreType.DMA]


def _all_gather(shard, name):
    phases = _gather_phases()

    def body(x_ref, out_ref, *sems):
        for phase in phases:
            phase([x_ref], [out_ref], sems)

    return pl.pallas_call(
        body, name=name, out_shape=jax.ShapeDtypeStruct((N_DEV,) + shard.shape, shard.dtype),
        in_specs=[_ANY], out_specs=_ANY, scratch_shapes=list(_GATHER_SEMS),
    )(shard)


def _gather_hook(shard):
    start, middle, finish = _gather_phases()
    return _Hook([shard], [jax.ShapeDtypeStruct((N_DEV,) + shard.shape, shard.dtype)], _GATHER_SEMS,
                 [(0.0, start), (0.85, middle), (1.0, finish)])


def _swap_with_sibling(parts, name):
    def body(p_ref, out_ref, send_sems, recv_sems):
        x, y, c, _ = _place()
        copies = [pltpu.make_async_remote_copy(
            src_ref=p_ref.at[2 * chip + (1 - c)], dst_ref=out_ref.at[chip],
            send_sem=send_sems.at[chip], recv_sem=recv_sems.at[chip],
            device_id=(x, y, 1 - c), device_id_type=MESH) for chip in range(4)]
        for cp in copies:
            cp.start()
        for cp in copies:
            cp.wait()

    return pl.pallas_call(
        body, name=name, out_shape=jax.ShapeDtypeStruct((4,) + parts.shape[1:], parts.dtype),
        in_specs=[pl.BlockSpec(memory_space=pl.ANY)], out_specs=pl.BlockSpec(memory_space=pl.ANY),
        scratch_shapes=[pltpu.SemaphoreType.DMA((4,)), pltpu.SemaphoreType.DMA((4,))],
    )(parts)


def _chips_phases():
    def copies(p_ref, out_ref, send_sems, recv_sems):
        x, y, c, chips = _place()
        return [pltpu.make_async_remote_copy(
            src_ref=p_ref.at[2 * px + py], dst_ref=out_ref.at[j],
            send_sem=send_sems.at[j], recv_sem=recv_sems.at[j],
            device_id=(px, py, c), device_id_type=MESH) for j, (px, py) in enumerate(chips)]

    def start(ins, outs, sems):
        for cp in copies(ins[0], outs[0], *sems):
            cp.start()

    def finish(ins, outs, sems):
        for cp in copies(ins[0], outs[0], *sems):
            cp.wait()

    return start, finish


def _chips_hook(parts):
    start, finish = _chips_phases()
    return _Hook([parts], [jax.ShapeDtypeStruct((3,) + parts.shape[1:], parts.dtype)],
                 [pltpu.SemaphoreType.DMA((3,)), pltpu.SemaphoreType.DMA((3,))], [(0.0, start), (1.0, finish)])


def _pack_rows(n_elem, row_mult=PACK_ROW_MULT):
    rows = -(-n_elem // PACK_COLS)
    return -(-rows // row_mult) * row_mult


def _pack(flats, dtype, row_mult=PACK_ROW_MULT):
    flat = jnp.concatenate([f.reshape(-1).astype(dtype) for f in flats])
    rows = _pack_rows(flat.shape[0], row_mult)
    return jnp.pad(flat, (0, rows * PACK_COLS - flat.shape[0])).reshape(rows, PACK_COLS)


def _unpack(buf, shapes):
    lead = buf.shape[:-2]
    flat = buf.reshape(lead + (-1,))
    outs, o = [], 0
    for s in shapes:
        n = math.prod(s)
        outs.append(flat[..., o:o + n].reshape(lead + tuple(s)))
        o += n
    return outs


BIG_LAYOUT = (("w_in_ab", D_MODEL, PACK_COLS), ("s5_w_glu", S5_W // N_DEV, PACK_COLS),
              ("w_out_ab", OUT_AB // N_DEV, 2 * PACK_COLS), ("w_in_c", D_MODEL, PACK_COLS),
              ("w_out_c", GLA_W // N_DEV, 2 * PACK_COLS))


def _to_rows(a):
    if a.shape[-1] == PACK_COLS:
        return a
    assert a.shape[-1] == 2 * PACK_COLS
    return jnp.concatenate([a[..., :PACK_COLS], a[..., PACK_COLS:]], axis=-2)


def _from_rows(p, cols):
    if cols == PACK_COLS:
        return p
    r = p.shape[-2] // 2
    return jnp.concatenate([p[..., :r, :], p[..., r:, :]], axis=-1)


FIRST_LAYOUT = BIG_LAYOUT[:1]
OTHER_LAYOUT = BIG_LAYOUT[1:3] + BIG_LAYOUT[4:]
IN_C_LAYOUT = BIG_LAYOUT[3:4]


def _pack_big(pieces, layout):
    return jnp.concatenate([_to_rows(pieces[name]) for name, _, _ in layout], axis=-2)


def _unpack_big(buf, layout):
    out, o = {}, 0
    for name, rows, cols in layout:
        r = rows * cols // PACK_COLS
        out[name] = _from_rows(buf[..., o:o + r, :], cols)
        o += r
    return out


def _rows1024(a):
    r, c = a.shape
    if c > PACK_COLS:
        a = jnp.concatenate([a[:, i * PACK_COLS:(i + 1) * PACK_COLS] for i in range(c // PACK_COLS)], axis=0)
    elif c < PACK_COLS:
        a = jnp.pad(a, ((0, 0), (0, PACK_COLS - c)))
    return jnp.pad(a, ((0, -a.shape[0] % 8), (0, 0)))


def _unrows1024(p, r, c):
    if c > PACK_COLS:
        return jnp.concatenate([p[i * r:(i + 1) * r] for i in range(c // PACK_COLS)], axis=1)
    return p[:r, :c]


def _lane_select(a, off, sign, n_out, out_dtype, exact, name):
    rows, n_in = a.shape
    tr = _tile(rows, 256, 16)

    def body(off_ref, a_ref, o_ref):
        sel = _iota2((n_in, n_out), 0) + off_ref[0] * sign == _iota2((n_in, n_out), 1)
        if exact:
            r = jnp.dot(a_ref[...], sel.astype(F32), precision=lax.Precision.HIGHEST, preferred_element_type=F32)
        else:
            r = _dg(a_ref[...], sel.astype(BF16), 1, 0)
        o_ref[...] = r.astype(out_dtype)

    return pl.pallas_call(
        body, name=name, grid=(rows // tr,),
        in_specs=[pl.BlockSpec(memory_space=pltpu.SMEM), pl.BlockSpec((tr, n_in), lambda i: (i, 0))],
        out_specs=pl.BlockSpec((tr, n_out), lambda i: (i, 0)),
        out_shape=jax.ShapeDtypeStruct((rows, n_out), out_dtype),
        compiler_params=_params(("arbitrary",)),
    )(off, a)


def _adamw(w, g, m, v, name):
    rows, cols = w.shape
    tr = _tile(rows, 256, 8) if rows % 8 == 0 else rows

    def fn(i, w_, g_, m_, v_):
        m_new = ADAM_B1 * m_ + (1.0 - ADAM_B1) * g_
        v_new = ADAM_B2 * v_ + (1.0 - ADAM_B2) * (g_ * g_)
        m_hat = m_new / (1.0 - ADAM_B1 ** ADAM_STEP)
        v_hat = v_new / (1.0 - ADAM_B2 ** ADAM_STEP)
        delta = -ADAM_LR * (m_hat / (jnp.sqrt(v_hat) + ADAM_EPS) + ADAM_WD * w_)
        return (delta, m_new, v_new), ()

    outs, _ = _rows(fn, [_win(w), _win(g), _win(m), _win(v)], [], [(cols, F32)] * 3, [], name=name,
                    nrow=rows, tr=tr)
    return outs


def _as2d(a):
    if a.ndim == 1:
        return a.reshape(1, -1)
    if a.ndim == 2:
        return a
    a = a.reshape(a.shape[1:])
    return a if a.ndim == 2 else a.reshape(a.shape[0], -1)


def kernel(x, meta, norm_ab_w, w_in_ab, ret_norm_w, s5_lam_re, s5_lam_im, s5_log_dt, s5_b_re, s5_b_im, s5_c_re, s5_c_im, s5_d, s5_w_glu, w_out_ab, norm_c_w, w_in_c, gla_w_gate, gla_b_gate, gla_norm_w, w_out_c, final_norm_w, loss_target, m_meta, m_norm_ab_w, m_w_in_ab, m_ret_norm_w, m_s5_lam_re, m_s5_lam_im, m_s5_log_dt, m_s5_b_re, m_s5_b_im, m_s5_c_re, m_s5_c_im, m_s5_d, m_s5_w_glu, m_w_out_ab, m_norm_c_w, m_w_in_c, m_gla_w_gate, m_gla_b_gate, m_gla_norm_w, m_w_out_c, m_final_norm_w, v_meta, v_norm_ab_w, v_w_in_ab, v_ret_norm_w, v_s5_lam_re, v_s5_lam_im, v_s5_log_dt, v_s5_b_re, v_s5_b_im, v_s5_c_re, v_s5_c_im, v_s5_d, v_s5_w_glu, v_w_out_ab, v_norm_c_w, v_w_in_c, v_gla_w_gate, v_gla_b_gate, v_gla_norm_w, v_w_out_c, v_final_norm_w):
    weights = dict(meta=meta, norm_ab_w=norm_ab_w, w_in_ab=w_in_ab, ret_norm_w=ret_norm_w, s5_lam_re=s5_lam_re,
                   s5_lam_im=s5_lam_im, s5_log_dt=s5_log_dt, s5_b_re=s5_b_re, s5_b_im=s5_b_im, s5_c_re=s5_c_re,
                   s5_c_im=s5_c_im, s5_d=s5_d, s5_w_glu=s5_w_glu, w_out_ab=w_out_ab, norm_c_w=norm_c_w,
                   w_in_c=w_in_c, gla_w_gate=gla_w_gate, gla_b_gate=gla_b_gate, gla_norm_w=gla_norm_w,
                   w_out_c=w_out_c, final_norm_w=final_norm_w)
    mom_m = dict(meta=m_meta, norm_ab_w=m_norm_ab_w, w_in_ab=m_w_in_ab, ret_norm_w=m_ret_norm_w,
                 s5_lam_re=m_s5_lam_re, s5_lam_im=m_s5_lam_im, s5_log_dt=m_s5_log_dt, s5_b_re=m_s5_b_re,
                 s5_b_im=m_s5_b_im, s5_c_re=m_s5_c_re, s5_c_im=m_s5_c_im, s5_d=m_s5_d, s5_w_glu=m_s5_w_glu,
                 w_out_ab=m_w_out_ab, norm_c_w=m_norm_c_w, w_in_c=m_w_in_c, gla_w_gate=m_gla_w_gate,
                 gla_b_gate=m_gla_b_gate, gla_norm_w=m_gla_norm_w, w_out_c=m_w_out_c, final_norm_w=m_final_norm_w)
    mom_v = dict(meta=v_meta, norm_ab_w=v_norm_ab_w, w_in_ab=v_w_in_ab, ret_norm_w=v_ret_norm_w,
                 s5_lam_re=v_s5_lam_re, s5_lam_im=v_s5_lam_im, s5_log_dt=v_s5_log_dt, s5_b_re=v_s5_b_re,
                 s5_b_im=v_s5_b_im, s5_c_re=v_s5_c_re, s5_c_im=v_s5_c_im, s5_d=v_s5_d, s5_w_glu=v_s5_w_glu,
                 w_out_ab=v_w_out_ab, norm_c_w=v_norm_c_w, w_in_c=v_w_in_c, gla_w_gate=v_gla_w_gate,
                 gla_b_gate=v_gla_b_gate, gla_norm_w=v_gla_norm_w, w_out_c=v_w_out_c, final_norm_w=v_final_norm_w)
    order = list(weights)

    seq = x.shape[1]
    lp = CHUNK + seq
    nchunk = lp // CHUNK
    dev = 4 * lax.axis_index("x") + 2 * lax.axis_index("y") + lax.axis_index("c")
    core = lax.axis_index("c")
    chip = 2 * lax.axis_index("x") + lax.axis_index("y")

    win_off = jnp.reshape(2 * dev, (1,)).astype(jnp.int32)
    shard_c = jnp.pad(w_in_c[0].astype(BF16), ((0, 0), (0, 896 - SHARD_C)))
    big_shards = dict(w_in_ab=w_in_ab[0].astype(BF16), s5_w_glu=s5_w_glu[0].astype(BF16),
                      w_out_ab=w_out_ab[0].astype(BF16), w_out_c=w_out_c[0].astype(BF16),
                      w_in_c=_lane_select(shard_c, win_off, 1, WIN_COLS, BF16, False, "w_in_c_to_window"))
    def pad_to(a, rows, cols):
        return jnp.pad(a, ((0, rows - a.shape[0]), (0, cols - a.shape[1])))

    shard_w = D_MODEL // N_DEV
    small_pack = jnp.concatenate([meta, pad_to(norm_c_w, 8, shard_w), pad_to(gla_w_gate[0], GLA_RANK, shard_w),
                                  pad_to(gla_b_gate, 8, shard_w), pad_to(gla_norm_w, 8, shard_w)], axis=0)
    w_in_ab_g = _all_gather(big_shards["w_in_ab"], "gather_first")
    half = D_MODEL // 2
    in_c_hook_a = _gather_hook(big_shards["w_in_c"][:half])
    in_c_hook_b = _gather_hook(big_shards["w_in_c"][half:])
    other_hook = _gather_hook(_pack_big(big_shards, OTHER_LAYOUT))
    gs = _all_gather(small_pack, "gather_small")
    gate_w = GLA_QK // N_DEV
    s_meta, s_norm_c = gs[:, :N_META], gs[:, N_META]
    s_wgate, s_bgate, s_gnorm = gs[:, 24:24 + GLA_RANK, :gate_w], gs[:, 40, :gate_w], gs[:, 48]
    meta_f = s_meta.transpose(1, 0, 2).reshape(N_META, D_MODEL)
    norm_c_f = s_norm_c.reshape(1, D_MODEL)
    w_gate_f = jnp.pad(s_wgate.transpose(1, 0, 2).reshape(GLA_RANK, GLA_QK), ((0, GATE_PAD - GLA_RANK), (0, 0)))
    b_gate_f = s_bgate.reshape(1, GLA_QK)
    gla_norm_f = s_gnorm.reshape(GLA_H, 1, GLA_DV)

    pos = jnp.maximum(jnp.arange(lp, dtype=F32) - float(PAD), 0.0)
    inv_freq = jnp.power(ROPE_BASE, -jnp.arange(0, RET_DK, 2, dtype=F32) / RET_DK)
    ang = pos[:, None] * inv_freq[None, :]
    cos2 = jnp.concatenate([jnp.cos(ang), jnp.cos(ang)], axis=1)
    sin2 = jnp.concatenate([-jnp.sin(ang), jnp.sin(ang)], axis=1)
    log_g = jnp.log1p(-jnp.exp2(-5.0 - jnp.arange(RET_H, dtype=F32)))
    lg = jnp.broadcast_to(log_g[:, None, None], (RET_H, 1, 128))
    ret_norm_h = ret_norm_w.reshape(RET_H, 1, RET_DV)

    h0 = jnp.concatenate([jnp.zeros((PAD, D_MODEL), F32), meta_f, x[0]], axis=0)

    def rowmask(i):
        return (_iota2((CHUNK, 1), 0) + i * CHUNK) >= PAD

    (hn0,), _ = _rows(lambda i, h, w: ((_rms(h, w),), ()), [_win(h0)], [norm_ab_w], [(D_MODEL, BF16)], [],
                      name="norm_ab_fwd", nrow=lp)
    proj_ab, (w_in_c_ga,) = _mm(hn0, w_in_ab_g, "nn", name="in_ab_fwd", hook=in_c_hook_a, b_dev=True)

    q_off, k_off, v_off, za_off = 0, RET_QK, 2 * RET_QK, 2 * RET_QK + RET_W
    u_off, zb_off = 2 * RET_QK + 2 * RET_W, 2 * RET_QK + 2 * RET_W + S5_W
    ret_xs = [(proj_ab, RET_DK, lambda h: q_off // RET_DK + h), (proj_ab, RET_DK, lambda h: k_off // RET_DK + h),
              (proj_ab, RET_DV, lambda h: v_off // RET_DV + h), (proj_ab, RET_DV, lambda h: za_off // RET_DV + h)]
    ret_cs = [(cos2, RET_DK, lambda h: 0), (sin2, RET_DK, lambda h: 0)]
    ret_kw = dict(heads=RET_H, nchunk=nchunk, s_shape=(RET_DK, RET_DV), out_w=RET_DV, pre=_ret_pre, hpb=4)
    o_a, ret_sprev, (gathered_other,) = _scan_fwd(_ret_chunk, ret_xs, ret_cs, [ret_norm_h], [lg], name="ret_fwd",
                                                  hook=other_hook, **ret_kw)
    gb = _unpack_big(gathered_other, OTHER_LAYOUT)
    w_glu_f = gb["s5_w_glu"].reshape(S5_W, S5_W)
    w_out_ab_f = gb["w_out_ab"].reshape(OUT_AB, D_MODEL)
    w_out_c_f = gb["w_out_c"].reshape(GLA_W, D_MODEL)

    expand = jnp.repeat(jnp.eye(S5_P, dtype=F32), S5_GH, axis=1)
    disc_args = (s5_lam_re[0], s5_lam_im[0], s5_log_dt[0].reshape(S5_G, 1),
                 s5_b_re[0].reshape(S5_G, S5_P * S5_GH), s5_b_im[0].reshape(S5_G, S5_P * S5_GH), expand)
    ab_re, ab_im, bb_re, bb_im = _s5_disc_fwd(disc_args)
    gt = S5_SUBL
    eye_t = jnp.eye(gt, dtype=F32)

    def tiles_in(bb):
        return jnp.einsum("sgph,gk->sghkp", bb.reshape(gt, gt, S5_P, S5_GH), eye_t).reshape(gt, 128, S5_LANES)

    def tiles_out(cc):
        return jnp.einsum("sghp,gk->sgpkh", cc.reshape(gt, gt, S5_GH, S5_P), eye_t).reshape(gt, S5_LANES, 128)

    wb_t = jnp.concatenate([tiles_in(bb_re), tiles_in(bb_im)], axis=0).astype(BF16)
    wc_t = jnp.concatenate([tiles_out(s5_c_re[0]), -tiles_out(s5_c_im[0])], axis=0).astype(BF16)
    a_re, a_im = ab_re.reshape(S5_SUBL, S5_LANES), ab_im.reshape(S5_SUBL, S5_LANES)
    tm5, tk5, nt5 = _tile(lp, 1408, 8), _tile(lp, 1408, 8), 2 * gt
    u_blk = u_off // 128
    wide = pl.BlockSpec((tm5, S5_LANES), lambda i, j, k: (i, j))
    wide_k = pl.BlockSpec((tm5, S5_LANES), lambda i, j, k: (i, k * gt + j))
    narrow = pl.BlockSpec((tm5, 128), lambda i, j, k: (i, j))
    wb_j = pl.BlockSpec((None, 128, S5_LANES), lambda i, j, k: (j, 0, 0))
    wc_j = pl.BlockSpec((None, S5_LANES, 128), lambda i, j, k: (j, 0, 0))
    wb_k = pl.BlockSpec((None, 128, S5_LANES), lambda i, j, k: (k * gt + j, 0, 0))
    wc_k = pl.BlockSpec((None, S5_LANES, 128), lambda i, j, k: (k * gt + j, 0, 0))
    wide_shape = jax.ShapeDtypeStruct((lp, 2 * S5_N), F32)
    bu = _mm_core(proj_ab, wb_t, dims=NN, grid=(lp // tm5, nt5, 1), name="s5_bu",
                  a_spec=pl.BlockSpec((tm5, 128), lambda i, j, k: (i, u_blk + j % gt)), b_spec=wb_j,
                  o_spec=wide, out_shape=wide_shape, acc_shape=(tm5, S5_LANES))
    xs5 = _s5_scan_fwd(bu.reshape(lp, 2 * S5_SUBL, S5_LANES), a_re, a_im)
    xs5_2d = xs5.reshape(lp, 2 * S5_N)
    y_pre = _mm_core(xs5_2d, wc_t, dims=NN, grid=(lp // tm5, gt, 2), name="s5_cx", a_spec=wide_k, b_spec=wc_k,
                     o_spec=narrow, out_shape=jax.ShapeDtypeStruct((lp, S5_W), F32), acc_shape=(tm5, 128))
    (y_s5, yg_bf), _ = _rows(
        lambda i, yp, u, d: ((yp + d * u, _gelu(yp + d * u)), ()),
        [_win(y_pre), _win(proj_ab, u_off, S5_W)], [s5_d], [(S5_W, F32), (S5_W, BF16)], [], name="s5_gelu_fwd", nrow=lp)
    t_glu = _mm(yg_bf, w_glu_f, "nn", name="s5_glu_fwd")

    def s5_gate(y, t, zb):
        return _gelu(y) * _sigmoid(t) * _silu(zb)

    (o_b,), _ = _rows(lambda i, y, t, zb: ((s5_gate(y, t, zb),), ()),
                      [_win(y_s5), _win(t_glu), _win(proj_ab, zb_off, S5_W)], [], [(S5_W, BF16)], [],
                      name="s5_gate_fwd", nrow=lp)
    o_ab = jnp.concatenate([o_a, o_b], axis=1)
    h1, (w_in_c_gb,) = _mm(o_ab, w_out_ab_f, "nn", name="out_ab_fwd", add=h0, hook=in_c_hook_b)
    w_in_c_g = jnp.concatenate([w_in_c_ga, w_in_c_gb], axis=1)
    w_in_c_f = sum(jnp.pad(w_in_c_g[d], ((0, 0), (WIN_STEP * d, IN_C_PAD - WIN_STEP * d - WIN_COLS)))
                   for d in range(N_DEV))

    (hn1,), _ = _rows(lambda i, h, w: ((_rms(h, w),), ()), [_win(h1)], [norm_c_f], [(D_MODEL, BF16)], [],
                      name="norm_c_fwd", nrow=lp)
    proj_c = _mm(hn1, w_in_c_f, "nn", name="in_c_fwd")
    gl_off = 2 * GLA_QK + 2 * GLA_W
    pre_gate = _mm(proj_c, w_gate_f, "nn", name="gate_fwd", a_win=(gl_off, GATE_PAD), bias=b_gate_f)
    gla_xs = [(proj_c, GLA_DK, lambda h: h), (proj_c, GLA_DK, lambda h: GLA_QK // GLA_DK + h),
              (proj_c, GLA_DV, lambda h: 2 * GLA_QK // GLA_DV + h),
              (proj_c, GLA_DV, lambda h: (2 * GLA_QK + GLA_W) // GLA_DV + h),
              (pre_gate, GLA_DK, lambda h: h)]
    gla_kw = dict(heads=GLA_H, nchunk=nchunk, s_shape=(GLA_DV, GLA_DK), out_w=GLA_DV, hpb=GLA_H)
    o_c, gla_sprev = _scan_fwd(_gla_chunk, gla_xs, [], [gla_norm_f], [], name="gla_fwd", **gla_kw)
    h2 = _mm(o_c, w_out_c_f, "nn", name="out_c_fwd", add=h1)

    fnw = final_norm_w.reshape(1, D_MODEL)

    def final_fn(i, h, tgt, w):
        def loss_of(h_, w_):
            err = _rms(h_, w_) - tgt
            return 0.5 * jnp.sum(jnp.mean(err * err, axis=-1))

        real = (i > 0).astype(F32)
        loss_i, (dh, dw) = jax.value_and_grad(loss_of, argnums=(0, 1))(h, w)
        return (dh * real,), (jnp.full((1, 128), loss_i * real, F32), dw * real)

    (dh2,), (loss_acc, g_final) = _rows(final_fn, [_win(h2), _win(loss_target[0], roff=1)], [fnw],
                                        [(D_MODEL, F32)], [(1, 128), (1, D_MODEL)], name="final_loss", nrow=lp)

    def rs_front(pieces, layout, tag):
        g_full = _pack_big(pieces, layout)
        prow = g_full.shape[1]
        from_sibling = _swap_with_sibling(g_full, "rs_sibling_" + tag)
        mine_by_chip = lax.dynamic_index_in_dim(g_full.reshape(4, 2, prow, PACK_COLS), core, axis=1, keepdims=False)
        (p1, p1_bf), _ = _rows(
            lambda i, a, b: ((a.astype(F32) + b.astype(F32), a.astype(F32) + b.astype(F32)), ()),
            [_win(mine_by_chip.reshape(4 * prow, PACK_COLS)), _win(from_sibling.reshape(4 * prow, PACK_COLS))], [],
            [(PACK_COLS, F32), (PACK_COLS, BF16)], [], name="rs_sum_sibling_" + tag, nrow=4 * prow,
            tr=_tile(prow, 512, 16))
        return p1.reshape(4, prow, PACK_COLS), p1_bf.reshape(4, prow, PACK_COLS)

    def rs_back(p1, from_chips, layout, tag):
        prow = p1.shape[1]
        tr = _tile(prow, 512, 16)
        own = lax.dynamic_index_in_dim(p1, chip, axis=0, keepdims=False)
        fc2 = from_chips.reshape(3 * prow, PACK_COLS)
        nblk = prow // tr
        (g_shard,), _ = _rows(
            lambda i, a, b0, b1, b2: ((((a + b0.astype(F32)) + b1.astype(F32)) + b2.astype(F32),), ()),
            [_win(own), _win(fc2), _win(fc2, roff=-nblk), _win(fc2, roff=-2 * nblk)], [], [(PACK_COLS, F32)], [],
            name="rs_sum_chips_" + tag, nrow=prow, tr=tr)
        return _unpack_big(g_shard, layout)

    dh2_bf = dh2.astype(BF16)
    do_c = _mm(dh2_bf, w_out_c_f, "nt", name="out_c_dx", out_dtype=BF16)
    gw_out_c = _mm(o_c, dh2_bf, "tn", name="out_c_dw", out_dtype=BF16)
    (dq_c, dk_c, dv_c, dz_c, dpre), (g_gla_norm,) = _scan_bwd(
        _gla_chunk, gla_xs, [], [gla_norm_f], [], do_c, gla_sprev, name="gla_bwd", **gla_kw)
    dglow = _mm(dpre, w_gate_f, "nt", name="gate_dx", out_dtype=BF16)
    g_wgate = _mm(proj_c, dpre, "tn", name="gate_dw", a_win=(gl_off, GATE_PAD))[:GLA_RANK]
    (), (g_bgate,) = _rows(lambda i, d: ((), (jnp.sum(d.astype(F32), axis=0, keepdims=True),)), [_win(dpre)], [], [],
                           [(1, GLA_QK)], name="gate_db", nrow=lp)
    dproj_c = jnp.concatenate([dq_c, dk_c, dv_c, dz_c, dglow], axis=1)
    dhn1 = _mm(dproj_c, w_in_c_f, "nt", name="in_c_dx")
    gw_in_c = _mm(hn1, dproj_c, "tn", name="in_c_dw", out_dtype=BF16)
    p1_c, p1_c_bf = rs_front(dict(
        w_in_c=jnp.stack([gw_in_c[:, WIN_STEP * d:WIN_STEP * d + WIN_COLS] for d in range(N_DEV)])),
        IN_C_LAYOUT, "in_c")

    def norm_bwd(i, h, dhn, dres, w):
        _, vjp = jax.vjp(_rms, h, w)
        dh, dw = vjp(dhn)
        return (jnp.where(rowmask(i), dh + dres, 0.0),), (dw,)

    (dh1,), (g_norm_c,) = _rows(norm_bwd, [_win(h1), _win(dhn1), _win(dh2)], [norm_c_f], [(D_MODEL, F32)],
                                [(1, D_MODEL)], name="norm_c_bwd", nrow=lp)

    dh1_bf = dh1.astype(BF16)
    do_ab = _mm(dh1_bf, w_out_ab_f, "nt", name="out_ab_dx", out_dtype=BF16)
    gw_out_ab = _mm(o_ab, dh1_bf, "tn", name="out_ab_dw", out_dtype=BF16)

    def s5_gate_bwd(i, dob, y, t, zb):
        _, vjp = jax.vjp(s5_gate, y, t, zb)
        dy, dt, dzb = vjp(dob.astype(F32))
        return (dy, dt, dzb), ()

    (dy_a, dt_glu, dzb), _ = _rows(
        s5_gate_bwd, [_win(do_ab, RET_W, S5_W), _win(y_s5), _win(t_glu), _win(proj_ab, zb_off, S5_W)], [],
        [(S5_W, F32), (S5_W, BF16), (S5_W, BF16)], [], name="s5_gate_bwd", nrow=lp)
    dyg2 = _mm(dt_glu, w_glu_f, "nt", name="s5_glu_dx")
    gw_glu = _mm(yg_bf, dt_glu, "tn", name="s5_glu_dw", out_dtype=BF16)

    def s5_y_bwd(i, dya, dyg, y, u, d):
        _, vjp = jax.vjp(_gelu, y)
        (dy_g,) = vjp(dyg)
        dy = dya + dy_g
        return (dy, d * dy), (jnp.sum(dy * u, axis=0, keepdims=True),)

    (dy_s5, du1), (g_d,) = _rows(
        s5_y_bwd, [_win(dy_a), _win(dyg2), _win(y_s5), _win(proj_ab, u_off, S5_W)], [s5_d],
        [(S5_W, BF16), (S5_W, F32)], [(1, S5_W)], name="s5_y_bwd", nrow=lp)
    gx = _mm_core(dy_s5, wc_t, dims=NT, grid=(lp // tm5, nt5, 1), name="s5_cx_dx",
                  a_spec=pl.BlockSpec((tm5, 128), lambda i, j, k: (i, j % gt)), b_spec=wc_j,
                  o_spec=wide, out_shape=wide_shape, acc_shape=(tm5, S5_LANES))
    rows_k = lambda col: pl.BlockSpec((tk5, col), lambda i, j, k: (k, i))
    gwc = _mm_core(xs5_2d, dy_s5, dims=TN, grid=(nt5, 1, lp // tk5), name="s5_cx_dw", a_spec=rows_k(S5_LANES),
                   b_spec=pl.BlockSpec((tk5, 128), lambda i, j, k: (k, i % gt)),
                   o_spec=pl.BlockSpec((None, S5_LANES, 128), lambda i, j, k: (i, 0, 0)),
                   out_shape=jax.ShapeDtypeStruct((nt5, S5_LANES, 128), F32), acc_shape=(S5_LANES, 128))
    g_s5, da = _s5_scan_bwd(gx.reshape(lp, 2 * S5_SUBL, S5_LANES), xs5, a_re, a_im)
    g_s5_2d = g_s5.reshape(lp, 2 * S5_N)
    du = _mm_core(g_s5_2d, wb_t, dims=NT, grid=(lp // tm5, gt, 2), name="s5_bu_dx", a_spec=wide_k, b_spec=wb_k,
                  o_spec=narrow, out_shape=jax.ShapeDtypeStruct((lp, S5_W), BF16), acc_shape=(tm5, 128),
                  extra=[(du1, narrow)])
    gwb = _mm_core(proj_ab, g_s5_2d, dims=TN, grid=(nt5, 1, lp // tk5), name="s5_bu_dw",
                   a_spec=pl.BlockSpec((tk5, 128), lambda i, j, k: (k, u_blk + i % gt)), b_spec=rows_k(S5_LANES),
                   o_spec=pl.BlockSpec((None, 128, S5_LANES), lambda i, j, k: (i, 0, 0)),
                   out_shape=jax.ShapeDtypeStruct((nt5, 128, S5_LANES), F32), acc_shape=(128, S5_LANES))
    gwc6 = gwc.reshape(2, gt, gt, S5_P, gt, S5_GH)
    g_c = jnp.einsum("rsgpgh->rsghp", gwc6).reshape(2, S5_G, S5_GH, S5_P)
    g_c_re, g_c_im = g_c[0], -g_c[1]
    gwb6 = gwb.reshape(2, gt, gt, S5_GH, gt, S5_P)
    d_bb = jnp.einsum("rsghgp->rsgph", gwb6).reshape(2, S5_G, S5_P * S5_GH)
    d_bb_re, d_bb_im = d_bb[0], d_bb[1]
    g_lam_re, g_lam_im, g_log_dt, g_b_re, g_b_im = _s5_disc_bwd(
        disc_args, (da[0].reshape(S5_G, S5_P), da[1].reshape(S5_G, S5_P), d_bb_re, d_bb_im))

    (dq_a, dk_a, dv_a, dz_a), (g_ret_norm,), (from_chips_c,) = _scan_bwd(
        _ret_chunk, ret_xs, ret_cs, [ret_norm_h], [lg], do_ab, ret_sprev, name="ret_bwd", post=_ret_post,
        hook=_chips_hook(p1_c_bf), **ret_kw)
    dproj_ab = jnp.concatenate([dq_a, dk_a, dv_a, dz_a, du, dzb], axis=1)
    p1_o, p1_o_bf = rs_front(dict(s5_w_glu=gw_glu.reshape(N_DEV, S5_W // N_DEV, S5_W),
                                  w_out_ab=gw_out_ab.reshape(N_DEV, OUT_AB // N_DEV, D_MODEL),
                                  w_out_c=gw_out_c.reshape(N_DEV, GLA_W // N_DEV, D_MODEL)), OTHER_LAYOUT, "other")

    lane = lambda a_: pad_to(a_, a_.shape[0], 128)

    def sum8(i, *blocks):
        acc = blocks[0]
        for b in blocks[1:]:
            acc = acc + b
        return (acc,), ()

    def pack_small(pieces):
        return jnp.concatenate([_rows1024(p) for _, p in pieces], axis=0)

    def sum_small(gathered, pieces, tag):
        srow = gathered.shape[1]
        tr = _tile(srow, 128, 8)
        flat = gathered.reshape(N_DEV * srow, PACK_COLS)
        (total,), _ = _rows(sum8, [_win(flat, roff=-d * (srow // tr)) for d in range(N_DEV)], [], [(PACK_COLS, F32)],
                            [], name="sum_small_" + tag, nrow=srow, tr=tr)
        out, o = {}, 0
        for name_, p in pieces:
            r8 = _rows1024(p).shape[0]
            out[name_] = _unrows1024(total[o:o + r8], *p.shape)
            o += r8
        return out

    early_pieces = [
        ("vec2048", jnp.concatenate([g_final, g_norm_c], axis=0)),
        ("vec1024", jnp.concatenate([g_d, g_bgate, pad_to(loss_acc[:, :1], 1, PACK_COLS)], axis=0)),
        ("lam3", jnp.concatenate([lane(g_lam_re), lane(g_lam_im), lane(g_log_dt)], axis=1)),
        ("s5_b_re", g_b_re), ("s5_b_im", g_b_im),
        ("s5_c_re", g_c_re.reshape(S5_G, S5_GH * S5_P)), ("s5_c_im", g_c_im.reshape(S5_G, S5_GH * S5_P)),
        ("ret_norm_w", g_ret_norm.reshape(RET_H, RET_DV)), ("gla_norm_w", g_gla_norm.reshape(GLA_H, GLA_DV)),
        ("gla_w_gate", g_wgate)]
    gw_in_ab, (from_chips_o, early_all) = _mm(
        hn0, dproj_ab, "tn", name="in_ab_dw", out_dest=True, out_dtype=BF16,
        hook=_merge_hooks(_chips_hook(p1_o_bf), _gather_hook(pack_small(early_pieces))))
    p1_first, p1_first_bf = rs_front(dict(w_in_ab=gw_in_ab), FIRST_LAYOUT, "first")
    dhn0, (from_chips_first,) = _mm(dproj_ab, w_in_ab_g, "nt", name="in_ab_dx", b_dev=True,
                                    hook=_chips_hook(p1_first_bf))
    (dh0,), (g_norm_ab,) = _rows(norm_bwd, [_win(h0), _win(dhn0), _win(dh1)], [norm_ab_w], [(D_MODEL, F32)],
                                 [(1, D_MODEL)], name="norm_ab_bwd", nrow=lp)
    grad_x = dh0[CHUNK:][None]
    late_pieces = [("norm_ab_w", g_norm_ab), ("meta", dh0[PAD:CHUNK])]
    small = sum_small(early_all, early_pieces, "early")
    small.update(sum_small(_all_gather(pack_small(late_pieces), "gather_grads"), late_pieces, "late"))

    big_grads = {**rs_back(p1_c, from_chips_c, IN_C_LAYOUT, "in_c"), **rs_back(p1_o, from_chips_o, OTHER_LAYOUT, "other"),
                 **rs_back(p1_first, from_chips_first, FIRST_LAYOUT, "first")}
    big_grads["w_in_c"] = _lane_select(big_grads["w_in_c"], win_off, -1, 896, F32, True,
                                       "w_in_c_from_window")[:, :SHARD_C]
    small["final_norm_w"], small["norm_c_w"] = small["vec2048"][0:1], small["vec2048"][1:2]
    small["s5_d"], small["gla_b_gate"] = small["vec1024"][0:1], small["vec1024"][1:2]
    loss = small["vec1024"][2, 0]
    small["s5_lam_re"], small["s5_lam_im"] = small["lam3"][:, :S5_P], small["lam3"][:, 128:128 + S5_P]
    small["s5_log_dt"] = small["lam3"][:, 256:257]

    def my_cols(g, n):
        return lax.dynamic_slice_in_dim(g, dev * n, n, axis=g.ndim - 1)

    grads = dict(
        meta=my_cols(small["meta"], D_MODEL // N_DEV),
        norm_ab_w=small["norm_ab_w"], w_in_ab=big_grads["w_in_ab"][None], ret_norm_w=small["ret_norm_w"].reshape(1, RET_W),
        s5_lam_re=small["s5_lam_re"][None], s5_lam_im=small["s5_lam_im"][None],
        s5_log_dt=small["s5_log_dt"].reshape(1, S5_G),
        s5_b_re=small["s5_b_re"].reshape(1, S5_G, S5_P, S5_GH), s5_b_im=small["s5_b_im"].reshape(1, S5_G, S5_P, S5_GH),
        s5_c_re=small["s5_c_re"][None], s5_c_im=small["s5_c_im"][None], s5_d=small["s5_d"],
        s5_w_glu=big_grads["s5_w_glu"][None], w_out_ab=big_grads["w_out_ab"][None],
        norm_c_w=my_cols(small["norm_c_w"], D_MODEL // N_DEV), w_in_c=big_grads["w_in_c"][None],
        gla_w_gate=my_cols(small["gla_w_gate"], GLA_QK // N_DEV)[None],
        gla_b_gate=my_cols(small["gla_b_gate"], GLA_QK // N_DEV),
        gla_norm_w=my_cols(small["gla_norm_w"].reshape(1, GLA_W), GLA_W // N_DEV),
        w_out_c=big_grads["w_out_c"][None], final_norm_w=small["final_norm_w"].reshape(D_MODEL))

    deltas, new_m, new_v = {}, {}, {}
    for k in order:
        w = weights[k]
        d2, m2, v2 = _adamw(_as2d(w), _as2d(grads[k].reshape(w.shape)), _as2d(mom_m[k]), _as2d(mom_v[k]), "adamw_" + k)
        deltas[k], new_m[k], new_v[k] = d2.reshape(w.shape), m2.reshape(w.shape), v2.reshape(w.shape)
        grads[k] = grads[k].reshape(w.shape)

    return (loss, grad_x, *[grads[k] for k in order], *[deltas[k] for k in order],
            *[new_m[k] for k in order], *[new_v[k] for k in order])
```

```python
import functools
import math

import jax
import jax.numpy as jnp
from jax import lax
from jax.experimental import pallas as pl
from jax.experimental.pallas import tpu as pltpu

F32, BF16 = jnp.float32, jnp.bfloat16
MESH = pl.DeviceIdType.MESH
N_DEV = 8

D_MODEL = 2048
CHUNK = 128
N_META = 16
PAD = CHUNK - N_META
SUB = 16
EPS = 1e-6
RET_H, RET_DK, RET_DV = 8, 128, 256
RET_QK, RET_W = RET_H * RET_DK, RET_H * RET_DV
ROPE_BASE = 10000.0
S5_W, S5_G, S5_P, S5_GH = 1024, 64, 64, 16
S5_N = S5_G * S5_P
GLA_H, GLA_DK, GLA_DV, GLA_RANK, GLA_TAU = 4, 256, 512, 16, 16.0
GLA_QK, GLA_W = GLA_H * GLA_DK, GLA_H * GLA_DV
IN_AB = 2 * RET_QK + 2 * RET_W + 2 * S5_W
OUT_AB = RET_W + S5_W
IN_C = 2 * GLA_QK + 2 * GLA_W + GLA_RANK
GATE_PAD = 256
IN_C_PAD = 2 * GLA_QK + 2 * GLA_W + GATE_PAD
ADAM_LR, ADAM_B1, ADAM_B2, ADAM_EPS, ADAM_WD, ADAM_STEP = 0.001, 0.9, 0.999, 1e-08, 0.01, 10

VMEM_LIMIT_BYTES = 48 * 2 ** 20
PACK_COLS = 1024
PACK_ROW_MULT = 8
SHARD_C = IN_C // N_DEV
WIN_STEP = 768
WIN_COLS = 1024


def _params(sem):
    return pltpu.CompilerParams(dimension_semantics=sem, vmem_limit_bytes=VMEM_LIMIT_BYTES)


def _tile(n, cap, mult):
    best = None
    for t in range(mult, min(n, cap) + 1, mult):
        if n % t == 0:
            best = t
    assert best is not None, (n, cap, mult)
    return best


def _dg(a, b, ca, cb):
    return lax.dot_general(a.astype(BF16), b.astype(BF16), (((ca,), (cb,)), ((), ())),
                           preferred_element_type=F32)


@functools.partial(jax.custom_vjp, nondiff_argnums=(2, 3))
def _bdot(a, b, ca, cb):
    return _dg(a, b, ca, cb)


def _bdot_fwd(a, b, ca, cb):
    return _dg(a, b, ca, cb), (a, b)


def _bdot_bwd(ca, cb, res, g):
    a, b = res
    da = _dg(g, b, 1, 1 - cb) if ca == 1 else _dg(b, g, 1 - cb, 1)
    db = _dg(a, g, 1 - ca, 0) if cb == 0 else _dg(g, a, 0, 1 - ca)
    return da.astype(a.dtype), db.astype(b.dtype)


_bdot.defvjp(_bdot_fwd, _bdot_bwd)


def _sigmoid(x):
    return 1.0 / (1.0 + jnp.exp(-x))


def _silu(x):
    return x * _sigmoid(x)


def _log_sigmoid(x):
    return jnp.minimum(x, 0.0) - jnp.log(1.0 + jnp.exp(-jnp.abs(x)))


def _gelu(x):
    return 0.5 * x * (1.0 + jnp.tanh(math.sqrt(2.0 / math.pi) * (x + 0.044715 * (x * x * x))))


def _rms(x, w):
    return x * lax.rsqrt(jnp.mean(x * x, axis=-1, keepdims=True) + EPS) * w


class _Hook:
    def __init__(self, ins, outs, sems, phases):
        self.ins, self.outs, self.sems, self.phases = list(ins), list(outs), list(sems), list(phases)


def _merge_hooks(first, second):
    ni, no, ns = len(first.ins), len(first.outs), len(first.sems)
    phases = [(f, lambda i, o, s, fn=fn: fn(i[:ni], o[:no], s[:ns])) for f, fn in first.phases]
    phases += [(f, lambda i, o, s, fn=fn: fn(i[ni:], o[no:], s[ns:])) for f, fn in second.phases]
    return _Hook(first.ins + second.ins, first.outs + second.outs, first.sems + second.sems,
                 sorted(phases, key=lambda p: p[0]))


_NO_HOOK = _Hook([], [], [], [])
_ANY = pl.BlockSpec(memory_space=pl.ANY)


def _run_hook(hook, lin, total, in_refs, out_refs, sem_refs):
    for frac, fn in hook.phases:
        at = min(int(frac * total), total - 1)

        @pl.when(lin == at)
        def _(fn=fn):
            fn(in_refs, out_refs, sem_refs)


def _mm_core(a, b, *, dims, grid, a_spec, b_spec, o_spec, out_shape, acc_shape, name, extra=(), hook=None,
             b_parts=0):
    nk = grid[2]
    n_extra = len(extra)
    hook = _NO_HOOK if hook is None else hook
    hi, ho = len(hook.ins), len(hook.outs)

    def body(*refs):
        a_ref, b_ref = refs[0], refs[1]
        o_ref, acc = refs[2 + n_extra + hi], refs[3 + n_extra + hi + ho]
        k = pl.program_id(2)
        lin = (pl.program_id(0) * grid[1] + pl.program_id(1)) * nk + k
        _run_hook(hook, lin, grid[0] * grid[1] * nk, refs[2 + n_extra:2 + n_extra + hi],
                  refs[3 + n_extra + hi:3 + n_extra + hi + ho], refs[4 + n_extra + hi + ho:])

        if b_parts:
            part = sum(lax.dot_general(a_ref[:, d * PACK_COLS:(d + 1) * PACK_COLS].astype(BF16), b_ref[d].astype(BF16),
                                       dims, preferred_element_type=F32) for d in range(b_parts))
        else:
            part = lax.dot_general(a_ref[...].astype(BF16), b_ref[...].astype(BF16), dims, preferred_element_type=F32)

        def finish(r):
            for e in range(n_extra):
                r = r + refs[2 + e][...].astype(F32)
            o_ref[...] = r.astype(o_ref.dtype)

        if nk == 1:
            finish(part)
        else:
            @pl.when(k == 0)
            def _():
                acc[...] = part

            @pl.when(k > 0)
            def _():
                acc[...] += part

            @pl.when(k == nk - 1)
            def _():
                finish(acc[...])

    res = pl.pallas_call(
        body, name=name, grid=grid,
        in_specs=[a_spec, b_spec] + [sp for _, sp in extra] + [_ANY] * hi,
        out_specs=[o_spec] + [_ANY] * ho, out_shape=[out_shape] + hook.outs,
        scratch_shapes=[pltpu.VMEM(acc_shape if nk > 1 else (8, 128), F32)] + hook.sems,
        compiler_params=_params(("arbitrary", "arbitrary", "arbitrary")),
    )(a, b, *[arr for arr, _ in extra], *hook.ins)
    return res[0] if hook is _NO_HOOK else (res[0], res[1:])


NN, NT, TN = (((1,), (0,)), ((), ())), (((1,), (1,)), ((), ())), (((0,), (0,)), ((), ()))


FULL_K = 2048


def _mm(a, b, mode, *, name, out_dtype=F32, a_win=None, add=None, bias=None, hook=None, b_dev=False,
        out_dest=False):
    b_parts = 0
    if mode == "tn":
        kdim, n = a.shape[0], b.shape[1]
        m = a.shape[1] if a_win is None else a_win[1]
        tm, tn, tk = _tile(m, 512, 128), _tile(n, 640, 128), kdim
        off = 0 if a_win is None else a_win[0] // tm
        a_spec = pl.BlockSpec((tk, tm), lambda i, j, k: (k, i + off))
        b_spec = pl.BlockSpec((tk, tn), lambda i, j, k: (k, j))
        dims = TN
    else:
        m = a.shape[0]
        kdim = a.shape[1] if a_win is None else a_win[1]
        if b_dev:
            n = b.shape[0] * b.shape[2] if mode == "nn" else b.shape[1]
        else:
            n = b.shape[1] if mode == "nn" else b.shape[0]
        tm = _tile(m, 1408 if kdim <= FULL_K else 352, 8)
        tn, tk = _tile(n, 640, 128), kdim
        off = 0 if a_win is None else a_win[0] // tk
        a_spec = pl.BlockSpec((tm, tk), lambda i, j, k: (i, k + off))
        if mode == "nn":
            dims = NN
            if b_dev:
                per = PACK_COLS // tn
                b_spec = pl.BlockSpec((None, tk, tn), lambda i, j, k: (j // per, k, j % per))
            else:
                b_spec = pl.BlockSpec((tk, tn), lambda i, j, k: (k, j))
        else:
            dims = NT
            if b_dev:
                b_parts = kdim // PACK_COLS
                b_spec = pl.BlockSpec((b_parts, tn, PACK_COLS), lambda i, j, k: (0, j, 0))
            else:
                b_spec = pl.BlockSpec((tn, tk), lambda i, j, k: (j, k))
    if a_win is not None:
        assert a_win[0] % (tm if mode == "tn" else tk) == 0
    extra = []
    if add is not None:
        extra.append((add, pl.BlockSpec((tm, tn), lambda i, j, k: (i, j))))
    if bias is not None:
        extra.append((bias, pl.BlockSpec((1, tn), lambda i, j, k: (0, j))))
    if out_dest:
        per = PACK_COLS // tn
        o_spec = pl.BlockSpec((None, tm, tn), lambda i, j, k: (j // per, i, j % per))
        out_shape = jax.ShapeDtypeStruct((n // PACK_COLS, m, PACK_COLS), out_dtype)
    else:
        o_spec = pl.BlockSpec((tm, tn), lambda i, j, k: (i, j))
        out_shape = jax.ShapeDtypeStruct((m, n), out_dtype)
    return _mm_core(a, b, dims=dims, grid=(m // tm, n // tn, kdim // tk), a_spec=a_spec, b_spec=b_spec,
                    o_spec=o_spec, out_shape=out_shape, acc_shape=(tm, tn), name=name, extra=extra, hook=hook,
                    b_parts=b_parts)


def _win(arr, col0=0, width=None, roff=0):
    return (arr, col0, arr.shape[1] if width is None else width, roff)


def _rows(fn, rows, consts, outs, accs, *, name, nrow, tr=CHUNK):
    nr, nc, no = len(rows), len(consts), len(outs)

    def body(*refs):
        i = pl.program_id(0)
        ins = [r[...] for r in refs[:nr + nc]]
        o_refs = refs[nr + nc:nr + nc + no]
        a_refs = refs[nr + nc + no:]
        res_o, res_a = fn(i, *ins)
        for r, v in zip(o_refs, res_o):
            r[...] = v.astype(r.dtype)
        if a_refs:
            @pl.when(i == 0)
            def _():
                for r in a_refs:
                    r[...] = jnp.zeros_like(r)

            for r, v in zip(a_refs, res_a):
                r[...] += v

    in_specs = []
    for (arr, col0, width, roff) in rows:
        assert col0 % width == 0 and arr.shape[0] % tr == 0
        in_specs.append(pl.BlockSpec((tr, width), lambda i, c=col0 // width, ro=roff: (jnp.maximum(i - ro, 0), c)))
    for c in consts:
        in_specs.append(pl.BlockSpec(c.shape, lambda i, nd=c.ndim: (0,) * nd))
    out_specs = [pl.BlockSpec((tr, w), lambda i: (i, 0)) for (w, _) in outs]
    out_specs += [pl.BlockSpec(s, lambda i, nd=len(s): (0,) * nd) for s in accs]
    out_shape = [jax.ShapeDtypeStruct((nrow, w), dt) for (w, dt) in outs]
    out_shape += [jax.ShapeDtypeStruct(s, F32) for s in accs]
    res = pl.pallas_call(
        body, name=name, grid=(nrow // tr,), in_specs=in_specs, out_specs=out_specs, out_shape=out_shape,
        compiler_params=_params(("arbitrary",)),
    )(*[r[0] for r in rows], *consts)
    return res[:no], res[no:]


HEADS_PER_STEP = 2


def _scan_specs(xs, cs, ws, ks, chunk_of, hpb):
    specs = []
    for (arr, width, colfn) in xs:
        specs.append(pl.BlockSpec((CHUNK, width * hpb), lambda h, n, f=colfn: (chunk_of(n), f(h * hpb) // hpb)))
    for (arr, width, colfn) in cs:
        specs.append(pl.BlockSpec((CHUNK, width), lambda h, n, f=colfn: (chunk_of(n), f(h))))
    for arr in list(ws) + list(ks):
        specs.append(pl.BlockSpec((hpb, 1, arr.shape[2]), lambda h, n: (h, 0, 0)))
    return specs


def _scan_fwd(fn, xs, cs, ws, ks, *, heads, nchunk, s_shape, out_w, name, pre=None, hook=None,
              hpb=HEADS_PER_STEP):
    nx, ncs, nw = len(xs), len(cs), len(ws)
    hook = _NO_HOOK if hook is None else hook
    hi, ho = len(hook.ins), len(hook.outs)
    hblocks = heads // hpb

    def body(*refs):
        n = pl.program_id(1)
        nin = nx + ncs + nw + len(ks)
        y_ref, sp_ref = refs[nin + hi], refs[nin + hi + 1]
        s_scr = refs[nin + hi + 2 + ho]
        _run_hook(hook, pl.program_id(0) * nchunk + n, hblocks * nchunk, refs[nin:nin + hi],
                  refs[nin + hi + 2:nin + hi + 2 + ho], refs[nin + hi + 3 + ho:])

        @pl.when(n == 0)
        def _():
            s_scr[...] = jnp.zeros_like(s_scr)

        cv = [r[...] for r in refs[nx:nx + ncs]]
        for e in range(hpb):
            state = s_scr[e]
            sp_ref[e, 0] = state
            xv = [r[:, e * w:(e + 1) * w] for r, (_, w, _) in zip(refs[:nx], xs)]
            wv = [r[e] for r in refs[nx + ncs:nx + ncs + nw]]
            kv = [r[e] for r in refs[nx + ncs + nw:nin]]
            if pre is not None:
                xv = pre(xv, cv)
            y, s_new = fn(n, xv, state, cv, wv, kv)
            y_ref[:, e * out_w:(e + 1) * out_w] = y.astype(y_ref.dtype)
            s_scr[e] = s_new

    lp = nchunk * CHUNK
    res = pl.pallas_call(
        body, name=name, grid=(hblocks, nchunk),
        in_specs=_scan_specs(xs, cs, ws, ks, lambda n: n, hpb) + [_ANY] * hi,
        out_specs=[pl.BlockSpec((CHUNK, out_w * hpb), lambda h, n: (n, h)),
                   pl.BlockSpec((hpb, 1) + s_shape, lambda h, n: (h, n, 0, 0))] + [_ANY] * ho,
        out_shape=[jax.ShapeDtypeStruct((lp, heads * out_w), BF16),
                   jax.ShapeDtypeStruct((heads, nchunk) + s_shape, F32)] + hook.outs,
        scratch_shapes=[pltpu.VMEM((hpb,) + s_shape, F32)] + hook.sems,
        compiler_params=_params(("arbitrary", "arbitrary")),
    )(*[t[0] for t in xs], *[t[0] for t in cs], *ws, *ks, *hook.ins)
    return (res[0], res[1]) if hook is _NO_HOOK else (res[0], res[1], res[2:])


def _scan_bwd(fn, xs, cs, ws, ks, dy, sprev, *, heads, nchunk, s_shape, out_w, name, pre=None, post=None,
              hook=None, hpb=HEADS_PER_STEP):
    nx, ncs, nw = len(xs), len(cs), len(ws)
    nin = nx + ncs + nw + len(ks)
    hook = _NO_HOOK if hook is None else hook
    hi, ho = len(hook.ins), len(hook.outs)
    hblocks = heads // hpb

    def body(*refs):
        step = pl.program_id(1)
        n = nchunk - 1 - step
        dy_ref, sp_ref = refs[nin], refs[nin + 1]
        o0 = nin + 2 + hi
        dx_refs = refs[o0:o0 + nx]
        dw_refs = refs[o0 + nx:o0 + nx + nw]
        ds_scr = refs[o0 + nx + nw + ho]
        _run_hook(hook, pl.program_id(0) * nchunk + step, hblocks * nchunk, refs[nin + 2:o0],
                  refs[o0 + nx + nw:o0 + nx + nw + ho], refs[o0 + nx + nw + ho + 1:])

        @pl.when(step == 0)
        def _():
            ds_scr[...] = jnp.zeros_like(ds_scr)
            for r in dw_refs:
                r[...] = jnp.zeros_like(r)

        cv = [r[...] for r in refs[nx:nx + ncs]]
        for e in range(hpb):
            xv = [r[:, e * w:(e + 1) * w] for r, (_, w, _) in zip(refs[:nx], xs)]
            wv = [r[e] for r in refs[nx + ncs:nx + ncs + nw]]
            kv = [r[e] for r in refs[nx + ncs + nw:nin]]
            if pre is not None:
                xv = pre(xv, cv)
            _, vjp = jax.vjp(lambda xs_, s_, ws_, kv=kv: fn(n, xs_, s_, cv, ws_, kv), xv, sp_ref[e, 0], wv)
            dxs, ds_prev, dws = vjp((dy_ref[:, e * out_w:(e + 1) * out_w].astype(F32), ds_scr[e]))
            if post is not None:
                dxs = post(dxs, cv)
            for r, v, (_, w, _) in zip(dx_refs, dxs, xs):
                r[:, e * w:(e + 1) * w] = v.astype(r.dtype)
            for r, v in zip(dw_refs, dws):
                r[e] += v
            ds_scr[e] = ds_prev

    lp = nchunk * CHUNK
    rev = lambda n: nchunk - 1 - n
    in_specs = _scan_specs(xs, cs, ws, ks, rev, hpb)
    in_specs.append(pl.BlockSpec((CHUNK, out_w * hpb), lambda h, n: (rev(n), h)))
    in_specs.append(pl.BlockSpec((hpb, 1) + s_shape, lambda h, n: (h, rev(n), 0, 0)))
    out_specs = [pl.BlockSpec((CHUNK, w * hpb), lambda h, n: (rev(n), h)) for (_, w, _) in xs]
    out_specs += [pl.BlockSpec((hpb, 1, w.shape[2]), lambda h, n: (h, 0, 0)) for w in ws]
    out_shape = [jax.ShapeDtypeStruct((lp, heads * w), BF16) for (_, w, _) in xs]
    out_shape += [jax.ShapeDtypeStruct(w.shape, F32) for w in ws]
    res = pl.pallas_call(
        body, name=name, grid=(hblocks, nchunk), in_specs=in_specs + [_ANY] * hi,
        out_specs=out_specs + [_ANY] * ho, out_shape=out_shape + hook.outs,
        scratch_shapes=[pltpu.VMEM((hpb,) + s_shape, F32)] + hook.sems,
        compiler_params=_params(("arbitrary", "arbitrary")),
    )(*[t[0] for t in xs], *[t[0] for t in cs], *ws, *ks, dy, sprev, *hook.ins)
    if hook is _NO_HOOK:
        return res[:nx], res[nx:]
    return res[:nx], res[nx:nx + nw], res[nx + nw:]


def _iota2(shape, dim):
    return lax.broadcasted_iota(jnp.int32, shape, dim)


def _ret_chunk(n, xs, state, cs, ws, ks):
    q, k, v, z = xs
    (w,), (lg,) = ws, ks
    lgc = lg[:, :1]
    row, col = _iota2((CHUNK, CHUNK), 0), _iota2((CHUNK, CHUNK), 1)
    diff = jnp.maximum(row - col, 0).astype(F32)
    decay = jnp.where(row >= col, jnp.exp(lg * diff), 0.0)
    scores = _bdot(q, k, 1, 1) * decay
    o_intra = _bdot(scores, v, 1, 0)
    idx = _iota2((CHUNK, 1), 0).astype(F32)
    k_w = k * jnp.exp(lgc * (CHUNK - 1.0 - idx))
    kv = _bdot(k_w, v, 0, 0)
    s_new = state * jnp.exp(lgc * float(CHUNK)) + kv
    q_w = q * jnp.exp(lgc * (idx + 1.0))
    o = o_intra + _bdot(q_w, state, 1, 0)
    return _rms(o, w) * _silu(z), s_new


def _rope(t, cos2, sin2):
    return t * cos2 + pltpu.roll(t, RET_DK // 2, 1) * sin2


def _rope_t(g, cos2, sin2):
    return g * cos2 - pltpu.roll(g, RET_DK // 2, 1) * sin2


def _ret_pre(xv, cv):
    q, k, v, z = xv
    cos2, sin2 = cv
    return [_rope(q, cos2, sin2), _rope(k, cos2, sin2) * (RET_DK ** -0.5), v, z]


def _ret_post(dxs, cv):
    dq, dk, dv, dz = dxs
    cos2, sin2 = cv
    return [_rope_t(dq, cos2, sin2), _rope_t(dk, cos2, sin2) * (RET_DK ** -0.5), dv, dz]


def _gla_chunk(n, xs, state_t, cs, ws, ks):
    q, k, v, z, pre = xs
    (w,) = ws
    q = q * (GLA_DK ** -0.5)
    rowc = _iota2((CHUNK, 1), 0)
    valid = jnp.logical_or(n > 0, rowc >= PAD)
    log_a = jnp.where(valid, _log_sigmoid(pre) / GLA_TAU, 0.0)
    row, col = _iota2((CHUNK, CHUNK), 0), _iota2((CHUNK, CHUNK), 1)
    tri = (row >= col).astype(F32)
    b = jnp.dot(tri, log_a, precision=lax.Precision.HIGHEST, preferred_element_type=F32)
    b_last = b[CHUNK - 1:CHUNK, :]
    kv_t = _bdot(v, k * jnp.exp(b_last - b), 0, 0)
    s_new = state_t * jnp.exp(b_last) + kv_t
    o_inter = _bdot(q * jnp.exp(b), state_t, 1, 1)
    outs = []
    for s in range(CHUNK // SUB):
        lo, hi = s * SUB, (s + 1) * SUB
        b_ref = jnp.zeros_like(b_last) if s == 0 else b[lo - 1:lo, :]
        q_hat = q[lo:hi] * jnp.exp(b[lo:hi] - b_ref)
        k_hat = k[:hi] * jnp.exp(b_ref - b[:hi])
        sc = _bdot(q_hat, k_hat, 1, 1)
        causal = _iota2((SUB, hi), 0) + lo >= _iota2((SUB, hi), 1)
        outs.append(_bdot(jnp.where(causal, sc, 0.0), v[:hi], 1, 0))
    o = jnp.concatenate(outs, axis=0) + o_inter
    return _rms(o, w) * _silu(z), s_new


def _s5_disc(lam_re, lam_im, log_dt, b_re, b_im, expand):
    dt = jnp.exp(log_dt)
    mag = jnp.exp(lam_re * dt)
    ab_re, ab_im = mag * jnp.cos(lam_im * dt), mag * jnp.sin(lam_im * dt)
    den = lam_re * lam_re + lam_im * lam_im
    nr, ni = ab_re - 1.0, ab_im
    f_re = (nr * lam_re + ni * lam_im) / den
    f_im = (ni * lam_re - nr * lam_im) / den
    hp = lax.Precision.HIGHEST
    f_re = jnp.dot(f_re, expand, precision=hp, preferred_element_type=F32)
    f_im = jnp.dot(f_im, expand, precision=hp, preferred_element_type=F32)
    return ab_re, ab_im, f_re * b_re - f_im * b_im, f_re * b_im + f_im * b_re


def _s5_disc_fwd(args):
    def body(*refs):
        outs = _s5_disc(*[r[...] for r in refs[:6]])
        for r, v in zip(refs[6:], outs):
            r[...] = v

    g, p = args[0].shape
    return pl.pallas_call(
        body, name="s5_disc_fwd",
        out_shape=[jax.ShapeDtypeStruct((g, p), F32)] * 2 + [jax.ShapeDtypeStruct(args[3].shape, F32)] * 2,
    )(*args)


def _s5_disc_bwd(args, cts):
    def body(*refs):
        prim = [r[...] for r in refs[:5]]
        expand = refs[5][...]
        ct = tuple(r[...] for r in refs[6:10])
        _, vjp = jax.vjp(lambda *a: _s5_disc(*a, expand), *prim)
        for r, v in zip(refs[10:], vjp(ct)):
            r[...] = v

    return pl.pallas_call(
        body, name="s5_disc_bwd", out_shape=[jax.ShapeDtypeStruct(a.shape, F32) for a in args[:5]],
    )(*args, *cts)


S5_SUBL = 8
S5_LANES = S5_N // S5_SUBL
S5_TB = 64


def _s5_scan_fwd(bu, a_re, a_im):
    lp = bu.shape[0]

    def body(bu_ref, ar_ref, ai_ref, x_ref, st):
        @pl.when(pl.program_id(0) == 0)
        def _():
            st[...] = jnp.zeros_like(st)

        ar, ai = ar_ref[...], ai_ref[...]

        def step(t, carry):
            xr, xi = carry
            nr = ar * xr - ai * xi + bu_ref[t, 0:S5_SUBL, :]
            ni = ar * xi + ai * xr + bu_ref[t, S5_SUBL:2 * S5_SUBL, :]
            x_ref[t, 0:S5_SUBL, :] = nr
            x_ref[t, S5_SUBL:2 * S5_SUBL, :] = ni
            return nr, ni

        xr, xi = lax.fori_loop(0, S5_TB, step, (st[0], st[1]))
        st[0] = xr
        st[1] = xi

    blk = pl.BlockSpec((S5_TB, 2 * S5_SUBL, S5_LANES), lambda i: (i, 0, 0))
    cst = pl.BlockSpec((S5_SUBL, S5_LANES), lambda i: (0, 0))
    return pl.pallas_call(
        body, name="s5_scan_fwd", grid=(lp // S5_TB,), in_specs=[blk, cst, cst], out_specs=blk,
        out_shape=jax.ShapeDtypeStruct(bu.shape, F32),
        scratch_shapes=[pltpu.VMEM((2, S5_SUBL, S5_LANES), F32)],
        compiler_params=_params(("arbitrary",)),
    )(bu, a_re, a_im)


def _s5_scan_bwd(gx, x, a_re, a_im):
    lp = gx.shape[0]
    nb = lp // S5_TB

    def body(gx_ref, x_ref, xp_ref, ar_ref, ai_ref, g_ref, da_ref, st):
        i = pl.program_id(0)

        @pl.when(i == 0)
        def _():
            st[...] = jnp.zeros_like(st)
            da_ref[...] = jnp.zeros_like(da_ref)

        ar, ai = ar_ref[...], ai_ref[...]
        first = (i == nb - 1).astype(F32)

        def step(s, carry):
            gr, gi, dar, dai = carry
            t = S5_TB - 1 - s
            ngr = gx_ref[t, 0:S5_SUBL, :] + ar * gr + ai * gi
            ngi = gx_ref[t, S5_SUBL:2 * S5_SUBL, :] + ar * gi - ai * gr
            g_ref[t, 0:S5_SUBL, :] = ngr
            g_ref[t, S5_SUBL:2 * S5_SUBL, :] = ngi
            tp = jnp.maximum(t - 1, 0)
            at0 = (t == 0).astype(F32)
            keep = 1.0 - at0
            pr = keep * x_ref[tp, 0:S5_SUBL, :] + at0 * (1.0 - first) * xp_ref[0, 0:S5_SUBL, :]
            pi = keep * x_ref[tp, S5_SUBL:2 * S5_SUBL, :] + at0 * (1.0 - first) * xp_ref[0, S5_SUBL:2 * S5_SUBL, :]
            return ngr, ngi, dar + ngr * pr + ngi * pi, dai + ngi * pr - ngr * pi

        zero = jnp.zeros((S5_SUBL, S5_LANES), F32)
        gr, gi, dar, dai = lax.fori_loop(0, S5_TB, step, (st[0], st[1], zero, zero))
        st[0] = gr
        st[1] = gi
        da_ref[0] += dar
        da_ref[1] += dai

    rev = lambda i: nb - 1 - i
    blk = pl.BlockSpec((S5_TB, 2 * S5_SUBL, S5_LANES), lambda i: (rev(i), 0, 0))
    prev = pl.BlockSpec((1, 2 * S5_SUBL, S5_LANES), lambda i: (jnp.maximum(rev(i) * S5_TB - 1, 0), 0, 0))
    cst = pl.BlockSpec((S5_SUBL, S5_LANES), lambda i: (0, 0))
    return pl.pallas_call(
        body, name="s5_scan_bwd", grid=(nb,), in_specs=[blk, blk, prev, cst, cst],
        out_specs=[blk, pl.BlockSpec((2, S5_SUBL, S5_LANES), lambda i: (0, 0, 0))],
        out_shape=[jax.ShapeDtypeStruct(gx.shape, F32), jax.ShapeDtypeStruct((2, S5_SUBL, S5_LANES), F32)],
        scratch_shapes=[pltpu.VMEM((2, S5_SUBL, S5_LANES), F32)],
        compiler_params=_params(("arbitrary",)),
    )(gx, x, x, a_re, a_im)


def _place():
    x, y, c = lax.axis_index("x"), lax.axis_index("y"), lax.axis_index("c")
    return x, y, c, [(1 - x, y), (x, 1 - y), (1 - x, 1 - y)]


def _gather_phases():
    def plan(x_ref, out_ref, send_sems, recv_sems, local_sem):
        x, y, c, chips = _place()
        me, sibling = (x, y, c), (x, y, 1 - c)

        def rows(px, py, pc):
            return out_ref.at[4 * px + 2 * py + pc]

        def copy(k, block, to, src=None):
            return pltpu.make_async_remote_copy(
                src_ref=rows(*block) if src is None else src, dst_ref=rows(*block),
                send_sem=send_sems.at[k], recv_sem=recv_sems.at[k], device_id=to, device_id_type=MESH)

        mine = pltpu.make_async_copy(x_ref, rows(*me), local_sem)
        first = [copy(0, me, sibling, src=x_ref)]
        first += [copy(1 + j, me, (*chip, c), src=x_ref) for j, chip in enumerate(chips)]
        passed = [copy(4 + j, (*chip, c), sibling) for j, chip in enumerate(chips)]
        return c, chips, me, sibling, copy, mine, first, passed

    def start(ins, outs, sems):
        _, _, _, _, _, mine, first, _ = plan(ins[0], outs[0], *sems)
        mine.start()
        for cp in first:
            cp.start()

    def middle(ins, outs, sems):
        c, chips, me, _, copy, _, _, passed = plan(ins[0], outs[0], *sems)
        for j, chip in enumerate(chips):
            copy(1 + j, (*chip, c), me).wait_recv()
            passed[j].start()

    def finish(ins, outs, sems):
        c, chips, me, sibling, copy, mine, first, passed = plan(ins[0], outs[0], *sems)
        copy(0, sibling, me).wait_recv()
        for j, chip in enumerate(chips):
            copy(4 + j, (*chip, 1 - c), me).wait_recv()
        for cp in first + passed:
            cp.wait_send()
        mine.wait()

    return start, middle, finish


_GATHER_SEMS = [pltpu.SemaphoreType.DMA((7,)), pltpu.SemaphoreType.DMA((7,)), pltpu.SemaphoreType.DMA]


def _all_gather(shard, name):
    phases = _gather_phases()

    def body(x_ref, out_ref, *sems):
        for phase in phases:
            phase([x_ref], [out_ref], sems)

    return pl.pallas_call(
        body, name=name, out_shape=jax.ShapeDtypeStruct((N_DEV,) + shard.shape, shard.dtype),
        in_specs=[_ANY], out_specs=_ANY, scratch_shapes=list(_GATHER_SEMS),
    )(shard)


def _gather_hook(shard):
    start, middle, finish = _gather_phases()
    return _Hook([shard], [jax.ShapeDtypeStruct((N_DEV,) + shard.shape, shard.dtype)], _GATHER_SEMS,
                 [(0.0, start), (0.85, middle), (1.0, finish)])


def _swap_with_sibling(parts, name):
    def body(p_ref, out_ref, send_sems, recv_sems):
        x, y, c, _ = _place()
        copies = [pltpu.make_async_remote_copy(
            src_ref=p_ref.at[2 * chip + (1 - c)], dst_ref=out_ref.at[chip],
            send_sem=send_sems.at[chip], recv_sem=recv_sems.at[chip],
            device_id=(x, y, 1 - c), device_id_type=MESH) for chip in range(4)]
        for cp in copies:
            cp.start()
        for cp in copies:
            cp.wait()

    return pl.pallas_call(
        body, name=name, out_shape=jax.ShapeDtypeStruct((4,) + parts.shape[1:], parts.dtype),
        in_specs=[pl.BlockSpec(memory_space=pl.ANY)], out_specs=pl.BlockSpec(memory_space=pl.ANY),
        scratch_shapes=[pltpu.SemaphoreType.DMA((4,)), pltpu.SemaphoreType.DMA((4,))],
    )(parts)


def _chips_phases(lo, rows):
    def copies(p_ref, out_ref, send_sems, recv_sems):
        x, y, c, chips = _place()
        return [pltpu.make_async_remote_copy(
            src_ref=p_ref.at[2 * px + py, pl.ds(lo, rows)], dst_ref=out_ref.at[j],
            send_sem=send_sems.at[j], recv_sem=recv_sems.at[j],
            device_id=(px, py, c), device_id_type=MESH) for j, (px, py) in enumerate(chips)]

    def start(ins, outs, sems):
        for cp in copies(ins[0], outs[0], *sems):
            cp.start()

    def finish(ins, outs, sems):
        for cp in copies(ins[0], outs[0], *sems):
            cp.wait()

    return start, finish


def _chips_hook(parts, lo=0, hi=None):
    rows = (parts.shape[1] if hi is None else hi) - lo
    start, finish = _chips_phases(lo, rows)
    return _Hook([parts], [jax.ShapeDtypeStruct((3, rows) + parts.shape[2:], parts.dtype)],
                 [pltpu.SemaphoreType.DMA((3,)), pltpu.SemaphoreType.DMA((3,))], [(0.0, start), (1.0, finish)])


def _pack_rows(n_elem, row_mult=PACK_ROW_MULT):
    rows = -(-n_elem // PACK_COLS)
    return -(-rows // row_mult) * row_mult


def _pack(flats, dtype, row_mult=PACK_ROW_MULT):
    flat = jnp.concatenate([f.reshape(-1).astype(dtype) for f in flats])
    rows = _pack_rows(flat.shape[0], row_mult)
    return jnp.pad(flat, (0, rows * PACK_COLS - flat.shape[0])).reshape(rows, PACK_COLS)


def _unpack(buf, shapes):
    lead = buf.shape[:-2]
    flat = buf.reshape(lead + (-1,))
    outs, o = [], 0
    for s in shapes:
        n = math.prod(s)
        outs.append(flat[..., o:o + n].reshape(lead + tuple(s)))
        o += n
    return outs


BIG_LAYOUT = (("w_in_ab", D_MODEL, PACK_COLS), ("s5_w_glu", S5_W // N_DEV, PACK_COLS),
              ("w_out_ab", OUT_AB // N_DEV, 2 * PACK_COLS), ("w_in_c", D_MODEL, PACK_COLS),
              ("w_out_c", GLA_W // N_DEV, 2 * PACK_COLS))


def _to_rows(a):
    if a.shape[-1] == PACK_COLS:
        return a
    assert a.shape[-1] == 2 * PACK_COLS
    return jnp.concatenate([a[..., :PACK_COLS], a[..., PACK_COLS:]], axis=-2)


def _from_rows(p, cols):
    if cols == PACK_COLS:
        return p
    r = p.shape[-2] // 2
    return jnp.concatenate([p[..., :r, :], p[..., r:, :]], axis=-1)


FIRST_LAYOUT = BIG_LAYOUT[:1]
OTHER_LAYOUT = BIG_LAYOUT[1:3] + BIG_LAYOUT[4:]
GLU_AB_LAYOUT = BIG_LAYOUT[1:3]
IN_C_LAYOUT = BIG_LAYOUT[3:4]


def _pack_big(pieces, layout):
    return jnp.concatenate([_to_rows(pieces[name]) for name, _, _ in layout], axis=-2)


def _unpack_big(buf, layout):
    out, o = {}, 0
    for name, rows, cols in layout:
        r = rows * cols // PACK_COLS
        out[name] = _from_rows(buf[..., o:o + r, :], cols)
        o += r
    return out


def _rows1024(a):
    r, c = a.shape
    if c > PACK_COLS:
        a = jnp.concatenate([a[:, i * PACK_COLS:(i + 1) * PACK_COLS] for i in range(c // PACK_COLS)], axis=0)
    elif c < PACK_COLS:
        a = jnp.pad(a, ((0, 0), (0, PACK_COLS - c)))
    return jnp.pad(a, ((0, -a.shape[0] % 8), (0, 0)))


def _unrows1024(p, r, c):
    if c > PACK_COLS:
        return jnp.concatenate([p[i * r:(i + 1) * r] for i in range(c // PACK_COLS)], axis=1)
    return p[:r, :c]


def _lane_select(a, off, sign, n_out, out_dtype, exact, name):
    rows, n_in = a.shape
    tr = _tile(rows, 256, 16)

    def body(off_ref, a_ref, o_ref):
        sel = _iota2((n_in, n_out), 0) + off_ref[0] * sign == _iota2((n_in, n_out), 1)
        if exact:
            r = jnp.dot(a_ref[...], sel.astype(F32), precision=lax.Precision.HIGHEST, preferred_element_type=F32)
        else:
            r = _dg(a_ref[...], sel.astype(BF16), 1, 0)
        o_ref[...] = r.astype(out_dtype)

    return pl.pallas_call(
        body, name=name, grid=(rows // tr,),
        in_specs=[pl.BlockSpec(memory_space=pltpu.SMEM), pl.BlockSpec((tr, n_in), lambda i: (i, 0))],
        out_specs=pl.BlockSpec((tr, n_out), lambda i: (i, 0)),
        out_shape=jax.ShapeDtypeStruct((rows, n_out), out_dtype),
        compiler_params=_params(("arbitrary",)),
    )(off, a)


def _adamw(w, g, m, v, name):
    rows, cols = w.shape
    tr = _tile(rows, 256, 8) if rows % 8 == 0 else rows

    def fn(i, w_, g_, m_, v_):
        m_new = ADAM_B1 * m_ + (1.0 - ADAM_B1) * g_
        v_new = ADAM_B2 * v_ + (1.0 - ADAM_B2) * (g_ * g_)
        m_hat = m_new / (1.0 - ADAM_B1 ** ADAM_STEP)
        v_hat = v_new / (1.0 - ADAM_B2 ** ADAM_STEP)
        delta = -ADAM_LR * (m_hat / (jnp.sqrt(v_hat) + ADAM_EPS) + ADAM_WD * w_)
        return (delta, m_new, v_new), ()

    outs, _ = _rows(fn, [_win(w), _win(g), _win(m), _win(v)], [], [(cols, F32)] * 3, [], name=name,
                    nrow=rows, tr=tr)
    return outs


def _as2d(a):
    if a.ndim == 1:
        return a.reshape(1, -1)
    if a.ndim == 2:
        return a
    a = a.reshape(a.shape[1:])
    return a if a.ndim == 2 else a.reshape(a.shape[0], -1)


def kernel(x, meta, norm_ab_w, w_in_ab, ret_norm_w, s5_lam_re, s5_lam_im, s5_log_dt, s5_b_re, s5_b_im, s5_c_re, s5_c_im, s5_d, s5_w_glu, w_out_ab, norm_c_w, w_in_c, gla_w_gate, gla_b_gate, gla_norm_w, w_out_c, final_norm_w, loss_target, m_meta, m_norm_ab_w, m_w_in_ab, m_ret_norm_w, m_s5_lam_re, m_s5_lam_im, m_s5_log_dt, m_s5_b_re, m_s5_b_im, m_s5_c_re, m_s5_c_im, m_s5_d, m_s5_w_glu, m_w_out_ab, m_norm_c_w, m_w_in_c, m_gla_w_gate, m_gla_b_gate, m_gla_norm_w, m_w_out_c, m_final_norm_w, v_meta, v_norm_ab_w, v_w_in_ab, v_ret_norm_w, v_s5_lam_re, v_s5_lam_im, v_s5_log_dt, v_s5_b_re, v_s5_b_im, v_s5_c_re, v_s5_c_im, v_s5_d, v_s5_w_glu, v_w_out_ab, v_norm_c_w, v_w_in_c, v_gla_w_gate, v_gla_b_gate, v_gla_norm_w, v_w_out_c, v_final_norm_w):
    weights = dict(meta=meta, norm_ab_w=norm_ab_w, w_in_ab=w_in_ab, ret_norm_w=ret_norm_w, s5_lam_re=s5_lam_re,
                   s5_lam_im=s5_lam_im, s5_log_dt=s5_log_dt, s5_b_re=s5_b_re, s5_b_im=s5_b_im, s5_c_re=s5_c_re,
                   s5_c_im=s5_c_im, s5_d=s5_d, s5_w_glu=s5_w_glu, w_out_ab=w_out_ab, norm_c_w=norm_c_w,
                   w_in_c=w_in_c, gla_w_gate=gla_w_gate, gla_b_gate=gla_b_gate, gla_norm_w=gla_norm_w,
                   w_out_c=w_out_c, final_norm_w=final_norm_w)
    mom_m = dict(meta=m_meta, norm_ab_w=m_norm_ab_w, w_in_ab=m_w_in_ab, ret_norm_w=m_ret_norm_w,
                 s5_lam_re=m_s5_lam_re, s5_lam_im=m_s5_lam_im, s5_log_dt=m_s5_log_dt, s5_b_re=m_s5_b_re,
                 s5_b_im=m_s5_b_im, s5_c_re=m_s5_c_re, s5_c_im=m_s5_c_im, s5_d=m_s5_d, s5_w_glu=m_s5_w_glu,
                 w_out_ab=m_w_out_ab, norm_c_w=m_norm_c_w, w_in_c=m_w_in_c, gla_w_gate=m_gla_w_gate,
                 gla_b_gate=m_gla_b_gate, gla_norm_w=m_gla_norm_w, w_out_c=m_w_out_c, final_norm_w=m_final_norm_w)
    mom_v = dict(meta=v_meta, norm_ab_w=v_norm_ab_w, w_in_ab=v_w_in_ab, ret_norm_w=v_ret_norm_w,
                 s5_lam_re=v_s5_lam_re, s5_lam_im=v_s5_lam_im, s5_log_dt=v_s5_log_dt, s5_b_re=v_s5_b_re,
                 s5_b_im=v_s5_b_im, s5_c_re=v_s5_c_re, s5_c_im=v_s5_c_im, s5_d=v_s5_d, s5_w_glu=v_s5_w_glu,
                 w_out_ab=v_w_out_ab, norm_c_w=v_norm_c_w, w_in_c=v_w_in_c, gla_w_gate=v_gla_w_gate,
                 gla_b_gate=v_gla_b_gate, gla_norm_w=v_gla_norm_w, w_out_c=v_w_out_c, final_norm_w=v_final_norm_w)
    order = list(weights)

    seq = x.shape[1]
    lp = CHUNK + seq
    nchunk = lp // CHUNK
    dev = 4 * lax.axis_index("x") + 2 * lax.axis_index("y") + lax.axis_index("c")
    core = lax.axis_index("c")
    chip = 2 * lax.axis_index("x") + lax.axis_index("y")

    win_off = jnp.reshape(2 * dev, (1,)).astype(jnp.int32)
    shard_c = jnp.pad(w_in_c[0].astype(BF16), ((0, 0), (0, 896 - SHARD_C)))
    big_shards = dict(w_in_ab=w_in_ab[0].astype(BF16), s5_w_glu=s5_w_glu[0].astype(BF16),
                      w_out_ab=w_out_ab[0].astype(BF16), w_out_c=w_out_c[0].astype(BF16),
                      w_in_c=_lane_select(shard_c, win_off, 1, WIN_COLS, BF16, False, "w_in_c_to_window"))
    def pad_to(a, rows, cols):
        return jnp.pad(a, ((0, rows - a.shape[0]), (0, cols - a.shape[1])))

    shard_w = D_MODEL // N_DEV
    small_pack = jnp.concatenate([meta, pad_to(norm_c_w, 8, shard_w), pad_to(gla_w_gate[0], GLA_RANK, shard_w),
                                  pad_to(gla_b_gate, 8, shard_w), pad_to(gla_norm_w, 8, shard_w)], axis=0)
    w_in_ab_g = _all_gather(big_shards["w_in_ab"], "gather_first")
    win_cut = 1408
    in_c_hook_a = _gather_hook(big_shards["w_in_c"][:win_cut])
    in_c_hook_b = _gather_hook(big_shards["w_in_c"][win_cut:])
    glu_ab_hook = _gather_hook(_pack_big(big_shards, GLU_AB_LAYOUT))
    out_c_hook = _gather_hook(_to_rows(big_shards["w_out_c"]))
    gs = _all_gather(small_pack, "gather_small")
    gate_w = GLA_QK // N_DEV
    s_meta, s_norm_c = gs[:, :N_META], gs[:, N_META]
    s_wgate, s_bgate, s_gnorm = gs[:, 24:24 + GLA_RANK, :gate_w], gs[:, 40, :gate_w], gs[:, 48]
    meta_f = s_meta.transpose(1, 0, 2).reshape(N_META, D_MODEL)
    norm_c_f = s_norm_c.reshape(1, D_MODEL)
    w_gate_f = jnp.pad(s_wgate.transpose(1, 0, 2).reshape(GLA_RANK, GLA_QK), ((0, GATE_PAD - GLA_RANK), (0, 0)))
    b_gate_f = s_bgate.reshape(1, GLA_QK)
    gla_norm_f = s_gnorm.reshape(GLA_H, 1, GLA_DV)

    pos = jnp.maximum(jnp.arange(lp, dtype=F32) - float(PAD), 0.0)
    inv_freq = jnp.power(ROPE_BASE, -jnp.arange(0, RET_DK, 2, dtype=F32) / RET_DK)
    ang = pos[:, None] * inv_freq[None, :]
    cos2 = jnp.concatenate([jnp.cos(ang), jnp.cos(ang)], axis=1)
    sin2 = jnp.concatenate([-jnp.sin(ang), jnp.sin(ang)], axis=1)
    log_g = jnp.log1p(-jnp.exp2(-5.0 - jnp.arange(RET_H, dtype=F32)))
    lg = jnp.broadcast_to(log_g[:, None, None], (RET_H, 1, 128))
    ret_norm_h = ret_norm_w.reshape(RET_H, 1, RET_DV)

    h0 = jnp.concatenate([jnp.zeros((PAD, D_MODEL), F32), meta_f, x[0]], axis=0)

    def rowmask(i):
        return (_iota2((CHUNK, 1), 0) + i * CHUNK) >= PAD

    (hn0,), _ = _rows(lambda i, h, w: ((_rms(h, w),), ()), [_win(h0)], [norm_ab_w], [(D_MODEL, BF16)], [],
                      name="norm_ab_fwd", nrow=lp)
    proj_ab, (w_in_c_ga,) = _mm(hn0, w_in_ab_g, "nn", name="in_ab_fwd", hook=in_c_hook_a, b_dev=True)

    q_off, k_off, v_off, za_off = 0, RET_QK, 2 * RET_QK, 2 * RET_QK + RET_W
    u_off, zb_off = 2 * RET_QK + 2 * RET_W, 2 * RET_QK + 2 * RET_W + S5_W
    ret_xs = [(proj_ab, RET_DK, lambda h: q_off // RET_DK + h), (proj_ab, RET_DK, lambda h: k_off // RET_DK + h),
              (proj_ab, RET_DV, lambda h: v_off // RET_DV + h), (proj_ab, RET_DV, lambda h: za_off // RET_DV + h)]
    ret_cs = [(cos2, RET_DK, lambda h: 0), (sin2, RET_DK, lambda h: 0)]
    ret_kw = dict(heads=RET_H, nchunk=nchunk, s_shape=(RET_DK, RET_DV), out_w=RET_DV, pre=_ret_pre, hpb=4)
    o_a, ret_sprev, (gathered_glu_ab,) = _scan_fwd(_ret_chunk, ret_xs, ret_cs, [ret_norm_h], [lg], name="ret_fwd",
                                                   hook=glu_ab_hook, **ret_kw)
    gb = _unpack_big(gathered_glu_ab, GLU_AB_LAYOUT)
    w_glu_f = gb["s5_w_glu"].reshape(S5_W, S5_W)
    w_out_ab_f = gb["w_out_ab"].reshape(OUT_AB, D_MODEL)

    expand = jnp.repeat(jnp.eye(S5_P, dtype=F32), S5_GH, axis=1)
    disc_args = (s5_lam_re[0], s5_lam_im[0], s5_log_dt[0].reshape(S5_G, 1),
                 s5_b_re[0].reshape(S5_G, S5_P * S5_GH), s5_b_im[0].reshape(S5_G, S5_P * S5_GH), expand)
    ab_re, ab_im, bb_re, bb_im = _s5_disc_fwd(disc_args)
    gt = S5_SUBL
    eye_t = jnp.eye(gt, dtype=F32)

    def tiles_in(bb):
        return jnp.einsum("sgph,gk->sghkp", bb.reshape(gt, gt, S5_P, S5_GH), eye_t).reshape(gt, 128, S5_LANES)

    def tiles_out(cc):
        return jnp.einsum("sghp,gk->sgpkh", cc.reshape(gt, gt, S5_GH, S5_P), eye_t).reshape(gt, S5_LANES, 128)

    wb_t = jnp.concatenate([tiles_in(bb_re), tiles_in(bb_im)], axis=0).astype(BF16)
    wc_t = jnp.concatenate([tiles_out(s5_c_re[0]), -tiles_out(s5_c_im[0])], axis=0).astype(BF16)
    a_re, a_im = ab_re.reshape(S5_SUBL, S5_LANES), ab_im.reshape(S5_SUBL, S5_LANES)
    tm5, tk5, nt5 = _tile(lp, 1408, 8), _tile(lp, 1408, 8), 2 * gt
    u_blk = u_off // 128
    wide = pl.BlockSpec((tm5, S5_LANES), lambda i, j, k: (i, j))
    wide_k = pl.BlockSpec((tm5, S5_LANES), lambda i, j, k: (i, k * gt + j))
    narrow = pl.BlockSpec((tm5, 128), lambda i, j, k: (i, j))
    wb_j = pl.BlockSpec((None, 128, S5_LANES), lambda i, j, k: (j, 0, 0))
    wc_j = pl.BlockSpec((None, S5_LANES, 128), lambda i, j, k: (j, 0, 0))
    wb_k = pl.BlockSpec((None, 128, S5_LANES), lambda i, j, k: (k * gt + j, 0, 0))
    wc_k = pl.BlockSpec((None, S5_LANES, 128), lambda i, j, k: (k * gt + j, 0, 0))
    wide_shape = jax.ShapeDtypeStruct((lp, 2 * S5_N), F32)
    bu, (w_out_c_g,) = _mm_core(proj_ab, wb_t, dims=NN, grid=(lp // tm5, nt5, 1), name="s5_bu",
                                a_spec=pl.BlockSpec((tm5, 128), lambda i, j, k: (i, u_blk + j % gt)), b_spec=wb_j,
                                o_spec=wide, out_shape=wide_shape, acc_shape=(tm5, S5_LANES), hook=out_c_hook)
    w_out_c_f = _from_rows(w_out_c_g, D_MODEL).reshape(GLA_W, D_MODEL)
    xs5 = _s5_scan_fwd(bu.reshape(lp, 2 * S5_SUBL, S5_LANES), a_re, a_im)
    xs5_2d = xs5.reshape(lp, 2 * S5_N)
    y_pre, (w_in_c_gb,) = _mm_core(xs5_2d, wc_t, dims=NN, grid=(lp // tm5, gt, 2), name="s5_cx", a_spec=wide_k,
                                   b_spec=wc_k, o_spec=narrow, out_shape=jax.ShapeDtypeStruct((lp, S5_W), F32),
                                   acc_shape=(tm5, 128), hook=in_c_hook_b)
    (y_s5, yg_bf), _ = _rows(
        lambda i, yp, u, d: ((yp + d * u, _gelu(yp + d * u)), ()),
        [_win(y_pre), _win(proj_ab, u_off, S5_W)], [s5_d], [(S5_W, F32), (S5_W, BF16)], [], name="s5_gelu_fwd", nrow=lp)
    t_glu = _mm(yg_bf, w_glu_f, "nn", name="s5_glu_fwd")

    def s5_gate(y, t, zb):
        return _gelu(y) * _sigmoid(t) * _silu(zb)

    (o_b,), _ = _rows(lambda i, y, t, zb: ((s5_gate(y, t, zb),), ()),
                      [_win(y_s5), _win(t_glu), _win(proj_ab, zb_off, S5_W)], [], [(S5_W, BF16)], [],
                      name="s5_gate_fwd", nrow=lp)
    o_ab = jnp.concatenate([o_a, o_b], axis=1)
    h1 = _mm(o_ab, w_out_ab_f, "nn", name="out_ab_fwd", add=h0)
    w_in_c_g = jnp.concatenate([w_in_c_ga, w_in_c_gb], axis=1)
    w_in_c_f = sum(jnp.pad(w_in_c_g[d], ((0, 0), (WIN_STEP * d, IN_C_PAD - WIN_STEP * d - WIN_COLS)))
                   for d in range(N_DEV))

    (hn1,), _ = _rows(lambda i, h, w: ((_rms(h, w),), ()), [_win(h1)], [norm_c_f], [(D_MODEL, BF16)], [],
                      name="norm_c_fwd", nrow=lp)
    proj_c = _mm(hn1, w_in_c_f, "nn", name="in_c_fwd")
    gl_off = 2 * GLA_QK + 2 * GLA_W
    pre_gate = _mm(proj_c, w_gate_f, "nn", name="gate_fwd", a_win=(gl_off, GATE_PAD), bias=b_gate_f)
    gla_xs = [(proj_c, GLA_DK, lambda h: h), (proj_c, GLA_DK, lambda h: GLA_QK // GLA_DK + h),
              (proj_c, GLA_DV, lambda h: 2 * GLA_QK // GLA_DV + h),
              (proj_c, GLA_DV, lambda h: (2 * GLA_QK + GLA_W) // GLA_DV + h),
              (pre_gate, GLA_DK, lambda h: h)]
    gla_kw = dict(heads=GLA_H, nchunk=nchunk, s_shape=(GLA_DV, GLA_DK), out_w=GLA_DV, hpb=GLA_H)
    o_c, gla_sprev = _scan_fwd(_gla_chunk, gla_xs, [], [gla_norm_f], [], name="gla_fwd", **gla_kw)
    h2 = _mm(o_c, w_out_c_f, "nn", name="out_c_fwd", add=h1)

    fnw = final_norm_w.reshape(1, D_MODEL)

    def final_fn(i, h, tgt, w):
        def loss_of(h_, w_):
            err = _rms(h_, w_) - tgt
            return 0.5 * jnp.sum(jnp.mean(err * err, axis=-1))

        real = (i > 0).astype(F32)
        loss_i, (dh, dw) = jax.value_and_grad(loss_of, argnums=(0, 1))(h, w)
        return (dh * real,), (jnp.full((1, 128), loss_i * real, F32), dw * real)

    (dh2,), (loss_acc, g_final) = _rows(final_fn, [_win(h2), _win(loss_target[0], roff=1)], [fnw],
                                        [(D_MODEL, F32)], [(1, 128), (1, D_MODEL)], name="final_loss", nrow=lp)

    def rs_front(pieces, layout, tag):
        g_full = _pack_big(pieces, layout)
        prow = g_full.shape[1]
        from_sibling = _swap_with_sibling(g_full, "rs_sibling_" + tag)
        mine_by_chip = lax.dynamic_index_in_dim(g_full.reshape(4, 2, prow, PACK_COLS), core, axis=1, keepdims=False)
        (p1, p1_bf), _ = _rows(
            lambda i, a, b: ((a.astype(F32) + b.astype(F32), a.astype(F32) + b.astype(F32)), ()),
            [_win(mine_by_chip.reshape(4 * prow, PACK_COLS)), _win(from_sibling.reshape(4 * prow, PACK_COLS))], [],
            [(PACK_COLS, F32), (PACK_COLS, BF16)], [], name="rs_sum_sibling_" + tag, nrow=4 * prow,
            tr=_tile(prow, 512, 16))
        return p1.reshape(4, prow, PACK_COLS), p1_bf.reshape(4, prow, PACK_COLS)

    def rs_back(p1, from_chips, layout, tag):
        prow = p1.shape[1]
        tr = _tile(prow, 512, 16)
        own = lax.dynamic_index_in_dim(p1, chip, axis=0, keepdims=False)
        fc2 = from_chips.reshape(3 * prow, PACK_COLS)
        nblk = prow // tr
        (g_shard,), _ = _rows(
            lambda i, a, b0, b1, b2: ((((a + b0.astype(F32)) + b1.astype(F32)) + b2.astype(F32),), ()),
            [_win(own), _win(fc2), _win(fc2, roff=-nblk), _win(fc2, roff=-2 * nblk)], [], [(PACK_COLS, F32)], [],
            name="rs_sum_chips_" + tag, nrow=prow, tr=tr)
        return _unpack_big(g_shard, layout)

    dh2_bf = dh2.astype(BF16)
    do_c = _mm(dh2_bf, w_out_c_f, "nt", name="out_c_dx", out_dtype=BF16)
    gw_out_c = _mm(o_c, dh2_bf, "tn", name="out_c_dw", out_dtype=BF16)
    (dq_c, dk_c, dv_c, dz_c, dpre), (g_gla_norm,) = _scan_bwd(
        _gla_chunk, gla_xs, [], [gla_norm_f], [], do_c, gla_sprev, name="gla_bwd", **gla_kw)
    dglow = _mm(dpre, w_gate_f, "nt", name="gate_dx", out_dtype=BF16)
    g_wgate = _mm(proj_c, dpre, "tn", name="gate_dw", a_win=(gl_off, GATE_PAD))[:GLA_RANK]
    (), (g_bgate,) = _rows(lambda i, d: ((), (jnp.sum(d.astype(F32), axis=0, keepdims=True),)), [_win(dpre)], [], [],
                           [(1, GLA_QK)], name="gate_db", nrow=lp)
    dproj_c = jnp.concatenate([dq_c, dk_c, dv_c, dz_c, dglow], axis=1)
    dhn1 = _mm(dproj_c, w_in_c_f, "nt", name="in_c_dx")
    gw_in_c = _mm(hn1, dproj_c, "tn", name="in_c_dw", out_dtype=BF16)
    p1_c, p1_c_bf = rs_front(dict(
        w_in_c=jnp.stack([gw_in_c[:, WIN_STEP * d:WIN_STEP * d + WIN_COLS] for d in range(N_DEV)])),
        IN_C_LAYOUT, "in_c")

    def norm_bwd(i, h, dhn, dres, w):
        _, vjp = jax.vjp(_rms, h, w)
        dh, dw = vjp(dhn)
        return (jnp.where(rowmask(i), dh + dres, 0.0),), (dw,)

    (dh1,), (g_norm_c,) = _rows(norm_bwd, [_win(h1), _win(dhn1), _win(dh2)], [norm_c_f], [(D_MODEL, F32)],
                                [(1, D_MODEL)], name="norm_c_bwd", nrow=lp)

    dh1_bf = dh1.astype(BF16)
    do_ab = _mm(dh1_bf, w_out_ab_f, "nt", name="out_ab_dx", out_dtype=BF16)
    gw_out_ab = _mm(o_ab, dh1_bf, "tn", name="out_ab_dw", out_dtype=BF16)

    def s5_gate_bwd(i, dob, y, t, zb):
        _, vjp = jax.vjp(s5_gate, y, t, zb)
        dy, dt, dzb = vjp(dob.astype(F32))
        return (dy, dt, dzb), ()

    (dy_a, dt_glu, dzb), _ = _rows(
        s5_gate_bwd, [_win(do_ab, RET_W, S5_W), _win(y_s5), _win(t_glu), _win(proj_ab, zb_off, S5_W)], [],
        [(S5_W, F32), (S5_W, BF16), (S5_W, BF16)], [], name="s5_gate_bwd", nrow=lp)
    dyg2 = _mm(dt_glu, w_glu_f, "nt", name="s5_glu_dx")
    gw_glu = _mm(yg_bf, dt_glu, "tn", name="s5_glu_dw", out_dtype=BF16)

    def s5_y_bwd(i, dya, dyg, y, u, d):
        _, vjp = jax.vjp(_gelu, y)
        (dy_g,) = vjp(dyg)
        dy = dya + dy_g
        return (dy, d * dy), (jnp.sum(dy * u, axis=0, keepdims=True),)

    (dy_s5, du1), (g_d,) = _rows(
        s5_y_bwd, [_win(dy_a), _win(dyg2), _win(y_s5), _win(proj_ab, u_off, S5_W)], [s5_d],
        [(S5_W, BF16), (S5_W, F32)], [(1, S5_W)], name="s5_y_bwd", nrow=lp)
    p1_o, p1_o_bf = rs_front(dict(s5_w_glu=gw_glu.reshape(N_DEV, S5_W // N_DEV, S5_W),
                                  w_out_ab=gw_out_ab.reshape(N_DEV, OUT_AB // N_DEV, D_MODEL),
                                  w_out_c=gw_out_c.reshape(N_DEV, GLA_W // N_DEV, D_MODEL)), OTHER_LAYOUT, "other")
    o_cut = 640
    gx, (from_chips_oa,) = _mm_core(dy_s5, wc_t, dims=NT, grid=(lp // tm5, nt5, 1), name="s5_cx_dx",
                                    a_spec=pl.BlockSpec((tm5, 128), lambda i, j, k: (i, j % gt)), b_spec=wc_j,
                                    o_spec=wide, out_shape=wide_shape, acc_shape=(tm5, S5_LANES),
                                    hook=_chips_hook(p1_o_bf, 0, o_cut))
    rows_k = lambda col: pl.BlockSpec((tk5, col), lambda i, j, k: (k, i))
    gwc = _mm_core(xs5_2d, dy_s5, dims=TN, grid=(nt5, 1, lp // tk5), name="s5_cx_dw", a_spec=rows_k(S5_LANES),
                   b_spec=pl.BlockSpec((tk5, 128), lambda i, j, k: (k, i % gt)),
                   o_spec=pl.BlockSpec((None, S5_LANES, 128), lambda i, j, k: (i, 0, 0)),
                   out_shape=jax.ShapeDtypeStruct((nt5, S5_LANES, 128), F32), acc_shape=(S5_LANES, 128))
    g_s5, da = _s5_scan_bwd(gx.reshape(lp, 2 * S5_SUBL, S5_LANES), xs5, a_re, a_im)
    g_s5_2d = g_s5.reshape(lp, 2 * S5_N)
    du, (from_chips_ob,) = _mm_core(g_s5_2d, wb_t, dims=NT, grid=(lp // tm5, gt, 2), name="s5_bu_dx", a_spec=wide_k,
                                    b_spec=wb_k, o_spec=narrow, out_shape=jax.ShapeDtypeStruct((lp, S5_W), BF16),
                                    acc_shape=(tm5, 128), extra=[(du1, narrow)],
                                    hook=_chips_hook(p1_o_bf, o_cut, None))
    from_chips_o = jnp.concatenate([from_chips_oa, from_chips_ob], axis=1)
    gwb = _mm_core(proj_ab, g_s5_2d, dims=TN, grid=(nt5, 1, lp // tk5), name="s5_bu_dw",
                   a_spec=pl.BlockSpec((tk5, 128), lambda i, j, k: (k, u_blk + i % gt)), b_spec=rows_k(S5_LANES),
                   o_spec=pl.BlockSpec((None, 128, S5_LANES), lambda i, j, k: (i, 0, 0)),
                   out_shape=jax.ShapeDtypeStruct((nt5, 128, S5_LANES), F32), acc_shape=(128, S5_LANES))
    gwc6 = gwc.reshape(2, gt, gt, S5_P, gt, S5_GH)
    g_c = jnp.einsum("rsgpgh->rsghp", gwc6).reshape(2, S5_G, S5_GH, S5_P)
    g_c_re, g_c_im = g_c[0], -g_c[1]
    gwb6 = gwb.reshape(2, gt, gt, S5_GH, gt, S5_P)
    d_bb = jnp.einsum("rsghgp->rsgph", gwb6).reshape(2, S5_G, S5_P * S5_GH)
    d_bb_re, d_bb_im = d_bb[0], d_bb[1]
    g_lam_re, g_lam_im, g_log_dt, g_b_re, g_b_im = _s5_disc_bwd(
        disc_args, (da[0].reshape(S5_G, S5_P), da[1].reshape(S5_G, S5_P), d_bb_re, d_bb_im))

    (dq_a, dk_a, dv_a, dz_a), (g_ret_norm,), (from_chips_c,) = _scan_bwd(
        _ret_chunk, ret_xs, ret_cs, [ret_norm_h], [lg], do_ab, ret_sprev, name="ret_bwd", post=_ret_post,
        hook=_chips_hook(p1_c_bf), **ret_kw)
    dproj_ab = jnp.concatenate([dq_a, dk_a, dv_a, dz_a, du, dzb], axis=1)

    lane = lambda a_: pad_to(a_, a_.shape[0], 128)

    def sum8(i, *blocks):
        acc = blocks[0]
        for b in blocks[1:]:
            acc = acc + b
        return (acc,), ()

    def pack_small(pieces):
        return jnp.concatenate([_rows1024(p) for _, p in pieces], axis=0)

    def sum_small(gathered, pieces, tag):
        srow = gathered.shape[1]
        tr = _tile(srow, 128, 8)
        flat = gathered.reshape(N_DEV * srow, PACK_COLS)
        (total,), _ = _rows(sum8, [_win(flat, roff=-d * (srow // tr)) for d in range(N_DEV)], [], [(PACK_COLS, F32)],
                            [], name="sum_small_" + tag, nrow=srow, tr=tr)
        out, o = {}, 0
        for name_, p in pieces:
            r8 = _rows1024(p).shape[0]
            out[name_] = _unrows1024(total[o:o + r8], *p.shape)
            o += r8
        return out

    early_pieces = [
        ("vec2048", jnp.concatenate([g_final, g_norm_c], axis=0)),
        ("vec1024", jnp.concatenate([g_d, g_bgate, pad_to(loss_acc[:, :1], 1, PACK_COLS)], axis=0)),
        ("lam3", jnp.concatenate([lane(g_lam_re), lane(g_lam_im), lane(g_log_dt)], axis=1)),
        ("s5_b_re", g_b_re), ("s5_b_im", g_b_im),
        ("s5_c_re", g_c_re.reshape(S5_G, S5_GH * S5_P)), ("s5_c_im", g_c_im.reshape(S5_G, S5_GH * S5_P)),
        ("ret_norm_w", g_ret_norm.reshape(RET_H, RET_DV)), ("gla_norm_w", g_gla_norm.reshape(GLA_H, GLA_DV)),
        ("gla_w_gate", g_wgate)]
    gw_in_ab, (early_all,) = _mm(
        hn0, dproj_ab, "tn", name="in_ab_dw", out_dest=True, out_dtype=BF16,
        hook=_gather_hook(pack_small(early_pieces)))
    p1_first, p1_first_bf = rs_front(dict(w_in_ab=gw_in_ab), FIRST_LAYOUT, "first")
    dhn0, (from_chips_first,) = _mm(dproj_ab, w_in_ab_g, "nt", name="in_ab_dx", b_dev=True,
                                    hook=_chips_hook(p1_first_bf))
    (dh0,), (g_norm_ab,) = _rows(norm_bwd, [_win(h0), _win(dhn0), _win(dh1)], [norm_ab_w], [(D_MODEL, F32)],
                                 [(1, D_MODEL)], name="norm_ab_bwd", nrow=lp)
    grad_x = dh0[CHUNK:][None]
    late_pieces = [("norm_ab_w", g_norm_ab), ("meta", dh0[PAD:CHUNK])]
    small = sum_small(early_all, early_pieces, "early")
    small.update(sum_small(_all_gather(pack_small(late_pieces), "gather_grads"), late_pieces, "late"))

    big_grads = {**rs_back(p1_c, from_chips_c, IN_C_LAYOUT, "in_c"), **rs_back(p1_o, from_chips_o, OTHER_LAYOUT, "other"),
                 **rs_back(p1_first, from_chips_first, FIRST_LAYOUT, "first")}
    big_grads["w_in_c"] = _lane_select(big_grads["w_in_c"], win_off, -1, 896, F32, True,
                                       "w_in_c_from_window")[:, :SHARD_C]
    small["final_norm_w"], small["norm_c_w"] = small["vec2048"][0:1], small["vec2048"][1:2]
    small["s5_d"], small["gla_b_gate"] = small["vec1024"][0:1], small["vec1024"][1:2]
    loss = small["vec1024"][2, 0]
    small["s5_lam_re"], small["s5_lam_im"] = small["lam3"][:, :S5_P], small["lam3"][:, 128:128 + S5_P]
    small["s5_log_dt"] = small["lam3"][:, 256:257]

    def my_cols(g, n):
        return lax.dynamic_slice_in_dim(g, dev * n, n, axis=g.ndim - 1)

    grads = dict(
        meta=my_cols(small["meta"], D_MODEL // N_DEV),
        norm_ab_w=small["norm_ab_w"], w_in_ab=big_grads["w_in_ab"][None], ret_norm_w=small["ret_norm_w"].reshape(1, RET_W),
        s5_lam_re=small["s5_lam_re"][None], s5_lam_im=small["s5_lam_im"][None],
        s5_log_dt=small["s5_log_dt"].reshape(1, S5_G),
        s5_b_re=small["s5_b_re"].reshape(1, S5_G, S5_P, S5_GH), s5_b_im=small["s5_b_im"].reshape(1, S5_G, S5_P, S5_GH),
        s5_c_re=small["s5_c_re"][None], s5_c_im=small["s5_c_im"][None], s5_d=small["s5_d"],
        s5_w_glu=big_grads["s5_w_glu"][None], w_out_ab=big_grads["w_out_ab"][None],
        norm_c_w=my_cols(small["norm_c_w"], D_MODEL // N_DEV), w_in_c=big_grads["w_in_c"][None],
        gla_w_gate=my_cols(small["gla_w_gate"], GLA_QK // N_DEV)[None],
        gla_b_gate=my_cols(small["gla_b_gate"], GLA_QK // N_DEV),
        gla_norm_w=my_cols(small["gla_norm_w"].reshape(1, GLA_W), GLA_W // N_DEV),
        w_out_c=big_grads["w_out_c"][None], final_norm_w=small["final_norm_w"].reshape(D_MODEL))

    deltas, new_m, new_v = {}, {}, {}
    for k in order:
        w = weights[k]
        d2, m2, v2 = _adamw(_as2d(w), _as2d(grads[k].reshape(w.shape)), _as2d(mom_m[k]), _as2d(mom_v[k]), "adamw_" + k)
        deltas[k], new_m[k], new_v[k] = d2.reshape(w.shape), m2.reshape(w.shape), v2.reshape(w.shape)
        grads[k] = grads[k].reshape(w.shape)

    return (loss, grad_x, *[grads[k] for k in order], *[deltas[k] for k in order],
            *[new_m[k] for k in order], *[new_v[k] for k in order])
```

```python
import functools
import math

import jax
import jax.numpy as jnp
from jax import lax
from jax.experimental import pallas as pl
from jax.experimental.pallas import tpu as pltpu

F32, BF16 = jnp.float32, jnp.bfloat16
MESH = pl.DeviceIdType.MESH
N_DEV = 8

D_MODEL = 2048
CHUNK = 128
N_META = 16
PAD = CHUNK - N_META
SUB = 16
EPS = 1e-6
RET_H, RET_DK, RET_DV = 8, 128, 256
RET_QK, RET_W = RET_H * RET_DK, RET_H * RET_DV
ROPE_BASE = 10000.0
S5_W, S5_G, S5_P, S5_GH = 1024, 64, 64, 16
S5_N = S5_G * S5_P
GLA_H, GLA_DK, GLA_DV, GLA_RANK, GLA_TAU = 4, 256, 512, 16, 16.0
GLA_QK, GLA_W = GLA_H * GLA_DK, GLA_H * GLA_DV
IN_AB = 2 * RET_QK + 2 * RET_W + 2 * S5_W
OUT_AB = RET_W + S5_W
IN_C = 2 * GLA_QK + 2 * GLA_W + GLA_RANK
GATE_PAD = 256
IN_C_PAD = 2 * GLA_QK + 2 * GLA_W + GATE_PAD
ADAM_LR, ADAM_B1, ADAM_B2, ADAM_EPS, ADAM_WD, ADAM_STEP = 0.001, 0.9, 0.999, 1e-08, 0.01, 10

VMEM_LIMIT_BYTES = 48 * 2 ** 20
PACK_COLS = 1024
PACK_ROW_MULT = 8
SHARD_C = IN_C // N_DEV
WIN_STEP = 768
WIN_COLS = 1024


def _params(sem):
    return pltpu.CompilerParams(dimension_semantics=sem, vmem_limit_bytes=VMEM_LIMIT_BYTES)


def _tile(n, cap, mult):
    best = None
    for t in range(mult, min(n, cap) + 1, mult):
        if n % t == 0:
            best = t
    assert best is not None, (n, cap, mult)
    return best


def _dg(a, b, ca, cb):
    return lax.dot_general(a.astype(BF16), b.astype(BF16), (((ca,), (cb,)), ((), ())),
                           preferred_element_type=F32)


@functools.partial(jax.custom_vjp, nondiff_argnums=(2, 3))
def _bdot(a, b, ca, cb):
    return _dg(a, b, ca, cb)


def _bdot_fwd(a, b, ca, cb):
    return _dg(a, b, ca, cb), (a, b)


def _bdot_bwd(ca, cb, res, g):
    a, b = res
    da = _dg(g, b, 1, 1 - cb) if ca == 1 else _dg(b, g, 1 - cb, 1)
    db = _dg(a, g, 1 - ca, 0) if cb == 0 else _dg(g, a, 0, 1 - ca)
    return da.astype(a.dtype), db.astype(b.dtype)


_bdot.defvjp(_bdot_fwd, _bdot_bwd)


def _sigmoid(x):
    return 1.0 / (1.0 + jnp.exp(-x))


def _silu(x):
    return x * _sigmoid(x)


def _log_sigmoid(x):
    return jnp.minimum(x, 0.0) - jnp.log(1.0 + jnp.exp(-jnp.abs(x)))


def _gelu(x):
    return 0.5 * x * (1.0 + jnp.tanh(math.sqrt(2.0 / math.pi) * (x + 0.044715 * (x * x * x))))


def _rms(x, w):
    return x * lax.rsqrt(jnp.mean(x * x, axis=-1, keepdims=True) + EPS) * w


class _Hook:
    def __init__(self, ins, outs, sems, phases):
        self.ins, self.outs, self.sems, self.phases = list(ins), list(outs), list(sems), list(phases)


def _merge_hooks(first, second):
    ni, no, ns = len(first.ins), len(first.outs), len(first.sems)
    phases = [(f, lambda i, o, s, fn=fn: fn(i[:ni], o[:no], s[:ns])) for f, fn in first.phases]
    phases += [(f, lambda i, o, s, fn=fn: fn(i[ni:], o[no:], s[ns:])) for f, fn in second.phases]
    return _Hook(first.ins + second.ins, first.outs + second.outs, first.sems + second.sems,
                 sorted(phases, key=lambda p: p[0]))


_NO_HOOK = _Hook([], [], [], [])
_ANY = pl.BlockSpec(memory_space=pl.ANY)


def _run_hook(hook, lin, total, in_refs, out_refs, sem_refs):
    for frac, fn in hook.phases:
        at = min(int(frac * total), total - 1)

        @pl.when(lin == at)
        def _(fn=fn):
            fn(in_refs, out_refs, sem_refs)


def _mm_core(a, b, *, dims, grid, a_spec, b_spec, o_spec, out_shape, acc_shape, name, extra=(), hook=None,
             b_parts=0):
    nk = grid[2]
    n_extra = len(extra)
    hook = _NO_HOOK if hook is None else hook
    hi, ho = len(hook.ins), len(hook.outs)

    def body(*refs):
        a_ref, b_ref = refs[0], refs[1]
        o_ref, acc = refs[2 + n_extra + hi], refs[3 + n_extra + hi + ho]
        k = pl.program_id(2)
        lin = (pl.program_id(0) * grid[1] + pl.program_id(1)) * nk + k
        _run_hook(hook, lin, grid[0] * grid[1] * nk, refs[2 + n_extra:2 + n_extra + hi],
                  refs[3 + n_extra + hi:3 + n_extra + hi + ho], refs[4 + n_extra + hi + ho:])

        if b_parts:
            part = sum(lax.dot_general(a_ref[:, d * PACK_COLS:(d + 1) * PACK_COLS].astype(BF16), b_ref[d].astype(BF16),
                                       dims, preferred_element_type=F32) for d in range(b_parts))
        else:
            part = lax.dot_general(a_ref[...].astype(BF16), b_ref[...].astype(BF16), dims, preferred_element_type=F32)

        def finish(r):
            for e in range(n_extra):
                r = r + refs[2 + e][...].astype(F32)
            o_ref[...] = r.astype(o_ref.dtype)

        if nk == 1:
            finish(part)
        else:
            @pl.when(k == 0)
            def _():
                acc[...] = part

            @pl.when(k > 0)
            def _():
                acc[...] += part

            @pl.when(k == nk - 1)
            def _():
                finish(acc[...])

    res = pl.pallas_call(
        body, name=name, grid=grid,
        in_specs=[a_spec, b_spec] + [sp for _, sp in extra] + [_ANY] * hi,
        out_specs=[o_spec] + [_ANY] * ho, out_shape=[out_shape] + hook.outs,
        scratch_shapes=[pltpu.VMEM(acc_shape if nk > 1 else (8, 128), F32)] + hook.sems,
        compiler_params=_params(("arbitrary", "arbitrary", "arbitrary")),
    )(a, b, *[arr for arr, _ in extra], *hook.ins)
    return res[0] if hook is _NO_HOOK else (res[0], res[1:])


NN, NT, TN = (((1,), (0,)), ((), ())), (((1,), (1,)), ((), ())), (((0,), (0,)), ((), ()))


FULL_K = 2048


def _mm(a, b, mode, *, name, out_dtype=F32, a_win=None, add=None, bias=None, hook=None, b_dev=False,
        out_dest=False):
    b_parts = 0
    if mode == "tn":
        kdim, n = a.shape[0], b.shape[1]
        m = a.shape[1] if a_win is None else a_win[1]
        tm, tn, tk = _tile(m, 512, 128), _tile(n, 640, 128), kdim
        off = 0 if a_win is None else a_win[0] // tm
        a_spec = pl.BlockSpec((tk, tm), lambda i, j, k: (k, i + off))
        b_spec = pl.BlockSpec((tk, tn), lambda i, j, k: (k, j))
        dims = TN
    else:
        m = a.shape[0]
        kdim = a.shape[1] if a_win is None else a_win[1]
        if b_dev:
            n = b.shape[0] * b.shape[2] if mode == "nn" else b.shape[1]
        else:
            n = b.shape[1] if mode == "nn" else b.shape[0]
        if FULL_K < kdim <= 2 * FULL_K and not b_dev:
            tm, tn, tk = _tile(m, 1408, 8), _tile(n, 1024, 128), _tile(kdim, 1024, 128)
        else:
            tm = _tile(m, 1408 if kdim <= FULL_K else 352, 8)
            tn, tk = _tile(n, 640, 128), kdim
        off = 0 if a_win is None else a_win[0] // tk
        a_spec = pl.BlockSpec((tm, tk), lambda i, j, k: (i, k + off))
        if mode == "nn":
            dims = NN
            if b_dev:
                per = PACK_COLS // tn
                b_spec = pl.BlockSpec((None, tk, tn), lambda i, j, k: (j // per, k, j % per))
            else:
                b_spec = pl.BlockSpec((tk, tn), lambda i, j, k: (k, j))
        else:
            dims = NT
            if b_dev:
                b_parts = kdim // PACK_COLS
                b_spec = pl.BlockSpec((b_parts, tn, PACK_COLS), lambda i, j, k: (0, j, 0))
            else:
                b_spec = pl.BlockSpec((tn, tk), lambda i, j, k: (j, k))
    if a_win is not None:
        assert a_win[0] % (tm if mode == "tn" else tk) == 0
    extra = []
    if add is not None:
        extra.append((add, pl.BlockSpec((tm, tn), lambda i, j, k: (i, j))))
    if bias is not None:
        extra.append((bias, pl.BlockSpec((1, tn), lambda i, j, k: (0, j))))
    if out_dest:
        per = PACK_COLS // tn
        o_spec = pl.BlockSpec((None, tm, tn), lambda i, j, k: (j // per, i, j % per))
        out_shape = jax.ShapeDtypeStruct((n // PACK_COLS, m, PACK_COLS), out_dtype)
    else:
        o_spec = pl.BlockSpec((tm, tn), lambda i, j, k: (i, j))
        out_shape = jax.ShapeDtypeStruct((m, n), out_dtype)
    return _mm_core(a, b, dims=dims, grid=(m // tm, n // tn, kdim // tk), a_spec=a_spec, b_spec=b_spec,
                    o_spec=o_spec, out_shape=out_shape, acc_shape=(tm, tn), name=name, extra=extra, hook=hook,
                    b_parts=b_parts)


def _win(arr, col0=0, width=None, roff=0):
    return (arr, col0, arr.shape[1] if width is None else width, roff)


def _rows(fn, rows, consts, outs, accs, *, name, nrow, tr=CHUNK):
    nr, nc, no = len(rows), len(consts), len(outs)

    def body(*refs):
        i = pl.program_id(0)
        ins = [r[...] for r in refs[:nr + nc]]
        o_refs = refs[nr + nc:nr + nc + no]
        a_refs = refs[nr + nc + no:]
        res_o, res_a = fn(i, *ins)
        for r, v in zip(o_refs, res_o):
            r[...] = v.astype(r.dtype)
        if a_refs:
            @pl.when(i == 0)
            def _():
                for r in a_refs:
                    r[...] = jnp.zeros_like(r)

            for r, v in zip(a_refs, res_a):
                r[...] += v

    in_specs = []
    for (arr, col0, width, roff) in rows:
        assert col0 % width == 0 and arr.shape[0] % tr == 0
        in_specs.append(pl.BlockSpec((tr, width), lambda i, c=col0 // width, ro=roff: (jnp.maximum(i - ro, 0), c)))
    for c in consts:
        in_specs.append(pl.BlockSpec(c.shape, lambda i, nd=c.ndim: (0,) * nd))
    outs = [tuple(o) + (0,) * (3 - len(o)) for o in outs]
    out_specs = [pl.BlockSpec((tr, w), lambda i, ro=ro: (jnp.maximum(i - ro, 0), 0)) for (w, _, ro) in outs]
    out_specs += [pl.BlockSpec(s, lambda i, nd=len(s): (0,) * nd) for s in accs]
    out_shape = [jax.ShapeDtypeStruct((nrow - ro * tr, w), dt) for (w, dt, ro) in outs]
    out_shape += [jax.ShapeDtypeStruct(s, F32) for s in accs]
    res = pl.pallas_call(
        body, name=name, grid=(nrow // tr,), in_specs=in_specs, out_specs=out_specs, out_shape=out_shape,
        compiler_params=_params(("arbitrary",)),
    )(*[r[0] for r in rows], *consts)
    return res[:no], res[no:]


HEADS_PER_STEP = 2


def _scan_specs(xs, cs, ws, ks, chunk_of, hpb):
    specs = []
    for (arr, width, colfn) in xs:
        specs.append(pl.BlockSpec((CHUNK, width * hpb), lambda h, n, f=colfn: (chunk_of(n), f(h * hpb) // hpb)))
    for (arr, width, colfn) in cs:
        specs.append(pl.BlockSpec((CHUNK, width), lambda h, n, f=colfn: (chunk_of(n), f(h))))
    for arr in list(ws) + list(ks):
        specs.append(pl.BlockSpec((hpb, 1, arr.shape[2]), lambda h, n: (h, 0, 0)))
    return specs


def _scan_fwd(fn, xs, cs, ws, ks, *, heads, nchunk, s_shape, out_w, name, pre=None, hook=None,
              hpb=HEADS_PER_STEP):
    nx, ncs, nw = len(xs), len(cs), len(ws)
    hook = _NO_HOOK if hook is None else hook
    hi, ho = len(hook.ins), len(hook.outs)
    hblocks = heads // hpb

    def body(*refs):
        n = pl.program_id(1)
        nin = nx + ncs + nw + len(ks)
        y_ref, sp_ref = refs[nin + hi], refs[nin + hi + 1]
        s_scr = refs[nin + hi + 2 + ho]
        _run_hook(hook, pl.program_id(0) * nchunk + n, hblocks * nchunk, refs[nin:nin + hi],
                  refs[nin + hi + 2:nin + hi + 2 + ho], refs[nin + hi + 3 + ho:])

        @pl.when(n == 0)
        def _():
            s_scr[...] = jnp.zeros_like(s_scr)

        cv = [r[...] for r in refs[nx:nx + ncs]]
        for e in range(hpb):
            state = s_scr[e]
            sp_ref[e, 0] = state
            xv = [r[:, e * w:(e + 1) * w] for r, (_, w, _) in zip(refs[:nx], xs)]
            wv = [r[e] for r in refs[nx + ncs:nx + ncs + nw]]
            kv = [r[e] for r in refs[nx + ncs + nw:nin]]
            if pre is not None:
                xv = pre(xv, cv)
            y, s_new = fn(n, xv, state, cv, wv, kv)
            y_ref[:, e * out_w:(e + 1) * out_w] = y.astype(y_ref.dtype)
            s_scr[e] = s_new

    lp = nchunk * CHUNK
    res = pl.pallas_call(
        body, name=name, grid=(hblocks, nchunk),
        in_specs=_scan_specs(xs, cs, ws, ks, lambda n: n, hpb) + [_ANY] * hi,
        out_specs=[pl.BlockSpec((CHUNK, out_w * hpb), lambda h, n: (n, h)),
                   pl.BlockSpec((hpb, 1) + s_shape, lambda h, n: (h, n, 0, 0))] + [_ANY] * ho,
        out_shape=[jax.ShapeDtypeStruct((lp, heads * out_w), BF16),
                   jax.ShapeDtypeStruct((heads, nchunk) + s_shape, F32)] + hook.outs,
        scratch_shapes=[pltpu.VMEM((hpb,) + s_shape, F32)] + hook.sems,
        compiler_params=_params(("arbitrary", "arbitrary")),
    )(*[t[0] for t in xs], *[t[0] for t in cs], *ws, *ks, *hook.ins)
    return (res[0], res[1]) if hook is _NO_HOOK else (res[0], res[1], res[2:])


def _scan_bwd(fn, xs, cs, ws, ks, dy, sprev, *, heads, nchunk, s_shape, out_w, name, pre=None, post=None,
              hook=None, hpb=HEADS_PER_STEP):
    nx, ncs, nw = len(xs), len(cs), len(ws)
    nin = nx + ncs + nw + len(ks)
    hook = _NO_HOOK if hook is None else hook
    hi, ho = len(hook.ins), len(hook.outs)
    hblocks = heads // hpb

    def body(*refs):
        step = pl.program_id(1)
        n = nchunk - 1 - step
        dy_ref, sp_ref = refs[nin], refs[nin + 1]
        o0 = nin + 2 + hi
        dx_refs = refs[o0:o0 + nx]
        dw_refs = refs[o0 + nx:o0 + nx + nw]
        ds_scr = refs[o0 + nx + nw + ho]
        _run_hook(hook, pl.program_id(0) * nchunk + step, hblocks * nchunk, refs[nin + 2:o0],
                  refs[o0 + nx + nw:o0 + nx + nw + ho], refs[o0 + nx + nw + ho + 1:])

        @pl.when(step == 0)
        def _():
            ds_scr[...] = jnp.zeros_like(ds_scr)
            for r in dw_refs:
                r[...] = jnp.zeros_like(r)

        cv = [r[...] for r in refs[nx:nx + ncs]]
        for e in range(hpb):
            xv = [r[:, e * w:(e + 1) * w] for r, (_, w, _) in zip(refs[:nx], xs)]
            wv = [r[e] for r in refs[nx + ncs:nx + ncs + nw]]
            kv = [r[e] for r in refs[nx + ncs + nw:nin]]
            if pre is not None:
                xv = pre(xv, cv)
            _, vjp = jax.vjp(lambda xs_, s_, ws_, kv=kv: fn(n, xs_, s_, cv, ws_, kv), xv, sp_ref[e, 0], wv)
            dxs, ds_prev, dws = vjp((dy_ref[:, e * out_w:(e + 1) * out_w].astype(F32), ds_scr[e]))
            if post is not None:
                dxs = post(dxs, cv)
            for r, v, (_, w, _) in zip(dx_refs, dxs, xs):
                r[:, e * w:(e + 1) * w] = v.astype(r.dtype)
            for r, v in zip(dw_refs, dws):
                r[e] += v
            ds_scr[e] = ds_prev

    lp = nchunk * CHUNK
    rev = lambda n: nchunk - 1 - n
    in_specs = _scan_specs(xs, cs, ws, ks, rev, hpb)
    in_specs.append(pl.BlockSpec((CHUNK, out_w * hpb), lambda h, n: (rev(n), h)))
    in_specs.append(pl.BlockSpec((hpb, 1) + s_shape, lambda h, n: (h, rev(n), 0, 0)))
    out_specs = [pl.BlockSpec((CHUNK, w * hpb), lambda h, n: (rev(n), h)) for (_, w, _) in xs]
    out_specs += [pl.BlockSpec((hpb, 1, w.shape[2]), lambda h, n: (h, 0, 0)) for w in ws]
    out_shape = [jax.ShapeDtypeStruct((lp, heads * w), BF16) for (_, w, _) in xs]
    out_shape += [jax.ShapeDtypeStruct(w.shape, F32) for w in ws]
    res = pl.pallas_call(
        body, name=name, grid=(hblocks, nchunk), in_specs=in_specs + [_ANY] * hi,
        out_specs=out_specs + [_ANY] * ho, out_shape=out_shape + hook.outs,
        scratch_shapes=[pltpu.VMEM((hpb,) + s_shape, F32)] + hook.sems,
        compiler_params=_params(("arbitrary", "arbitrary")),
    )(*[t[0] for t in xs], *[t[0] for t in cs], *ws, *ks, dy, sprev, *hook.ins)
    if hook is _NO_HOOK:
        return res[:nx], res[nx:]
    return res[:nx], res[nx:nx + nw], res[nx + nw:]


def _iota2(shape, dim):
    return lax.broadcasted_iota(jnp.int32, shape, dim)


def _ret_chunk(n, xs, state, cs, ws, ks):
    q, k, v, z = xs
    (w,), (lg,) = ws, ks
    lgc = lg[:, :1]
    row, col = _iota2((CHUNK, CHUNK), 0), _iota2((CHUNK, CHUNK), 1)
    diff = jnp.maximum(row - col, 0).astype(F32)
    decay = jnp.where(row >= col, jnp.exp(lg * diff), 0.0)
    scores = _bdot(q, k, 1, 1) * decay
    o_intra = _bdot(scores, v, 1, 0)
    idx = _iota2((CHUNK, 1), 0).astype(F32)
    k_w = k * jnp.exp(lgc * (CHUNK - 1.0 - idx))
    kv = _bdot(k_w, v, 0, 0)
    s_new = state * jnp.exp(lgc * float(CHUNK)) + kv
    q_w = q * jnp.exp(lgc * (idx + 1.0))
    o = o_intra + _bdot(q_w, state, 1, 0)
    return _rms(o, w) * _silu(z), s_new


def _rope(t, cos2, sin2):
    return t * cos2 + pltpu.roll(t, RET_DK // 2, 1) * sin2


def _rope_t(g, cos2, sin2):
    return g * cos2 - pltpu.roll(g, RET_DK // 2, 1) * sin2


def _ret_pre(xv, cv):
    q, k, v, z = xv
    cos2, sin2 = cv
    return [_rope(q, cos2, sin2), _rope(k, cos2, sin2) * (RET_DK ** -0.5), v, z]


def _ret_post(dxs, cv):
    dq, dk, dv, dz = dxs
    cos2, sin2 = cv
    return [_rope_t(dq, cos2, sin2), _rope_t(dk, cos2, sin2) * (RET_DK ** -0.5), dv, dz]


def _gla_chunk(n, xs, state_t, cs, ws, ks):
    q, k, v, z, pre = xs
    (w,) = ws
    q = q * (GLA_DK ** -0.5)
    rowc = _iota2((CHUNK, 1), 0)
    valid = jnp.logical_or(n > 0, rowc >= PAD)
    log_a = jnp.where(valid, _log_sigmoid(pre) / GLA_TAU, 0.0)
    row, col = _iota2((CHUNK, CHUNK), 0), _iota2((CHUNK, CHUNK), 1)
    tri = (row >= col).astype(F32)
    b = jnp.dot(tri, log_a, precision=lax.Precision.HIGHEST, preferred_element_type=F32)
    b_last = b[CHUNK - 1:CHUNK, :]
    kv_t = _bdot(v, k * jnp.exp(b_last - b), 0, 0)
    s_new = state_t * jnp.exp(b_last) + kv_t
    o_inter = _bdot(q * jnp.exp(b), state_t, 1, 1)
    outs = []
    for s in range(CHUNK // SUB):
        lo, hi = s * SUB, (s + 1) * SUB
        b_ref = jnp.zeros_like(b_last) if s == 0 else b[lo - 1:lo, :]
        q_hat = q[lo:hi] * jnp.exp(b[lo:hi] - b_ref)
        k_hat = k[:hi] * jnp.exp(b_ref - b[:hi])
        sc = _bdot(q_hat, k_hat, 1, 1)
        causal = _iota2((SUB, hi), 0) + lo >= _iota2((SUB, hi), 1)
        outs.append(_bdot(jnp.where(causal, sc, 0.0), v[:hi], 1, 0))
    o = jnp.concatenate(outs, axis=0) + o_inter
    return _rms(o, w) * _silu(z), s_new


def _s5_disc(lam_re, lam_im, log_dt, b_re, b_im, expand):
    dt = jnp.exp(log_dt)
    mag = jnp.exp(lam_re * dt)
    ab_re, ab_im = mag * jnp.cos(lam_im * dt), mag * jnp.sin(lam_im * dt)
    den = lam_re * lam_re + lam_im * lam_im
    nr, ni = ab_re - 1.0, ab_im
    f_re = (nr * lam_re + ni * lam_im) / den
    f_im = (ni * lam_re - nr * lam_im) / den
    hp = lax.Precision.HIGHEST
    f_re = jnp.dot(f_re, expand, precision=hp, preferred_element_type=F32)
    f_im = jnp.dot(f_im, expand, precision=hp, preferred_element_type=F32)
    return ab_re, ab_im, f_re * b_re - f_im * b_im, f_re * b_im + f_im * b_re


def _s5_disc_fwd(args):
    def body(*refs):
        outs = _s5_disc(*[r[...] for r in refs[:6]])
        for r, v in zip(refs[6:], outs):
            r[...] = v

    g, p = args[0].shape
    return pl.pallas_call(
        body, name="s5_disc_fwd",
        out_shape=[jax.ShapeDtypeStruct((g, p), F32)] * 2 + [jax.ShapeDtypeStruct(args[3].shape, F32)] * 2,
    )(*args)


def _s5_disc_bwd(args, cts):
    def body(*refs):
        prim = [r[...] for r in refs[:5]]
        expand = refs[5][...]
        ct = tuple(r[...] for r in refs[6:10])
        _, vjp = jax.vjp(lambda *a: _s5_disc(*a, expand), *prim)
        for r, v in zip(refs[10:], vjp(ct)):
            r[...] = v

    return pl.pallas_call(
        body, name="s5_disc_bwd", out_shape=[jax.ShapeDtypeStruct(a.shape, F32) for a in args[:5]],
    )(*args, *cts)


S5_SUBL = 8
S5_LANES = S5_N // S5_SUBL
S5_TB = 64


def _s5_scan_fwd(bu, a_re, a_im):
    lp = bu.shape[0]

    def body(bu_ref, ar_ref, ai_ref, x_ref, st):
        @pl.when(pl.program_id(0) == 0)
        def _():
            st[...] = jnp.zeros_like(st)

        ar, ai = ar_ref[...], ai_ref[...]

        def step(t, carry):
            xr, xi = carry
            nr = ar * xr - ai * xi + bu_ref[t, 0:S5_SUBL, :]
            ni = ar * xi + ai * xr + bu_ref[t, S5_SUBL:2 * S5_SUBL, :]
            x_ref[t, 0:S5_SUBL, :] = nr
            x_ref[t, S5_SUBL:2 * S5_SUBL, :] = ni
            return nr, ni

        xr, xi = lax.fori_loop(0, S5_TB, step, (st[0], st[1]))
        st[0] = xr
        st[1] = xi

    blk = pl.BlockSpec((S5_TB, 2 * S5_SUBL, S5_LANES), lambda i: (i, 0, 0))
    cst = pl.BlockSpec((S5_SUBL, S5_LANES), lambda i: (0, 0))
    return pl.pallas_call(
        body, name="s5_scan_fwd", grid=(lp // S5_TB,), in_specs=[blk, cst, cst], out_specs=blk,
        out_shape=jax.ShapeDtypeStruct(bu.shape, F32),
        scratch_shapes=[pltpu.VMEM((2, S5_SUBL, S5_LANES), F32)],
        compiler_params=_params(("arbitrary",)),
    )(bu, a_re, a_im)


def _s5_scan_bwd(gx, x, a_re, a_im):
    lp = gx.shape[0]
    nb = lp // S5_TB

    def body(gx_ref, x_ref, xp_ref, ar_ref, ai_ref, g_ref, da_ref, st):
        i = pl.program_id(0)

        @pl.when(i == 0)
        def _():
            st[...] = jnp.zeros_like(st)
            da_ref[...] = jnp.zeros_like(da_ref)

        ar, ai = ar_ref[...], ai_ref[...]
        first = (i == nb - 1).astype(F32)

        def step(s, carry):
            gr, gi, dar, dai = carry
            t = S5_TB - 1 - s
            ngr = gx_ref[t, 0:S5_SUBL, :] + ar * gr + ai * gi
            ngi = gx_ref[t, S5_SUBL:2 * S5_SUBL, :] + ar * gi - ai * gr
            g_ref[t, 0:S5_SUBL, :] = ngr
            g_ref[t, S5_SUBL:2 * S5_SUBL, :] = ngi
            tp = jnp.maximum(t - 1, 0)
            at0 = (t == 0).astype(F32)
            keep = 1.0 - at0
            pr = keep * x_ref[tp, 0:S5_SUBL, :] + at0 * (1.0 - first) * xp_ref[0, 0:S5_SUBL, :]
            pi = keep * x_ref[tp, S5_SUBL:2 * S5_SUBL, :] + at0 * (1.0 - first) * xp_ref[0, S5_SUBL:2 * S5_SUBL, :]
            return ngr, ngi, dar + ngr * pr + ngi * pi, dai + ngi * pr - ngr * pi

        zero = jnp.zeros((S5_SUBL, S5_LANES), F32)
        gr, gi, dar, dai = lax.fori_loop(0, S5_TB, step, (st[0], st[1], zero, zero))
        st[0] = gr
        st[1] = gi
        da_ref[0] += dar
        da_ref[1] += dai

    rev = lambda i: nb - 1 - i
    blk = pl.BlockSpec((S5_TB, 2 * S5_SUBL, S5_LANES), lambda i: (rev(i), 0, 0))
    prev = pl.BlockSpec((1, 2 * S5_SUBL, S5_LANES), lambda i: (jnp.maximum(rev(i) * S5_TB - 1, 0), 0, 0))
    cst = pl.BlockSpec((S5_SUBL, S5_LANES), lambda i: (0, 0))
    return pl.pallas_call(
        body, name="s5_scan_bwd", grid=(nb,), in_specs=[blk, blk, prev, cst, cst],
        out_specs=[blk, pl.BlockSpec((2, S5_SUBL, S5_LANES), lambda i: (0, 0, 0))],
        out_shape=[jax.ShapeDtypeStruct(gx.shape, F32), jax.ShapeDtypeStruct((2, S5_SUBL, S5_LANES), F32)],
        scratch_shapes=[pltpu.VMEM((2, S5_SUBL, S5_LANES), F32)],
        compiler_params=_params(("arbitrary",)),
    )(gx, x, x, a_re, a_im)


def _place():
    x, y, c = lax.axis_index("x"), lax.axis_index("y"), lax.axis_index("c")
    return x, y, c, [(1 - x, y), (x, 1 - y), (1 - x, 1 - y)]


def _gather_phases():
    def plan(x_ref, out_ref, send_sems, recv_sems, local_sem):
        x, y, c, chips = _place()
        me, sibling = (x, y, c), (x, y, 1 - c)

        def rows(px, py, pc):
            return out_ref.at[4 * px + 2 * py + pc]

        def copy(k, block, to, src=None):
            return pltpu.make_async_remote_copy(
                src_ref=rows(*block) if src is None else src, dst_ref=rows(*block),
                send_sem=send_sems.at[k], recv_sem=recv_sems.at[k], device_id=to, device_id_type=MESH)

        mine = pltpu.make_async_copy(x_ref, rows(*me), local_sem)
        first = [copy(0, me, sibling, src=x_ref)]
        first += [copy(1 + j, me, (*chip, c), src=x_ref) for j, chip in enumerate(chips)]
        passed = [copy(4 + j, (*chip, c), sibling) for j, chip in enumerate(chips)]
        return c, chips, me, sibling, copy, mine, first, passed

    def start(ins, outs, sems):
        _, _, _, _, _, mine, first, _ = plan(ins[0], outs[0], *sems)
        mine.start()
        for cp in first:
            cp.start()

    def middle(ins, outs, sems):
        c, chips, me, _, copy, _, _, passed = plan(ins[0], outs[0], *sems)
        for j, chip in enumerate(chips):
            copy(1 + j, (*chip, c), me).wait_recv()
            passed[j].start()

    def finish(ins, outs, sems):
        c, chips, me, sibling, copy, mine, first, passed = plan(ins[0], outs[0], *sems)
        copy(0, sibling, me).wait_recv()
        for j, chip in enumerate(chips):
            copy(4 + j, (*chip, 1 - c), me).wait_recv()
        for cp in first + passed:
            cp.wait_send()
        mine.wait()

    return start, middle, finish


_GATHER_SEMS = [pltpu.SemaphoreType.DMA((7,)), pltpu.SemaphoreType.DMA((7,)), pltpu.SemaphoreType.DMA]


def _all_gather(shard, name):
    phases = _gather_phases()

    def body(x_ref, out_ref, *sems):
        for phase in phases:
            phase([x_ref], [out_ref], sems)

    return pl.pallas_call(
        body, name=name, out_shape=jax.ShapeDtypeStruct((N_DEV,) + shard.shape, shard.dtype),
        in_specs=[_ANY], out_specs=_ANY, scratch_shapes=list(_GATHER_SEMS),
    )(shard)


def _gather_hook(shard):
    start, middle, finish = _gather_phases()
    return _Hook([shard], [jax.ShapeDtypeStruct((N_DEV,) + shard.shape, shard.dtype)], _GATHER_SEMS,
                 [(0.0, start), (0.85, middle), (1.0, finish)])


def _swap_with_sibling(parts, name):
    def body(p_ref, out_ref, send_sems, recv_sems):
        x, y, c, _ = _place()
        copies = [pltpu.make_async_remote_copy(
            src_ref=p_ref.at[2 * chip + (1 - c)], dst_ref=out_ref.at[chip],
            send_sem=send_sems.at[chip], recv_sem=recv_sems.at[chip],
            device_id=(x, y, 1 - c), device_id_type=MESH) for chip in range(4)]
        for cp in copies:
            cp.start()
        for cp in copies:
            cp.wait()

    return pl.pallas_call(
        body, name=name, out_shape=jax.ShapeDtypeStruct((4,) + parts.shape[1:], parts.dtype),
        in_specs=[pl.BlockSpec(memory_space=pl.ANY)], out_specs=pl.BlockSpec(memory_space=pl.ANY),
        scratch_shapes=[pltpu.SemaphoreType.DMA((4,)), pltpu.SemaphoreType.DMA((4,))],
    )(parts)


def _chips_phases(lo, rows):
    def copies(p_ref, out_ref, send_sems, recv_sems):
        x, y, c, chips = _place()
        return [pltpu.make_async_remote_copy(
            src_ref=p_ref.at[2 * px + py, pl.ds(lo, rows)], dst_ref=out_ref.at[j],
            send_sem=send_sems.at[j], recv_sem=recv_sems.at[j],
            device_id=(px, py, c), device_id_type=MESH) for j, (px, py) in enumerate(chips)]

    def start(ins, outs, sems):
        for cp in copies(ins[0], outs[0], *sems):
            cp.start()

    def finish(ins, outs, sems):
        for cp in copies(ins[0], outs[0], *sems):
            cp.wait()

    return start, finish


def _chips_hook(parts, lo=0, hi=None):
    rows = (parts.shape[1] if hi is None else hi) - lo
    start, finish = _chips_phases(lo, rows)
    return _Hook([parts], [jax.ShapeDtypeStruct((3, rows) + parts.shape[2:], parts.dtype)],
                 [pltpu.SemaphoreType.DMA((3,)), pltpu.SemaphoreType.DMA((3,))], [(0.0, start), (1.0, finish)])


def _pack_rows(n_elem, row_mult=PACK_ROW_MULT):
    rows = -(-n_elem // PACK_COLS)
    return -(-rows // row_mult) * row_mult


def _pack(flats, dtype, row_mult=PACK_ROW_MULT):
    flat = jnp.concatenate([f.reshape(-1).astype(dtype) for f in flats])
    rows = _pack_rows(flat.shape[0], row_mult)
    return jnp.pad(flat, (0, rows * PACK_COLS - flat.shape[0])).reshape(rows, PACK_COLS)


def _unpack(buf, shapes):
    lead = buf.shape[:-2]
    flat = buf.reshape(lead + (-1,))
    outs, o = [], 0
    for s in shapes:
        n = math.prod(s)
        outs.append(flat[..., o:o + n].reshape(lead + tuple(s)))
        o += n
    return outs


BIG_LAYOUT = (("w_in_ab", D_MODEL, PACK_COLS), ("s5_w_glu", S5_W // N_DEV, PACK_COLS),
              ("w_out_ab", OUT_AB // N_DEV, 2 * PACK_COLS), ("w_in_c", D_MODEL, PACK_COLS),
              ("w_out_c", GLA_W // N_DEV, 2 * PACK_COLS))


def _to_rows(a):
    if a.shape[-1] == PACK_COLS:
        return a
    assert a.shape[-1] == 2 * PACK_COLS
    return jnp.concatenate([a[..., :PACK_COLS], a[..., PACK_COLS:]], axis=-2)


def _from_rows(p, cols):
    if cols == PACK_COLS:
        return p
    r = p.shape[-2] // 2
    return jnp.concatenate([p[..., :r, :], p[..., r:, :]], axis=-1)


FIRST_LAYOUT = BIG_LAYOUT[:1]
OTHER_LAYOUT = BIG_LAYOUT[1:3] + BIG_LAYOUT[4:]
GLU_AB_LAYOUT = BIG_LAYOUT[1:3]
IN_C_LAYOUT = BIG_LAYOUT[3:4]


def _pack_big(pieces, layout):
    return jnp.concatenate([_to_rows(pieces[name]) for name, _, _ in layout], axis=-2)


def _unpack_big(buf, layout):
    out, o = {}, 0
    for name, rows, cols in layout:
        r = rows * cols // PACK_COLS
        out[name] = _from_rows(buf[..., o:o + r, :], cols)
        o += r
    return out


def _column_windows(g):
    rows, quarter = g.shape[0], WIN_COLS // 4

    def body(g_ref, o_ref):
        o_ref[...] = g_ref[...]

    return pl.pallas_call(
        body, name="w_in_c_grad_windows", grid=(N_DEV, WIN_COLS // quarter),
        in_specs=[pl.BlockSpec((rows, quarter), lambda d, c: (0, (WIN_STEP // quarter) * d + c))],
        out_specs=pl.BlockSpec((None, rows, quarter), lambda d, c: (d, 0, c)),
        out_shape=jax.ShapeDtypeStruct((N_DEV, rows, WIN_COLS), g.dtype),
        compiler_params=_params(("arbitrary", "arbitrary")),
    )(g)


def _rows1024(a):
    r, c = a.shape
    if c > PACK_COLS:
        a = jnp.concatenate([a[:, i * PACK_COLS:(i + 1) * PACK_COLS] for i in range(c // PACK_COLS)], axis=0)
    elif c < PACK_COLS:
        a = jnp.pad(a, ((0, 0), (0, PACK_COLS - c)))
    return jnp.pad(a, ((0, -a.shape[0] % 8), (0, 0)))


def _unrows1024(p, r, c):
    if c > PACK_COLS:
        return jnp.concatenate([p[i * r:(i + 1) * r] for i in range(c // PACK_COLS)], axis=1)
    return p[:r, :c]


def _lane_select(a, off, sign, n_out, out_dtype, exact, name):
    rows, n_in = a.shape
    tr = _tile(rows, 256, 16)

    def body(off_ref, a_ref, o_ref):
        sel = _iota2((n_in, n_out), 0) + off_ref[0] * sign == _iota2((n_in, n_out), 1)
        if exact:
            r = jnp.dot(a_ref[...], sel.astype(F32), precision=lax.Precision.HIGHEST, preferred_element_type=F32)
        else:
            r = _dg(a_ref[...], sel.astype(BF16), 1, 0)
        o_ref[...] = r.astype(out_dtype)

    return pl.pallas_call(
        body, name=name, grid=(rows // tr,),
        in_specs=[pl.BlockSpec(memory_space=pltpu.SMEM), pl.BlockSpec((tr, n_in), lambda i: (i, 0))],
        out_specs=pl.BlockSpec((tr, n_out), lambda i: (i, 0)),
        out_shape=jax.ShapeDtypeStruct((rows, n_out), out_dtype),
        compiler_params=_params(("arbitrary",)),
    )(off, a)


def _adamw(w, g, m, v, name):
    rows, cols = w.shape
    tr = _tile(rows, 256, 8) if rows % 8 == 0 else rows

    def fn(i, w_, g_, m_, v_):
        m_new = ADAM_B1 * m_ + (1.0 - ADAM_B1) * g_
        v_new = ADAM_B2 * v_ + (1.0 - ADAM_B2) * (g_ * g_)
        m_hat = m_new / (1.0 - ADAM_B1 ** ADAM_STEP)
        v_hat = v_new / (1.0 - ADAM_B2 ** ADAM_STEP)
        delta = -ADAM_LR * (m_hat / (jnp.sqrt(v_hat) + ADAM_EPS) + ADAM_WD * w_)
        return (delta, m_new, v_new), ()

    outs, _ = _rows(fn, [_win(w), _win(g), _win(m), _win(v)], [], [(cols, F32)] * 3, [], name=name,
                    nrow=rows, tr=tr)
    return outs


def _as2d(a):
    if a.ndim == 1:
        return a.reshape(1, -1)
    if a.ndim == 2:
        return a
    a = a.reshape(a.shape[1:])
    return a if a.ndim == 2 else a.reshape(a.shape[0], -1)


def kernel(x, meta, norm_ab_w, w_in_ab, ret_norm_w, s5_lam_re, s5_lam_im, s5_log_dt, s5_b_re, s5_b_im, s5_c_re, s5_c_im, s5_d, s5_w_glu, w_out_ab, norm_c_w, w_in_c, gla_w_gate, gla_b_gate, gla_norm_w, w_out_c, final_norm_w, loss_target, m_meta, m_norm_ab_w, m_w_in_ab, m_ret_norm_w, m_s5_lam_re, m_s5_lam_im, m_s5_log_dt, m_s5_b_re, m_s5_b_im, m_s5_c_re, m_s5_c_im, m_s5_d, m_s5_w_glu, m_w_out_ab, m_norm_c_w, m_w_in_c, m_gla_w_gate, m_gla_b_gate, m_gla_norm_w, m_w_out_c, m_final_norm_w, v_meta, v_norm_ab_w, v_w_in_ab, v_ret_norm_w, v_s5_lam_re, v_s5_lam_im, v_s5_log_dt, v_s5_b_re, v_s5_b_im, v_s5_c_re, v_s5_c_im, v_s5_d, v_s5_w_glu, v_w_out_ab, v_norm_c_w, v_w_in_c, v_gla_w_gate, v_gla_b_gate, v_gla_norm_w, v_w_out_c, v_final_norm_w):
    weights = dict(meta=meta, norm_ab_w=norm_ab_w, w_in_ab=w_in_ab, ret_norm_w=ret_norm_w, s5_lam_re=s5_lam_re,
                   s5_lam_im=s5_lam_im, s5_log_dt=s5_log_dt, s5_b_re=s5_b_re, s5_b_im=s5_b_im, s5_c_re=s5_c_re,
                   s5_c_im=s5_c_im, s5_d=s5_d, s5_w_glu=s5_w_glu, w_out_ab=w_out_ab, norm_c_w=norm_c_w,
                   w_in_c=w_in_c, gla_w_gate=gla_w_gate, gla_b_gate=gla_b_gate, gla_norm_w=gla_norm_w,
                   w_out_c=w_out_c, final_norm_w=final_norm_w)
    mom_m = dict(meta=m_meta, norm_ab_w=m_norm_ab_w, w_in_ab=m_w_in_ab, ret_norm_w=m_ret_norm_w,
                 s5_lam_re=m_s5_lam_re, s5_lam_im=m_s5_lam_im, s5_log_dt=m_s5_log_dt, s5_b_re=m_s5_b_re,
                 s5_b_im=m_s5_b_im, s5_c_re=m_s5_c_re, s5_c_im=m_s5_c_im, s5_d=m_s5_d, s5_w_glu=m_s5_w_glu,
                 w_out_ab=m_w_out_ab, norm_c_w=m_norm_c_w, w_in_c=m_w_in_c, gla_w_gate=m_gla_w_gate,
                 gla_b_gate=m_gla_b_gate, gla_norm_w=m_gla_norm_w, w_out_c=m_w_out_c, final_norm_w=m_final_norm_w)
    mom_v = dict(meta=v_meta, norm_ab_w=v_norm_ab_w, w_in_ab=v_w_in_ab, ret_norm_w=v_ret_norm_w,
                 s5_lam_re=v_s5_lam_re, s5_lam_im=v_s5_lam_im, s5_log_dt=v_s5_log_dt, s5_b_re=v_s5_b_re,
                 s5_b_im=v_s5_b_im, s5_c_re=v_s5_c_re, s5_c_im=v_s5_c_im, s5_d=v_s5_d, s5_w_glu=v_s5_w_glu,
                 w_out_ab=v_w_out_ab, norm_c_w=v_norm_c_w, w_in_c=v_w_in_c, gla_w_gate=v_gla_w_gate,
                 gla_b_gate=v_gla_b_gate, gla_norm_w=v_gla_norm_w, w_out_c=v_w_out_c, final_norm_w=v_final_norm_w)
    order = list(weights)

    seq = x.shape[1]
    lp = CHUNK + seq
    nchunk = lp // CHUNK
    dev = 4 * lax.axis_index("x") + 2 * lax.axis_index("y") + lax.axis_index("c")
    core = lax.axis_index("c")
    chip = 2 * lax.axis_index("x") + lax.axis_index("y")

    win_off = jnp.reshape(2 * dev, (1,)).astype(jnp.int32)
    shard_c = jnp.pad(w_in_c[0].astype(BF16), ((0, 0), (0, 896 - SHARD_C)))
    big_shards = dict(w_in_ab=w_in_ab[0].astype(BF16), s5_w_glu=s5_w_glu[0].astype(BF16),
                      w_out_ab=w_out_ab[0].astype(BF16), w_out_c=w_out_c[0].astype(BF16),
                      w_in_c=_lane_select(shard_c, win_off, 1, WIN_COLS, BF16, False, "w_in_c_to_window"))
    def pad_to(a, rows, cols):
        return jnp.pad(a, ((0, rows - a.shape[0]), (0, cols - a.shape[1])))

    shard_w = D_MODEL // N_DEV
    small_pack = jnp.concatenate([meta, pad_to(norm_c_w, 8, shard_w), pad_to(gla_w_gate[0], GLA_RANK, shard_w),
                                  pad_to(gla_b_gate, 8, shard_w), pad_to(gla_norm_w, 8, shard_w)], axis=0)
    w_in_ab_g = _all_gather(big_shards["w_in_ab"], "gather_first")
    win_cut = 1408
    in_c_hook_a = _gather_hook(big_shards["w_in_c"][:win_cut])
    in_c_hook_b = _gather_hook(big_shards["w_in_c"][win_cut:])
    glu_ab_hook = _gather_hook(_pack_big(big_shards, GLU_AB_LAYOUT))
    out_c_hook = _gather_hook(_to_rows(big_shards["w_out_c"]))
    gs = _all_gather(small_pack, "gather_small")
    gate_w = GLA_QK // N_DEV
    s_meta, s_norm_c = gs[:, :N_META], gs[:, N_META]
    s_wgate, s_bgate, s_gnorm = gs[:, 24:24 + GLA_RANK, :gate_w], gs[:, 40, :gate_w], gs[:, 48]
    meta_f = s_meta.transpose(1, 0, 2).reshape(N_META, D_MODEL)
    norm_c_f = s_norm_c.reshape(1, D_MODEL)
    w_gate_f = jnp.pad(s_wgate.transpose(1, 0, 2).reshape(GLA_RANK, GLA_QK), ((0, GATE_PAD - GLA_RANK), (0, 0)))
    b_gate_f = s_bgate.reshape(1, GLA_QK)
    gla_norm_f = s_gnorm.reshape(GLA_H, 1, GLA_DV)

    pos = jnp.maximum(jnp.arange(lp, dtype=F32) - float(PAD), 0.0)
    inv_freq = jnp.power(ROPE_BASE, -jnp.arange(0, RET_DK, 2, dtype=F32) / RET_DK)
    ang = pos[:, None] * inv_freq[None, :]
    cos2 = jnp.concatenate([jnp.cos(ang), jnp.cos(ang)], axis=1)
    sin2 = jnp.concatenate([-jnp.sin(ang), jnp.sin(ang)], axis=1)
    log_g = jnp.log1p(-jnp.exp2(-5.0 - jnp.arange(RET_H, dtype=F32)))
    lg = jnp.broadcast_to(log_g[:, None, None], (RET_H, 1, 128))
    ret_norm_h = ret_norm_w.reshape(RET_H, 1, RET_DV)

    h0 = jnp.concatenate([jnp.zeros((PAD, D_MODEL), F32), meta_f, x[0]], axis=0)

    def rowmask(i):
        return (_iota2((CHUNK, 1), 0) + i * CHUNK) >= PAD

    (hn0,), _ = _rows(lambda i, h, w: ((_rms(h, w),), ()), [_win(h0)], [norm_ab_w], [(D_MODEL, BF16)], [],
                      name="norm_ab_fwd", nrow=lp)
    proj_ab, (w_in_c_ga,) = _mm(hn0, w_in_ab_g, "nn", name="in_ab_fwd", hook=in_c_hook_a, b_dev=True)

    q_off, k_off, v_off, za_off = 0, RET_QK, 2 * RET_QK, 2 * RET_QK + RET_W
    u_off, zb_off = 2 * RET_QK + 2 * RET_W, 2 * RET_QK + 2 * RET_W + S5_W
    ret_xs = [(proj_ab, RET_DK, lambda h: q_off // RET_DK + h), (proj_ab, RET_DK, lambda h: k_off // RET_DK + h),
              (proj_ab, RET_DV, lambda h: v_off // RET_DV + h), (proj_ab, RET_DV, lambda h: za_off // RET_DV + h)]
    ret_cs = [(cos2, RET_DK, lambda h: 0), (sin2, RET_DK, lambda h: 0)]
    ret_kw = dict(heads=RET_H, nchunk=nchunk, s_shape=(RET_DK, RET_DV), out_w=RET_DV, pre=_ret_pre, hpb=4)
    o_a, ret_sprev, (gathered_glu_ab,) = _scan_fwd(_ret_chunk, ret_xs, ret_cs, [ret_norm_h], [lg], name="ret_fwd",
                                                   hook=glu_ab_hook, **ret_kw)
    gb = _unpack_big(gathered_glu_ab, GLU_AB_LAYOUT)
    w_glu_f = gb["s5_w_glu"].reshape(S5_W, S5_W)
    w_out_ab_f = gb["w_out_ab"].reshape(OUT_AB, D_MODEL)

    expand = jnp.repeat(jnp.eye(S5_P, dtype=F32), S5_GH, axis=1)
    disc_args = (s5_lam_re[0], s5_lam_im[0], s5_log_dt[0].reshape(S5_G, 1),
                 s5_b_re[0].reshape(S5_G, S5_P * S5_GH), s5_b_im[0].reshape(S5_G, S5_P * S5_GH), expand)
    ab_re, ab_im, bb_re, bb_im = _s5_disc_fwd(disc_args)
    gt = S5_SUBL
    eye_t = jnp.eye(gt, dtype=F32)

    def tiles_in(bb):
        return jnp.einsum("sgph,gk->sghkp", bb.reshape(gt, gt, S5_P, S5_GH), eye_t).reshape(gt, 128, S5_LANES)

    def tiles_out(cc):
        return jnp.einsum("sghp,gk->sgpkh", cc.reshape(gt, gt, S5_GH, S5_P), eye_t).reshape(gt, S5_LANES, 128)

    wb_t = jnp.concatenate([tiles_in(bb_re), tiles_in(bb_im)], axis=0).astype(BF16)
    wc_t = jnp.concatenate([tiles_out(s5_c_re[0]), -tiles_out(s5_c_im[0])], axis=0).astype(BF16)
    a_re, a_im = ab_re.reshape(S5_SUBL, S5_LANES), ab_im.reshape(S5_SUBL, S5_LANES)
    tm5, tk5, nt5 = _tile(lp, 1408, 8), _tile(lp, 1408, 8), 2 * gt
    u_blk = u_off // 128
    wide = pl.BlockSpec((tm5, S5_LANES), lambda i, j, k: (i, j))
    wide_k = pl.BlockSpec((tm5, S5_LANES), lambda i, j, k: (i, k * gt + j))
    narrow = pl.BlockSpec((tm5, 128), lambda i, j, k: (i, j))
    wb_j = pl.BlockSpec((None, 128, S5_LANES), lambda i, j, k: (j, 0, 0))
    wc_j = pl.BlockSpec((None, S5_LANES, 128), lambda i, j, k: (j, 0, 0))
    wb_k = pl.BlockSpec((None, 128, S5_LANES), lambda i, j, k: (k * gt + j, 0, 0))
    wc_k = pl.BlockSpec((None, S5_LANES, 128), lambda i, j, k: (k * gt + j, 0, 0))
    wide_shape = jax.ShapeDtypeStruct((lp, 2 * S5_N), F32)
    bu, (w_out_c_g,) = _mm_core(proj_ab, wb_t, dims=NN, grid=(lp // tm5, nt5, 1), name="s5_bu",
                                a_spec=pl.BlockSpec((tm5, 128), lambda i, j, k: (i, u_blk + j % gt)), b_spec=wb_j,
                                o_spec=wide, out_shape=wide_shape, acc_shape=(tm5, S5_LANES), hook=out_c_hook)
    w_out_c_f = _from_rows(w_out_c_g, D_MODEL).reshape(GLA_W, D_MODEL)
    xs5 = _s5_scan_fwd(bu.reshape(lp, 2 * S5_SUBL, S5_LANES), a_re, a_im)
    xs5_2d = xs5.reshape(lp, 2 * S5_N)
    y_pre, (w_in_c_gb,) = _mm_core(xs5_2d, wc_t, dims=NN, grid=(lp // tm5, gt, 2), name="s5_cx", a_spec=wide_k,
                                   b_spec=wc_k, o_spec=narrow, out_shape=jax.ShapeDtypeStruct((lp, S5_W), F32),
                                   acc_shape=(tm5, 128), hook=in_c_hook_b)
    (y_s5, yg_bf), _ = _rows(
        lambda i, yp, u, d: ((yp + d * u, _gelu(yp + d * u)), ()),
        [_win(y_pre), _win(proj_ab, u_off, S5_W)], [s5_d], [(S5_W, F32), (S5_W, BF16)], [], name="s5_gelu_fwd", nrow=lp)
    t_glu = _mm(yg_bf, w_glu_f, "nn", name="s5_glu_fwd")

    def s5_gate(y, t, zb):
        return _gelu(y) * _sigmoid(t) * _silu(zb)

    (o_b,), _ = _rows(lambda i, y, t, zb: ((s5_gate(y, t, zb),), ()),
                      [_win(y_s5), _win(t_glu), _win(proj_ab, zb_off, S5_W)], [], [(S5_W, BF16)], [],
                      name="s5_gate_fwd", nrow=lp)
    o_ab = jnp.concatenate([o_a, o_b], axis=1)
    h1 = _mm(o_ab, w_out_ab_f, "nn", name="out_ab_fwd", add=h0)
    w_in_c_g = jnp.concatenate([w_in_c_ga, w_in_c_gb], axis=1)
    w_in_c_f = sum(jnp.pad(w_in_c_g[d], ((0, 0), (WIN_STEP * d, IN_C_PAD - WIN_STEP * d - WIN_COLS)))
                   for d in range(N_DEV))

    (hn1,), _ = _rows(lambda i, h, w: ((_rms(h, w),), ()), [_win(h1)], [norm_c_f], [(D_MODEL, BF16)], [],
                      name="norm_c_fwd", nrow=lp)
    proj_c = _mm(hn1, w_in_c_f, "nn", name="in_c_fwd")
    gl_off = 2 * GLA_QK + 2 * GLA_W
    pre_gate = _mm(proj_c, w_gate_f, "nn", name="gate_fwd", a_win=(gl_off, GATE_PAD), bias=b_gate_f)
    gla_xs = [(proj_c, GLA_DK, lambda h: h), (proj_c, GLA_DK, lambda h: GLA_QK // GLA_DK + h),
              (proj_c, GLA_DV, lambda h: 2 * GLA_QK // GLA_DV + h),
              (proj_c, GLA_DV, lambda h: (2 * GLA_QK + GLA_W) // GLA_DV + h),
              (pre_gate, GLA_DK, lambda h: h)]
    gla_kw = dict(heads=GLA_H, nchunk=nchunk, s_shape=(GLA_DV, GLA_DK), out_w=GLA_DV, hpb=GLA_H)
    o_c, gla_sprev = _scan_fwd(_gla_chunk, gla_xs, [], [gla_norm_f], [], name="gla_fwd", **gla_kw)
    h2 = _mm(o_c, w_out_c_f, "nn", name="out_c_fwd", add=h1)

    fnw = final_norm_w.reshape(1, D_MODEL)

    def final_fn(i, h, tgt, w):
        def loss_of(h_, w_):
            err = _rms(h_, w_) - tgt
            return 0.5 * jnp.sum(jnp.mean(err * err, axis=-1))

        real = (i > 0).astype(F32)
        loss_i, (dh, dw) = jax.value_and_grad(loss_of, argnums=(0, 1))(h, w)
        return (dh * real, dh * real), (jnp.full((1, 128), loss_i * real, F32), dw * real)

    (dh2, dh2_bf), (loss_acc, g_final) = _rows(
        final_fn, [_win(h2), _win(loss_target[0], roff=1)], [fnw], [(D_MODEL, F32), (D_MODEL, BF16)],
        [(1, 128), (1, D_MODEL)], name="final_loss", nrow=lp)

    def rs_front(pieces, layout, tag):
        g_full = _pack_big(pieces, layout)
        prow = g_full.shape[1]
        from_sibling = _swap_with_sibling(g_full, "rs_sibling_" + tag)
        mine_by_chip = lax.dynamic_index_in_dim(g_full.reshape(4, 2, prow, PACK_COLS), core, axis=1, keepdims=False)
        (p1, p1_bf), _ = _rows(
            lambda i, a, b: ((a.astype(F32) + b.astype(F32), a.astype(F32) + b.astype(F32)), ()),
            [_win(mine_by_chip.reshape(4 * prow, PACK_COLS)), _win(from_sibling.reshape(4 * prow, PACK_COLS))], [],
            [(PACK_COLS, F32), (PACK_COLS, BF16)], [], name="rs_sum_sibling_" + tag, nrow=4 * prow,
            tr=_tile(prow, 512, 16))
        return p1.reshape(4, prow, PACK_COLS), p1_bf.reshape(4, prow, PACK_COLS)

    def rs_back(p1, from_chips, layout, tag):
        prow = p1.shape[1]
        tr = _tile(prow, 512, 16)
        own = lax.dynamic_index_in_dim(p1, chip, axis=0, keepdims=False)
        fc2 = from_chips.reshape(3 * prow, PACK_COLS)
        nblk = prow // tr
        (g_shard,), _ = _rows(
            lambda i, a, b0, b1, b2: ((((a + b0.astype(F32)) + b1.astype(F32)) + b2.astype(F32),), ()),
            [_win(own), _win(fc2), _win(fc2, roff=-nblk), _win(fc2, roff=-2 * nblk)], [], [(PACK_COLS, F32)], [],
            name="rs_sum_chips_" + tag, nrow=prow, tr=tr)
        return _unpack_big(g_shard, layout)

    do_c = _mm(dh2_bf, w_out_c_f, "nt", name="out_c_dx", out_dtype=BF16)
    gw_out_c = _mm(o_c, dh2_bf, "tn", name="out_c_dw", out_dtype=BF16)
    (dq_c, dk_c, dv_c, dz_c, dpre), (g_gla_norm,) = _scan_bwd(
        _gla_chunk, gla_xs, [], [gla_norm_f], [], do_c, gla_sprev, name="gla_bwd", **gla_kw)
    dglow = _mm(dpre, w_gate_f, "nt", name="gate_dx", out_dtype=BF16)
    g_wgate = _mm(proj_c, dpre, "tn", name="gate_dw", a_win=(gl_off, GATE_PAD))[:GLA_RANK]
    (), (g_bgate,) = _rows(lambda i, d: ((), (jnp.sum(d.astype(F32), axis=0, keepdims=True),)), [_win(dpre)], [], [],
                           [(1, GLA_QK)], name="gate_db", nrow=lp)
    dproj_c = jnp.concatenate([dq_c, dk_c, dv_c, dz_c, dglow], axis=1)
    dhn1 = _mm(dproj_c, w_in_c_f, "nt", name="in_c_dx")
    gw_in_c = _mm(hn1, dproj_c, "tn", name="in_c_dw", out_dtype=BF16)
    p1_c, p1_c_bf = rs_front(dict(
        w_in_c=_column_windows(gw_in_c)),
        IN_C_LAYOUT, "in_c")

    def norm_bwd(i, h, dhn, dres, w):
        _, vjp = jax.vjp(_rms, h, w)
        dh, dw = vjp(dhn)
        return (jnp.where(rowmask(i), dh + dres, 0.0),), (dw,)

    def norm_bwd_both(i, h, dhn, dres, w):
        (dh,), acc = norm_bwd(i, h, dhn, dres, w)
        return (dh, dh), acc

    (dh1, dh1_bf), (g_norm_c,) = _rows(norm_bwd_both, [_win(h1), _win(dhn1), _win(dh2)], [norm_c_f],
                                       [(D_MODEL, F32), (D_MODEL, BF16)], [(1, D_MODEL)], name="norm_c_bwd", nrow=lp)

    do_ab = _mm(dh1_bf, w_out_ab_f, "nt", name="out_ab_dx", out_dtype=BF16)
    gw_out_ab = _mm(o_ab, dh1_bf, "tn", name="out_ab_dw", out_dtype=BF16)

    def s5_gate_bwd(i, dob, y, t, zb):
        _, vjp = jax.vjp(s5_gate, y, t, zb)
        dy, dt, dzb = vjp(dob.astype(F32))
        return (dy, dt, dzb), ()

    (dy_a, dt_glu, dzb), _ = _rows(
        s5_gate_bwd, [_win(do_ab, RET_W, S5_W), _win(y_s5), _win(t_glu), _win(proj_ab, zb_off, S5_W)], [],
        [(S5_W, F32), (S5_W, BF16), (S5_W, BF16)], [], name="s5_gate_bwd", nrow=lp)
    dyg2 = _mm(dt_glu, w_glu_f, "nt", name="s5_glu_dx")
    gw_glu = _mm(yg_bf, dt_glu, "tn", name="s5_glu_dw", out_dtype=BF16)

    def s5_y_bwd(i, dya, dyg, y, u, d):
        _, vjp = jax.vjp(_gelu, y)
        (dy_g,) = vjp(dyg)
        dy = dya + dy_g
        return (dy, d * dy), (jnp.sum(dy * u, axis=0, keepdims=True),)

    (dy_s5, du1), (g_d,) = _rows(
        s5_y_bwd, [_win(dy_a), _win(dyg2), _win(y_s5), _win(proj_ab, u_off, S5_W)], [s5_d],
        [(S5_W, BF16), (S5_W, F32)], [(1, S5_W)], name="s5_y_bwd", nrow=lp)
    p1_o, p1_o_bf = rs_front(dict(s5_w_glu=gw_glu.reshape(N_DEV, S5_W // N_DEV, S5_W),
                                  w_out_ab=gw_out_ab.reshape(N_DEV, OUT_AB // N_DEV, D_MODEL),
                                  w_out_c=gw_out_c.reshape(N_DEV, GLA_W // N_DEV, D_MODEL)), OTHER_LAYOUT, "other")
    o_cut = 640
    gx, (from_chips_oa,) = _mm_core(dy_s5, wc_t, dims=NT, grid=(lp // tm5, nt5, 1), name="s5_cx_dx",
                                    a_spec=pl.BlockSpec((tm5, 128), lambda i, j, k: (i, j % gt)), b_spec=wc_j,
                                    o_spec=wide, out_shape=wide_shape, acc_shape=(tm5, S5_LANES),
                                    hook=_chips_hook(p1_o_bf, 0, o_cut))
    rows_k = lambda col: pl.BlockSpec((tk5, col), lambda i, j, k: (k, i))
    gwc = _mm_core(xs5_2d, dy_s5, dims=TN, grid=(nt5, 1, lp // tk5), name="s5_cx_dw", a_spec=rows_k(S5_LANES),
                   b_spec=pl.BlockSpec((tk5, 128), lambda i, j, k: (k, i % gt)),
                   o_spec=pl.BlockSpec((None, S5_LANES, 128), lambda i, j, k: (i, 0, 0)),
                   out_shape=jax.ShapeDtypeStruct((nt5, S5_LANES, 128), F32), acc_shape=(S5_LANES, 128))
    g_s5, da = _s5_scan_bwd(gx.reshape(lp, 2 * S5_SUBL, S5_LANES), xs5, a_re, a_im)
    g_s5_2d = g_s5.reshape(lp, 2 * S5_N)
    du, (from_chips_ob,) = _mm_core(g_s5_2d, wb_t, dims=NT, grid=(lp // tm5, gt, 2), name="s5_bu_dx", a_spec=wide_k,
                                    b_spec=wb_k, o_spec=narrow, out_shape=jax.ShapeDtypeStruct((lp, S5_W), BF16),
                                    acc_shape=(tm5, 128), extra=[(du1, narrow)],
                                    hook=_chips_hook(p1_o_bf, o_cut, None))
    from_chips_o = jnp.concatenate([from_chips_oa, from_chips_ob], axis=1)
    gwb = _mm_core(proj_ab, g_s5_2d, dims=TN, grid=(nt5, 1, lp // tk5), name="s5_bu_dw",
                   a_spec=pl.BlockSpec((tk5, 128), lambda i, j, k: (k, u_blk + i % gt)), b_spec=rows_k(S5_LANES),
                   o_spec=pl.BlockSpec((None, 128, S5_LANES), lambda i, j, k: (i, 0, 0)),
                   out_shape=jax.ShapeDtypeStruct((nt5, 128, S5_LANES), F32), acc_shape=(128, S5_LANES))
    gwc6 = gwc.reshape(2, gt, gt, S5_P, gt, S5_GH)
    g_c = jnp.einsum("rsgpgh->rsghp", gwc6).reshape(2, S5_G, S5_GH, S5_P)
    g_c_re, g_c_im = g_c[0], -g_c[1]
    gwb6 = gwb.reshape(2, gt, gt, S5_GH, gt, S5_P)
    d_bb = jnp.einsum("rsghgp->rsgph", gwb6).reshape(2, S5_G, S5_P * S5_GH)
    d_bb_re, d_bb_im = d_bb[0], d_bb[1]
    g_lam_re, g_lam_im, g_log_dt, g_b_re, g_b_im = _s5_disc_bwd(
        disc_args, (da[0].reshape(S5_G, S5_P), da[1].reshape(S5_G, S5_P), d_bb_re, d_bb_im))

    (dq_a, dk_a, dv_a, dz_a), (g_ret_norm,), (from_chips_c,) = _scan_bwd(
        _ret_chunk, ret_xs, ret_cs, [ret_norm_h], [lg], do_ab, ret_sprev, name="ret_bwd", post=_ret_post,
        hook=_chips_hook(p1_c_bf), **ret_kw)
    dproj_ab = jnp.concatenate([dq_a, dk_a, dv_a, dz_a, du, dzb], axis=1)

    lane = lambda a_: pad_to(a_, a_.shape[0], 128)

    def sum8(i, *blocks):
        acc = blocks[0]
        for b in blocks[1:]:
            acc = acc + b
        return (acc,), ()

    def pack_small(pieces):
        return jnp.concatenate([_rows1024(p) for _, p in pieces], axis=0)

    def sum_small(gathered, pieces, tag):
        srow = gathered.shape[1]
        tr = _tile(srow, 128, 8)
        flat = gathered.reshape(N_DEV * srow, PACK_COLS)
        (total,), _ = _rows(sum8, [_win(flat, roff=-d * (srow // tr)) for d in range(N_DEV)], [], [(PACK_COLS, F32)],
                            [], name="sum_small_" + tag, nrow=srow, tr=tr)
        out, o = {}, 0
        for name_, p in pieces:
            r8 = _rows1024(p).shape[0]
            out[name_] = _unrows1024(total[o:o + r8], *p.shape)
            o += r8
        return out

    early_pieces = [
        ("vec2048", jnp.concatenate([g_final, g_norm_c], axis=0)),
        ("vec1024", jnp.concatenate([g_d, g_bgate, pad_to(loss_acc[:, :1], 1, PACK_COLS)], axis=0)),
        ("lam3", jnp.concatenate([lane(g_lam_re), lane(g_lam_im), lane(g_log_dt)], axis=1)),
        ("s5_b_re", g_b_re), ("s5_b_im", g_b_im),
        ("s5_c_re", g_c_re.reshape(S5_G, S5_GH * S5_P)), ("s5_c_im", g_c_im.reshape(S5_G, S5_GH * S5_P)),
        ("ret_norm_w", g_ret_norm.reshape(RET_H, RET_DV)), ("gla_norm_w", g_gla_norm.reshape(GLA_H, GLA_DV)),
        ("gla_w_gate", g_wgate)]
    gw_in_ab, (early_all,) = _mm(
        hn0, dproj_ab, "tn", name="in_ab_dw", out_dest=True, out_dtype=BF16,
        hook=_gather_hook(pack_small(early_pieces)))
    p1_first, p1_first_bf = rs_front(dict(w_in_ab=gw_in_ab), FIRST_LAYOUT, "first")
    dhn0, (from_chips_first,) = _mm(dproj_ab, w_in_ab_g, "nt", name="in_ab_dx", b_dev=True,
                                    hook=_chips_hook(p1_first_bf))
    def norm_bwd_first(i, h, dhn, dres, w):
        (dh,), (dw,) = norm_bwd(i, h, dhn, dres, w)
        return (dh,), (dw, dh * (i == 0).astype(F32))

    (grad_x2d,), (g_norm_ab, dh0_first) = _rows(
        norm_bwd_first, [_win(h0), _win(dhn0), _win(dh1)], [norm_ab_w], [(D_MODEL, F32, 1)],
        [(1, D_MODEL), (CHUNK, D_MODEL)], name="norm_ab_bwd", nrow=lp)
    grad_x = grad_x2d[None]
    late_pieces = [("norm_ab_w", g_norm_ab), ("meta", dh0_first[PAD:CHUNK])]
    small = sum_small(early_all, early_pieces, "early")
    small.update(sum_small(_all_gather(pack_small(late_pieces), "gather_grads"), late_pieces, "late"))

    big_grads = {**rs_back(p1_c, from_chips_c, IN_C_LAYOUT, "in_c"), **rs_back(p1_o, from_chips_o, OTHER_LAYOUT, "other"),
                 **rs_back(p1_first, from_chips_first, FIRST_LAYOUT, "first")}
    big_grads["w_in_c"] = _lane_select(big_grads["w_in_c"], win_off, -1, 896, F32, True,
                                       "w_in_c_from_window")[:, :SHARD_C]
    small["final_norm_w"], small["norm_c_w"] = small["vec2048"][0:1], small["vec2048"][1:2]
    small["s5_d"], small["gla_b_gate"] = small["vec1024"][0:1], small["vec1024"][1:2]
    loss = small["vec1024"][2, 0]
    small["s5_lam_re"], small["s5_lam_im"] = small["lam3"][:, :S5_P], small["lam3"][:, 128:128 + S5_P]
    small["s5_log_dt"] = small["lam3"][:, 256:257]

    def my_cols(g, n):
        return lax.dynamic_slice_in_dim(g, dev * n, n, axis=g.ndim - 1)

    grads = dict(
        meta=my_cols(small["meta"], D_MODEL // N_DEV),
        norm_ab_w=small["norm_ab_w"], w_in_ab=big_grads["w_in_ab"][None], ret_norm_w=small["ret_norm_w"].reshape(1, RET_W),
        s5_lam_re=small["s5_lam_re"][None], s5_lam_im=small["s5_lam_im"][None],
        s5_log_dt=small["s5_log_dt"].reshape(1, S5_G),
        s5_b_re=small["s5_b_re"].reshape(1, S5_G, S5_P, S5_GH), s5_b_im=small["s5_b_im"].reshape(1, S5_G, S5_P, S5_GH),
        s5_c_re=small["s5_c_re"][None], s5_c_im=small["s5_c_im"][None], s5_d=small["s5_d"],
        s5_w_glu=big_grads["s5_w_glu"][None], w_out_ab=big_grads["w_out_ab"][None],
        norm_c_w=my_cols(small["norm_c_w"], D_MODEL // N_DEV), w_in_c=big_grads["w_in_c"][None],
        gla_w_gate=my_cols(small["gla_w_gate"], GLA_QK // N_DEV)[None],
        gla_b_gate=my_cols(small["gla_b_gate"], GLA_QK // N_DEV),
        gla_norm_w=my_cols(small["gla_norm_w"].reshape(1, GLA_W), GLA_W // N_DEV),
        w_out_c=big_grads["w_out_c"][None], final_norm_w=small["final_norm_w"].reshape(D_MODEL))

    deltas, new_m, new_v = {}, {}, {}
    for k in order:
        w = weights[k]
        d2, m2, v2 = _adamw(_as2d(w), _as2d(grads[k].reshape(w.shape)), _as2d(mom_m[k]), _as2d(mom_v[k]), "adamw_" + k)
        deltas[k], new_m[k], new_v[k] = d2.reshape(w.shape), m2.reshape(w.shape), v2.reshape(w.shape)
        grads[k] = grads[k].reshape(w.shape)

    return (loss, grad_x, *[grads[k] for k in order], *[deltas[k] for k in order],
            *[new_m[k] for k in order], *[new_v[k] for k in order])
```

```python
import functools
import math

import jax
import jax.numpy as jnp
from jax import lax
from jax.experimental import pallas as pl
from jax.experimental.pallas import tpu as pltpu

F32, BF16 = jnp.float32, jnp.bfloat16
MESH = pl.DeviceIdType.MESH
N_DEV = 8

D_MODEL = 2048
CHUNK = 128
N_META = 16
PAD = CHUNK - N_META
SUB = 16
EPS = 1e-6
RET_H, RET_DK, RET_DV = 8, 128, 256
RET_QK, RET_W = RET_H * RET_DK, RET_H * RET_DV
ROPE_BASE = 10000.0
S5_W, S5_G, S5_P, S5_GH = 1024, 64, 64, 16
S5_N = S5_G * S5_P
GLA_H, GLA_DK, GLA_DV, GLA_RANK, GLA_TAU = 4, 256, 512, 16, 16.0
GLA_QK, GLA_W = GLA_H * GLA_DK, GLA_H * GLA_DV
IN_AB = 2 * RET_QK + 2 * RET_W + 2 * S5_W
OUT_AB = RET_W + S5_W
IN_C = 2 * GLA_QK + 2 * GLA_W + GLA_RANK
GATE_PAD = 256
IN_C_PAD = 2 * GLA_QK + 2 * GLA_W + GATE_PAD
ADAM_LR, ADAM_B1, ADAM_B2, ADAM_EPS, ADAM_WD, ADAM_STEP = 0.001, 0.9, 0.999, 1e-08, 0.01, 10

VMEM_LIMIT_BYTES = 48 * 2 ** 20
PACK_COLS = 1024
PACK_ROW_MULT = 8
SHARD_C = IN_C // N_DEV
WIN_STEP = 768
WIN_COLS = 1024


def _params(sem):
    return pltpu.CompilerParams(dimension_semantics=sem, vmem_limit_bytes=VMEM_LIMIT_BYTES)


def _tile(n, cap, mult):
    best = None
    for t in range(mult, min(n, cap) + 1, mult):
        if n % t == 0:
            best = t
    assert best is not None, (n, cap, mult)
    return best


def _dg(a, b, ca, cb):
    return lax.dot_general(a.astype(BF16), b.astype(BF16), (((ca,), (cb,)), ((), ())),
                           preferred_element_type=F32)


@functools.partial(jax.custom_vjp, nondiff_argnums=(2, 3))
def _bdot(a, b, ca, cb):
    return _dg(a, b, ca, cb)


def _bdot_fwd(a, b, ca, cb):
    return _dg(a, b, ca, cb), (a, b)


def _bdot_bwd(ca, cb, res, g):
    a, b = res
    da = _dg(g, b, 1, 1 - cb) if ca == 1 else _dg(b, g, 1 - cb, 1)
    db = _dg(a, g, 1 - ca, 0) if cb == 0 else _dg(g, a, 0, 1 - ca)
    return da.astype(a.dtype), db.astype(b.dtype)


_bdot.defvjp(_bdot_fwd, _bdot_bwd)


def _sigmoid(x):
    return 1.0 / (1.0 + jnp.exp(-x))


def _silu(x):
    return x * _sigmoid(x)


def _log_sigmoid(x):
    return jnp.minimum(x, 0.0) - jnp.log(1.0 + jnp.exp(-jnp.abs(x)))


def _gelu(x):
    return 0.5 * x * (1.0 + jnp.tanh(math.sqrt(2.0 / math.pi) * (x + 0.044715 * (x * x * x))))


def _rms(x, w):
    return x * lax.rsqrt(jnp.mean(x * x, axis=-1, keepdims=True) + EPS) * w


class _Hook:
    def __init__(self, ins, outs, sems, phases):
        self.ins, self.outs, self.sems, self.phases = list(ins), list(outs), list(sems), list(phases)


def _merge_hooks(first, second):
    ni, no, ns = len(first.ins), len(first.outs), len(first.sems)
    phases = [(f, lambda i, o, s, fn=fn: fn(i[:ni], o[:no], s[:ns])) for f, fn in first.phases]
    phases += [(f, lambda i, o, s, fn=fn: fn(i[ni:], o[no:], s[ns:])) for f, fn in second.phases]
    return _Hook(first.ins + second.ins, first.outs + second.outs, first.sems + second.sems,
                 sorted(phases, key=lambda p: p[0]))


_NO_HOOK = _Hook([], [], [], [])
_ANY = pl.BlockSpec(memory_space=pl.ANY)


def _run_hook(hook, lin, total, in_refs, out_refs, sem_refs):
    for frac, fn in hook.phases:
        at = min(int(frac * total), total - 1)

        @pl.when(lin == at)
        def _(fn=fn):
            fn(in_refs, out_refs, sem_refs)


def _mm_core(a, b, *, dims, grid, a_spec, b_spec, o_spec, out_shape, acc_shape, name, extra=(), hook=None,
             b_parts=0):
    nk = grid[2]
    n_extra = len(extra)
    hook = _NO_HOOK if hook is None else hook
    hi, ho = len(hook.ins), len(hook.outs)

    def body(*refs):
        a_ref, b_ref = refs[0], refs[1]
        o_ref, acc = refs[2 + n_extra + hi], refs[3 + n_extra + hi + ho]
        k = pl.program_id(2)
        lin = (pl.program_id(0) * grid[1] + pl.program_id(1)) * nk + k
        _run_hook(hook, lin, grid[0] * grid[1] * nk, refs[2 + n_extra:2 + n_extra + hi],
                  refs[3 + n_extra + hi:3 + n_extra + hi + ho], refs[4 + n_extra + hi + ho:])

        if b_parts:
            part = sum(lax.dot_general(a_ref[:, d * PACK_COLS:(d + 1) * PACK_COLS].astype(BF16), b_ref[d].astype(BF16),
                                       dims, preferred_element_type=F32) for d in range(b_parts))
        else:
            part = lax.dot_general(a_ref[...].astype(BF16), b_ref[...].astype(BF16), dims, preferred_element_type=F32)

        def finish(r):
            for e in range(n_extra):
                r = r + refs[2 + e][...].astype(F32)
            o_ref[...] = r.astype(o_ref.dtype)

        if nk == 1:
            finish(part)
        else:
            @pl.when(k == 0)
            def _():
                acc[...] = part

            @pl.when(k > 0)
            def _():
                acc[...] += part

            @pl.when(k == nk - 1)
            def _():
                finish(acc[...])

    res = pl.pallas_call(
        body, name=name, grid=grid,
        in_specs=[a_spec, b_spec] + [sp for _, sp in extra] + [_ANY] * hi,
        out_specs=[o_spec] + [_ANY] * ho, out_shape=[out_shape] + hook.outs,
        scratch_shapes=[pltpu.VMEM(acc_shape if nk > 1 else (8, 128), F32)] + hook.sems,
        compiler_params=_params(("arbitrary", "arbitrary", "arbitrary")),
    )(a, b, *[arr for arr, _ in extra], *hook.ins)
    return res[0] if hook is _NO_HOOK else (res[0], res[1:])


NN, NT, TN = (((1,), (0,)), ((), ())), (((1,), (1,)), ((), ())), (((0,), (0,)), ((), ()))


FULL_K = 2048


def _mm(a, b, mode, *, name, out_dtype=F32, a_win=None, add=None, bias=None, hook=None, b_dev=False,
        out_dest=False):
    b_parts = 0
    if mode == "tn":
        kdim, n = a.shape[0], b.shape[1]
        m = a.shape[1] if a_win is None else a_win[1]
        tm, tn, tk = _tile(m, 512, 128), _tile(n, 640, 128), kdim
        off = 0 if a_win is None else a_win[0] // tm
        a_spec = pl.BlockSpec((tk, tm), lambda i, j, k: (k, i + off))
        b_spec = pl.BlockSpec((tk, tn), lambda i, j, k: (k, j))
        dims = TN
    else:
        m = a.shape[0]
        kdim = a.shape[1] if a_win is None else a_win[1]
        if b_dev:
            n = b.shape[0] * b.shape[2] if mode == "nn" else b.shape[1]
        else:
            n = b.shape[1] if mode == "nn" else b.shape[0]
        if FULL_K < kdim <= 2 * FULL_K and not b_dev:
            tm, tn, tk = _tile(m, 1408, 8), _tile(n, 1024, 128), _tile(kdim, 1024, 128)
        else:
            tm = _tile(m, 1408 if kdim <= FULL_K else 352, 8)
            tn, tk = _tile(n, 640, 128), kdim
        off = 0 if a_win is None else a_win[0] // tk
        a_spec = pl.BlockSpec((tm, tk), lambda i, j, k: (i, k + off))
        if mode == "nn":
            dims = NN
            if b_dev:
                per = PACK_COLS // tn
                b_spec = pl.BlockSpec((None, tk, tn), lambda i, j, k: (j // per, k, j % per))
            else:
                b_spec = pl.BlockSpec((tk, tn), lambda i, j, k: (k, j))
        else:
            dims = NT
            if b_dev:
                b_parts = kdim // PACK_COLS
                b_spec = pl.BlockSpec((b_parts, tn, PACK_COLS), lambda i, j, k: (0, j, 0))
            else:
                b_spec = pl.BlockSpec((tn, tk), lambda i, j, k: (j, k))
    if a_win is not None:
        assert a_win[0] % (tm if mode == "tn" else tk) == 0
    extra = []
    if add is not None:
        extra.append((add, pl.BlockSpec((tm, tn), lambda i, j, k: (i, j))))
    if bias is not None:
        extra.append((bias, pl.BlockSpec((1, tn), lambda i, j, k: (0, j))))
    if out_dest:
        per = PACK_COLS // tn
        o_spec = pl.BlockSpec((None, tm, tn), lambda i, j, k: (j // per, i, j % per))
        out_shape = jax.ShapeDtypeStruct((n // PACK_COLS, m, PACK_COLS), out_dtype)
    else:
        o_spec = pl.BlockSpec((tm, tn), lambda i, j, k: (i, j))
        out_shape = jax.ShapeDtypeStruct((m, n), out_dtype)
    return _mm_core(a, b, dims=dims, grid=(m // tm, n // tn, kdim // tk), a_spec=a_spec, b_spec=b_spec,
                    o_spec=o_spec, out_shape=out_shape, acc_shape=(tm, tn), name=name, extra=extra, hook=hook,
                    b_parts=b_parts)


def _win(arr, col0=0, width=None, roff=0):
    return (arr, col0, arr.shape[1] if width is None else width, roff)


def _rows(fn, rows, consts, outs, accs, *, name, nrow, tr=CHUNK):
    nr, nc, no = len(rows), len(consts), len(outs)

    def body(*refs):
        i = pl.program_id(0)
        ins = [r[...] for r in refs[:nr + nc]]
        o_refs = refs[nr + nc:nr + nc + no]
        a_refs = refs[nr + nc + no:]
        res_o, res_a = fn(i, *ins)
        for r, v in zip(o_refs, res_o):
            r[...] = v.astype(r.dtype)
        if a_refs:
            @pl.when(i == 0)
            def _():
                for r in a_refs:
                    r[...] = jnp.zeros_like(r)

            for r, v in zip(a_refs, res_a):
                r[...] += v

    in_specs = []
    for (arr, col0, width, roff) in rows:
        assert col0 % width == 0 and arr.shape[0] % tr == 0
        in_specs.append(pl.BlockSpec((tr, width), lambda i, c=col0 // width, ro=roff: (jnp.maximum(i - ro, 0), c)))
    for c in consts:
        in_specs.append(pl.BlockSpec(c.shape, lambda i, nd=c.ndim: (0,) * nd))
    outs = [tuple(o) + (0,) * (3 - len(o)) for o in outs]
    out_specs = [pl.BlockSpec((tr, w), lambda i, ro=ro: (jnp.maximum(i - ro, 0), 0)) for (w, _, ro) in outs]
    out_specs += [pl.BlockSpec(s, lambda i, nd=len(s): (0,) * nd) for s in accs]
    out_shape = [jax.ShapeDtypeStruct((nrow - ro * tr, w), dt) for (w, dt, ro) in outs]
    out_shape += [jax.ShapeDtypeStruct(s, F32) for s in accs]
    res = pl.pallas_call(
        body, name=name, grid=(nrow // tr,), in_specs=in_specs, out_specs=out_specs, out_shape=out_shape,
        compiler_params=_params(("arbitrary",)),
    )(*[r[0] for r in rows], *consts)
    return res[:no], res[no:]


HEADS_PER_STEP = 2


def _scan_specs(xs, cs, ws, ks, chunk_of, hpb):
    specs = []
    for (arr, width, colfn) in xs:
        specs.append(pl.BlockSpec((CHUNK, width * hpb), lambda h, n, f=colfn: (chunk_of(n), f(h * hpb) // hpb)))
    for (arr, width, colfn) in cs:
        specs.append(pl.BlockSpec((CHUNK, width), lambda h, n, f=colfn: (chunk_of(n), f(h))))
    for arr in list(ws) + list(ks):
        specs.append(pl.BlockSpec((hpb, 1, arr.shape[2]), lambda h, n: (h, 0, 0)))
    return specs


def _scan_fwd(fn, xs, cs, ws, ks, *, heads, nchunk, s_shape, out_w, name, pre=None, hook=None,
              hpb=HEADS_PER_STEP):
    nx, ncs, nw = len(xs), len(cs), len(ws)
    hook = _NO_HOOK if hook is None else hook
    hi, ho = len(hook.ins), len(hook.outs)
    hblocks = heads // hpb

    def body(*refs):
        n = pl.program_id(1)
        nin = nx + ncs + nw + len(ks)
        y_ref, sp_ref = refs[nin + hi], refs[nin + hi + 1]
        s_scr = refs[nin + hi + 2 + ho]
        _run_hook(hook, pl.program_id(0) * nchunk + n, hblocks * nchunk, refs[nin:nin + hi],
                  refs[nin + hi + 2:nin + hi + 2 + ho], refs[nin + hi + 3 + ho:])

        @pl.when(n == 0)
        def _():
            s_scr[...] = jnp.zeros_like(s_scr)

        cv = [r[...] for r in refs[nx:nx + ncs]]
        for e in range(hpb):
            state = s_scr[e]
            sp_ref[e, 0] = state
            xv = [r[:, e * w:(e + 1) * w] for r, (_, w, _) in zip(refs[:nx], xs)]
            wv = [r[e] for r in refs[nx + ncs:nx + ncs + nw]]
            kv = [r[e] for r in refs[nx + ncs + nw:nin]]
            if pre is not None:
                xv = pre(xv, cv)
            y, s_new = fn(n, xv, state, cv, wv, kv)
            y_ref[:, e * out_w:(e + 1) * out_w] = y.astype(y_ref.dtype)
            s_scr[e] = s_new

    lp = nchunk * CHUNK
    res = pl.pallas_call(
        body, name=name, grid=(hblocks, nchunk),
        in_specs=_scan_specs(xs, cs, ws, ks, lambda n: n, hpb) + [_ANY] * hi,
        out_specs=[pl.BlockSpec((CHUNK, out_w * hpb), lambda h, n: (n, h)),
                   pl.BlockSpec((hpb, 1) + s_shape, lambda h, n: (h, n, 0, 0))] + [_ANY] * ho,
        out_shape=[jax.ShapeDtypeStruct((lp, heads * out_w), BF16),
                   jax.ShapeDtypeStruct((heads, nchunk) + s_shape, F32)] + hook.outs,
        scratch_shapes=[pltpu.VMEM((hpb,) + s_shape, F32)] + hook.sems,
        compiler_params=_params(("arbitrary", "arbitrary")),
    )(*[t[0] for t in xs], *[t[0] for t in cs], *ws, *ks, *hook.ins)
    return (res[0], res[1]) if hook is _NO_HOOK else (res[0], res[1], res[2:])


def _scan_bwd(fn, xs, cs, ws, ks, dy, sprev, *, heads, nchunk, s_shape, out_w, name, pre=None, post=None,
              hook=None, hpb=HEADS_PER_STEP):
    nx, ncs, nw = len(xs), len(cs), len(ws)
    nin = nx + ncs + nw + len(ks)
    hook = _NO_HOOK if hook is None else hook
    hi, ho = len(hook.ins), len(hook.outs)
    hblocks = heads // hpb

    def body(*refs):
        step = pl.program_id(1)
        n = nchunk - 1 - step
        dy_ref, sp_ref = refs[nin], refs[nin + 1]
        o0 = nin + 2 + hi
        dx_refs = refs[o0:o0 + nx]
        dw_refs = refs[o0 + nx:o0 + nx + nw]
        ds_scr = refs[o0 + nx + nw + ho]
        _run_hook(hook, pl.program_id(0) * nchunk + step, hblocks * nchunk, refs[nin + 2:o0],
                  refs[o0 + nx + nw:o0 + nx + nw + ho], refs[o0 + nx + nw + ho + 1:])

        @pl.when(step == 0)
        def _():
            ds_scr[...] = jnp.zeros_like(ds_scr)
            for r in dw_refs:
                r[...] = jnp.zeros_like(r)

        cv = [r[...] for r in refs[nx:nx + ncs]]
        for e in range(hpb):
            xv = [r[:, e * w:(e + 1) * w] for r, (_, w, _) in zip(refs[:nx], xs)]
            wv = [r[e] for r in refs[nx + ncs:nx + ncs + nw]]
            kv = [r[e] for r in refs[nx + ncs + nw:nin]]
            if pre is not None:
                xv = pre(xv, cv)
            _, vjp = jax.vjp(lambda xs_, s_, ws_, kv=kv: fn(n, xs_, s_, cv, ws_, kv), xv, sp_ref[e, 0], wv)
            dxs, ds_prev, dws = vjp((dy_ref[:, e * out_w:(e + 1) * out_w].astype(F32), ds_scr[e]))
            if post is not None:
                dxs = post(dxs, cv)
            for r, v, (_, w, _) in zip(dx_refs, dxs, xs):
                r[:, e * w:(e + 1) * w] = v.astype(r.dtype)
            for r, v in zip(dw_refs, dws):
                r[e] += v
            ds_scr[e] = ds_prev

    lp = nchunk * CHUNK
    rev = lambda n: nchunk - 1 - n
    in_specs = _scan_specs(xs, cs, ws, ks, rev, hpb)
    in_specs.append(pl.BlockSpec((CHUNK, out_w * hpb), lambda h, n: (rev(n), h)))
    in_specs.append(pl.BlockSpec((hpb, 1) + s_shape, lambda h, n: (h, rev(n), 0, 0)))
    out_specs = [pl.BlockSpec((CHUNK, w * hpb), lambda h, n: (rev(n), h)) for (_, w, _) in xs]
    out_specs += [pl.BlockSpec((hpb, 1, w.shape[2]), lambda h, n: (h, 0, 0)) for w in ws]
    out_shape = [jax.ShapeDtypeStruct((lp, heads * w), BF16) for (_, w, _) in xs]
    out_shape += [jax.ShapeDtypeStruct(w.shape, F32) for w in ws]
    res = pl.pallas_call(
        body, name=name, grid=(hblocks, nchunk), in_specs=in_specs + [_ANY] * hi,
        out_specs=out_specs + [_ANY] * ho, out_shape=out_shape + hook.outs,
        scratch_shapes=[pltpu.VMEM((hpb,) + s_shape, F32)] + hook.sems,
        compiler_params=_params(("arbitrary", "arbitrary")),
    )(*[t[0] for t in xs], *[t[0] for t in cs], *ws, *ks, dy, sprev, *hook.ins)
    if hook is _NO_HOOK:
        return res[:nx], res[nx:]
    return res[:nx], res[nx:nx + nw], res[nx + nw:]


def _iota2(shape, dim):
    return lax.broadcasted_iota(jnp.int32, shape, dim)


def _ret_chunk(n, xs, state, cs, ws, ks):
    q, k, v, z = xs
    (w,), (lg,) = ws, ks
    lgc = lg[:, :1]
    row, col = _iota2((CHUNK, CHUNK), 0), _iota2((CHUNK, CHUNK), 1)
    diff = jnp.maximum(row - col, 0).astype(F32)
    decay = jnp.where(row >= col, jnp.exp(lg * diff), 0.0)
    scores = _bdot(q, k, 1, 1) * decay
    o_intra = _bdot(scores, v, 1, 0)
    idx = _iota2((CHUNK, 1), 0).astype(F32)
    k_w = k * jnp.exp(lgc * (CHUNK - 1.0 - idx))
    kv = _bdot(k_w, v, 0, 0)
    s_new = state * jnp.exp(lgc * float(CHUNK)) + kv
    q_w = q * jnp.exp(lgc * (idx + 1.0))
    o = o_intra + _bdot(q_w, state, 1, 0)
    return _rms(o, w) * _silu(z), s_new


def _rope(t, cos2, sin2):
    return t * cos2 + pltpu.roll(t, RET_DK // 2, 1) * sin2


def _rope_t(g, cos2, sin2):
    return g * cos2 - pltpu.roll(g, RET_DK // 2, 1) * sin2


def _ret_pre(xv, cv):
    q, k, v, z = xv
    cos2, sin2 = cv
    return [_rope(q, cos2, sin2), _rope(k, cos2, sin2) * (RET_DK ** -0.5), v, z]


def _ret_post(dxs, cv):
    dq, dk, dv, dz = dxs
    cos2, sin2 = cv
    return [_rope_t(dq, cos2, sin2), _rope_t(dk, cos2, sin2) * (RET_DK ** -0.5), dv, dz]


def _gla_chunk(n, xs, state_t, cs, ws, ks):
    q, k, v, z, pre = xs
    (w,) = ws
    q = q * (GLA_DK ** -0.5)
    rowc = _iota2((CHUNK, 1), 0)
    valid = jnp.logical_or(n > 0, rowc >= PAD)
    log_a = jnp.where(valid, _log_sigmoid(pre) / GLA_TAU, 0.0)
    row, col = _iota2((CHUNK, CHUNK), 0), _iota2((CHUNK, CHUNK), 1)
    tri = (row >= col).astype(F32)
    b = jnp.dot(tri, log_a, precision=lax.Precision.HIGHEST, preferred_element_type=F32)
    b_last = b[CHUNK - 1:CHUNK, :]
    kv_t = _bdot(v, k * jnp.exp(b_last - b), 0, 0)
    s_new = state_t * jnp.exp(b_last) + kv_t
    o_inter = _bdot(q * jnp.exp(b), state_t, 1, 1)
    outs = []
    for s in range(CHUNK // SUB):
        lo, hi = s * SUB, (s + 1) * SUB
        b_ref = jnp.zeros_like(b_last) if s == 0 else b[lo - 1:lo, :]
        q_hat = q[lo:hi] * jnp.exp(b[lo:hi] - b_ref)
        k_hat = k[:hi] * jnp.exp(b_ref - b[:hi])
        sc = _bdot(q_hat, k_hat, 1, 1)
        causal = _iota2((SUB, hi), 0) + lo >= _iota2((SUB, hi), 1)
        outs.append(_bdot(jnp.where(causal, sc, 0.0), v[:hi], 1, 0))
    o = jnp.concatenate(outs, axis=0) + o_inter
    return _rms(o, w) * _silu(z), s_new


def _s5_disc(lam_re, lam_im, log_dt, b_re, b_im, expand):
    dt = jnp.exp(log_dt)
    mag = jnp.exp(lam_re * dt)
    ab_re, ab_im = mag * jnp.cos(lam_im * dt), mag * jnp.sin(lam_im * dt)
    den = lam_re * lam_re + lam_im * lam_im
    nr, ni = ab_re - 1.0, ab_im
    f_re = (nr * lam_re + ni * lam_im) / den
    f_im = (ni * lam_re - nr * lam_im) / den
    hp = lax.Precision.HIGHEST
    f_re = jnp.dot(f_re, expand, precision=hp, preferred_element_type=F32)
    f_im = jnp.dot(f_im, expand, precision=hp, preferred_element_type=F32)
    return ab_re, ab_im, f_re * b_re - f_im * b_im, f_re * b_im + f_im * b_re


def _s5_disc_fwd(args):
    def body(*refs):
        outs = _s5_disc(*[r[...] for r in refs[:6]])
        for r, v in zip(refs[6:], outs):
            r[...] = v

    g, p = args[0].shape
    return pl.pallas_call(
        body, name="s5_disc_fwd",
        out_shape=[jax.ShapeDtypeStruct((g, p), F32)] * 2 + [jax.ShapeDtypeStruct(args[3].shape, F32)] * 2,
    )(*args)


def _s5_disc_bwd(args, cts):
    def body(*refs):
        prim = [r[...] for r in refs[:5]]
        expand = refs[5][...]
        ct = tuple(r[...] for r in refs[6:10])
        _, vjp = jax.vjp(lambda *a: _s5_disc(*a, expand), *prim)
        for r, v in zip(refs[10:], vjp(ct)):
            r[...] = v

    return pl.pallas_call(
        body, name="s5_disc_bwd", out_shape=[jax.ShapeDtypeStruct(a.shape, F32) for a in args[:5]],
    )(*args, *cts)


SCAN_ROWS, SCAN_LANES = 32, 128
TILE_G = 8
TILE_W = TILE_G * S5_P
S5_TB = 64


def _s5_scan_fwd(bu, a_re, a_im):
    lp = bu.shape[0]

    def body(bu_ref, ar_ref, ai_ref, x_ref, st):
        @pl.when(pl.program_id(0) == 0)
        def _():
            st[...] = jnp.zeros_like(st)

        ar, ai = ar_ref[...], ai_ref[...]

        def step(t, carry):
            xr, xi = carry
            nr = ar * xr - ai * xi + bu_ref[t, 0:SCAN_ROWS, :]
            ni = ar * xi + ai * xr + bu_ref[t, SCAN_ROWS:2 * SCAN_ROWS, :]
            x_ref[t, 0:SCAN_ROWS, :] = nr
            x_ref[t, SCAN_ROWS:2 * SCAN_ROWS, :] = ni
            return nr, ni

        xr, xi = lax.fori_loop(0, S5_TB, step, (st[0], st[1]))
        st[0] = xr
        st[1] = xi

    blk = pl.BlockSpec((S5_TB, 2 * SCAN_ROWS, SCAN_LANES), lambda i: (i, 0, 0))
    cst = pl.BlockSpec((SCAN_ROWS, SCAN_LANES), lambda i: (0, 0))
    return pl.pallas_call(
        body, name="s5_scan_fwd", grid=(lp // S5_TB,), in_specs=[blk, cst, cst], out_specs=blk,
        out_shape=jax.ShapeDtypeStruct(bu.shape, F32),
        scratch_shapes=[pltpu.VMEM((2, SCAN_ROWS, SCAN_LANES), F32)],
        compiler_params=_params(("arbitrary",)),
    )(bu, a_re, a_im)


def _s5_expand(a, w_t, *, a_blk, dims, name, hook=None):
    lp, nt = a.shape[0], w_t.shape[0]
    tm = _tile(lp, 176, 8)
    steps = lp // tm
    rows3 = 2 * SCAN_ROWS
    per = TILE_W // SCAN_LANES
    hook = _NO_HOOK if hook is None else hook
    hi, ho = len(hook.ins), len(hook.outs)

    def body(*refs):
        a_ref, w_ref, o_ref = refs[0], refs[1], refs[2 + hi]
        _run_hook(hook, pl.program_id(0), steps, refs[2:2 + hi], refs[3 + hi:3 + hi + ho], refs[3 + hi + ho:])
        for j in range(nt):
            s = j % TILE_G
            r = lax.dot_general(a_ref[:, 128 * s:128 * (s + 1)].astype(BF16), w_ref[j], dims,
                                preferred_element_type=F32)
            for c in range(per):
                o_ref[pl.ds(per * j + c, tm, stride=rows3), :] = r[:, SCAN_LANES * c:SCAN_LANES * (c + 1)]

    res = pl.pallas_call(
        body, name=name, grid=(steps,),
        in_specs=[pl.BlockSpec((tm, S5_W), lambda i: (i, a_blk)), pl.BlockSpec(w_t.shape, lambda i: (0, 0, 0))]
        + [_ANY] * hi,
        out_specs=[pl.BlockSpec((tm * rows3, SCAN_LANES), lambda i: (i, 0))] + [_ANY] * ho,
        out_shape=[jax.ShapeDtypeStruct((lp * rows3, SCAN_LANES), F32)] + hook.outs,
        scratch_shapes=hook.sems, compiler_params=_params(("arbitrary",)),
    )(a, w_t, *hook.ins)
    out3 = res[0].reshape(lp, rows3, SCAN_LANES)
    return out3 if hook is _NO_HOOK else (out3, res[1:])


def _s5_scan_bwd(gx, x, a_re, a_im):
    lp = gx.shape[0]
    nb = lp // S5_TB

    def body(gx_ref, x_ref, xp_ref, ar_ref, ai_ref, g_ref, da_ref, st):
        i = pl.program_id(0)

        @pl.when(i == 0)
        def _():
            st[...] = jnp.zeros_like(st)
            da_ref[...] = jnp.zeros_like(da_ref)

        ar, ai = ar_ref[...], ai_ref[...]
        first = (i == nb - 1).astype(F32)

        def step(s, carry):
            gr, gi, dar, dai = carry
            t = S5_TB - 1 - s
            ngr = gx_ref[t, 0:SCAN_ROWS, :] + ar * gr + ai * gi
            ngi = gx_ref[t, SCAN_ROWS:2 * SCAN_ROWS, :] + ar * gi - ai * gr
            g_ref[t, 0:SCAN_ROWS, :] = ngr
            g_ref[t, SCAN_ROWS:2 * SCAN_ROWS, :] = ngi
            tp = jnp.maximum(t - 1, 0)
            at0 = (t == 0).astype(F32)
            keep = 1.0 - at0
            pr = keep * x_ref[tp, 0:SCAN_ROWS, :] + at0 * (1.0 - first) * xp_ref[0, 0:SCAN_ROWS, :]
            pi = keep * x_ref[tp, SCAN_ROWS:2 * SCAN_ROWS, :] + at0 * (1.0 - first) * xp_ref[0, SCAN_ROWS:2 * SCAN_ROWS, :]
            return ngr, ngi, dar + ngr * pr + ngi * pi, dai + ngi * pr - ngr * pi

        zero = jnp.zeros((SCAN_ROWS, SCAN_LANES), F32)
        gr, gi, dar, dai = lax.fori_loop(0, S5_TB, step, (st[0], st[1], zero, zero))
        st[0] = gr
        st[1] = gi
        da_ref[0] += dar
        da_ref[1] += dai

    rev = lambda i: nb - 1 - i
    blk = pl.BlockSpec((S5_TB, 2 * SCAN_ROWS, SCAN_LANES), lambda i: (rev(i), 0, 0))
    prev = pl.BlockSpec((1, 2 * SCAN_ROWS, SCAN_LANES), lambda i: (jnp.maximum(rev(i) * S5_TB - 1, 0), 0, 0))
    cst = pl.BlockSpec((SCAN_ROWS, SCAN_LANES), lambda i: (0, 0))
    return pl.pallas_call(
        body, name="s5_scan_bwd", grid=(nb,), in_specs=[blk, blk, prev, cst, cst],
        out_specs=[blk, pl.BlockSpec((2, SCAN_ROWS, SCAN_LANES), lambda i: (0, 0, 0))],
        out_shape=[jax.ShapeDtypeStruct(gx.shape, F32), jax.ShapeDtypeStruct((2, SCAN_ROWS, SCAN_LANES), F32)],
        scratch_shapes=[pltpu.VMEM((2, SCAN_ROWS, SCAN_LANES), F32)],
        compiler_params=_params(("arbitrary",)),
    )(gx, x, x, a_re, a_im)


def _place():
    x, y, c = lax.axis_index("x"), lax.axis_index("y"), lax.axis_index("c")
    return x, y, c, [(1 - x, y), (x, 1 - y), (1 - x, 1 - y)]


def _gather_phases():
    def plan(x_ref, out_ref, send_sems, recv_sems, local_sem):
        x, y, c, chips = _place()
        me, sibling = (x, y, c), (x, y, 1 - c)

        def rows(px, py, pc):
            return out_ref.at[4 * px + 2 * py + pc]

        def copy(k, block, to, src=None):
            return pltpu.make_async_remote_copy(
                src_ref=rows(*block) if src is None else src, dst_ref=rows(*block),
                send_sem=send_sems.at[k], recv_sem=recv_sems.at[k], device_id=to, device_id_type=MESH)

        mine = pltpu.make_async_copy(x_ref, rows(*me), local_sem)
        first = [copy(0, me, sibling, src=x_ref)]
        first += [copy(1 + j, me, (*chip, c), src=x_ref) for j, chip in enumerate(chips)]
        passed = [copy(4 + j, (*chip, c), sibling) for j, chip in enumerate(chips)]
        return c, chips, me, sibling, copy, mine, first, passed

    def start(ins, outs, sems):
        _, _, _, _, _, mine, first, _ = plan(ins[0], outs[0], *sems)
        mine.start()
        for cp in first:
            cp.start()

    def middle(ins, outs, sems):
        c, chips, me, _, copy, _, _, passed = plan(ins[0], outs[0], *sems)
        for j, chip in enumerate(chips):
            copy(1 + j, (*chip, c), me).wait_recv()
            passed[j].start()

    def finish(ins, outs, sems):
        c, chips, me, sibling, copy, mine, first, passed = plan(ins[0], outs[0], *sems)
        copy(0, sibling, me).wait_recv()
        for j, chip in enumerate(chips):
            copy(4 + j, (*chip, 1 - c), me).wait_recv()
        for cp in first + passed:
            cp.wait_send()
        mine.wait()

    return start, middle, finish


_GATHER_SEMS = [pltpu.SemaphoreType.DMA((7,)), pltpu.SemaphoreType.DMA((7,)), pltpu.SemaphoreType.DMA]


def _all_gather(shard, name):
    phases = _gather_phases()

    def body(x_ref, out_ref, *sems):
        for phase in phases:
            phase([x_ref], [out_ref], sems)

    return pl.pallas_call(
        body, name=name, out_shape=jax.ShapeDtypeStruct((N_DEV,) + shard.shape, shard.dtype),
        in_specs=[_ANY], out_specs=_ANY, scratch_shapes=list(_GATHER_SEMS),
    )(shard)


def _gather_hook(shard):
    start, middle, finish = _gather_phases()
    return _Hook([shard], [jax.ShapeDtypeStruct((N_DEV,) + shard.shape, shard.dtype)], _GATHER_SEMS,
                 [(0.0, start), (0.85, middle), (1.0, finish)])


def _swap_with_sibling(parts, name):
    def body(p_ref, out_ref, send_sems, recv_sems):
        x, y, c, _ = _place()
        copies = [pltpu.make_async_remote_copy(
            src_ref=p_ref.at[2 * chip + (1 - c)], dst_ref=out_ref.at[chip],
            send_sem=send_sems.at[chip], recv_sem=recv_sems.at[chip],
            device_id=(x, y, 1 - c), device_id_type=MESH) for chip in range(4)]
        for cp in copies:
            cp.start()
        for cp in copies:
            cp.wait()

    return pl.pallas_call(
        body, name=name, out_shape=jax.ShapeDtypeStruct((4,) + parts.shape[1:], parts.dtype),
        in_specs=[pl.BlockSpec(memory_space=pl.ANY)], out_specs=pl.BlockSpec(memory_space=pl.ANY),
        scratch_shapes=[pltpu.SemaphoreType.DMA((4,)), pltpu.SemaphoreType.DMA((4,))],
    )(parts)


def _chips_phases(lo, rows):
    def copies(p_ref, out_ref, send_sems, recv_sems):
        x, y, c, chips = _place()
        return [pltpu.make_async_remote_copy(
            src_ref=p_ref.at[2 * px + py, pl.ds(lo, rows)], dst_ref=out_ref.at[j],
            send_sem=send_sems.at[j], recv_sem=recv_sems.at[j],
            device_id=(px, py, c), device_id_type=MESH) for j, (px, py) in enumerate(chips)]

    def start(ins, outs, sems):
        for cp in copies(ins[0], outs[0], *sems):
            cp.start()

    def finish(ins, outs, sems):
        for cp in copies(ins[0], outs[0], *sems):
            cp.wait()

    return start, finish


def _chips_hook(parts, lo=0, hi=None):
    rows = (parts.shape[1] if hi is None else hi) - lo
    start, finish = _chips_phases(lo, rows)
    return _Hook([parts], [jax.ShapeDtypeStruct((3, rows) + parts.shape[2:], parts.dtype)],
                 [pltpu.SemaphoreType.DMA((3,)), pltpu.SemaphoreType.DMA((3,))], [(0.0, start), (1.0, finish)])


def _pack_rows(n_elem, row_mult=PACK_ROW_MULT):
    rows = -(-n_elem // PACK_COLS)
    return -(-rows // row_mult) * row_mult


def _pack(flats, dtype, row_mult=PACK_ROW_MULT):
    flat = jnp.concatenate([f.reshape(-1).astype(dtype) for f in flats])
    rows = _pack_rows(flat.shape[0], row_mult)
    return jnp.pad(flat, (0, rows * PACK_COLS - flat.shape[0])).reshape(rows, PACK_COLS)


def _unpack(buf, shapes):
    lead = buf.shape[:-2]
    flat = buf.reshape(lead + (-1,))
    outs, o = [], 0
    for s in shapes:
        n = math.prod(s)
        outs.append(flat[..., o:o + n].reshape(lead + tuple(s)))
        o += n
    return outs


BIG_LAYOUT = (("w_in_ab", D_MODEL, PACK_COLS), ("s5_w_glu", S5_W // N_DEV, PACK_COLS),
              ("w_out_ab", OUT_AB // N_DEV, 2 * PACK_COLS), ("w_in_c", D_MODEL, PACK_COLS),
              ("w_out_c", GLA_W // N_DEV, 2 * PACK_COLS))


def _to_rows(a):
    if a.shape[-1] == PACK_COLS:
        return a
    assert a.shape[-1] == 2 * PACK_COLS
    return jnp.concatenate([a[..., :PACK_COLS], a[..., PACK_COLS:]], axis=-2)


def _from_rows(p, cols):
    if cols == PACK_COLS:
        return p
    r = p.shape[-2] // 2
    return jnp.concatenate([p[..., :r, :], p[..., r:, :]], axis=-1)


FIRST_LAYOUT = BIG_LAYOUT[:1]
OTHER_LAYOUT = BIG_LAYOUT[1:3] + BIG_LAYOUT[4:]
GLU_AB_LAYOUT = BIG_LAYOUT[1:3]
IN_C_LAYOUT = BIG_LAYOUT[3:4]


def _pack_big(pieces, layout):
    return jnp.concatenate([_to_rows(pieces[name]) for name, _, _ in layout], axis=-2)


def _unpack_big(buf, layout):
    out, o = {}, 0
    for name, rows, cols in layout:
        r = rows * cols // PACK_COLS
        out[name] = _from_rows(buf[..., o:o + r, :], cols)
        o += r
    return out


def _column_windows(g):
    rows, quarter = g.shape[0], WIN_COLS // 4

    def body(g_ref, o_ref):
        o_ref[...] = g_ref[...]

    return pl.pallas_call(
        body, name="w_in_c_grad_windows", grid=(N_DEV, WIN_COLS // quarter),
        in_specs=[pl.BlockSpec((rows, quarter), lambda d, c: (0, (WIN_STEP // quarter) * d + c))],
        out_specs=pl.BlockSpec((None, rows, quarter), lambda d, c: (d, 0, c)),
        out_shape=jax.ShapeDtypeStruct((N_DEV, rows, WIN_COLS), g.dtype),
        compiler_params=_params(("arbitrary", "arbitrary")),
    )(g)


def _rows1024(a):
    r, c = a.shape
    if c > PACK_COLS:
        a = jnp.concatenate([a[:, i * PACK_COLS:(i + 1) * PACK_COLS] for i in range(c // PACK_COLS)], axis=0)
    elif c < PACK_COLS:
        a = jnp.pad(a, ((0, 0), (0, PACK_COLS - c)))
    return jnp.pad(a, ((0, -a.shape[0] % 8), (0, 0)))


def _unrows1024(p, r, c):
    if c > PACK_COLS:
        return jnp.concatenate([p[i * r:(i + 1) * r] for i in range(c // PACK_COLS)], axis=1)
    return p[:r, :c]


def _lane_select(a, off, sign, n_out, out_dtype, exact, name):
    rows, n_in = a.shape
    tr = _tile(rows, 256, 16)

    def body(off_ref, a_ref, o_ref):
        sel = _iota2((n_in, n_out), 0) + off_ref[0] * sign == _iota2((n_in, n_out), 1)
        if exact:
            r = jnp.dot(a_ref[...], sel.astype(F32), precision=lax.Precision.HIGHEST, preferred_element_type=F32)
        else:
            r = _dg(a_ref[...], sel.astype(BF16), 1, 0)
        o_ref[...] = r.astype(out_dtype)

    return pl.pallas_call(
        body, name=name, grid=(rows // tr,),
        in_specs=[pl.BlockSpec(memory_space=pltpu.SMEM), pl.BlockSpec((tr, n_in), lambda i: (i, 0))],
        out_specs=pl.BlockSpec((tr, n_out), lambda i: (i, 0)),
        out_shape=jax.ShapeDtypeStruct((rows, n_out), out_dtype),
        compiler_params=_params(("arbitrary",)),
    )(off, a)


def _adamw(w, g, m, v, name):
    rows, cols = w.shape
    tr = _tile(rows, 256, 8) if rows % 8 == 0 else rows

    def fn(i, w_, g_, m_, v_):
        m_new = ADAM_B1 * m_ + (1.0 - ADAM_B1) * g_
        v_new = ADAM_B2 * v_ + (1.0 - ADAM_B2) * (g_ * g_)
        m_hat = m_new / (1.0 - ADAM_B1 ** ADAM_STEP)
        v_hat = v_new / (1.0 - ADAM_B2 ** ADAM_STEP)
        delta = -ADAM_LR * (m_hat / (jnp.sqrt(v_hat) + ADAM_EPS) + ADAM_WD * w_)
        return (delta, m_new, v_new), ()

    outs, _ = _rows(fn, [_win(w), _win(g), _win(m), _win(v)], [], [(cols, F32)] * 3, [], name=name,
                    nrow=rows, tr=tr)
    return outs


def _as2d(a):
    if a.ndim == 1:
        return a.reshape(1, -1)
    if a.ndim == 2:
        return a
    a = a.reshape(a.shape[1:])
    return a if a.ndim == 2 else a.reshape(a.shape[0], -1)


def kernel(x, meta, norm_ab_w, w_in_ab, ret_norm_w, s5_lam_re, s5_lam_im, s5_log_dt, s5_b_re, s5_b_im, s5_c_re, s5_c_im, s5_d, s5_w_glu, w_out_ab, norm_c_w, w_in_c, gla_w_gate, gla_b_gate, gla_norm_w, w_out_c, final_norm_w, loss_target, m_meta, m_norm_ab_w, m_w_in_ab, m_ret_norm_w, m_s5_lam_re, m_s5_lam_im, m_s5_log_dt, m_s5_b_re, m_s5_b_im, m_s5_c_re, m_s5_c_im, m_s5_d, m_s5_w_glu, m_w_out_ab, m_norm_c_w, m_w_in_c, m_gla_w_gate, m_gla_b_gate, m_gla_norm_w, m_w_out_c, m_final_norm_w, v_meta, v_norm_ab_w, v_w_in_ab, v_ret_norm_w, v_s5_lam_re, v_s5_lam_im, v_s5_log_dt, v_s5_b_re, v_s5_b_im, v_s5_c_re, v_s5_c_im, v_s5_d, v_s5_w_glu, v_w_out_ab, v_norm_c_w, v_w_in_c, v_gla_w_gate, v_gla_b_gate, v_gla_norm_w, v_w_out_c, v_final_norm_w):
    weights = dict(meta=meta, norm_ab_w=norm_ab_w, w_in_ab=w_in_ab, ret_norm_w=ret_norm_w, s5_lam_re=s5_lam_re,
                   s5_lam_im=s5_lam_im, s5_log_dt=s5_log_dt, s5_b_re=s5_b_re, s5_b_im=s5_b_im, s5_c_re=s5_c_re,
                   s5_c_im=s5_c_im, s5_d=s5_d, s5_w_glu=s5_w_glu, w_out_ab=w_out_ab, norm_c_w=norm_c_w,
                   w_in_c=w_in_c, gla_w_gate=gla_w_gate, gla_b_gate=gla_b_gate, gla_norm_w=gla_norm_w,
                   w_out_c=w_out_c, final_norm_w=final_norm_w)
    mom_m = dict(meta=m_meta, norm_ab_w=m_norm_ab_w, w_in_ab=m_w_in_ab, ret_norm_w=m_ret_norm_w,
                 s5_lam_re=m_s5_lam_re, s5_lam_im=m_s5_lam_im, s5_log_dt=m_s5_log_dt, s5_b_re=m_s5_b_re,
                 s5_b_im=m_s5_b_im, s5_c_re=m_s5_c_re, s5_c_im=m_s5_c_im, s5_d=m_s5_d, s5_w_glu=m_s5_w_glu,
                 w_out_ab=m_w_out_ab, norm_c_w=m_norm_c_w, w_in_c=m_w_in_c, gla_w_gate=m_gla_w_gate,
                 gla_b_gate=m_gla_b_gate, gla_norm_w=m_gla_norm_w, w_out_c=m_w_out_c, final_norm_w=m_final_norm_w)
    mom_v = dict(meta=v_meta, norm_ab_w=v_norm_ab_w, w_in_ab=v_w_in_ab, ret_norm_w=v_ret_norm_w,
                 s5_lam_re=v_s5_lam_re, s5_lam_im=v_s5_lam_im, s5_log_dt=v_s5_log_dt, s5_b_re=v_s5_b_re,
                 s5_b_im=v_s5_b_im, s5_c_re=v_s5_c_re, s5_c_im=v_s5_c_im, s5_d=v_s5_d, s5_w_glu=v_s5_w_glu,
                 w_out_ab=v_w_out_ab, norm_c_w=v_norm_c_w, w_in_c=v_w_in_c, gla_w_gate=v_gla_w_gate,
                 gla_b_gate=v_gla_b_gate, gla_norm_w=v_gla_norm_w, w_out_c=v_w_out_c, final_norm_w=v_final_norm_w)
    order = list(weights)

    seq = x.shape[1]
    lp = CHUNK + seq
    nchunk = lp // CHUNK
    dev = 4 * lax.axis_index("x") + 2 * lax.axis_index("y") + lax.axis_index("c")
    core = lax.axis_index("c")
    chip = 2 * lax.axis_index("x") + lax.axis_index("y")

    win_off = jnp.reshape(2 * dev, (1,)).astype(jnp.int32)
    shard_c = jnp.pad(w_in_c[0].astype(BF16), ((0, 0), (0, 896 - SHARD_C)))
    big_shards = dict(w_in_ab=w_in_ab[0].astype(BF16), s5_w_glu=s5_w_glu[0].astype(BF16),
                      w_out_ab=w_out_ab[0].astype(BF16), w_out_c=w_out_c[0].astype(BF16),
                      w_in_c=_lane_select(shard_c, win_off, 1, WIN_COLS, BF16, False, "w_in_c_to_window"))
    def pad_to(a, rows, cols):
        return jnp.pad(a, ((0, rows - a.shape[0]), (0, cols - a.shape[1])))

    shard_w = D_MODEL // N_DEV
    small_pack = jnp.concatenate([meta, pad_to(norm_c_w, 8, shard_w), pad_to(gla_w_gate[0], GLA_RANK, shard_w),
                                  pad_to(gla_b_gate, 8, shard_w), pad_to(gla_norm_w, 8, shard_w)], axis=0)
    w_in_ab_g = _all_gather(big_shards["w_in_ab"], "gather_first")
    win_cut = 1408
    in_c_hook_a = _gather_hook(big_shards["w_in_c"][:win_cut])
    in_c_hook_b = _gather_hook(big_shards["w_in_c"][win_cut:])
    glu_ab_hook = _gather_hook(_pack_big(big_shards, GLU_AB_LAYOUT))
    out_c_hook = _gather_hook(_to_rows(big_shards["w_out_c"]))
    gs = _all_gather(small_pack, "gather_small")
    gate_w = GLA_QK // N_DEV
    s_meta, s_norm_c = gs[:, :N_META], gs[:, N_META]
    s_wgate, s_bgate, s_gnorm = gs[:, 24:24 + GLA_RANK, :gate_w], gs[:, 40, :gate_w], gs[:, 48]
    meta_f = s_meta.transpose(1, 0, 2).reshape(N_META, D_MODEL)
    norm_c_f = s_norm_c.reshape(1, D_MODEL)
    w_gate_f = jnp.pad(s_wgate.transpose(1, 0, 2).reshape(GLA_RANK, GLA_QK), ((0, GATE_PAD - GLA_RANK), (0, 0)))
    b_gate_f = s_bgate.reshape(1, GLA_QK)
    gla_norm_f = s_gnorm.reshape(GLA_H, 1, GLA_DV)

    pos = jnp.maximum(jnp.arange(lp, dtype=F32) - float(PAD), 0.0)
    inv_freq = jnp.power(ROPE_BASE, -jnp.arange(0, RET_DK, 2, dtype=F32) / RET_DK)
    ang = pos[:, None] * inv_freq[None, :]
    cos2 = jnp.concatenate([jnp.cos(ang), jnp.cos(ang)], axis=1)
    sin2 = jnp.concatenate([-jnp.sin(ang), jnp.sin(ang)], axis=1)
    log_g = jnp.log1p(-jnp.exp2(-5.0 - jnp.arange(RET_H, dtype=F32)))
    lg = jnp.broadcast_to(log_g[:, None, None], (RET_H, 1, 128))
    ret_norm_h = ret_norm_w.reshape(RET_H, 1, RET_DV)

    h0 = jnp.concatenate([jnp.zeros((PAD, D_MODEL), F32), meta_f, x[0]], axis=0)

    def rowmask(i):
        return (_iota2((CHUNK, 1), 0) + i * CHUNK) >= PAD

    (hn0,), _ = _rows(lambda i, h, w: ((_rms(h, w),), ()), [_win(h0)], [norm_ab_w], [(D_MODEL, BF16)], [],
                      name="norm_ab_fwd", nrow=lp)
    proj_ab, (w_in_c_ga,) = _mm(hn0, w_in_ab_g, "nn", name="in_ab_fwd", hook=in_c_hook_a, b_dev=True)

    q_off, k_off, v_off, za_off = 0, RET_QK, 2 * RET_QK, 2 * RET_QK + RET_W
    u_off, zb_off = 2 * RET_QK + 2 * RET_W, 2 * RET_QK + 2 * RET_W + S5_W
    ret_xs = [(proj_ab, RET_DK, lambda h: q_off // RET_DK + h), (proj_ab, RET_DK, lambda h: k_off // RET_DK + h),
              (proj_ab, RET_DV, lambda h: v_off // RET_DV + h), (proj_ab, RET_DV, lambda h: za_off // RET_DV + h)]
    ret_cs = [(cos2, RET_DK, lambda h: 0), (sin2, RET_DK, lambda h: 0)]
    ret_kw = dict(heads=RET_H, nchunk=nchunk, s_shape=(RET_DK, RET_DV), out_w=RET_DV, pre=_ret_pre, hpb=4)
    o_a, ret_sprev, (gathered_glu_ab,) = _scan_fwd(_ret_chunk, ret_xs, ret_cs, [ret_norm_h], [lg], name="ret_fwd",
                                                   hook=glu_ab_hook, **ret_kw)
    gb = _unpack_big(gathered_glu_ab, GLU_AB_LAYOUT)
    w_glu_f = gb["s5_w_glu"].reshape(S5_W, S5_W)
    w_out_ab_f = gb["w_out_ab"].reshape(OUT_AB, D_MODEL)

    expand = jnp.repeat(jnp.eye(S5_P, dtype=F32), S5_GH, axis=1)
    disc_args = (s5_lam_re[0], s5_lam_im[0], s5_log_dt[0].reshape(S5_G, 1),
                 s5_b_re[0].reshape(S5_G, S5_P * S5_GH), s5_b_im[0].reshape(S5_G, S5_P * S5_GH), expand)
    ab_re, ab_im, bb_re, bb_im = _s5_disc_fwd(disc_args)
    gt = TILE_G
    eye_t = jnp.eye(gt, dtype=F32)

    def tiles_in(bb):
        return jnp.einsum("sgph,gk->sghkp", bb.reshape(gt, gt, S5_P, S5_GH), eye_t).reshape(gt, 128, TILE_W)

    def tiles_out(cc):
        return jnp.einsum("sghp,gk->sgpkh", cc.reshape(gt, gt, S5_GH, S5_P), eye_t).reshape(gt, TILE_W, 128)

    wb_t = jnp.concatenate([tiles_in(bb_re), tiles_in(bb_im)], axis=0).astype(BF16)
    wc_t = jnp.concatenate([tiles_out(s5_c_re[0]), -tiles_out(s5_c_im[0])], axis=0).astype(BF16)
    a_re, a_im = ab_re.reshape(SCAN_ROWS, SCAN_LANES), ab_im.reshape(SCAN_ROWS, SCAN_LANES)
    tm5, tk5, nt5 = _tile(lp, 1408, 8), _tile(lp, 1408, 8), 2 * gt
    u_blk = u_off // 128
    wide_k = pl.BlockSpec((tm5, TILE_W), lambda i, j, k: (i, k * gt + j))
    narrow = pl.BlockSpec((tm5, 128), lambda i, j, k: (i, j))
    wb_k = pl.BlockSpec((None, 128, TILE_W), lambda i, j, k: (k * gt + j, 0, 0))
    wc_k = pl.BlockSpec((None, TILE_W, 128), lambda i, j, k: (k * gt + j, 0, 0))
    wide_shape = jax.ShapeDtypeStruct((lp, 2 * S5_N), F32)
    bu3, (w_out_c_g,) = _s5_expand(proj_ab, wb_t, a_blk=u_off // S5_W, dims=NN, name="s5_bu", hook=out_c_hook)
    w_out_c_f = _from_rows(w_out_c_g, D_MODEL).reshape(GLA_W, D_MODEL)
    xs5 = _s5_scan_fwd(bu3, a_re, a_im)
    xs5_2d = xs5.reshape(lp, 2 * S5_N)
    y_pre, (w_in_c_gb,) = _mm_core(xs5_2d, wc_t, dims=NN, grid=(lp // tm5, gt, 2), name="s5_cx", a_spec=wide_k,
                                   b_spec=wc_k, o_spec=narrow, out_shape=jax.ShapeDtypeStruct((lp, S5_W), F32),
                                   acc_shape=(tm5, 128), hook=in_c_hook_b)
    (y_s5, yg_bf), _ = _rows(
        lambda i, yp, u, d: ((yp + d * u, _gelu(yp + d * u)), ()),
        [_win(y_pre), _win(proj_ab, u_off, S5_W)], [s5_d], [(S5_W, F32), (S5_W, BF16)], [], name="s5_gelu_fwd", nrow=lp)
    t_glu = _mm(yg_bf, w_glu_f, "nn", name="s5_glu_fwd")

    def s5_gate(y, t, zb):
        return _gelu(y) * _sigmoid(t) * _silu(zb)

    (o_b,), _ = _rows(lambda i, y, t, zb: ((s5_gate(y, t, zb),), ()),
                      [_win(y_s5), _win(t_glu), _win(proj_ab, zb_off, S5_W)], [], [(S5_W, BF16)], [],
                      name="s5_gate_fwd", nrow=lp)
    o_ab = jnp.concatenate([o_a, o_b], axis=1)
    h1 = _mm(o_ab, w_out_ab_f, "nn", name="out_ab_fwd", add=h0)
    w_in_c_g = jnp.concatenate([w_in_c_ga, w_in_c_gb], axis=1)
    w_in_c_f = sum(jnp.pad(w_in_c_g[d], ((0, 0), (WIN_STEP * d, IN_C_PAD - WIN_STEP * d - WIN_COLS)))
                   for d in range(N_DEV))

    (hn1,), _ = _rows(lambda i, h, w: ((_rms(h, w),), ()), [_win(h1)], [norm_c_f], [(D_MODEL, BF16)], [],
                      name="norm_c_fwd", nrow=lp)
    proj_c = _mm(hn1, w_in_c_f, "nn", name="in_c_fwd")
    gl_off = 2 * GLA_QK + 2 * GLA_W
    pre_gate = _mm(proj_c, w_gate_f, "nn", name="gate_fwd", a_win=(gl_off, GATE_PAD), bias=b_gate_f)
    gla_xs = [(proj_c, GLA_DK, lambda h: h), (proj_c, GLA_DK, lambda h: GLA_QK // GLA_DK + h),
              (proj_c, GLA_DV, lambda h: 2 * GLA_QK // GLA_DV + h),
              (proj_c, GLA_DV, lambda h: (2 * GLA_QK + GLA_W) // GLA_DV + h),
              (pre_gate, GLA_DK, lambda h: h)]
    gla_kw = dict(heads=GLA_H, nchunk=nchunk, s_shape=(GLA_DV, GLA_DK), out_w=GLA_DV, hpb=GLA_H)
    o_c, gla_sprev = _scan_fwd(_gla_chunk, gla_xs, [], [gla_norm_f], [], name="gla_fwd", **gla_kw)
    h2 = _mm(o_c, w_out_c_f, "nn", name="out_c_fwd", add=h1)

    fnw = final_norm_w.reshape(1, D_MODEL)

    def final_fn(i, h, tgt, w):
        def loss_of(h_, w_):
            err = _rms(h_, w_) - tgt
            return 0.5 * jnp.sum(jnp.mean(err * err, axis=-1))

        real = (i > 0).astype(F32)
        loss_i, (dh, dw) = jax.value_and_grad(loss_of, argnums=(0, 1))(h, w)
        return (dh * real, dh * real), (jnp.full((1, 128), loss_i * real, F32), dw * real)

    (dh2, dh2_bf), (loss_acc, g_final) = _rows(
        final_fn, [_win(h2), _win(loss_target[0], roff=1)], [fnw], [(D_MODEL, F32), (D_MODEL, BF16)],
        [(1, 128), (1, D_MODEL)], name="final_loss", nrow=lp)

    def rs_front(pieces, layout, tag):
        g_full = _pack_big(pieces, layout)
        prow = g_full.shape[1]
        from_sibling = _swap_with_sibling(g_full, "rs_sibling_" + tag)
        mine_by_chip = lax.dynamic_index_in_dim(g_full.reshape(4, 2, prow, PACK_COLS), core, axis=1, keepdims=False)
        (p1, p1_bf), _ = _rows(
            lambda i, a, b: ((a.astype(F32) + b.astype(F32), a.astype(F32) + b.astype(F32)), ()),
            [_win(mine_by_chip.reshape(4 * prow, PACK_COLS)), _win(from_sibling.reshape(4 * prow, PACK_COLS))], [],
            [(PACK_COLS, F32), (PACK_COLS, BF16)], [], name="rs_sum_sibling_" + tag, nrow=4 * prow,
            tr=_tile(prow, 512, 16))
        return p1.reshape(4, prow, PACK_COLS), p1_bf.reshape(4, prow, PACK_COLS)

    def rs_back(p1, from_chips, layout, tag):
        prow = p1.shape[1]
        tr = _tile(prow, 512, 16)
        own = lax.dynamic_index_in_dim(p1, chip, axis=0, keepdims=False)
        fc2 = from_chips.reshape(3 * prow, PACK_COLS)
        nblk = prow // tr
        (g_shard,), _ = _rows(
            lambda i, a, b0, b1, b2: ((((a + b0.astype(F32)) + b1.astype(F32)) + b2.astype(F32),), ()),
            [_win(own), _win(fc2), _win(fc2, roff=-nblk), _win(fc2, roff=-2 * nblk)], [], [(PACK_COLS, F32)], [],
            name="rs_sum_chips_" + tag, nrow=prow, tr=tr)
        return _unpack_big(g_shard, layout)

    do_c = _mm(dh2_bf, w_out_c_f, "nt", name="out_c_dx", out_dtype=BF16)
    gw_out_c = _mm(o_c, dh2_bf, "tn", name="out_c_dw", out_dtype=BF16)
    (dq_c, dk_c, dv_c, dz_c, dpre), (g_gla_norm,) = _scan_bwd(
        _gla_chunk, gla_xs, [], [gla_norm_f], [], do_c, gla_sprev, name="gla_bwd", **gla_kw)
    dglow = _mm(dpre, w_gate_f, "nt", name="gate_dx", out_dtype=BF16)
    g_wgate = _mm(proj_c, dpre, "tn", name="gate_dw", a_win=(gl_off, GATE_PAD))[:GLA_RANK]
    (), (g_bgate,) = _rows(lambda i, d: ((), (jnp.sum(d.astype(F32), axis=0, keepdims=True),)), [_win(dpre)], [], [],
                           [(1, GLA_QK)], name="gate_db", nrow=lp)
    dproj_c = jnp.concatenate([dq_c, dk_c, dv_c, dz_c, dglow], axis=1)
    dhn1 = _mm(dproj_c, w_in_c_f, "nt", name="in_c_dx")
    gw_in_c = _mm(hn1, dproj_c, "tn", name="in_c_dw", out_dtype=BF16)
    p1_c, p1_c_bf = rs_front(dict(
        w_in_c=_column_windows(gw_in_c)),
        IN_C_LAYOUT, "in_c")

    def norm_bwd(i, h, dhn, dres, w):
        _, vjp = jax.vjp(_rms, h, w)
        dh, dw = vjp(dhn)
        return (jnp.where(rowmask(i), dh + dres, 0.0),), (dw,)

    def norm_bwd_both(i, h, dhn, dres, w):
        (dh,), acc = norm_bwd(i, h, dhn, dres, w)
        return (dh, dh), acc

    (dh1, dh1_bf), (g_norm_c,) = _rows(norm_bwd_both, [_win(h1), _win(dhn1), _win(dh2)], [norm_c_f],
                                       [(D_MODEL, F32), (D_MODEL, BF16)], [(1, D_MODEL)], name="norm_c_bwd", nrow=lp)

    do_ab = _mm(dh1_bf, w_out_ab_f, "nt", name="out_ab_dx", out_dtype=BF16)
    gw_out_ab = _mm(o_ab, dh1_bf, "tn", name="out_ab_dw", out_dtype=BF16)

    def s5_gate_bwd(i, dob, y, t, zb):
        _, vjp = jax.vjp(s5_gate, y, t, zb)
        dy, dt, dzb = vjp(dob.astype(F32))
        return (dy, dt, dzb), ()

    (dy_a, dt_glu, dzb), _ = _rows(
        s5_gate_bwd, [_win(do_ab, RET_W, S5_W), _win(y_s5), _win(t_glu), _win(proj_ab, zb_off, S5_W)], [],
        [(S5_W, F32), (S5_W, BF16), (S5_W, BF16)], [], name="s5_gate_bwd", nrow=lp)
    dyg2 = _mm(dt_glu, w_glu_f, "nt", name="s5_glu_dx")
    gw_glu = _mm(yg_bf, dt_glu, "tn", name="s5_glu_dw", out_dtype=BF16)

    def s5_y_bwd(i, dya, dyg, y, u, d):
        _, vjp = jax.vjp(_gelu, y)
        (dy_g,) = vjp(dyg)
        dy = dya + dy_g
        return (dy, d * dy), (jnp.sum(dy * u, axis=0, keepdims=True),)

    (dy_s5, du1), (g_d,) = _rows(
        s5_y_bwd, [_win(dy_a), _win(dyg2), _win(y_s5), _win(proj_ab, u_off, S5_W)], [s5_d],
        [(S5_W, BF16), (S5_W, F32)], [(1, S5_W)], name="s5_y_bwd", nrow=lp)
    p1_o, p1_o_bf = rs_front(dict(s5_w_glu=gw_glu.reshape(N_DEV, S5_W // N_DEV, S5_W),
                                  w_out_ab=gw_out_ab.reshape(N_DEV, OUT_AB // N_DEV, D_MODEL),
                                  w_out_c=gw_out_c.reshape(N_DEV, GLA_W // N_DEV, D_MODEL)), OTHER_LAYOUT, "other")
    o_cut = 640
    gx3, (from_chips_oa,) = _s5_expand(dy_s5, wc_t, a_blk=0, dims=NT, name="s5_cx_dx",
                                       hook=_chips_hook(p1_o_bf, 0, o_cut))
    rows_k = lambda col: pl.BlockSpec((tk5, col), lambda i, j, k: (k, i))
    gwc = _mm_core(xs5_2d, dy_s5, dims=TN, grid=(nt5, 1, lp // tk5), name="s5_cx_dw", a_spec=rows_k(TILE_W),
                   b_spec=pl.BlockSpec((tk5, 128), lambda i, j, k: (k, i % gt)),
                   o_spec=pl.BlockSpec((None, TILE_W, 128), lambda i, j, k: (i, 0, 0)),
                   out_shape=jax.ShapeDtypeStruct((nt5, TILE_W, 128), F32), acc_shape=(TILE_W, 128))
    g_s5, da = _s5_scan_bwd(gx3, xs5, a_re, a_im)
    g_s5_2d = g_s5.reshape(lp, 2 * S5_N)
    du, (from_chips_ob,) = _mm_core(g_s5_2d, wb_t, dims=NT, grid=(lp // tm5, gt, 2), name="s5_bu_dx", a_spec=wide_k,
                                    b_spec=wb_k, o_spec=narrow, out_shape=jax.ShapeDtypeStruct((lp, S5_W), BF16),
                                    acc_shape=(tm5, 128), extra=[(du1, narrow)],
                                    hook=_chips_hook(p1_o_bf, o_cut, None))
    from_chips_o = jnp.concatenate([from_chips_oa, from_chips_ob], axis=1)
    gwb = _mm_core(proj_ab, g_s5_2d, dims=TN, grid=(nt5, 1, lp // tk5), name="s5_bu_dw",
                   a_spec=pl.BlockSpec((tk5, 128), lambda i, j, k: (k, u_blk + i % gt)), b_spec=rows_k(TILE_W),
                   o_spec=pl.BlockSpec((None, 128, TILE_W), lambda i, j, k: (i, 0, 0)),
                   out_shape=jax.ShapeDtypeStruct((nt5, 128, TILE_W), F32), acc_shape=(128, TILE_W))
    gwc6 = gwc.reshape(2, gt, gt, S5_P, gt, S5_GH)
    g_c = jnp.einsum("rsgpgh->rsghp", gwc6).reshape(2, S5_G, S5_GH, S5_P)
    g_c_re, g_c_im = g_c[0], -g_c[1]
    gwb6 = gwb.reshape(2, gt, gt, S5_GH, gt, S5_P)
    d_bb = jnp.einsum("rsghgp->rsgph", gwb6).reshape(2, S5_G, S5_P * S5_GH)
    d_bb_re, d_bb_im = d_bb[0], d_bb[1]
    g_lam_re, g_lam_im, g_log_dt, g_b_re, g_b_im = _s5_disc_bwd(
        disc_args, (da[0].reshape(S5_G, S5_P), da[1].reshape(S5_G, S5_P), d_bb_re, d_bb_im))

    (dq_a, dk_a, dv_a, dz_a), (g_ret_norm,), (from_chips_c,) = _scan_bwd(
        _ret_chunk, ret_xs, ret_cs, [ret_norm_h], [lg], do_ab, ret_sprev, name="ret_bwd", post=_ret_post,
        hook=_chips_hook(p1_c_bf), **ret_kw)
    dproj_ab = jnp.concatenate([dq_a, dk_a, dv_a, dz_a, du, dzb], axis=1)

    lane = lambda a_: pad_to(a_, a_.shape[0], 128)

    def sum8(i, *blocks):
        acc = blocks[0]
        for b in blocks[1:]:
            acc = acc + b
        return (acc,), ()

    def pack_small(pieces):
        return jnp.concatenate([_rows1024(p) for _, p in pieces], axis=0)

    def sum_small(gathered, pieces, tag):
        srow = gathered.shape[1]
        tr = _tile(srow, 128, 8)
        flat = gathered.reshape(N_DEV * srow, PACK_COLS)
        (total,), _ = _rows(sum8, [_win(flat, roff=-d * (srow // tr)) for d in range(N_DEV)], [], [(PACK_COLS, F32)],
                            [], name="sum_small_" + tag, nrow=srow, tr=tr)
        out, o = {}, 0
        for name_, p in pieces:
            r8 = _rows1024(p).shape[0]
            out[name_] = _unrows1024(total[o:o + r8], *p.shape)
            o += r8
        return out

    early_pieces = [
        ("vec2048", jnp.concatenate([g_final, g_norm_c], axis=0)),
        ("vec1024", jnp.concatenate([g_d, g_bgate, pad_to(loss_acc[:, :1], 1, PACK_COLS)], axis=0)),
        ("lam3", jnp.concatenate([lane(g_lam_re), lane(g_lam_im), lane(g_log_dt)], axis=1)),
        ("s5_b_re", g_b_re), ("s5_b_im", g_b_im),
        ("s5_c_re", g_c_re.reshape(S5_G, S5_GH * S5_P)), ("s5_c_im", g_c_im.reshape(S5_G, S5_GH * S5_P)),
        ("ret_norm_w", g_ret_norm.reshape(RET_H, RET_DV)), ("gla_norm_w", g_gla_norm.reshape(GLA_H, GLA_DV)),
        ("gla_w_gate", g_wgate)]
    gw_in_ab, (early_all,) = _mm(
        hn0, dproj_ab, "tn", name="in_ab_dw", out_dest=True, out_dtype=BF16,
        hook=_gather_hook(pack_small(early_pieces)))
    p1_first, p1_first_bf = rs_front(dict(w_in_ab=gw_in_ab), FIRST_LAYOUT, "first")
    dhn0, (from_chips_first,) = _mm(dproj_ab, w_in_ab_g, "nt", name="in_ab_dx", b_dev=True,
                                    hook=_chips_hook(p1_first_bf))
    def norm_bwd_first(i, h, dhn, dres, w):
        (dh,), (dw,) = norm_bwd(i, h, dhn, dres, w)
        return (dh,), (dw, dh * (i == 0).astype(F32))

    (grad_x2d,), (g_norm_ab, dh0_first) = _rows(
        norm_bwd_first, [_win(h0), _win(dhn0), _win(dh1)], [norm_ab_w], [(D_MODEL, F32, 1)],
        [(1, D_MODEL), (CHUNK, D_MODEL)], name="norm_ab_bwd", nrow=lp)
    grad_x = grad_x2d[None]
    late_pieces = [("norm_ab_w", g_norm_ab), ("meta", dh0_first[PAD:CHUNK])]
    small = sum_small(early_all, early_pieces, "early")
    small.update(sum_small(_all_gather(pack_small(late_pieces), "gather_grads"), late_pieces, "late"))

    big_grads = {**rs_back(p1_c, from_chips_c, IN_C_LAYOUT, "in_c"), **rs_back(p1_o, from_chips_o, OTHER_LAYOUT, "other"),
                 **rs_back(p1_first, from_chips_first, FIRST_LAYOUT, "first")}
    big_grads["w_in_c"] = _lane_select(big_grads["w_in_c"], win_off, -1, 896, F32, True,
                                       "w_in_c_from_window")[:, :SHARD_C]
    small["final_norm_w"], small["norm_c_w"] = small["vec2048"][0:1], small["vec2048"][1:2]
    small["s5_d"], small["gla_b_gate"] = small["vec1024"][0:1], small["vec1024"][1:2]
    loss = small["vec1024"][2, 0]
    small["s5_lam_re"], small["s5_lam_im"] = small["lam3"][:, :S5_P], small["lam3"][:, 128:128 + S5_P]
    small["s5_log_dt"] = small["lam3"][:, 256:257]

    def my_cols(g, n):
        return lax.dynamic_slice_in_dim(g, dev * n, n, axis=g.ndim - 1)

    grads = dict(
        meta=my_cols(small["meta"], D_MODEL // N_DEV),
        norm_ab_w=small["norm_ab_w"], w_in_ab=big_grads["w_in_ab"][None], ret_norm_w=small["ret_norm_w"].reshape(1, RET_W),
        s5_lam_re=small["s5_lam_re"][None], s5_lam_im=small["s5_lam_im"][None],
        s5_log_dt=small["s5_log_dt"].reshape(1, S5_G),
        s5_b_re=small["s5_b_re"].reshape(1, S5_G, S5_P, S5_GH), s5_b_im=small["s5_b_im"].reshape(1, S5_G, S5_P, S5_GH),
        s5_c_re=small["s5_c_re"][None], s5_c_im=small["s5_c_im"][None], s5_d=small["s5_d"],
        s5_w_glu=big_grads["s5_w_glu"][None], w_out_ab=big_grads["w_out_ab"][None],
        norm_c_w=my_cols(small["norm_c_w"], D_MODEL // N_DEV), w_in_c=big_grads["w_in_c"][None],
        gla_w_gate=my_cols(small["gla_w_gate"], GLA_QK // N_DEV)[None],
        gla_b_gate=my_cols(small["gla_b_gate"], GLA_QK // N_DEV),
        gla_norm_w=my_cols(small["gla_norm_w"].reshape(1, GLA_W), GLA_W // N_DEV),
        w_out_c=big_grads["w_out_c"][None], final_norm_w=small["final_norm_w"].reshape(D_MODEL))

    deltas, new_m, new_v = {}, {}, {}
    for k in order:
        w = weights[k]
        d2, m2, v2 = _adamw(_as2d(w), _as2d(grads[k].reshape(w.shape)), _as2d(mom_m[k]), _as2d(mom_v[k]), "adamw_" + k)
        deltas[k], new_m[k], new_v[k] = d2.reshape(w.shape), m2.reshape(w.shape), v2.reshape(w.shape)
        grads[k] = grads[k].reshape(w.shape)

    return (loss, grad_x, *[grads[k] for k in order], *[deltas[k] for k in order],
            *[new_m[k] for k in order], *[new_v[k] for k in order])
```

```python
import functools
import math

import jax
import jax.numpy as jnp
from jax import lax
from jax.experimental import pallas as pl
from jax.experimental.pallas import tpu as pltpu

F32, BF16 = jnp.float32, jnp.bfloat16
MESH = pl.DeviceIdType.MESH
N_DEV = 8

D_MODEL = 2048
CHUNK = 128
N_META = 16
PAD = CHUNK - N_META
SUB = 16
EPS = 1e-6
RET_H, RET_DK, RET_DV = 8, 128, 256
RET_QK, RET_W = RET_H * RET_DK, RET_H * RET_DV
ROPE_BASE = 10000.0
S5_W, S5_G, S5_P, S5_GH = 1024, 64, 64, 16
S5_N = S5_G * S5_P
GLA_H, GLA_DK, GLA_DV, GLA_RANK, GLA_TAU = 4, 256, 512, 16, 16.0
GLA_QK, GLA_W = GLA_H * GLA_DK, GLA_H * GLA_DV
IN_AB = 2 * RET_QK + 2 * RET_W + 2 * S5_W
OUT_AB = RET_W + S5_W
IN_C = 2 * GLA_QK + 2 * GLA_W + GLA_RANK
GATE_PAD = 256
IN_C_PAD = 2 * GLA_QK + 2 * GLA_W + GATE_PAD
ADAM_LR, ADAM_B1, ADAM_B2, ADAM_EPS, ADAM_WD, ADAM_STEP = 0.001, 0.9, 0.999, 1e-08, 0.01, 10

VMEM_LIMIT_BYTES = 48 * 2 ** 20
PACK_COLS = 1024
SHARD_C = IN_C // N_DEV
WIN_STEP = 768
WIN_COLS = 1024


def _params(sem):
    return pltpu.CompilerParams(dimension_semantics=sem, vmem_limit_bytes=VMEM_LIMIT_BYTES)


def _tile(n, cap, mult):
    best = None
    for t in range(mult, min(n, cap) + 1, mult):
        if n % t == 0:
            best = t
    assert best is not None, (n, cap, mult)
    return best


def _dg(a, b, ca, cb):
    return lax.dot_general(a.astype(BF16), b.astype(BF16), (((ca,), (cb,)), ((), ())),
                           preferred_element_type=F32)


@functools.partial(jax.custom_vjp, nondiff_argnums=(2, 3))
def _bdot(a, b, ca, cb):
    return _dg(a, b, ca, cb)


def _bdot_fwd(a, b, ca, cb):
    return _dg(a, b, ca, cb), (a, b)


def _bdot_bwd(ca, cb, res, g):
    a, b = res
    da = _dg(g, b, 1, 1 - cb) if ca == 1 else _dg(b, g, 1 - cb, 1)
    db = _dg(a, g, 1 - ca, 0) if cb == 0 else _dg(g, a, 0, 1 - ca)
    return da.astype(a.dtype), db.astype(b.dtype)


_bdot.defvjp(_bdot_fwd, _bdot_bwd)


def _sigmoid(x):
    return 1.0 / (1.0 + jnp.exp(-x))


def _silu(x):
    return x * _sigmoid(x)


def _log_sigmoid(x):
    return jnp.minimum(x, 0.0) - jnp.log(1.0 + jnp.exp(-jnp.abs(x)))


def _gelu(x):
    return 0.5 * x * (1.0 + jnp.tanh(math.sqrt(2.0 / math.pi) * (x + 0.044715 * (x * x * x))))


def _rms(x, w):
    return x * lax.rsqrt(jnp.mean(x * x, axis=-1, keepdims=True) + EPS) * w


class _Hook:
    def __init__(self, ins, outs, sems, phases):
        self.ins, self.outs, self.sems, self.phases = list(ins), list(outs), list(sems), list(phases)


_NO_HOOK = _Hook([], [], [], [])
_ANY = pl.BlockSpec(memory_space=pl.ANY)


def _run_hook(hook, lin, total, in_refs, out_refs, sem_refs):
    for frac, fn in hook.phases:
        at = min(int(frac * total), total - 1)

        @pl.when(lin == at)
        def _(fn=fn):
            fn(in_refs, out_refs, sem_refs)


def _mm_core(a, b, *, dims, grid, a_spec, b_spec, o_spec, out_shape, acc_shape, name, extra=(), hook=None,
             b_parts=0):
    nk = grid[2]
    n_extra = len(extra)
    hook = _NO_HOOK if hook is None else hook
    hi, ho = len(hook.ins), len(hook.outs)

    def body(*refs):
        a_ref, b_ref = refs[0], refs[1]
        o_ref, acc = refs[2 + n_extra + hi], refs[3 + n_extra + hi + ho]
        k = pl.program_id(2)
        lin = (pl.program_id(0) * grid[1] + pl.program_id(1)) * nk + k
        _run_hook(hook, lin, grid[0] * grid[1] * nk, refs[2 + n_extra:2 + n_extra + hi],
                  refs[3 + n_extra + hi:3 + n_extra + hi + ho], refs[4 + n_extra + hi + ho:])

        if b_parts:
            part = sum(lax.dot_general(a_ref[:, d * PACK_COLS:(d + 1) * PACK_COLS].astype(BF16), b_ref[d].astype(BF16),
                                       dims, preferred_element_type=F32) for d in range(b_parts))
        else:
            part = lax.dot_general(a_ref[...].astype(BF16), b_ref[...].astype(BF16), dims, preferred_element_type=F32)

        def finish(r):
            for e in range(n_extra):
                r = r + refs[2 + e][...].astype(F32)
            o_ref[...] = r.astype(o_ref.dtype)

        if nk == 1:
            finish(part)
        else:
            @pl.when(k == 0)
            def _():
                acc[...] = part

            @pl.when(k > 0)
            def _():
                acc[...] += part

            @pl.when(k == nk - 1)
            def _():
                finish(acc[...])

    res = pl.pallas_call(
        body, name=name, grid=grid,
        in_specs=[a_spec, b_spec] + [sp for _, sp in extra] + [_ANY] * hi,
        out_specs=[o_spec] + [_ANY] * ho, out_shape=[out_shape] + hook.outs,
        scratch_shapes=[pltpu.VMEM(acc_shape if nk > 1 else (8, 128), F32)] + hook.sems,
        compiler_params=_params(("arbitrary", "arbitrary", "arbitrary")),
    )(a, b, *[arr for arr, _ in extra], *hook.ins)
    return res[0] if hook is _NO_HOOK else (res[0], res[1:])


NN, NT, TN = (((1,), (0,)), ((), ())), (((1,), (1,)), ((), ())), (((0,), (0,)), ((), ()))


FULL_K = 2048


def _mm(a, b, mode, *, name, out_dtype=F32, a_win=None, add=None, bias=None, hook=None, b_dev=False,
        out_dest=False):
    b_parts = 0
    if mode == "tn":
        kdim, n = a.shape[0], b.shape[1]
        m = a.shape[1] if a_win is None else a_win[1]
        tm, tn, tk = _tile(m, 512, 128), _tile(n, 640, 128), kdim
        off = 0 if a_win is None else a_win[0] // tm
        a_spec = pl.BlockSpec((tk, tm), lambda i, j, k: (k, i + off))
        b_spec = pl.BlockSpec((tk, tn), lambda i, j, k: (k, j))
        dims = TN
    else:
        m = a.shape[0]
        kdim = a.shape[1] if a_win is None else a_win[1]
        if b_dev:
            n = b.shape[0] * b.shape[2] if mode == "nn" else b.shape[1]
        else:
            n = b.shape[1] if mode == "nn" else b.shape[0]
        if FULL_K < kdim <= 2 * FULL_K and not b_dev:
            tm, tn, tk = _tile(m, 1408, 8), _tile(n, 1024, 128), _tile(kdim, 1024, 128)
        else:
            tm = _tile(m, 1408 if kdim <= FULL_K else 352, 8)
            tn, tk = _tile(n, 640, 128), kdim
        off = 0 if a_win is None else a_win[0] // tk
        a_spec = pl.BlockSpec((tm, tk), lambda i, j, k: (i, k + off))
        if mode == "nn":
            dims = NN
            if b_dev:
                per = PACK_COLS // tn
                b_spec = pl.BlockSpec((None, tk, tn), lambda i, j, k: (j // per, k, j % per))
            else:
                b_spec = pl.BlockSpec((tk, tn), lambda i, j, k: (k, j))
        else:
            dims = NT
            if b_dev:
                b_parts = kdim // PACK_COLS
                b_spec = pl.BlockSpec((b_parts, tn, PACK_COLS), lambda i, j, k: (0, j, 0))
            else:
                b_spec = pl.BlockSpec((tn, tk), lambda i, j, k: (j, k))
    if a_win is not None:
        assert a_win[0] % (tm if mode == "tn" else tk) == 0
    extra = []
    if add is not None:
        extra.append((add, pl.BlockSpec((tm, tn), lambda i, j, k: (i, j))))
    if bias is not None:
        extra.append((bias, pl.BlockSpec((1, tn), lambda i, j, k: (0, j))))
    if out_dest:
        per = PACK_COLS // tn
        o_spec = pl.BlockSpec((None, tm, tn), lambda i, j, k: (j // per, i, j % per))
        out_shape = jax.ShapeDtypeStruct((n // PACK_COLS, m, PACK_COLS), out_dtype)
    else:
        o_spec = pl.BlockSpec((tm, tn), lambda i, j, k: (i, j))
        out_shape = jax.ShapeDtypeStruct((m, n), out_dtype)
    return _mm_core(a, b, dims=dims, grid=(m // tm, n // tn, kdim // tk), a_spec=a_spec, b_spec=b_spec,
                    o_spec=o_spec, out_shape=out_shape, acc_shape=(tm, tn), name=name, extra=extra, hook=hook,
                    b_parts=b_parts)


def _win(arr, col0=0, width=None, roff=0):
    return (arr, col0, arr.shape[1] if width is None else width, roff)


def _rows(fn, rows, consts, outs, accs, *, name, nrow, tr=CHUNK):
    nr, nc, no = len(rows), len(consts), len(outs)

    def body(*refs):
        i = pl.program_id(0)
        ins = [r[...] for r in refs[:nr + nc]]
        o_refs = refs[nr + nc:nr + nc + no]
        a_refs = refs[nr + nc + no:]
        res_o, res_a = fn(i, *ins)
        for r, v in zip(o_refs, res_o):
            r[...] = v.astype(r.dtype)
        if a_refs:
            @pl.when(i == 0)
            def _():
                for r in a_refs:
                    r[...] = jnp.zeros_like(r)

            for r, v in zip(a_refs, res_a):
                r[...] += v

    in_specs = []
    for (arr, col0, width, roff) in rows:
        assert col0 % width == 0 and arr.shape[0] % tr == 0
        in_specs.append(pl.BlockSpec((tr, width), lambda i, c=col0 // width, ro=roff: (jnp.maximum(i - ro, 0), c)))
    for c in consts:
        in_specs.append(pl.BlockSpec(c.shape, lambda i, nd=c.ndim: (0,) * nd))
    outs = [tuple(o) + (0,) * (3 - len(o)) for o in outs]
    out_specs = [pl.BlockSpec((tr, w), lambda i, ro=ro: (jnp.maximum(i - ro, 0), 0)) for (w, _, ro) in outs]
    out_specs += [pl.BlockSpec(s, lambda i, nd=len(s): (0,) * nd) for s in accs]
    out_shape = [jax.ShapeDtypeStruct((nrow - ro * tr, w), dt) for (w, dt, ro) in outs]
    out_shape += [jax.ShapeDtypeStruct(s, F32) for s in accs]
    res = pl.pallas_call(
        body, name=name, grid=(nrow // tr,), in_specs=in_specs, out_specs=out_specs, out_shape=out_shape,
        compiler_params=_params(("arbitrary",)),
    )(*[r[0] for r in rows], *consts)
    return res[:no], res[no:]


HEADS_PER_STEP = 2


def _scan_specs(xs, cs, ws, ks, chunk_of, hpb):
    specs = []
    for (arr, width, colfn) in xs:
        specs.append(pl.BlockSpec((CHUNK, width * hpb), lambda h, n, f=colfn: (chunk_of(n), f(h * hpb) // hpb)))
    for (arr, width, colfn) in cs:
        specs.append(pl.BlockSpec((CHUNK, width), lambda h, n, f=colfn: (chunk_of(n), f(h))))
    for arr in list(ws) + list(ks):
        specs.append(pl.BlockSpec((hpb, 1, arr.shape[2]), lambda h, n: (h, 0, 0)))
    return specs


def _scan_fwd(fn, xs, cs, ws, ks, *, heads, nchunk, s_shape, out_w, name, pre=None, hook=None,
              hpb=HEADS_PER_STEP):
    nx, ncs, nw = len(xs), len(cs), len(ws)
    hook = _NO_HOOK if hook is None else hook
    hi, ho = len(hook.ins), len(hook.outs)
    hblocks = heads // hpb

    def body(*refs):
        n = pl.program_id(1)
        nin = nx + ncs + nw + len(ks)
        y_ref, sp_ref = refs[nin + hi], refs[nin + hi + 1]
        s_scr = refs[nin + hi + 2 + ho]
        _run_hook(hook, pl.program_id(0) * nchunk + n, hblocks * nchunk, refs[nin:nin + hi],
                  refs[nin + hi + 2:nin + hi + 2 + ho], refs[nin + hi + 3 + ho:])

        @pl.when(n == 0)
        def _():
            s_scr[...] = jnp.zeros_like(s_scr)

        cv = [r[...] for r in refs[nx:nx + ncs]]
        for e in range(hpb):
            state = s_scr[e]
            sp_ref[e, 0] = state
            xv = [r[:, e * w:(e + 1) * w] for r, (_, w, _) in zip(refs[:nx], xs)]
            wv = [r[e] for r in refs[nx + ncs:nx + ncs + nw]]
            kv = [r[e] for r in refs[nx + ncs + nw:nin]]
            if pre is not None:
                xv = pre(xv, cv)
            y, s_new = fn(n, xv, state, cv, wv, kv)
            y_ref[:, e * out_w:(e + 1) * out_w] = y.astype(y_ref.dtype)
            s_scr[e] = s_new

    lp = nchunk * CHUNK
    res = pl.pallas_call(
        body, name=name, grid=(hblocks, nchunk),
        in_specs=_scan_specs(xs, cs, ws, ks, lambda n: n, hpb) + [_ANY] * hi,
        out_specs=[pl.BlockSpec((CHUNK, out_w * hpb), lambda h, n: (n, h)),
                   pl.BlockSpec((hpb, 1) + s_shape, lambda h, n: (h, n, 0, 0))] + [_ANY] * ho,
        out_shape=[jax.ShapeDtypeStruct((lp, heads * out_w), BF16),
                   jax.ShapeDtypeStruct((heads, nchunk) + s_shape, F32)] + hook.outs,
        scratch_shapes=[pltpu.VMEM((hpb,) + s_shape, F32)] + hook.sems,
        compiler_params=_params(("arbitrary", "arbitrary")),
    )(*[t[0] for t in xs], *[t[0] for t in cs], *ws, *ks, *hook.ins)
    return (res[0], res[1]) if hook is _NO_HOOK else (res[0], res[1], res[2:])


def _scan_bwd(fn, xs, cs, ws, ks, dy, sprev, *, heads, nchunk, s_shape, out_w, name, pre=None, post=None,
              hook=None, hpb=HEADS_PER_STEP):
    nx, ncs, nw = len(xs), len(cs), len(ws)
    nin = nx + ncs + nw + len(ks)
    hook = _NO_HOOK if hook is None else hook
    hi, ho = len(hook.ins), len(hook.outs)
    hblocks = heads // hpb

    def body(*refs):
        step = pl.program_id(1)
        n = nchunk - 1 - step
        dy_ref, sp_ref = refs[nin], refs[nin + 1]
        o0 = nin + 2 + hi
        dx_refs = refs[o0:o0 + nx]
        dw_refs = refs[o0 + nx:o0 + nx + nw]
        ds_scr = refs[o0 + nx + nw + ho]
        _run_hook(hook, pl.program_id(0) * nchunk + step, hblocks * nchunk, refs[nin + 2:o0],
                  refs[o0 + nx + nw:o0 + nx + nw + ho], refs[o0 + nx + nw + ho + 1:])

        @pl.when(step == 0)
        def _():
            ds_scr[...] = jnp.zeros_like(ds_scr)
            for r in dw_refs:
                r[...] = jnp.zeros_like(r)

        cv = [r[...] for r in refs[nx:nx + ncs]]
        for e in range(hpb):
            xv = [r[:, e * w:(e + 1) * w] for r, (_, w, _) in zip(refs[:nx], xs)]
            wv = [r[e] for r in refs[nx + ncs:nx + ncs + nw]]
            kv = [r[e] for r in refs[nx + ncs + nw:nin]]
            if pre is not None:
                xv = pre(xv, cv)
            _, vjp = jax.vjp(lambda xs_, s_, ws_, kv=kv: fn(n, xs_, s_, cv, ws_, kv), xv, sp_ref[e, 0], wv)
            dxs, ds_prev, dws = vjp((dy_ref[:, e * out_w:(e + 1) * out_w].astype(F32), ds_scr[e]))
            if post is not None:
                dxs = post(dxs, cv)
            for r, v, (_, w, _) in zip(dx_refs, dxs, xs):
                r[:, e * w:(e + 1) * w] = v.astype(r.dtype)
            for r, v in zip(dw_refs, dws):
                r[e] += v
            ds_scr[e] = ds_prev

    lp = nchunk * CHUNK
    rev = lambda n: nchunk - 1 - n
    in_specs = _scan_specs(xs, cs, ws, ks, rev, hpb)
    in_specs.append(pl.BlockSpec((CHUNK, out_w * hpb), lambda h, n: (rev(n), h)))
    in_specs.append(pl.BlockSpec((hpb, 1) + s_shape, lambda h, n: (h, rev(n), 0, 0)))
    out_specs = [pl.BlockSpec((CHUNK, w * hpb), lambda h, n: (rev(n), h)) for (_, w, _) in xs]
    out_specs += [pl.BlockSpec((hpb, 1, w.shape[2]), lambda h, n: (h, 0, 0)) for w in ws]
    out_shape = [jax.ShapeDtypeStruct((lp, heads * w), BF16) for (_, w, _) in xs]
    out_shape += [jax.ShapeDtypeStruct(w.shape, F32) for w in ws]
    res = pl.pallas_call(
        body, name=name, grid=(hblocks, nchunk), in_specs=in_specs + [_ANY] * hi,
        out_specs=out_specs + [_ANY] * ho, out_shape=out_shape + hook.outs,
        scratch_shapes=[pltpu.VMEM((hpb,) + s_shape, F32)] + hook.sems,
        compiler_params=_params(("arbitrary", "arbitrary")),
    )(*[t[0] for t in xs], *[t[0] for t in cs], *ws, *ks, dy, sprev, *hook.ins)
    if hook is _NO_HOOK:
        return res[:nx], res[nx:]
    return res[:nx], res[nx:nx + nw], res[nx + nw:]


def _iota2(shape, dim):
    return lax.broadcasted_iota(jnp.int32, shape, dim)


def _ret_chunk(n, xs, state, cs, ws, ks):
    q, k, v, z = xs
    (w,), (lg,) = ws, ks
    lgc = lg[:, :1]
    row, col = _iota2((CHUNK, CHUNK), 0), _iota2((CHUNK, CHUNK), 1)
    diff = jnp.maximum(row - col, 0).astype(F32)
    decay = jnp.where(row >= col, jnp.exp(lg * diff), 0.0)
    scores = _bdot(q, k, 1, 1) * decay
    o_intra = _bdot(scores, v, 1, 0)
    idx = _iota2((CHUNK, 1), 0).astype(F32)
    k_w = k * jnp.exp(lgc * (CHUNK - 1.0 - idx))
    kv = _bdot(k_w, v, 0, 0)
    s_new = state * jnp.exp(lgc * float(CHUNK)) + kv
    q_w = q * jnp.exp(lgc * (idx + 1.0))
    o = o_intra + _bdot(q_w, state, 1, 0)
    return _rms(o, w) * _silu(z), s_new


def _rope(t, cos2, sin2):
    return t * cos2 + pltpu.roll(t, RET_DK // 2, 1) * sin2


def _rope_t(g, cos2, sin2):
    return g * cos2 - pltpu.roll(g, RET_DK // 2, 1) * sin2


def _ret_pre(xv, cv):
    q, k, v, z = xv
    cos2, sin2 = cv
    return [_rope(q, cos2, sin2), _rope(k, cos2, sin2) * (RET_DK ** -0.5), v, z]


def _ret_post(dxs, cv):
    dq, dk, dv, dz = dxs
    cos2, sin2 = cv
    return [_rope_t(dq, cos2, sin2), _rope_t(dk, cos2, sin2) * (RET_DK ** -0.5), dv, dz]


def _tri_apply(x, lower):
    n = x.shape[0]
    row, col = _iota2((n, n), 0), _iota2((n, n), 1)
    tri = (row >= col if lower else row <= col).astype(BF16)
    hi = x.astype(BF16)
    rest = x - hi.astype(F32)
    mid = rest.astype(BF16)
    lo = (rest - mid.astype(F32)).astype(BF16)
    return sum(lax.dot_general(tri, p, NN, preferred_element_type=F32) for p in (hi, mid, lo))


@jax.custom_vjp
def _cumsum_rows(x):
    return _tri_apply(x, True)


_cumsum_rows.defvjp(lambda x: (_tri_apply(x, True), None), lambda _, g: (_tri_apply(g, False),))


def _gla_chunk(n, xs, state_t, cs, ws, ks):
    q, k, v, z, pre = xs
    (w,) = ws
    q = q * (GLA_DK ** -0.5)
    rowc = _iota2((CHUNK, 1), 0)
    valid = jnp.logical_or(n > 0, rowc >= PAD)
    log_a = jnp.where(valid, _log_sigmoid(pre) / GLA_TAU, 0.0)
    b = _cumsum_rows(log_a)
    b_last = b[CHUNK - 1:CHUNK, :]
    kv_t = _bdot(v, k * jnp.exp(b_last - b), 0, 0)
    s_new = state_t * jnp.exp(b_last) + kv_t
    o_inter = _bdot(q * jnp.exp(b), state_t, 1, 1)
    outs = []
    for s in range(CHUNK // SUB):
        lo, hi = s * SUB, (s + 1) * SUB
        b_ref = jnp.zeros_like(b_last) if s == 0 else b[lo - 1:lo, :]
        q_hat = q[lo:hi] * jnp.exp(b[lo:hi] - b_ref)
        k_hat = k[:hi] * jnp.exp(b_ref - b[:hi])
        sc = _bdot(q_hat, k_hat, 1, 1)
        causal = _iota2((SUB, hi), 0) + lo >= _iota2((SUB, hi), 1)
        outs.append(_bdot(jnp.where(causal, sc, 0.0), v[:hi], 1, 0))
    o = jnp.concatenate(outs, axis=0) + o_inter
    return _rms(o, w) * _silu(z), s_new


def _s5_disc(lam_re, lam_im, log_dt, b_re, b_im, expand):
    dt = jnp.exp(log_dt)
    mag = jnp.exp(lam_re * dt)
    ab_re, ab_im = mag * jnp.cos(lam_im * dt), mag * jnp.sin(lam_im * dt)
    den = lam_re * lam_re + lam_im * lam_im
    nr, ni = ab_re - 1.0, ab_im
    f_re = (nr * lam_re + ni * lam_im) / den
    f_im = (ni * lam_re - nr * lam_im) / den
    hp = lax.Precision.HIGHEST
    f_re = jnp.dot(f_re, expand, precision=hp, preferred_element_type=F32)
    f_im = jnp.dot(f_im, expand, precision=hp, preferred_element_type=F32)
    return ab_re, ab_im, f_re * b_re - f_im * b_im, f_re * b_im + f_im * b_re


def _s5_disc_fwd(args):
    def body(*refs):
        outs = _s5_disc(*[r[...] for r in refs[:6]])
        for r, v in zip(refs[6:], outs):
            r[...] = v

    g, p = args[0].shape
    return pl.pallas_call(
        body, name="s5_disc_fwd",
        out_shape=[jax.ShapeDtypeStruct((g, p), F32)] * 2 + [jax.ShapeDtypeStruct(args[3].shape, F32)] * 2,
    )(*args)


def _s5_disc_bwd(args, cts):
    def body(*refs):
        prim = [r[...] for r in refs[:5]]
        expand = refs[5][...]
        ct = tuple(r[...] for r in refs[6:10])
        _, vjp = jax.vjp(lambda *a: _s5_disc(*a, expand), *prim)
        for r, v in zip(refs[10:], vjp(ct)):
            r[...] = v

    return pl.pallas_call(
        body, name="s5_disc_bwd", out_shape=[jax.ShapeDtypeStruct(a.shape, F32) for a in args[:5]],
    )(*args, *cts)


SCAN_ROWS, SCAN_LANES = 32, 128
TILE_G = 8
TILE_W = TILE_G * S5_P
S5_TB = 64


def _s5_scan_fwd(bu, a_re, a_im):
    lp = bu.shape[0]

    def body(bu_ref, ar_ref, ai_ref, x_ref, st):
        @pl.when(pl.program_id(0) == 0)
        def _():
            st[...] = jnp.zeros_like(st)

        ar, ai = ar_ref[...], ai_ref[...]

        def step(t, carry):
            xr, xi = carry
            nr = ar * xr - ai * xi + bu_ref[t, 0:SCAN_ROWS, :]
            ni = ar * xi + ai * xr + bu_ref[t, SCAN_ROWS:2 * SCAN_ROWS, :]
            x_ref[t, 0:SCAN_ROWS, :] = nr
            x_ref[t, SCAN_ROWS:2 * SCAN_ROWS, :] = ni
            return nr, ni

        xr, xi = lax.fori_loop(0, S5_TB, step, (st[0], st[1]))
        st[0] = xr
        st[1] = xi

    blk = pl.BlockSpec((S5_TB, 2 * SCAN_ROWS, SCAN_LANES), lambda i: (i, 0, 0))
    cst = pl.BlockSpec((SCAN_ROWS, SCAN_LANES), lambda i: (0, 0))
    return pl.pallas_call(
        body, name="s5_scan_fwd", grid=(lp // S5_TB,), in_specs=[blk, cst, cst], out_specs=blk,
        out_shape=jax.ShapeDtypeStruct(bu.shape, F32),
        scratch_shapes=[pltpu.VMEM((2, SCAN_ROWS, SCAN_LANES), F32)],
        compiler_params=_params(("arbitrary",)),
    )(bu, a_re, a_im)


def _s5_expand(a, w_t, *, a_blk, dims, name, hook=None):
    lp, nt = a.shape[0], w_t.shape[0]
    tm = _tile(lp, 176, 8)
    steps = lp // tm
    rows3 = 2 * SCAN_ROWS
    per = TILE_W // SCAN_LANES
    hook = _NO_HOOK if hook is None else hook
    hi, ho = len(hook.ins), len(hook.outs)

    def body(*refs):
        a_ref, w_ref, o_ref = refs[0], refs[1], refs[2 + hi]
        _run_hook(hook, pl.program_id(0), steps, refs[2:2 + hi], refs[3 + hi:3 + hi + ho], refs[3 + hi + ho:])
        for j in range(nt):
            s = j % TILE_G
            r = lax.dot_general(a_ref[:, 128 * s:128 * (s + 1)].astype(BF16), w_ref[j], dims,
                                preferred_element_type=F32)
            for c in range(per):
                o_ref[pl.ds(per * j + c, tm, stride=rows3), :] = r[:, SCAN_LANES * c:SCAN_LANES * (c + 1)]

    res = pl.pallas_call(
        body, name=name, grid=(steps,),
        in_specs=[pl.BlockSpec((tm, S5_W), lambda i: (i, a_blk)), pl.BlockSpec(w_t.shape, lambda i: (0, 0, 0))]
        + [_ANY] * hi,
        out_specs=[pl.BlockSpec((tm * rows3, SCAN_LANES), lambda i: (i, 0))] + [_ANY] * ho,
        out_shape=[jax.ShapeDtypeStruct((lp * rows3, SCAN_LANES), F32)] + hook.outs,
        scratch_shapes=hook.sems, compiler_params=_params(("arbitrary",)),
    )(a, w_t, *hook.ins)
    out3 = res[0].reshape(lp, rows3, SCAN_LANES)
    return out3 if hook is _NO_HOOK else (out3, res[1:])


def _s5_scan_bwd(gx, x, a_re, a_im):
    lp = gx.shape[0]
    nb = lp // S5_TB

    def body(gx_ref, x_ref, xp_ref, ar_ref, ai_ref, g_ref, da_ref, st):
        i = pl.program_id(0)

        @pl.when(i == 0)
        def _():
            st[...] = jnp.zeros_like(st)
            da_ref[...] = jnp.zeros_like(da_ref)

        ar, ai = ar_ref[...], ai_ref[...]
        first = (i == nb - 1).astype(F32)

        def step(s, carry):
            gr, gi, dar, dai = carry
            t = S5_TB - 1 - s
            ngr = gx_ref[t, 0:SCAN_ROWS, :] + ar * gr + ai * gi
            ngi = gx_ref[t, SCAN_ROWS:2 * SCAN_ROWS, :] + ar * gi - ai * gr
            g_ref[t, 0:SCAN_ROWS, :] = ngr
            g_ref[t, SCAN_ROWS:2 * SCAN_ROWS, :] = ngi
            tp = jnp.maximum(t - 1, 0)
            at0 = (t == 0).astype(F32)
            keep = 1.0 - at0
            pr = keep * x_ref[tp, 0:SCAN_ROWS, :] + at0 * (1.0 - first) * xp_ref[0, 0:SCAN_ROWS, :]
            pi = keep * x_ref[tp, SCAN_ROWS:2 * SCAN_ROWS, :] + at0 * (1.0 - first) * xp_ref[0, SCAN_ROWS:2 * SCAN_ROWS, :]
            return ngr, ngi, dar + ngr * pr + ngi * pi, dai + ngi * pr - ngr * pi

        zero = jnp.zeros((SCAN_ROWS, SCAN_LANES), F32)
        gr, gi, dar, dai = lax.fori_loop(0, S5_TB, step, (st[0], st[1], zero, zero))
        st[0] = gr
        st[1] = gi
        da_ref[0] += dar
        da_ref[1] += dai

    rev = lambda i: nb - 1 - i
    blk = pl.BlockSpec((S5_TB, 2 * SCAN_ROWS, SCAN_LANES), lambda i: (rev(i), 0, 0))
    prev = pl.BlockSpec((1, 2 * SCAN_ROWS, SCAN_LANES), lambda i: (jnp.maximum(rev(i) * S5_TB - 1, 0), 0, 0))
    cst = pl.BlockSpec((SCAN_ROWS, SCAN_LANES), lambda i: (0, 0))
    return pl.pallas_call(
        body, name="s5_scan_bwd", grid=(nb,), in_specs=[blk, blk, prev, cst, cst],
        out_specs=[blk, pl.BlockSpec((2, SCAN_ROWS, SCAN_LANES), lambda i: (0, 0, 0))],
        out_shape=[jax.ShapeDtypeStruct(gx.shape, F32), jax.ShapeDtypeStruct((2, SCAN_ROWS, SCAN_LANES), F32)],
        scratch_shapes=[pltpu.VMEM((2, SCAN_ROWS, SCAN_LANES), F32)],
        compiler_params=_params(("arbitrary",)),
    )(gx, x, x, a_re, a_im)


def _place():
    x, y, c = lax.axis_index("x"), lax.axis_index("y"), lax.axis_index("c")
    return x, y, c, [(1 - x, y), (x, 1 - y), (1 - x, 1 - y)]


def _gather_phases():
    def plan(x_ref, out_ref, send_sems, recv_sems, local_sem):
        x, y, c, chips = _place()
        me, sibling = (x, y, c), (x, y, 1 - c)

        def rows(px, py, pc):
            return out_ref.at[4 * px + 2 * py + pc]

        def copy(k, block, to, src=None):
            return pltpu.make_async_remote_copy(
                src_ref=rows(*block) if src is None else src, dst_ref=rows(*block),
                send_sem=send_sems.at[k], recv_sem=recv_sems.at[k], device_id=to, device_id_type=MESH)

        mine = pltpu.make_async_copy(x_ref, rows(*me), local_sem)
        first = [copy(0, me, sibling, src=x_ref)]
        first += [copy(1 + j, me, (*chip, c), src=x_ref) for j, chip in enumerate(chips)]
        passed = [copy(4 + j, (*chip, c), sibling) for j, chip in enumerate(chips)]
        return c, chips, me, sibling, copy, mine, first, passed

    def start(ins, outs, sems):
        _, _, _, _, _, mine, first, _ = plan(ins[0], outs[0], *sems)
        mine.start()
        for cp in first:
            cp.start()

    def middle(ins, outs, sems):
        c, chips, me, _, copy, _, _, passed = plan(ins[0], outs[0], *sems)
        for j, chip in enumerate(chips):
            copy(1 + j, (*chip, c), me).wait_recv()
            passed[j].start()

    def finish(ins, outs, sems):
        c, chips, me, sibling, copy, mine, first, passed = plan(ins[0], outs[0], *sems)
        copy(0, sibling, me).wait_recv()
        for j, chip in enumerate(chips):
            copy(4 + j, (*chip, 1 - c), me).wait_recv()
        for cp in first + passed:
            cp.wait_send()
        mine.wait()

    return start, middle, finish


_GATHER_SEMS = [pltpu.SemaphoreType.DMA((7,)), pltpu.SemaphoreType.DMA((7,)), pltpu.SemaphoreType.DMA]


def _all_gather(shard, name):
    phases = _gather_phases()

    def body(x_ref, out_ref, *sems):
        for phase in phases:
            phase([x_ref], [out_ref], sems)

    return pl.pallas_call(
        body, name=name, out_shape=jax.ShapeDtypeStruct((N_DEV,) + shard.shape, shard.dtype),
        in_specs=[_ANY], out_specs=_ANY, scratch_shapes=list(_GATHER_SEMS),
    )(shard)


def _gather_hook(shard):
    start, middle, finish = _gather_phases()
    return _Hook([shard], [jax.ShapeDtypeStruct((N_DEV,) + shard.shape, shard.dtype)], _GATHER_SEMS,
                 [(0.0, start), (0.85, middle), (1.0, finish)])


def _swap_with_sibling(parts, name):
    def body(p_ref, out_ref, send_sems, recv_sems):
        x, y, c, _ = _place()
        copies = [pltpu.make_async_remote_copy(
            src_ref=p_ref.at[2 * chip + (1 - c)], dst_ref=out_ref.at[chip],
            send_sem=send_sems.at[chip], recv_sem=recv_sems.at[chip],
            device_id=(x, y, 1 - c), device_id_type=MESH) for chip in range(4)]
        for cp in copies:
            cp.start()
        for cp in copies:
            cp.wait()

    return pl.pallas_call(
        body, name=name, out_shape=jax.ShapeDtypeStruct((4,) + parts.shape[1:], parts.dtype),
        in_specs=[pl.BlockSpec(memory_space=pl.ANY)], out_specs=pl.BlockSpec(memory_space=pl.ANY),
        scratch_shapes=[pltpu.SemaphoreType.DMA((4,)), pltpu.SemaphoreType.DMA((4,))],
    )(parts)


def _chips_phases(lo, rows):
    def copies(p_ref, out_ref, send_sems, recv_sems):
        x, y, c, chips = _place()
        return [pltpu.make_async_remote_copy(
            src_ref=p_ref.at[2 * px + py, pl.ds(lo, rows)], dst_ref=out_ref.at[j],
            send_sem=send_sems.at[j], recv_sem=recv_sems.at[j],
            device_id=(px, py, c), device_id_type=MESH) for j, (px, py) in enumerate(chips)]

    def start(ins, outs, sems):
        for cp in copies(ins[0], outs[0], *sems):
            cp.start()

    def finish(ins, outs, sems):
        for cp in copies(ins[0], outs[0], *sems):
            cp.wait()

    return start, finish


def _chips_hook(parts, lo=0, hi=None):
    rows = (parts.shape[1] if hi is None else hi) - lo
    start, finish = _chips_phases(lo, rows)
    return _Hook([parts], [jax.ShapeDtypeStruct((3, rows) + parts.shape[2:], parts.dtype)],
                 [pltpu.SemaphoreType.DMA((3,)), pltpu.SemaphoreType.DMA((3,))], [(0.0, start), (1.0, finish)])


BIG_LAYOUT = (("w_in_ab", D_MODEL, PACK_COLS), ("s5_w_glu", S5_W // N_DEV, PACK_COLS),
              ("w_out_ab", OUT_AB // N_DEV, 2 * PACK_COLS), ("w_in_c", D_MODEL, PACK_COLS),
              ("w_out_c", GLA_W // N_DEV, 2 * PACK_COLS))


def _to_rows(a):
    if a.shape[-1] == PACK_COLS:
        return a
    assert a.shape[-1] == 2 * PACK_COLS
    return jnp.concatenate([a[..., :PACK_COLS], a[..., PACK_COLS:]], axis=-2)


def _from_rows(p, cols):
    if cols == PACK_COLS:
        return p
    r = p.shape[-2] // 2
    return jnp.concatenate([p[..., :r, :], p[..., r:, :]], axis=-1)


FIRST_LAYOUT = BIG_LAYOUT[:1]
OTHER_LAYOUT = BIG_LAYOUT[1:3] + BIG_LAYOUT[4:]
GLU_AB_LAYOUT = BIG_LAYOUT[1:3]
IN_C_LAYOUT = BIG_LAYOUT[3:4]


def _pack_big(pieces, layout):
    return jnp.concatenate([_to_rows(pieces[name]) for name, _, _ in layout], axis=-2)


def _unpack_big(buf, layout):
    out, o = {}, 0
    for name, rows, cols in layout:
        r = rows * cols // PACK_COLS
        out[name] = _from_rows(buf[..., o:o + r, :], cols)
        o += r
    return out


def _column_windows(g):
    rows, quarter = g.shape[0], WIN_COLS // 4

    def body(g_ref, o_ref):
        o_ref[...] = g_ref[...]

    return pl.pallas_call(
        body, name="w_in_c_grad_windows", grid=(N_DEV, WIN_COLS // quarter),
        in_specs=[pl.BlockSpec((rows, quarter), lambda d, c: (0, (WIN_STEP // quarter) * d + c))],
        out_specs=pl.BlockSpec((None, rows, quarter), lambda d, c: (d, 0, c)),
        out_shape=jax.ShapeDtypeStruct((N_DEV, rows, WIN_COLS), g.dtype),
        compiler_params=_params(("arbitrary", "arbitrary")),
    )(g)


def _rows1024(a):
    r, c = a.shape
    if c > PACK_COLS:
        a = jnp.concatenate([a[:, i * PACK_COLS:(i + 1) * PACK_COLS] for i in range(c // PACK_COLS)], axis=0)
    elif c < PACK_COLS:
        a = jnp.pad(a, ((0, 0), (0, PACK_COLS - c)))
    return jnp.pad(a, ((0, -a.shape[0] % 8), (0, 0)))


def _unrows1024(p, r, c):
    if c > PACK_COLS:
        return jnp.concatenate([p[i * r:(i + 1) * r] for i in range(c // PACK_COLS)], axis=1)
    return p[:r, :c]


def _lane_select(a, off, sign, n_out, out_dtype, exact, name):
    rows, n_in = a.shape
    tr = _tile(rows, 256, 16)

    def body(off_ref, a_ref, o_ref):
        sel = _iota2((n_in, n_out), 0) + off_ref[0] * sign == _iota2((n_in, n_out), 1)
        if exact:
            r = jnp.dot(a_ref[...], sel.astype(F32), precision=lax.Precision.HIGHEST, preferred_element_type=F32)
        else:
            r = _dg(a_ref[...], sel.astype(BF16), 1, 0)
        o_ref[...] = r.astype(out_dtype)

    return pl.pallas_call(
        body, name=name, grid=(rows // tr,),
        in_specs=[pl.BlockSpec(memory_space=pltpu.SMEM), pl.BlockSpec((tr, n_in), lambda i: (i, 0))],
        out_specs=pl.BlockSpec((tr, n_out), lambda i: (i, 0)),
        out_shape=jax.ShapeDtypeStruct((rows, n_out), out_dtype),
        compiler_params=_params(("arbitrary",)),
    )(off, a)


def _adamw(w, g, m, v, name):
    rows, cols = w.shape
    tr = _tile(rows, 256, 8) if rows % 8 == 0 else rows

    def fn(i, w_, g_, m_, v_):
        m_new = ADAM_B1 * m_ + (1.0 - ADAM_B1) * g_
        v_new = ADAM_B2 * v_ + (1.0 - ADAM_B2) * (g_ * g_)
        m_hat = m_new / (1.0 - ADAM_B1 ** ADAM_STEP)
        v_hat = v_new / (1.0 - ADAM_B2 ** ADAM_STEP)
        delta = -ADAM_LR * (m_hat / (jnp.sqrt(v_hat) + ADAM_EPS) + ADAM_WD * w_)
        return (delta, m_new, v_new), ()

    outs, _ = _rows(fn, [_win(w), _win(g), _win(m), _win(v)], [], [(cols, F32)] * 3, [], name=name,
                    nrow=rows, tr=tr)
    return outs


def _as2d(a):
    if a.ndim == 1:
        return a.reshape(1, -1)
    if a.ndim == 2:
        return a
    a = a.reshape(a.shape[1:])
    return a if a.ndim == 2 else a.reshape(a.shape[0], -1)


def kernel(x, meta, norm_ab_w, w_in_ab, ret_norm_w, s5_lam_re, s5_lam_im, s5_log_dt, s5_b_re, s5_b_im, s5_c_re, s5_c_im, s5_d, s5_w_glu, w_out_ab, norm_c_w, w_in_c, gla_w_gate, gla_b_gate, gla_norm_w, w_out_c, final_norm_w, loss_target, m_meta, m_norm_ab_w, m_w_in_ab, m_ret_norm_w, m_s5_lam_re, m_s5_lam_im, m_s5_log_dt, m_s5_b_re, m_s5_b_im, m_s5_c_re, m_s5_c_im, m_s5_d, m_s5_w_glu, m_w_out_ab, m_norm_c_w, m_w_in_c, m_gla_w_gate, m_gla_b_gate, m_gla_norm_w, m_w_out_c, m_final_norm_w, v_meta, v_norm_ab_w, v_w_in_ab, v_ret_norm_w, v_s5_lam_re, v_s5_lam_im, v_s5_log_dt, v_s5_b_re, v_s5_b_im, v_s5_c_re, v_s5_c_im, v_s5_d, v_s5_w_glu, v_w_out_ab, v_norm_c_w, v_w_in_c, v_gla_w_gate, v_gla_b_gate, v_gla_norm_w, v_w_out_c, v_final_norm_w):
    weights = dict(meta=meta, norm_ab_w=norm_ab_w, w_in_ab=w_in_ab, ret_norm_w=ret_norm_w, s5_lam_re=s5_lam_re,
                   s5_lam_im=s5_lam_im, s5_log_dt=s5_log_dt, s5_b_re=s5_b_re, s5_b_im=s5_b_im, s5_c_re=s5_c_re,
                   s5_c_im=s5_c_im, s5_d=s5_d, s5_w_glu=s5_w_glu, w_out_ab=w_out_ab, norm_c_w=norm_c_w,
                   w_in_c=w_in_c, gla_w_gate=gla_w_gate, gla_b_gate=gla_b_gate, gla_norm_w=gla_norm_w,
                   w_out_c=w_out_c, final_norm_w=final_norm_w)
    mom_m = dict(meta=m_meta, norm_ab_w=m_norm_ab_w, w_in_ab=m_w_in_ab, ret_norm_w=m_ret_norm_w,
                 s5_lam_re=m_s5_lam_re, s5_lam_im=m_s5_lam_im, s5_log_dt=m_s5_log_dt, s5_b_re=m_s5_b_re,
                 s5_b_im=m_s5_b_im, s5_c_re=m_s5_c_re, s5_c_im=m_s5_c_im, s5_d=m_s5_d, s5_w_glu=m_s5_w_glu,
                 w_out_ab=m_w_out_ab, norm_c_w=m_norm_c_w, w_in_c=m_w_in_c, gla_w_gate=m_gla_w_gate,
                 gla_b_gate=m_gla_b_gate, gla_norm_w=m_gla_norm_w, w_out_c=m_w_out_c, final_norm_w=m_final_norm_w)
    mom_v = dict(meta=v_meta, norm_ab_w=v_norm_ab_w, w_in_ab=v_w_in_ab, ret_norm_w=v_ret_norm_w,
                 s5_lam_re=v_s5_lam_re, s5_lam_im=v_s5_lam_im, s5_log_dt=v_s5_log_dt, s5_b_re=v_s5_b_re,
                 s5_b_im=v_s5_b_im, s5_c_re=v_s5_c_re, s5_c_im=v_s5_c_im, s5_d=v_s5_d, s5_w_glu=v_s5_w_glu,
                 w_out_ab=v_w_out_ab, norm_c_w=v_norm_c_w, w_in_c=v_w_in_c, gla_w_gate=v_gla_w_gate,
                 gla_b_gate=v_gla_b_gate, gla_norm_w=v_gla_norm_w, w_out_c=v_w_out_c, final_norm_w=v_final_norm_w)
    order = list(weights)

    seq = x.shape[1]
    lp = CHUNK + seq
    nchunk = lp // CHUNK
    dev = 4 * lax.axis_index("x") + 2 * lax.axis_index("y") + lax.axis_index("c")
    core = lax.axis_index("c")
    chip = 2 * lax.axis_index("x") + lax.axis_index("y")

    win_off = jnp.reshape(2 * dev, (1,)).astype(jnp.int32)
    shard_c = jnp.pad(w_in_c[0].astype(BF16), ((0, 0), (0, 896 - SHARD_C)))
    big_shards = dict(w_in_ab=w_in_ab[0].astype(BF16), s5_w_glu=s5_w_glu[0].astype(BF16),
                      w_out_ab=w_out_ab[0].astype(BF16), w_out_c=w_out_c[0].astype(BF16),
                      w_in_c=_lane_select(shard_c, win_off, 1, WIN_COLS, BF16, False, "w_in_c_to_window"))
    def pad_to(a, rows, cols):
        return jnp.pad(a, ((0, rows - a.shape[0]), (0, cols - a.shape[1])))

    shard_w = D_MODEL // N_DEV
    small_pack = jnp.concatenate([meta, pad_to(norm_c_w, 8, shard_w), pad_to(gla_w_gate[0], GLA_RANK, shard_w),
                                  pad_to(gla_b_gate, 8, shard_w), pad_to(gla_norm_w, 8, shard_w)], axis=0)
    w_in_ab_g = _all_gather(big_shards["w_in_ab"], "gather_first")
    win_cut = 1408
    in_c_hook_a = _gather_hook(big_shards["w_in_c"][:win_cut])
    in_c_hook_b = _gather_hook(big_shards["w_in_c"][win_cut:])
    glu_ab_hook = _gather_hook(_pack_big(big_shards, GLU_AB_LAYOUT))
    out_c_hook = _gather_hook(_to_rows(big_shards["w_out_c"]))
    gs = _all_gather(small_pack, "gather_small")
    gate_w = GLA_QK // N_DEV
    s_meta, s_norm_c = gs[:, :N_META], gs[:, N_META]
    s_wgate, s_bgate, s_gnorm = gs[:, 24:24 + GLA_RANK, :gate_w], gs[:, 40, :gate_w], gs[:, 48]
    meta_f = s_meta.transpose(1, 0, 2).reshape(N_META, D_MODEL)
    norm_c_f = s_norm_c.reshape(1, D_MODEL)
    w_gate_f = jnp.pad(s_wgate.transpose(1, 0, 2).reshape(GLA_RANK, GLA_QK), ((0, GATE_PAD - GLA_RANK), (0, 0)))
    b_gate_f = s_bgate.reshape(1, GLA_QK)
    gla_norm_f = s_gnorm.reshape(GLA_H, 1, GLA_DV)

    pos = jnp.maximum(jnp.arange(lp, dtype=F32) - float(PAD), 0.0)
    inv_freq = jnp.power(ROPE_BASE, -jnp.arange(0, RET_DK, 2, dtype=F32) / RET_DK)
    ang = pos[:, None] * inv_freq[None, :]
    cos2 = jnp.concatenate([jnp.cos(ang), jnp.cos(ang)], axis=1)
    sin2 = jnp.concatenate([-jnp.sin(ang), jnp.sin(ang)], axis=1)
    log_g = jnp.log1p(-jnp.exp2(-5.0 - jnp.arange(RET_H, dtype=F32)))
    lg = jnp.broadcast_to(log_g[:, None, None], (RET_H, 1, 128))
    ret_norm_h = ret_norm_w.reshape(RET_H, 1, RET_DV)

    h0 = jnp.concatenate([jnp.zeros((PAD, D_MODEL), F32), meta_f, x[0]], axis=0)

    def rowmask(i):
        return (_iota2((CHUNK, 1), 0) + i * CHUNK) >= PAD

    (hn0,), _ = _rows(lambda i, h, w: ((_rms(h, w),), ()), [_win(h0)], [norm_ab_w], [(D_MODEL, BF16)], [],
                      name="norm_ab_fwd", nrow=lp)
    proj_ab, (w_in_c_ga,) = _mm(hn0, w_in_ab_g, "nn", name="in_ab_fwd", hook=in_c_hook_a, b_dev=True)

    q_off, k_off, v_off, za_off = 0, RET_QK, 2 * RET_QK, 2 * RET_QK + RET_W
    u_off, zb_off = 2 * RET_QK + 2 * RET_W, 2 * RET_QK + 2 * RET_W + S5_W
    ret_xs = [(proj_ab, RET_DK, lambda h: q_off // RET_DK + h), (proj_ab, RET_DK, lambda h: k_off // RET_DK + h),
              (proj_ab, RET_DV, lambda h: v_off // RET_DV + h), (proj_ab, RET_DV, lambda h: za_off // RET_DV + h)]
    ret_cs = [(cos2, RET_DK, lambda h: 0), (sin2, RET_DK, lambda h: 0)]
    ret_kw = dict(heads=RET_H, nchunk=nchunk, s_shape=(RET_DK, RET_DV), out_w=RET_DV, pre=_ret_pre, hpb=4)
    o_a, ret_sprev, (gathered_glu_ab,) = _scan_fwd(_ret_chunk, ret_xs, ret_cs, [ret_norm_h], [lg], name="ret_fwd",
                                                   hook=glu_ab_hook, **ret_kw)
    gb = _unpack_big(gathered_glu_ab, GLU_AB_LAYOUT)
    w_glu_f = gb["s5_w_glu"].reshape(S5_W, S5_W)
    w_out_ab_f = gb["w_out_ab"].reshape(OUT_AB, D_MODEL)

    expand = jnp.repeat(jnp.eye(S5_P, dtype=F32), S5_GH, axis=1)
    disc_args = (s5_lam_re[0], s5_lam_im[0], s5_log_dt[0].reshape(S5_G, 1),
                 s5_b_re[0].reshape(S5_G, S5_P * S5_GH), s5_b_im[0].reshape(S5_G, S5_P * S5_GH), expand)
    ab_re, ab_im, bb_re, bb_im = _s5_disc_fwd(disc_args)
    gt = TILE_G
    eye_t = jnp.eye(gt, dtype=F32)

    def tiles_in(bb):
        return jnp.einsum("sgph,gk->sghkp", bb.reshape(gt, gt, S5_P, S5_GH), eye_t).reshape(gt, 128, TILE_W)

    def tiles_out(cc):
        return jnp.einsum("sghp,gk->sgpkh", cc.reshape(gt, gt, S5_GH, S5_P), eye_t).reshape(gt, TILE_W, 128)

    wb_t = jnp.concatenate([tiles_in(bb_re), tiles_in(bb_im)], axis=0).astype(BF16)
    wc_t = jnp.concatenate([tiles_out(s5_c_re[0]), -tiles_out(s5_c_im[0])], axis=0).astype(BF16)
    a_re, a_im = ab_re.reshape(SCAN_ROWS, SCAN_LANES), ab_im.reshape(SCAN_ROWS, SCAN_LANES)
    tm5, tk5, nt5 = _tile(lp, 1408, 8), _tile(lp, 1408, 8), 2 * gt
    u_blk = u_off // 128
    wide_k = pl.BlockSpec((tm5, TILE_W), lambda i, j, k: (i, k * gt + j))
    narrow = pl.BlockSpec((tm5, 128), lambda i, j, k: (i, j))
    wb_k = pl.BlockSpec((None, 128, TILE_W), lambda i, j, k: (k * gt + j, 0, 0))
    wc_k = pl.BlockSpec((None, TILE_W, 128), lambda i, j, k: (k * gt + j, 0, 0))
    bu3, (w_out_c_g,) = _s5_expand(proj_ab, wb_t, a_blk=u_off // S5_W, dims=NN, name="s5_bu", hook=out_c_hook)
    w_out_c_f = _from_rows(w_out_c_g, D_MODEL).reshape(GLA_W, D_MODEL)
    xs5 = _s5_scan_fwd(bu3, a_re, a_im)
    xs5_2d = xs5.reshape(lp, 2 * S5_N)
    y_pre, (w_in_c_gb,) = _mm_core(xs5_2d, wc_t, dims=NN, grid=(lp // tm5, gt, 2), name="s5_cx", a_spec=wide_k,
                                   b_spec=wc_k, o_spec=narrow, out_shape=jax.ShapeDtypeStruct((lp, S5_W), F32),
                                   acc_shape=(tm5, 128), hook=in_c_hook_b)
    (y_s5, yg_bf), _ = _rows(
        lambda i, yp, u, d: ((yp + d * u, _gelu(yp + d * u)), ()),
        [_win(y_pre), _win(proj_ab, u_off, S5_W)], [s5_d], [(S5_W, F32), (S5_W, BF16)], [], name="s5_gelu_fwd", nrow=lp)
    t_glu = _mm(yg_bf, w_glu_f, "nn", name="s5_glu_fwd")

    def s5_gate(y, t, zb):
        return _gelu(y) * _sigmoid(t) * _silu(zb)

    (o_b,), _ = _rows(lambda i, y, t, zb: ((s5_gate(y, t, zb),), ()),
                      [_win(y_s5), _win(t_glu), _win(proj_ab, zb_off, S5_W)], [], [(S5_W, BF16)], [],
                      name="s5_gate_fwd", nrow=lp)
    o_ab = jnp.concatenate([o_a, o_b], axis=1)
    h1 = _mm(o_ab, w_out_ab_f, "nn", name="out_ab_fwd", add=h0)
    w_in_c_g = jnp.concatenate([w_in_c_ga, w_in_c_gb], axis=1)
    w_in_c_f = sum(jnp.pad(w_in_c_g[d], ((0, 0), (WIN_STEP * d, IN_C_PAD - WIN_STEP * d - WIN_COLS)))
                   for d in range(N_DEV))

    (hn1,), _ = _rows(lambda i, h, w: ((_rms(h, w),), ()), [_win(h1)], [norm_c_f], [(D_MODEL, BF16)], [],
                      name="norm_c_fwd", nrow=lp)
    proj_c = _mm(hn1, w_in_c_f, "nn", name="in_c_fwd")
    gl_off = 2 * GLA_QK + 2 * GLA_W
    pre_gate = _mm(proj_c, w_gate_f, "nn", name="gate_fwd", a_win=(gl_off, GATE_PAD), bias=b_gate_f)
    gla_xs = [(proj_c, GLA_DK, lambda h: h), (proj_c, GLA_DK, lambda h: GLA_QK // GLA_DK + h),
              (proj_c, GLA_DV, lambda h: 2 * GLA_QK // GLA_DV + h),
              (proj_c, GLA_DV, lambda h: (2 * GLA_QK + GLA_W) // GLA_DV + h),
              (pre_gate, GLA_DK, lambda h: h)]
    gla_kw = dict(heads=GLA_H, nchunk=nchunk, s_shape=(GLA_DV, GLA_DK), out_w=GLA_DV, hpb=GLA_H)
    o_c, gla_sprev = _scan_fwd(_gla_chunk, gla_xs, [], [gla_norm_f], [], name="gla_fwd", **gla_kw)
    h2 = _mm(o_c, w_out_c_f, "nn", name="out_c_fwd", add=h1)

    fnw = final_norm_w.reshape(1, D_MODEL)

    def final_fn(i, h, tgt, w):
        def loss_of(h_, w_):
            err = _rms(h_, w_) - tgt
            return 0.5 * jnp.sum(jnp.mean(err * err, axis=-1))

        real = (i > 0).astype(F32)
        loss_i, (dh, dw) = jax.value_and_grad(loss_of, argnums=(0, 1))(h, w)
        return (dh * real, dh * real), (jnp.full((1, 128), loss_i * real, F32), dw * real)

    (dh2, dh2_bf), (loss_acc, g_final) = _rows(
        final_fn, [_win(h2), _win(loss_target[0], roff=1)], [fnw], [(D_MODEL, F32), (D_MODEL, BF16)],
        [(1, 128), (1, D_MODEL)], name="final_loss", nrow=lp)

    def rs_front(pieces, layout, tag):
        g_full = _pack_big(pieces, layout)
        prow = g_full.shape[1]
        from_sibling = _swap_with_sibling(g_full, "rs_sibling_" + tag)
        mine_by_chip = lax.dynamic_index_in_dim(g_full.reshape(4, 2, prow, PACK_COLS), core, axis=1, keepdims=False)
        (p1, p1_bf), _ = _rows(
            lambda i, a, b: ((a.astype(F32) + b.astype(F32), a.astype(F32) + b.astype(F32)), ()),
            [_win(mine_by_chip.reshape(4 * prow, PACK_COLS)), _win(from_sibling.reshape(4 * prow, PACK_COLS))], [],
            [(PACK_COLS, F32), (PACK_COLS, BF16)], [], name="rs_sum_sibling_" + tag, nrow=4 * prow,
            tr=_tile(prow, 512, 16))
        return p1.reshape(4, prow, PACK_COLS), p1_bf.reshape(4, prow, PACK_COLS)

    def rs_back(p1, from_chips, layout, tag):
        prow = p1.shape[1]
        tr = _tile(prow, 512, 16)
        own = lax.dynamic_index_in_dim(p1, chip, axis=0, keepdims=False)
        fc2 = from_chips.reshape(3 * prow, PACK_COLS)
        nblk = prow // tr
        (g_shard,), _ = _rows(
            lambda i, a, b0, b1, b2: ((((a + b0.astype(F32)) + b1.astype(F32)) + b2.astype(F32),), ()),
            [_win(own), _win(fc2), _win(fc2, roff=-nblk), _win(fc2, roff=-2 * nblk)], [], [(PACK_COLS, F32)], [],
            name="rs_sum_chips_" + tag, nrow=prow, tr=tr)
        return _unpack_big(g_shard, layout)

    do_c = _mm(dh2_bf, w_out_c_f, "nt", name="out_c_dx", out_dtype=BF16)
    gw_out_c = _mm(o_c, dh2_bf, "tn", name="out_c_dw", out_dtype=BF16)
    (dq_c, dk_c, dv_c, dz_c, dpre), (g_gla_norm,) = _scan_bwd(
        _gla_chunk, gla_xs, [], [gla_norm_f], [], do_c, gla_sprev, name="gla_bwd", **gla_kw)
    dglow = _mm(dpre, w_gate_f, "nt", name="gate_dx", out_dtype=BF16)
    g_wgate = _mm(proj_c, dpre, "tn", name="gate_dw", a_win=(gl_off, GATE_PAD))[:GLA_RANK]
    (), (g_bgate,) = _rows(lambda i, d: ((), (jnp.sum(d.astype(F32), axis=0, keepdims=True),)), [_win(dpre)], [], [],
                           [(1, GLA_QK)], name="gate_db", nrow=lp)
    dproj_c = jnp.concatenate([dq_c, dk_c, dv_c, dz_c, dglow], axis=1)
    dhn1 = _mm(dproj_c, w_in_c_f, "nt", name="in_c_dx")
    gw_in_c = _mm(hn1, dproj_c, "tn", name="in_c_dw", out_dtype=BF16)
    p1_c, p1_c_bf = rs_front(dict(
        w_in_c=_column_windows(gw_in_c)),
        IN_C_LAYOUT, "in_c")

    def norm_bwd(i, h, dhn, dres, w):
        _, vjp = jax.vjp(_rms, h, w)
        dh, dw = vjp(dhn)
        return (jnp.where(rowmask(i), dh + dres, 0.0),), (dw,)

    def norm_bwd_both(i, h, dhn, dres, w):
        (dh,), acc = norm_bwd(i, h, dhn, dres, w)
        return (dh, dh), acc

    (dh1, dh1_bf), (g_norm_c,) = _rows(norm_bwd_both, [_win(h1), _win(dhn1), _win(dh2)], [norm_c_f],
                                       [(D_MODEL, F32), (D_MODEL, BF16)], [(1, D_MODEL)], name="norm_c_bwd", nrow=lp)

    do_ab = _mm(dh1_bf, w_out_ab_f, "nt", name="out_ab_dx", out_dtype=BF16)
    gw_out_ab = _mm(o_ab, dh1_bf, "tn", name="out_ab_dw", out_dtype=BF16)

    def s5_gate_bwd(i, dob, y, t, zb):
        _, vjp = jax.vjp(s5_gate, y, t, zb)
        dy, dt, dzb = vjp(dob.astype(F32))
        return (dy, dt, dzb), ()

    (dy_a, dt_glu, dzb), _ = _rows(
        s5_gate_bwd, [_win(do_ab, RET_W, S5_W), _win(y_s5), _win(t_glu), _win(proj_ab, zb_off, S5_W)], [],
        [(S5_W, F32), (S5_W, BF16), (S5_W, BF16)], [], name="s5_gate_bwd", nrow=lp)
    dyg2 = _mm(dt_glu, w_glu_f, "nt", name="s5_glu_dx")
    gw_glu = _mm(yg_bf, dt_glu, "tn", name="s5_glu_dw", out_dtype=BF16)

    def s5_y_bwd(i, dya, dyg, y, u, d):
        _, vjp = jax.vjp(_gelu, y)
        (dy_g,) = vjp(dyg)
        dy = dya + dy_g
        return (dy, d * dy), (jnp.sum(dy * u, axis=0, keepdims=True),)

    (dy_s5, du1), (g_d,) = _rows(
        s5_y_bwd, [_win(dy_a), _win(dyg2), _win(y_s5), _win(proj_ab, u_off, S5_W)], [s5_d],
        [(S5_W, BF16), (S5_W, F32)], [(1, S5_W)], name="s5_y_bwd", nrow=lp)
    p1_o, p1_o_bf = rs_front(dict(s5_w_glu=gw_glu.reshape(N_DEV, S5_W // N_DEV, S5_W),
                                  w_out_ab=gw_out_ab.reshape(N_DEV, OUT_AB // N_DEV, D_MODEL),
                                  w_out_c=gw_out_c.reshape(N_DEV, GLA_W // N_DEV, D_MODEL)), OTHER_LAYOUT, "other")
    o_cut = 640
    gx3, (from_chips_oa,) = _s5_expand(dy_s5, wc_t, a_blk=0, dims=NT, name="s5_cx_dx",
                                       hook=_chips_hook(p1_o_bf, 0, o_cut))
    rows_k = lambda col: pl.BlockSpec((tk5, col), lambda i, j, k: (k, i))
    gwc = _mm_core(xs5_2d, dy_s5, dims=TN, grid=(nt5, 1, lp // tk5), name="s5_cx_dw", a_spec=rows_k(TILE_W),
                   b_spec=pl.BlockSpec((tk5, 128), lambda i, j, k: (k, i % gt)),
                   o_spec=pl.BlockSpec((None, TILE_W, 128), lambda i, j, k: (i, 0, 0)),
                   out_shape=jax.ShapeDtypeStruct((nt5, TILE_W, 128), F32), acc_shape=(TILE_W, 128))
    g_s5, da = _s5_scan_bwd(gx3, xs5, a_re, a_im)
    g_s5_2d = g_s5.reshape(lp, 2 * S5_N)
    du, (from_chips_ob,) = _mm_core(g_s5_2d, wb_t, dims=NT, grid=(lp // tm5, gt, 2), name="s5_bu_dx", a_spec=wide_k,
                                    b_spec=wb_k, o_spec=narrow, out_shape=jax.ShapeDtypeStruct((lp, S5_W), BF16),
                                    acc_shape=(tm5, 128), extra=[(du1, narrow)],
                                    hook=_chips_hook(p1_o_bf, o_cut, None))
    from_chips_o = jnp.concatenate([from_chips_oa, from_chips_ob], axis=1)
    gwb = _mm_core(proj_ab, g_s5_2d, dims=TN, grid=(nt5, 1, lp // tk5), name="s5_bu_dw",
                   a_spec=pl.BlockSpec((tk5, 128), lambda i, j, k: (k, u_blk + i % gt)), b_spec=rows_k(TILE_W),
                   o_spec=pl.BlockSpec((None, 128, TILE_W), lambda i, j, k: (i, 0, 0)),
                   out_shape=jax.ShapeDtypeStruct((nt5, 128, TILE_W), F32), acc_shape=(128, TILE_W))
    gwc6 = gwc.reshape(2, gt, gt, S5_P, gt, S5_GH)
    g_c = jnp.einsum("rsgpgh->rsghp", gwc6).reshape(2, S5_G, S5_GH, S5_P)
    g_c_re, g_c_im = g_c[0], -g_c[1]
    gwb6 = gwb.reshape(2, gt, gt, S5_GH, gt, S5_P)
    d_bb = jnp.einsum("rsghgp->rsgph", gwb6).reshape(2, S5_G, S5_P * S5_GH)
    d_bb_re, d_bb_im = d_bb[0], d_bb[1]
    g_lam_re, g_lam_im, g_log_dt, g_b_re, g_b_im = _s5_disc_bwd(
        disc_args, (da[0].reshape(S5_G, S5_P), da[1].reshape(S5_G, S5_P), d_bb_re, d_bb_im))

    (dq_a, dk_a, dv_a, dz_a), (g_ret_norm,), (from_chips_c,) = _scan_bwd(
        _ret_chunk, ret_xs, ret_cs, [ret_norm_h], [lg], do_ab, ret_sprev, name="ret_bwd", post=_ret_post,
        hook=_chips_hook(p1_c_bf), **ret_kw)
    dproj_ab = jnp.concatenate([dq_a, dk_a, dv_a, dz_a, du, dzb], axis=1)

    lane = lambda a_: pad_to(a_, a_.shape[0], 128)

    def sum8(i, *blocks):
        acc = blocks[0]
        for b in blocks[1:]:
            acc = acc + b
        return (acc,), ()

    def pack_small(pieces):
        return jnp.concatenate([_rows1024(p) for _, p in pieces], axis=0)

    def sum_small(gathered, pieces, tag):
        srow = gathered.shape[1]
        tr = _tile(srow, 128, 8)
        flat = gathered.reshape(N_DEV * srow, PACK_COLS)
        (total,), _ = _rows(sum8, [_win(flat, roff=-d * (srow // tr)) for d in range(N_DEV)], [], [(PACK_COLS, F32)],
                            [], name="sum_small_" + tag, nrow=srow, tr=tr)
        out, o = {}, 0
        for name_, p in pieces:
            r8 = _rows1024(p).shape[0]
            out[name_] = _unrows1024(total[o:o + r8], *p.shape)
            o += r8
        return out

    early_pieces = [
        ("vec2048", jnp.concatenate([g_final, g_norm_c], axis=0)),
        ("vec1024", jnp.concatenate([g_d, g_bgate, pad_to(loss_acc[:, :1], 1, PACK_COLS)], axis=0)),
        ("lam3", jnp.concatenate([lane(g_lam_re), lane(g_lam_im), lane(g_log_dt)], axis=1)),
        ("s5_b_re", g_b_re), ("s5_b_im", g_b_im),
        ("s5_c_re", g_c_re.reshape(S5_G, S5_GH * S5_P)), ("s5_c_im", g_c_im.reshape(S5_G, S5_GH * S5_P)),
        ("ret_norm_w", g_ret_norm.reshape(RET_H, RET_DV)), ("gla_norm_w", g_gla_norm.reshape(GLA_H, GLA_DV)),
        ("gla_w_gate", g_wgate)]
    gw_in_ab, (early_all,) = _mm(
        hn0, dproj_ab, "tn", name="in_ab_dw", out_dest=True, out_dtype=BF16,
        hook=_gather_hook(pack_small(early_pieces)))
    p1_first, p1_first_bf = rs_front(dict(w_in_ab=gw_in_ab), FIRST_LAYOUT, "first")
    dhn0, (from_chips_first,) = _mm(dproj_ab, w_in_ab_g, "nt", name="in_ab_dx", b_dev=True,
                                    hook=_chips_hook(p1_first_bf))
    def norm_bwd_first(i, h, dhn, dres, w):
        (dh,), (dw,) = norm_bwd(i, h, dhn, dres, w)
        return (dh,), (dw, dh * (i == 0).astype(F32))

    (grad_x2d,), (g_norm_ab, dh0_first) = _rows(
        norm_bwd_first, [_win(h0), _win(dhn0), _win(dh1)], [norm_ab_w], [(D_MODEL, F32, 1)],
        [(1, D_MODEL), (CHUNK, D_MODEL)], name="norm_ab_bwd", nrow=lp)
    grad_x = grad_x2d[None]
    late_pieces = [("norm_ab_w", g_norm_ab), ("meta", dh0_first[PAD:CHUNK])]
    small = sum_small(early_all, early_pieces, "early")
    small.update(sum_small(_all_gather(pack_small(late_pieces), "gather_grads"), late_pieces, "late"))

    big_grads = {**rs_back(p1_c, from_chips_c, IN_C_LAYOUT, "in_c"), **rs_back(p1_o, from_chips_o, OTHER_LAYOUT, "other"),
                 **rs_back(p1_first, from_chips_first, FIRST_LAYOUT, "first")}
    big_grads["w_in_c"] = _lane_select(big_grads["w_in_c"], win_off, -1, 896, F32, True,
                                       "w_in_c_from_window")[:, :SHARD_C]
    small["final_norm_w"], small["norm_c_w"] = small["vec2048"][0:1], small["vec2048"][1:2]
    small["s5_d"], small["gla_b_gate"] = small["vec1024"][0:1], small["vec1024"][1:2]
    loss = small["vec1024"][2, 0]
    small["s5_lam_re"], small["s5_lam_im"] = small["lam3"][:, :S5_P], small["lam3"][:, 128:128 + S5_P]
    small["s5_log_dt"] = small["lam3"][:, 256:257]

    def my_cols(g, n):
        return lax.dynamic_slice_in_dim(g, dev * n, n, axis=g.ndim - 1)

    grads = dict(
        meta=my_cols(small["meta"], D_MODEL // N_DEV),
        norm_ab_w=small["norm_ab_w"], w_in_ab=big_grads["w_in_ab"][None], ret_norm_w=small["ret_norm_w"].reshape(1, RET_W),
        s5_lam_re=small["s5_lam_re"][None], s5_lam_im=small["s5_lam_im"][None],
        s5_log_dt=small["s5_log_dt"].reshape(1, S5_G),
        s5_b_re=small["s5_b_re"].reshape(1, S5_G, S5_P, S5_GH), s5_b_im=small["s5_b_im"].reshape(1, S5_G, S5_P, S5_GH),
        s5_c_re=small["s5_c_re"][None], s5_c_im=small["s5_c_im"][None], s5_d=small["s5_d"],
        s5_w_glu=big_grads["s5_w_glu"][None], w_out_ab=big_grads["w_out_ab"][None],
        norm_c_w=my_cols(small["norm_c_w"], D_MODEL // N_DEV), w_in_c=big_grads["w_in_c"][None],
        gla_w_gate=my_cols(small["gla_w_gate"], GLA_QK // N_DEV)[None],
        gla_b_gate=my_cols(small["gla_b_gate"], GLA_QK // N_DEV),
        gla_norm_w=my_cols(small["gla_norm_w"].reshape(1, GLA_W), GLA_W // N_DEV),
        w_out_c=big_grads["w_out_c"][None], final_norm_w=small["final_norm_w"].reshape(D_MODEL))

    deltas, new_m, new_v = {}, {}, {}
    for k in order:
        w = weights[k]
        d2, m2, v2 = _adamw(_as2d(w), _as2d(grads[k].reshape(w.shape)), _as2d(mom_m[k]), _as2d(mom_v[k]), "adamw_" + k)
        deltas[k], new_m[k], new_v[k] = d2.reshape(w.shape), m2.reshape(w.shape), v2.reshape(w.shape)
        grads[k] = grads[k].reshape(w.shape)

    return (loss, grad_x, *[grads[k] for k in order], *[deltas[k] for k in order],
            *[new_m[k] for k in order], *[new_v[k] for k in order])
```

```python
import functools
import math

import jax
import jax.numpy as jnp
from jax import lax
from jax.experimental import pallas as pl
from jax.experimental.pallas import tpu as pltpu

F32, BF16 = jnp.float32, jnp.bfloat16
MESH = pl.DeviceIdType.MESH
N_DEV = 8

D_MODEL = 2048
CHUNK = 128
N_META = 16
PAD = CHUNK - N_META
SUB = 16
EPS = 1e-6
RET_H, RET_DK, RET_DV = 8, 128, 256
RET_QK, RET_W = RET_H * RET_DK, RET_H * RET_DV
ROPE_BASE = 10000.0
S5_W, S5_G, S5_P, S5_GH = 1024, 64, 64, 16
S5_N = S5_G * S5_P
GLA_H, GLA_DK, GLA_DV, GLA_RANK, GLA_TAU = 4, 256, 512, 16, 16.0
GLA_QK, GLA_W = GLA_H * GLA_DK, GLA_H * GLA_DV
IN_AB = 2 * RET_QK + 2 * RET_W + 2 * S5_W
OUT_AB = RET_W + S5_W
IN_C = 2 * GLA_QK + 2 * GLA_W + GLA_RANK
GATE_PAD = 256
IN_C_PAD = 2 * GLA_QK + 2 * GLA_W + GATE_PAD
ADAM_LR, ADAM_B1, ADAM_B2, ADAM_EPS, ADAM_WD, ADAM_STEP = 0.001, 0.9, 0.999, 1e-08, 0.01, 10

VMEM_LIMIT_BYTES = 48 * 2 ** 20
PACK_COLS = 1024
SHARD_C = IN_C // N_DEV
WIN_STEP = 768
WIN_COLS = 1024


def _params(sem):
    return pltpu.CompilerParams(dimension_semantics=sem, vmem_limit_bytes=VMEM_LIMIT_BYTES)


def _tile(n, cap, mult):
    best = None
    for t in range(mult, min(n, cap) + 1, mult):
        if n % t == 0:
            best = t
    assert best is not None, (n, cap, mult)
    return best


def _dg(a, b, ca, cb):
    return lax.dot_general(a.astype(BF16), b.astype(BF16), (((ca,), (cb,)), ((), ())),
                           preferred_element_type=F32)


@functools.partial(jax.custom_vjp, nondiff_argnums=(2, 3))
def _bdot(a, b, ca, cb):
    return _dg(a, b, ca, cb)


def _bdot_fwd(a, b, ca, cb):
    return _dg(a, b, ca, cb), (a, b)


def _bdot_bwd(ca, cb, res, g):
    a, b = res
    da = _dg(g, b, 1, 1 - cb) if ca == 1 else _dg(b, g, 1 - cb, 1)
    db = _dg(a, g, 1 - ca, 0) if cb == 0 else _dg(g, a, 0, 1 - ca)
    return da.astype(a.dtype), db.astype(b.dtype)


_bdot.defvjp(_bdot_fwd, _bdot_bwd)


def _sigmoid(x):
    return 1.0 / (1.0 + jnp.exp(-x))


def _silu(x):
    return x * _sigmoid(x)


def _log_sigmoid(x):
    return jnp.minimum(x, 0.0) - jnp.log(1.0 + jnp.exp(-jnp.abs(x)))


def _gelu(x):
    return 0.5 * x * (1.0 + jnp.tanh(math.sqrt(2.0 / math.pi) * (x + 0.044715 * (x * x * x))))


def _rms(x, w):
    return x * lax.rsqrt(jnp.mean(x * x, axis=-1, keepdims=True) + EPS) * w


class _Hook:
    def __init__(self, ins, outs, sems, phases):
        self.ins, self.outs, self.sems, self.phases = list(ins), list(outs), list(sems), list(phases)


_NO_HOOK = _Hook([], [], [], [])
_ANY = pl.BlockSpec(memory_space=pl.ANY)


def _run_hook(hook, lin, total, in_refs, out_refs, sem_refs):
    for frac, fn in hook.phases:
        at = min(int(frac * total), total - 1)

        @pl.when(lin == at)
        def _(fn=fn):
            fn(in_refs, out_refs, sem_refs)


def _mm_core(a, b, *, dims, grid, a_spec, b_spec, o_spec, out_shape, acc_shape, name, extra=(), hook=None,
             b_parts=0):
    nk = grid[2]
    n_extra = len(extra)
    hook = _NO_HOOK if hook is None else hook
    hi, ho = len(hook.ins), len(hook.outs)

    def body(*refs):
        a_ref, b_ref = refs[0], refs[1]
        o_ref, acc = refs[2 + n_extra + hi], refs[3 + n_extra + hi + ho]
        k = pl.program_id(2)
        lin = (pl.program_id(0) * grid[1] + pl.program_id(1)) * nk + k
        _run_hook(hook, lin, grid[0] * grid[1] * nk, refs[2 + n_extra:2 + n_extra + hi],
                  refs[3 + n_extra + hi:3 + n_extra + hi + ho], refs[4 + n_extra + hi + ho:])

        if b_parts:
            part = sum(lax.dot_general(a_ref[:, d * PACK_COLS:(d + 1) * PACK_COLS].astype(BF16), b_ref[d].astype(BF16),
                                       dims, preferred_element_type=F32) for d in range(b_parts))
        else:
            part = lax.dot_general(a_ref[...].astype(BF16), b_ref[...].astype(BF16), dims, preferred_element_type=F32)

        def finish(r):
            for e in range(n_extra):
                r = r + refs[2 + e][...].astype(F32)
            o_ref[...] = r.astype(o_ref.dtype)

        if nk == 1:
            finish(part)
        else:
            @pl.when(k == 0)
            def _():
                acc[...] = part

            @pl.when(k > 0)
            def _():
                acc[...] += part

            @pl.when(k == nk - 1)
            def _():
                finish(acc[...])

    res = pl.pallas_call(
        body, name=name, grid=grid,
        in_specs=[a_spec, b_spec] + [sp for _, sp in extra] + [_ANY] * hi,
        out_specs=[o_spec] + [_ANY] * ho, out_shape=[out_shape] + hook.outs,
        scratch_shapes=[pltpu.VMEM(acc_shape if nk > 1 else (8, 128), F32)] + hook.sems,
        compiler_params=_params(("arbitrary", "arbitrary", "arbitrary")),
    )(a, b, *[arr for arr, _ in extra], *hook.ins)
    return res[0] if hook is _NO_HOOK else (res[0], res[1:])


NN, NT, TN = (((1,), (0,)), ((), ())), (((1,), (1,)), ((), ())), (((0,), (0,)), ((), ()))


FULL_K = 2048


def _mm(a, b, mode, *, name, out_dtype=F32, a_win=None, add=None, bias=None, hook=None, b_dev=False,
        out_dest=False):
    b_parts = 0
    if mode == "tn":
        kdim, n = a.shape[0], b.shape[1]
        m = a.shape[1] if a_win is None else a_win[1]
        tm, tn, tk = _tile(m, 512, 128), _tile(n, 640, 128), kdim
        off = 0 if a_win is None else a_win[0] // tm
        a_spec = pl.BlockSpec((tk, tm), lambda i, j, k: (k, i + off))
        b_spec = pl.BlockSpec((tk, tn), lambda i, j, k: (k, j))
        dims = TN
    else:
        m = a.shape[0]
        kdim = a.shape[1] if a_win is None else a_win[1]
        if b_dev:
            n = b.shape[0] * b.shape[2] if mode == "nn" else b.shape[1]
        else:
            n = b.shape[1] if mode == "nn" else b.shape[0]
        if FULL_K < kdim <= 2 * FULL_K and not b_dev:
            tm, tn, tk = _tile(m, 1408, 8), _tile(n, 1024, 128), _tile(kdim, 1024, 128)
        else:
            tm = _tile(m, 1408 if kdim <= FULL_K else 352, 8)
            tn, tk = _tile(n, 640, 128), kdim
        off = 0 if a_win is None else a_win[0] // tk
        a_spec = pl.BlockSpec((tm, tk), lambda i, j, k: (i, k + off))
        if mode == "nn":
            dims = NN
            if b_dev:
                per = PACK_COLS // tn
                b_spec = pl.BlockSpec((None, tk, tn), lambda i, j, k: (j // per, k, j % per))
            else:
                b_spec = pl.BlockSpec((tk, tn), lambda i, j, k: (k, j))
        else:
            dims = NT
            if b_dev:
                b_parts = kdim // PACK_COLS
                b_spec = pl.BlockSpec((b_parts, tn, PACK_COLS), lambda i, j, k: (0, j, 0))
            else:
                b_spec = pl.BlockSpec((tn, tk), lambda i, j, k: (j, k))
    if a_win is not None:
        assert a_win[0] % (tm if mode == "tn" else tk) == 0
    extra = []
    if add is not None:
        extra.append((add, pl.BlockSpec((tm, tn), lambda i, j, k: (i, j))))
    if bias is not None:
        extra.append((bias, pl.BlockSpec((1, tn), lambda i, j, k: (0, j))))
    if out_dest:
        per = PACK_COLS // tn
        o_spec = pl.BlockSpec((None, tm, tn), lambda i, j, k: (j // per, i, j % per))
        out_shape = jax.ShapeDtypeStruct((n // PACK_COLS, m, PACK_COLS), out_dtype)
    else:
        o_spec = pl.BlockSpec((tm, tn), lambda i, j, k: (i, j))
        out_shape = jax.ShapeDtypeStruct((m, n), out_dtype)
    return _mm_core(a, b, dims=dims, grid=(m // tm, n // tn, kdim // tk), a_spec=a_spec, b_spec=b_spec,
                    o_spec=o_spec, out_shape=out_shape, acc_shape=(tm, tn), name=name, extra=extra, hook=hook,
                    b_parts=b_parts)


def _win(arr, col0=0, width=None, roff=0):
    return (arr, col0, arr.shape[1] if width is None else width, roff)


def _rows(fn, rows, consts, outs, accs, *, name, nrow, tr=CHUNK):
    nr, nc, no = len(rows), len(consts), len(outs)

    def body(*refs):
        i = pl.program_id(0)
        ins = [r[...] for r in refs[:nr + nc]]
        o_refs = refs[nr + nc:nr + nc + no]
        a_refs = refs[nr + nc + no:]
        res_o, res_a = fn(i, *ins)
        for r, v in zip(o_refs, res_o):
            r[...] = v.astype(r.dtype)
        if a_refs:
            @pl.when(i == 0)
            def _():
                for r in a_refs:
                    r[...] = jnp.zeros_like(r)

            for r, v in zip(a_refs, res_a):
                r[...] += v

    in_specs = []
    for (arr, col0, width, roff) in rows:
        assert col0 % width == 0 and arr.shape[0] % tr == 0
        in_specs.append(pl.BlockSpec((tr, width), lambda i, c=col0 // width, ro=roff: (jnp.maximum(i - ro, 0), c)))
    for c in consts:
        in_specs.append(pl.BlockSpec(c.shape, lambda i, nd=c.ndim: (0,) * nd))
    outs = [tuple(o) + (0,) * (3 - len(o)) for o in outs]
    out_specs = [pl.BlockSpec((tr, w), lambda i, ro=ro: (jnp.maximum(i - ro, 0), 0)) for (w, _, ro) in outs]
    out_specs += [pl.BlockSpec(s, lambda i, nd=len(s): (0,) * nd) for s in accs]
    out_shape = [jax.ShapeDtypeStruct((nrow - ro * tr, w), dt) for (w, dt, ro) in outs]
    out_shape += [jax.ShapeDtypeStruct(s, F32) for s in accs]
    res = pl.pallas_call(
        body, name=name, grid=(nrow // tr,), in_specs=in_specs, out_specs=out_specs, out_shape=out_shape,
        compiler_params=_params(("arbitrary",)),
    )(*[r[0] for r in rows], *consts)
    return res[:no], res[no:]


HEADS_PER_STEP = 2


def _scan_specs(xs, cs, ws, ks, chunk_of, hpb):
    specs = []
    for (arr, width, colfn) in xs:
        specs.append(pl.BlockSpec((CHUNK, width * hpb), lambda h, n, f=colfn: (chunk_of(n), f(h * hpb) // hpb)))
    for (arr, width, colfn) in cs:
        specs.append(pl.BlockSpec((CHUNK, width), lambda h, n, f=colfn: (chunk_of(n), f(h))))
    for arr in list(ws) + list(ks):
        specs.append(pl.BlockSpec((hpb, 1, arr.shape[2]), lambda h, n: (h, 0, 0)))
    return specs


def _scan_fwd(fn, xs, cs, ws, ks, *, heads, nchunk, s_shape, out_w, name, pre=None, hook=None,
              hpb=HEADS_PER_STEP):
    nx, ncs, nw = len(xs), len(cs), len(ws)
    hook = _NO_HOOK if hook is None else hook
    hi, ho = len(hook.ins), len(hook.outs)
    hblocks = heads // hpb

    def body(*refs):
        n = pl.program_id(1)
        nin = nx + ncs + nw + len(ks)
        y_ref, sp_ref = refs[nin + hi], refs[nin + hi + 1]
        s_scr = refs[nin + hi + 2 + ho]
        _run_hook(hook, pl.program_id(0) * nchunk + n, hblocks * nchunk, refs[nin:nin + hi],
                  refs[nin + hi + 2:nin + hi + 2 + ho], refs[nin + hi + 3 + ho:])

        @pl.when(n == 0)
        def _():
            s_scr[...] = jnp.zeros_like(s_scr)

        cv = [r[...] for r in refs[nx:nx + ncs]]
        for e in range(hpb):
            state = s_scr[e]
            sp_ref[e, 0] = state
            xv = [r[:, e * w:(e + 1) * w] for r, (_, w, _) in zip(refs[:nx], xs)]
            wv = [r[e] for r in refs[nx + ncs:nx + ncs + nw]]
            kv = [r[e] for r in refs[nx + ncs + nw:nin]]
            if pre is not None:
                xv = pre(xv, cv)
            y, s_new = fn(n, xv, state, cv, wv, kv)
            y_ref[:, e * out_w:(e + 1) * out_w] = y.astype(y_ref.dtype)
            s_scr[e] = s_new

    lp = nchunk * CHUNK
    res = pl.pallas_call(
        body, name=name, grid=(hblocks, nchunk),
        in_specs=_scan_specs(xs, cs, ws, ks, lambda n: n, hpb) + [_ANY] * hi,
        out_specs=[pl.BlockSpec((CHUNK, out_w * hpb), lambda h, n: (n, h)),
                   pl.BlockSpec((hpb, 1) + s_shape, lambda h, n: (h, n, 0, 0))] + [_ANY] * ho,
        out_shape=[jax.ShapeDtypeStruct((lp, heads * out_w), BF16),
                   jax.ShapeDtypeStruct((heads, nchunk) + s_shape, F32)] + hook.outs,
        scratch_shapes=[pltpu.VMEM((hpb,) + s_shape, F32)] + hook.sems,
        compiler_params=_params(("arbitrary", "arbitrary")),
    )(*[t[0] for t in xs], *[t[0] for t in cs], *ws, *ks, *hook.ins)
    return (res[0], res[1]) if hook is _NO_HOOK else (res[0], res[1], res[2:])


def _scan_bwd(fn, xs, cs, ws, ks, dy, sprev, *, heads, nchunk, s_shape, out_w, name, pre=None, post=None,
              hook=None, hpb=HEADS_PER_STEP):
    nx, ncs, nw = len(xs), len(cs), len(ws)
    nin = nx + ncs + nw + len(ks)
    hook = _NO_HOOK if hook is None else hook
    hi, ho = len(hook.ins), len(hook.outs)
    hblocks = heads // hpb

    def body(*refs):
        step = pl.program_id(1)
        n = nchunk - 1 - step
        dy_ref, sp_ref = refs[nin], refs[nin + 1]
        o0 = nin + 2 + hi
        dx_refs = refs[o0:o0 + nx]
        dw_refs = refs[o0 + nx:o0 + nx + nw]
        ds_scr = refs[o0 + nx + nw + ho]
        _run_hook(hook, pl.program_id(0) * nchunk + step, hblocks * nchunk, refs[nin + 2:o0],
                  refs[o0 + nx + nw:o0 + nx + nw + ho], refs[o0 + nx + nw + ho + 1:])

        @pl.when(step == 0)
        def _():
            ds_scr[...] = jnp.zeros_like(ds_scr)
            for r in dw_refs:
                r[...] = jnp.zeros_like(r)

        cv = [r[...] for r in refs[nx:nx + ncs]]
        for e in range(hpb):
            xv = [r[:, e * w:(e + 1) * w] for r, (_, w, _) in zip(refs[:nx], xs)]
            wv = [r[e] for r in refs[nx + ncs:nx + ncs + nw]]
            kv = [r[e] for r in refs[nx + ncs + nw:nin]]
            if pre is not None:
                xv = pre(xv, cv)
            _, vjp = jax.vjp(lambda xs_, s_, ws_, kv=kv: fn(n, xs_, s_, cv, ws_, kv), xv, sp_ref[e, 0], wv)
            dxs, ds_prev, dws = vjp((dy_ref[:, e * out_w:(e + 1) * out_w].astype(F32), ds_scr[e]))
            if post is not None:
                dxs = post(dxs, cv)
            for r, v, (_, w, _) in zip(dx_refs, dxs, xs):
                r[:, e * w:(e + 1) * w] = v.astype(r.dtype)
            for r, v in zip(dw_refs, dws):
                r[e] += v
            ds_scr[e] = ds_prev

    lp = nchunk * CHUNK
    rev = lambda n: nchunk - 1 - n
    in_specs = _scan_specs(xs, cs, ws, ks, rev, hpb)
    in_specs.append(pl.BlockSpec((CHUNK, out_w * hpb), lambda h, n: (rev(n), h)))
    in_specs.append(pl.BlockSpec((hpb, 1) + s_shape, lambda h, n: (h, rev(n), 0, 0)))
    out_specs = [pl.BlockSpec((CHUNK, w * hpb), lambda h, n: (rev(n), h)) for (_, w, _) in xs]
    out_specs += [pl.BlockSpec((hpb, 1, w.shape[2]), lambda h, n: (h, 0, 0)) for w in ws]
    out_shape = [jax.ShapeDtypeStruct((lp, heads * w), BF16) for (_, w, _) in xs]
    out_shape += [jax.ShapeDtypeStruct(w.shape, F32) for w in ws]
    res = pl.pallas_call(
        body, name=name, grid=(hblocks, nchunk), in_specs=in_specs + [_ANY] * hi,
        out_specs=out_specs + [_ANY] * ho, out_shape=out_shape + hook.outs,
        scratch_shapes=[pltpu.VMEM((hpb,) + s_shape, F32)] + hook.sems,
        compiler_params=_params(("arbitrary", "arbitrary")),
    )(*[t[0] for t in xs], *[t[0] for t in cs], *ws, *ks, dy, sprev, *hook.ins)
    if hook is _NO_HOOK:
        return res[:nx], res[nx:]
    return res[:nx], res[nx:nx + nw], res[nx + nw:]


def _iota2(shape, dim):
    return lax.broadcasted_iota(jnp.int32, shape, dim)


def _ret_chunk(n, xs, state, cs, ws, ks):
    q, k, v, z = xs
    (w,), (lg,) = ws, ks
    lgc = lg[:, :1]
    row, col = _iota2((CHUNK, CHUNK), 0), _iota2((CHUNK, CHUNK), 1)
    diff = jnp.maximum(row - col, 0).astype(F32)
    decay = jnp.where(row >= col, jnp.exp(lg * diff), 0.0)
    scores = _bdot(q, k, 1, 1) * decay
    o_intra = _bdot(scores, v, 1, 0)
    idx = _iota2((CHUNK, 1), 0).astype(F32)
    k_w = k * jnp.exp(lgc * (CHUNK - 1.0 - idx))
    kv = _bdot(k_w, v, 0, 0)
    s_new = state * jnp.exp(lgc * float(CHUNK)) + kv
    q_w = q * jnp.exp(lgc * (idx + 1.0))
    o = o_intra + _bdot(q_w, state, 1, 0)
    return _rms(o, w) * _silu(z), s_new


def _rope(t, cos2, sin2):
    return t * cos2 + pltpu.roll(t, RET_DK // 2, 1) * sin2


def _rope_t(g, cos2, sin2):
    return g * cos2 - pltpu.roll(g, RET_DK // 2, 1) * sin2


def _ret_pre(xv, cv):
    q, k, v, z = xv
    cos2, sin2 = cv
    return [_rope(q, cos2, sin2), _rope(k, cos2, sin2) * (RET_DK ** -0.5), v, z]


def _ret_post(dxs, cv):
    dq, dk, dv, dz = dxs
    cos2, sin2 = cv
    return [_rope_t(dq, cos2, sin2), _rope_t(dk, cos2, sin2) * (RET_DK ** -0.5), dv, dz]


def _tri_apply(x, lower):
    n = x.shape[0]
    row, col = _iota2((n, n), 0), _iota2((n, n), 1)
    tri = (row >= col if lower else row <= col).astype(BF16)
    hi = x.astype(BF16)
    rest = x - hi.astype(F32)
    mid = rest.astype(BF16)
    lo = (rest - mid.astype(F32)).astype(BF16)
    return sum(lax.dot_general(tri, p, NN, preferred_element_type=F32) for p in (hi, mid, lo))


@jax.custom_vjp
def _cumsum_rows(x):
    return _tri_apply(x, True)


_cumsum_rows.defvjp(lambda x: (_tri_apply(x, True), None), lambda _, g: (_tri_apply(g, False),))


def _gla_chunk(n, xs, state_t, cs, ws, ks):
    q, k, v, z, pre = xs
    (w,) = ws
    q = q * (GLA_DK ** -0.5)
    rowc = _iota2((CHUNK, 1), 0)
    valid = jnp.logical_or(n > 0, rowc >= PAD)
    log_a = jnp.where(valid, _log_sigmoid(pre) / GLA_TAU, 0.0)
    b = _cumsum_rows(log_a)
    b_last = b[CHUNK - 1:CHUNK, :]
    kv_t = _bdot(v, k * jnp.exp(b_last - b), 0, 0)
    s_new = state_t * jnp.exp(b_last) + kv_t
    o_inter = _bdot(q * jnp.exp(b), state_t, 1, 1)
    outs = []
    for s in range(CHUNK // SUB):
        lo, hi = s * SUB, (s + 1) * SUB
        b_ref = jnp.zeros_like(b_last) if s == 0 else b[lo - 1:lo, :]
        q_hat = q[lo:hi] * jnp.exp(b[lo:hi] - b_ref)
        k_hat = k[:hi] * jnp.exp(b_ref - b[:hi])
        sc = _bdot(q_hat, k_hat, 1, 1)
        causal = _iota2((SUB, hi), 0) + lo >= _iota2((SUB, hi), 1)
        outs.append(_bdot(jnp.where(causal, sc, 0.0), v[:hi], 1, 0))
    o = jnp.concatenate(outs, axis=0) + o_inter
    return _rms(o, w) * _silu(z), s_new


def _s5_disc(lam_re, lam_im, log_dt, b_re, b_im, expand):
    dt = jnp.exp(log_dt)
    mag = jnp.exp(lam_re * dt)
    ab_re, ab_im = mag * jnp.cos(lam_im * dt), mag * jnp.sin(lam_im * dt)
    den = lam_re * lam_re + lam_im * lam_im
    nr, ni = ab_re - 1.0, ab_im
    f_re = (nr * lam_re + ni * lam_im) / den
    f_im = (ni * lam_re - nr * lam_im) / den
    hp = lax.Precision.HIGHEST
    f_re = jnp.dot(f_re, expand, precision=hp, preferred_element_type=F32)
    f_im = jnp.dot(f_im, expand, precision=hp, preferred_element_type=F32)
    return ab_re, ab_im, f_re * b_re - f_im * b_im, f_re * b_im + f_im * b_re


def _s5_disc_fwd(args):
    def body(*refs):
        outs = _s5_disc(*[r[...] for r in refs[:6]])
        for r, v in zip(refs[6:], outs):
            r[...] = v

    g, p = args[0].shape
    return pl.pallas_call(
        body, name="s5_disc_fwd",
        out_shape=[jax.ShapeDtypeStruct((g, p), F32)] * 2 + [jax.ShapeDtypeStruct(args[3].shape, F32)] * 2,
    )(*args)


def _s5_disc_bwd(args, cts):
    def body(*refs):
        prim = [r[...] for r in refs[:5]]
        expand = refs[5][...]
        ct = tuple(r[...] for r in refs[6:10])
        _, vjp = jax.vjp(lambda *a: _s5_disc(*a, expand), *prim)
        for r, v in zip(refs[10:], vjp(ct)):
            r[...] = v

    return pl.pallas_call(
        body, name="s5_disc_bwd", out_shape=[jax.ShapeDtypeStruct(a.shape, F32) for a in args[:5]],
    )(*args, *cts)


SCAN_ROWS, SCAN_LANES = 32, 128
TILE_G = 8
TILE_W = TILE_G * S5_P
S5_TB = 64


def _s5_scan_fwd(bu, a_re, a_im):
    lp = bu.shape[0]

    def body(bu_ref, ar_ref, ai_ref, x_ref, st):
        @pl.when(pl.program_id(0) == 0)
        def _():
            st[...] = jnp.zeros_like(st)

        ar, ai = ar_ref[...], ai_ref[...]

        def step(t, carry):
            xr, xi = carry
            nr = ar * xr - ai * xi + bu_ref[t, 0:SCAN_ROWS, :]
            ni = ar * xi + ai * xr + bu_ref[t, SCAN_ROWS:2 * SCAN_ROWS, :]
            x_ref[t, 0:SCAN_ROWS, :] = nr
            x_ref[t, SCAN_ROWS:2 * SCAN_ROWS, :] = ni
            return nr, ni

        xr, xi = lax.fori_loop(0, S5_TB, step, (st[0], st[1]))
        st[0] = xr
        st[1] = xi

    blk = pl.BlockSpec((S5_TB, 2 * SCAN_ROWS, SCAN_LANES), lambda i: (i, 0, 0))
    cst = pl.BlockSpec((SCAN_ROWS, SCAN_LANES), lambda i: (0, 0))
    return pl.pallas_call(
        body, name="s5_scan_fwd", grid=(lp // S5_TB,), in_specs=[blk, cst, cst], out_specs=blk,
        out_shape=jax.ShapeDtypeStruct(bu.shape, F32),
        scratch_shapes=[pltpu.VMEM((2, SCAN_ROWS, SCAN_LANES), F32)],
        compiler_params=_params(("arbitrary",)),
    )(bu, a_re, a_im)


def _s5_expand(a, w_t, *, a_blk, dims, name, hook=None):
    lp, nt = a.shape[0], w_t.shape[0]
    tm = _tile(lp, 176, 8)
    steps = lp // tm
    rows3 = 2 * SCAN_ROWS
    per = TILE_W // SCAN_LANES
    hook = _NO_HOOK if hook is None else hook
    hi, ho = len(hook.ins), len(hook.outs)

    def body(*refs):
        a_ref, w_ref, o_ref = refs[0], refs[1], refs[2 + hi]
        _run_hook(hook, pl.program_id(0), steps, refs[2:2 + hi], refs[3 + hi:3 + hi + ho], refs[3 + hi + ho:])
        for j in range(nt):
            s = j % TILE_G
            r = lax.dot_general(a_ref[:, 128 * s:128 * (s + 1)].astype(BF16), w_ref[j], dims,
                                preferred_element_type=F32)
            for c in range(per):
                o_ref[pl.ds(per * j + c, tm, stride=rows3), :] = r[:, SCAN_LANES * c:SCAN_LANES * (c + 1)]

    res = pl.pallas_call(
        body, name=name, grid=(steps,),
        in_specs=[pl.BlockSpec((tm, S5_W), lambda i: (i, a_blk)), pl.BlockSpec(w_t.shape, lambda i: (0, 0, 0))]
        + [_ANY] * hi,
        out_specs=[pl.BlockSpec((tm * rows3, SCAN_LANES), lambda i: (i, 0))] + [_ANY] * ho,
        out_shape=[jax.ShapeDtypeStruct((lp * rows3, SCAN_LANES), F32)] + hook.outs,
        scratch_shapes=hook.sems, compiler_params=_params(("arbitrary",)),
    )(a, w_t, *hook.ins)
    out3 = res[0].reshape(lp, rows3, SCAN_LANES)
    return out3 if hook is _NO_HOOK else (out3, res[1:])


def _s5_scan_bwd(gx, x, a_re, a_im):
    lp = gx.shape[0]
    nb = lp // S5_TB

    def body(gx_ref, x_ref, xp_ref, ar_ref, ai_ref, g_ref, da_ref, st):
        i = pl.program_id(0)

        @pl.when(i == 0)
        def _():
            st[...] = jnp.zeros_like(st)
            da_ref[...] = jnp.zeros_like(da_ref)

        ar, ai = ar_ref[...], ai_ref[...]
        first = (i == nb - 1).astype(F32)

        def step(s, carry):
            gr, gi, dar, dai = carry
            t = S5_TB - 1 - s
            ngr = gx_ref[t, 0:SCAN_ROWS, :] + ar * gr + ai * gi
            ngi = gx_ref[t, SCAN_ROWS:2 * SCAN_ROWS, :] + ar * gi - ai * gr
            g_ref[t, 0:SCAN_ROWS, :] = ngr
            g_ref[t, SCAN_ROWS:2 * SCAN_ROWS, :] = ngi
            tp = jnp.maximum(t - 1, 0)
            at0 = (t == 0).astype(F32)
            keep = 1.0 - at0
            pr = keep * x_ref[tp, 0:SCAN_ROWS, :] + at0 * (1.0 - first) * xp_ref[0, 0:SCAN_ROWS, :]
            pi = keep * x_ref[tp, SCAN_ROWS:2 * SCAN_ROWS, :] + at0 * (1.0 - first) * xp_ref[0, SCAN_ROWS:2 * SCAN_ROWS, :]
            return ngr, ngi, dar + ngr * pr + ngi * pi, dai + ngi * pr - ngr * pi

        zero = jnp.zeros((SCAN_ROWS, SCAN_LANES), F32)
        gr, gi, dar, dai = lax.fori_loop(0, S5_TB, step, (st[0], st[1], zero, zero))
        st[0] = gr
        st[1] = gi
        da_ref[0] += dar
        da_ref[1] += dai

    rev = lambda i: nb - 1 - i
    blk = pl.BlockSpec((S5_TB, 2 * SCAN_ROWS, SCAN_LANES), lambda i: (rev(i), 0, 0))
    prev = pl.BlockSpec((1, 2 * SCAN_ROWS, SCAN_LANES), lambda i: (jnp.maximum(rev(i) * S5_TB - 1, 0), 0, 0))
    cst = pl.BlockSpec((SCAN_ROWS, SCAN_LANES), lambda i: (0, 0))
    return pl.pallas_call(
        body, name="s5_scan_bwd", grid=(nb,), in_specs=[blk, blk, prev, cst, cst],
        out_specs=[blk, pl.BlockSpec((2, SCAN_ROWS, SCAN_LANES), lambda i: (0, 0, 0))],
        out_shape=[jax.ShapeDtypeStruct(gx.shape, F32), jax.ShapeDtypeStruct((2, SCAN_ROWS, SCAN_LANES), F32)],
        scratch_shapes=[pltpu.VMEM((2, SCAN_ROWS, SCAN_LANES), F32)],
        compiler_params=_params(("arbitrary",)),
    )(gx, x, x, a_re, a_im)


def _place():
    x, y, c = lax.axis_index("x"), lax.axis_index("y"), lax.axis_index("c")
    return x, y, c, [(1 - x, y), (x, 1 - y), (1 - x, 1 - y)]


def _gather_phases(nrows):
    half = (nrows // 32) * 16
    assert 0 < half < nrows

    def plan(x_ref, out_ref, send_sems, recv_sems, local_sem):
        x, y, c, _ = _place()
        me, sibling = (x, y, c), (x, y, 1 - c)
        at_x, at_y, at_d = (1 - x, y, c), (x, 1 - y, c), (1 - x, 1 - y, c)

        def rows(block, part=None):
            idx = 4 * block[0] + 2 * block[1] + block[2]
            if part is None:
                return out_ref.at[idx]
            return out_ref.at[idx, pl.ds(0, half)] if part == 0 else out_ref.at[idx, pl.ds(half, nrows - half)]

        def copy(k, block, to, src=None, part=None):
            return pltpu.make_async_remote_copy(
                src_ref=rows(block, part) if src is None else src, dst_ref=rows(block, part),
                send_sem=send_sems.at[k], recv_sem=recv_sems.at[k], device_id=to, device_id_type=MESH)

        def other_core(block):
            return (block[0], block[1], 1 - c)

        mine = pltpu.make_async_copy(x_ref, rows(me), local_sem)
        direct = [copy(0, me, sibling, src=x_ref), copy(1, me, at_x, src=x_ref), copy(2, me, at_y, src=x_ref)]
        relays = [copy(3, at_x, at_y, part=0), copy(7, at_y, at_x, part=1)]
        passed = [copy(4, at_x, sibling), copy(5, at_y, sibling), copy(6, at_d, sibling)]
        landed = [copy(1, at_x, me), copy(2, at_y, me), copy(3, at_d, me, part=0), copy(7, at_d, me, part=1)]
        from_sibling = [copy(0, sibling, me)] + [copy(4 + j, other_core(b), me) for j, b in enumerate((at_x, at_y, at_d))]
        return mine, direct, relays, passed, landed, from_sibling

    def start(ins, outs, sems):
        mine, direct, _, _, _, _ = plan(ins[0], outs[0], *sems)
        mine.start()
        for cp in direct:
            cp.start()

    def middle(ins, outs, sems):
        _, _, relays, passed, landed, _ = plan(ins[0], outs[0], *sems)
        for j in range(2):
            landed[j].wait_recv()
            passed[j].start()
            relays[j].start()

    def late(ins, outs, sems):
        _, _, _, passed, landed, _ = plan(ins[0], outs[0], *sems)
        landed[2].wait_recv()
        landed[3].wait_recv()
        passed[2].start()

    def finish(ins, outs, sems):
        mine, direct, relays, passed, _, from_sibling = plan(ins[0], outs[0], *sems)
        for cp in from_sibling:
            cp.wait_recv()
        for cp in direct + relays + passed:
            cp.wait_send()
        mine.wait()

    return start, middle, late, finish


_GATHER_SEMS = [pltpu.SemaphoreType.DMA((8,)), pltpu.SemaphoreType.DMA((8,)), pltpu.SemaphoreType.DMA]


def _all_gather(shard, name):
    phases = _gather_phases(shard.shape[0])

    def body(x_ref, out_ref, *sems):
        for phase in phases:
            phase([x_ref], [out_ref], sems)

    return pl.pallas_call(
        body, name=name, out_shape=jax.ShapeDtypeStruct((N_DEV,) + shard.shape, shard.dtype),
        in_specs=[_ANY], out_specs=_ANY, scratch_shapes=list(_GATHER_SEMS),
    )(shard)


def _gather_hook(shard):
    start, middle, late, finish = _gather_phases(shard.shape[0])
    return _Hook([shard], [jax.ShapeDtypeStruct((N_DEV,) + shard.shape, shard.dtype)], _GATHER_SEMS,
                 [(0.0, start), (0.5, middle), (0.85, late), (1.0, finish)])


def _swap_with_sibling(parts, name):
    def body(p_ref, out_ref, send_sems, recv_sems):
        x, y, c, _ = _place()
        copies = [pltpu.make_async_remote_copy(
            src_ref=p_ref.at[2 * chip + (1 - c)], dst_ref=out_ref.at[chip],
            send_sem=send_sems.at[chip], recv_sem=recv_sems.at[chip],
            device_id=(x, y, 1 - c), device_id_type=MESH) for chip in range(4)]
        for cp in copies:
            cp.start()
        for cp in copies:
            cp.wait()

    return pl.pallas_call(
        body, name=name, out_shape=jax.ShapeDtypeStruct((4,) + parts.shape[1:], parts.dtype),
        in_specs=[pl.BlockSpec(memory_space=pl.ANY)], out_specs=pl.BlockSpec(memory_space=pl.ANY),
        scratch_shapes=[pltpu.SemaphoreType.DMA((4,)), pltpu.SemaphoreType.DMA((4,))],
    )(parts)


def _chips_phases(lo, rows):
    def copies(p_ref, out_ref, send_sems, recv_sems):
        x, y, c, chips = _place()
        return [pltpu.make_async_remote_copy(
            src_ref=p_ref.at[2 * px + py, pl.ds(lo, rows)], dst_ref=out_ref.at[j],
            send_sem=send_sems.at[j], recv_sem=recv_sems.at[j],
            device_id=(px, py, c), device_id_type=MESH) for j, (px, py) in enumerate(chips)]

    def start(ins, outs, sems):
        for cp in copies(ins[0], outs[0], *sems):
            cp.start()

    def finish(ins, outs, sems):
        for cp in copies(ins[0], outs[0], *sems):
            cp.wait()

    return start, finish


def _chips_hook(parts, lo=0, hi=None):
    rows = (parts.shape[1] if hi is None else hi) - lo
    start, finish = _chips_phases(lo, rows)
    return _Hook([parts], [jax.ShapeDtypeStruct((3, rows) + parts.shape[2:], parts.dtype)],
                 [pltpu.SemaphoreType.DMA((3,)), pltpu.SemaphoreType.DMA((3,))], [(0.0, start), (1.0, finish)])


BIG_LAYOUT = (("w_in_ab", D_MODEL, PACK_COLS), ("s5_w_glu", S5_W // N_DEV, PACK_COLS),
              ("w_out_ab", OUT_AB // N_DEV, 2 * PACK_COLS), ("w_in_c", D_MODEL, PACK_COLS),
              ("w_out_c", GLA_W // N_DEV, 2 * PACK_COLS))


def _to_rows(a):
    if a.shape[-1] == PACK_COLS:
        return a
    assert a.shape[-1] == 2 * PACK_COLS
    return jnp.concatenate([a[..., :PACK_COLS], a[..., PACK_COLS:]], axis=-2)


def _from_rows(p, cols):
    if cols == PACK_COLS:
        return p
    r = p.shape[-2] // 2
    return jnp.concatenate([p[..., :r, :], p[..., r:, :]], axis=-1)


FIRST_LAYOUT = BIG_LAYOUT[:1]
OTHER_LAYOUT = BIG_LAYOUT[1:3] + BIG_LAYOUT[4:]
GLU_AB_LAYOUT = BIG_LAYOUT[1:3]
IN_C_LAYOUT = BIG_LAYOUT[3:4]


def _pack_big(pieces, layout):
    return jnp.concatenate([_to_rows(pieces[name]) for name, _, _ in layout], axis=-2)


def _unpack_big(buf, layout):
    out, o = {}, 0
    for name, rows, cols in layout:
        r = rows * cols // PACK_COLS
        out[name] = _from_rows(buf[..., o:o + r, :], cols)
        o += r
    return out


def _column_windows(g):
    rows, quarter = g.shape[0], WIN_COLS // 4

    def body(g_ref, o_ref):
        o_ref[...] = g_ref[...]

    return pl.pallas_call(
        body, name="w_in_c_grad_windows", grid=(N_DEV, WIN_COLS // quarter),
        in_specs=[pl.BlockSpec((rows, quarter), lambda d, c: (0, (WIN_STEP // quarter) * d + c))],
        out_specs=pl.BlockSpec((None, rows, quarter), lambda d, c: (d, 0, c)),
        out_shape=jax.ShapeDtypeStruct((N_DEV, rows, WIN_COLS), g.dtype),
        compiler_params=_params(("arbitrary", "arbitrary")),
    )(g)


def _rows1024(a):
    r, c = a.shape
    if c > PACK_COLS:
        a = jnp.concatenate([a[:, i * PACK_COLS:(i + 1) * PACK_COLS] for i in range(c // PACK_COLS)], axis=0)
    elif c < PACK_COLS:
        a = jnp.pad(a, ((0, 0), (0, PACK_COLS - c)))
    return jnp.pad(a, ((0, -a.shape[0] % 8), (0, 0)))


def _unrows1024(p, r, c):
    if c > PACK_COLS:
        return jnp.concatenate([p[i * r:(i + 1) * r] for i in range(c // PACK_COLS)], axis=1)
    return p[:r, :c]


def _lane_select(a, off, sign, n_out, out_dtype, exact, name):
    rows, n_in = a.shape
    tr = _tile(rows, 256, 16)

    def body(off_ref, a_ref, o_ref):
        sel = _iota2((n_in, n_out), 0) + off_ref[0] * sign == _iota2((n_in, n_out), 1)
        if exact:
            r = jnp.dot(a_ref[...], sel.astype(F32), precision=lax.Precision.HIGHEST, preferred_element_type=F32)
        else:
            r = _dg(a_ref[...], sel.astype(BF16), 1, 0)
        o_ref[...] = r.astype(out_dtype)

    return pl.pallas_call(
        body, name=name, grid=(rows // tr,),
        in_specs=[pl.BlockSpec(memory_space=pltpu.SMEM), pl.BlockSpec((tr, n_in), lambda i: (i, 0))],
        out_specs=pl.BlockSpec((tr, n_out), lambda i: (i, 0)),
        out_shape=jax.ShapeDtypeStruct((rows, n_out), out_dtype),
        compiler_params=_params(("arbitrary",)),
    )(off, a)


def _adamw(w, g, m, v, name):
    rows, cols = w.shape
    tr = _tile(rows, 256, 8) if rows % 8 == 0 else rows

    def fn(i, w_, g_, m_, v_):
        m_new = ADAM_B1 * m_ + (1.0 - ADAM_B1) * g_
        v_new = ADAM_B2 * v_ + (1.0 - ADAM_B2) * (g_ * g_)
        m_hat = m_new / (1.0 - ADAM_B1 ** ADAM_STEP)
        v_hat = v_new / (1.0 - ADAM_B2 ** ADAM_STEP)
        delta = -ADAM_LR * (m_hat / (jnp.sqrt(v_hat) + ADAM_EPS) + ADAM_WD * w_)
        return (delta, m_new, v_new), ()

    outs, _ = _rows(fn, [_win(w), _win(g), _win(m), _win(v)], [], [(cols, F32)] * 3, [], name=name,
                    nrow=rows, tr=tr)
    return outs


def _as2d(a):
    if a.ndim == 1:
        return a.reshape(1, -1)
    if a.ndim == 2:
        return a
    a = a.reshape(a.shape[1:])
    return a if a.ndim == 2 else a.reshape(a.shape[0], -1)


def kernel(x, meta, norm_ab_w, w_in_ab, ret_norm_w, s5_lam_re, s5_lam_im, s5_log_dt, s5_b_re, s5_b_im, s5_c_re, s5_c_im, s5_d, s5_w_glu, w_out_ab, norm_c_w, w_in_c, gla_w_gate, gla_b_gate, gla_norm_w, w_out_c, final_norm_w, loss_target, m_meta, m_norm_ab_w, m_w_in_ab, m_ret_norm_w, m_s5_lam_re, m_s5_lam_im, m_s5_log_dt, m_s5_b_re, m_s5_b_im, m_s5_c_re, m_s5_c_im, m_s5_d, m_s5_w_glu, m_w_out_ab, m_norm_c_w, m_w_in_c, m_gla_w_gate, m_gla_b_gate, m_gla_norm_w, m_w_out_c, m_final_norm_w, v_meta, v_norm_ab_w, v_w_in_ab, v_ret_norm_w, v_s5_lam_re, v_s5_lam_im, v_s5_log_dt, v_s5_b_re, v_s5_b_im, v_s5_c_re, v_s5_c_im, v_s5_d, v_s5_w_glu, v_w_out_ab, v_norm_c_w, v_w_in_c, v_gla_w_gate, v_gla_b_gate, v_gla_norm_w, v_w_out_c, v_final_norm_w):
    weights = dict(meta=meta, norm_ab_w=norm_ab_w, w_in_ab=w_in_ab, ret_norm_w=ret_norm_w, s5_lam_re=s5_lam_re,
                   s5_lam_im=s5_lam_im, s5_log_dt=s5_log_dt, s5_b_re=s5_b_re, s5_b_im=s5_b_im, s5_c_re=s5_c_re,
                   s5_c_im=s5_c_im, s5_d=s5_d, s5_w_glu=s5_w_glu, w_out_ab=w_out_ab, norm_c_w=norm_c_w,
                   w_in_c=w_in_c, gla_w_gate=gla_w_gate, gla_b_gate=gla_b_gate, gla_norm_w=gla_norm_w,
                   w_out_c=w_out_c, final_norm_w=final_norm_w)
    mom_m = dict(meta=m_meta, norm_ab_w=m_norm_ab_w, w_in_ab=m_w_in_ab, ret_norm_w=m_ret_norm_w,
                 s5_lam_re=m_s5_lam_re, s5_lam_im=m_s5_lam_im, s5_log_dt=m_s5_log_dt, s5_b_re=m_s5_b_re,
                 s5_b_im=m_s5_b_im, s5_c_re=m_s5_c_re, s5_c_im=m_s5_c_im, s5_d=m_s5_d, s5_w_glu=m_s5_w_glu,
                 w_out_ab=m_w_out_ab, norm_c_w=m_norm_c_w, w_in_c=m_w_in_c, gla_w_gate=m_gla_w_gate,
                 gla_b_gate=m_gla_b_gate, gla_norm_w=m_gla_norm_w, w_out_c=m_w_out_c, final_norm_w=m_final_norm_w)
    mom_v = dict(meta=v_meta, norm_ab_w=v_norm_ab_w, w_in_ab=v_w_in_ab, ret_norm_w=v_ret_norm_w,
                 s5_lam_re=v_s5_lam_re, s5_lam_im=v_s5_lam_im, s5_log_dt=v_s5_log_dt, s5_b_re=v_s5_b_re,
                 s5_b_im=v_s5_b_im, s5_c_re=v_s5_c_re, s5_c_im=v_s5_c_im, s5_d=v_s5_d, s5_w_glu=v_s5_w_glu,
                 w_out_ab=v_w_out_ab, norm_c_w=v_norm_c_w, w_in_c=v_w_in_c, gla_w_gate=v_gla_w_gate,
                 gla_b_gate=v_gla_b_gate, gla_norm_w=v_gla_norm_w, w_out_c=v_w_out_c, final_norm_w=v_final_norm_w)
    order = list(weights)

    seq = x.shape[1]
    lp = CHUNK + seq
    nchunk = lp // CHUNK
    dev = 4 * lax.axis_index("x") + 2 * lax.axis_index("y") + lax.axis_index("c")
    core = lax.axis_index("c")
    chip = 2 * lax.axis_index("x") + lax.axis_index("y")

    win_off = jnp.reshape(2 * dev, (1,)).astype(jnp.int32)
    shard_c = jnp.pad(w_in_c[0].astype(BF16), ((0, 0), (0, 896 - SHARD_C)))
    big_shards = dict(w_in_ab=w_in_ab[0].astype(BF16), s5_w_glu=s5_w_glu[0].astype(BF16),
                      w_out_ab=w_out_ab[0].astype(BF16), w_out_c=w_out_c[0].astype(BF16),
                      w_in_c=_lane_select(shard_c, win_off, 1, WIN_COLS, BF16, False, "w_in_c_to_window"))
    def pad_to(a, rows, cols):
        return jnp.pad(a, ((0, rows - a.shape[0]), (0, cols - a.shape[1])))

    shard_w = D_MODEL // N_DEV
    small_pack = jnp.concatenate([meta, pad_to(norm_c_w, 8, shard_w), pad_to(gla_w_gate[0], GLA_RANK, shard_w),
                                  pad_to(gla_b_gate, 8, shard_w), pad_to(gla_norm_w, 8, shard_w)], axis=0)
    w_in_ab_g = _all_gather(big_shards["w_in_ab"], "gather_first")
    win_cut = 1408
    in_c_hook_a = _gather_hook(big_shards["w_in_c"][:win_cut])
    in_c_hook_b = _gather_hook(big_shards["w_in_c"][win_cut:])
    glu_ab_hook = _gather_hook(_pack_big(big_shards, GLU_AB_LAYOUT))
    out_c_hook = _gather_hook(_to_rows(big_shards["w_out_c"]))
    gs = _all_gather(small_pack, "gather_small")
    gate_w = GLA_QK // N_DEV
    s_meta, s_norm_c = gs[:, :N_META], gs[:, N_META]
    s_wgate, s_bgate, s_gnorm = gs[:, 24:24 + GLA_RANK, :gate_w], gs[:, 40, :gate_w], gs[:, 48]
    meta_f = s_meta.transpose(1, 0, 2).reshape(N_META, D_MODEL)
    norm_c_f = s_norm_c.reshape(1, D_MODEL)
    w_gate_f = jnp.pad(s_wgate.transpose(1, 0, 2).reshape(GLA_RANK, GLA_QK), ((0, GATE_PAD - GLA_RANK), (0, 0)))
    b_gate_f = s_bgate.reshape(1, GLA_QK)
    gla_norm_f = s_gnorm.reshape(GLA_H, 1, GLA_DV)

    pos = jnp.maximum(jnp.arange(lp, dtype=F32) - float(PAD), 0.0)
    inv_freq = jnp.power(ROPE_BASE, -jnp.arange(0, RET_DK, 2, dtype=F32) / RET_DK)
    ang = pos[:, None] * inv_freq[None, :]
    cos2 = jnp.concatenate([jnp.cos(ang), jnp.cos(ang)], axis=1)
    sin2 = jnp.concatenate([-jnp.sin(ang), jnp.sin(ang)], axis=1)
    log_g = jnp.log1p(-jnp.exp2(-5.0 - jnp.arange(RET_H, dtype=F32)))
    lg = jnp.broadcast_to(log_g[:, None, None], (RET_H, 1, 128))
    ret_norm_h = ret_norm_w.reshape(RET_H, 1, RET_DV)

    h0 = jnp.concatenate([jnp.zeros((PAD, D_MODEL), F32), meta_f, x[0]], axis=0)

    def rowmask(i):
        return (_iota2((CHUNK, 1), 0) + i * CHUNK) >= PAD

    (hn0,), _ = _rows(lambda i, h, w: ((_rms(h, w),), ()), [_win(h0)], [norm_ab_w], [(D_MODEL, BF16)], [],
                      name="norm_ab_fwd", nrow=lp)
    proj_ab, (w_in_c_ga,) = _mm(hn0, w_in_ab_g, "nn", name="in_ab_fwd", hook=in_c_hook_a, b_dev=True)

    q_off, k_off, v_off, za_off = 0, RET_QK, 2 * RET_QK, 2 * RET_QK + RET_W
    u_off, zb_off = 2 * RET_QK + 2 * RET_W, 2 * RET_QK + 2 * RET_W + S5_W
    ret_xs = [(proj_ab, RET_DK, lambda h: q_off // RET_DK + h), (proj_ab, RET_DK, lambda h: k_off // RET_DK + h),
              (proj_ab, RET_DV, lambda h: v_off // RET_DV + h), (proj_ab, RET_DV, lambda h: za_off // RET_DV + h)]
    ret_cs = [(cos2, RET_DK, lambda h: 0), (sin2, RET_DK, lambda h: 0)]
    ret_kw = dict(heads=RET_H, nchunk=nchunk, s_shape=(RET_DK, RET_DV), out_w=RET_DV, pre=_ret_pre, hpb=4)
    o_a, ret_sprev, (gathered_glu_ab,) = _scan_fwd(_ret_chunk, ret_xs, ret_cs, [ret_norm_h], [lg], name="ret_fwd",
                                                   hook=glu_ab_hook, **ret_kw)
    gb = _unpack_big(gathered_glu_ab, GLU_AB_LAYOUT)
    w_glu_f = gb["s5_w_glu"].reshape(S5_W, S5_W)
    w_out_ab_f = gb["w_out_ab"].reshape(OUT_AB, D_MODEL)

    expand = jnp.repeat(jnp.eye(S5_P, dtype=F32), S5_GH, axis=1)
    disc_args = (s5_lam_re[0], s5_lam_im[0], s5_log_dt[0].reshape(S5_G, 1),
                 s5_b_re[0].reshape(S5_G, S5_P * S5_GH), s5_b_im[0].reshape(S5_G, S5_P * S5_GH), expand)
    ab_re, ab_im, bb_re, bb_im = _s5_disc_fwd(disc_args)
    gt = TILE_G
    eye_t = jnp.eye(gt, dtype=F32)

    def tiles_in(bb):
        return jnp.einsum("sgph,gk->sghkp", bb.reshape(gt, gt, S5_P, S5_GH), eye_t).reshape(gt, 128, TILE_W)

    def tiles_out(cc):
        return jnp.einsum("sghp,gk->sgpkh", cc.reshape(gt, gt, S5_GH, S5_P), eye_t).reshape(gt, TILE_W, 128)

    wb_t = jnp.concatenate([tiles_in(bb_re), tiles_in(bb_im)], axis=0).astype(BF16)
    wc_t = jnp.concatenate([tiles_out(s5_c_re[0]), -tiles_out(s5_c_im[0])], axis=0).astype(BF16)
    a_re, a_im = ab_re.reshape(SCAN_ROWS, SCAN_LANES), ab_im.reshape(SCAN_ROWS, SCAN_LANES)
    tm5, tk5, nt5 = _tile(lp, 1408, 8), _tile(lp, 1408, 8), 2 * gt
    u_blk = u_off // 128
    wide_k = pl.BlockSpec((tm5, TILE_W), lambda i, j, k: (i, k * gt + j))
    narrow = pl.BlockSpec((tm5, 128), lambda i, j, k: (i, j))
    wb_k = pl.BlockSpec((None, 128, TILE_W), lambda i, j, k: (k * gt + j, 0, 0))
    wc_k = pl.BlockSpec((None, TILE_W, 128), lambda i, j, k: (k * gt + j, 0, 0))
    bu3, (w_out_c_g,) = _s5_expand(proj_ab, wb_t, a_blk=u_off // S5_W, dims=NN, name="s5_bu", hook=out_c_hook)
    w_out_c_f = _from_rows(w_out_c_g, D_MODEL).reshape(GLA_W, D_MODEL)
    xs5 = _s5_scan_fwd(bu3, a_re, a_im)
    xs5_2d = xs5.reshape(lp, 2 * S5_N)
    y_pre, (w_in_c_gb,) = _mm_core(xs5_2d, wc_t, dims=NN, grid=(lp // tm5, gt, 2), name="s5_cx", a_spec=wide_k,
                                   b_spec=wc_k, o_spec=narrow, out_shape=jax.ShapeDtypeStruct((lp, S5_W), F32),
                                   acc_shape=(tm5, 128), hook=in_c_hook_b)
    (y_s5, yg_bf), _ = _rows(
        lambda i, yp, u, d: ((yp + d * u, _gelu(yp + d * u)), ()),
        [_win(y_pre), _win(proj_ab, u_off, S5_W)], [s5_d], [(S5_W, F32), (S5_W, BF16)], [], name="s5_gelu_fwd", nrow=lp)
    t_glu = _mm(yg_bf, w_glu_f, "nn", name="s5_glu_fwd")

    def s5_gate(y, t, zb):
        return _gelu(y) * _sigmoid(t) * _silu(zb)

    (o_b,), _ = _rows(lambda i, y, t, zb: ((s5_gate(y, t, zb),), ()),
                      [_win(y_s5), _win(t_glu), _win(proj_ab, zb_off, S5_W)], [], [(S5_W, BF16)], [],
                      name="s5_gate_fwd", nrow=lp)
    o_ab = jnp.concatenate([o_a, o_b], axis=1)
    h1 = _mm(o_ab, w_out_ab_f, "nn", name="out_ab_fwd", add=h0)
    w_in_c_g = jnp.concatenate([w_in_c_ga, w_in_c_gb], axis=1)
    w_in_c_f = sum(jnp.pad(w_in_c_g[d], ((0, 0), (WIN_STEP * d, IN_C_PAD - WIN_STEP * d - WIN_COLS)))
                   for d in range(N_DEV))

    (hn1,), _ = _rows(lambda i, h, w: ((_rms(h, w),), ()), [_win(h1)], [norm_c_f], [(D_MODEL, BF16)], [],
                      name="norm_c_fwd", nrow=lp)
    proj_c = _mm(hn1, w_in_c_f, "nn", name="in_c_fwd")
    gl_off = 2 * GLA_QK + 2 * GLA_W
    pre_gate = _mm(proj_c, w_gate_f, "nn", name="gate_fwd", a_win=(gl_off, GATE_PAD), bias=b_gate_f)
    gla_xs = [(proj_c, GLA_DK, lambda h: h), (proj_c, GLA_DK, lambda h: GLA_QK // GLA_DK + h),
              (proj_c, GLA_DV, lambda h: 2 * GLA_QK // GLA_DV + h),
              (proj_c, GLA_DV, lambda h: (2 * GLA_QK + GLA_W) // GLA_DV + h),
              (pre_gate, GLA_DK, lambda h: h)]
    gla_kw = dict(heads=GLA_H, nchunk=nchunk, s_shape=(GLA_DV, GLA_DK), out_w=GLA_DV, hpb=GLA_H)
    o_c, gla_sprev = _scan_fwd(_gla_chunk, gla_xs, [], [gla_norm_f], [], name="gla_fwd", **gla_kw)
    h2 = _mm(o_c, w_out_c_f, "nn", name="out_c_fwd", add=h1)

    fnw = final_norm_w.reshape(1, D_MODEL)

    def final_fn(i, h, tgt, w):
        def loss_of(h_, w_):
            err = _rms(h_, w_) - tgt
            return 0.5 * jnp.sum(jnp.mean(err * err, axis=-1))

        real = (i > 0).astype(F32)
        loss_i, (dh, dw) = jax.value_and_grad(loss_of, argnums=(0, 1))(h, w)
        return (dh * real, dh * real), (jnp.full((1, 128), loss_i * real, F32), dw * real)

    (dh2, dh2_bf), (loss_acc, g_final) = _rows(
        final_fn, [_win(h2), _win(loss_target[0], roff=1)], [fnw], [(D_MODEL, F32), (D_MODEL, BF16)],
        [(1, 128), (1, D_MODEL)], name="final_loss", nrow=lp)

    def rs_front(pieces, layout, tag):
        g_full = _pack_big(pieces, layout)
        prow = g_full.shape[1]
        from_sibling = _swap_with_sibling(g_full, "rs_sibling_" + tag)
        mine_by_chip = lax.dynamic_index_in_dim(g_full.reshape(4, 2, prow, PACK_COLS), core, axis=1, keepdims=False)
        (p1, p1_bf), _ = _rows(
            lambda i, a, b: ((a.astype(F32) + b.astype(F32), a.astype(F32) + b.astype(F32)), ()),
            [_win(mine_by_chip.reshape(4 * prow, PACK_COLS)), _win(from_sibling.reshape(4 * prow, PACK_COLS))], [],
            [(PACK_COLS, F32), (PACK_COLS, BF16)], [], name="rs_sum_sibling_" + tag, nrow=4 * prow,
            tr=_tile(prow, 512, 16))
        return p1.reshape(4, prow, PACK_COLS), p1_bf.reshape(4, prow, PACK_COLS)

    def rs_back(p1, from_chips, layout, tag):
        prow = p1.shape[1]
        tr = _tile(prow, 512, 16)
        own = lax.dynamic_index_in_dim(p1, chip, axis=0, keepdims=False)
        fc2 = from_chips.reshape(3 * prow, PACK_COLS)
        nblk = prow // tr
        (g_shard,), _ = _rows(
            lambda i, a, b0, b1, b2: ((((a + b0.astype(F32)) + b1.astype(F32)) + b2.astype(F32),), ()),
            [_win(own), _win(fc2), _win(fc2, roff=-nblk), _win(fc2, roff=-2 * nblk)], [], [(PACK_COLS, F32)], [],
            name="rs_sum_chips_" + tag, nrow=prow, tr=tr)
        return _unpack_big(g_shard, layout)

    do_c = _mm(dh2_bf, w_out_c_f, "nt", name="out_c_dx", out_dtype=BF16)
    gw_out_c = _mm(o_c, dh2_bf, "tn", name="out_c_dw", out_dtype=BF16)
    (dq_c, dk_c, dv_c, dz_c, dpre), (g_gla_norm,) = _scan_bwd(
        _gla_chunk, gla_xs, [], [gla_norm_f], [], do_c, gla_sprev, name="gla_bwd", **gla_kw)
    dglow = _mm(dpre, w_gate_f, "nt", name="gate_dx", out_dtype=BF16)
    g_wgate = _mm(proj_c, dpre, "tn", name="gate_dw", a_win=(gl_off, GATE_PAD))[:GLA_RANK]
    (), (g_bgate,) = _rows(lambda i, d: ((), (jnp.sum(d.astype(F32), axis=0, keepdims=True),)), [_win(dpre)], [], [],
                           [(1, GLA_QK)], name="gate_db", nrow=lp)
    dproj_c = jnp.concatenate([dq_c, dk_c, dv_c, dz_c, dglow], axis=1)
    dhn1 = _mm(dproj_c, w_in_c_f, "nt", name="in_c_dx")
    gw_in_c = _mm(hn1, dproj_c, "tn", name="in_c_dw", out_dtype=BF16)
    p1_c, p1_c_bf = rs_front(dict(
        w_in_c=_column_windows(gw_in_c)),
        IN_C_LAYOUT, "in_c")

    def norm_bwd(i, h, dhn, dres, w):
        _, vjp = jax.vjp(_rms, h, w)
        dh, dw = vjp(dhn)
        return (jnp.where(rowmask(i), dh + dres, 0.0),), (dw,)

    def norm_bwd_both(i, h, dhn, dres, w):
        (dh,), acc = norm_bwd(i, h, dhn, dres, w)
        return (dh, dh), acc

    (dh1, dh1_bf), (g_norm_c,) = _rows(norm_bwd_both, [_win(h1), _win(dhn1), _win(dh2)], [norm_c_f],
                                       [(D_MODEL, F32), (D_MODEL, BF16)], [(1, D_MODEL)], name="norm_c_bwd", nrow=lp)

    do_ab = _mm(dh1_bf, w_out_ab_f, "nt", name="out_ab_dx", out_dtype=BF16)
    gw_out_ab = _mm(o_ab, dh1_bf, "tn", name="out_ab_dw", out_dtype=BF16)

    def s5_gate_bwd(i, dob, y, t, zb):
        _, vjp = jax.vjp(s5_gate, y, t, zb)
        dy, dt, dzb = vjp(dob.astype(F32))
        return (dy, dt, dzb), ()

    (dy_a, dt_glu, dzb), _ = _rows(
        s5_gate_bwd, [_win(do_ab, RET_W, S5_W), _win(y_s5), _win(t_glu), _win(proj_ab, zb_off, S5_W)], [],
        [(S5_W, F32), (S5_W, BF16), (S5_W, BF16)], [], name="s5_gate_bwd", nrow=lp)
    dyg2 = _mm(dt_glu, w_glu_f, "nt", name="s5_glu_dx")
    gw_glu = _mm(yg_bf, dt_glu, "tn", name="s5_glu_dw", out_dtype=BF16)

    def s5_y_bwd(i, dya, dyg, y, u, d):
        _, vjp = jax.vjp(_gelu, y)
        (dy_g,) = vjp(dyg)
        dy = dya + dy_g
        return (dy, d * dy), (jnp.sum(dy * u, axis=0, keepdims=True),)

    (dy_s5, du1), (g_d,) = _rows(
        s5_y_bwd, [_win(dy_a), _win(dyg2), _win(y_s5), _win(proj_ab, u_off, S5_W)], [s5_d],
        [(S5_W, BF16), (S5_W, F32)], [(1, S5_W)], name="s5_y_bwd", nrow=lp)
    p1_o, p1_o_bf = rs_front(dict(s5_w_glu=gw_glu.reshape(N_DEV, S5_W // N_DEV, S5_W),
                                  w_out_ab=gw_out_ab.reshape(N_DEV, OUT_AB // N_DEV, D_MODEL),
                                  w_out_c=gw_out_c.reshape(N_DEV, GLA_W // N_DEV, D_MODEL)), OTHER_LAYOUT, "other")
    o_cut = 640
    gx3, (from_chips_oa,) = _s5_expand(dy_s5, wc_t, a_blk=0, dims=NT, name="s5_cx_dx",
                                       hook=_chips_hook(p1_o_bf, 0, o_cut))
    rows_k = lambda col: pl.BlockSpec((tk5, col), lambda i, j, k: (k, i))
    gwc = _mm_core(xs5_2d, dy_s5, dims=TN, grid=(nt5, 1, lp // tk5), name="s5_cx_dw", a_spec=rows_k(TILE_W),
                   b_spec=pl.BlockSpec((tk5, 128), lambda i, j, k: (k, i % gt)),
                   o_spec=pl.BlockSpec((None, TILE_W, 128), lambda i, j, k: (i, 0, 0)),
                   out_shape=jax.ShapeDtypeStruct((nt5, TILE_W, 128), F32), acc_shape=(TILE_W, 128))
    g_s5, da = _s5_scan_bwd(gx3, xs5, a_re, a_im)
    g_s5_2d = g_s5.reshape(lp, 2 * S5_N)
    du, (from_chips_ob,) = _mm_core(g_s5_2d, wb_t, dims=NT, grid=(lp // tm5, gt, 2), name="s5_bu_dx", a_spec=wide_k,
                                    b_spec=wb_k, o_spec=narrow, out_shape=jax.ShapeDtypeStruct((lp, S5_W), BF16),
                                    acc_shape=(tm5, 128), extra=[(du1, narrow)],
                                    hook=_chips_hook(p1_o_bf, o_cut, None))
    from_chips_o = jnp.concatenate([from_chips_oa, from_chips_ob], axis=1)
    gwb = _mm_core(proj_ab, g_s5_2d, dims=TN, grid=(nt5, 1, lp // tk5), name="s5_bu_dw",
                   a_spec=pl.BlockSpec((tk5, 128), lambda i, j, k: (k, u_blk + i % gt)), b_spec=rows_k(TILE_W),
                   o_spec=pl.BlockSpec((None, 128, TILE_W), lambda i, j, k: (i, 0, 0)),
                   out_shape=jax.ShapeDtypeStruct((nt5, 128, TILE_W), F32), acc_shape=(128, TILE_W))
    gwc6 = gwc.reshape(2, gt, gt, S5_P, gt, S5_GH)
    g_c = jnp.einsum("rsgpgh->rsghp", gwc6).reshape(2, S5_G, S5_GH, S5_P)
    g_c_re, g_c_im = g_c[0], -g_c[1]
    gwb6 = gwb.reshape(2, gt, gt, S5_GH, gt, S5_P)
    d_bb = jnp.einsum("rsghgp->rsgph", gwb6).reshape(2, S5_G, S5_P * S5_GH)
    d_bb_re, d_bb_im = d_bb[0], d_bb[1]
    g_lam_re, g_lam_im, g_log_dt, g_b_re, g_b_im = _s5_disc_bwd(
        disc_args, (da[0].reshape(S5_G, S5_P), da[1].reshape(S5_G, S5_P), d_bb_re, d_bb_im))

    (dq_a, dk_a, dv_a, dz_a), (g_ret_norm,), (from_chips_c,) = _scan_bwd(
        _ret_chunk, ret_xs, ret_cs, [ret_norm_h], [lg], do_ab, ret_sprev, name="ret_bwd", post=_ret_post,
        hook=_chips_hook(p1_c_bf), **ret_kw)
    dproj_ab = jnp.concatenate([dq_a, dk_a, dv_a, dz_a, du, dzb], axis=1)

    lane = lambda a_: pad_to(a_, a_.shape[0], 128)

    def sum8(i, *blocks):
        acc = blocks[0]
        for b in blocks[1:]:
            acc = acc + b
        return (acc,), ()

    def pack_small(pieces):
        return jnp.concatenate([_rows1024(p) for _, p in pieces], axis=0)

    def sum_small(gathered, pieces, tag):
        srow = gathered.shape[1]
        tr = _tile(srow, 128, 8)
        flat = gathered.reshape(N_DEV * srow, PACK_COLS)
        (total,), _ = _rows(sum8, [_win(flat, roff=-d * (srow // tr)) for d in range(N_DEV)], [], [(PACK_COLS, F32)],
                            [], name="sum_small_" + tag, nrow=srow, tr=tr)
        out, o = {}, 0
        for name_, p in pieces:
            r8 = _rows1024(p).shape[0]
            out[name_] = _unrows1024(total[o:o + r8], *p.shape)
            o += r8
        return out

    early_pieces = [
        ("vec2048", jnp.concatenate([g_final, g_norm_c], axis=0)),
        ("vec1024", jnp.concatenate([g_d, g_bgate, pad_to(loss_acc[:, :1], 1, PACK_COLS)], axis=0)),
        ("lam3", jnp.concatenate([lane(g_lam_re), lane(g_lam_im), lane(g_log_dt)], axis=1)),
        ("s5_b_re", g_b_re), ("s5_b_im", g_b_im),
        ("s5_c_re", g_c_re.reshape(S5_G, S5_GH * S5_P)), ("s5_c_im", g_c_im.reshape(S5_G, S5_GH * S5_P)),
        ("ret_norm_w", g_ret_norm.reshape(RET_H, RET_DV)), ("gla_norm_w", g_gla_norm.reshape(GLA_H, GLA_DV)),
        ("gla_w_gate", g_wgate)]
    gw_in_ab, (early_all,) = _mm(
        hn0, dproj_ab, "tn", name="in_ab_dw", out_dest=True, out_dtype=BF16,
        hook=_gather_hook(pack_small(early_pieces)))
    p1_first, p1_first_bf = rs_front(dict(w_in_ab=gw_in_ab), FIRST_LAYOUT, "first")
    dhn0, (from_chips_first,) = _mm(dproj_ab, w_in_ab_g, "nt", name="in_ab_dx", b_dev=True,
                                    hook=_chips_hook(p1_first_bf))
    def norm_bwd_first(i, h, dhn, dres, w):
        (dh,), (dw,) = norm_bwd(i, h, dhn, dres, w)
        return (dh,), (dw, dh * (i == 0).astype(F32))

    (grad_x2d,), (g_norm_ab, dh0_first) = _rows(
        norm_bwd_first, [_win(h0), _win(dhn0), _win(dh1)], [norm_ab_w], [(D_MODEL, F32, 1)],
        [(1, D_MODEL), (CHUNK, D_MODEL)], name="norm_ab_bwd", nrow=lp)
    grad_x = grad_x2d[None]
    late_pieces = [("norm_ab_w", g_norm_ab), ("meta", dh0_first[PAD:CHUNK])]
    small = sum_small(early_all, early_pieces, "early")
    small.update(sum_small(_all_gather(pack_small(late_pieces), "gather_grads"), late_pieces, "late"))

    big_grads = {**rs_back(p1_c, from_chips_c, IN_C_LAYOUT, "in_c"), **rs_back(p1_o, from_chips_o, OTHER_LAYOUT, "other"),
                 **rs_back(p1_first, from_chips_first, FIRST_LAYOUT, "first")}
    big_grads["w_in_c"] = _lane_select(big_grads["w_in_c"], win_off, -1, 896, F32, True,
                                       "w_in_c_from_window")[:, :SHARD_C]
    small["final_norm_w"], small["norm_c_w"] = small["vec2048"][0:1], small["vec2048"][1:2]
    small["s5_d"], small["gla_b_gate"] = small["vec1024"][0:1], small["vec1024"][1:2]
    loss = small["vec1024"][2, 0]
    small["s5_lam_re"], small["s5_lam_im"] = small["lam3"][:, :S5_P], small["lam3"][:, 128:128 + S5_P]
    small["s5_log_dt"] = small["lam3"][:, 256:257]

    def my_cols(g, n):
        return lax.dynamic_slice_in_dim(g, dev * n, n, axis=g.ndim - 1)

    grads = dict(
        meta=my_cols(small["meta"], D_MODEL // N_DEV),
        norm_ab_w=small["norm_ab_w"], w_in_ab=big_grads["w_in_ab"][None], ret_norm_w=small["ret_norm_w"].reshape(1, RET_W),
        s5_lam_re=small["s5_lam_re"][None], s5_lam_im=small["s5_lam_im"][None],
        s5_log_dt=small["s5_log_dt"].reshape(1, S5_G),
        s5_b_re=small["s5_b_re"].reshape(1, S5_G, S5_P, S5_GH), s5_b_im=small["s5_b_im"].reshape(1, S5_G, S5_P, S5_GH),
        s5_c_re=small["s5_c_re"][None], s5_c_im=small["s5_c_im"][None], s5_d=small["s5_d"],
        s5_w_glu=big_grads["s5_w_glu"][None], w_out_ab=big_grads["w_out_ab"][None],
        norm_c_w=my_cols(small["norm_c_w"], D_MODEL // N_DEV), w_in_c=big_grads["w_in_c"][None],
        gla_w_gate=my_cols(small["gla_w_gate"], GLA_QK // N_DEV)[None],
        gla_b_gate=my_cols(small["gla_b_gate"], GLA_QK // N_DEV),
        gla_norm_w=my_cols(small["gla_norm_w"].reshape(1, GLA_W), GLA_W // N_DEV),
        w_out_c=big_grads["w_out_c"][None], final_norm_w=small["final_norm_w"].reshape(D_MODEL))

    deltas, new_m, new_v = {}, {}, {}
    for k in order:
        w = weights[k]
        d2, m2, v2 = _adamw(_as2d(w), _as2d(grads[k].reshape(w.shape)), _as2d(mom_m[k]), _as2d(mom_v[k]), "adamw_" + k)
        deltas[k], new_m[k], new_v[k] = d2.reshape(w.shape), m2.reshape(w.shape), v2.reshape(w.shape)
        grads[k] = grads[k].reshape(w.shape)

    return (loss, grad_x, *[grads[k] for k in order], *[deltas[k] for k in order],
            *[new_m[k] for k in order], *[new_v[k] for k in order])
```

```python
import functools
import math

import jax
import jax.numpy as jnp
from jax import lax
from jax.experimental import pallas as pl
from jax.experimental.pallas import tpu as pltpu

F32, BF16 = jnp.float32, jnp.bfloat16
MESH = pl.DeviceIdType.MESH
N_DEV = 8

D_MODEL = 2048
CHUNK = 128
N_META = 16
PAD = CHUNK - N_META
SUB = 16
EPS = 1e-6
RET_H, RET_DK, RET_DV = 8, 128, 256
RET_QK, RET_W = RET_H * RET_DK, RET_H * RET_DV
ROPE_BASE = 10000.0
S5_W, S5_G, S5_P, S5_GH = 1024, 64, 64, 16
S5_N = S5_G * S5_P
GLA_H, GLA_DK, GLA_DV, GLA_RANK, GLA_TAU = 4, 256, 512, 16, 16.0
GLA_QK, GLA_W = GLA_H * GLA_DK, GLA_H * GLA_DV
IN_AB = 2 * RET_QK + 2 * RET_W + 2 * S5_W
OUT_AB = RET_W + S5_W
IN_C = 2 * GLA_QK + 2 * GLA_W + GLA_RANK
GATE_PAD = 256
IN_C_PAD = 2 * GLA_QK + 2 * GLA_W + GATE_PAD
ADAM_LR, ADAM_B1, ADAM_B2, ADAM_EPS, ADAM_WD, ADAM_STEP = 0.001, 0.9, 0.999, 1e-08, 0.01, 10

VMEM_LIMIT_BYTES = 48 * 2 ** 20
PACK_COLS = 1024
SHARD_C = IN_C // N_DEV
WIN_STEP = 768
WIN_COLS = 1024


def _params(sem):
    return pltpu.CompilerParams(dimension_semantics=sem, vmem_limit_bytes=VMEM_LIMIT_BYTES)


def _tile(n, cap, mult):
    best = None
    for t in range(mult, min(n, cap) + 1, mult):
        if n % t == 0:
            best = t
    assert best is not None, (n, cap, mult)
    return best


def _dg(a, b, ca, cb):
    return lax.dot_general(a.astype(BF16), b.astype(BF16), (((ca,), (cb,)), ((), ())),
                           preferred_element_type=F32)


@functools.partial(jax.custom_vjp, nondiff_argnums=(2, 3))
def _bdot(a, b, ca, cb):
    return _dg(a, b, ca, cb)


def _bdot_fwd(a, b, ca, cb):
    return _dg(a, b, ca, cb), (a, b)


def _bdot_bwd(ca, cb, res, g):
    a, b = res
    da = _dg(g, b, 1, 1 - cb) if ca == 1 else _dg(b, g, 1 - cb, 1)
    db = _dg(a, g, 1 - ca, 0) if cb == 0 else _dg(g, a, 0, 1 - ca)
    return da.astype(a.dtype), db.astype(b.dtype)


_bdot.defvjp(_bdot_fwd, _bdot_bwd)


def _sigmoid(x):
    return 1.0 / (1.0 + jnp.exp(-x))


def _silu(x):
    return x * _sigmoid(x)


def _log_sigmoid(x):
    return jnp.minimum(x, 0.0) - jnp.log(1.0 + jnp.exp(-jnp.abs(x)))


def _gelu(x):
    return 0.5 * x * (1.0 + jnp.tanh(math.sqrt(2.0 / math.pi) * (x + 0.044715 * (x * x * x))))


def _rms(x, w):
    return x * lax.rsqrt(jnp.mean(x * x, axis=-1, keepdims=True) + EPS) * w


class _Hook:
    def __init__(self, ins, outs, sems, phases):
        self.ins, self.outs, self.sems, self.phases = list(ins), list(outs), list(sems), list(phases)


_NO_HOOK = _Hook([], [], [], [])
_ANY = pl.BlockSpec(memory_space=pl.ANY)


def _run_hook(hook, lin, total, in_refs, out_refs, sem_refs):
    for frac, fn in hook.phases:
        at = min(int(frac * total), total - 1)

        @pl.when(lin == at)
        def _(fn=fn):
            fn(in_refs, out_refs, sem_refs)


def _mm_core(a, b, *, dims, grid, a_spec, b_spec, o_spec, out_shape, acc_shape, name, extra=(), hook=None,
             b_parts=0):
    nk = grid[2]
    n_extra = len(extra)
    hook = _NO_HOOK if hook is None else hook
    hi, ho = len(hook.ins), len(hook.outs)

    def body(*refs):
        a_ref, b_ref = refs[0], refs[1]
        o_ref, acc = refs[2 + n_extra + hi], refs[3 + n_extra + hi + ho]
        k = pl.program_id(2)
        lin = (pl.program_id(0) * grid[1] + pl.program_id(1)) * nk + k
        _run_hook(hook, lin, grid[0] * grid[1] * nk, refs[2 + n_extra:2 + n_extra + hi],
                  refs[3 + n_extra + hi:3 + n_extra + hi + ho], refs[4 + n_extra + hi + ho:])

        if b_parts:
            part = sum(lax.dot_general(a_ref[:, d * PACK_COLS:(d + 1) * PACK_COLS].astype(BF16), b_ref[d].astype(BF16),
                                       dims, preferred_element_type=F32) for d in range(b_parts))
        else:
            part = lax.dot_general(a_ref[...].astype(BF16), b_ref[...].astype(BF16), dims, preferred_element_type=F32)

        def finish(r):
            for e in range(n_extra):
                r = r + refs[2 + e][...].astype(F32)
            o_ref[...] = r.astype(o_ref.dtype)

        if nk == 1:
            finish(part)
        else:
            @pl.when(k == 0)
            def _():
                acc[...] = part

            @pl.when(k > 0)
            def _():
                acc[...] += part

            @pl.when(k == nk - 1)
            def _():
                finish(acc[...])

    res = pl.pallas_call(
        body, name=name, grid=grid,
        in_specs=[a_spec, b_spec] + [sp for _, sp in extra] + [_ANY] * hi,
        out_specs=[o_spec] + [_ANY] * ho, out_shape=[out_shape] + hook.outs,
        scratch_shapes=[pltpu.VMEM(acc_shape if nk > 1 else (8, 128), F32)] + hook.sems,
        compiler_params=_params(("arbitrary", "arbitrary", "arbitrary")),
    )(a, b, *[arr for arr, _ in extra], *hook.ins)
    return res[0] if hook is _NO_HOOK else (res[0], res[1:])


NN, NT, TN = (((1,), (0,)), ((), ())), (((1,), (1,)), ((), ())), (((0,), (0,)), ((), ()))


FULL_K = 2048


def _mm(a, b, mode, *, name, out_dtype=F32, a_win=None, add=None, bias=None, hook=None, b_dev=False,
        out_dest=False):
    b_parts = 0
    if mode == "tn":
        kdim, n = a.shape[0], b.shape[1]
        m = a.shape[1] if a_win is None else a_win[1]
        tm, tn, tk = _tile(m, 512, 128), _tile(n, 640, 128), kdim
        off = 0 if a_win is None else a_win[0] // tm
        a_spec = pl.BlockSpec((tk, tm), lambda i, j, k: (k, i + off))
        b_spec = pl.BlockSpec((tk, tn), lambda i, j, k: (k, j))
        dims = TN
    else:
        m = a.shape[0]
        kdim = a.shape[1] if a_win is None else a_win[1]
        if b_dev:
            n = b.shape[0] * b.shape[2] if mode == "nn" else b.shape[1]
        else:
            n = b.shape[1] if mode == "nn" else b.shape[0]
        if FULL_K < kdim <= 2 * FULL_K and not b_dev:
            tm, tn, tk = _tile(m, 1408, 8), _tile(n, 1024, 128), _tile(kdim, 1024, 128)
        else:
            tm = _tile(m, 1408 if kdim <= FULL_K else 352, 8)
            tn, tk = _tile(n, 640, 128), kdim
        off = 0 if a_win is None else a_win[0] // tk
        a_spec = pl.BlockSpec((tm, tk), lambda i, j, k: (i, k + off))
        if mode == "nn":
            dims = NN
            if b_dev:
                per = PACK_COLS // tn
                b_spec = pl.BlockSpec((None, tk, tn), lambda i, j, k: (j // per, k, j % per))
            else:
                b_spec = pl.BlockSpec((tk, tn), lambda i, j, k: (k, j))
        else:
            dims = NT
            if b_dev:
                b_parts = kdim // PACK_COLS
                b_spec = pl.BlockSpec((b_parts, tn, PACK_COLS), lambda i, j, k: (0, j, 0))
            else:
                b_spec = pl.BlockSpec((tn, tk), lambda i, j, k: (j, k))
    if a_win is not None:
        assert a_win[0] % (tm if mode == "tn" else tk) == 0
    extra = []
    if add is not None:
        extra.append((add, pl.BlockSpec((tm, tn), lambda i, j, k: (i, j))))
    if bias is not None:
        extra.append((bias, pl.BlockSpec((1, tn), lambda i, j, k: (0, j))))
    if out_dest:
        per = PACK_COLS // tn
        o_spec = pl.BlockSpec((None, tm, tn), lambda i, j, k: (j // per, i, j % per))
        out_shape = jax.ShapeDtypeStruct((n // PACK_COLS, m, PACK_COLS), out_dtype)
    else:
        o_spec = pl.BlockSpec((tm, tn), lambda i, j, k: (i, j))
        out_shape = jax.ShapeDtypeStruct((m, n), out_dtype)
    return _mm_core(a, b, dims=dims, grid=(m // tm, n // tn, kdim // tk), a_spec=a_spec, b_spec=b_spec,
                    o_spec=o_spec, out_shape=out_shape, acc_shape=(tm, tn), name=name, extra=extra, hook=hook,
                    b_parts=b_parts)


def _win(arr, col0=0, width=None, roff=0):
    return (arr, col0, arr.shape[1] if width is None else width, roff)


def _rows(fn, rows, consts, outs, accs, *, name, nrow, tr=CHUNK):
    nr, nc, no = len(rows), len(consts), len(outs)

    def body(*refs):
        i = pl.program_id(0)
        ins = [r[...] for r in refs[:nr + nc]]
        o_refs = refs[nr + nc:nr + nc + no]
        a_refs = refs[nr + nc + no:]
        res_o, res_a = fn(i, *ins)
        for r, v in zip(o_refs, res_o):
            r[...] = v.astype(r.dtype)
        if a_refs:
            @pl.when(i == 0)
            def _():
                for r in a_refs:
                    r[...] = jnp.zeros_like(r)

            for r, v in zip(a_refs, res_a):
                r[...] += v

    in_specs = []
    for (arr, col0, width, roff) in rows:
        assert col0 % width == 0 and arr.shape[0] % tr == 0
        in_specs.append(pl.BlockSpec((tr, width), lambda i, c=col0 // width, ro=roff: (jnp.maximum(i - ro, 0), c)))
    for c in consts:
        in_specs.append(pl.BlockSpec(c.shape, lambda i, nd=c.ndim: (0,) * nd))
    outs = [tuple(o) + (0,) * (3 - len(o)) for o in outs]
    out_specs = [pl.BlockSpec((tr, w), lambda i, ro=ro: (jnp.maximum(i - ro, 0), 0)) for (w, _, ro) in outs]
    out_specs += [pl.BlockSpec(s, lambda i, nd=len(s): (0,) * nd) for s in accs]
    out_shape = [jax.ShapeDtypeStruct((nrow - ro * tr, w), dt) for (w, dt, ro) in outs]
    out_shape += [jax.ShapeDtypeStruct(s, F32) for s in accs]
    res = pl.pallas_call(
        body, name=name, grid=(nrow // tr,), in_specs=in_specs, out_specs=out_specs, out_shape=out_shape,
        compiler_params=_params(("arbitrary",)),
    )(*[r[0] for r in rows], *consts)
    return res[:no], res[no:]


HEADS_PER_STEP = 2


def _scan_specs(xs, cs, ws, ks, chunk_of, hpb):
    specs = []
    for (arr, width, colfn) in xs:
        specs.append(pl.BlockSpec((CHUNK, width * hpb), lambda h, n, f=colfn: (chunk_of(n), f(h * hpb) // hpb)))
    for (arr, width, colfn) in cs:
        specs.append(pl.BlockSpec((CHUNK, width), lambda h, n, f=colfn: (chunk_of(n), f(h))))
    for arr in list(ws) + list(ks):
        specs.append(pl.BlockSpec((hpb, 1, arr.shape[2]), lambda h, n: (h, 0, 0)))
    return specs


def _scan_fwd(fn, xs, cs, ws, ks, *, heads, nchunk, s_shape, out_w, name, pre=None, hook=None,
              hpb=HEADS_PER_STEP):
    nx, ncs, nw = len(xs), len(cs), len(ws)
    hook = _NO_HOOK if hook is None else hook
    hi, ho = len(hook.ins), len(hook.outs)
    hblocks = heads // hpb

    def body(*refs):
        n = pl.program_id(1)
        nin = nx + ncs + nw + len(ks)
        y_ref, sp_ref = refs[nin + hi], refs[nin + hi + 1]
        s_scr = refs[nin + hi + 2 + ho]
        _run_hook(hook, pl.program_id(0) * nchunk + n, hblocks * nchunk, refs[nin:nin + hi],
                  refs[nin + hi + 2:nin + hi + 2 + ho], refs[nin + hi + 3 + ho:])

        @pl.when(n == 0)
        def _():
            s_scr[...] = jnp.zeros_like(s_scr)

        cv = [r[...] for r in refs[nx:nx + ncs]]
        for e in range(hpb):
            state = s_scr[e]
            sp_ref[e, 0] = state
            xv = [r[:, e * w:(e + 1) * w] for r, (_, w, _) in zip(refs[:nx], xs)]
            wv = [r[e] for r in refs[nx + ncs:nx + ncs + nw]]
            kv = [r[e] for r in refs[nx + ncs + nw:nin]]
            if pre is not None:
                xv = pre(xv, cv)
            y, s_new = fn(n, xv, state, cv, wv, kv)
            y_ref[:, e * out_w:(e + 1) * out_w] = y.astype(y_ref.dtype)
            s_scr[e] = s_new

    lp = nchunk * CHUNK
    res = pl.pallas_call(
        body, name=name, grid=(hblocks, nchunk),
        in_specs=_scan_specs(xs, cs, ws, ks, lambda n: n, hpb) + [_ANY] * hi,
        out_specs=[pl.BlockSpec((CHUNK, out_w * hpb), lambda h, n: (n, h)),
                   pl.BlockSpec((hpb, 1) + s_shape, lambda h, n: (h, n, 0, 0))] + [_ANY] * ho,
        out_shape=[jax.ShapeDtypeStruct((lp, heads * out_w), BF16),
                   jax.ShapeDtypeStruct((heads, nchunk) + s_shape, F32)] + hook.outs,
        scratch_shapes=[pltpu.VMEM((hpb,) + s_shape, F32)] + hook.sems,
        compiler_params=_params(("arbitrary", "arbitrary")),
    )(*[t[0] for t in xs], *[t[0] for t in cs], *ws, *ks, *hook.ins)
    return (res[0], res[1]) if hook is _NO_HOOK else (res[0], res[1], res[2:])


def _scan_bwd(fn, xs, cs, ws, ks, dy, sprev, *, heads, nchunk, s_shape, out_w, name, pre=None, post=None,
              hook=None, hpb=HEADS_PER_STEP):
    nx, ncs, nw = len(xs), len(cs), len(ws)
    nin = nx + ncs + nw + len(ks)
    hook = _NO_HOOK if hook is None else hook
    hi, ho = len(hook.ins), len(hook.outs)
    hblocks = heads // hpb

    def body(*refs):
        step = pl.program_id(1)
        n = nchunk - 1 - step
        dy_ref, sp_ref = refs[nin], refs[nin + 1]
        o0 = nin + 2 + hi
        dx_refs = refs[o0:o0 + nx]
        dw_refs = refs[o0 + nx:o0 + nx + nw]
        ds_scr = refs[o0 + nx + nw + ho]
        _run_hook(hook, pl.program_id(0) * nchunk + step, hblocks * nchunk, refs[nin + 2:o0],
                  refs[o0 + nx + nw:o0 + nx + nw + ho], refs[o0 + nx + nw + ho + 1:])

        @pl.when(step == 0)
        def _():
            ds_scr[...] = jnp.zeros_like(ds_scr)
            for r in dw_refs:
                r[...] = jnp.zeros_like(r)

        cv = [r[...] for r in refs[nx:nx + ncs]]
        for e in range(hpb):
            xv = [r[:, e * w:(e + 1) * w] for r, (_, w, _) in zip(refs[:nx], xs)]
            wv = [r[e] for r in refs[nx + ncs:nx + ncs + nw]]
            kv = [r[e] for r in refs[nx + ncs + nw:nin]]
            if pre is not None:
                xv = pre(xv, cv)
            _, vjp = jax.vjp(lambda xs_, s_, ws_, kv=kv: fn(n, xs_, s_, cv, ws_, kv), xv, sp_ref[e, 0], wv)
            dxs, ds_prev, dws = vjp((dy_ref[:, e * out_w:(e + 1) * out_w].astype(F32), ds_scr[e]))
            if post is not None:
                dxs = post(dxs, cv)
            for r, v, (_, w, _) in zip(dx_refs, dxs, xs):
                r[:, e * w:(e + 1) * w] = v.astype(r.dtype)
            for r, v in zip(dw_refs, dws):
                r[e] += v
            ds_scr[e] = ds_prev

    lp = nchunk * CHUNK
    rev = lambda n: nchunk - 1 - n
    in_specs = _scan_specs(xs, cs, ws, ks, rev, hpb)
    in_specs.append(pl.BlockSpec((CHUNK, out_w * hpb), lambda h, n: (rev(n), h)))
    in_specs.append(pl.BlockSpec((hpb, 1) + s_shape, lambda h, n: (h, rev(n), 0, 0)))
    out_specs = [pl.BlockSpec((CHUNK, w * hpb), lambda h, n: (rev(n), h)) for (_, w, _) in xs]
    out_specs += [pl.BlockSpec((hpb, 1, w.shape[2]), lambda h, n: (h, 0, 0)) for w in ws]
    out_shape = [jax.ShapeDtypeStruct((lp, heads * w), BF16) for (_, w, _) in xs]
    out_shape += [jax.ShapeDtypeStruct(w.shape, F32) for w in ws]
    res = pl.pallas_call(
        body, name=name, grid=(hblocks, nchunk), in_specs=in_specs + [_ANY] * hi,
        out_specs=out_specs + [_ANY] * ho, out_shape=out_shape + hook.outs,
        scratch_shapes=[pltpu.VMEM((hpb,) + s_shape, F32)] + hook.sems,
        compiler_params=_params(("arbitrary", "arbitrary")),
    )(*[t[0] for t in xs], *[t[0] for t in cs], *ws, *ks, dy, sprev, *hook.ins)
    if hook is _NO_HOOK:
        return res[:nx], res[nx:]
    return res[:nx], res[nx:nx + nw], res[nx + nw:]


def _iota2(shape, dim):
    return lax.broadcasted_iota(jnp.int32, shape, dim)


def _ret_chunk(n, xs, state, cs, ws, ks):
    q, k, v, z = xs
    (w,), (lg,) = ws, ks
    lgc = lg[:, :1]
    row, col = _iota2((CHUNK, CHUNK), 0), _iota2((CHUNK, CHUNK), 1)
    diff = jnp.maximum(row - col, 0).astype(F32)
    decay = jnp.where(row >= col, jnp.exp(lg * diff), 0.0)
    scores = _bdot(q, k, 1, 1) * decay
    o_intra = _bdot(scores, v, 1, 0)
    idx = _iota2((CHUNK, 1), 0).astype(F32)
    k_w = k * jnp.exp(lgc * (CHUNK - 1.0 - idx))
    kv = _bdot(k_w, v, 0, 0)
    s_new = state * jnp.exp(lgc * float(CHUNK)) + kv
    q_w = q * jnp.exp(lgc * (idx + 1.0))
    o = o_intra + _bdot(q_w, state, 1, 0)
    return _rms(o, w) * _silu(z), s_new


def _rope(t, cos2, sin2):
    return t * cos2 + pltpu.roll(t, RET_DK // 2, 1) * sin2


def _rope_t(g, cos2, sin2):
    return g * cos2 - pltpu.roll(g, RET_DK // 2, 1) * sin2


def _ret_pre(xv, cv):
    q, k, v, z = xv
    cos2, sin2 = cv
    return [_rope(q, cos2, sin2), _rope(k, cos2, sin2) * (RET_DK ** -0.5), v, z]


def _ret_post(dxs, cv):
    dq, dk, dv, dz = dxs
    cos2, sin2 = cv
    return [_rope_t(dq, cos2, sin2), _rope_t(dk, cos2, sin2) * (RET_DK ** -0.5), dv, dz]


def _tri_apply(x, lower):
    n = x.shape[0]
    row, col = _iota2((n, n), 0), _iota2((n, n), 1)
    tri = (row >= col if lower else row <= col).astype(BF16)
    hi = x.astype(BF16)
    rest = x - hi.astype(F32)
    mid = rest.astype(BF16)
    lo = (rest - mid.astype(F32)).astype(BF16)
    return sum(lax.dot_general(tri, p, NN, preferred_element_type=F32) for p in (hi, mid, lo))


@jax.custom_vjp
def _cumsum_rows(x):
    n = x.shape[0]
    tri = (_iota2((n, n), 0) >= _iota2((n, n), 1)).astype(F32)
    return jnp.dot(tri, x, precision=lax.Precision.HIGHEST, preferred_element_type=F32)


_cumsum_rows.defvjp(lambda x: (_cumsum_rows(x), None), lambda _, g: (_tri_apply(g, False),))


def _gla_chunk(n, xs, state_t, cs, ws, ks):
    q, k, v, z, pre = xs
    (w,) = ws
    q = q * (GLA_DK ** -0.5)
    rowc = _iota2((CHUNK, 1), 0)
    valid = jnp.logical_or(n > 0, rowc >= PAD)
    log_a = jnp.where(valid, _log_sigmoid(pre) / GLA_TAU, 0.0)
    b = _cumsum_rows(log_a)
    b_last = b[CHUNK - 1:CHUNK, :]
    kv_t = _bdot(v, k * jnp.exp(b_last - b), 0, 0)
    s_new = state_t * jnp.exp(b_last) + kv_t
    o_inter = _bdot(q * jnp.exp(b), state_t, 1, 1)
    outs = []
    for s in range(CHUNK // SUB):
        lo, hi = s * SUB, (s + 1) * SUB
        b_ref = jnp.zeros_like(b_last) if s == 0 else b[lo - 1:lo, :]
        q_hat = q[lo:hi] * jnp.exp(b[lo:hi] - b_ref)
        k_hat = k[:hi] * jnp.exp(b_ref - b[:hi])
        sc = _bdot(q_hat, k_hat, 1, 1)
        causal = _iota2((SUB, hi), 0) + lo >= _iota2((SUB, hi), 1)
        outs.append(_bdot(jnp.where(causal, sc, 0.0), v[:hi], 1, 0))
    o = jnp.concatenate(outs, axis=0) + o_inter
    return _rms(o, w) * _silu(z), s_new


def _s5_disc(lam_re, lam_im, log_dt, b_re, b_im, expand):
    dt = jnp.exp(log_dt)
    mag = jnp.exp(lam_re * dt)
    ab_re, ab_im = mag * jnp.cos(lam_im * dt), mag * jnp.sin(lam_im * dt)
    den = lam_re * lam_re + lam_im * lam_im
    nr, ni = ab_re - 1.0, ab_im
    f_re = (nr * lam_re + ni * lam_im) / den
    f_im = (ni * lam_re - nr * lam_im) / den
    hp = lax.Precision.HIGHEST
    f_re = jnp.dot(f_re, expand, precision=hp, preferred_element_type=F32)
    f_im = jnp.dot(f_im, expand, precision=hp, preferred_element_type=F32)
    return ab_re, ab_im, f_re * b_re - f_im * b_im, f_re * b_im + f_im * b_re


def _s5_disc_fwd(args):
    def body(*refs):
        outs = _s5_disc(*[r[...] for r in refs[:6]])
        for r, v in zip(refs[6:], outs):
            r[...] = v

    g, p = args[0].shape
    return pl.pallas_call(
        body, name="s5_disc_fwd",
        out_shape=[jax.ShapeDtypeStruct((g, p), F32)] * 2 + [jax.ShapeDtypeStruct(args[3].shape, F32)] * 2,
    )(*args)


def _s5_disc_bwd(args, cts):
    def body(*refs):
        prim = [r[...] for r in refs[:5]]
        expand = refs[5][...]
        ct = tuple(r[...] for r in refs[6:10])
        _, vjp = jax.vjp(lambda *a: _s5_disc(*a, expand), *prim)
        for r, v in zip(refs[10:], vjp(ct)):
            r[...] = v

    return pl.pallas_call(
        body, name="s5_disc_bwd", out_shape=[jax.ShapeDtypeStruct(a.shape, F32) for a in args[:5]],
    )(*args, *cts)


SCAN_ROWS, SCAN_LANES = 32, 128
TILE_G = 8
TILE_W = TILE_G * S5_P
S5_TB = 64


def _s5_scan_fwd(bu, a_re, a_im):
    lp = bu.shape[0]

    def body(bu_ref, ar_ref, ai_ref, x_ref, st):
        @pl.when(pl.program_id(0) == 0)
        def _():
            st[...] = jnp.zeros_like(st)

        ar, ai = ar_ref[...], ai_ref[...]

        def step(t, carry):
            xr, xi = carry
            nr = ar * xr - ai * xi + bu_ref[t, 0:SCAN_ROWS, :]
            ni = ar * xi + ai * xr + bu_ref[t, SCAN_ROWS:2 * SCAN_ROWS, :]
            x_ref[t, 0:SCAN_ROWS, :] = nr
            x_ref[t, SCAN_ROWS:2 * SCAN_ROWS, :] = ni
            return nr, ni

        xr, xi = lax.fori_loop(0, S5_TB, step, (st[0], st[1]))
        st[0] = xr
        st[1] = xi

    blk = pl.BlockSpec((S5_TB, 2 * SCAN_ROWS, SCAN_LANES), lambda i: (i, 0, 0))
    cst = pl.BlockSpec((SCAN_ROWS, SCAN_LANES), lambda i: (0, 0))
    return pl.pallas_call(
        body, name="s5_scan_fwd", grid=(lp // S5_TB,), in_specs=[blk, cst, cst], out_specs=blk,
        out_shape=jax.ShapeDtypeStruct(bu.shape, F32),
        scratch_shapes=[pltpu.VMEM((2, SCAN_ROWS, SCAN_LANES), F32)],
        compiler_params=_params(("arbitrary",)),
    )(bu, a_re, a_im)


def _s5_expand(a, w_t, *, a_blk, dims, name, hook=None):
    lp, nt = a.shape[0], w_t.shape[0]
    tm = _tile(lp, 176, 8)
    steps = lp // tm
    rows3 = 2 * SCAN_ROWS
    per = TILE_W // SCAN_LANES
    hook = _NO_HOOK if hook is None else hook
    hi, ho = len(hook.ins), len(hook.outs)

    def body(*refs):
        a_ref, w_ref, o_ref = refs[0], refs[1], refs[2 + hi]
        _run_hook(hook, pl.program_id(0), steps, refs[2:2 + hi], refs[3 + hi:3 + hi + ho], refs[3 + hi + ho:])
        for j in range(nt):
            s = j % TILE_G
            r = lax.dot_general(a_ref[:, 128 * s:128 * (s + 1)].astype(BF16), w_ref[j], dims,
                                preferred_element_type=F32)
            for c in range(per):
                o_ref[pl.ds(per * j + c, tm, stride=rows3), :] = r[:, SCAN_LANES * c:SCAN_LANES * (c + 1)]

    res = pl.pallas_call(
        body, name=name, grid=(steps,),
        in_specs=[pl.BlockSpec((tm, S5_W), lambda i: (i, a_blk)), pl.BlockSpec(w_t.shape, lambda i: (0, 0, 0))]
        + [_ANY] * hi,
        out_specs=[pl.BlockSpec((tm * rows3, SCAN_LANES), lambda i: (i, 0))] + [_ANY] * ho,
        out_shape=[jax.ShapeDtypeStruct((lp * rows3, SCAN_LANES), F32)] + hook.outs,
        scratch_shapes=hook.sems, compiler_params=_params(("arbitrary",)),
    )(a, w_t, *hook.ins)
    out3 = res[0].reshape(lp, rows3, SCAN_LANES)
    return out3 if hook is _NO_HOOK else (out3, res[1:])


def _s5_scan_bwd(gx, x, a_re, a_im):
    lp = gx.shape[0]
    nb = lp // S5_TB

    def body(gx_ref, x_ref, xp_ref, ar_ref, ai_ref, g_ref, da_ref, st):
        i = pl.program_id(0)

        @pl.when(i == 0)
        def _():
            st[...] = jnp.zeros_like(st)
            da_ref[...] = jnp.zeros_like(da_ref)

        ar, ai = ar_ref[...], ai_ref[...]
        first = (i == nb - 1).astype(F32)

        def step(s, carry):
            gr, gi, dar, dai = carry
            t = S5_TB - 1 - s
            ngr = gx_ref[t, 0:SCAN_ROWS, :] + ar * gr + ai * gi
            ngi = gx_ref[t, SCAN_ROWS:2 * SCAN_ROWS, :] + ar * gi - ai * gr
            g_ref[t, 0:SCAN_ROWS, :] = ngr
            g_ref[t, SCAN_ROWS:2 * SCAN_ROWS, :] = ngi
            tp = jnp.maximum(t - 1, 0)
            at0 = (t == 0).astype(F32)
            keep = 1.0 - at0
            pr = keep * x_ref[tp, 0:SCAN_ROWS, :] + at0 * (1.0 - first) * xp_ref[0, 0:SCAN_ROWS, :]
            pi = keep * x_ref[tp, SCAN_ROWS:2 * SCAN_ROWS, :] + at0 * (1.0 - first) * xp_ref[0, SCAN_ROWS:2 * SCAN_ROWS, :]
            return ngr, ngi, dar + ngr * pr + ngi * pi, dai + ngi * pr - ngr * pi

        zero = jnp.zeros((SCAN_ROWS, SCAN_LANES), F32)
        gr, gi, dar, dai = lax.fori_loop(0, S5_TB, step, (st[0], st[1], zero, zero))
        st[0] = gr
        st[1] = gi
        da_ref[0] += dar
        da_ref[1] += dai

    rev = lambda i: nb - 1 - i
    blk = pl.BlockSpec((S5_TB, 2 * SCAN_ROWS, SCAN_LANES), lambda i: (rev(i), 0, 0))
    prev = pl.BlockSpec((1, 2 * SCAN_ROWS, SCAN_LANES), lambda i: (jnp.maximum(rev(i) * S5_TB - 1, 0), 0, 0))
    cst = pl.BlockSpec((SCAN_ROWS, SCAN_LANES), lambda i: (0, 0))
    return pl.pallas_call(
        body, name="s5_scan_bwd", grid=(nb,), in_specs=[blk, blk, prev, cst, cst],
        out_specs=[blk, pl.BlockSpec((2, SCAN_ROWS, SCAN_LANES), lambda i: (0, 0, 0))],
        out_shape=[jax.ShapeDtypeStruct(gx.shape, F32), jax.ShapeDtypeStruct((2, SCAN_ROWS, SCAN_LANES), F32)],
        scratch_shapes=[pltpu.VMEM((2, SCAN_ROWS, SCAN_LANES), F32)],
        compiler_params=_params(("arbitrary",)),
    )(gx, x, x, a_re, a_im)


def _place():
    x, y, c = lax.axis_index("x"), lax.axis_index("y"), lax.axis_index("c")
    return x, y, c, [(1 - x, y), (x, 1 - y), (1 - x, 1 - y)]


def _gather_phases(nrows):
    half = (nrows // 32) * 16
    assert 0 < half < nrows

    def plan(x_ref, out_ref, send_sems, recv_sems, local_sem):
        x, y, c, _ = _place()
        me, sibling = (x, y, c), (x, y, 1 - c)
        at_x, at_y, at_d = (1 - x, y, c), (x, 1 - y, c), (1 - x, 1 - y, c)

        def rows(block, part=None):
            idx = 4 * block[0] + 2 * block[1] + block[2]
            if part is None:
                return out_ref.at[idx]
            return out_ref.at[idx, pl.ds(0, half)] if part == 0 else out_ref.at[idx, pl.ds(half, nrows - half)]

        def copy(k, block, to, src=None, part=None):
            return pltpu.make_async_remote_copy(
                src_ref=rows(block, part) if src is None else src, dst_ref=rows(block, part),
                send_sem=send_sems.at[k], recv_sem=recv_sems.at[k], device_id=to, device_id_type=MESH)

        def other_core(block):
            return (block[0], block[1], 1 - c)

        mine = pltpu.make_async_copy(x_ref, rows(me), local_sem)
        direct = [copy(0, me, sibling, src=x_ref), copy(1, me, at_x, src=x_ref), copy(2, me, at_y, src=x_ref)]
        relays = [copy(3, at_x, at_y, part=0), copy(7, at_y, at_x, part=1)]
        passed = [copy(4, at_x, sibling), copy(5, at_y, sibling), copy(6, at_d, sibling)]
        landed = [copy(1, at_x, me), copy(2, at_y, me), copy(3, at_d, me, part=0), copy(7, at_d, me, part=1)]
        from_sibling = [copy(0, sibling, me)] + [copy(4 + j, other_core(b), me) for j, b in enumerate((at_x, at_y, at_d))]
        return mine, direct, relays, passed, landed, from_sibling

    def start(ins, outs, sems):
        mine, direct, _, _, _, _ = plan(ins[0], outs[0], *sems)
        mine.start()
        for cp in direct:
            cp.start()

    def middle(ins, outs, sems):
        _, _, relays, passed, landed, _ = plan(ins[0], outs[0], *sems)
        for j in range(2):
            landed[j].wait_recv()
            passed[j].start()
            relays[j].start()

    def late(ins, outs, sems):
        _, _, _, passed, landed, _ = plan(ins[0], outs[0], *sems)
        landed[2].wait_recv()
        landed[3].wait_recv()
        passed[2].start()

    def finish(ins, outs, sems):
        mine, direct, relays, passed, _, from_sibling = plan(ins[0], outs[0], *sems)
        for cp in from_sibling:
            cp.wait_recv()
        for cp in direct + relays + passed:
            cp.wait_send()
        mine.wait()

    return start, middle, late, finish


_GATHER_SEMS = [pltpu.SemaphoreType.DMA((8,)), pltpu.SemaphoreType.DMA((8,)), pltpu.SemaphoreType.DMA]


def _all_gather(shard, name):
    phases = _gather_phases(shard.shape[0])

    def body(x_ref, out_ref, *sems):
        for phase in phases:
            phase([x_ref], [out_ref], sems)

    return pl.pallas_call(
        body, name=name, out_shape=jax.ShapeDtypeStruct((N_DEV,) + shard.shape, shard.dtype),
        in_specs=[_ANY], out_specs=_ANY, scratch_shapes=list(_GATHER_SEMS),
    )(shard)


def _gather_hook(shard):
    start, middle, late, finish = _gather_phases(shard.shape[0])
    return _Hook([shard], [jax.ShapeDtypeStruct((N_DEV,) + shard.shape, shard.dtype)], _GATHER_SEMS,
                 [(0.0, start), (0.5, middle), (0.85, late), (1.0, finish)])


def _swap_with_sibling(parts, name):
    def body(p_ref, out_ref, send_sems, recv_sems):
        x, y, c, _ = _place()
        copies = [pltpu.make_async_remote_copy(
            src_ref=p_ref.at[2 * chip + (1 - c)], dst_ref=out_ref.at[chip],
            send_sem=send_sems.at[chip], recv_sem=recv_sems.at[chip],
            device_id=(x, y, 1 - c), device_id_type=MESH) for chip in range(4)]
        for cp in copies:
            cp.start()
        for cp in copies:
            cp.wait()

    return pl.pallas_call(
        body, name=name, out_shape=jax.ShapeDtypeStruct((4,) + parts.shape[1:], parts.dtype),
        in_specs=[pl.BlockSpec(memory_space=pl.ANY)], out_specs=pl.BlockSpec(memory_space=pl.ANY),
        scratch_shapes=[pltpu.SemaphoreType.DMA((4,)), pltpu.SemaphoreType.DMA((4,))],
    )(parts)


def _chips_phases(lo, rows):
    def copies(p_ref, out_ref, send_sems, recv_sems):
        x, y, c, chips = _place()
        return [pltpu.make_async_remote_copy(
            src_ref=p_ref.at[2 * px + py, pl.ds(lo, rows)], dst_ref=out_ref.at[j],
            send_sem=send_sems.at[j], recv_sem=recv_sems.at[j],
            device_id=(px, py, c), device_id_type=MESH) for j, (px, py) in enumerate(chips)]

    def start(ins, outs, sems):
        for cp in copies(ins[0], outs[0], *sems):
            cp.start()

    def finish(ins, outs, sems):
        for cp in copies(ins[0], outs[0], *sems):
            cp.wait()

    return start, finish


def _chips_relay_hook(parts):
    nrows = parts.shape[1]
    half = nrows // 2
    assert half % 16 == 0

    def plan(p_ref, out_ref, transit_ref, send_sems, recv_sems):
        x, y, c, _ = _place()
        at_x, at_y = (1 - x, y, c), (x, 1 - y, c)
        diag = p_ref.at[2 * (1 - x) + (1 - y)]

        def copy(k, src, dst, to):
            return pltpu.make_async_remote_copy(src_ref=src, dst_ref=dst, send_sem=send_sems.at[k],
                                                recv_sem=recv_sems.at[k], device_id=to, device_id_type=MESH)

        lo, hi = pl.ds(0, half), pl.ds(half, half)
        direct = [copy(0, p_ref.at[2 * (1 - x) + y], out_ref.at[0], at_x),
                  copy(1, p_ref.at[2 * x + (1 - y)], out_ref.at[1], at_y),
                  copy(2, diag.at[lo], transit_ref.at[0], at_x),
                  copy(3, diag.at[hi], transit_ref.at[1], at_y)]
        relays = [copy(4, transit_ref.at[0], out_ref.at[2, lo], at_y),
                  copy(5, transit_ref.at[1], out_ref.at[2, hi], at_x)]
        return direct, relays

    def start(ins, outs, sems):
        for cp in plan(ins[0], outs[0], outs[1], *sems)[0]:
            cp.start()

    def middle(ins, outs, sems):
        direct, relays = plan(ins[0], outs[0], outs[1], *sems)
        for j in range(2):
            direct[2 + j].wait_recv()
            relays[j].start()

    def finish(ins, outs, sems):
        direct, relays = plan(ins[0], outs[0], outs[1], *sems)
        for cp in direct[:2] + relays:
            cp.wait_recv()
        for cp in direct + relays:
            cp.wait_send()

    shape = parts.shape[2:]
    return _Hook([parts], [jax.ShapeDtypeStruct((3, nrows) + shape, parts.dtype),
                           jax.ShapeDtypeStruct((2, half) + shape, parts.dtype)],
                 [pltpu.SemaphoreType.DMA((6,)), pltpu.SemaphoreType.DMA((6,))],
                 [(0.0, start), (0.5, middle), (1.0, finish)])


def _chips_hook(parts, lo=0, hi=None):
    rows = (parts.shape[1] if hi is None else hi) - lo
    start, finish = _chips_phases(lo, rows)
    return _Hook([parts], [jax.ShapeDtypeStruct((3, rows) + parts.shape[2:], parts.dtype)],
                 [pltpu.SemaphoreType.DMA((3,)), pltpu.SemaphoreType.DMA((3,))], [(0.0, start), (1.0, finish)])


BIG_LAYOUT = (("w_in_ab", D_MODEL, PACK_COLS), ("s5_w_glu", S5_W // N_DEV, PACK_COLS),
              ("w_out_ab", OUT_AB // N_DEV, 2 * PACK_COLS), ("w_in_c", D_MODEL, PACK_COLS),
              ("w_out_c", GLA_W // N_DEV, 2 * PACK_COLS))


def _to_rows(a):
    if a.shape[-1] == PACK_COLS:
        return a
    assert a.shape[-1] == 2 * PACK_COLS
    return jnp.concatenate([a[..., :PACK_COLS], a[..., PACK_COLS:]], axis=-2)


def _from_rows(p, cols):
    if cols == PACK_COLS:
        return p
    r = p.shape[-2] // 2
    return jnp.concatenate([p[..., :r, :], p[..., r:, :]], axis=-1)


FIRST_LAYOUT = BIG_LAYOUT[:1]
OTHER_LAYOUT = BIG_LAYOUT[1:3] + BIG_LAYOUT[4:]
GLU_AB_LAYOUT = BIG_LAYOUT[1:3]
IN_C_LAYOUT = BIG_LAYOUT[3:4]


def _pack_big(pieces, layout):
    return jnp.concatenate([_to_rows(pieces[name]) for name, _, _ in layout], axis=-2)


def _unpack_big(buf, layout):
    out, o = {}, 0
    for name, rows, cols in layout:
        r = rows * cols // PACK_COLS
        out[name] = _from_rows(buf[..., o:o + r, :], cols)
        o += r
    return out


def _column_windows(g):
    rows, quarter = g.shape[0], WIN_COLS // 4

    def body(g_ref, o_ref):
        o_ref[...] = g_ref[...]

    return pl.pallas_call(
        body, name="w_in_c_grad_windows", grid=(N_DEV, WIN_COLS // quarter),
        in_specs=[pl.BlockSpec((rows, quarter), lambda d, c: (0, (WIN_STEP // quarter) * d + c))],
        out_specs=pl.BlockSpec((None, rows, quarter), lambda d, c: (d, 0, c)),
        out_shape=jax.ShapeDtypeStruct((N_DEV, rows, WIN_COLS), g.dtype),
        compiler_params=_params(("arbitrary", "arbitrary")),
    )(g)


def _rows1024(a):
    r, c = a.shape
    if c > PACK_COLS:
        a = jnp.concatenate([a[:, i * PACK_COLS:(i + 1) * PACK_COLS] for i in range(c // PACK_COLS)], axis=0)
    elif c < PACK_COLS:
        a = jnp.pad(a, ((0, 0), (0, PACK_COLS - c)))
    return jnp.pad(a, ((0, -a.shape[0] % 8), (0, 0)))


def _unrows1024(p, r, c):
    if c > PACK_COLS:
        return jnp.concatenate([p[i * r:(i + 1) * r] for i in range(c // PACK_COLS)], axis=1)
    return p[:r, :c]


def _lane_select(a, off, sign, n_out, out_dtype, exact, name):
    rows, n_in = a.shape
    tr = _tile(rows, 256, 16)

    def body(off_ref, a_ref, o_ref):
        sel = _iota2((n_in, n_out), 0) + off_ref[0] * sign == _iota2((n_in, n_out), 1)
        if exact:
            r = jnp.dot(a_ref[...], sel.astype(F32), precision=lax.Precision.HIGHEST, preferred_element_type=F32)
        else:
            r = _dg(a_ref[...], sel.astype(BF16), 1, 0)
        o_ref[...] = r.astype(out_dtype)

    return pl.pallas_call(
        body, name=name, grid=(rows // tr,),
        in_specs=[pl.BlockSpec(memory_space=pltpu.SMEM), pl.BlockSpec((tr, n_in), lambda i: (i, 0))],
        out_specs=pl.BlockSpec((tr, n_out), lambda i: (i, 0)),
        out_shape=jax.ShapeDtypeStruct((rows, n_out), out_dtype),
        compiler_params=_params(("arbitrary",)),
    )(off, a)


def _adamw(w, g, m, v, name):
    rows, cols = w.shape
    tr = _tile(rows, 256, 8) if rows % 8 == 0 else rows

    def fn(i, w_, g_, m_, v_):
        m_new = ADAM_B1 * m_ + (1.0 - ADAM_B1) * g_
        v_new = ADAM_B2 * v_ + (1.0 - ADAM_B2) * (g_ * g_)
        m_hat = m_new / (1.0 - ADAM_B1 ** ADAM_STEP)
        v_hat = v_new / (1.0 - ADAM_B2 ** ADAM_STEP)
        delta = -ADAM_LR * (m_hat / (jnp.sqrt(v_hat) + ADAM_EPS) + ADAM_WD * w_)
        return (delta, m_new, v_new), ()

    outs, _ = _rows(fn, [_win(w), _win(g), _win(m), _win(v)], [], [(cols, F32)] * 3, [], name=name,
                    nrow=rows, tr=tr)
    return outs


def _as2d(a):
    if a.ndim == 1:
        return a.reshape(1, -1)
    if a.ndim == 2:
        return a
    a = a.reshape(a.shape[1:])
    return a if a.ndim == 2 else a.reshape(a.shape[0], -1)


def kernel(x, meta, norm_ab_w, w_in_ab, ret_norm_w, s5_lam_re, s5_lam_im, s5_log_dt, s5_b_re, s5_b_im, s5_c_re, s5_c_im, s5_d, s5_w_glu, w_out_ab, norm_c_w, w_in_c, gla_w_gate, gla_b_gate, gla_norm_w, w_out_c, final_norm_w, loss_target, m_meta, m_norm_ab_w, m_w_in_ab, m_ret_norm_w, m_s5_lam_re, m_s5_lam_im, m_s5_log_dt, m_s5_b_re, m_s5_b_im, m_s5_c_re, m_s5_c_im, m_s5_d, m_s5_w_glu, m_w_out_ab, m_norm_c_w, m_w_in_c, m_gla_w_gate, m_gla_b_gate, m_gla_norm_w, m_w_out_c, m_final_norm_w, v_meta, v_norm_ab_w, v_w_in_ab, v_ret_norm_w, v_s5_lam_re, v_s5_lam_im, v_s5_log_dt, v_s5_b_re, v_s5_b_im, v_s5_c_re, v_s5_c_im, v_s5_d, v_s5_w_glu, v_w_out_ab, v_norm_c_w, v_w_in_c, v_gla_w_gate, v_gla_b_gate, v_gla_norm_w, v_w_out_c, v_final_norm_w):
    weights = dict(meta=meta, norm_ab_w=norm_ab_w, w_in_ab=w_in_ab, ret_norm_w=ret_norm_w, s5_lam_re=s5_lam_re,
                   s5_lam_im=s5_lam_im, s5_log_dt=s5_log_dt, s5_b_re=s5_b_re, s5_b_im=s5_b_im, s5_c_re=s5_c_re,
                   s5_c_im=s5_c_im, s5_d=s5_d, s5_w_glu=s5_w_glu, w_out_ab=w_out_ab, norm_c_w=norm_c_w,
                   w_in_c=w_in_c, gla_w_gate=gla_w_gate, gla_b_gate=gla_b_gate, gla_norm_w=gla_norm_w,
                   w_out_c=w_out_c, final_norm_w=final_norm_w)
    mom_m = dict(meta=m_meta, norm_ab_w=m_norm_ab_w, w_in_ab=m_w_in_ab, ret_norm_w=m_ret_norm_w,
                 s5_lam_re=m_s5_lam_re, s5_lam_im=m_s5_lam_im, s5_log_dt=m_s5_log_dt, s5_b_re=m_s5_b_re,
                 s5_b_im=m_s5_b_im, s5_c_re=m_s5_c_re, s5_c_im=m_s5_c_im, s5_d=m_s5_d, s5_w_glu=m_s5_w_glu,
                 w_out_ab=m_w_out_ab, norm_c_w=m_norm_c_w, w_in_c=m_w_in_c, gla_w_gate=m_gla_w_gate,
                 gla_b_gate=m_gla_b_gate, gla_norm_w=m_gla_norm_w, w_out_c=m_w_out_c, final_norm_w=m_final_norm_w)
    mom_v = dict(meta=v_meta, norm_ab_w=v_norm_ab_w, w_in_ab=v_w_in_ab, ret_norm_w=v_ret_norm_w,
                 s5_lam_re=v_s5_lam_re, s5_lam_im=v_s5_lam_im, s5_log_dt=v_s5_log_dt, s5_b_re=v_s5_b_re,
                 s5_b_im=v_s5_b_im, s5_c_re=v_s5_c_re, s5_c_im=v_s5_c_im, s5_d=v_s5_d, s5_w_glu=v_s5_w_glu,
                 w_out_ab=v_w_out_ab, norm_c_w=v_norm_c_w, w_in_c=v_w_in_c, gla_w_gate=v_gla_w_gate,
                 gla_b_gate=v_gla_b_gate, gla_norm_w=v_gla_norm_w, w_out_c=v_w_out_c, final_norm_w=v_final_norm_w)
    order = list(weights)

    seq = x.shape[1]
    lp = CHUNK + seq
    nchunk = lp // CHUNK
    dev = 4 * lax.axis_index("x") + 2 * lax.axis_index("y") + lax.axis_index("c")
    core = lax.axis_index("c")
    chip = 2 * lax.axis_index("x") + lax.axis_index("y")

    win_off = jnp.reshape(2 * dev, (1,)).astype(jnp.int32)
    shard_c = jnp.pad(w_in_c[0].astype(BF16), ((0, 0), (0, 896 - SHARD_C)))
    big_shards = dict(w_in_ab=w_in_ab[0].astype(BF16), s5_w_glu=s5_w_glu[0].astype(BF16),
                      w_out_ab=w_out_ab[0].astype(BF16), w_out_c=w_out_c[0].astype(BF16),
                      w_in_c=_lane_select(shard_c, win_off, 1, WIN_COLS, BF16, False, "w_in_c_to_window"))
    def pad_to(a, rows, cols):
        return jnp.pad(a, ((0, rows - a.shape[0]), (0, cols - a.shape[1])))

    shard_w = D_MODEL // N_DEV
    small_pack = jnp.concatenate([meta, pad_to(norm_c_w, 8, shard_w), pad_to(gla_w_gate[0], GLA_RANK, shard_w),
                                  pad_to(gla_b_gate, 8, shard_w), pad_to(gla_norm_w, 8, shard_w)], axis=0)
    w_in_ab_g = _all_gather(big_shards["w_in_ab"], "gather_first")
    win_cut = 1408
    in_c_hook_a = _gather_hook(big_shards["w_in_c"][:win_cut])
    in_c_hook_b = _gather_hook(big_shards["w_in_c"][win_cut:])
    glu_ab_hook = _gather_hook(_pack_big(big_shards, GLU_AB_LAYOUT))
    out_c_hook = _gather_hook(_to_rows(big_shards["w_out_c"]))
    gs = _all_gather(small_pack, "gather_small")
    gate_w = GLA_QK // N_DEV
    s_meta, s_norm_c = gs[:, :N_META], gs[:, N_META]
    s_wgate, s_bgate, s_gnorm = gs[:, 24:24 + GLA_RANK, :gate_w], gs[:, 40, :gate_w], gs[:, 48]
    meta_f = s_meta.transpose(1, 0, 2).reshape(N_META, D_MODEL)
    norm_c_f = s_norm_c.reshape(1, D_MODEL)
    w_gate_f = jnp.pad(s_wgate.transpose(1, 0, 2).reshape(GLA_RANK, GLA_QK), ((0, GATE_PAD - GLA_RANK), (0, 0)))
    b_gate_f = s_bgate.reshape(1, GLA_QK)
    gla_norm_f = s_gnorm.reshape(GLA_H, 1, GLA_DV)

    pos = jnp.maximum(jnp.arange(lp, dtype=F32) - float(PAD), 0.0)
    inv_freq = jnp.power(ROPE_BASE, -jnp.arange(0, RET_DK, 2, dtype=F32) / RET_DK)
    ang = pos[:, None] * inv_freq[None, :]
    cos2 = jnp.concatenate([jnp.cos(ang), jnp.cos(ang)], axis=1)
    sin2 = jnp.concatenate([-jnp.sin(ang), jnp.sin(ang)], axis=1)
    log_g = jnp.log1p(-jnp.exp2(-5.0 - jnp.arange(RET_H, dtype=F32)))
    lg = jnp.broadcast_to(log_g[:, None, None], (RET_H, 1, 128))
    ret_norm_h = ret_norm_w.reshape(RET_H, 1, RET_DV)

    h0 = jnp.concatenate([jnp.zeros((PAD, D_MODEL), F32), meta_f, x[0]], axis=0)

    def rowmask(i):
        return (_iota2((CHUNK, 1), 0) + i * CHUNK) >= PAD

    (hn0,), _ = _rows(lambda i, h, w: ((_rms(h, w),), ()), [_win(h0)], [norm_ab_w], [(D_MODEL, BF16)], [],
                      name="norm_ab_fwd", nrow=lp)
    proj_ab, (w_in_c_ga,) = _mm(hn0, w_in_ab_g, "nn", name="in_ab_fwd", hook=in_c_hook_a, b_dev=True)

    q_off, k_off, v_off, za_off = 0, RET_QK, 2 * RET_QK, 2 * RET_QK + RET_W
    u_off, zb_off = 2 * RET_QK + 2 * RET_W, 2 * RET_QK + 2 * RET_W + S5_W
    ret_xs = [(proj_ab, RET_DK, lambda h: q_off // RET_DK + h), (proj_ab, RET_DK, lambda h: k_off // RET_DK + h),
              (proj_ab, RET_DV, lambda h: v_off // RET_DV + h), (proj_ab, RET_DV, lambda h: za_off // RET_DV + h)]
    ret_cs = [(cos2, RET_DK, lambda h: 0), (sin2, RET_DK, lambda h: 0)]
    ret_kw = dict(heads=RET_H, nchunk=nchunk, s_shape=(RET_DK, RET_DV), out_w=RET_DV, pre=_ret_pre, hpb=4)
    o_a, ret_sprev, (gathered_glu_ab,) = _scan_fwd(_ret_chunk, ret_xs, ret_cs, [ret_norm_h], [lg], name="ret_fwd",
                                                   hook=glu_ab_hook, **ret_kw)
    gb = _unpack_big(gathered_glu_ab, GLU_AB_LAYOUT)
    w_glu_f = gb["s5_w_glu"].reshape(S5_W, S5_W)
    w_out_ab_f = gb["w_out_ab"].reshape(OUT_AB, D_MODEL)

    expand = jnp.repeat(jnp.eye(S5_P, dtype=F32), S5_GH, axis=1)
    disc_args = (s5_lam_re[0], s5_lam_im[0], s5_log_dt[0].reshape(S5_G, 1),
                 s5_b_re[0].reshape(S5_G, S5_P * S5_GH), s5_b_im[0].reshape(S5_G, S5_P * S5_GH), expand)
    ab_re, ab_im, bb_re, bb_im = _s5_disc_fwd(disc_args)
    gt = TILE_G
    eye_t = jnp.eye(gt, dtype=F32)

    def tiles_in(bb):
        return jnp.einsum("sgph,gk->sghkp", bb.reshape(gt, gt, S5_P, S5_GH), eye_t).reshape(gt, 128, TILE_W)

    def tiles_out(cc):
        return jnp.einsum("sghp,gk->sgpkh", cc.reshape(gt, gt, S5_GH, S5_P), eye_t).reshape(gt, TILE_W, 128)

    wb_t = jnp.concatenate([tiles_in(bb_re), tiles_in(bb_im)], axis=0).astype(BF16)
    wc_t = jnp.concatenate([tiles_out(s5_c_re[0]), -tiles_out(s5_c_im[0])], axis=0).astype(BF16)
    a_re, a_im = ab_re.reshape(SCAN_ROWS, SCAN_LANES), ab_im.reshape(SCAN_ROWS, SCAN_LANES)
    tm5, tk5, nt5 = _tile(lp, 1408, 8), _tile(lp, 1408, 8), 2 * gt
    u_blk = u_off // 128
    wide_k = pl.BlockSpec((tm5, TILE_W), lambda i, j, k: (i, k * gt + j))
    narrow = pl.BlockSpec((tm5, 128), lambda i, j, k: (i, j))
    wb_k = pl.BlockSpec((None, 128, TILE_W), lambda i, j, k: (k * gt + j, 0, 0))
    wc_k = pl.BlockSpec((None, TILE_W, 128), lambda i, j, k: (k * gt + j, 0, 0))
    bu3, (w_out_c_g,) = _s5_expand(proj_ab, wb_t, a_blk=u_off // S5_W, dims=NN, name="s5_bu", hook=out_c_hook)
    w_out_c_f = _from_rows(w_out_c_g, D_MODEL).reshape(GLA_W, D_MODEL)
    xs5 = _s5_scan_fwd(bu3, a_re, a_im)
    xs5_2d = xs5.reshape(lp, 2 * S5_N)
    y_pre, (w_in_c_gb,) = _mm_core(xs5_2d, wc_t, dims=NN, grid=(lp // tm5, gt, 2), name="s5_cx", a_spec=wide_k,
                                   b_spec=wc_k, o_spec=narrow, out_shape=jax.ShapeDtypeStruct((lp, S5_W), F32),
                                   acc_shape=(tm5, 128), hook=in_c_hook_b)
    (y_s5, yg_bf), _ = _rows(
        lambda i, yp, u, d: ((yp + d * u, _gelu(yp + d * u)), ()),
        [_win(y_pre), _win(proj_ab, u_off, S5_W)], [s5_d], [(S5_W, F32), (S5_W, BF16)], [], name="s5_gelu_fwd", nrow=lp)
    t_glu = _mm(yg_bf, w_glu_f, "nn", name="s5_glu_fwd")

    def s5_gate(y, t, zb):
        return _gelu(y) * _sigmoid(t) * _silu(zb)

    (o_b,), _ = _rows(lambda i, y, t, zb: ((s5_gate(y, t, zb),), ()),
                      [_win(y_s5), _win(t_glu), _win(proj_ab, zb_off, S5_W)], [], [(S5_W, BF16)], [],
                      name="s5_gate_fwd", nrow=lp)
    o_ab = jnp.concatenate([o_a, o_b], axis=1)
    h1 = _mm(o_ab, w_out_ab_f, "nn", name="out_ab_fwd", add=h0)
    w_in_c_g = jnp.concatenate([w_in_c_ga, w_in_c_gb], axis=1)
    w_in_c_f = sum(jnp.pad(w_in_c_g[d], ((0, 0), (WIN_STEP * d, IN_C_PAD - WIN_STEP * d - WIN_COLS)))
                   for d in range(N_DEV))

    (hn1,), _ = _rows(lambda i, h, w: ((_rms(h, w),), ()), [_win(h1)], [norm_c_f], [(D_MODEL, BF16)], [],
                      name="norm_c_fwd", nrow=lp)
    proj_c = _mm(hn1, w_in_c_f, "nn", name="in_c_fwd")
    gl_off = 2 * GLA_QK + 2 * GLA_W
    pre_gate = _mm(proj_c, w_gate_f, "nn", name="gate_fwd", a_win=(gl_off, GATE_PAD), bias=b_gate_f)
    gla_xs = [(proj_c, GLA_DK, lambda h: h), (proj_c, GLA_DK, lambda h: GLA_QK // GLA_DK + h),
              (proj_c, GLA_DV, lambda h: 2 * GLA_QK // GLA_DV + h),
              (proj_c, GLA_DV, lambda h: (2 * GLA_QK + GLA_W) // GLA_DV + h),
              (pre_gate, GLA_DK, lambda h: h)]
    gla_kw = dict(heads=GLA_H, nchunk=nchunk, s_shape=(GLA_DV, GLA_DK), out_w=GLA_DV, hpb=GLA_H)
    o_c, gla_sprev = _scan_fwd(_gla_chunk, gla_xs, [], [gla_norm_f], [], name="gla_fwd", **gla_kw)
    h2 = _mm(o_c, w_out_c_f, "nn", name="out_c_fwd", add=h1)

    fnw = final_norm_w.reshape(1, D_MODEL)

    def final_fn(i, h, tgt, w):
        def loss_of(h_, w_):
            err = _rms(h_, w_) - tgt
            return 0.5 * jnp.sum(jnp.mean(err * err, axis=-1))

        real = (i > 0).astype(F32)
        loss_i, (dh, dw) = jax.value_and_grad(loss_of, argnums=(0, 1))(h, w)
        return (dh * real, dh * real), (jnp.full((1, 128), loss_i * real, F32), dw * real)

    (dh2, dh2_bf), (loss_acc, g_final) = _rows(
        final_fn, [_win(h2), _win(loss_target[0], roff=1)], [fnw], [(D_MODEL, F32), (D_MODEL, BF16)],
        [(1, 128), (1, D_MODEL)], name="final_loss", nrow=lp)

    def rs_front(pieces, layout, tag):
        g_full = _pack_big(pieces, layout)
        prow = g_full.shape[1]
        from_sibling = _swap_with_sibling(g_full, "rs_sibling_" + tag)
        mine_by_chip = lax.dynamic_index_in_dim(g_full.reshape(4, 2, prow, PACK_COLS), core, axis=1, keepdims=False)
        (p1, p1_bf), _ = _rows(
            lambda i, a, b: ((a.astype(F32) + b.astype(F32), a.astype(F32) + b.astype(F32)), ()),
            [_win(mine_by_chip.reshape(4 * prow, PACK_COLS)), _win(from_sibling.reshape(4 * prow, PACK_COLS))], [],
            [(PACK_COLS, F32), (PACK_COLS, BF16)], [], name="rs_sum_sibling_" + tag, nrow=4 * prow,
            tr=_tile(prow, 512, 16))
        return p1.reshape(4, prow, PACK_COLS), p1_bf.reshape(4, prow, PACK_COLS)

    def rs_back(p1, from_chips, layout, tag):
        prow = p1.shape[1]
        tr = _tile(prow, 512, 16)
        own = lax.dynamic_index_in_dim(p1, chip, axis=0, keepdims=False)
        fc2 = from_chips.reshape(3 * prow, PACK_COLS)
        nblk = prow // tr
        (g_shard,), _ = _rows(
            lambda i, a, b0, b1, b2: ((((a + b0.astype(F32)) + b1.astype(F32)) + b2.astype(F32),), ()),
            [_win(own), _win(fc2), _win(fc2, roff=-nblk), _win(fc2, roff=-2 * nblk)], [], [(PACK_COLS, F32)], [],
            name="rs_sum_chips_" + tag, nrow=prow, tr=tr)
        return _unpack_big(g_shard, layout)

    do_c = _mm(dh2_bf, w_out_c_f, "nt", name="out_c_dx", out_dtype=BF16)
    gw_out_c = _mm(o_c, dh2_bf, "tn", name="out_c_dw", out_dtype=BF16)
    (dq_c, dk_c, dv_c, dz_c, dpre), (g_gla_norm,) = _scan_bwd(
        _gla_chunk, gla_xs, [], [gla_norm_f], [], do_c, gla_sprev, name="gla_bwd", **gla_kw)
    dglow = _mm(dpre, w_gate_f, "nt", name="gate_dx", out_dtype=BF16)
    g_wgate = _mm(proj_c, dpre, "tn", name="gate_dw", a_win=(gl_off, GATE_PAD))[:GLA_RANK]
    (), (g_bgate,) = _rows(lambda i, d: ((), (jnp.sum(d.astype(F32), axis=0, keepdims=True),)), [_win(dpre)], [], [],
                           [(1, GLA_QK)], name="gate_db", nrow=lp)
    dproj_c = jnp.concatenate([dq_c, dk_c, dv_c, dz_c, dglow], axis=1)
    dhn1 = _mm(dproj_c, w_in_c_f, "nt", name="in_c_dx")
    gw_in_c = _mm(hn1, dproj_c, "tn", name="in_c_dw", out_dtype=BF16)
    p1_c, p1_c_bf = rs_front(dict(
        w_in_c=_column_windows(gw_in_c)),
        IN_C_LAYOUT, "in_c")

    def norm_bwd(i, h, dhn, dres, w):
        _, vjp = jax.vjp(_rms, h, w)
        dh, dw = vjp(dhn)
        return (jnp.where(rowmask(i), dh + dres, 0.0),), (dw,)

    def norm_bwd_both(i, h, dhn, dres, w):
        (dh,), acc = norm_bwd(i, h, dhn, dres, w)
        return (dh, dh), acc

    (dh1, dh1_bf), (g_norm_c,) = _rows(norm_bwd_both, [_win(h1), _win(dhn1), _win(dh2)], [norm_c_f],
                                       [(D_MODEL, F32), (D_MODEL, BF16)], [(1, D_MODEL)], name="norm_c_bwd", nrow=lp)

    do_ab = _mm(dh1_bf, w_out_ab_f, "nt", name="out_ab_dx", out_dtype=BF16)
    gw_out_ab = _mm(o_ab, dh1_bf, "tn", name="out_ab_dw", out_dtype=BF16)

    def s5_gate_bwd(i, dob, y, t, zb):
        _, vjp = jax.vjp(s5_gate, y, t, zb)
        dy, dt, dzb = vjp(dob.astype(F32))
        return (dy, dt, dzb), ()

    (dy_a, dt_glu, dzb), _ = _rows(
        s5_gate_bwd, [_win(do_ab, RET_W, S5_W), _win(y_s5), _win(t_glu), _win(proj_ab, zb_off, S5_W)], [],
        [(S5_W, F32), (S5_W, BF16), (S5_W, BF16)], [], name="s5_gate_bwd", nrow=lp)
    dyg2 = _mm(dt_glu, w_glu_f, "nt", name="s5_glu_dx")
    gw_glu = _mm(yg_bf, dt_glu, "tn", name="s5_glu_dw", out_dtype=BF16)

    def s5_y_bwd(i, dya, dyg, y, u, d):
        _, vjp = jax.vjp(_gelu, y)
        (dy_g,) = vjp(dyg)
        dy = dya + dy_g
        return (dy, d * dy), (jnp.sum(dy * u, axis=0, keepdims=True),)

    (dy_s5, du1), (g_d,) = _rows(
        s5_y_bwd, [_win(dy_a), _win(dyg2), _win(y_s5), _win(proj_ab, u_off, S5_W)], [s5_d],
        [(S5_W, BF16), (S5_W, F32)], [(1, S5_W)], name="s5_y_bwd", nrow=lp)
    p1_o, p1_o_bf = rs_front(dict(s5_w_glu=gw_glu.reshape(N_DEV, S5_W // N_DEV, S5_W),
                                  w_out_ab=gw_out_ab.reshape(N_DEV, OUT_AB // N_DEV, D_MODEL),
                                  w_out_c=gw_out_c.reshape(N_DEV, GLA_W // N_DEV, D_MODEL)), OTHER_LAYOUT, "other")
    o_cut = 640
    gx3, (from_chips_oa,) = _s5_expand(dy_s5, wc_t, a_blk=0, dims=NT, name="s5_cx_dx",
                                       hook=_chips_hook(p1_o_bf, 0, o_cut))
    rows_k = lambda col: pl.BlockSpec((tk5, col), lambda i, j, k: (k, i))
    gwc = _mm_core(xs5_2d, dy_s5, dims=TN, grid=(nt5, 1, lp // tk5), name="s5_cx_dw", a_spec=rows_k(TILE_W),
                   b_spec=pl.BlockSpec((tk5, 128), lambda i, j, k: (k, i % gt)),
                   o_spec=pl.BlockSpec((None, TILE_W, 128), lambda i, j, k: (i, 0, 0)),
                   out_shape=jax.ShapeDtypeStruct((nt5, TILE_W, 128), F32), acc_shape=(TILE_W, 128))
    g_s5, da = _s5_scan_bwd(gx3, xs5, a_re, a_im)
    g_s5_2d = g_s5.reshape(lp, 2 * S5_N)
    du, (from_chips_ob,) = _mm_core(g_s5_2d, wb_t, dims=NT, grid=(lp // tm5, gt, 2), name="s5_bu_dx", a_spec=wide_k,
                                    b_spec=wb_k, o_spec=narrow, out_shape=jax.ShapeDtypeStruct((lp, S5_W), BF16),
                                    acc_shape=(tm5, 128), extra=[(du1, narrow)],
                                    hook=_chips_hook(p1_o_bf, o_cut, None))
    from_chips_o = jnp.concatenate([from_chips_oa, from_chips_ob], axis=1)
    gwb = _mm_core(proj_ab, g_s5_2d, dims=TN, grid=(nt5, 1, lp // tk5), name="s5_bu_dw",
                   a_spec=pl.BlockSpec((tk5, 128), lambda i, j, k: (k, u_blk + i % gt)), b_spec=rows_k(TILE_W),
                   o_spec=pl.BlockSpec((None, 128, TILE_W), lambda i, j, k: (i, 0, 0)),
                   out_shape=jax.ShapeDtypeStruct((nt5, 128, TILE_W), F32), acc_shape=(128, TILE_W))
    gwc6 = gwc.reshape(2, gt, gt, S5_P, gt, S5_GH)
    g_c = jnp.einsum("rsgpgh->rsghp", gwc6).reshape(2, S5_G, S5_GH, S5_P)
    g_c_re, g_c_im = g_c[0], -g_c[1]
    gwb6 = gwb.reshape(2, gt, gt, S5_GH, gt, S5_P)
    d_bb = jnp.einsum("rsghgp->rsgph", gwb6).reshape(2, S5_G, S5_P * S5_GH)
    d_bb_re, d_bb_im = d_bb[0], d_bb[1]
    g_lam_re, g_lam_im, g_log_dt, g_b_re, g_b_im = _s5_disc_bwd(
        disc_args, (da[0].reshape(S5_G, S5_P), da[1].reshape(S5_G, S5_P), d_bb_re, d_bb_im))

    (dq_a, dk_a, dv_a, dz_a), (g_ret_norm,), (from_chips_c,) = _scan_bwd(
        _ret_chunk, ret_xs, ret_cs, [ret_norm_h], [lg], do_ab, ret_sprev, name="ret_bwd", post=_ret_post,
        hook=_chips_hook(p1_c_bf), **ret_kw)
    dproj_ab = jnp.concatenate([dq_a, dk_a, dv_a, dz_a, du, dzb], axis=1)

    lane = lambda a_: pad_to(a_, a_.shape[0], 128)

    def sum8(i, *blocks):
        acc = blocks[0]
        for b in blocks[1:]:
            acc = acc + b
        return (acc,), ()

    def pack_small(pieces):
        return jnp.concatenate([_rows1024(p) for _, p in pieces], axis=0)

    def sum_small(gathered, pieces, tag):
        srow = gathered.shape[1]
        tr = _tile(srow, 128, 8)
        flat = gathered.reshape(N_DEV * srow, PACK_COLS)
        (total,), _ = _rows(sum8, [_win(flat, roff=-d * (srow // tr)) for d in range(N_DEV)], [], [(PACK_COLS, F32)],
                            [], name="sum_small_" + tag, nrow=srow, tr=tr)
        out, o = {}, 0
        for name_, p in pieces:
            r8 = _rows1024(p).shape[0]
            out[name_] = _unrows1024(total[o:o + r8], *p.shape)
            o += r8
        return out

    early_pieces = [
        ("vec2048", jnp.concatenate([g_final, g_norm_c], axis=0)),
        ("vec1024", jnp.concatenate([g_d, g_bgate, pad_to(loss_acc[:, :1], 1, PACK_COLS)], axis=0)),
        ("lam3", jnp.concatenate([lane(g_lam_re), lane(g_lam_im), lane(g_log_dt)], axis=1)),
        ("s5_b_re", g_b_re), ("s5_b_im", g_b_im),
        ("s5_c_re", g_c_re.reshape(S5_G, S5_GH * S5_P)), ("s5_c_im", g_c_im.reshape(S5_G, S5_GH * S5_P)),
        ("ret_norm_w", g_ret_norm.reshape(RET_H, RET_DV)), ("gla_norm_w", g_gla_norm.reshape(GLA_H, GLA_DV)),
        ("gla_w_gate", g_wgate)]
    gw_in_ab, (early_all,) = _mm(
        hn0, dproj_ab, "tn", name="in_ab_dw", out_dest=True, out_dtype=BF16,
        hook=_gather_hook(pack_small(early_pieces)))
    p1_first, p1_first_bf = rs_front(dict(w_in_ab=gw_in_ab), FIRST_LAYOUT, "first")
    dhn0, (from_chips_first, _) = _mm(dproj_ab, w_in_ab_g, "nt", name="in_ab_dx", b_dev=True,
                                      hook=_chips_relay_hook(p1_first_bf))
    def norm_bwd_first(i, h, dhn, dres, w):
        (dh,), (dw,) = norm_bwd(i, h, dhn, dres, w)
        return (dh,), (dw, dh * (i == 0).astype(F32))

    (grad_x2d,), (g_norm_ab, dh0_first) = _rows(
        norm_bwd_first, [_win(h0), _win(dhn0), _win(dh1)], [norm_ab_w], [(D_MODEL, F32, 1)],
        [(1, D_MODEL), (CHUNK, D_MODEL)], name="norm_ab_bwd", nrow=lp)
    grad_x = grad_x2d[None]
    late_pieces = [("norm_ab_w", g_norm_ab), ("meta", dh0_first[PAD:CHUNK])]
    small = sum_small(early_all, early_pieces, "early")
    small.update(sum_small(_all_gather(pack_small(late_pieces), "gather_grads"), late_pieces, "late"))

    big_grads = {**rs_back(p1_c, from_chips_c, IN_C_LAYOUT, "in_c"), **rs_back(p1_o, from_chips_o, OTHER_LAYOUT, "other"),
                 **rs_back(p1_first, from_chips_first, FIRST_LAYOUT, "first")}
    big_grads["w_in_c"] = _lane_select(big_grads["w_in_c"], win_off, -1, 896, F32, True,
                                       "w_in_c_from_window")[:, :SHARD_C]
    small["final_norm_w"], small["norm_c_w"] = small["vec2048"][0:1], small["vec2048"][1:2]
    small["s5_d"], small["gla_b_gate"] = small["vec1024"][0:1], small["vec1024"][1:2]
    loss = small["vec1024"][2, 0]
    small["s5_lam_re"], small["s5_lam_im"] = small["lam3"][:, :S5_P], small["lam3"][:, 128:128 + S5_P]
    small["s5_log_dt"] = small["lam3"][:, 256:257]

    def my_cols(g, n):
        return lax.dynamic_slice_in_dim(g, dev * n, n, axis=g.ndim - 1)

    grads = dict(
        meta=my_cols(small["meta"], D_MODEL // N_DEV),
        norm_ab_w=small["norm_ab_w"], w_in_ab=big_grads["w_in_ab"][None], ret_norm_w=small["ret_norm_w"].reshape(1, RET_W),
        s5_lam_re=small["s5_lam_re"][None], s5_lam_im=small["s5_lam_im"][None],
        s5_log_dt=small["s5_log_dt"].reshape(1, S5_G),
        s5_b_re=small["s5_b_re"].reshape(1, S5_G, S5_P, S5_GH), s5_b_im=small["s5_b_im"].reshape(1, S5_G, S5_P, S5_GH),
        s5_c_re=small["s5_c_re"][None], s5_c_im=small["s5_c_im"][None], s5_d=small["s5_d"],
        s5_w_glu=big_grads["s5_w_glu"][None], w_out_ab=big_grads["w_out_ab"][None],
        norm_c_w=my_cols(small["norm_c_w"], D_MODEL // N_DEV), w_in_c=big_grads["w_in_c"][None],
        gla_w_gate=my_cols(small["gla_w_gate"], GLA_QK // N_DEV)[None],
        gla_b_gate=my_cols(small["gla_b_gate"], GLA_QK // N_DEV),
        gla_norm_w=my_cols(small["gla_norm_w"].reshape(1, GLA_W), GLA_W // N_DEV),
        w_out_c=big_grads["w_out_c"][None], final_norm_w=small["final_norm_w"].reshape(D_MODEL))

    deltas, new_m, new_v = {}, {}, {}
    for k in order:
        w = weights[k]
        d2, m2, v2 = _adamw(_as2d(w), _as2d(grads[k].reshape(w.shape)), _as2d(mom_m[k]), _as2d(mom_v[k]), "adamw_" + k)
        deltas[k], new_m[k], new_v[k] = d2.reshape(w.shape), m2.reshape(w.shape), v2.reshape(w.shape)
        grads[k] = grads[k].reshape(w.shape)

    return (loss, grad_x, *[grads[k] for k in order], *[deltas[k] for k in order],
            *[new_m[k] for k in order], *[new_v[k] for k in order])
```

```python
import functools
import math

import jax
import jax.numpy as jnp
from jax import lax
from jax.experimental import pallas as pl
from jax.experimental.pallas import tpu as pltpu

F32, BF16 = jnp.float32, jnp.bfloat16
MESH = pl.DeviceIdType.MESH
N_DEV = 8

D_MODEL = 2048
CHUNK = 128
N_META = 16
PAD = CHUNK - N_META
SUB = 16
EPS = 1e-6
RET_H, RET_DK, RET_DV = 8, 128, 256
RET_QK, RET_W = RET_H * RET_DK, RET_H * RET_DV
ROPE_BASE = 10000.0
S5_W, S5_G, S5_P, S5_GH = 1024, 64, 64, 16
S5_N = S5_G * S5_P
GLA_H, GLA_DK, GLA_DV, GLA_RANK, GLA_TAU = 4, 256, 512, 16, 16.0
GLA_QK, GLA_W = GLA_H * GLA_DK, GLA_H * GLA_DV
IN_AB = 2 * RET_QK + 2 * RET_W + 2 * S5_W
OUT_AB = RET_W + S5_W
IN_C = 2 * GLA_QK + 2 * GLA_W + GLA_RANK
GATE_PAD = 256
IN_C_PAD = 2 * GLA_QK + 2 * GLA_W + GATE_PAD
ADAM_LR, ADAM_B1, ADAM_B2, ADAM_EPS, ADAM_WD, ADAM_STEP = 0.001, 0.9, 0.999, 1e-08, 0.01, 10

VMEM_LIMIT_BYTES = 48 * 2 ** 20
PACK_COLS = 1024
SHARD_C = IN_C // N_DEV
WIN_STEP = 768
WIN_COLS = 1024


def _params(sem):
    return pltpu.CompilerParams(dimension_semantics=sem, vmem_limit_bytes=VMEM_LIMIT_BYTES)


def _tile(n, cap, mult):
    best = None
    for t in range(mult, min(n, cap) + 1, mult):
        if n % t == 0:
            best = t
    assert best is not None, (n, cap, mult)
    return best


def _dg(a, b, ca, cb):
    return lax.dot_general(a.astype(BF16), b.astype(BF16), (((ca,), (cb,)), ((), ())),
                           preferred_element_type=F32)


@functools.partial(jax.custom_vjp, nondiff_argnums=(2, 3))
def _bdot(a, b, ca, cb):
    return _dg(a, b, ca, cb)


def _bdot_fwd(a, b, ca, cb):
    return _dg(a, b, ca, cb), (a, b)


def _bdot_bwd(ca, cb, res, g):
    a, b = res
    da = _dg(g, b, 1, 1 - cb) if ca == 1 else _dg(b, g, 1 - cb, 1)
    db = _dg(a, g, 1 - ca, 0) if cb == 0 else _dg(g, a, 0, 1 - ca)
    return da.astype(a.dtype), db.astype(b.dtype)


_bdot.defvjp(_bdot_fwd, _bdot_bwd)


def _sigmoid(x):
    return 1.0 / (1.0 + jnp.exp(-x))


def _silu(x):
    return x * _sigmoid(x)


def _log_sigmoid(x):
    return jnp.minimum(x, 0.0) - jnp.log(1.0 + jnp.exp(-jnp.abs(x)))


def _gelu(x):
    return 0.5 * x * (1.0 + jnp.tanh(math.sqrt(2.0 / math.pi) * (x + 0.044715 * (x * x * x))))


def _rms(x, w):
    return x * lax.rsqrt(jnp.mean(x * x, axis=-1, keepdims=True) + EPS) * w


class _Hook:
    def __init__(self, ins, outs, sems, phases):
        self.ins, self.outs, self.sems, self.phases = list(ins), list(outs), list(sems), list(phases)


_NO_HOOK = _Hook([], [], [], [])
_ANY = pl.BlockSpec(memory_space=pl.ANY)


def _run_hook(hook, lin, total, in_refs, out_refs, sem_refs):
    for frac, fn in hook.phases:
        at = min(int(frac * total), total - 1)

        @pl.when(lin == at)
        def _(fn=fn):
            fn(in_refs, out_refs, sem_refs)


def _mm_core(a, b, *, dims, grid, a_spec, b_spec, o_spec, out_shape, acc_shape, name, extra=(), hook=None,
             b_parts=0):
    nk = grid[2]
    n_extra = len(extra)
    hook = _NO_HOOK if hook is None else hook
    hi, ho = len(hook.ins), len(hook.outs)

    def body(*refs):
        a_ref, b_ref = refs[0], refs[1]
        o_ref, acc = refs[2 + n_extra + hi], refs[3 + n_extra + hi + ho]
        k = pl.program_id(2)
        lin = (pl.program_id(0) * grid[1] + pl.program_id(1)) * nk + k
        _run_hook(hook, lin, grid[0] * grid[1] * nk, refs[2 + n_extra:2 + n_extra + hi],
                  refs[3 + n_extra + hi:3 + n_extra + hi + ho], refs[4 + n_extra + hi + ho:])

        if b_parts:
            part = sum(lax.dot_general(a_ref[:, d * PACK_COLS:(d + 1) * PACK_COLS].astype(BF16), b_ref[d].astype(BF16),
                                       dims, preferred_element_type=F32) for d in range(b_parts))
        else:
            part = lax.dot_general(a_ref[...].astype(BF16), b_ref[...].astype(BF16), dims, preferred_element_type=F32)

        def finish(r):
            for e in range(n_extra):
                r = r + refs[2 + e][...].astype(F32)
            o_ref[...] = r.astype(o_ref.dtype)

        if nk == 1:
            finish(part)
        else:
            @pl.when(k == 0)
            def _():
                acc[...] = part

            @pl.when(k > 0)
            def _():
                acc[...] += part

            @pl.when(k == nk - 1)
            def _():
                finish(acc[...])

    res = pl.pallas_call(
        body, name=name, grid=grid,
        in_specs=[a_spec, b_spec] + [sp for _, sp in extra] + [_ANY] * hi,
        out_specs=[o_spec] + [_ANY] * ho, out_shape=[out_shape] + hook.outs,
        scratch_shapes=[pltpu.VMEM(acc_shape if nk > 1 else (8, 128), F32)] + hook.sems,
        compiler_params=_params(("arbitrary", "arbitrary", "arbitrary")),
    )(a, b, *[arr for arr, _ in extra], *hook.ins)
    return res[0] if hook is _NO_HOOK else (res[0], res[1:])


NN, NT, TN = (((1,), (0,)), ((), ())), (((1,), (1,)), ((), ())), (((0,), (0,)), ((), ()))


FULL_K = 2048


def _mm(a, b, mode, *, name, out_dtype=F32, a_win=None, add=None, bias=None, hook=None, b_dev=False,
        out_dest=False):
    b_parts = 0
    if mode == "tn":
        kdim, n = a.shape[0], b.shape[1]
        m = a.shape[1] if a_win is None else a_win[1]
        tm, tn, tk = _tile(m, 512, 128), _tile(n, 640, 128), kdim
        off = 0 if a_win is None else a_win[0] // tm
        a_spec = pl.BlockSpec((tk, tm), lambda i, j, k: (k, i + off))
        b_spec = pl.BlockSpec((tk, tn), lambda i, j, k: (k, j))
        dims = TN
    else:
        m = a.shape[0]
        kdim = a.shape[1] if a_win is None else a_win[1]
        if b_dev:
            n = b.shape[0] * b.shape[2] if mode == "nn" else b.shape[1]
        else:
            n = b.shape[1] if mode == "nn" else b.shape[0]
        if FULL_K < kdim <= 2 * FULL_K and not b_dev:
            tm, tn, tk = _tile(m, 1408, 8), _tile(n, 1024, 128), _tile(kdim, 1024, 128)
        else:
            tm = _tile(m, 1408 if kdim <= FULL_K else 352, 8)
            tn, tk = _tile(n, 640, 128), kdim
        off = 0 if a_win is None else a_win[0] // tk
        a_spec = pl.BlockSpec((tm, tk), lambda i, j, k: (i, k + off))
        if mode == "nn":
            dims = NN
            if b_dev:
                per = PACK_COLS // tn
                b_spec = pl.BlockSpec((None, tk, tn), lambda i, j, k: (j // per, k, j % per))
            else:
                b_spec = pl.BlockSpec((tk, tn), lambda i, j, k: (k, j))
        else:
            dims = NT
            if b_dev:
                b_parts = kdim // PACK_COLS
                b_spec = pl.BlockSpec((b_parts, tn, PACK_COLS), lambda i, j, k: (0, j, 0))
            else:
                b_spec = pl.BlockSpec((tn, tk), lambda i, j, k: (j, k))
    if a_win is not None:
        assert a_win[0] % (tm if mode == "tn" else tk) == 0
    extra = []
    if add is not None:
        extra.append((add, pl.BlockSpec((tm, tn), lambda i, j, k: (i, j))))
    if bias is not None:
        extra.append((bias, pl.BlockSpec((1, tn), lambda i, j, k: (0, j))))
    if out_dest:
        per = PACK_COLS // tn
        o_spec = pl.BlockSpec((None, tm, tn), lambda i, j, k: (j // per, i, j % per))
        out_shape = jax.ShapeDtypeStruct((n // PACK_COLS, m, PACK_COLS), out_dtype)
    else:
        o_spec = pl.BlockSpec((tm, tn), lambda i, j, k: (i, j))
        out_shape = jax.ShapeDtypeStruct((m, n), out_dtype)
    return _mm_core(a, b, dims=dims, grid=(m // tm, n // tn, kdim // tk), a_spec=a_spec, b_spec=b_spec,
                    o_spec=o_spec, out_shape=out_shape, acc_shape=(tm, tn), name=name, extra=extra, hook=hook,
                    b_parts=b_parts)


def _win(arr, col0=0, width=None, roff=0):
    return (arr, col0, arr.shape[1] if width is None else width, roff)


def _rows(fn, rows, consts, outs, accs, *, name, nrow, tr=CHUNK):
    nr, nc, no = len(rows), len(consts), len(outs)

    def body(*refs):
        i = pl.program_id(0)
        ins = [r[...] for r in refs[:nr + nc]]
        o_refs = refs[nr + nc:nr + nc + no]
        a_refs = refs[nr + nc + no:]
        res_o, res_a = fn(i, *ins)
        for r, v in zip(o_refs, res_o):
            r[...] = v.astype(r.dtype)
        if a_refs:
            @pl.when(i == 0)
            def _():
                for r in a_refs:
                    r[...] = jnp.zeros_like(r)

            for r, v in zip(a_refs, res_a):
                r[...] += v

    in_specs = []
    for (arr, col0, width, roff) in rows:
        assert col0 % width == 0 and arr.shape[0] % tr == 0
        in_specs.append(pl.BlockSpec((tr, width), lambda i, c=col0 // width, ro=roff: (jnp.maximum(i - ro, 0), c)))
    for c in consts:
        in_specs.append(pl.BlockSpec(c.shape, lambda i, nd=c.ndim: (0,) * nd))
    outs = [tuple(o) + (0,) * (3 - len(o)) for o in outs]
    out_specs = [pl.BlockSpec((tr, w), lambda i, ro=ro: (jnp.maximum(i - ro, 0), 0)) for (w, _, ro) in outs]
    out_specs += [pl.BlockSpec(s, lambda i, nd=len(s): (0,) * nd) for s in accs]
    out_shape = [jax.ShapeDtypeStruct((nrow - ro * tr, w), dt) for (w, dt, ro) in outs]
    out_shape += [jax.ShapeDtypeStruct(s, F32) for s in accs]
    res = pl.pallas_call(
        body, name=name, grid=(nrow // tr,), in_specs=in_specs, out_specs=out_specs, out_shape=out_shape,
        compiler_params=_params(("arbitrary",)),
    )(*[r[0] for r in rows], *consts)
    return res[:no], res[no:]


HEADS_PER_STEP = 2


def _scan_specs(xs, cs, ws, ks, chunk_of, hpb):
    specs = []
    for (arr, width, colfn) in xs:
        specs.append(pl.BlockSpec((CHUNK, width * hpb), lambda h, n, f=colfn: (chunk_of(n), f(h * hpb) // hpb)))
    for (arr, width, colfn) in cs:
        specs.append(pl.BlockSpec((CHUNK, width), lambda h, n, f=colfn: (chunk_of(n), f(h))))
    for arr in list(ws) + list(ks):
        specs.append(pl.BlockSpec((hpb, 1, arr.shape[2]), lambda h, n: (h, 0, 0)))
    return specs


def _scan_fwd(fn, xs, cs, ws, ks, *, heads, nchunk, s_shape, out_w, name, pre=None, hook=None,
              hpb=HEADS_PER_STEP):
    nx, ncs, nw = len(xs), len(cs), len(ws)
    hook = _NO_HOOK if hook is None else hook
    hi, ho = len(hook.ins), len(hook.outs)
    hblocks = heads // hpb

    def body(*refs):
        n = pl.program_id(1)
        nin = nx + ncs + nw + len(ks)
        y_ref, sp_ref = refs[nin + hi], refs[nin + hi + 1]
        s_scr = refs[nin + hi + 2 + ho]
        _run_hook(hook, pl.program_id(0) * nchunk + n, hblocks * nchunk, refs[nin:nin + hi],
                  refs[nin + hi + 2:nin + hi + 2 + ho], refs[nin + hi + 3 + ho:])

        @pl.when(n == 0)
        def _():
            s_scr[...] = jnp.zeros_like(s_scr)

        cv = [r[...] for r in refs[nx:nx + ncs]]
        for e in range(hpb):
            state = s_scr[e]
            sp_ref[e, 0] = state
            xv = [r[:, e * w:(e + 1) * w] for r, (_, w, _) in zip(refs[:nx], xs)]
            wv = [r[e] for r in refs[nx + ncs:nx + ncs + nw]]
            kv = [r[e] for r in refs[nx + ncs + nw:nin]]
            if pre is not None:
                xv = pre(xv, cv)
            y, s_new = fn(n, xv, state, cv, wv, kv)
            y_ref[:, e * out_w:(e + 1) * out_w] = y.astype(y_ref.dtype)
            s_scr[e] = s_new

    lp = nchunk * CHUNK
    res = pl.pallas_call(
        body, name=name, grid=(hblocks, nchunk),
        in_specs=_scan_specs(xs, cs, ws, ks, lambda n: n, hpb) + [_ANY] * hi,
        out_specs=[pl.BlockSpec((CHUNK, out_w * hpb), lambda h, n: (n, h)),
                   pl.BlockSpec((hpb, 1) + s_shape, lambda h, n: (h, n, 0, 0))] + [_ANY] * ho,
        out_shape=[jax.ShapeDtypeStruct((lp, heads * out_w), BF16),
                   jax.ShapeDtypeStruct((heads, nchunk) + s_shape, F32)] + hook.outs,
        scratch_shapes=[pltpu.VMEM((hpb,) + s_shape, F32)] + hook.sems,
        compiler_params=_params(("arbitrary", "arbitrary")),
    )(*[t[0] for t in xs], *[t[0] for t in cs], *ws, *ks, *hook.ins)
    return (res[0], res[1]) if hook is _NO_HOOK else (res[0], res[1], res[2:])


def _scan_bwd(fn, xs, cs, ws, ks, dy, sprev, *, heads, nchunk, s_shape, out_w, name, pre=None, post=None,
              hook=None, hpb=HEADS_PER_STEP):
    nx, ncs, nw = len(xs), len(cs), len(ws)
    nin = nx + ncs + nw + len(ks)
    hook = _NO_HOOK if hook is None else hook
    hi, ho = len(hook.ins), len(hook.outs)
    hblocks = heads // hpb

    def body(*refs):
        step = pl.program_id(1)
        n = nchunk - 1 - step
        dy_ref, sp_ref = refs[nin], refs[nin + 1]
        o0 = nin + 2 + hi
        dx_refs = refs[o0:o0 + nx]
        dw_refs = refs[o0 + nx:o0 + nx + nw]
        ds_scr = refs[o0 + nx + nw + ho]
        _run_hook(hook, pl.program_id(0) * nchunk + step, hblocks * nchunk, refs[nin + 2:o0],
                  refs[o0 + nx + nw:o0 + nx + nw + ho], refs[o0 + nx + nw + ho + 1:])

        @pl.when(step == 0)
        def _():
            ds_scr[...] = jnp.zeros_like(ds_scr)
            for r in dw_refs:
                r[...] = jnp.zeros_like(r)

        cv = [r[...] for r in refs[nx:nx + ncs]]
        for e in range(hpb):
            xv = [r[:, e * w:(e + 1) * w] for r, (_, w, _) in zip(refs[:nx], xs)]
            wv = [r[e] for r in refs[nx + ncs:nx + ncs + nw]]
            kv = [r[e] for r in refs[nx + ncs + nw:nin]]
            if pre is not None:
                xv = pre(xv, cv)
            _, vjp = jax.vjp(lambda xs_, s_, ws_, kv=kv: fn(n, xs_, s_, cv, ws_, kv), xv, sp_ref[e, 0], wv)
            dxs, ds_prev, dws = vjp((dy_ref[:, e * out_w:(e + 1) * out_w].astype(F32), ds_scr[e]))
            if post is not None:
                dxs = post(dxs, cv)
            for r, v, (_, w, _) in zip(dx_refs, dxs, xs):
                r[:, e * w:(e + 1) * w] = v.astype(r.dtype)
            for r, v in zip(dw_refs, dws):
                r[e] += v
            ds_scr[e] = ds_prev

    lp = nchunk * CHUNK
    rev = lambda n: nchunk - 1 - n
    in_specs = _scan_specs(xs, cs, ws, ks, rev, hpb)
    in_specs.append(pl.BlockSpec((CHUNK, out_w * hpb), lambda h, n: (rev(n), h)))
    in_specs.append(pl.BlockSpec((hpb, 1) + s_shape, lambda h, n: (h, rev(n), 0, 0)))
    out_specs = [pl.BlockSpec((CHUNK, w * hpb), lambda h, n: (rev(n), h)) for (_, w, _) in xs]
    out_specs += [pl.BlockSpec((hpb, 1, w.shape[2]), lambda h, n: (h, 0, 0)) for w in ws]
    out_shape = [jax.ShapeDtypeStruct((lp, heads * w), BF16) for (_, w, _) in xs]
    out_shape += [jax.ShapeDtypeStruct(w.shape, F32) for w in ws]
    res = pl.pallas_call(
        body, name=name, grid=(hblocks, nchunk), in_specs=in_specs + [_ANY] * hi,
        out_specs=out_specs + [_ANY] * ho, out_shape=out_shape + hook.outs,
        scratch_shapes=[pltpu.VMEM((hpb,) + s_shape, F32)] + hook.sems,
        compiler_params=_params(("arbitrary", "arbitrary")),
    )(*[t[0] for t in xs], *[t[0] for t in cs], *ws, *ks, dy, sprev, *hook.ins)
    if hook is _NO_HOOK:
        return res[:nx], res[nx:]
    return res[:nx], res[nx:nx + nw], res[nx + nw:]


def _iota2(shape, dim):
    return lax.broadcasted_iota(jnp.int32, shape, dim)


def _ret_chunk(n, xs, state, cs, ws, ks):
    q, k, v, z = xs
    (w,), (lg,) = ws, ks
    lgc = lg[:, :1]
    row, col = _iota2((CHUNK, CHUNK), 0), _iota2((CHUNK, CHUNK), 1)
    diff = jnp.maximum(row - col, 0).astype(F32)
    decay = jnp.where(row >= col, jnp.exp(lg * diff), 0.0)
    scores = _bdot(q, k, 1, 1) * decay
    o_intra = _bdot(scores, v, 1, 0)
    idx = _iota2((CHUNK, 1), 0).astype(F32)
    k_w = k * jnp.exp(lgc * (CHUNK - 1.0 - idx))
    kv = _bdot(k_w, v, 0, 0)
    s_new = state * jnp.exp(lgc * float(CHUNK)) + kv
    q_w = q * jnp.exp(lgc * (idx + 1.0))
    o = o_intra + _bdot(q_w, state, 1, 0)
    return _rms(o, w) * _silu(z), s_new


def _rope(t, cos2, sin2):
    return t * cos2 + pltpu.roll(t, RET_DK // 2, 1) * sin2


def _rope_t(g, cos2, sin2):
    return g * cos2 - pltpu.roll(g, RET_DK // 2, 1) * sin2


def _ret_pre(xv, cv):
    q, k, v, z = xv
    cos2, sin2 = cv
    return [_rope(q, cos2, sin2), _rope(k, cos2, sin2) * (RET_DK ** -0.5), v, z]


def _ret_post(dxs, cv):
    dq, dk, dv, dz = dxs
    cos2, sin2 = cv
    return [_rope_t(dq, cos2, sin2), _rope_t(dk, cos2, sin2) * (RET_DK ** -0.5), dv, dz]


def _tri_apply(x, lower):
    n = x.shape[0]
    row, col = _iota2((n, n), 0), _iota2((n, n), 1)
    tri = (row >= col if lower else row <= col).astype(BF16)
    hi = x.astype(BF16)
    rest = x - hi.astype(F32)
    mid = rest.astype(BF16)
    lo = (rest - mid.astype(F32)).astype(BF16)
    return sum(lax.dot_general(tri, p, NN, preferred_element_type=F32) for p in (hi, mid, lo))


@jax.custom_vjp
def _cumsum_rows(x):
    return _tri_apply(x, True)


_cumsum_rows.defvjp(lambda x: (_tri_apply(x, True), None), lambda _, g: (_tri_apply(g, False),))


def _cumsum_rows_f32(x):
    n = x.shape[0]
    tri = (_iota2((n, n), 0) >= _iota2((n, n), 1)).astype(F32)
    return jnp.dot(tri, x, precision=lax.Precision.HIGHEST, preferred_element_type=F32)


def _gla_chunk(n, xs, state_t, cs, ws, ks, cumsum=_cumsum_rows):
    q, k, v, z, pre = xs
    (w,) = ws
    q = q * (GLA_DK ** -0.5)
    rowc = _iota2((CHUNK, 1), 0)
    valid = jnp.logical_or(n > 0, rowc >= PAD)
    log_a = jnp.where(valid, _log_sigmoid(pre) / GLA_TAU, 0.0)
    b = cumsum(log_a)
    b_last = b[CHUNK - 1:CHUNK, :]
    kv_t = _bdot(v, k * jnp.exp(b_last - b), 0, 0)
    s_new = state_t * jnp.exp(b_last) + kv_t
    o_inter = _bdot(q * jnp.exp(b), state_t, 1, 1)
    outs = []
    for s in range(CHUNK // SUB):
        lo, hi = s * SUB, (s + 1) * SUB
        b_ref = jnp.zeros_like(b_last) if s == 0 else b[lo - 1:lo, :]
        q_hat = q[lo:hi] * jnp.exp(b[lo:hi] - b_ref)
        k_hat = k[:hi] * jnp.exp(b_ref - b[:hi])
        sc = _bdot(q_hat, k_hat, 1, 1)
        causal = _iota2((SUB, hi), 0) + lo >= _iota2((SUB, hi), 1)
        outs.append(_bdot(jnp.where(causal, sc, 0.0), v[:hi], 1, 0))
    o = jnp.concatenate(outs, axis=0) + o_inter
    return _rms(o, w) * _silu(z), s_new


def _s5_disc(lam_re, lam_im, log_dt, b_re, b_im, expand):
    dt = jnp.exp(log_dt)
    mag = jnp.exp(lam_re * dt)
    ab_re, ab_im = mag * jnp.cos(lam_im * dt), mag * jnp.sin(lam_im * dt)
    den = lam_re * lam_re + lam_im * lam_im
    nr, ni = ab_re - 1.0, ab_im
    f_re = (nr * lam_re + ni * lam_im) / den
    f_im = (ni * lam_re - nr * lam_im) / den
    hp = lax.Precision.HIGHEST
    f_re = jnp.dot(f_re, expand, precision=hp, preferred_element_type=F32)
    f_im = jnp.dot(f_im, expand, precision=hp, preferred_element_type=F32)
    return ab_re, ab_im, f_re * b_re - f_im * b_im, f_re * b_im + f_im * b_re


def _s5_disc_fwd(args):
    def body(*refs):
        outs = _s5_disc(*[r[...] for r in refs[:6]])
        for r, v in zip(refs[6:], outs):
            r[...] = v

    g, p = args[0].shape
    return pl.pallas_call(
        body, name="s5_disc_fwd",
        out_shape=[jax.ShapeDtypeStruct((g, p), F32)] * 2 + [jax.ShapeDtypeStruct(args[3].shape, F32)] * 2,
    )(*args)


def _s5_disc_bwd(args, cts):
    def body(*refs):
        prim = [r[...] for r in refs[:5]]
        expand = refs[5][...]
        ct = tuple(r[...] for r in refs[6:10])
        _, vjp = jax.vjp(lambda *a: _s5_disc(*a, expand), *prim)
        for r, v in zip(refs[10:], vjp(ct)):
            r[...] = v

    return pl.pallas_call(
        body, name="s5_disc_bwd", out_shape=[jax.ShapeDtypeStruct(a.shape, F32) for a in args[:5]],
    )(*args, *cts)


SCAN_ROWS, SCAN_LANES = 32, 128
TILE_G = 8
TILE_W = TILE_G * S5_P
S5_TB = 64


def _s5_scan_fwd(bu, a_re, a_im):
    lp = bu.shape[0]

    def body(bu_ref, ar_ref, ai_ref, x_ref, st):
        @pl.when(pl.program_id(0) == 0)
        def _():
            st[...] = jnp.zeros_like(st)

        ar, ai = ar_ref[...], ai_ref[...]

        def step(t, carry):
            xr, xi = carry
            nr = ar * xr - ai * xi + bu_ref[t, 0:SCAN_ROWS, :]
            ni = ar * xi + ai * xr + bu_ref[t, SCAN_ROWS:2 * SCAN_ROWS, :]
            x_ref[t, 0:SCAN_ROWS, :] = nr
            x_ref[t, SCAN_ROWS:2 * SCAN_ROWS, :] = ni
            return nr, ni

        xr, xi = lax.fori_loop(0, S5_TB, step, (st[0], st[1]))
        st[0] = xr
        st[1] = xi

    blk = pl.BlockSpec((S5_TB, 2 * SCAN_ROWS, SCAN_LANES), lambda i: (i, 0, 0))
    cst = pl.BlockSpec((SCAN_ROWS, SCAN_LANES), lambda i: (0, 0))
    return pl.pallas_call(
        body, name="s5_scan_fwd", grid=(lp // S5_TB,), in_specs=[blk, cst, cst], out_specs=blk,
        out_shape=jax.ShapeDtypeStruct(bu.shape, F32),
        scratch_shapes=[pltpu.VMEM((2, SCAN_ROWS, SCAN_LANES), F32)],
        compiler_params=_params(("arbitrary",)),
    )(bu, a_re, a_im)


def _s5_expand(a, w_t, *, a_blk, dims, name, hook=None):
    lp, nt = a.shape[0], w_t.shape[0]
    tm = _tile(lp, 176, 8)
    steps = lp // tm
    rows3 = 2 * SCAN_ROWS
    per = TILE_W // SCAN_LANES
    hook = _NO_HOOK if hook is None else hook
    hi, ho = len(hook.ins), len(hook.outs)

    def body(*refs):
        a_ref, w_ref, o_ref = refs[0], refs[1], refs[2 + hi]
        _run_hook(hook, pl.program_id(0), steps, refs[2:2 + hi], refs[3 + hi:3 + hi + ho], refs[3 + hi + ho:])
        for j in range(nt):
            s = j % TILE_G
            r = lax.dot_general(a_ref[:, 128 * s:128 * (s + 1)].astype(BF16), w_ref[j], dims,
                                preferred_element_type=F32)
            for c in range(per):
                o_ref[pl.ds(per * j + c, tm, stride=rows3), :] = r[:, SCAN_LANES * c:SCAN_LANES * (c + 1)]

    res = pl.pallas_call(
        body, name=name, grid=(steps,),
        in_specs=[pl.BlockSpec((tm, S5_W), lambda i: (i, a_blk)), pl.BlockSpec(w_t.shape, lambda i: (0, 0, 0))]
        + [_ANY] * hi,
        out_specs=[pl.BlockSpec((tm * rows3, SCAN_LANES), lambda i: (i, 0))] + [_ANY] * ho,
        out_shape=[jax.ShapeDtypeStruct((lp * rows3, SCAN_LANES), F32)] + hook.outs,
        scratch_shapes=hook.sems, compiler_params=_params(("arbitrary",)),
    )(a, w_t, *hook.ins)
    out3 = res[0].reshape(lp, rows3, SCAN_LANES)
    return out3 if hook is _NO_HOOK else (out3, res[1:])


def _s5_scan_bwd(gx, x, a_re, a_im):
    lp = gx.shape[0]
    nb = lp // S5_TB

    def body(gx_ref, x_ref, xp_ref, ar_ref, ai_ref, g_ref, da_ref, st):
        i = pl.program_id(0)

        @pl.when(i == 0)
        def _():
            st[...] = jnp.zeros_like(st)
            da_ref[...] = jnp.zeros_like(da_ref)

        ar, ai = ar_ref[...], ai_ref[...]
        first = (i == nb - 1).astype(F32)

        def step(s, carry):
            gr, gi, dar, dai = carry
            t = S5_TB - 1 - s
            ngr = gx_ref[t, 0:SCAN_ROWS, :] + ar * gr + ai * gi
            ngi = gx_ref[t, SCAN_ROWS:2 * SCAN_ROWS, :] + ar * gi - ai * gr
            g_ref[t, 0:SCAN_ROWS, :] = ngr
            g_ref[t, SCAN_ROWS:2 * SCAN_ROWS, :] = ngi
            tp = jnp.maximum(t - 1, 0)
            at0 = (t == 0).astype(F32)
            keep = 1.0 - at0
            pr = keep * x_ref[tp, 0:SCAN_ROWS, :] + at0 * (1.0 - first) * xp_ref[0, 0:SCAN_ROWS, :]
            pi = keep * x_ref[tp, SCAN_ROWS:2 * SCAN_ROWS, :] + at0 * (1.0 - first) * xp_ref[0, SCAN_ROWS:2 * SCAN_ROWS, :]
            return ngr, ngi, dar + ngr * pr + ngi * pi, dai + ngi * pr - ngr * pi

        zero = jnp.zeros((SCAN_ROWS, SCAN_LANES), F32)
        gr, gi, dar, dai = lax.fori_loop(0, S5_TB, step, (st[0], st[1], zero, zero))
        st[0] = gr
        st[1] = gi
        da_ref[0] += dar
        da_ref[1] += dai

    rev = lambda i: nb - 1 - i
    blk = pl.BlockSpec((S5_TB, 2 * SCAN_ROWS, SCAN_LANES), lambda i: (rev(i), 0, 0))
    prev = pl.BlockSpec((1, 2 * SCAN_ROWS, SCAN_LANES), lambda i: (jnp.maximum(rev(i) * S5_TB - 1, 0), 0, 0))
    cst = pl.BlockSpec((SCAN_ROWS, SCAN_LANES), lambda i: (0, 0))
    return pl.pallas_call(
        body, name="s5_scan_bwd", grid=(nb,), in_specs=[blk, blk, prev, cst, cst],
        out_specs=[blk, pl.BlockSpec((2, SCAN_ROWS, SCAN_LANES), lambda i: (0, 0, 0))],
        out_shape=[jax.ShapeDtypeStruct(gx.shape, F32), jax.ShapeDtypeStruct((2, SCAN_ROWS, SCAN_LANES), F32)],
        scratch_shapes=[pltpu.VMEM((2, SCAN_ROWS, SCAN_LANES), F32)],
        compiler_params=_params(("arbitrary",)),
    )(gx, x, x, a_re, a_im)


def _place():
    x, y, c = lax.axis_index("x"), lax.axis_index("y"), lax.axis_index("c")
    return x, y, c, [(1 - x, y), (x, 1 - y), (1 - x, 1 - y)]


def _gather_phases(nrows):
    half = (nrows // 32) * 16
    assert 0 < half < nrows

    def plan(x_ref, out_ref, send_sems, recv_sems, local_sem):
        x, y, c, _ = _place()
        me, sibling = (x, y, c), (x, y, 1 - c)
        at_x, at_y, at_d = (1 - x, y, c), (x, 1 - y, c), (1 - x, 1 - y, c)

        def rows(block, part=None):
            idx = 4 * block[0] + 2 * block[1] + block[2]
            if part is None:
                return out_ref.at[idx]
            return out_ref.at[idx, pl.ds(0, half)] if part == 0 else out_ref.at[idx, pl.ds(half, nrows - half)]

        def copy(k, block, to, src=None, part=None):
            return pltpu.make_async_remote_copy(
                src_ref=rows(block, part) if src is None else src, dst_ref=rows(block, part),
                send_sem=send_sems.at[k], recv_sem=recv_sems.at[k], device_id=to, device_id_type=MESH)

        def other_core(block):
            return (block[0], block[1], 1 - c)

        mine = pltpu.make_async_copy(x_ref, rows(me), local_sem)
        direct = [copy(0, me, sibling, src=x_ref), copy(1, me, at_x, src=x_ref), copy(2, me, at_y, src=x_ref)]
        relays = [copy(3, at_x, at_y, part=0), copy(7, at_y, at_x, part=1)]
        passed = [copy(4, at_x, sibling), copy(5, at_y, sibling), copy(6, at_d, sibling)]
        landed = [copy(1, at_x, me), copy(2, at_y, me), copy(3, at_d, me, part=0), copy(7, at_d, me, part=1)]
        from_sibling = [copy(0, sibling, me)] + [copy(4 + j, other_core(b), me) for j, b in enumerate((at_x, at_y, at_d))]
        return mine, direct, relays, passed, landed, from_sibling

    def start(ins, outs, sems):
        mine, direct, _, _, _, _ = plan(ins[0], outs[0], *sems)
        mine.start()
        for cp in direct:
            cp.start()

    def middle(ins, outs, sems):
        _, _, relays, passed, landed, _ = plan(ins[0], outs[0], *sems)
        for j in range(2):
            landed[j].wait_recv()
            passed[j].start()
            relays[j].start()

    def late(ins, outs, sems):
        _, _, _, passed, landed, _ = plan(ins[0], outs[0], *sems)
        landed[2].wait_recv()
        landed[3].wait_recv()
        passed[2].start()

    def finish(ins, outs, sems):
        mine, direct, relays, passed, _, from_sibling = plan(ins[0], outs[0], *sems)
        for cp in from_sibling:
            cp.wait_recv()
        for cp in direct + relays + passed:
            cp.wait_send()
        mine.wait()

    return start, middle, late, finish


_GATHER_SEMS = [pltpu.SemaphoreType.DMA((8,)), pltpu.SemaphoreType.DMA((8,)), pltpu.SemaphoreType.DMA]


def _all_gather(shard, name):
    phases = _gather_phases(shard.shape[0])

    def body(x_ref, out_ref, *sems):
        for phase in phases:
            phase([x_ref], [out_ref], sems)

    return pl.pallas_call(
        body, name=name, out_shape=jax.ShapeDtypeStruct((N_DEV,) + shard.shape, shard.dtype),
        in_specs=[_ANY], out_specs=_ANY, scratch_shapes=list(_GATHER_SEMS),
    )(shard)


def _gather_hook(shard):
    start, middle, late, finish = _gather_phases(shard.shape[0])
    return _Hook([shard], [jax.ShapeDtypeStruct((N_DEV,) + shard.shape, shard.dtype)], _GATHER_SEMS,
                 [(0.0, start), (0.5, middle), (0.85, late), (1.0, finish)])


def _swap_with_sibling(parts, name):
    def body(p_ref, out_ref, send_sems, recv_sems):
        x, y, c, _ = _place()
        copies = [pltpu.make_async_remote_copy(
            src_ref=p_ref.at[2 * chip + (1 - c)], dst_ref=out_ref.at[chip],
            send_sem=send_sems.at[chip], recv_sem=recv_sems.at[chip],
            device_id=(x, y, 1 - c), device_id_type=MESH) for chip in range(4)]
        for cp in copies:
            cp.start()
        for cp in copies:
            cp.wait()

    return pl.pallas_call(
        body, name=name, out_shape=jax.ShapeDtypeStruct((4,) + parts.shape[1:], parts.dtype),
        in_specs=[pl.BlockSpec(memory_space=pl.ANY)], out_specs=pl.BlockSpec(memory_space=pl.ANY),
        scratch_shapes=[pltpu.SemaphoreType.DMA((4,)), pltpu.SemaphoreType.DMA((4,))],
    )(parts)


def _chips_phases(lo, rows):
    def copies(p_ref, out_ref, send_sems, recv_sems):
        x, y, c, chips = _place()
        return [pltpu.make_async_remote_copy(
            src_ref=p_ref.at[2 * px + py, pl.ds(lo, rows)], dst_ref=out_ref.at[j],
            send_sem=send_sems.at[j], recv_sem=recv_sems.at[j],
            device_id=(px, py, c), device_id_type=MESH) for j, (px, py) in enumerate(chips)]

    def start(ins, outs, sems):
        for cp in copies(ins[0], outs[0], *sems):
            cp.start()

    def finish(ins, outs, sems):
        for cp in copies(ins[0], outs[0], *sems):
            cp.wait()

    return start, finish


def _chips_hook(parts, lo=0, hi=None):
    rows = (parts.shape[1] if hi is None else hi) - lo
    start, finish = _chips_phases(lo, rows)
    return _Hook([parts], [jax.ShapeDtypeStruct((3, rows) + parts.shape[2:], parts.dtype)],
                 [pltpu.SemaphoreType.DMA((3,)), pltpu.SemaphoreType.DMA((3,))], [(0.0, start), (1.0, finish)])


BIG_LAYOUT = (("w_in_ab", D_MODEL, PACK_COLS), ("s5_w_glu", S5_W // N_DEV, PACK_COLS),
              ("w_out_ab", OUT_AB // N_DEV, 2 * PACK_COLS), ("w_in_c", D_MODEL, PACK_COLS),
              ("w_out_c", GLA_W // N_DEV, 2 * PACK_COLS))


def _to_rows(a):
    if a.shape[-1] == PACK_COLS:
        return a
    assert a.shape[-1] == 2 * PACK_COLS
    return jnp.concatenate([a[..., :PACK_COLS], a[..., PACK_COLS:]], axis=-2)


def _from_rows(p, cols):
    if cols == PACK_COLS:
        return p
    r = p.shape[-2] // 2
    return jnp.concatenate([p[..., :r, :], p[..., r:, :]], axis=-1)


FIRST_LAYOUT = BIG_LAYOUT[:1]
OTHER_LAYOUT = BIG_LAYOUT[1:3] + BIG_LAYOUT[4:]
GLU_AB_LAYOUT = BIG_LAYOUT[1:3]
IN_C_LAYOUT = BIG_LAYOUT[3:4]


def _pack_big(pieces, layout):
    return jnp.concatenate([_to_rows(pieces[name]) for name, _, _ in layout], axis=-2)


def _unpack_big(buf, layout):
    out, o = {}, 0
    for name, rows, cols in layout:
        r = rows * cols // PACK_COLS
        out[name] = _from_rows(buf[..., o:o + r, :], cols)
        o += r
    return out


def _column_windows(g):
    rows, quarter = g.shape[0], WIN_COLS // 4

    def body(g_ref, o_ref):
        o_ref[...] = g_ref[...]

    return pl.pallas_call(
        body, name="w_in_c_grad_windows", grid=(N_DEV, WIN_COLS // quarter),
        in_specs=[pl.BlockSpec((rows, quarter), lambda d, c: (0, (WIN_STEP // quarter) * d + c))],
        out_specs=pl.BlockSpec((None, rows, quarter), lambda d, c: (d, 0, c)),
        out_shape=jax.ShapeDtypeStruct((N_DEV, rows, WIN_COLS), g.dtype),
        compiler_params=_params(("arbitrary", "arbitrary")),
    )(g)


def _rows1024(a):
    r, c = a.shape
    if c > PACK_COLS:
        a = jnp.concatenate([a[:, i * PACK_COLS:(i + 1) * PACK_COLS] for i in range(c // PACK_COLS)], axis=0)
    elif c < PACK_COLS:
        a = jnp.pad(a, ((0, 0), (0, PACK_COLS - c)))
    return jnp.pad(a, ((0, -a.shape[0] % 8), (0, 0)))


def _unrows1024(p, r, c):
    if c > PACK_COLS:
        return jnp.concatenate([p[i * r:(i + 1) * r] for i in range(c // PACK_COLS)], axis=1)
    return p[:r, :c]


def _lane_select(a, off, sign, n_out, out_dtype, exact, name):
    rows, n_in = a.shape
    tr = _tile(rows, 256, 16)

    def body(off_ref, a_ref, o_ref):
        sel = _iota2((n_in, n_out), 0) + off_ref[0] * sign == _iota2((n_in, n_out), 1)
        if exact:
            r = jnp.dot(a_ref[...], sel.astype(F32), precision=lax.Precision.HIGHEST, preferred_element_type=F32)
        else:
            r = _dg(a_ref[...], sel.astype(BF16), 1, 0)
        o_ref[...] = r.astype(out_dtype)

    return pl.pallas_call(
        body, name=name, grid=(rows // tr,),
        in_specs=[pl.BlockSpec(memory_space=pltpu.SMEM), pl.BlockSpec((tr, n_in), lambda i: (i, 0))],
        out_specs=pl.BlockSpec((tr, n_out), lambda i: (i, 0)),
        out_shape=jax.ShapeDtypeStruct((rows, n_out), out_dtype),
        compiler_params=_params(("arbitrary",)),
    )(off, a)


def _adamw(w, g, m, v, name):
    rows, cols = w.shape
    tr = _tile(rows, 256, 8) if rows % 8 == 0 else rows

    def fn(i, w_, g_, m_, v_):
        m_new = ADAM_B1 * m_ + (1.0 - ADAM_B1) * g_
        v_new = ADAM_B2 * v_ + (1.0 - ADAM_B2) * (g_ * g_)
        m_hat = m_new / (1.0 - ADAM_B1 ** ADAM_STEP)
        v_hat = v_new / (1.0 - ADAM_B2 ** ADAM_STEP)
        delta = -ADAM_LR * (m_hat / (jnp.sqrt(v_hat) + ADAM_EPS) + ADAM_WD * w_)
        return (delta, m_new, v_new), ()

    outs, _ = _rows(fn, [_win(w), _win(g), _win(m), _win(v)], [], [(cols, F32)] * 3, [], name=name,
                    nrow=rows, tr=tr)
    return outs


def _as2d(a):
    if a.ndim == 1:
        return a.reshape(1, -1)
    if a.ndim == 2:
        return a
    a = a.reshape(a.shape[1:])
    return a if a.ndim == 2 else a.reshape(a.shape[0], -1)


def kernel(x, meta, norm_ab_w, w_in_ab, ret_norm_w, s5_lam_re, s5_lam_im, s5_log_dt, s5_b_re, s5_b_im, s5_c_re, s5_c_im, s5_d, s5_w_glu, w_out_ab, norm_c_w, w_in_c, gla_w_gate, gla_b_gate, gla_norm_w, w_out_c, final_norm_w, loss_target, m_meta, m_norm_ab_w, m_w_in_ab, m_ret_norm_w, m_s5_lam_re, m_s5_lam_im, m_s5_log_dt, m_s5_b_re, m_s5_b_im, m_s5_c_re, m_s5_c_im, m_s5_d, m_s5_w_glu, m_w_out_ab, m_norm_c_w, m_w_in_c, m_gla_w_gate, m_gla_b_gate, m_gla_norm_w, m_w_out_c, m_final_norm_w, v_meta, v_norm_ab_w, v_w_in_ab, v_ret_norm_w, v_s5_lam_re, v_s5_lam_im, v_s5_log_dt, v_s5_b_re, v_s5_b_im, v_s5_c_re, v_s5_c_im, v_s5_d, v_s5_w_glu, v_w_out_ab, v_norm_c_w, v_w_in_c, v_gla_w_gate, v_gla_b_gate, v_gla_norm_w, v_w_out_c, v_final_norm_w):
    weights = dict(meta=meta, norm_ab_w=norm_ab_w, w_in_ab=w_in_ab, ret_norm_w=ret_norm_w, s5_lam_re=s5_lam_re,
                   s5_lam_im=s5_lam_im, s5_log_dt=s5_log_dt, s5_b_re=s5_b_re, s5_b_im=s5_b_im, s5_c_re=s5_c_re,
                   s5_c_im=s5_c_im, s5_d=s5_d, s5_w_glu=s5_w_glu, w_out_ab=w_out_ab, norm_c_w=norm_c_w,
                   w_in_c=w_in_c, gla_w_gate=gla_w_gate, gla_b_gate=gla_b_gate, gla_norm_w=gla_norm_w,
                   w_out_c=w_out_c, final_norm_w=final_norm_w)
    mom_m = dict(meta=m_meta, norm_ab_w=m_norm_ab_w, w_in_ab=m_w_in_ab, ret_norm_w=m_ret_norm_w,
                 s5_lam_re=m_s5_lam_re, s5_lam_im=m_s5_lam_im, s5_log_dt=m_s5_log_dt, s5_b_re=m_s5_b_re,
                 s5_b_im=m_s5_b_im, s5_c_re=m_s5_c_re, s5_c_im=m_s5_c_im, s5_d=m_s5_d, s5_w_glu=m_s5_w_glu,
                 w_out_ab=m_w_out_ab, norm_c_w=m_norm_c_w, w_in_c=m_w_in_c, gla_w_gate=m_gla_w_gate,
                 gla_b_gate=m_gla_b_gate, gla_norm_w=m_gla_norm_w, w_out_c=m_w_out_c, final_norm_w=m_final_norm_w)
    mom_v = dict(meta=v_meta, norm_ab_w=v_norm_ab_w, w_in_ab=v_w_in_ab, ret_norm_w=v_ret_norm_w,
                 s5_lam_re=v_s5_lam_re, s5_lam_im=v_s5_lam_im, s5_log_dt=v_s5_log_dt, s5_b_re=v_s5_b_re,
                 s5_b_im=v_s5_b_im, s5_c_re=v_s5_c_re, s5_c_im=v_s5_c_im, s5_d=v_s5_d, s5_w_glu=v_s5_w_glu,
                 w_out_ab=v_w_out_ab, norm_c_w=v_norm_c_w, w_in_c=v_w_in_c, gla_w_gate=v_gla_w_gate,
                 gla_b_gate=v_gla_b_gate, gla_norm_w=v_gla_norm_w, w_out_c=v_w_out_c, final_norm_w=v_final_norm_w)
    order = list(weights)

    seq = x.shape[1]
    lp = CHUNK + seq
    nchunk = lp // CHUNK
    dev = 4 * lax.axis_index("x") + 2 * lax.axis_index("y") + lax.axis_index("c")
    core = lax.axis_index("c")
    chip = 2 * lax.axis_index("x") + lax.axis_index("y")

    win_off = jnp.reshape(2 * dev, (1,)).astype(jnp.int32)
    shard_c = jnp.pad(w_in_c[0].astype(BF16), ((0, 0), (0, 896 - SHARD_C)))
    big_shards = dict(w_in_ab=w_in_ab[0].astype(BF16), s5_w_glu=s5_w_glu[0].astype(BF16),
                      w_out_ab=w_out_ab[0].astype(BF16), w_out_c=w_out_c[0].astype(BF16),
                      w_in_c=_lane_select(shard_c, win_off, 1, WIN_COLS, BF16, False, "w_in_c_to_window"))
    def pad_to(a, rows, cols):
        return jnp.pad(a, ((0, rows - a.shape[0]), (0, cols - a.shape[1])))

    shard_w = D_MODEL // N_DEV
    small_pack = jnp.concatenate([meta, pad_to(norm_c_w, 8, shard_w), pad_to(gla_w_gate[0], GLA_RANK, shard_w),
                                  pad_to(gla_b_gate, 8, shard_w), pad_to(gla_norm_w, 8, shard_w)], axis=0)
    w_in_ab_g = _all_gather(big_shards["w_in_ab"], "gather_first")
    win_cut = 1408
    in_c_hook_a = _gather_hook(big_shards["w_in_c"][:win_cut])
    in_c_hook_b = _gather_hook(big_shards["w_in_c"][win_cut:])
    glu_ab_hook = _gather_hook(_pack_big(big_shards, GLU_AB_LAYOUT))
    out_c_hook = _gather_hook(_to_rows(big_shards["w_out_c"]))
    gs = _all_gather(small_pack, "gather_small")
    gate_w = GLA_QK // N_DEV
    s_meta, s_norm_c = gs[:, :N_META], gs[:, N_META]
    s_wgate, s_bgate, s_gnorm = gs[:, 24:24 + GLA_RANK, :gate_w], gs[:, 40, :gate_w], gs[:, 48]
    meta_f = s_meta.transpose(1, 0, 2).reshape(N_META, D_MODEL)
    norm_c_f = s_norm_c.reshape(1, D_MODEL)
    w_gate_f = jnp.pad(s_wgate.transpose(1, 0, 2).reshape(GLA_RANK, GLA_QK), ((0, GATE_PAD - GLA_RANK), (0, 0)))
    b_gate_f = s_bgate.reshape(1, GLA_QK)
    gla_norm_f = s_gnorm.reshape(GLA_H, 1, GLA_DV)

    pos = jnp.maximum(jnp.arange(lp, dtype=F32) - float(PAD), 0.0)
    inv_freq = jnp.power(ROPE_BASE, -jnp.arange(0, RET_DK, 2, dtype=F32) / RET_DK)
    ang = pos[:, None] * inv_freq[None, :]
    cos2 = jnp.concatenate([jnp.cos(ang), jnp.cos(ang)], axis=1)
    sin2 = jnp.concatenate([-jnp.sin(ang), jnp.sin(ang)], axis=1)
    log_g = jnp.log1p(-jnp.exp2(-5.0 - jnp.arange(RET_H, dtype=F32)))
    lg = jnp.broadcast_to(log_g[:, None, None], (RET_H, 1, 128))
    ret_norm_h = ret_norm_w.reshape(RET_H, 1, RET_DV)

    h0 = jnp.concatenate([jnp.zeros((PAD, D_MODEL), F32), meta_f, x[0]], axis=0)

    def rowmask(i):
        return (_iota2((CHUNK, 1), 0) + i * CHUNK) >= PAD

    (hn0,), _ = _rows(lambda i, h, w: ((_rms(h, w),), ()), [_win(h0)], [norm_ab_w], [(D_MODEL, BF16)], [],
                      name="norm_ab_fwd", nrow=lp)
    proj_ab, (w_in_c_ga,) = _mm(hn0, w_in_ab_g, "nn", name="in_ab_fwd", hook=in_c_hook_a, b_dev=True)

    q_off, k_off, v_off, za_off = 0, RET_QK, 2 * RET_QK, 2 * RET_QK + RET_W
    u_off, zb_off = 2 * RET_QK + 2 * RET_W, 2 * RET_QK + 2 * RET_W + S5_W
    ret_xs = [(proj_ab, RET_DK, lambda h: q_off // RET_DK + h), (proj_ab, RET_DK, lambda h: k_off // RET_DK + h),
              (proj_ab, RET_DV, lambda h: v_off // RET_DV + h), (proj_ab, RET_DV, lambda h: za_off // RET_DV + h)]
    ret_cs = [(cos2, RET_DK, lambda h: 0), (sin2, RET_DK, lambda h: 0)]
    ret_kw = dict(heads=RET_H, nchunk=nchunk, s_shape=(RET_DK, RET_DV), out_w=RET_DV, pre=_ret_pre, hpb=4)
    o_a, ret_sprev, (gathered_glu_ab,) = _scan_fwd(_ret_chunk, ret_xs, ret_cs, [ret_norm_h], [lg], name="ret_fwd",
                                                   hook=glu_ab_hook, **ret_kw)
    gb = _unpack_big(gathered_glu_ab, GLU_AB_LAYOUT)
    w_glu_f = gb["s5_w_glu"].reshape(S5_W, S5_W)
    w_out_ab_f = gb["w_out_ab"].reshape(OUT_AB, D_MODEL)

    expand = jnp.repeat(jnp.eye(S5_P, dtype=F32), S5_GH, axis=1)
    disc_args = (s5_lam_re[0], s5_lam_im[0], s5_log_dt[0].reshape(S5_G, 1),
                 s5_b_re[0].reshape(S5_G, S5_P * S5_GH), s5_b_im[0].reshape(S5_G, S5_P * S5_GH), expand)
    ab_re, ab_im, bb_re, bb_im = _s5_disc_fwd(disc_args)
    gt = TILE_G
    eye_t = jnp.eye(gt, dtype=F32)

    def tiles_in(bb):
        return jnp.einsum("sgph,gk->sghkp", bb.reshape(gt, gt, S5_P, S5_GH), eye_t).reshape(gt, 128, TILE_W)

    def tiles_out(cc):
        return jnp.einsum("sghp,gk->sgpkh", cc.reshape(gt, gt, S5_GH, S5_P), eye_t).reshape(gt, TILE_W, 128)

    wb_t = jnp.concatenate([tiles_in(bb_re), tiles_in(bb_im)], axis=0).astype(BF16)
    wc_t = jnp.concatenate([tiles_out(s5_c_re[0]), -tiles_out(s5_c_im[0])], axis=0).astype(BF16)
    a_re, a_im = ab_re.reshape(SCAN_ROWS, SCAN_LANES), ab_im.reshape(SCAN_ROWS, SCAN_LANES)
    tm5, tk5, nt5 = _tile(lp, 1408, 8), _tile(lp, 1408, 8), 2 * gt
    u_blk = u_off // 128
    wide_k = pl.BlockSpec((tm5, TILE_W), lambda i, j, k: (i, k * gt + j))
    narrow = pl.BlockSpec((tm5, 128), lambda i, j, k: (i, j))
    wb_k = pl.BlockSpec((None, 128, TILE_W), lambda i, j, k: (k * gt + j, 0, 0))
    wc_k = pl.BlockSpec((None, TILE_W, 128), lambda i, j, k: (k * gt + j, 0, 0))
    bu3, (w_out_c_g,) = _s5_expand(proj_ab, wb_t, a_blk=u_off // S5_W, dims=NN, name="s5_bu", hook=out_c_hook)
    w_out_c_f = _from_rows(w_out_c_g, D_MODEL).reshape(GLA_W, D_MODEL)
    xs5 = _s5_scan_fwd(bu3, a_re, a_im)
    xs5_2d = xs5.reshape(lp, 2 * S5_N)
    y_pre, (w_in_c_gb,) = _mm_core(xs5_2d, wc_t, dims=NN, grid=(lp // tm5, gt, 2), name="s5_cx", a_spec=wide_k,
                                   b_spec=wc_k, o_spec=narrow, out_shape=jax.ShapeDtypeStruct((lp, S5_W), F32),
                                   acc_shape=(tm5, 128), hook=in_c_hook_b)
    (y_s5, yg_bf), _ = _rows(
        lambda i, yp, u, d: ((yp + d * u, _gelu(yp + d * u)), ()),
        [_win(y_pre), _win(proj_ab, u_off, S5_W)], [s5_d], [(S5_W, F32), (S5_W, BF16)], [], name="s5_gelu_fwd", nrow=lp)
    t_glu = _mm(yg_bf, w_glu_f, "nn", name="s5_glu_fwd")

    def s5_gate(y, t, zb):
        return _gelu(y) * _sigmoid(t) * _silu(zb)

    (o_b,), _ = _rows(lambda i, y, t, zb: ((s5_gate(y, t, zb),), ()),
                      [_win(y_s5), _win(t_glu), _win(proj_ab, zb_off, S5_W)], [], [(S5_W, BF16)], [],
                      name="s5_gate_fwd", nrow=lp)
    o_ab = jnp.concatenate([o_a, o_b], axis=1)
    h1 = _mm(o_ab, w_out_ab_f, "nn", name="out_ab_fwd", add=h0)
    w_in_c_g = jnp.concatenate([w_in_c_ga, w_in_c_gb], axis=1)
    w_in_c_f = sum(jnp.pad(w_in_c_g[d], ((0, 0), (WIN_STEP * d, IN_C_PAD - WIN_STEP * d - WIN_COLS)))
                   for d in range(N_DEV))

    (hn1,), _ = _rows(lambda i, h, w: ((_rms(h, w),), ()), [_win(h1)], [norm_c_f], [(D_MODEL, BF16)], [],
                      name="norm_c_fwd", nrow=lp)
    proj_c = _mm(hn1, w_in_c_f, "nn", name="in_c_fwd")
    gl_off = 2 * GLA_QK + 2 * GLA_W
    pre_gate = _mm(proj_c, w_gate_f, "nn", name="gate_fwd", a_win=(gl_off, GATE_PAD), bias=b_gate_f)
    gla_xs = [(proj_c, GLA_DK, lambda h: h), (proj_c, GLA_DK, lambda h: GLA_QK // GLA_DK + h),
              (proj_c, GLA_DV, lambda h: 2 * GLA_QK // GLA_DV + h),
              (proj_c, GLA_DV, lambda h: (2 * GLA_QK + GLA_W) // GLA_DV + h),
              (pre_gate, GLA_DK, lambda h: h)]
    gla_kw = dict(heads=GLA_H, nchunk=nchunk, s_shape=(GLA_DV, GLA_DK), out_w=GLA_DV, hpb=GLA_H)
    o_c, gla_sprev = _scan_fwd(functools.partial(_gla_chunk, cumsum=_cumsum_rows_f32), gla_xs, [], [gla_norm_f], [],
                               name="gla_fwd", **gla_kw)
    h2 = _mm(o_c, w_out_c_f, "nn", name="out_c_fwd", add=h1)

    fnw = final_norm_w.reshape(1, D_MODEL)

    def final_fn(i, h, tgt, w):
        def loss_of(h_, w_):
            err = _rms(h_, w_) - tgt
            return 0.5 * jnp.sum(jnp.mean(err * err, axis=-1))

        real = (i > 0).astype(F32)
        loss_i, (dh, dw) = jax.value_and_grad(loss_of, argnums=(0, 1))(h, w)
        return (dh * real, dh * real), (jnp.full((1, 128), loss_i * real, F32), dw * real)

    (dh2, dh2_bf), (loss_acc, g_final) = _rows(
        final_fn, [_win(h2), _win(loss_target[0], roff=1)], [fnw], [(D_MODEL, F32), (D_MODEL, BF16)],
        [(1, 128), (1, D_MODEL)], name="final_loss", nrow=lp)

    def rs_front(pieces, layout, tag):
        g_full = _pack_big(pieces, layout)
        prow = g_full.shape[1]
        from_sibling = _swap_with_sibling(g_full, "rs_sibling_" + tag)
        mine_by_chip = lax.dynamic_index_in_dim(g_full.reshape(4, 2, prow, PACK_COLS), core, axis=1, keepdims=False)
        (p1, p1_bf), _ = _rows(
            lambda i, a, b: ((a.astype(F32) + b.astype(F32), a.astype(F32) + b.astype(F32)), ()),
            [_win(mine_by_chip.reshape(4 * prow, PACK_COLS)), _win(from_sibling.reshape(4 * prow, PACK_COLS))], [],
            [(PACK_COLS, F32), (PACK_COLS, BF16)], [], name="rs_sum_sibling_" + tag, nrow=4 * prow,
            tr=_tile(prow, 512, 16))
        return p1.reshape(4, prow, PACK_COLS), p1_bf.reshape(4, prow, PACK_COLS)

    def rs_back(p1, from_chips, layout, tag):
        prow = p1.shape[1]
        tr = _tile(prow, 512, 16)
        own = lax.dynamic_index_in_dim(p1, chip, axis=0, keepdims=False)
        fc2 = from_chips.reshape(3 * prow, PACK_COLS)
        nblk = prow // tr
        (g_shard,), _ = _rows(
            lambda i, a, b0, b1, b2: ((((a + b0.astype(F32)) + b1.astype(F32)) + b2.astype(F32),), ()),
            [_win(own), _win(fc2), _win(fc2, roff=-nblk), _win(fc2, roff=-2 * nblk)], [], [(PACK_COLS, F32)], [],
            name="rs_sum_chips_" + tag, nrow=prow, tr=tr)
        return _unpack_big(g_shard, layout)

    do_c = _mm(dh2_bf, w_out_c_f, "nt", name="out_c_dx", out_dtype=BF16)
    gw_out_c = _mm(o_c, dh2_bf, "tn", name="out_c_dw", out_dtype=BF16)
    (dq_c, dk_c, dv_c, dz_c, dpre), (g_gla_norm,) = _scan_bwd(
        _gla_chunk, gla_xs, [], [gla_norm_f], [], do_c, gla_sprev, name="gla_bwd", **gla_kw)
    dglow = _mm(dpre, w_gate_f, "nt", name="gate_dx", out_dtype=BF16)
    g_wgate = _mm(proj_c, dpre, "tn", name="gate_dw", a_win=(gl_off, GATE_PAD))[:GLA_RANK]
    (), (g_bgate,) = _rows(lambda i, d: ((), (jnp.sum(d.astype(F32), axis=0, keepdims=True),)), [_win(dpre)], [], [],
                           [(1, GLA_QK)], name="gate_db", nrow=lp)
    dproj_c = jnp.concatenate([dq_c, dk_c, dv_c, dz_c, dglow], axis=1)
    dhn1 = _mm(dproj_c, w_in_c_f, "nt", name="in_c_dx")
    gw_in_c = _mm(hn1, dproj_c, "tn", name="in_c_dw", out_dtype=BF16)
    p1_c, p1_c_bf = rs_front(dict(
        w_in_c=_column_windows(gw_in_c)),
        IN_C_LAYOUT, "in_c")

    def norm_bwd(i, h, dhn, dres, w):
        _, vjp = jax.vjp(_rms, h, w)
        dh, dw = vjp(dhn)
        return (jnp.where(rowmask(i), dh + dres, 0.0),), (dw,)

    def norm_bwd_both(i, h, dhn, dres, w):
        (dh,), acc = norm_bwd(i, h, dhn, dres, w)
        return (dh, dh), acc

    (dh1, dh1_bf), (g_norm_c,) = _rows(norm_bwd_both, [_win(h1), _win(dhn1), _win(dh2)], [norm_c_f],
                                       [(D_MODEL, F32), (D_MODEL, BF16)], [(1, D_MODEL)], name="norm_c_bwd", nrow=lp)

    do_ab = _mm(dh1_bf, w_out_ab_f, "nt", name="out_ab_dx", out_dtype=BF16)
    gw_out_ab = _mm(o_ab, dh1_bf, "tn", name="out_ab_dw", out_dtype=BF16)

    def s5_gate_bwd(i, dob, y, t, zb):
        _, vjp = jax.vjp(s5_gate, y, t, zb)
        dy, dt, dzb = vjp(dob.astype(F32))
        return (dy, dt, dzb), ()

    (dy_a, dt_glu, dzb), _ = _rows(
        s5_gate_bwd, [_win(do_ab, RET_W, S5_W), _win(y_s5), _win(t_glu), _win(proj_ab, zb_off, S5_W)], [],
        [(S5_W, F32), (S5_W, BF16), (S5_W, BF16)], [], name="s5_gate_bwd", nrow=lp)
    dyg2 = _mm(dt_glu, w_glu_f, "nt", name="s5_glu_dx")
    gw_glu = _mm(yg_bf, dt_glu, "tn", name="s5_glu_dw", out_dtype=BF16)

    def s5_y_bwd(i, dya, dyg, y, u, d):
        _, vjp = jax.vjp(_gelu, y)
        (dy_g,) = vjp(dyg)
        dy = dya + dy_g
        return (dy, d * dy), (jnp.sum(dy * u, axis=0, keepdims=True),)

    (dy_s5, du1), (g_d,) = _rows(
        s5_y_bwd, [_win(dy_a), _win(dyg2), _win(y_s5), _win(proj_ab, u_off, S5_W)], [s5_d],
        [(S5_W, BF16), (S5_W, F32)], [(1, S5_W)], name="s5_y_bwd", nrow=lp)
    p1_o, p1_o_bf = rs_front(dict(s5_w_glu=gw_glu.reshape(N_DEV, S5_W // N_DEV, S5_W),
                                  w_out_ab=gw_out_ab.reshape(N_DEV, OUT_AB // N_DEV, D_MODEL),
                                  w_out_c=gw_out_c.reshape(N_DEV, GLA_W // N_DEV, D_MODEL)), OTHER_LAYOUT, "other")
    o_cut = 640
    gx3, (from_chips_oa,) = _s5_expand(dy_s5, wc_t, a_blk=0, dims=NT, name="s5_cx_dx",
                                       hook=_chips_hook(p1_o_bf, 0, o_cut))
    rows_k = lambda col: pl.BlockSpec((tk5, col), lambda i, j, k: (k, i))
    gwc = _mm_core(xs5_2d, dy_s5, dims=TN, grid=(nt5, 1, lp // tk5), name="s5_cx_dw", a_spec=rows_k(TILE_W),
                   b_spec=pl.BlockSpec((tk5, 128), lambda i, j, k: (k, i % gt)),
                   o_spec=pl.BlockSpec((None, TILE_W, 128), lambda i, j, k: (i, 0, 0)),
                   out_shape=jax.ShapeDtypeStruct((nt5, TILE_W, 128), F32), acc_shape=(TILE_W, 128))
    g_s5, da = _s5_scan_bwd(gx3, xs5, a_re, a_im)
    g_s5_2d = g_s5.reshape(lp, 2 * S5_N)
    du, (from_chips_ob,) = _mm_core(g_s5_2d, wb_t, dims=NT, grid=(lp // tm5, gt, 2), name="s5_bu_dx", a_spec=wide_k,
                                    b_spec=wb_k, o_spec=narrow, out_shape=jax.ShapeDtypeStruct((lp, S5_W), BF16),
                                    acc_shape=(tm5, 128), extra=[(du1, narrow)],
                                    hook=_chips_hook(p1_o_bf, o_cut, None))
    from_chips_o = jnp.concatenate([from_chips_oa, from_chips_ob], axis=1)
    gwb = _mm_core(proj_ab, g_s5_2d, dims=TN, grid=(nt5, 1, lp // tk5), name="s5_bu_dw",
                   a_spec=pl.BlockSpec((tk5, 128), lambda i, j, k: (k, u_blk + i % gt)), b_spec=rows_k(TILE_W),
                   o_spec=pl.BlockSpec((None, 128, TILE_W), lambda i, j, k: (i, 0, 0)),
                   out_shape=jax.ShapeDtypeStruct((nt5, 128, TILE_W), F32), acc_shape=(128, TILE_W))
    gwc6 = gwc.reshape(2, gt, gt, S5_P, gt, S5_GH)
    g_c = jnp.einsum("rsgpgh->rsghp", gwc6).reshape(2, S5_G, S5_GH, S5_P)
    g_c_re, g_c_im = g_c[0], -g_c[1]
    gwb6 = gwb.reshape(2, gt, gt, S5_GH, gt, S5_P)
    d_bb = jnp.einsum("rsghgp->rsgph", gwb6).reshape(2, S5_G, S5_P * S5_GH)
    d_bb_re, d_bb_im = d_bb[0], d_bb[1]
    g_lam_re, g_lam_im, g_log_dt, g_b_re, g_b_im = _s5_disc_bwd(
        disc_args, (da[0].reshape(S5_G, S5_P), da[1].reshape(S5_G, S5_P), d_bb_re, d_bb_im))

    (dq_a, dk_a, dv_a, dz_a), (g_ret_norm,), (from_chips_c,) = _scan_bwd(
        _ret_chunk, ret_xs, ret_cs, [ret_norm_h], [lg], do_ab, ret_sprev, name="ret_bwd", post=_ret_post,
        hook=_chips_hook(p1_c_bf), **ret_kw)
    dproj_ab = jnp.concatenate([dq_a, dk_a, dv_a, dz_a, du, dzb], axis=1)

    lane = lambda a_: pad_to(a_, a_.shape[0], 128)

    def sum8(i, *blocks):
        acc = blocks[0]
        for b in blocks[1:]:
            acc = acc + b
        return (acc,), ()

    def pack_small(pieces):
        return jnp.concatenate([_rows1024(p) for _, p in pieces], axis=0)

    def sum_small(gathered, pieces, tag):
        srow = gathered.shape[1]
        tr = _tile(srow, 128, 8)
        flat = gathered.reshape(N_DEV * srow, PACK_COLS)
        (total,), _ = _rows(sum8, [_win(flat, roff=-d * (srow // tr)) for d in range(N_DEV)], [], [(PACK_COLS, F32)],
                            [], name="sum_small_" + tag, nrow=srow, tr=tr)
        out, o = {}, 0
        for name_, p in pieces:
            r8 = _rows1024(p).shape[0]
            out[name_] = _unrows1024(total[o:o + r8], *p.shape)
            o += r8
        return out

    early_pieces = [
        ("vec2048", jnp.concatenate([g_final, g_norm_c], axis=0)),
        ("vec1024", jnp.concatenate([g_d, g_bgate, pad_to(loss_acc[:, :1], 1, PACK_COLS)], axis=0)),
        ("lam3", jnp.concatenate([lane(g_lam_re), lane(g_lam_im), lane(g_log_dt)], axis=1)),
        ("s5_b_re", g_b_re), ("s5_b_im", g_b_im),
        ("s5_c_re", g_c_re.reshape(S5_G, S5_GH * S5_P)), ("s5_c_im", g_c_im.reshape(S5_G, S5_GH * S5_P)),
        ("ret_norm_w", g_ret_norm.reshape(RET_H, RET_DV)), ("gla_norm_w", g_gla_norm.reshape(GLA_H, GLA_DV)),
        ("gla_w_gate", g_wgate)]
    gw_in_ab, (early_all,) = _mm(
        hn0, dproj_ab, "tn", name="in_ab_dw", out_dest=True, out_dtype=BF16,
        hook=_gather_hook(pack_small(early_pieces)))
    p1_first, p1_first_bf = rs_front(dict(w_in_ab=gw_in_ab), FIRST_LAYOUT, "first")
    dhn0, (from_chips_first,) = _mm(dproj_ab, w_in_ab_g, "nt", name="in_ab_dx", b_dev=True,
                                    hook=_chips_hook(p1_first_bf))
    def norm_bwd_first(i, h, dhn, dres, w):
        (dh,), (dw,) = norm_bwd(i, h, dhn, dres, w)
        return (dh,), (dw, dh * (i == 0).astype(F32))

    (grad_x2d,), (g_norm_ab, dh0_first) = _rows(
        norm_bwd_first, [_win(h0), _win(dhn0), _win(dh1)], [norm_ab_w], [(D_MODEL, F32, 1)],
        [(1, D_MODEL), (CHUNK, D_MODEL)], name="norm_ab_bwd", nrow=lp)
    grad_x = grad_x2d[None]
    late_pieces = [("norm_ab_w", g_norm_ab), ("meta", dh0_first[PAD:CHUNK])]
    small = sum_small(early_all, early_pieces, "early")
    small.update(sum_small(_all_gather(pack_small(late_pieces), "gather_grads"), late_pieces, "late"))

    big_grads = {**rs_back(p1_c, from_chips_c, IN_C_LAYOUT, "in_c"), **rs_back(p1_o, from_chips_o, OTHER_LAYOUT, "other"),
                 **rs_back(p1_first, from_chips_first, FIRST_LAYOUT, "first")}
    big_grads["w_in_c"] = _lane_select(big_grads["w_in_c"], win_off, -1, 896, F32, True,
                                       "w_in_c_from_window")[:, :SHARD_C]
    small["final_norm_w"], small["norm_c_w"] = small["vec2048"][0:1], small["vec2048"][1:2]
    small["s5_d"], small["gla_b_gate"] = small["vec1024"][0:1], small["vec1024"][1:2]
    loss = small["vec1024"][2, 0]
    small["s5_lam_re"], small["s5_lam_im"] = small["lam3"][:, :S5_P], small["lam3"][:, 128:128 + S5_P]
    small["s5_log_dt"] = small["lam3"][:, 256:257]

    def my_cols(g, n):
        return lax.dynamic_slice_in_dim(g, dev * n, n, axis=g.ndim - 1)

    grads = dict(
        meta=my_cols(small["meta"], D_MODEL // N_DEV),
        norm_ab_w=small["norm_ab_w"], w_in_ab=big_grads["w_in_ab"][None], ret_norm_w=small["ret_norm_w"].reshape(1, RET_W),
        s5_lam_re=small["s5_lam_re"][None], s5_lam_im=small["s5_lam_im"][None],
        s5_log_dt=small["s5_log_dt"].reshape(1, S5_G),
        s5_b_re=small["s5_b_re"].reshape(1, S5_G, S5_P, S5_GH), s5_b_im=small["s5_b_im"].reshape(1, S5_G, S5_P, S5_GH),
        s5_c_re=small["s5_c_re"][None], s5_c_im=small["s5_c_im"][None], s5_d=small["s5_d"],
        s5_w_glu=big_grads["s5_w_glu"][None], w_out_ab=big_grads["w_out_ab"][None],
        norm_c_w=my_cols(small["norm_c_w"], D_MODEL // N_DEV), w_in_c=big_grads["w_in_c"][None],
        gla_w_gate=my_cols(small["gla_w_gate"], GLA_QK // N_DEV)[None],
        gla_b_gate=my_cols(small["gla_b_gate"], GLA_QK // N_DEV),
        gla_norm_w=my_cols(small["gla_norm_w"].reshape(1, GLA_W), GLA_W // N_DEV),
        w_out_c=big_grads["w_out_c"][None], final_norm_w=small["final_norm_w"].reshape(D_MODEL))

    deltas, new_m, new_v = {}, {}, {}
    for k in order:
        w = weights[k]
        d2, m2, v2 = _adamw(_as2d(w), _as2d(grads[k].reshape(w.shape)), _as2d(mom_m[k]), _as2d(mom_v[k]), "adamw_" + k)
        deltas[k], new_m[k], new_v[k] = d2.reshape(w.shape), m2.reshape(w.shape), v2.reshape(w.shape)
        grads[k] = grads[k].reshape(w.shape)

    return (loss, grad_x, *[grads[k] for k in order], *[deltas[k] for k in order],
            *[new_m[k] for k in order], *[new_v[k] for k in order])
```

```python
import functools
import math

import jax
import jax.numpy as jnp
from jax import lax
from jax.experimental import pallas as pl
from jax.experimental.pallas import tpu as pltpu

F32, BF16 = jnp.float32, jnp.bfloat16
MESH = pl.DeviceIdType.MESH
N_DEV = 8

D_MODEL = 2048
CHUNK = 128
N_META = 16
PAD = CHUNK - N_META
SUB = 16
EPS = 1e-6
RET_H, RET_DK, RET_DV = 8, 128, 256
RET_QK, RET_W = RET_H * RET_DK, RET_H * RET_DV
ROPE_BASE = 10000.0
S5_W, S5_G, S5_P, S5_GH = 1024, 64, 64, 16
S5_N = S5_G * S5_P
GLA_H, GLA_DK, GLA_DV, GLA_RANK, GLA_TAU = 4, 256, 512, 16, 16.0
GLA_QK, GLA_W = GLA_H * GLA_DK, GLA_H * GLA_DV
IN_AB = 2 * RET_QK + 2 * RET_W + 2 * S5_W
OUT_AB = RET_W + S5_W
IN_C = 2 * GLA_QK + 2 * GLA_W + GLA_RANK
GATE_PAD = 256
IN_C_PAD = 2 * GLA_QK + 2 * GLA_W + GATE_PAD
ADAM_LR, ADAM_B1, ADAM_B2, ADAM_EPS, ADAM_WD, ADAM_STEP = 0.001, 0.9, 0.999, 1e-08, 0.01, 10

VMEM_LIMIT_BYTES = 48 * 2 ** 20
PACK_COLS = 1024
SHARD_C = IN_C // N_DEV
WIN_STEP = 768
WIN_COLS = 1024


def _params(sem):
    return pltpu.CompilerParams(dimension_semantics=sem, vmem_limit_bytes=VMEM_LIMIT_BYTES)


def _tile(n, cap, mult):
    best = None
    for t in range(mult, min(n, cap) + 1, mult):
        if n % t == 0:
            best = t
    assert best is not None, (n, cap, mult)
    return best


def _dg(a, b, ca, cb):
    return lax.dot_general(a.astype(BF16), b.astype(BF16), (((ca,), (cb,)), ((), ())),
                           preferred_element_type=F32)


@functools.partial(jax.custom_vjp, nondiff_argnums=(2, 3))
def _bdot(a, b, ca, cb):
    return _dg(a, b, ca, cb)


def _bdot_fwd(a, b, ca, cb):
    return _dg(a, b, ca, cb), (a, b)


def _bdot_bwd(ca, cb, res, g):
    a, b = res
    da = _dg(g, b, 1, 1 - cb) if ca == 1 else _dg(b, g, 1 - cb, 1)
    db = _dg(a, g, 1 - ca, 0) if cb == 0 else _dg(g, a, 0, 1 - ca)
    return da.astype(a.dtype), db.astype(b.dtype)


_bdot.defvjp(_bdot_fwd, _bdot_bwd)


def _sigmoid(x):
    return 1.0 / (1.0 + jnp.exp(-x))


def _silu(x):
    return x * _sigmoid(x)


def _log_sigmoid(x):
    return jnp.minimum(x, 0.0) - jnp.log(1.0 + jnp.exp(-jnp.abs(x)))


def _gelu(x):
    return 0.5 * x * (1.0 + jnp.tanh(math.sqrt(2.0 / math.pi) * (x + 0.044715 * (x * x * x))))


def _rms(x, w):
    return x * lax.rsqrt(jnp.mean(x * x, axis=-1, keepdims=True) + EPS) * w


class _Hook:
    def __init__(self, ins, outs, sems, phases):
        self.ins, self.outs, self.sems, self.phases = list(ins), list(outs), list(sems), list(phases)


_NO_HOOK = _Hook([], [], [], [])
_ANY = pl.BlockSpec(memory_space=pl.ANY)


def _run_hook(hook, lin, total, in_refs, out_refs, sem_refs):
    for frac, fn in hook.phases:
        at = min(int(frac * total), total - 1)

        @pl.when(lin == at)
        def _(fn=fn):
            fn(in_refs, out_refs, sem_refs)


def _mm_core(a, b, *, dims, grid, a_spec, b_spec, o_spec, out_shape, acc_shape, name, extra=(), hook=None,
             b_parts=0):
    nk = grid[2]
    n_extra = len(extra)
    hook = _NO_HOOK if hook is None else hook
    hi, ho = len(hook.ins), len(hook.outs)

    def body(*refs):
        a_ref, b_ref = refs[0], refs[1]
        o_ref, acc = refs[2 + n_extra + hi], refs[3 + n_extra + hi + ho]
        k = pl.program_id(2)
        lin = (pl.program_id(0) * grid[1] + pl.program_id(1)) * nk + k
        _run_hook(hook, lin, grid[0] * grid[1] * nk, refs[2 + n_extra:2 + n_extra + hi],
                  refs[3 + n_extra + hi:3 + n_extra + hi + ho], refs[4 + n_extra + hi + ho:])

        if b_parts:
            part = sum(lax.dot_general(a_ref[:, d * PACK_COLS:(d + 1) * PACK_COLS].astype(BF16), b_ref[d].astype(BF16),
                                       dims, preferred_element_type=F32) for d in range(b_parts))
        else:
            part = lax.dot_general(a_ref[...].astype(BF16), b_ref[...].astype(BF16), dims, preferred_element_type=F32)

        def finish(r):
            for e in range(n_extra):
                r = r + refs[2 + e][...].astype(F32)
            o_ref[...] = r.astype(o_ref.dtype)

        if nk == 1:
            finish(part)
        else:
            @pl.when(k == 0)
            def _():
                acc[...] = part

            @pl.when(k > 0)
            def _():
                acc[...] += part

            @pl.when(k == nk - 1)
            def _():
                finish(acc[...])

    res = pl.pallas_call(
        body, name=name, grid=grid,
        in_specs=[a_spec, b_spec] + [sp for _, sp in extra] + [_ANY] * hi,
        out_specs=[o_spec] + [_ANY] * ho, out_shape=[out_shape] + hook.outs,
        scratch_shapes=[pltpu.VMEM(acc_shape if nk > 1 else (8, 128), F32)] + hook.sems,
        compiler_params=_params(("arbitrary", "arbitrary", "arbitrary")),
    )(a, b, *[arr for arr, _ in extra], *hook.ins)
    return res[0] if hook is _NO_HOOK else (res[0], res[1:])


NN, NT, TN = (((1,), (0,)), ((), ())), (((1,), (1,)), ((), ())), (((0,), (0,)), ((), ()))


FULL_K = 2048


def _mm(a, b, mode, *, name, out_dtype=F32, a_win=None, add=None, bias=None, hook=None, b_dev=False,
        out_dest=False):
    b_parts = 0
    if mode == "tn":
        kdim, n = a.shape[0], b.shape[1]
        m = a.shape[1] if a_win is None else a_win[1]
        tm, tn, tk = _tile(m, 512, 128), _tile(n, 640, 128), kdim
        off = 0 if a_win is None else a_win[0] // tm
        a_spec = pl.BlockSpec((tk, tm), lambda i, j, k: (k, i + off))
        b_spec = pl.BlockSpec((tk, tn), lambda i, j, k: (k, j))
        dims = TN
    else:
        m = a.shape[0]
        kdim = a.shape[1] if a_win is None else a_win[1]
        if b_dev:
            n = b.shape[0] * b.shape[2] if mode == "nn" else b.shape[1]
        else:
            n = b.shape[1] if mode == "nn" else b.shape[0]
        if FULL_K < kdim <= 2 * FULL_K and not b_dev:
            tm, tn, tk = _tile(m, 1408, 8), _tile(n, 1024, 128), _tile(kdim, 1024, 128)
        else:
            tm = _tile(m, 1408 if kdim <= FULL_K else 352, 8)
            tn, tk = _tile(n, 640, 128), kdim
        off = 0 if a_win is None else a_win[0] // tk
        a_spec = pl.BlockSpec((tm, tk), lambda i, j, k: (i, k + off))
        if mode == "nn":
            dims = NN
            if b_dev:
                per = PACK_COLS // tn
                b_spec = pl.BlockSpec((None, tk, tn), lambda i, j, k: (j // per, k, j % per))
            else:
                b_spec = pl.BlockSpec((tk, tn), lambda i, j, k: (k, j))
        else:
            dims = NT
            if b_dev:
                b_parts = kdim // PACK_COLS
                b_spec = pl.BlockSpec((b_parts, tn, PACK_COLS), lambda i, j, k: (0, j, 0))
            else:
                b_spec = pl.BlockSpec((tn, tk), lambda i, j, k: (j, k))
    if a_win is not None:
        assert a_win[0] % (tm if mode == "tn" else tk) == 0
    extra = []
    if add is not None:
        extra.append((add, pl.BlockSpec((tm, tn), lambda i, j, k: (i, j))))
    if bias is not None:
        extra.append((bias, pl.BlockSpec((1, tn), lambda i, j, k: (0, j))))
    if out_dest:
        per = PACK_COLS // tn
        o_spec = pl.BlockSpec((None, tm, tn), lambda i, j, k: (j // per, i, j % per))
        out_shape = jax.ShapeDtypeStruct((n // PACK_COLS, m, PACK_COLS), out_dtype)
    else:
        o_spec = pl.BlockSpec((tm, tn), lambda i, j, k: (i, j))
        out_shape = jax.ShapeDtypeStruct((m, n), out_dtype)
    return _mm_core(a, b, dims=dims, grid=(m // tm, n // tn, kdim // tk), a_spec=a_spec, b_spec=b_spec,
                    o_spec=o_spec, out_shape=out_shape, acc_shape=(tm, tn), name=name, extra=extra, hook=hook,
                    b_parts=b_parts)


def _win(arr, col0=0, width=None, roff=0):
    return (arr, col0, arr.shape[1] if width is None else width, roff)


def _rows(fn, rows, consts, outs, accs, *, name, nrow, tr=CHUNK):
    nr, nc, no = len(rows), len(consts), len(outs)

    def body(*refs):
        i = pl.program_id(0)
        ins = [r[...] for r in refs[:nr + nc]]
        o_refs = refs[nr + nc:nr + nc + no]
        a_refs = refs[nr + nc + no:]
        res_o, res_a = fn(i, *ins)
        for r, v in zip(o_refs, res_o):
            r[...] = v.astype(r.dtype)
        if a_refs:
            @pl.when(i == 0)
            def _():
                for r in a_refs:
                    r[...] = jnp.zeros_like(r)

            for r, v in zip(a_refs, res_a):
                r[...] += v

    in_specs = []
    for (arr, col0, width, roff) in rows:
        assert col0 % width == 0 and arr.shape[0] % tr == 0
        in_specs.append(pl.BlockSpec((tr, width), lambda i, c=col0 // width, ro=roff: (jnp.maximum(i - ro, 0), c)))
    for c in consts:
        in_specs.append(pl.BlockSpec(c.shape, lambda i, nd=c.ndim: (0,) * nd))
    outs = [tuple(o) + (0,) * (3 - len(o)) for o in outs]
    out_specs = [pl.BlockSpec((tr, w), lambda i, ro=ro: (jnp.maximum(i - ro, 0), 0)) for (w, _, ro) in outs]
    out_specs += [pl.BlockSpec(s, lambda i, nd=len(s): (0,) * nd) for s in accs]
    out_shape = [jax.ShapeDtypeStruct((nrow - ro * tr, w), dt) for (w, dt, ro) in outs]
    out_shape += [jax.ShapeDtypeStruct(s, F32) for s in accs]
    res = pl.pallas_call(
        body, name=name, grid=(nrow // tr,), in_specs=in_specs, out_specs=out_specs, out_shape=out_shape,
        compiler_params=_params(("arbitrary",)),
    )(*[r[0] for r in rows], *consts)
    return res[:no], res[no:]


HEADS_PER_STEP = 2


def _scan_specs(xs, cs, ws, ks, chunk_of, hpb):
    specs = []
    for (arr, width, colfn) in xs:
        specs.append(pl.BlockSpec((CHUNK, width * hpb), lambda h, n, f=colfn: (chunk_of(n), f(h * hpb) // hpb)))
    for (arr, width, colfn) in cs:
        specs.append(pl.BlockSpec((CHUNK, width), lambda h, n, f=colfn: (chunk_of(n), f(h))))
    for arr in list(ws) + list(ks):
        specs.append(pl.BlockSpec((hpb, 1, arr.shape[2]), lambda h, n: (h, 0, 0)))
    return specs


def _scan_fwd(fn, xs, cs, ws, ks, *, heads, nchunk, s_shape, out_w, name, pre=None, hook=None,
              hpb=HEADS_PER_STEP):
    nx, ncs, nw = len(xs), len(cs), len(ws)
    hook = _NO_HOOK if hook is None else hook
    hi, ho = len(hook.ins), len(hook.outs)
    hblocks = heads // hpb

    def body(*refs):
        n = pl.program_id(1)
        nin = nx + ncs + nw + len(ks)
        y_ref, sp_ref = refs[nin + hi], refs[nin + hi + 1]
        s_scr = refs[nin + hi + 2 + ho]
        _run_hook(hook, pl.program_id(0) * nchunk + n, hblocks * nchunk, refs[nin:nin + hi],
                  refs[nin + hi + 2:nin + hi + 2 + ho], refs[nin + hi + 3 + ho:])

        @pl.when(n == 0)
        def _():
            s_scr[...] = jnp.zeros_like(s_scr)

        cv = [r[...] for r in refs[nx:nx + ncs]]
        for e in range(hpb):
            state = s_scr[e]
            sp_ref[e, 0] = state
            xv = [r[:, e * w:(e + 1) * w] for r, (_, w, _) in zip(refs[:nx], xs)]
            wv = [r[e] for r in refs[nx + ncs:nx + ncs + nw]]
            kv = [r[e] for r in refs[nx + ncs + nw:nin]]
            if pre is not None:
                xv = pre(xv, cv)
            y, s_new = fn(n, xv, state, cv, wv, kv)
            y_ref[:, e * out_w:(e + 1) * out_w] = y.astype(y_ref.dtype)
            s_scr[e] = s_new

    lp = nchunk * CHUNK
    res = pl.pallas_call(
        body, name=name, grid=(hblocks, nchunk),
        in_specs=_scan_specs(xs, cs, ws, ks, lambda n: n, hpb) + [_ANY] * hi,
        out_specs=[pl.BlockSpec((CHUNK, out_w * hpb), lambda h, n: (n, h)),
                   pl.BlockSpec((hpb, 1) + s_shape, lambda h, n: (h, n, 0, 0))] + [_ANY] * ho,
        out_shape=[jax.ShapeDtypeStruct((lp, heads * out_w), BF16),
                   jax.ShapeDtypeStruct((heads, nchunk) + s_shape, F32)] + hook.outs,
        scratch_shapes=[pltpu.VMEM((hpb,) + s_shape, F32)] + hook.sems,
        compiler_params=_params(("arbitrary", "arbitrary")),
    )(*[t[0] for t in xs], *[t[0] for t in cs], *ws, *ks, *hook.ins)
    return (res[0], res[1]) if hook is _NO_HOOK else (res[0], res[1], res[2:])


def _scan_bwd(fn, xs, cs, ws, ks, dy, sprev, *, heads, nchunk, s_shape, out_w, name, pre=None, post=None,
              hook=None, hpb=HEADS_PER_STEP):
    nx, ncs, nw = len(xs), len(cs), len(ws)
    nin = nx + ncs + nw + len(ks)
    hook = _NO_HOOK if hook is None else hook
    hi, ho = len(hook.ins), len(hook.outs)
    hblocks = heads // hpb

    def body(*refs):
        step = pl.program_id(1)
        n = nchunk - 1 - step
        dy_ref, sp_ref = refs[nin], refs[nin + 1]
        o0 = nin + 2 + hi
        dx_refs = refs[o0:o0 + nx]
        dw_refs = refs[o0 + nx:o0 + nx + nw]
        ds_scr = refs[o0 + nx + nw + ho]
        _run_hook(hook, pl.program_id(0) * nchunk + step, hblocks * nchunk, refs[nin + 2:o0],
                  refs[o0 + nx + nw:o0 + nx + nw + ho], refs[o0 + nx + nw + ho + 1:])

        @pl.when(step == 0)
        def _():
            ds_scr[...] = jnp.zeros_like(ds_scr)
            for r in dw_refs:
                r[...] = jnp.zeros_like(r)

        cv = [r[...] for r in refs[nx:nx + ncs]]
        for e in range(hpb):
            xv = [r[:, e * w:(e + 1) * w] for r, (_, w, _) in zip(refs[:nx], xs)]
            wv = [r[e] for r in refs[nx + ncs:nx + ncs + nw]]
            kv = [r[e] for r in refs[nx + ncs + nw:nin]]
            if pre is not None:
                xv = pre(xv, cv)
            _, vjp = jax.vjp(lambda xs_, s_, ws_, kv=kv: fn(n, xs_, s_, cv, ws_, kv), xv, sp_ref[e, 0], wv)
            dxs, ds_prev, dws = vjp((dy_ref[:, e * out_w:(e + 1) * out_w].astype(F32), ds_scr[e]))
            if post is not None:
                dxs = post(dxs, cv)
            for r, v, (_, w, _) in zip(dx_refs, dxs, xs):
                r[:, e * w:(e + 1) * w] = v.astype(r.dtype)
            for r, v in zip(dw_refs, dws):
                r[e] += v
            ds_scr[e] = ds_prev

    lp = nchunk * CHUNK
    rev = lambda n: nchunk - 1 - n
    in_specs = _scan_specs(xs, cs, ws, ks, rev, hpb)
    in_specs.append(pl.BlockSpec((CHUNK, out_w * hpb), lambda h, n: (rev(n), h)))
    in_specs.append(pl.BlockSpec((hpb, 1) + s_shape, lambda h, n: (h, rev(n), 0, 0)))
    out_specs = [pl.BlockSpec((CHUNK, w * hpb), lambda h, n: (rev(n), h)) for (_, w, _) in xs]
    out_specs += [pl.BlockSpec((hpb, 1, w.shape[2]), lambda h, n: (h, 0, 0)) for w in ws]
    out_shape = [jax.ShapeDtypeStruct((lp, heads * w), BF16) for (_, w, _) in xs]
    out_shape += [jax.ShapeDtypeStruct(w.shape, F32) for w in ws]
    res = pl.pallas_call(
        body, name=name, grid=(hblocks, nchunk), in_specs=in_specs + [_ANY] * hi,
        out_specs=out_specs + [_ANY] * ho, out_shape=out_shape + hook.outs,
        scratch_shapes=[pltpu.VMEM((hpb,) + s_shape, F32)] + hook.sems,
        compiler_params=_params(("arbitrary", "arbitrary")),
    )(*[t[0] for t in xs], *[t[0] for t in cs], *ws, *ks, dy, sprev, *hook.ins)
    if hook is _NO_HOOK:
        return res[:nx], res[nx:]
    return res[:nx], res[nx:nx + nw], res[nx + nw:]


def _iota2(shape, dim):
    return lax.broadcasted_iota(jnp.int32, shape, dim)


def _ret_chunk(n, xs, state, cs, ws, ks):
    q, k, v, z = xs
    (w,), (lg,) = ws, ks
    lgc = lg[:, :1]
    row, col = _iota2((CHUNK, CHUNK), 0), _iota2((CHUNK, CHUNK), 1)
    diff = jnp.maximum(row - col, 0).astype(F32)
    decay = jnp.where(row >= col, jnp.exp(lg * diff), 0.0)
    scores = _bdot(q, k, 1, 1) * decay
    o_intra = _bdot(scores, v, 1, 0)
    idx = _iota2((CHUNK, 1), 0).astype(F32)
    k_w = k * jnp.exp(lgc * (CHUNK - 1.0 - idx))
    kv = _bdot(k_w, v, 0, 0)
    s_new = state * jnp.exp(lgc * float(CHUNK)) + kv
    q_w = q * jnp.exp(lgc * (idx + 1.0))
    o = o_intra + _bdot(q_w, state, 1, 0)
    return _rms(o, w) * _silu(z), s_new


def _rope(t, cos2, sin2):
    return t * cos2 + pltpu.roll(t, RET_DK // 2, 1) * sin2


def _rope_t(g, cos2, sin2):
    return g * cos2 - pltpu.roll(g, RET_DK // 2, 1) * sin2


def _ret_pre(xv, cv):
    q, k, v, z = xv
    cos2, sin2 = cv
    return [_rope(q, cos2, sin2), _rope(k, cos2, sin2) * (RET_DK ** -0.5), v, z]


def _ret_post(dxs, cv):
    dq, dk, dv, dz = dxs
    cos2, sin2 = cv
    return [_rope_t(dq, cos2, sin2), _rope_t(dk, cos2, sin2) * (RET_DK ** -0.5), dv, dz]


def _tri_apply(x, lower):
    n = x.shape[0]
    row, col = _iota2((n, n), 0), _iota2((n, n), 1)
    tri = (row >= col if lower else row <= col).astype(BF16)
    hi = x.astype(BF16)
    rest = x - hi.astype(F32)
    mid = rest.astype(BF16)
    lo = (rest - mid.astype(F32)).astype(BF16)
    return sum(lax.dot_general(tri, p, NN, preferred_element_type=F32) for p in (hi, mid, lo))


@jax.custom_vjp
def _cumsum_rows(x):
    return _tri_apply(x, True)


_cumsum_rows.defvjp(lambda x: (_tri_apply(x, True), None), lambda _, g: (_tri_apply(g, False),))


def _cumsum_rows_f32(x):
    n = x.shape[0]
    tri = (_iota2((n, n), 0) >= _iota2((n, n), 1)).astype(F32)
    return jnp.dot(tri, x, precision=lax.Precision.HIGHEST, preferred_element_type=F32)


def _gla_chunk(n, xs, state_t, cs, ws, ks, cumsum=_cumsum_rows):
    q, k, v, z, pre = xs
    (w,) = ws
    q = q * (GLA_DK ** -0.5)
    rowc = _iota2((CHUNK, 1), 0)
    valid = jnp.logical_or(n > 0, rowc >= PAD)
    log_a = jnp.where(valid, _log_sigmoid(pre) / GLA_TAU, 0.0)
    b = cumsum(log_a)
    b_last = b[CHUNK - 1:CHUNK, :]
    kv_t = _bdot(v, k * jnp.exp(b_last - b), 0, 0)
    s_new = state_t * jnp.exp(b_last) + kv_t
    o_inter = _bdot(q * jnp.exp(b), state_t, 1, 1)
    outs = []
    for s in range(CHUNK // SUB):
        lo, hi = s * SUB, (s + 1) * SUB
        b_ref = jnp.zeros_like(b_last) if s == 0 else b[lo - 1:lo, :]
        q_hat = q[lo:hi] * jnp.exp(b[lo:hi] - b_ref)
        k_hat = k[:hi] * jnp.exp(b_ref - b[:hi])
        sc = _bdot(q_hat, k_hat, 1, 1)
        causal = _iota2((SUB, hi), 0) + lo >= _iota2((SUB, hi), 1)
        outs.append(_bdot(jnp.where(causal, sc, 0.0), v[:hi], 1, 0))
    o = jnp.concatenate(outs, axis=0) + o_inter
    return _rms(o, w) * _silu(z), s_new


def _s5_disc(lam_re, lam_im, log_dt, b_re, b_im, expand):
    dt = jnp.exp(log_dt)
    mag = jnp.exp(lam_re * dt)
    ab_re, ab_im = mag * jnp.cos(lam_im * dt), mag * jnp.sin(lam_im * dt)
    den = lam_re * lam_re + lam_im * lam_im
    nr, ni = ab_re - 1.0, ab_im
    f_re = (nr * lam_re + ni * lam_im) / den
    f_im = (ni * lam_re - nr * lam_im) / den
    hp = lax.Precision.HIGHEST
    f_re = jnp.dot(f_re, expand, precision=hp, preferred_element_type=F32)
    f_im = jnp.dot(f_im, expand, precision=hp, preferred_element_type=F32)
    return ab_re, ab_im, f_re * b_re - f_im * b_im, f_re * b_im + f_im * b_re


def _s5_disc_fwd(args):
    def body(*refs):
        outs = _s5_disc(*[r[...] for r in refs[:6]])
        for r, v in zip(refs[6:], outs):
            r[...] = v

    g, p = args[0].shape
    return pl.pallas_call(
        body, name="s5_disc_fwd",
        out_shape=[jax.ShapeDtypeStruct((g, p), F32)] * 2 + [jax.ShapeDtypeStruct(args[3].shape, F32)] * 2,
    )(*args)


def _s5_disc_bwd(args, cts):
    def body(*refs):
        prim = [r[...] for r in refs[:5]]
        expand = refs[5][...]
        ct = tuple(r[...] for r in refs[6:10])
        _, vjp = jax.vjp(lambda *a: _s5_disc(*a, expand), *prim)
        for r, v in zip(refs[10:], vjp(ct)):
            r[...] = v

    return pl.pallas_call(
        body, name="s5_disc_bwd", out_shape=[jax.ShapeDtypeStruct(a.shape, F32) for a in args[:5]],
    )(*args, *cts)


SCAN_ROWS, SCAN_LANES = 32, 128
TILE_G = 8
TILE_W = TILE_G * S5_P
S5_TB = 64


def _s5_scan_fwd(bu, a_re, a_im):
    lp = bu.shape[0]

    def body(bu_ref, ar_ref, ai_ref, x_ref, st):
        @pl.when(pl.program_id(0) == 0)
        def _():
            st[...] = jnp.zeros_like(st)

        ar, ai = ar_ref[...], ai_ref[...]

        def step(t, carry):
            xr, xi = carry
            nr = ar * xr - ai * xi + bu_ref[t, 0:SCAN_ROWS, :]
            ni = ar * xi + ai * xr + bu_ref[t, SCAN_ROWS:2 * SCAN_ROWS, :]
            x_ref[t, 0:SCAN_ROWS, :] = nr
            x_ref[t, SCAN_ROWS:2 * SCAN_ROWS, :] = ni
            return nr, ni

        xr, xi = lax.fori_loop(0, S5_TB, step, (st[0], st[1]))
        st[0] = xr
        st[1] = xi

    blk = pl.BlockSpec((S5_TB, 2 * SCAN_ROWS, SCAN_LANES), lambda i: (i, 0, 0))
    cst = pl.BlockSpec((SCAN_ROWS, SCAN_LANES), lambda i: (0, 0))
    return pl.pallas_call(
        body, name="s5_scan_fwd", grid=(lp // S5_TB,), in_specs=[blk, cst, cst], out_specs=blk,
        out_shape=jax.ShapeDtypeStruct(bu.shape, F32),
        scratch_shapes=[pltpu.VMEM((2, SCAN_ROWS, SCAN_LANES), F32)],
        compiler_params=_params(("arbitrary",)),
    )(bu, a_re, a_im)


def _s5_expand(a, w_t, *, a_blk, dims, name, hook=None):
    lp, nt = a.shape[0], w_t.shape[0]
    tm = _tile(lp, 176, 8)
    steps = lp // tm
    rows3 = 2 * SCAN_ROWS
    per = TILE_W // SCAN_LANES
    hook = _NO_HOOK if hook is None else hook
    hi, ho = len(hook.ins), len(hook.outs)

    def body(*refs):
        a_ref, w_ref, o_ref = refs[0], refs[1], refs[2 + hi]
        _run_hook(hook, pl.program_id(0), steps, refs[2:2 + hi], refs[3 + hi:3 + hi + ho], refs[3 + hi + ho:])
        for j in range(nt):
            s = j % TILE_G
            r = lax.dot_general(a_ref[:, 128 * s:128 * (s + 1)].astype(BF16), w_ref[j], dims,
                                preferred_element_type=F32)
            for c in range(per):
                o_ref[pl.ds(per * j + c, tm, stride=rows3), :] = r[:, SCAN_LANES * c:SCAN_LANES * (c + 1)]

    res = pl.pallas_call(
        body, name=name, grid=(steps,),
        in_specs=[pl.BlockSpec((tm, S5_W), lambda i: (i, a_blk)), pl.BlockSpec(w_t.shape, lambda i: (0, 0, 0))]
        + [_ANY] * hi,
        out_specs=[pl.BlockSpec((tm * rows3, SCAN_LANES), lambda i: (i, 0))] + [_ANY] * ho,
        out_shape=[jax.ShapeDtypeStruct((lp * rows3, SCAN_LANES), F32)] + hook.outs,
        scratch_shapes=hook.sems, compiler_params=_params(("arbitrary",)),
    )(a, w_t, *hook.ins)
    out3 = res[0].reshape(lp, rows3, SCAN_LANES)
    return out3 if hook is _NO_HOOK else (out3, res[1:])


def _s5_scan_bwd(gx, x, a_re, a_im):
    lp = gx.shape[0]
    nb = lp // S5_TB

    def body(gx_ref, x_ref, xp_ref, ar_ref, ai_ref, g_ref, da_ref, st):
        i = pl.program_id(0)

        @pl.when(i == 0)
        def _():
            st[...] = jnp.zeros_like(st)
            da_ref[...] = jnp.zeros_like(da_ref)

        ar, ai = ar_ref[...], ai_ref[...]
        first = (i == nb - 1).astype(F32)

        def step(s, carry):
            gr, gi, dar, dai = carry
            t = S5_TB - 1 - s
            ngr = gx_ref[t, 0:SCAN_ROWS, :] + ar * gr + ai * gi
            ngi = gx_ref[t, SCAN_ROWS:2 * SCAN_ROWS, :] + ar * gi - ai * gr
            g_ref[t, 0:SCAN_ROWS, :] = ngr
            g_ref[t, SCAN_ROWS:2 * SCAN_ROWS, :] = ngi
            tp = jnp.maximum(t - 1, 0)
            at0 = (t == 0).astype(F32)
            keep = 1.0 - at0
            pr = keep * x_ref[tp, 0:SCAN_ROWS, :] + at0 * (1.0 - first) * xp_ref[0, 0:SCAN_ROWS, :]
            pi = keep * x_ref[tp, SCAN_ROWS:2 * SCAN_ROWS, :] + at0 * (1.0 - first) * xp_ref[0, SCAN_ROWS:2 * SCAN_ROWS, :]
            return ngr, ngi, dar + ngr * pr + ngi * pi, dai + ngi * pr - ngr * pi

        zero = jnp.zeros((SCAN_ROWS, SCAN_LANES), F32)
        gr, gi, dar, dai = lax.fori_loop(0, S5_TB, step, (st[0], st[1], zero, zero))
        st[0] = gr
        st[1] = gi
        da_ref[0] += dar
        da_ref[1] += dai

    rev = lambda i: nb - 1 - i
    blk = pl.BlockSpec((S5_TB, 2 * SCAN_ROWS, SCAN_LANES), lambda i: (rev(i), 0, 0))
    prev = pl.BlockSpec((1, 2 * SCAN_ROWS, SCAN_LANES), lambda i: (jnp.maximum(rev(i) * S5_TB - 1, 0), 0, 0))
    cst = pl.BlockSpec((SCAN_ROWS, SCAN_LANES), lambda i: (0, 0))
    return pl.pallas_call(
        body, name="s5_scan_bwd", grid=(nb,), in_specs=[blk, blk, prev, cst, cst],
        out_specs=[blk, pl.BlockSpec((2, SCAN_ROWS, SCAN_LANES), lambda i: (0, 0, 0))],
        out_shape=[jax.ShapeDtypeStruct(gx.shape, F32), jax.ShapeDtypeStruct((2, SCAN_ROWS, SCAN_LANES), F32)],
        scratch_shapes=[pltpu.VMEM((2, SCAN_ROWS, SCAN_LANES), F32)],
        compiler_params=_params(("arbitrary",)),
    )(gx, x, x, a_re, a_im)


def _place():
    x, y, c = lax.axis_index("x"), lax.axis_index("y"), lax.axis_index("c")
    return x, y, c, [(1 - x, y), (x, 1 - y), (1 - x, 1 - y)]


def _gather_phases(nrows):
    half = (nrows // 32) * 16
    assert 0 < half < nrows

    def plan(x_ref, out_ref, send_sems, recv_sems, local_sem):
        x, y, c, _ = _place()
        me, sibling = (x, y, c), (x, y, 1 - c)
        at_x, at_y, at_d = (1 - x, y, c), (x, 1 - y, c), (1 - x, 1 - y, c)

        def rows(block, part=None):
            idx = 4 * block[0] + 2 * block[1] + block[2]
            if part is None:
                return out_ref.at[idx]
            return out_ref.at[idx, pl.ds(0, half)] if part == 0 else out_ref.at[idx, pl.ds(half, nrows - half)]

        def copy(k, block, to, src=None, part=None):
            return pltpu.make_async_remote_copy(
                src_ref=rows(block, part) if src is None else src, dst_ref=rows(block, part),
                send_sem=send_sems.at[k], recv_sem=recv_sems.at[k], device_id=to, device_id_type=MESH)

        def other_core(block):
            return (block[0], block[1], 1 - c)

        mine = pltpu.make_async_copy(x_ref, rows(me), local_sem)
        direct = [copy(0, me, sibling, src=x_ref), copy(1, me, at_x, src=x_ref), copy(2, me, at_y, src=x_ref)]
        relays = [copy(3, at_x, at_y, part=0), copy(7, at_y, at_x, part=1)]
        passed = [copy(4, at_x, sibling), copy(5, at_y, sibling), copy(6, at_d, sibling)]
        landed = [copy(1, at_x, me), copy(2, at_y, me), copy(3, at_d, me, part=0), copy(7, at_d, me, part=1)]
        from_sibling = [copy(0, sibling, me)] + [copy(4 + j, other_core(b), me) for j, b in enumerate((at_x, at_y, at_d))]
        return mine, direct, relays, passed, landed, from_sibling

    def start(ins, outs, sems):
        mine, direct, _, _, _, _ = plan(ins[0], outs[0], *sems)
        mine.start()
        for cp in direct:
            cp.start()

    def middle(ins, outs, sems):
        _, _, relays, passed, landed, _ = plan(ins[0], outs[0], *sems)
        for j in range(2):
            landed[j].wait_recv()
            passed[j].start()
            relays[j].start()

    def late(ins, outs, sems):
        _, _, _, passed, landed, _ = plan(ins[0], outs[0], *sems)
        landed[2].wait_recv()
        landed[3].wait_recv()
        passed[2].start()

    def finish(ins, outs, sems):
        mine, direct, relays, passed, _, from_sibling = plan(ins[0], outs[0], *sems)
        for cp in from_sibling:
            cp.wait_recv()
        for cp in direct + relays + passed:
            cp.wait_send()
        mine.wait()

    return start, middle, late, finish


_GATHER_SEMS = [pltpu.SemaphoreType.DMA((8,)), pltpu.SemaphoreType.DMA((8,)), pltpu.SemaphoreType.DMA]


def _all_gather(shard, name):
    phases = _gather_phases(shard.shape[0])

    def body(x_ref, out_ref, *sems):
        for phase in phases:
            phase([x_ref], [out_ref], sems)

    return pl.pallas_call(
        body, name=name, out_shape=jax.ShapeDtypeStruct((N_DEV,) + shard.shape, shard.dtype),
        in_specs=[_ANY], out_specs=_ANY, scratch_shapes=list(_GATHER_SEMS),
    )(shard)


def _gather_hook(shard):
    start, middle, late, finish = _gather_phases(shard.shape[0])
    return _Hook([shard], [jax.ShapeDtypeStruct((N_DEV,) + shard.shape, shard.dtype)], _GATHER_SEMS,
                 [(0.0, start), (0.5, middle), (0.85, late), (1.0, finish)])


def _swap_with_sibling(parts, name):
    def body(p_ref, out_ref, send_sems, recv_sems):
        x, y, c, _ = _place()
        copies = [pltpu.make_async_remote_copy(
            src_ref=p_ref.at[2 * chip + (1 - c)], dst_ref=out_ref.at[chip],
            send_sem=send_sems.at[chip], recv_sem=recv_sems.at[chip],
            device_id=(x, y, 1 - c), device_id_type=MESH) for chip in range(4)]
        for cp in copies:
            cp.start()
        for cp in copies:
            cp.wait()

    return pl.pallas_call(
        body, name=name, out_shape=jax.ShapeDtypeStruct((4,) + parts.shape[1:], parts.dtype),
        in_specs=[pl.BlockSpec(memory_space=pl.ANY)], out_specs=pl.BlockSpec(memory_space=pl.ANY),
        scratch_shapes=[pltpu.SemaphoreType.DMA((4,)), pltpu.SemaphoreType.DMA((4,))],
    )(parts)


def _chips_phases(lo, rows):
    def copies(p_ref, out_ref, send_sems, recv_sems):
        x, y, c, chips = _place()
        return [pltpu.make_async_remote_copy(
            src_ref=p_ref.at[2 * px + py, pl.ds(lo, rows)], dst_ref=out_ref.at[j],
            send_sem=send_sems.at[j], recv_sem=recv_sems.at[j],
            device_id=(px, py, c), device_id_type=MESH) for j, (px, py) in enumerate(chips)]

    def start(ins, outs, sems):
        for cp in copies(ins[0], outs[0], *sems):
            cp.start()

    def finish(ins, outs, sems):
        for cp in copies(ins[0], outs[0], *sems):
            cp.wait()

    return start, finish


def _chips_hook(parts, lo=0, hi=None):
    rows = (parts.shape[1] if hi is None else hi) - lo
    start, finish = _chips_phases(lo, rows)
    return _Hook([parts], [jax.ShapeDtypeStruct((3, rows) + parts.shape[2:], parts.dtype)],
                 [pltpu.SemaphoreType.DMA((3,)), pltpu.SemaphoreType.DMA((3,))], [(0.0, start), (1.0, finish)])


BIG_LAYOUT = (("w_in_ab", D_MODEL, PACK_COLS), ("s5_w_glu", S5_W // N_DEV, PACK_COLS),
              ("w_out_ab", OUT_AB // N_DEV, 2 * PACK_COLS), ("w_in_c", D_MODEL, PACK_COLS),
              ("w_out_c", GLA_W // N_DEV, 2 * PACK_COLS))


def _to_rows(a):
    if a.shape[-1] == PACK_COLS:
        return a
    assert a.shape[-1] == 2 * PACK_COLS
    return jnp.concatenate([a[..., :PACK_COLS], a[..., PACK_COLS:]], axis=-2)


def _from_rows(p, cols):
    if cols == PACK_COLS:
        return p
    r = p.shape[-2] // 2
    return jnp.concatenate([p[..., :r, :], p[..., r:, :]], axis=-1)


FIRST_LAYOUT = BIG_LAYOUT[:1]
OTHER_LAYOUT = BIG_LAYOUT[1:3] + BIG_LAYOUT[4:]
GLU_AB_LAYOUT = BIG_LAYOUT[1:3]
IN_C_LAYOUT = BIG_LAYOUT[3:4]


def _pack_big(pieces, layout):
    return jnp.concatenate([_to_rows(pieces[name]) for name, _, _ in layout], axis=-2)


def _unpack_big(buf, layout):
    out, o = {}, 0
    for name, rows, cols in layout:
        r = rows * cols // PACK_COLS
        out[name] = _from_rows(buf[..., o:o + r, :], cols)
        o += r
    return out


def _column_windows(g):
    rows, quarter = g.shape[0], WIN_COLS // 4

    def body(g_ref, o_ref):
        o_ref[...] = g_ref[...]

    return pl.pallas_call(
        body, name="w_in_c_grad_windows", grid=(N_DEV, WIN_COLS // quarter),
        in_specs=[pl.BlockSpec((rows, quarter), lambda d, c: (0, (WIN_STEP // quarter) * d + c))],
        out_specs=pl.BlockSpec((None, rows, quarter), lambda d, c: (d, 0, c)),
        out_shape=jax.ShapeDtypeStruct((N_DEV, rows, WIN_COLS), g.dtype),
        compiler_params=_params(("arbitrary", "arbitrary")),
    )(g)


def _rows1024(a):
    r, c = a.shape
    if c > PACK_COLS:
        a = jnp.concatenate([a[:, i * PACK_COLS:(i + 1) * PACK_COLS] for i in range(c // PACK_COLS)], axis=0)
    elif c < PACK_COLS:
        a = jnp.pad(a, ((0, 0), (0, PACK_COLS - c)))
    return jnp.pad(a, ((0, -a.shape[0] % 8), (0, 0)))


def _unrows1024(p, r, c):
    if c > PACK_COLS:
        return jnp.concatenate([p[i * r:(i + 1) * r] for i in range(c // PACK_COLS)], axis=1)
    return p[:r, :c]


def _lane_select(a, off, sign, n_out, out_dtype, exact, name):
    rows, n_in = a.shape
    tr = _tile(rows, 256, 16)

    def body(off_ref, a_ref, o_ref):
        sel = _iota2((n_in, n_out), 0) + off_ref[0] * sign == _iota2((n_in, n_out), 1)
        if exact:
            r = jnp.dot(a_ref[...], sel.astype(F32), precision=lax.Precision.HIGHEST, preferred_element_type=F32)
        else:
            r = _dg(a_ref[...], sel.astype(BF16), 1, 0)
        o_ref[...] = r.astype(out_dtype)

    return pl.pallas_call(
        body, name=name, grid=(rows // tr,),
        in_specs=[pl.BlockSpec(memory_space=pltpu.SMEM), pl.BlockSpec((tr, n_in), lambda i: (i, 0))],
        out_specs=pl.BlockSpec((tr, n_out), lambda i: (i, 0)),
        out_shape=jax.ShapeDtypeStruct((rows, n_out), out_dtype),
        compiler_params=_params(("arbitrary",)),
    )(off, a)


def _adamw(w, g, m, v, name):
    rows, cols = w.shape
    tr = _tile(rows, 256, 8) if rows % 8 == 0 else rows

    def fn(i, w_, g_, m_, v_):
        m_new = ADAM_B1 * m_ + (1.0 - ADAM_B1) * g_
        v_new = ADAM_B2 * v_ + (1.0 - ADAM_B2) * (g_ * g_)
        m_hat = m_new / (1.0 - ADAM_B1 ** ADAM_STEP)
        v_hat = v_new / (1.0 - ADAM_B2 ** ADAM_STEP)
        delta = -ADAM_LR * (m_hat / (jnp.sqrt(v_hat) + ADAM_EPS) + ADAM_WD * w_)
        return (delta, m_new, v_new), ()

    outs, _ = _rows(fn, [_win(w), _win(g), _win(m), _win(v)], [], [(cols, F32)] * 3, [], name=name,
                    nrow=rows, tr=tr)
    return outs


def _as2d(a):
    if a.ndim == 1:
        return a.reshape(1, -1)
    if a.ndim == 2:
        return a
    a = a.reshape(a.shape[1:])
    return a if a.ndim == 2 else a.reshape(a.shape[0], -1)


def kernel(x, meta, norm_ab_w, w_in_ab, ret_norm_w, s5_lam_re, s5_lam_im, s5_log_dt, s5_b_re, s5_b_im, s5_c_re, s5_c_im, s5_d, s5_w_glu, w_out_ab, norm_c_w, w_in_c, gla_w_gate, gla_b_gate, gla_norm_w, w_out_c, final_norm_w, loss_target, m_meta, m_norm_ab_w, m_w_in_ab, m_ret_norm_w, m_s5_lam_re, m_s5_lam_im, m_s5_log_dt, m_s5_b_re, m_s5_b_im, m_s5_c_re, m_s5_c_im, m_s5_d, m_s5_w_glu, m_w_out_ab, m_norm_c_w, m_w_in_c, m_gla_w_gate, m_gla_b_gate, m_gla_norm_w, m_w_out_c, m_final_norm_w, v_meta, v_norm_ab_w, v_w_in_ab, v_ret_norm_w, v_s5_lam_re, v_s5_lam_im, v_s5_log_dt, v_s5_b_re, v_s5_b_im, v_s5_c_re, v_s5_c_im, v_s5_d, v_s5_w_glu, v_w_out_ab, v_norm_c_w, v_w_in_c, v_gla_w_gate, v_gla_b_gate, v_gla_norm_w, v_w_out_c, v_final_norm_w):
    weights = dict(meta=meta, norm_ab_w=norm_ab_w, w_in_ab=w_in_ab, ret_norm_w=ret_norm_w, s5_lam_re=s5_lam_re,
                   s5_lam_im=s5_lam_im, s5_log_dt=s5_log_dt, s5_b_re=s5_b_re, s5_b_im=s5_b_im, s5_c_re=s5_c_re,
                   s5_c_im=s5_c_im, s5_d=s5_d, s5_w_glu=s5_w_glu, w_out_ab=w_out_ab, norm_c_w=norm_c_w,
                   w_in_c=w_in_c, gla_w_gate=gla_w_gate, gla_b_gate=gla_b_gate, gla_norm_w=gla_norm_w,
                   w_out_c=w_out_c, final_norm_w=final_norm_w)
    mom_m = dict(meta=m_meta, norm_ab_w=m_norm_ab_w, w_in_ab=m_w_in_ab, ret_norm_w=m_ret_norm_w,
                 s5_lam_re=m_s5_lam_re, s5_lam_im=m_s5_lam_im, s5_log_dt=m_s5_log_dt, s5_b_re=m_s5_b_re,
                 s5_b_im=m_s5_b_im, s5_c_re=m_s5_c_re, s5_c_im=m_s5_c_im, s5_d=m_s5_d, s5_w_glu=m_s5_w_glu,
                 w_out_ab=m_w_out_ab, norm_c_w=m_norm_c_w, w_in_c=m_w_in_c, gla_w_gate=m_gla_w_gate,
                 gla_b_gate=m_gla_b_gate, gla_norm_w=m_gla_norm_w, w_out_c=m_w_out_c, final_norm_w=m_final_norm_w)
    mom_v = dict(meta=v_meta, norm_ab_w=v_norm_ab_w, w_in_ab=v_w_in_ab, ret_norm_w=v_ret_norm_w,
                 s5_lam_re=v_s5_lam_re, s5_lam_im=v_s5_lam_im, s5_log_dt=v_s5_log_dt, s5_b_re=v_s5_b_re,
                 s5_b_im=v_s5_b_im, s5_c_re=v_s5_c_re, s5_c_im=v_s5_c_im, s5_d=v_s5_d, s5_w_glu=v_s5_w_glu,
                 w_out_ab=v_w_out_ab, norm_c_w=v_norm_c_w, w_in_c=v_w_in_c, gla_w_gate=v_gla_w_gate,
                 gla_b_gate=v_gla_b_gate, gla_norm_w=v_gla_norm_w, w_out_c=v_w_out_c, final_norm_w=v_final_norm_w)
    order = list(weights)

    seq = x.shape[1]
    lp = CHUNK + seq
    nchunk = lp // CHUNK
    dev = 4 * lax.axis_index("x") + 2 * lax.axis_index("y") + lax.axis_index("c")
    core = lax.axis_index("c")
    chip = 2 * lax.axis_index("x") + lax.axis_index("y")

    win_off = jnp.reshape(2 * dev, (1,)).astype(jnp.int32)
    shard_c = jnp.pad(w_in_c[0].astype(BF16), ((0, 0), (0, 896 - SHARD_C)))
    big_shards = dict(w_in_ab=w_in_ab[0].astype(BF16), s5_w_glu=s5_w_glu[0].astype(BF16),
                      w_out_ab=w_out_ab[0].astype(BF16), w_out_c=w_out_c[0].astype(BF16),
                      w_in_c=_lane_select(shard_c, win_off, 1, WIN_COLS, BF16, False, "w_in_c_to_window"))
    def pad_to(a, rows, cols):
        return jnp.pad(a, ((0, rows - a.shape[0]), (0, cols - a.shape[1])))

    shard_w = D_MODEL // N_DEV
    small_pack = jnp.concatenate([meta, pad_to(norm_c_w, 8, shard_w), pad_to(gla_w_gate[0], GLA_RANK, shard_w),
                                  pad_to(gla_b_gate, 8, shard_w), pad_to(gla_norm_w, 8, shard_w)], axis=0)
    w_in_ab_g = _all_gather(big_shards["w_in_ab"], "gather_first")
    win_cut = 1408
    in_c_hook_a = _gather_hook(big_shards["w_in_c"][:win_cut])
    in_c_hook_b = _gather_hook(big_shards["w_in_c"][win_cut:])
    glu_ab_hook = _gather_hook(_pack_big(big_shards, GLU_AB_LAYOUT))
    out_c_hook = _gather_hook(_to_rows(big_shards["w_out_c"]))
    gs = _all_gather(small_pack, "gather_small")
    gate_w = GLA_QK // N_DEV
    s_meta, s_norm_c = gs[:, :N_META], gs[:, N_META]
    s_wgate, s_bgate, s_gnorm = gs[:, 24:24 + GLA_RANK, :gate_w], gs[:, 40, :gate_w], gs[:, 48]
    meta_f = s_meta.transpose(1, 0, 2).reshape(N_META, D_MODEL)
    norm_c_f = s_norm_c.reshape(1, D_MODEL)
    w_gate_f = jnp.pad(s_wgate.transpose(1, 0, 2).reshape(GLA_RANK, GLA_QK), ((0, GATE_PAD - GLA_RANK), (0, 0)))
    b_gate_f = s_bgate.reshape(1, GLA_QK)
    gla_norm_f = s_gnorm.reshape(GLA_H, 1, GLA_DV)

    pos = jnp.maximum(jnp.arange(lp, dtype=F32) - float(PAD), 0.0)
    inv_freq = jnp.power(ROPE_BASE, -jnp.arange(0, RET_DK, 2, dtype=F32) / RET_DK)
    ang = pos[:, None] * inv_freq[None, :]
    cos2 = jnp.concatenate([jnp.cos(ang), jnp.cos(ang)], axis=1)
    sin2 = jnp.concatenate([-jnp.sin(ang), jnp.sin(ang)], axis=1)
    log_g = jnp.log1p(-jnp.exp2(-5.0 - jnp.arange(RET_H, dtype=F32)))
    lg = jnp.broadcast_to(log_g[:, None, None], (RET_H, 1, 128))
    ret_norm_h = ret_norm_w.reshape(RET_H, 1, RET_DV)

    h0 = jnp.concatenate([jnp.zeros((PAD, D_MODEL), F32), meta_f, x[0]], axis=0)

    def rowmask(i):
        return (_iota2((CHUNK, 1), 0) + i * CHUNK) >= PAD

    (hn0,), _ = _rows(lambda i, h, w: ((_rms(h, w),), ()), [_win(h0)], [norm_ab_w], [(D_MODEL, BF16)], [],
                      name="norm_ab_fwd", nrow=lp)
    proj_ab, (w_in_c_ga,) = _mm(hn0, w_in_ab_g, "nn", name="in_ab_fwd", hook=in_c_hook_a, b_dev=True)

    q_off, k_off, v_off, za_off = 0, RET_QK, 2 * RET_QK, 2 * RET_QK + RET_W
    u_off, zb_off = 2 * RET_QK + 2 * RET_W, 2 * RET_QK + 2 * RET_W + S5_W
    ret_xs = [(proj_ab, RET_DK, lambda h: q_off // RET_DK + h), (proj_ab, RET_DK, lambda h: k_off // RET_DK + h),
              (proj_ab, RET_DV, lambda h: v_off // RET_DV + h), (proj_ab, RET_DV, lambda h: za_off // RET_DV + h)]
    ret_cs = [(cos2, RET_DK, lambda h: 0), (sin2, RET_DK, lambda h: 0)]
    ret_kw = dict(heads=RET_H, nchunk=nchunk, s_shape=(RET_DK, RET_DV), out_w=RET_DV, pre=_ret_pre, hpb=RET_H)
    o_a, ret_sprev, (gathered_glu_ab,) = _scan_fwd(_ret_chunk, ret_xs, ret_cs, [ret_norm_h], [lg], name="ret_fwd",
                                                   hook=glu_ab_hook, **ret_kw)
    gb = _unpack_big(gathered_glu_ab, GLU_AB_LAYOUT)
    w_glu_f = gb["s5_w_glu"].reshape(S5_W, S5_W)
    w_out_ab_f = gb["w_out_ab"].reshape(OUT_AB, D_MODEL)

    expand = jnp.repeat(jnp.eye(S5_P, dtype=F32), S5_GH, axis=1)
    disc_args = (s5_lam_re[0], s5_lam_im[0], s5_log_dt[0].reshape(S5_G, 1),
                 s5_b_re[0].reshape(S5_G, S5_P * S5_GH), s5_b_im[0].reshape(S5_G, S5_P * S5_GH), expand)
    ab_re, ab_im, bb_re, bb_im = _s5_disc_fwd(disc_args)
    gt = TILE_G
    eye_t = jnp.eye(gt, dtype=F32)

    def tiles_in(bb):
        return jnp.einsum("sgph,gk->sghkp", bb.reshape(gt, gt, S5_P, S5_GH), eye_t).reshape(gt, 128, TILE_W)

    def tiles_out(cc):
        return jnp.einsum("sghp,gk->sgpkh", cc.reshape(gt, gt, S5_GH, S5_P), eye_t).reshape(gt, TILE_W, 128)

    wb_t = jnp.concatenate([tiles_in(bb_re), tiles_in(bb_im)], axis=0).astype(BF16)
    wc_t = jnp.concatenate([tiles_out(s5_c_re[0]), -tiles_out(s5_c_im[0])], axis=0).astype(BF16)
    a_re, a_im = ab_re.reshape(SCAN_ROWS, SCAN_LANES), ab_im.reshape(SCAN_ROWS, SCAN_LANES)
    tm5, tk5, nt5 = _tile(lp, 1408, 8), _tile(lp, 1408, 8), 2 * gt
    u_blk = u_off // 128
    wide_k = pl.BlockSpec((tm5, TILE_W), lambda i, j, k: (i, k * gt + j))
    narrow = pl.BlockSpec((tm5, 128), lambda i, j, k: (i, j))
    wb_k = pl.BlockSpec((None, 128, TILE_W), lambda i, j, k: (k * gt + j, 0, 0))
    wc_k = pl.BlockSpec((None, TILE_W, 128), lambda i, j, k: (k * gt + j, 0, 0))
    bu3, (w_out_c_g,) = _s5_expand(proj_ab, wb_t, a_blk=u_off // S5_W, dims=NN, name="s5_bu", hook=out_c_hook)
    w_out_c_f = _from_rows(w_out_c_g, D_MODEL).reshape(GLA_W, D_MODEL)
    xs5 = _s5_scan_fwd(bu3, a_re, a_im)
    xs5_2d = xs5.reshape(lp, 2 * S5_N)
    y_pre, (w_in_c_gb,) = _mm_core(xs5_2d, wc_t, dims=NN, grid=(lp // tm5, gt, 2), name="s5_cx", a_spec=wide_k,
                                   b_spec=wc_k, o_spec=narrow, out_shape=jax.ShapeDtypeStruct((lp, S5_W), F32),
                                   acc_shape=(tm5, 128), hook=in_c_hook_b)
    (y_s5, yg_bf), _ = _rows(
        lambda i, yp, u, d: ((yp + d * u, _gelu(yp + d * u)), ()),
        [_win(y_pre), _win(proj_ab, u_off, S5_W)], [s5_d], [(S5_W, F32), (S5_W, BF16)], [], name="s5_gelu_fwd", nrow=lp)
    t_glu = _mm(yg_bf, w_glu_f, "nn", name="s5_glu_fwd")

    def s5_gate(y, t, zb):
        return _gelu(y) * _sigmoid(t) * _silu(zb)

    (o_b,), _ = _rows(lambda i, y, t, zb: ((s5_gate(y, t, zb),), ()),
                      [_win(y_s5), _win(t_glu), _win(proj_ab, zb_off, S5_W)], [], [(S5_W, BF16)], [],
                      name="s5_gate_fwd", nrow=lp)
    o_ab = jnp.concatenate([o_a, o_b], axis=1)
    h1 = _mm(o_ab, w_out_ab_f, "nn", name="out_ab_fwd", add=h0)
    w_in_c_g = jnp.concatenate([w_in_c_ga, w_in_c_gb], axis=1)
    w_in_c_f = sum(jnp.pad(w_in_c_g[d], ((0, 0), (WIN_STEP * d, IN_C_PAD - WIN_STEP * d - WIN_COLS)))
                   for d in range(N_DEV))

    (hn1,), _ = _rows(lambda i, h, w: ((_rms(h, w),), ()), [_win(h1)], [norm_c_f], [(D_MODEL, BF16)], [],
                      name="norm_c_fwd", nrow=lp)
    proj_c = _mm(hn1, w_in_c_f, "nn", name="in_c_fwd")
    gl_off = 2 * GLA_QK + 2 * GLA_W
    pre_gate = _mm(proj_c, w_gate_f, "nn", name="gate_fwd", a_win=(gl_off, GATE_PAD), bias=b_gate_f)
    gla_xs = [(proj_c, GLA_DK, lambda h: h), (proj_c, GLA_DK, lambda h: GLA_QK // GLA_DK + h),
              (proj_c, GLA_DV, lambda h: 2 * GLA_QK // GLA_DV + h),
              (proj_c, GLA_DV, lambda h: (2 * GLA_QK + GLA_W) // GLA_DV + h),
              (pre_gate, GLA_DK, lambda h: h)]
    gla_kw = dict(heads=GLA_H, nchunk=nchunk, s_shape=(GLA_DV, GLA_DK), out_w=GLA_DV, hpb=GLA_H)
    o_c, gla_sprev = _scan_fwd(functools.partial(_gla_chunk, cumsum=_cumsum_rows_f32), gla_xs, [], [gla_norm_f], [],
                               name="gla_fwd", **gla_kw)
    h2 = _mm(o_c, w_out_c_f, "nn", name="out_c_fwd", add=h1)

    fnw = final_norm_w.reshape(1, D_MODEL)

    def final_fn(i, h, tgt, w):
        def loss_of(h_, w_):
            err = _rms(h_, w_) - tgt
            return 0.5 * jnp.sum(jnp.mean(err * err, axis=-1))

        real = (i > 0).astype(F32)
        loss_i, (dh, dw) = jax.value_and_grad(loss_of, argnums=(0, 1))(h, w)
        return (dh * real, dh * real), (jnp.full((1, 128), loss_i * real, F32), dw * real)

    (dh2, dh2_bf), (loss_acc, g_final) = _rows(
        final_fn, [_win(h2), _win(loss_target[0], roff=1)], [fnw], [(D_MODEL, F32), (D_MODEL, BF16)],
        [(1, 128), (1, D_MODEL)], name="final_loss", nrow=lp)

    def rs_front(pieces, layout, tag):
        g_full = _pack_big(pieces, layout)
        prow = g_full.shape[1]
        from_sibling = _swap_with_sibling(g_full, "rs_sibling_" + tag)
        mine_by_chip = lax.dynamic_index_in_dim(g_full.reshape(4, 2, prow, PACK_COLS), core, axis=1, keepdims=False)
        (p1, p1_bf), _ = _rows(
            lambda i, a, b: ((a.astype(F32) + b.astype(F32), a.astype(F32) + b.astype(F32)), ()),
            [_win(mine_by_chip.reshape(4 * prow, PACK_COLS)), _win(from_sibling.reshape(4 * prow, PACK_COLS))], [],
            [(PACK_COLS, F32), (PACK_COLS, BF16)], [], name="rs_sum_sibling_" + tag, nrow=4 * prow,
            tr=_tile(prow, 512, 16))
        return p1.reshape(4, prow, PACK_COLS), p1_bf.reshape(4, prow, PACK_COLS)

    def rs_back(p1, from_chips, layout, tag):
        prow = p1.shape[1]
        tr = _tile(prow, 512, 16)
        own = lax.dynamic_index_in_dim(p1, chip, axis=0, keepdims=False)
        fc2 = from_chips.reshape(3 * prow, PACK_COLS)
        nblk = prow // tr
        (g_shard,), _ = _rows(
            lambda i, a, b0, b1, b2: ((((a + b0.astype(F32)) + b1.astype(F32)) + b2.astype(F32),), ()),
            [_win(own), _win(fc2), _win(fc2, roff=-nblk), _win(fc2, roff=-2 * nblk)], [], [(PACK_COLS, F32)], [],
            name="rs_sum_chips_" + tag, nrow=prow, tr=tr)
        return _unpack_big(g_shard, layout)

    do_c = _mm(dh2_bf, w_out_c_f, "nt", name="out_c_dx", out_dtype=BF16)
    gw_out_c = _mm(o_c, dh2_bf, "tn", name="out_c_dw", out_dtype=BF16)
    (dq_c, dk_c, dv_c, dz_c, dpre), (g_gla_norm,) = _scan_bwd(
        _gla_chunk, gla_xs, [], [gla_norm_f], [], do_c, gla_sprev, name="gla_bwd", **gla_kw)
    dglow = _mm(dpre, w_gate_f, "nt", name="gate_dx", out_dtype=BF16)
    g_wgate = _mm(proj_c, dpre, "tn", name="gate_dw", a_win=(gl_off, GATE_PAD))[:GLA_RANK]
    (), (g_bgate,) = _rows(lambda i, d: ((), (jnp.sum(d.astype(F32), axis=0, keepdims=True),)), [_win(dpre)], [], [],
                           [(1, GLA_QK)], name="gate_db", nrow=lp)
    dproj_c = jnp.concatenate([dq_c, dk_c, dv_c, dz_c, dglow], axis=1)
    dhn1 = _mm(dproj_c, w_in_c_f, "nt", name="in_c_dx")
    gw_in_c = _mm(hn1, dproj_c, "tn", name="in_c_dw", out_dtype=BF16)
    p1_c, p1_c_bf = rs_front(dict(
        w_in_c=_column_windows(gw_in_c)),
        IN_C_LAYOUT, "in_c")

    def norm_bwd(i, h, dhn, dres, w):
        _, vjp = jax.vjp(_rms, h, w)
        dh, dw = vjp(dhn)
        return (jnp.where(rowmask(i), dh + dres, 0.0),), (dw,)

    def norm_bwd_both(i, h, dhn, dres, w):
        (dh,), acc = norm_bwd(i, h, dhn, dres, w)
        return (dh, dh), acc

    (dh1, dh1_bf), (g_norm_c,) = _rows(norm_bwd_both, [_win(h1), _win(dhn1), _win(dh2)], [norm_c_f],
                                       [(D_MODEL, F32), (D_MODEL, BF16)], [(1, D_MODEL)], name="norm_c_bwd", nrow=lp)

    do_ab = _mm(dh1_bf, w_out_ab_f, "nt", name="out_ab_dx", out_dtype=BF16)
    gw_out_ab = _mm(o_ab, dh1_bf, "tn", name="out_ab_dw", out_dtype=BF16)

    def s5_gate_bwd(i, dob, y, t, zb):
        _, vjp = jax.vjp(s5_gate, y, t, zb)
        dy, dt, dzb = vjp(dob.astype(F32))
        return (dy, dt, dzb), ()

    (dy_a, dt_glu, dzb), _ = _rows(
        s5_gate_bwd, [_win(do_ab, RET_W, S5_W), _win(y_s5), _win(t_glu), _win(proj_ab, zb_off, S5_W)], [],
        [(S5_W, F32), (S5_W, BF16), (S5_W, BF16)], [], name="s5_gate_bwd", nrow=lp)
    dyg2 = _mm(dt_glu, w_glu_f, "nt", name="s5_glu_dx")
    gw_glu = _mm(yg_bf, dt_glu, "tn", name="s5_glu_dw", out_dtype=BF16)

    def s5_y_bwd(i, dya, dyg, y, u, d):
        _, vjp = jax.vjp(_gelu, y)
        (dy_g,) = vjp(dyg)
        dy = dya + dy_g
        return (dy, d * dy), (jnp.sum(dy * u, axis=0, keepdims=True),)

    (dy_s5, du1), (g_d,) = _rows(
        s5_y_bwd, [_win(dy_a), _win(dyg2), _win(y_s5), _win(proj_ab, u_off, S5_W)], [s5_d],
        [(S5_W, BF16), (S5_W, F32)], [(1, S5_W)], name="s5_y_bwd", nrow=lp)
    p1_o, p1_o_bf = rs_front(dict(s5_w_glu=gw_glu.reshape(N_DEV, S5_W // N_DEV, S5_W),
                                  w_out_ab=gw_out_ab.reshape(N_DEV, OUT_AB // N_DEV, D_MODEL),
                                  w_out_c=gw_out_c.reshape(N_DEV, GLA_W // N_DEV, D_MODEL)), OTHER_LAYOUT, "other")
    o_cut = 640
    gx3, (from_chips_oa,) = _s5_expand(dy_s5, wc_t, a_blk=0, dims=NT, name="s5_cx_dx",
                                       hook=_chips_hook(p1_o_bf, 0, o_cut))
    rows_k = lambda col: pl.BlockSpec((tk5, col), lambda i, j, k: (k, i))
    gwc = _mm_core(xs5_2d, dy_s5, dims=TN, grid=(nt5, 1, lp // tk5), name="s5_cx_dw", a_spec=rows_k(TILE_W),
                   b_spec=pl.BlockSpec((tk5, 128), lambda i, j, k: (k, i % gt)),
                   o_spec=pl.BlockSpec((None, TILE_W, 128), lambda i, j, k: (i, 0, 0)),
                   out_shape=jax.ShapeDtypeStruct((nt5, TILE_W, 128), F32), acc_shape=(TILE_W, 128))
    g_s5, da = _s5_scan_bwd(gx3, xs5, a_re, a_im)
    g_s5_2d = g_s5.reshape(lp, 2 * S5_N)
    du, (from_chips_ob,) = _mm_core(g_s5_2d, wb_t, dims=NT, grid=(lp // tm5, gt, 2), name="s5_bu_dx", a_spec=wide_k,
                                    b_spec=wb_k, o_spec=narrow, out_shape=jax.ShapeDtypeStruct((lp, S5_W), BF16),
                                    acc_shape=(tm5, 128), extra=[(du1, narrow)],
                                    hook=_chips_hook(p1_o_bf, o_cut, None))
    from_chips_o = jnp.concatenate([from_chips_oa, from_chips_ob], axis=1)
    gwb = _mm_core(proj_ab, g_s5_2d, dims=TN, grid=(nt5, 1, lp // tk5), name="s5_bu_dw",
                   a_spec=pl.BlockSpec((tk5, 128), lambda i, j, k: (k, u_blk + i % gt)), b_spec=rows_k(TILE_W),
                   o_spec=pl.BlockSpec((None, 128, TILE_W), lambda i, j, k: (i, 0, 0)),
                   out_shape=jax.ShapeDtypeStruct((nt5, 128, TILE_W), F32), acc_shape=(128, TILE_W))
    gwc6 = gwc.reshape(2, gt, gt, S5_P, gt, S5_GH)
    g_c = jnp.einsum("rsgpgh->rsghp", gwc6).reshape(2, S5_G, S5_GH, S5_P)
    g_c_re, g_c_im = g_c[0], -g_c[1]
    gwb6 = gwb.reshape(2, gt, gt, S5_GH, gt, S5_P)
    d_bb = jnp.einsum("rsghgp->rsgph", gwb6).reshape(2, S5_G, S5_P * S5_GH)
    d_bb_re, d_bb_im = d_bb[0], d_bb[1]
    g_lam_re, g_lam_im, g_log_dt, g_b_re, g_b_im = _s5_disc_bwd(
        disc_args, (da[0].reshape(S5_G, S5_P), da[1].reshape(S5_G, S5_P), d_bb_re, d_bb_im))

    (dq_a, dk_a, dv_a, dz_a), (g_ret_norm,), (from_chips_c,) = _scan_bwd(
        _ret_chunk, ret_xs, ret_cs, [ret_norm_h], [lg], do_ab, ret_sprev, name="ret_bwd", post=_ret_post,
        hook=_chips_hook(p1_c_bf), **ret_kw)
    dproj_ab = jnp.concatenate([dq_a, dk_a, dv_a, dz_a, du, dzb], axis=1)

    lane = lambda a_: pad_to(a_, a_.shape[0], 128)

    def sum8(i, *blocks):
        acc = blocks[0]
        for b in blocks[1:]:
            acc = acc + b
        return (acc,), ()

    def pack_small(pieces):
        return jnp.concatenate([_rows1024(p) for _, p in pieces], axis=0)

    def sum_small(gathered, pieces, tag):
        srow = gathered.shape[1]
        tr = _tile(srow, 128, 8)
        flat = gathered.reshape(N_DEV * srow, PACK_COLS)
        (total,), _ = _rows(sum8, [_win(flat, roff=-d * (srow // tr)) for d in range(N_DEV)], [], [(PACK_COLS, F32)],
                            [], name="sum_small_" + tag, nrow=srow, tr=tr)
        out, o = {}, 0
        for name_, p in pieces:
            r8 = _rows1024(p).shape[0]
            out[name_] = _unrows1024(total[o:o + r8], *p.shape)
            o += r8
        return out

    early_pieces = [
        ("vec2048", jnp.concatenate([g_final, g_norm_c], axis=0)),
        ("vec1024", jnp.concatenate([g_d, g_bgate, pad_to(loss_acc[:, :1], 1, PACK_COLS)], axis=0)),
        ("lam3", jnp.concatenate([lane(g_lam_re), lane(g_lam_im), lane(g_log_dt)], axis=1)),
        ("s5_b_re", g_b_re), ("s5_b_im", g_b_im),
        ("s5_c_re", g_c_re.reshape(S5_G, S5_GH * S5_P)), ("s5_c_im", g_c_im.reshape(S5_G, S5_GH * S5_P)),
        ("ret_norm_w", g_ret_norm.reshape(RET_H, RET_DV)), ("gla_norm_w", g_gla_norm.reshape(GLA_H, GLA_DV)),
        ("gla_w_gate", g_wgate)]
    gw_in_ab, (early_all,) = _mm(
        hn0, dproj_ab, "tn", name="in_ab_dw", out_dest=True, out_dtype=BF16,
        hook=_gather_hook(pack_small(early_pieces)))
    p1_first, p1_first_bf = rs_front(dict(w_in_ab=gw_in_ab), FIRST_LAYOUT, "first")
    dhn0, (from_chips_first,) = _mm(dproj_ab, w_in_ab_g, "nt", name="in_ab_dx", b_dev=True,
                                    hook=_chips_hook(p1_first_bf))
    def norm_bwd_first(i, h, dhn, dres, w):
        (dh,), (dw,) = norm_bwd(i, h, dhn, dres, w)
        return (dh,), (dw, dh * (i == 0).astype(F32))

    (grad_x2d,), (g_norm_ab, dh0_first) = _rows(
        norm_bwd_first, [_win(h0), _win(dhn0), _win(dh1)], [norm_ab_w], [(D_MODEL, F32, 1)],
        [(1, D_MODEL), (CHUNK, D_MODEL)], name="norm_ab_bwd", nrow=lp)
    grad_x = grad_x2d[None]
    late_pieces = [("norm_ab_w", g_norm_ab), ("meta", dh0_first[PAD:CHUNK])]
    small = sum_small(early_all, early_pieces, "early")
    small.update(sum_small(_all_gather(pack_small(late_pieces), "gather_grads"), late_pieces, "late"))

    big_grads = {**rs_back(p1_c, from_chips_c, IN_C_LAYOUT, "in_c"), **rs_back(p1_o, from_chips_o, OTHER_LAYOUT, "other"),
                 **rs_back(p1_first, from_chips_first, FIRST_LAYOUT, "first")}
    big_grads["w_in_c"] = _lane_select(big_grads["w_in_c"], win_off, -1, 896, F32, True,
                                       "w_in_c_from_window")[:, :SHARD_C]
    small["final_norm_w"], small["norm_c_w"] = small["vec2048"][0:1], small["vec2048"][1:2]
    small["s5_d"], small["gla_b_gate"] = small["vec1024"][0:1], small["vec1024"][1:2]
    loss = small["vec1024"][2, 0]
    small["s5_lam_re"], small["s5_lam_im"] = small["lam3"][:, :S5_P], small["lam3"][:, 128:128 + S5_P]
    small["s5_log_dt"] = small["lam3"][:, 256:257]

    def my_cols(g, n):
        return lax.dynamic_slice_in_dim(g, dev * n, n, axis=g.ndim - 1)

    grads = dict(
        meta=my_cols(small["meta"], D_MODEL // N_DEV),
        norm_ab_w=small["norm_ab_w"], w_in_ab=big_grads["w_in_ab"][None], ret_norm_w=small["ret_norm_w"].reshape(1, RET_W),
        s5_lam_re=small["s5_lam_re"][None], s5_lam_im=small["s5_lam_im"][None],
        s5_log_dt=small["s5_log_dt"].reshape(1, S5_G),
        s5_b_re=small["s5_b_re"].reshape(1, S5_G, S5_P, S5_GH), s5_b_im=small["s5_b_im"].reshape(1, S5_G, S5_P, S5_GH),
        s5_c_re=small["s5_c_re"][None], s5_c_im=small["s5_c_im"][None], s5_d=small["s5_d"],
        s5_w_glu=big_grads["s5_w_glu"][None], w_out_ab=big_grads["w_out_ab"][None],
        norm_c_w=my_cols(small["norm_c_w"], D_MODEL // N_DEV), w_in_c=big_grads["w_in_c"][None],
        gla_w_gate=my_cols(small["gla_w_gate"], GLA_QK // N_DEV)[None],
        gla_b_gate=my_cols(small["gla_b_gate"], GLA_QK // N_DEV),
        gla_norm_w=my_cols(small["gla_norm_w"].reshape(1, GLA_W), GLA_W // N_DEV),
        w_out_c=big_grads["w_out_c"][None], final_norm_w=small["final_norm_w"].reshape(D_MODEL))

    deltas, new_m, new_v = {}, {}, {}
    for k in order:
        w = weights[k]
        d2, m2, v2 = _adamw(_as2d(w), _as2d(grads[k].reshape(w.shape)), _as2d(mom_m[k]), _as2d(mom_v[k]), "adamw_" + k)
        deltas[k], new_m[k], new_v[k] = d2.reshape(w.shape), m2.reshape(w.shape), v2.reshape(w.shape)
        grads[k] = grads[k].reshape(w.shape)

    return (loss, grad_x, *[grads[k] for k in order], *[deltas[k] for k in order],
            *[new_m[k] for k in order], *[new_v[k] for k in order])
```

```python
import functools
import math

import jax
import jax.numpy as jnp
from jax import lax
from jax.experimental import pallas as pl
from jax.experimental.pallas import tpu as pltpu

F32, BF16 = jnp.float32, jnp.bfloat16
MESH = pl.DeviceIdType.MESH
N_DEV = 8

D_MODEL = 2048
CHUNK = 128
N_META = 16
PAD = CHUNK - N_META
SUB = 16
EPS = 1e-6
RET_H, RET_DK, RET_DV = 8, 128, 256
RET_QK, RET_W = RET_H * RET_DK, RET_H * RET_DV
ROPE_BASE = 10000.0
S5_W, S5_G, S5_P, S5_GH = 1024, 64, 64, 16
S5_N = S5_G * S5_P
GLA_H, GLA_DK, GLA_DV, GLA_RANK, GLA_TAU = 4, 256, 512, 16, 16.0
GLA_QK, GLA_W = GLA_H * GLA_DK, GLA_H * GLA_DV
IN_AB = 2 * RET_QK + 2 * RET_W + 2 * S5_W
OUT_AB = RET_W + S5_W
IN_C = 2 * GLA_QK + 2 * GLA_W + GLA_RANK
GATE_PAD = 256
IN_C_PAD = 2 * GLA_QK + 2 * GLA_W + GATE_PAD
ADAM_LR, ADAM_B1, ADAM_B2, ADAM_EPS, ADAM_WD, ADAM_STEP = 0.001, 0.9, 0.999, 1e-08, 0.01, 10

VMEM_LIMIT_BYTES = 48 * 2 ** 20
PACK_COLS = 1024
SHARD_C = IN_C // N_DEV
WIN_STEP = 768
WIN_COLS = 1024


def _params(sem):
    return pltpu.CompilerParams(dimension_semantics=sem, vmem_limit_bytes=VMEM_LIMIT_BYTES)


def _tile(n, cap, mult):
    best = None
    for t in range(mult, min(n, cap) + 1, mult):
        if n % t == 0:
            best = t
    assert best is not None, (n, cap, mult)
    return best


def _dg(a, b, ca, cb):
    return lax.dot_general(a.astype(BF16), b.astype(BF16), (((ca,), (cb,)), ((), ())),
                           preferred_element_type=F32)


@functools.partial(jax.custom_vjp, nondiff_argnums=(2, 3))
def _bdot(a, b, ca, cb):
    return _dg(a, b, ca, cb)


def _bdot_fwd(a, b, ca, cb):
    return _dg(a, b, ca, cb), (a, b)


def _bdot_bwd(ca, cb, res, g):
    a, b = res
    da = _dg(g, b, 1, 1 - cb) if ca == 1 else _dg(b, g, 1 - cb, 1)
    db = _dg(a, g, 1 - ca, 0) if cb == 0 else _dg(g, a, 0, 1 - ca)
    return da.astype(a.dtype), db.astype(b.dtype)


_bdot.defvjp(_bdot_fwd, _bdot_bwd)


def _sigmoid(x):
    return 1.0 / (1.0 + jnp.exp(-x))


def _silu(x):
    return x * _sigmoid(x)


def _log_sigmoid(x):
    return jnp.minimum(x, 0.0) - jnp.log(1.0 + jnp.exp(-jnp.abs(x)))


def _gelu(x):
    return 0.5 * x * (1.0 + jnp.tanh(math.sqrt(2.0 / math.pi) * (x + 0.044715 * (x * x * x))))


def _rms(x, w):
    return x * lax.rsqrt(jnp.mean(x * x, axis=-1, keepdims=True) + EPS) * w


class _Hook:
    def __init__(self, ins, outs, sems, phases):
        self.ins, self.outs, self.sems, self.phases = list(ins), list(outs), list(sems), list(phases)


_NO_HOOK = _Hook([], [], [], [])
_ANY = pl.BlockSpec(memory_space=pl.ANY)


def _run_hook(hook, lin, total, in_refs, out_refs, sem_refs):
    for frac, fn in hook.phases:
        at = min(int(frac * total), total - 1)

        @pl.when(lin == at)
        def _(fn=fn):
            fn(in_refs, out_refs, sem_refs)


def _mm_core(a, b, *, dims, grid, a_spec, b_spec, o_spec, out_shape, acc_shape, name, extra=(), hook=None,
             b_parts=0):
    nk = grid[2]
    n_extra = len(extra)
    hook = _NO_HOOK if hook is None else hook
    hi, ho = len(hook.ins), len(hook.outs)

    def body(*refs):
        a_ref, b_ref = refs[0], refs[1]
        o_ref, acc = refs[2 + n_extra + hi], refs[3 + n_extra + hi + ho]
        k = pl.program_id(2)
        lin = (pl.program_id(0) * grid[1] + pl.program_id(1)) * nk + k
        _run_hook(hook, lin, grid[0] * grid[1] * nk, refs[2 + n_extra:2 + n_extra + hi],
                  refs[3 + n_extra + hi:3 + n_extra + hi + ho], refs[4 + n_extra + hi + ho:])

        if b_parts:
            part = sum(lax.dot_general(a_ref[:, d * PACK_COLS:(d + 1) * PACK_COLS].astype(BF16), b_ref[d].astype(BF16),
                                       dims, preferred_element_type=F32) for d in range(b_parts))
        else:
            part = lax.dot_general(a_ref[...].astype(BF16), b_ref[...].astype(BF16), dims, preferred_element_type=F32)

        def finish(r):
            for e in range(n_extra):
                r = r + refs[2 + e][...].astype(F32)
            o_ref[...] = r.astype(o_ref.dtype)

        if nk == 1:
            finish(part)
        else:
            @pl.when(k == 0)
            def _():
                acc[...] = part

            @pl.when(k > 0)
            def _():
                acc[...] += part

            @pl.when(k == nk - 1)
            def _():
                finish(acc[...])

    res = pl.pallas_call(
        body, name=name, grid=grid,
        in_specs=[a_spec, b_spec] + [sp for _, sp in extra] + [_ANY] * hi,
        out_specs=[o_spec] + [_ANY] * ho, out_shape=[out_shape] + hook.outs,
        scratch_shapes=[pltpu.VMEM(acc_shape if nk > 1 else (8, 128), F32)] + hook.sems,
        compiler_params=_params(("arbitrary", "arbitrary", "arbitrary")),
    )(a, b, *[arr for arr, _ in extra], *hook.ins)
    return res[0] if hook is _NO_HOOK else (res[0], res[1:])


NN, NT, TN = (((1,), (0,)), ((), ())), (((1,), (1,)), ((), ())), (((0,), (0,)), ((), ()))


FULL_K = 2048


def _mm(a, b, mode, *, name, out_dtype=F32, a_win=None, add=None, bias=None, hook=None, b_dev=False,
        out_dest=False):
    b_parts = 0
    if mode == "tn":
        kdim, n = a.shape[0], b.shape[1]
        m = a.shape[1] if a_win is None else a_win[1]
        tm, tn, tk = _tile(m, 512, 128), _tile(n, 640, 128), kdim
        off = 0 if a_win is None else a_win[0] // tm
        a_spec = pl.BlockSpec((tk, tm), lambda i, j, k: (k, i + off))
        b_spec = pl.BlockSpec((tk, tn), lambda i, j, k: (k, j))
        dims = TN
    else:
        m = a.shape[0]
        kdim = a.shape[1] if a_win is None else a_win[1]
        if b_dev:
            n = b.shape[0] * b.shape[2] if mode == "nn" else b.shape[1]
        else:
            n = b.shape[1] if mode == "nn" else b.shape[0]
        if FULL_K < kdim <= 2 * FULL_K and not b_dev:
            tm, tn, tk = _tile(m, 1408, 8), _tile(n, 1024, 128), _tile(kdim, 1024, 128)
        else:
            tm = _tile(m, 1408 if kdim <= FULL_K else 352, 8)
            tn, tk = _tile(n, 640, 128), kdim
        off = 0 if a_win is None else a_win[0] // tk
        a_spec = pl.BlockSpec((tm, tk), lambda i, j, k: (i, k + off))
        if mode == "nn":
            dims = NN
            if b_dev:
                per = PACK_COLS // tn
                b_spec = pl.BlockSpec((None, tk, tn), lambda i, j, k: (j // per, k, j % per))
            else:
                b_spec = pl.BlockSpec((tk, tn), lambda i, j, k: (k, j))
        else:
            dims = NT
            if b_dev:
                b_parts = kdim // PACK_COLS
                b_spec = pl.BlockSpec((b_parts, tn, PACK_COLS), lambda i, j, k: (0, j, 0))
            else:
                b_spec = pl.BlockSpec((tn, tk), lambda i, j, k: (j, k))
    if a_win is not None:
        assert a_win[0] % (tm if mode == "tn" else tk) == 0
    extra = []
    if add is not None:
        extra.append((add, pl.BlockSpec((tm, tn), lambda i, j, k: (i, j))))
    if bias is not None:
        extra.append((bias, pl.BlockSpec((1, tn), lambda i, j, k: (0, j))))
    if out_dest:
        per = PACK_COLS // tn
        o_spec = pl.BlockSpec((None, tm, tn), lambda i, j, k: (j // per, i, j % per))
        out_shape = jax.ShapeDtypeStruct((n // PACK_COLS, m, PACK_COLS), out_dtype)
    else:
        o_spec = pl.BlockSpec((tm, tn), lambda i, j, k: (i, j))
        out_shape = jax.ShapeDtypeStruct((m, n), out_dtype)
    return _mm_core(a, b, dims=dims, grid=(m // tm, n // tn, kdim // tk), a_spec=a_spec, b_spec=b_spec,
                    o_spec=o_spec, out_shape=out_shape, acc_shape=(tm, tn), name=name, extra=extra, hook=hook,
                    b_parts=b_parts)


def _win(arr, col0=0, width=None, roff=0):
    return (arr, col0, arr.shape[1] if width is None else width, roff)


def _rows(fn, rows, consts, outs, accs, *, name, nrow, tr=CHUNK):
    nr, nc, no = len(rows), len(consts), len(outs)

    def body(*refs):
        i = pl.program_id(0)
        ins = [r[...] for r in refs[:nr + nc]]
        o_refs = refs[nr + nc:nr + nc + no]
        a_refs = refs[nr + nc + no:]
        res_o, res_a = fn(i, *ins)
        for r, v in zip(o_refs, res_o):
            r[...] = v.astype(r.dtype)
        if a_refs:
            @pl.when(i == 0)
            def _():
                for r in a_refs:
                    r[...] = jnp.zeros_like(r)

            for r, v in zip(a_refs, res_a):
                r[...] += v

    in_specs = []
    for (arr, col0, width, roff) in rows:
        assert col0 % width == 0 and arr.shape[0] % tr == 0
        in_specs.append(pl.BlockSpec((tr, width), lambda i, c=col0 // width, ro=roff: (jnp.maximum(i - ro, 0), c)))
    for c in consts:
        in_specs.append(pl.BlockSpec(c.shape, lambda i, nd=c.ndim: (0,) * nd))
    outs = [tuple(o) + (0,) * (3 - len(o)) for o in outs]
    out_specs = [pl.BlockSpec((tr, w), lambda i, ro=ro: (jnp.maximum(i - ro, 0), 0)) for (w, _, ro) in outs]
    out_specs += [pl.BlockSpec(s, lambda i, nd=len(s): (0,) * nd) for s in accs]
    out_shape = [jax.ShapeDtypeStruct((nrow - ro * tr, w), dt) for (w, dt, ro) in outs]
    out_shape += [jax.ShapeDtypeStruct(s, F32) for s in accs]
    res = pl.pallas_call(
        body, name=name, grid=(nrow // tr,), in_specs=in_specs, out_specs=out_specs, out_shape=out_shape,
        compiler_params=_params(("arbitrary",)),
    )(*[r[0] for r in rows], *consts)
    return res[:no], res[no:]


HEADS_PER_STEP = 2


def _scan_specs(xs, cs, ws, ks, chunk_of, hpb):
    specs = []
    for (arr, width, colfn) in xs:
        specs.append(pl.BlockSpec((CHUNK, width * hpb), lambda h, n, f=colfn: (chunk_of(n), f(h * hpb) // hpb)))
    for (arr, width, colfn) in cs:
        specs.append(pl.BlockSpec((CHUNK, width), lambda h, n, f=colfn: (chunk_of(n), f(h))))
    for arr in list(ws) + list(ks):
        specs.append(pl.BlockSpec((hpb, 1, arr.shape[2]), lambda h, n: (h, 0, 0)))
    return specs


def _scan_fwd(fn, xs, cs, ws, ks, *, heads, nchunk, s_shape, out_w, name, pre=None, hook=None,
              hpb=HEADS_PER_STEP):
    nx, ncs, nw = len(xs), len(cs), len(ws)
    hook = _NO_HOOK if hook is None else hook
    hi, ho = len(hook.ins), len(hook.outs)
    hblocks = heads // hpb

    def body(*refs):
        n = pl.program_id(1)
        nin = nx + ncs + nw + len(ks)
        y_ref, sp_ref = refs[nin + hi], refs[nin + hi + 1]
        s_scr = refs[nin + hi + 2 + ho]
        _run_hook(hook, pl.program_id(0) * nchunk + n, hblocks * nchunk, refs[nin:nin + hi],
                  refs[nin + hi + 2:nin + hi + 2 + ho], refs[nin + hi + 3 + ho:])

        @pl.when(n == 0)
        def _():
            s_scr[...] = jnp.zeros_like(s_scr)

        cv = [r[...] for r in refs[nx:nx + ncs]]
        for e in range(hpb):
            state = s_scr[e]
            sp_ref[e, 0] = state
            xv = [r[:, e * w:(e + 1) * w] for r, (_, w, _) in zip(refs[:nx], xs)]
            wv = [r[e] for r in refs[nx + ncs:nx + ncs + nw]]
            kv = [r[e] for r in refs[nx + ncs + nw:nin]]
            if pre is not None:
                xv = pre(xv, cv)
            y, s_new = fn(n, xv, state, cv, wv, kv)
            y_ref[:, e * out_w:(e + 1) * out_w] = y.astype(y_ref.dtype)
            s_scr[e] = s_new

    lp = nchunk * CHUNK
    res = pl.pallas_call(
        body, name=name, grid=(hblocks, nchunk),
        in_specs=_scan_specs(xs, cs, ws, ks, lambda n: n, hpb) + [_ANY] * hi,
        out_specs=[pl.BlockSpec((CHUNK, out_w * hpb), lambda h, n: (n, h)),
                   pl.BlockSpec((hpb, 1) + s_shape, lambda h, n: (h, n, 0, 0))] + [_ANY] * ho,
        out_shape=[jax.ShapeDtypeStruct((lp, heads * out_w), BF16),
                   jax.ShapeDtypeStruct((heads, nchunk) + s_shape, F32)] + hook.outs,
        scratch_shapes=[pltpu.VMEM((hpb,) + s_shape, F32)] + hook.sems,
        compiler_params=_params(("arbitrary", "arbitrary")),
    )(*[t[0] for t in xs], *[t[0] for t in cs], *ws, *ks, *hook.ins)
    return (res[0], res[1]) if hook is _NO_HOOK else (res[0], res[1], res[2:])


def _scan_bwd(fn, xs, cs, ws, ks, dy, sprev, *, heads, nchunk, s_shape, out_w, name, pre=None, post=None,
              hook=None, hpb=HEADS_PER_STEP):
    nx, ncs, nw = len(xs), len(cs), len(ws)
    nin = nx + ncs + nw + len(ks)
    hook = _NO_HOOK if hook is None else hook
    hi, ho = len(hook.ins), len(hook.outs)
    hblocks = heads // hpb

    def body(*refs):
        step = pl.program_id(1)
        n = nchunk - 1 - step
        dy_ref, sp_ref = refs[nin], refs[nin + 1]
        o0 = nin + 2 + hi
        dx_refs = refs[o0:o0 + nx]
        dw_refs = refs[o0 + nx:o0 + nx + nw]
        ds_scr = refs[o0 + nx + nw + ho]
        _run_hook(hook, pl.program_id(0) * nchunk + step, hblocks * nchunk, refs[nin + 2:o0],
                  refs[o0 + nx + nw:o0 + nx + nw + ho], refs[o0 + nx + nw + ho + 1:])

        @pl.when(step == 0)
        def _():
            ds_scr[...] = jnp.zeros_like(ds_scr)
            for r in dw_refs:
                r[...] = jnp.zeros_like(r)

        cv = [r[...] for r in refs[nx:nx + ncs]]
        for e in range(hpb):
            xv = [r[:, e * w:(e + 1) * w] for r, (_, w, _) in zip(refs[:nx], xs)]
            wv = [r[e] for r in refs[nx + ncs:nx + ncs + nw]]
            kv = [r[e] for r in refs[nx + ncs + nw:nin]]
            if pre is not None:
                xv = pre(xv, cv)
            _, vjp = jax.vjp(lambda xs_, s_, ws_, kv=kv: fn(n, xs_, s_, cv, ws_, kv), xv, sp_ref[e, 0], wv)
            dxs, ds_prev, dws = vjp((dy_ref[:, e * out_w:(e + 1) * out_w].astype(F32), ds_scr[e]))
            if post is not None:
                dxs = post(dxs, cv)
            for r, v, (_, w, _) in zip(dx_refs, dxs, xs):
                r[:, e * w:(e + 1) * w] = v.astype(r.dtype)
            for r, v in zip(dw_refs, dws):
                r[e] += v
            ds_scr[e] = ds_prev

    lp = nchunk * CHUNK
    rev = lambda n: nchunk - 1 - n
    in_specs = _scan_specs(xs, cs, ws, ks, rev, hpb)
    in_specs.append(pl.BlockSpec((CHUNK, out_w * hpb), lambda h, n: (rev(n), h)))
    in_specs.append(pl.BlockSpec((hpb, 1) + s_shape, lambda h, n: (h, rev(n), 0, 0)))
    out_specs = [pl.BlockSpec((CHUNK, w * hpb), lambda h, n: (rev(n), h)) for (_, w, _) in xs]
    out_specs += [pl.BlockSpec((hpb, 1, w.shape[2]), lambda h, n: (h, 0, 0)) for w in ws]
    out_shape = [jax.ShapeDtypeStruct((lp, heads * w), BF16) for (_, w, _) in xs]
    out_shape += [jax.ShapeDtypeStruct(w.shape, F32) for w in ws]
    res = pl.pallas_call(
        body, name=name, grid=(hblocks, nchunk), in_specs=in_specs + [_ANY] * hi,
        out_specs=out_specs + [_ANY] * ho, out_shape=out_shape + hook.outs,
        scratch_shapes=[pltpu.VMEM((hpb,) + s_shape, F32)] + hook.sems,
        compiler_params=_params(("arbitrary", "arbitrary")),
    )(*[t[0] for t in xs], *[t[0] for t in cs], *ws, *ks, dy, sprev, *hook.ins)
    if hook is _NO_HOOK:
        return res[:nx], res[nx:]
    return res[:nx], res[nx:nx + nw], res[nx + nw:]


def _iota2(shape, dim):
    return lax.broadcasted_iota(jnp.int32, shape, dim)


def _ret_chunk(n, xs, state, cs, ws, ks):
    q, k, v, z = xs
    (w,), (lg,) = ws, ks
    lgc = lg[:, :1]
    row, col = _iota2((CHUNK, CHUNK), 0), _iota2((CHUNK, CHUNK), 1)
    diff = jnp.maximum(row - col, 0).astype(F32)
    decay = jnp.where(row >= col, jnp.exp(lg * diff), 0.0)
    scores = _bdot(q, k, 1, 1) * decay
    o_intra = _bdot(scores, v, 1, 0)
    idx = _iota2((CHUNK, 1), 0).astype(F32)
    k_w = k * jnp.exp(lgc * (CHUNK - 1.0 - idx))
    kv = _bdot(k_w, v, 0, 0)
    s_new = state * jnp.exp(lgc * float(CHUNK)) + kv
    q_w = q * jnp.exp(lgc * (idx + 1.0))
    o = o_intra + _bdot(q_w, state, 1, 0)
    return _rms(o, w) * _silu(z), s_new


def _rope(t, cos2, sin2):
    return t * cos2 + pltpu.roll(t, RET_DK // 2, 1) * sin2


def _rope_t(g, cos2, sin2):
    return g * cos2 - pltpu.roll(g, RET_DK // 2, 1) * sin2


def _ret_pre(xv, cv):
    q, k, v, z = xv
    cos2, sin2 = cv
    return [_rope(q, cos2, sin2), _rope(k, cos2, sin2) * (RET_DK ** -0.5), v, z]


def _ret_post(dxs, cv):
    dq, dk, dv, dz = dxs
    cos2, sin2 = cv
    return [_rope_t(dq, cos2, sin2), _rope_t(dk, cos2, sin2) * (RET_DK ** -0.5), dv, dz]


def _tri_apply(x, lower):
    n = x.shape[0]
    row, col = _iota2((n, n), 0), _iota2((n, n), 1)
    tri = (row >= col if lower else row <= col).astype(BF16)
    hi = x.astype(BF16)
    rest = x - hi.astype(F32)
    mid = rest.astype(BF16)
    lo = (rest - mid.astype(F32)).astype(BF16)
    return sum(lax.dot_general(tri, p, NN, preferred_element_type=F32) for p in (hi, mid, lo))


@jax.custom_vjp
def _cumsum_rows(x):
    return _tri_apply(x, True)


_cumsum_rows.defvjp(lambda x: (_tri_apply(x, True), None), lambda _, g: (_tri_apply(g, False),))


def _cumsum_rows_f32(x):
    n = x.shape[0]
    tri = (_iota2((n, n), 0) >= _iota2((n, n), 1)).astype(F32)
    return jnp.dot(tri, x, precision=lax.Precision.HIGHEST, preferred_element_type=F32)


def _gla_chunk(n, xs, state_t, cs, ws, ks, cumsum=_cumsum_rows):
    q, k, v, z, pre = xs
    (w,) = ws
    q = q * (GLA_DK ** -0.5)
    rowc = _iota2((CHUNK, 1), 0)
    valid = jnp.logical_or(n > 0, rowc >= PAD)
    log_a = jnp.where(valid, _log_sigmoid(pre) / GLA_TAU, 0.0)
    b = cumsum(log_a)
    b_last = b[CHUNK - 1:CHUNK, :]
    kv_t = _bdot(v, k * jnp.exp(b_last - b), 0, 0)
    s_new = state_t * jnp.exp(b_last) + kv_t
    o_inter = _bdot(q * jnp.exp(b), state_t, 1, 1)
    outs = []
    for s in range(CHUNK // SUB):
        lo, hi = s * SUB, (s + 1) * SUB
        b_ref = jnp.zeros_like(b_last) if s == 0 else b[lo - 1:lo, :]
        q_hat = q[lo:hi] * jnp.exp(b[lo:hi] - b_ref)
        k_hat = k[:hi] * jnp.exp(b_ref - b[:hi])
        sc = _bdot(q_hat, k_hat, 1, 1)
        causal = _iota2((SUB, hi), 0) + lo >= _iota2((SUB, hi), 1)
        outs.append(_bdot(jnp.where(causal, sc, 0.0), v[:hi], 1, 0))
    o = jnp.concatenate(outs, axis=0) + o_inter
    return _rms(o, w) * _silu(z), s_new


def _s5_disc(lam_re, lam_im, log_dt, b_re, b_im, expand):
    dt = jnp.exp(log_dt)
    mag = jnp.exp(lam_re * dt)
    ab_re, ab_im = mag * jnp.cos(lam_im * dt), mag * jnp.sin(lam_im * dt)
    den = lam_re * lam_re + lam_im * lam_im
    nr, ni = ab_re - 1.0, ab_im
    f_re = (nr * lam_re + ni * lam_im) / den
    f_im = (ni * lam_re - nr * lam_im) / den
    hp = lax.Precision.HIGHEST
    f_re = jnp.dot(f_re, expand, precision=hp, preferred_element_type=F32)
    f_im = jnp.dot(f_im, expand, precision=hp, preferred_element_type=F32)
    return ab_re, ab_im, f_re * b_re - f_im * b_im, f_re * b_im + f_im * b_re


def _s5_disc_fwd(args):
    def body(*refs):
        outs = _s5_disc(*[r[...] for r in refs[:6]])
        for r, v in zip(refs[6:], outs):
            r[...] = v

    g, p = args[0].shape
    return pl.pallas_call(
        body, name="s5_disc_fwd",
        out_shape=[jax.ShapeDtypeStruct((g, p), F32)] * 2 + [jax.ShapeDtypeStruct(args[3].shape, F32)] * 2,
    )(*args)


def _s5_disc_bwd(args, cts):
    def body(*refs):
        prim = [r[...] for r in refs[:5]]
        expand = refs[5][...]
        ct = tuple(r[...] for r in refs[6:10])
        _, vjp = jax.vjp(lambda *a: _s5_disc(*a, expand), *prim)
        for r, v in zip(refs[10:], vjp(ct)):
            r[...] = v

    return pl.pallas_call(
        body, name="s5_disc_bwd", out_shape=[jax.ShapeDtypeStruct(a.shape, F32) for a in args[:5]],
    )(*args, *cts)


SCAN_ROWS, SCAN_LANES = 32, 128
TILE_G = 8
TILE_W = TILE_G * S5_P
S5_TB = 128


def _s5_scan_fwd(bu, a_re, a_im):
    lp = bu.shape[0]

    def body(bu_ref, ar_ref, ai_ref, x_ref, st):
        @pl.when(pl.program_id(0) == 0)
        def _():
            st[...] = jnp.zeros_like(st)

        ar, ai = ar_ref[...], ai_ref[...]

        def step(t, carry):
            xr, xi = carry
            nr = ar * xr - ai * xi + bu_ref[t, 0:SCAN_ROWS, :]
            ni = ar * xi + ai * xr + bu_ref[t, SCAN_ROWS:2 * SCAN_ROWS, :]
            x_ref[t, 0:SCAN_ROWS, :] = nr
            x_ref[t, SCAN_ROWS:2 * SCAN_ROWS, :] = ni
            return nr, ni

        xr, xi = lax.fori_loop(0, S5_TB, step, (st[0], st[1]))
        st[0] = xr
        st[1] = xi

    blk = pl.BlockSpec((S5_TB, 2 * SCAN_ROWS, SCAN_LANES), lambda i: (i, 0, 0))
    cst = pl.BlockSpec((SCAN_ROWS, SCAN_LANES), lambda i: (0, 0))
    return pl.pallas_call(
        body, name="s5_scan_fwd", grid=(lp // S5_TB,), in_specs=[blk, cst, cst], out_specs=blk,
        out_shape=jax.ShapeDtypeStruct(bu.shape, F32),
        scratch_shapes=[pltpu.VMEM((2, SCAN_ROWS, SCAN_LANES), F32)],
        compiler_params=_params(("arbitrary",)),
    )(bu, a_re, a_im)


def _s5_expand(a, w_t, *, a_blk, dims, name, hook=None):
    lp, nt = a.shape[0], w_t.shape[0]
    tm = _tile(lp, 176, 8)
    steps = lp // tm
    rows3 = 2 * SCAN_ROWS
    per = TILE_W // SCAN_LANES
    hook = _NO_HOOK if hook is None else hook
    hi, ho = len(hook.ins), len(hook.outs)

    def body(*refs):
        a_ref, w_ref, o_ref = refs[0], refs[1], refs[2 + hi]
        _run_hook(hook, pl.program_id(0), steps, refs[2:2 + hi], refs[3 + hi:3 + hi + ho], refs[3 + hi + ho:])
        for j in range(nt):
            s = j % TILE_G
            r = lax.dot_general(a_ref[:, 128 * s:128 * (s + 1)].astype(BF16), w_ref[j], dims,
                                preferred_element_type=F32)
            for c in range(per):
                o_ref[pl.ds(per * j + c, tm, stride=rows3), :] = r[:, SCAN_LANES * c:SCAN_LANES * (c + 1)]

    res = pl.pallas_call(
        body, name=name, grid=(steps,),
        in_specs=[pl.BlockSpec((tm, S5_W), lambda i: (i, a_blk)), pl.BlockSpec(w_t.shape, lambda i: (0, 0, 0))]
        + [_ANY] * hi,
        out_specs=[pl.BlockSpec((tm * rows3, SCAN_LANES), lambda i: (i, 0))] + [_ANY] * ho,
        out_shape=[jax.ShapeDtypeStruct((lp * rows3, SCAN_LANES), F32)] + hook.outs,
        scratch_shapes=hook.sems, compiler_params=_params(("arbitrary",)),
    )(a, w_t, *hook.ins)
    out3 = res[0].reshape(lp, rows3, SCAN_LANES)
    return out3 if hook is _NO_HOOK else (out3, res[1:])


def _s5_scan_bwd(gx, x, a_re, a_im):
    lp = gx.shape[0]
    nb = lp // S5_TB

    def body(gx_ref, x_ref, xp_ref, ar_ref, ai_ref, g_ref, da_ref, st):
        i = pl.program_id(0)

        @pl.when(i == 0)
        def _():
            st[...] = jnp.zeros_like(st)
            da_ref[...] = jnp.zeros_like(da_ref)

        ar, ai = ar_ref[...], ai_ref[...]
        first = (i == nb - 1).astype(F32)

        def step(s, carry):
            gr, gi, dar, dai = carry
            t = S5_TB - 1 - s
            ngr = gx_ref[t, 0:SCAN_ROWS, :] + ar * gr + ai * gi
            ngi = gx_ref[t, SCAN_ROWS:2 * SCAN_ROWS, :] + ar * gi - ai * gr
            g_ref[t, 0:SCAN_ROWS, :] = ngr
            g_ref[t, SCAN_ROWS:2 * SCAN_ROWS, :] = ngi
            tp = jnp.maximum(t - 1, 0)
            at0 = (t == 0).astype(F32)
            keep = 1.0 - at0
            pr = keep * x_ref[tp, 0:SCAN_ROWS, :] + at0 * (1.0 - first) * xp_ref[0, 0:SCAN_ROWS, :]
            pi = keep * x_ref[tp, SCAN_ROWS:2 * SCAN_ROWS, :] + at0 * (1.0 - first) * xp_ref[0, SCAN_ROWS:2 * SCAN_ROWS, :]
            return ngr, ngi, dar + ngr * pr + ngi * pi, dai + ngi * pr - ngr * pi

        zero = jnp.zeros((SCAN_ROWS, SCAN_LANES), F32)
        gr, gi, dar, dai = lax.fori_loop(0, S5_TB, step, (st[0], st[1], zero, zero))
        st[0] = gr
        st[1] = gi
        da_ref[0] += dar
        da_ref[1] += dai

    rev = lambda i: nb - 1 - i
    blk = pl.BlockSpec((S5_TB, 2 * SCAN_ROWS, SCAN_LANES), lambda i: (rev(i), 0, 0))
    prev = pl.BlockSpec((1, 2 * SCAN_ROWS, SCAN_LANES), lambda i: (jnp.maximum(rev(i) * S5_TB - 1, 0), 0, 0))
    cst = pl.BlockSpec((SCAN_ROWS, SCAN_LANES), lambda i: (0, 0))
    return pl.pallas_call(
        body, name="s5_scan_bwd", grid=(nb,), in_specs=[blk, blk, prev, cst, cst],
        out_specs=[blk, pl.BlockSpec((2, SCAN_ROWS, SCAN_LANES), lambda i: (0, 0, 0))],
        out_shape=[jax.ShapeDtypeStruct(gx.shape, F32), jax.ShapeDtypeStruct((2, SCAN_ROWS, SCAN_LANES), F32)],
        scratch_shapes=[pltpu.VMEM((2, SCAN_ROWS, SCAN_LANES), F32)],
        compiler_params=_params(("arbitrary",)),
    )(gx, x, x, a_re, a_im)


def _place():
    x, y, c = lax.axis_index("x"), lax.axis_index("y"), lax.axis_index("c")
    return x, y, c, [(1 - x, y), (x, 1 - y), (1 - x, 1 - y)]


def _gather_phases(nrows):
    half = (nrows // 32) * 16
    assert 0 < half < nrows

    def plan(x_ref, out_ref, send_sems, recv_sems, local_sem):
        x, y, c, _ = _place()
        me, sibling = (x, y, c), (x, y, 1 - c)
        at_x, at_y, at_d = (1 - x, y, c), (x, 1 - y, c), (1 - x, 1 - y, c)

        def rows(block, part=None):
            idx = 4 * block[0] + 2 * block[1] + block[2]
            if part is None:
                return out_ref.at[idx]
            return out_ref.at[idx, pl.ds(0, half)] if part == 0 else out_ref.at[idx, pl.ds(half, nrows - half)]

        def copy(k, block, to, src=None, part=None):
            return pltpu.make_async_remote_copy(
                src_ref=rows(block, part) if src is None else src, dst_ref=rows(block, part),
                send_sem=send_sems.at[k], recv_sem=recv_sems.at[k], device_id=to, device_id_type=MESH)

        def other_core(block):
            return (block[0], block[1], 1 - c)

        mine = pltpu.make_async_copy(x_ref, rows(me), local_sem)
        direct = [copy(0, me, sibling, src=x_ref), copy(1, me, at_x, src=x_ref), copy(2, me, at_y, src=x_ref)]
        relays = [copy(3, at_x, at_y, part=0), copy(7, at_y, at_x, part=1)]
        passed = [copy(4, at_x, sibling), copy(5, at_y, sibling), copy(6, at_d, sibling)]
        landed = [copy(1, at_x, me), copy(2, at_y, me), copy(3, at_d, me, part=0), copy(7, at_d, me, part=1)]
        from_sibling = [copy(0, sibling, me)] + [copy(4 + j, other_core(b), me) for j, b in enumerate((at_x, at_y, at_d))]
        return mine, direct, relays, passed, landed, from_sibling

    def start(ins, outs, sems):
        mine, direct, _, _, _, _ = plan(ins[0], outs[0], *sems)
        mine.start()
        for cp in direct:
            cp.start()

    def middle(ins, outs, sems):
        _, _, relays, passed, landed, _ = plan(ins[0], outs[0], *sems)
        for j in range(2):
            landed[j].wait_recv()
            passed[j].start()
            relays[j].start()

    def late(ins, outs, sems):
        _, _, _, passed, landed, _ = plan(ins[0], outs[0], *sems)
        landed[2].wait_recv()
        landed[3].wait_recv()
        passed[2].start()

    def finish(ins, outs, sems):
        mine, direct, relays, passed, _, from_sibling = plan(ins[0], outs[0], *sems)
        for cp in from_sibling:
            cp.wait_recv()
        for cp in direct + relays + passed:
            cp.wait_send()
        mine.wait()

    return start, middle, late, finish


_GATHER_SEMS = [pltpu.SemaphoreType.DMA((8,)), pltpu.SemaphoreType.DMA((8,)), pltpu.SemaphoreType.DMA]


def _all_gather(shard, name):
    phases = _gather_phases(shard.shape[0])

    def body(x_ref, out_ref, *sems):
        for phase in phases:
            phase([x_ref], [out_ref], sems)

    return pl.pallas_call(
        body, name=name, out_shape=jax.ShapeDtypeStruct((N_DEV,) + shard.shape, shard.dtype),
        in_specs=[_ANY], out_specs=_ANY, scratch_shapes=list(_GATHER_SEMS),
    )(shard)


def _gather_hook(shard):
    start, middle, late, finish = _gather_phases(shard.shape[0])
    return _Hook([shard], [jax.ShapeDtypeStruct((N_DEV,) + shard.shape, shard.dtype)], _GATHER_SEMS,
                 [(0.0, start), (0.5, middle), (0.85, late), (1.0, finish)])


def _swap_with_sibling(parts, name):
    def body(p_ref, out_ref, send_sems, recv_sems):
        x, y, c, _ = _place()
        copies = [pltpu.make_async_remote_copy(
            src_ref=p_ref.at[2 * chip + (1 - c)], dst_ref=out_ref.at[chip],
            send_sem=send_sems.at[chip], recv_sem=recv_sems.at[chip],
            device_id=(x, y, 1 - c), device_id_type=MESH) for chip in range(4)]
        for cp in copies:
            cp.start()
        for cp in copies:
            cp.wait()

    return pl.pallas_call(
        body, name=name, out_shape=jax.ShapeDtypeStruct((4,) + parts.shape[1:], parts.dtype),
        in_specs=[pl.BlockSpec(memory_space=pl.ANY)], out_specs=pl.BlockSpec(memory_space=pl.ANY),
        scratch_shapes=[pltpu.SemaphoreType.DMA((4,)), pltpu.SemaphoreType.DMA((4,))],
    )(parts)


def _chips_phases(lo, rows):
    def copies(p_ref, out_ref, send_sems, recv_sems):
        x, y, c, chips = _place()
        return [pltpu.make_async_remote_copy(
            src_ref=p_ref.at[2 * px + py, pl.ds(lo, rows)], dst_ref=out_ref.at[j],
            send_sem=send_sems.at[j], recv_sem=recv_sems.at[j],
            device_id=(px, py, c), device_id_type=MESH) for j, (px, py) in enumerate(chips)]

    def start(ins, outs, sems):
        for cp in copies(ins[0], outs[0], *sems):
            cp.start()

    def finish(ins, outs, sems):
        for cp in copies(ins[0], outs[0], *sems):
            cp.wait()

    return start, finish


def _chips_hook(parts, lo=0, hi=None):
    rows = (parts.shape[1] if hi is None else hi) - lo
    start, finish = _chips_phases(lo, rows)
    return _Hook([parts], [jax.ShapeDtypeStruct((3, rows) + parts.shape[2:], parts.dtype)],
                 [pltpu.SemaphoreType.DMA((3,)), pltpu.SemaphoreType.DMA((3,))], [(0.0, start), (1.0, finish)])


BIG_LAYOUT = (("w_in_ab", D_MODEL, PACK_COLS), ("s5_w_glu", S5_W // N_DEV, PACK_COLS),
              ("w_out_ab", OUT_AB // N_DEV, 2 * PACK_COLS), ("w_in_c", D_MODEL, PACK_COLS),
              ("w_out_c", GLA_W // N_DEV, 2 * PACK_COLS))


def _to_rows(a):
    if a.shape[-1] == PACK_COLS:
        return a
    assert a.shape[-1] == 2 * PACK_COLS
    return jnp.concatenate([a[..., :PACK_COLS], a[..., PACK_COLS:]], axis=-2)


def _from_rows(p, cols):
    if cols == PACK_COLS:
        return p
    r = p.shape[-2] // 2
    return jnp.concatenate([p[..., :r, :], p[..., r:, :]], axis=-1)


FIRST_LAYOUT = BIG_LAYOUT[:1]
OTHER_LAYOUT = BIG_LAYOUT[1:3] + BIG_LAYOUT[4:]
GLU_AB_LAYOUT = BIG_LAYOUT[1:3]
IN_C_LAYOUT = BIG_LAYOUT[3:4]


def _pack_big(pieces, layout):
    return jnp.concatenate([_to_rows(pieces[name]) for name, _, _ in layout], axis=-2)


def _unpack_big(buf, layout):
    out, o = {}, 0
    for name, rows, cols in layout:
        r = rows * cols // PACK_COLS
        out[name] = _from_rows(buf[..., o:o + r, :], cols)
        o += r
    return out


def _column_windows(g):
    rows, quarter = g.shape[0], WIN_COLS // 4

    def body(g_ref, o_ref):
        o_ref[...] = g_ref[...]

    return pl.pallas_call(
        body, name="w_in_c_grad_windows", grid=(N_DEV, WIN_COLS // quarter),
        in_specs=[pl.BlockSpec((rows, quarter), lambda d, c: (0, (WIN_STEP // quarter) * d + c))],
        out_specs=pl.BlockSpec((None, rows, quarter), lambda d, c: (d, 0, c)),
        out_shape=jax.ShapeDtypeStruct((N_DEV, rows, WIN_COLS), g.dtype),
        compiler_params=_params(("arbitrary", "arbitrary")),
    )(g)


def _rows1024(a):
    r, c = a.shape
    if c > PACK_COLS:
        a = jnp.concatenate([a[:, i * PACK_COLS:(i + 1) * PACK_COLS] for i in range(c // PACK_COLS)], axis=0)
    elif c < PACK_COLS:
        a = jnp.pad(a, ((0, 0), (0, PACK_COLS - c)))
    return jnp.pad(a, ((0, -a.shape[0] % 8), (0, 0)))


def _unrows1024(p, r, c):
    if c > PACK_COLS:
        return jnp.concatenate([p[i * r:(i + 1) * r] for i in range(c // PACK_COLS)], axis=1)
    return p[:r, :c]


def _lane_select(a, off, sign, n_out, out_dtype, exact, name):
    rows, n_in = a.shape
    tr = _tile(rows, 256, 16)

    def body(off_ref, a_ref, o_ref):
        sel = _iota2((n_in, n_out), 0) + off_ref[0] * sign == _iota2((n_in, n_out), 1)
        if exact:
            r = jnp.dot(a_ref[...], sel.astype(F32), precision=lax.Precision.HIGHEST, preferred_element_type=F32)
        else:
            r = _dg(a_ref[...], sel.astype(BF16), 1, 0)
        o_ref[...] = r.astype(out_dtype)

    return pl.pallas_call(
        body, name=name, grid=(rows // tr,),
        in_specs=[pl.BlockSpec(memory_space=pltpu.SMEM), pl.BlockSpec((tr, n_in), lambda i: (i, 0))],
        out_specs=pl.BlockSpec((tr, n_out), lambda i: (i, 0)),
        out_shape=jax.ShapeDtypeStruct((rows, n_out), out_dtype),
        compiler_params=_params(("arbitrary",)),
    )(off, a)


def _adamw(w, g, m, v, name):
    rows, cols = w.shape
    tr = _tile(rows, 256, 8) if rows % 8 == 0 else rows

    def fn(i, w_, g_, m_, v_):
        m_new = ADAM_B1 * m_ + (1.0 - ADAM_B1) * g_
        v_new = ADAM_B2 * v_ + (1.0 - ADAM_B2) * (g_ * g_)
        m_hat = m_new / (1.0 - ADAM_B1 ** ADAM_STEP)
        v_hat = v_new / (1.0 - ADAM_B2 ** ADAM_STEP)
        delta = -ADAM_LR * (m_hat / (jnp.sqrt(v_hat) + ADAM_EPS) + ADAM_WD * w_)
        return (delta, m_new, v_new), ()

    outs, _ = _rows(fn, [_win(w), _win(g), _win(m), _win(v)], [], [(cols, F32)] * 3, [], name=name,
                    nrow=rows, tr=tr)
    return outs


def _as2d(a):
    if a.ndim == 1:
        return a.reshape(1, -1)
    if a.ndim == 2:
        return a
    a = a.reshape(a.shape[1:])
    return a if a.ndim == 2 else a.reshape(a.shape[0], -1)


def kernel(x, meta, norm_ab_w, w_in_ab, ret_norm_w, s5_lam_re, s5_lam_im, s5_log_dt, s5_b_re, s5_b_im, s5_c_re, s5_c_im, s5_d, s5_w_glu, w_out_ab, norm_c_w, w_in_c, gla_w_gate, gla_b_gate, gla_norm_w, w_out_c, final_norm_w, loss_target, m_meta, m_norm_ab_w, m_w_in_ab, m_ret_norm_w, m_s5_lam_re, m_s5_lam_im, m_s5_log_dt, m_s5_b_re, m_s5_b_im, m_s5_c_re, m_s5_c_im, m_s5_d, m_s5_w_glu, m_w_out_ab, m_norm_c_w, m_w_in_c, m_gla_w_gate, m_gla_b_gate, m_gla_norm_w, m_w_out_c, m_final_norm_w, v_meta, v_norm_ab_w, v_w_in_ab, v_ret_norm_w, v_s5_lam_re, v_s5_lam_im, v_s5_log_dt, v_s5_b_re, v_s5_b_im, v_s5_c_re, v_s5_c_im, v_s5_d, v_s5_w_glu, v_w_out_ab, v_norm_c_w, v_w_in_c, v_gla_w_gate, v_gla_b_gate, v_gla_norm_w, v_w_out_c, v_final_norm_w):
    weights = dict(meta=meta, norm_ab_w=norm_ab_w, w_in_ab=w_in_ab, ret_norm_w=ret_norm_w, s5_lam_re=s5_lam_re,
                   s5_lam_im=s5_lam_im, s5_log_dt=s5_log_dt, s5_b_re=s5_b_re, s5_b_im=s5_b_im, s5_c_re=s5_c_re,
                   s5_c_im=s5_c_im, s5_d=s5_d, s5_w_glu=s5_w_glu, w_out_ab=w_out_ab, norm_c_w=norm_c_w,
                   w_in_c=w_in_c, gla_w_gate=gla_w_gate, gla_b_gate=gla_b_gate, gla_norm_w=gla_norm_w,
                   w_out_c=w_out_c, final_norm_w=final_norm_w)
    mom_m = dict(meta=m_meta, norm_ab_w=m_norm_ab_w, w_in_ab=m_w_in_ab, ret_norm_w=m_ret_norm_w,
                 s5_lam_re=m_s5_lam_re, s5_lam_im=m_s5_lam_im, s5_log_dt=m_s5_log_dt, s5_b_re=m_s5_b_re,
                 s5_b_im=m_s5_b_im, s5_c_re=m_s5_c_re, s5_c_im=m_s5_c_im, s5_d=m_s5_d, s5_w_glu=m_s5_w_glu,
                 w_out_ab=m_w_out_ab, norm_c_w=m_norm_c_w, w_in_c=m_w_in_c, gla_w_gate=m_gla_w_gate,
                 gla_b_gate=m_gla_b_gate, gla_norm_w=m_gla_norm_w, w_out_c=m_w_out_c, final_norm_w=m_final_norm_w)
    mom_v = dict(meta=v_meta, norm_ab_w=v_norm_ab_w, w_in_ab=v_w_in_ab, ret_norm_w=v_ret_norm_w,
                 s5_lam_re=v_s5_lam_re, s5_lam_im=v_s5_lam_im, s5_log_dt=v_s5_log_dt, s5_b_re=v_s5_b_re,
                 s5_b_im=v_s5_b_im, s5_c_re=v_s5_c_re, s5_c_im=v_s5_c_im, s5_d=v_s5_d, s5_w_glu=v_s5_w_glu,
                 w_out_ab=v_w_out_ab, norm_c_w=v_norm_c_w, w_in_c=v_w_in_c, gla_w_gate=v_gla_w_gate,
                 gla_b_gate=v_gla_b_gate, gla_norm_w=v_gla_norm_w, w_out_c=v_w_out_c, final_norm_w=v_final_norm_w)
    order = list(weights)

    seq = x.shape[1]
    lp = CHUNK + seq
    nchunk = lp // CHUNK
    dev = 4 * lax.axis_index("x") + 2 * lax.axis_index("y") + lax.axis_index("c")
    core = lax.axis_index("c")
    chip = 2 * lax.axis_index("x") + lax.axis_index("y")

    win_off = jnp.reshape(2 * dev, (1,)).astype(jnp.int32)
    shard_c = jnp.pad(w_in_c[0].astype(BF16), ((0, 0), (0, 896 - SHARD_C)))
    big_shards = dict(w_in_ab=w_in_ab[0].astype(BF16), s5_w_glu=s5_w_glu[0].astype(BF16),
                      w_out_ab=w_out_ab[0].astype(BF16), w_out_c=w_out_c[0].astype(BF16),
                      w_in_c=_lane_select(shard_c, win_off, 1, WIN_COLS, BF16, False, "w_in_c_to_window"))
    def pad_to(a, rows, cols):
        return jnp.pad(a, ((0, rows - a.shape[0]), (0, cols - a.shape[1])))

    shard_w = D_MODEL // N_DEV
    small_pack = jnp.concatenate([meta, pad_to(norm_c_w, 8, shard_w), pad_to(gla_w_gate[0], GLA_RANK, shard_w),
                                  pad_to(gla_b_gate, 8, shard_w), pad_to(gla_norm_w, 8, shard_w)], axis=0)
    w_in_ab_g = _all_gather(big_shards["w_in_ab"], "gather_first")
    win_cut = 1408
    in_c_hook_a = _gather_hook(big_shards["w_in_c"][:win_cut])
    in_c_hook_b = _gather_hook(big_shards["w_in_c"][win_cut:])
    glu_ab_hook = _gather_hook(_pack_big(big_shards, GLU_AB_LAYOUT))
    out_c_hook = _gather_hook(_to_rows(big_shards["w_out_c"]))
    gs = _all_gather(small_pack, "gather_small")
    gate_w = GLA_QK // N_DEV
    s_meta, s_norm_c = gs[:, :N_META], gs[:, N_META]
    s_wgate, s_bgate, s_gnorm = gs[:, 24:24 + GLA_RANK, :gate_w], gs[:, 40, :gate_w], gs[:, 48]
    meta_f = s_meta.transpose(1, 0, 2).reshape(N_META, D_MODEL)
    norm_c_f = s_norm_c.reshape(1, D_MODEL)
    w_gate_f = jnp.pad(s_wgate.transpose(1, 0, 2).reshape(GLA_RANK, GLA_QK), ((0, GATE_PAD - GLA_RANK), (0, 0)))
    b_gate_f = s_bgate.reshape(1, GLA_QK)
    gla_norm_f = s_gnorm.reshape(GLA_H, 1, GLA_DV)

    pos = jnp.maximum(jnp.arange(lp, dtype=F32) - float(PAD), 0.0)
    inv_freq = jnp.power(ROPE_BASE, -jnp.arange(0, RET_DK, 2, dtype=F32) / RET_DK)
    ang = pos[:, None] * inv_freq[None, :]
    cos2 = jnp.concatenate([jnp.cos(ang), jnp.cos(ang)], axis=1)
    sin2 = jnp.concatenate([-jnp.sin(ang), jnp.sin(ang)], axis=1)
    log_g = jnp.log1p(-jnp.exp2(-5.0 - jnp.arange(RET_H, dtype=F32)))
    lg = jnp.broadcast_to(log_g[:, None, None], (RET_H, 1, 128))
    ret_norm_h = ret_norm_w.reshape(RET_H, 1, RET_DV)

    h0 = jnp.concatenate([jnp.zeros((PAD, D_MODEL), F32), meta_f, x[0]], axis=0)

    def rowmask(i):
        return (_iota2((CHUNK, 1), 0) + i * CHUNK) >= PAD

    (hn0,), _ = _rows(lambda i, h, w: ((_rms(h, w),), ()), [_win(h0)], [norm_ab_w], [(D_MODEL, BF16)], [],
                      name="norm_ab_fwd", nrow=lp)
    proj_ab, (w_in_c_ga,) = _mm(hn0, w_in_ab_g, "nn", name="in_ab_fwd", hook=in_c_hook_a, b_dev=True)

    q_off, k_off, v_off, za_off = 0, RET_QK, 2 * RET_QK, 2 * RET_QK + RET_W
    u_off, zb_off = 2 * RET_QK + 2 * RET_W, 2 * RET_QK + 2 * RET_W + S5_W
    ret_xs = [(proj_ab, RET_DK, lambda h: q_off // RET_DK + h), (proj_ab, RET_DK, lambda h: k_off // RET_DK + h),
              (proj_ab, RET_DV, lambda h: v_off // RET_DV + h), (proj_ab, RET_DV, lambda h: za_off // RET_DV + h)]
    ret_cs = [(cos2, RET_DK, lambda h: 0), (sin2, RET_DK, lambda h: 0)]
    ret_kw = dict(heads=RET_H, nchunk=nchunk, s_shape=(RET_DK, RET_DV), out_w=RET_DV, pre=_ret_pre, hpb=RET_H)
    o_a, ret_sprev, (gathered_glu_ab,) = _scan_fwd(_ret_chunk, ret_xs, ret_cs, [ret_norm_h], [lg], name="ret_fwd",
                                                   hook=glu_ab_hook, **ret_kw)
    gb = _unpack_big(gathered_glu_ab, GLU_AB_LAYOUT)
    w_glu_f = gb["s5_w_glu"].reshape(S5_W, S5_W)
    w_out_ab_f = gb["w_out_ab"].reshape(OUT_AB, D_MODEL)

    expand = jnp.repeat(jnp.eye(S5_P, dtype=F32), S5_GH, axis=1)
    disc_args = (s5_lam_re[0], s5_lam_im[0], s5_log_dt[0].reshape(S5_G, 1),
                 s5_b_re[0].reshape(S5_G, S5_P * S5_GH), s5_b_im[0].reshape(S5_G, S5_P * S5_GH), expand)
    ab_re, ab_im, bb_re, bb_im = _s5_disc_fwd(disc_args)
    gt = TILE_G
    eye_t = jnp.eye(gt, dtype=F32)

    def tiles_in(bb):
        return jnp.einsum("sgph,gk->sghkp", bb.reshape(gt, gt, S5_P, S5_GH), eye_t).reshape(gt, 128, TILE_W)

    def tiles_out(cc):
        return jnp.einsum("sghp,gk->sgpkh", cc.reshape(gt, gt, S5_GH, S5_P), eye_t).reshape(gt, TILE_W, 128)

    wb_t = jnp.concatenate([tiles_in(bb_re), tiles_in(bb_im)], axis=0).astype(BF16)
    wc_t = jnp.concatenate([tiles_out(s5_c_re[0]), -tiles_out(s5_c_im[0])], axis=0).astype(BF16)
    a_re, a_im = ab_re.reshape(SCAN_ROWS, SCAN_LANES), ab_im.reshape(SCAN_ROWS, SCAN_LANES)
    tm5, tk5, nt5 = _tile(lp, 1408, 8), _tile(lp, 1408, 8), 2 * gt
    u_blk = u_off // 128
    wide_k = pl.BlockSpec((tm5, TILE_W), lambda i, j, k: (i, k * gt + j))
    narrow = pl.BlockSpec((tm5, 128), lambda i, j, k: (i, j))
    wb_k = pl.BlockSpec((None, 128, TILE_W), lambda i, j, k: (k * gt + j, 0, 0))
    wc_k = pl.BlockSpec((None, TILE_W, 128), lambda i, j, k: (k * gt + j, 0, 0))
    bu3, (w_out_c_g,) = _s5_expand(proj_ab, wb_t, a_blk=u_off // S5_W, dims=NN, name="s5_bu", hook=out_c_hook)
    w_out_c_f = _from_rows(w_out_c_g, D_MODEL).reshape(GLA_W, D_MODEL)
    xs5 = _s5_scan_fwd(bu3, a_re, a_im)
    xs5_2d = xs5.reshape(lp, 2 * S5_N)
    y_pre, (w_in_c_gb,) = _mm_core(xs5_2d, wc_t, dims=NN, grid=(lp // tm5, gt, 2), name="s5_cx", a_spec=wide_k,
                                   b_spec=wc_k, o_spec=narrow, out_shape=jax.ShapeDtypeStruct((lp, S5_W), F32),
                                   acc_shape=(tm5, 128), hook=in_c_hook_b)
    (y_s5, yg_bf), _ = _rows(
        lambda i, yp, u, d: ((yp + d * u, _gelu(yp + d * u)), ()),
        [_win(y_pre), _win(proj_ab, u_off, S5_W)], [s5_d], [(S5_W, F32), (S5_W, BF16)], [], name="s5_gelu_fwd", nrow=lp)
    t_glu = _mm(yg_bf, w_glu_f, "nn", name="s5_glu_fwd")

    def s5_gate(y, t, zb):
        return _gelu(y) * _sigmoid(t) * _silu(zb)

    (o_b,), _ = _rows(lambda i, y, t, zb: ((s5_gate(y, t, zb),), ()),
                      [_win(y_s5), _win(t_glu), _win(proj_ab, zb_off, S5_W)], [], [(S5_W, BF16)], [],
                      name="s5_gate_fwd", nrow=lp)
    o_ab = jnp.concatenate([o_a, o_b], axis=1)
    h1 = _mm(o_ab, w_out_ab_f, "nn", name="out_ab_fwd", add=h0)
    w_in_c_g = jnp.concatenate([w_in_c_ga, w_in_c_gb], axis=1)
    w_in_c_f = sum(jnp.pad(w_in_c_g[d], ((0, 0), (WIN_STEP * d, IN_C_PAD - WIN_STEP * d - WIN_COLS)))
                   for d in range(N_DEV))

    (hn1,), _ = _rows(lambda i, h, w: ((_rms(h, w),), ()), [_win(h1)], [norm_c_f], [(D_MODEL, BF16)], [],
                      name="norm_c_fwd", nrow=lp)
    proj_c = _mm(hn1, w_in_c_f, "nn", name="in_c_fwd")
    gl_off = 2 * GLA_QK + 2 * GLA_W
    pre_gate = _mm(proj_c, w_gate_f, "nn", name="gate_fwd", a_win=(gl_off, GATE_PAD), bias=b_gate_f)
    gla_xs = [(proj_c, GLA_DK, lambda h: h), (proj_c, GLA_DK, lambda h: GLA_QK // GLA_DK + h),
              (proj_c, GLA_DV, lambda h: 2 * GLA_QK // GLA_DV + h),
              (proj_c, GLA_DV, lambda h: (2 * GLA_QK + GLA_W) // GLA_DV + h),
              (pre_gate, GLA_DK, lambda h: h)]
    gla_kw = dict(heads=GLA_H, nchunk=nchunk, s_shape=(GLA_DV, GLA_DK), out_w=GLA_DV, hpb=GLA_H)
    o_c, gla_sprev = _scan_fwd(functools.partial(_gla_chunk, cumsum=_cumsum_rows_f32), gla_xs, [], [gla_norm_f], [],
                               name="gla_fwd", **gla_kw)
    h2 = _mm(o_c, w_out_c_f, "nn", name="out_c_fwd", add=h1)

    fnw = final_norm_w.reshape(1, D_MODEL)

    def final_fn(i, h, tgt, w):
        def loss_of(h_, w_):
            err = _rms(h_, w_) - tgt
            return 0.5 * jnp.sum(jnp.mean(err * err, axis=-1))

        real = (i > 0).astype(F32)
        loss_i, (dh, dw) = jax.value_and_grad(loss_of, argnums=(0, 1))(h, w)
        return (dh * real, dh * real), (jnp.full((1, 128), loss_i * real, F32), dw * real)

    (dh2, dh2_bf), (loss_acc, g_final) = _rows(
        final_fn, [_win(h2), _win(loss_target[0], roff=1)], [fnw], [(D_MODEL, F32), (D_MODEL, BF16)],
        [(1, 128), (1, D_MODEL)], name="final_loss", nrow=lp)

    def rs_front(pieces, layout, tag):
        g_full = _pack_big(pieces, layout)
        prow = g_full.shape[1]
        from_sibling = _swap_with_sibling(g_full, "rs_sibling_" + tag)
        mine_by_chip = lax.dynamic_index_in_dim(g_full.reshape(4, 2, prow, PACK_COLS), core, axis=1, keepdims=False)
        (p1, p1_bf), _ = _rows(
            lambda i, a, b: ((a.astype(F32) + b.astype(F32), a.astype(F32) + b.astype(F32)), ()),
            [_win(mine_by_chip.reshape(4 * prow, PACK_COLS)), _win(from_sibling.reshape(4 * prow, PACK_COLS))], [],
            [(PACK_COLS, F32), (PACK_COLS, BF16)], [], name="rs_sum_sibling_" + tag, nrow=4 * prow,
            tr=_tile(prow, 512, 16))
        return p1.reshape(4, prow, PACK_COLS), p1_bf.reshape(4, prow, PACK_COLS)

    def rs_back(p1, from_chips, layout, tag):
        prow = p1.shape[1]
        tr = _tile(prow, 512, 16)
        own = lax.dynamic_index_in_dim(p1, chip, axis=0, keepdims=False)
        fc2 = from_chips.reshape(3 * prow, PACK_COLS)
        nblk = prow // tr
        (g_shard,), _ = _rows(
            lambda i, a, b0, b1, b2: ((((a + b0.astype(F32)) + b1.astype(F32)) + b2.astype(F32),), ()),
            [_win(own), _win(fc2), _win(fc2, roff=-nblk), _win(fc2, roff=-2 * nblk)], [], [(PACK_COLS, F32)], [],
            name="rs_sum_chips_" + tag, nrow=prow, tr=tr)
        return _unpack_big(g_shard, layout)

    do_c = _mm(dh2_bf, w_out_c_f, "nt", name="out_c_dx", out_dtype=BF16)
    gw_out_c = _mm(o_c, dh2_bf, "tn", name="out_c_dw", out_dtype=BF16)
    (dq_c, dk_c, dv_c, dz_c, dpre), (g_gla_norm,) = _scan_bwd(
        _gla_chunk, gla_xs, [], [gla_norm_f], [], do_c, gla_sprev, name="gla_bwd", **gla_kw)
    dglow = _mm(dpre, w_gate_f, "nt", name="gate_dx", out_dtype=BF16)
    g_wgate = _mm(proj_c, dpre, "tn", name="gate_dw", a_win=(gl_off, GATE_PAD))[:GLA_RANK]
    (), (g_bgate,) = _rows(lambda i, d: ((), (jnp.sum(d.astype(F32), axis=0, keepdims=True),)), [_win(dpre)], [], [],
                           [(1, GLA_QK)], name="gate_db", nrow=lp)
    dproj_c = jnp.concatenate([dq_c, dk_c, dv_c, dz_c, dglow], axis=1)
    dhn1 = _mm(dproj_c, w_in_c_f, "nt", name="in_c_dx")
    gw_in_c = _mm(hn1, dproj_c, "tn", name="in_c_dw", out_dtype=BF16)
    p1_c, p1_c_bf = rs_front(dict(
        w_in_c=_column_windows(gw_in_c)),
        IN_C_LAYOUT, "in_c")

    def norm_bwd(i, h, dhn, dres, w):
        _, vjp = jax.vjp(_rms, h, w)
        dh, dw = vjp(dhn)
        return (jnp.where(rowmask(i), dh + dres, 0.0),), (dw,)

    def norm_bwd_both(i, h, dhn, dres, w):
        (dh,), acc = norm_bwd(i, h, dhn, dres, w)
        return (dh, dh), acc

    (dh1, dh1_bf), (g_norm_c,) = _rows(norm_bwd_both, [_win(h1), _win(dhn1), _win(dh2)], [norm_c_f],
                                       [(D_MODEL, F32), (D_MODEL, BF16)], [(1, D_MODEL)], name="norm_c_bwd", nrow=lp)

    do_ab = _mm(dh1_bf, w_out_ab_f, "nt", name="out_ab_dx", out_dtype=BF16)
    gw_out_ab = _mm(o_ab, dh1_bf, "tn", name="out_ab_dw", out_dtype=BF16)

    def s5_gate_bwd(i, dob, y, t, zb):
        _, vjp = jax.vjp(s5_gate, y, t, zb)
        dy, dt, dzb = vjp(dob.astype(F32))
        return (dy, dt, dzb), ()

    (dy_a, dt_glu, dzb), _ = _rows(
        s5_gate_bwd, [_win(do_ab, RET_W, S5_W), _win(y_s5), _win(t_glu), _win(proj_ab, zb_off, S5_W)], [],
        [(S5_W, F32), (S5_W, BF16), (S5_W, BF16)], [], name="s5_gate_bwd", nrow=lp)
    dyg2 = _mm(dt_glu, w_glu_f, "nt", name="s5_glu_dx")
    gw_glu = _mm(yg_bf, dt_glu, "tn", name="s5_glu_dw", out_dtype=BF16)

    def s5_y_bwd(i, dya, dyg, y, u, d):
        _, vjp = jax.vjp(_gelu, y)
        (dy_g,) = vjp(dyg)
        dy = dya + dy_g
        return (dy, d * dy), (jnp.sum(dy * u, axis=0, keepdims=True),)

    (dy_s5, du1), (g_d,) = _rows(
        s5_y_bwd, [_win(dy_a), _win(dyg2), _win(y_s5), _win(proj_ab, u_off, S5_W)], [s5_d],
        [(S5_W, BF16), (S5_W, F32)], [(1, S5_W)], name="s5_y_bwd", nrow=lp)
    p1_o, p1_o_bf = rs_front(dict(s5_w_glu=gw_glu.reshape(N_DEV, S5_W // N_DEV, S5_W),
                                  w_out_ab=gw_out_ab.reshape(N_DEV, OUT_AB // N_DEV, D_MODEL),
                                  w_out_c=gw_out_c.reshape(N_DEV, GLA_W // N_DEV, D_MODEL)), OTHER_LAYOUT, "other")
    o_cut = 640
    gx3, (from_chips_oa,) = _s5_expand(dy_s5, wc_t, a_blk=0, dims=NT, name="s5_cx_dx",
                                       hook=_chips_hook(p1_o_bf, 0, o_cut))
    rows_k = lambda col: pl.BlockSpec((tk5, col), lambda i, j, k: (k, i))
    gwc = _mm_core(xs5_2d, dy_s5, dims=TN, grid=(nt5, 1, lp // tk5), name="s5_cx_dw", a_spec=rows_k(TILE_W),
                   b_spec=pl.BlockSpec((tk5, 128), lambda i, j, k: (k, i % gt)),
                   o_spec=pl.BlockSpec((None, TILE_W, 128), lambda i, j, k: (i, 0, 0)),
                   out_shape=jax.ShapeDtypeStruct((nt5, TILE_W, 128), F32), acc_shape=(TILE_W, 128))
    g_s5, da = _s5_scan_bwd(gx3, xs5, a_re, a_im)
    g_s5_2d = g_s5.reshape(lp, 2 * S5_N)
    du, (from_chips_ob,) = _mm_core(g_s5_2d, wb_t, dims=NT, grid=(lp // tm5, gt, 2), name="s5_bu_dx", a_spec=wide_k,
                                    b_spec=wb_k, o_spec=narrow, out_shape=jax.ShapeDtypeStruct((lp, S5_W), BF16),
                                    acc_shape=(tm5, 128), extra=[(du1, narrow)],
                                    hook=_chips_hook(p1_o_bf, o_cut, None))
    from_chips_o = jnp.concatenate([from_chips_oa, from_chips_ob], axis=1)
    gwb = _mm_core(proj_ab, g_s5_2d, dims=TN, grid=(nt5, 1, lp // tk5), name="s5_bu_dw",
                   a_spec=pl.BlockSpec((tk5, 128), lambda i, j, k: (k, u_blk + i % gt)), b_spec=rows_k(TILE_W),
                   o_spec=pl.BlockSpec((None, 128, TILE_W), lambda i, j, k: (i, 0, 0)),
                   out_shape=jax.ShapeDtypeStruct((nt5, 128, TILE_W), F32), acc_shape=(128, TILE_W))
    gwc6 = gwc.reshape(2, gt, gt, S5_P, gt, S5_GH)
    g_c = jnp.einsum("rsgpgh->rsghp", gwc6).reshape(2, S5_G, S5_GH, S5_P)
    g_c_re, g_c_im = g_c[0], -g_c[1]
    gwb6 = gwb.reshape(2, gt, gt, S5_GH, gt, S5_P)
    d_bb = jnp.einsum("rsghgp->rsgph", gwb6).reshape(2, S5_G, S5_P * S5_GH)
    d_bb_re, d_bb_im = d_bb[0], d_bb[1]
    g_lam_re, g_lam_im, g_log_dt, g_b_re, g_b_im = _s5_disc_bwd(
        disc_args, (da[0].reshape(S5_G, S5_P), da[1].reshape(S5_G, S5_P), d_bb_re, d_bb_im))

    (dq_a, dk_a, dv_a, dz_a), (g_ret_norm,), (from_chips_c,) = _scan_bwd(
        _ret_chunk, ret_xs, ret_cs, [ret_norm_h], [lg], do_ab, ret_sprev, name="ret_bwd", post=_ret_post,
        hook=_chips_hook(p1_c_bf), **ret_kw)
    dproj_ab = jnp.concatenate([dq_a, dk_a, dv_a, dz_a, du, dzb], axis=1)

    lane = lambda a_: pad_to(a_, a_.shape[0], 128)

    def sum8(i, *blocks):
        acc = blocks[0]
        for b in blocks[1:]:
            acc = acc + b
        return (acc,), ()

    def pack_small(pieces):
        return jnp.concatenate([_rows1024(p) for _, p in pieces], axis=0)

    def sum_small(gathered, pieces, tag):
        srow = gathered.shape[1]
        tr = _tile(srow, 128, 8)
        flat = gathered.reshape(N_DEV * srow, PACK_COLS)
        (total,), _ = _rows(sum8, [_win(flat, roff=-d * (srow // tr)) for d in range(N_DEV)], [], [(PACK_COLS, F32)],
                            [], name="sum_small_" + tag, nrow=srow, tr=tr)
        out, o = {}, 0
        for name_, p in pieces:
            r8 = _rows1024(p).shape[0]
            out[name_] = _unrows1024(total[o:o + r8], *p.shape)
            o += r8
        return out

    early_pieces = [
        ("vec2048", jnp.concatenate([g_final, g_norm_c], axis=0)),
        ("vec1024", jnp.concatenate([g_d, g_bgate, pad_to(loss_acc[:, :1], 1, PACK_COLS)], axis=0)),
        ("lam3", jnp.concatenate([lane(g_lam_re), lane(g_lam_im), lane(g_log_dt)], axis=1)),
        ("s5_b_re", g_b_re), ("s5_b_im", g_b_im),
        ("s5_c_re", g_c_re.reshape(S5_G, S5_GH * S5_P)), ("s5_c_im", g_c_im.reshape(S5_G, S5_GH * S5_P)),
        ("ret_norm_w", g_ret_norm.reshape(RET_H, RET_DV)), ("gla_norm_w", g_gla_norm.reshape(GLA_H, GLA_DV)),
        ("gla_w_gate", g_wgate)]
    gw_in_ab, (early_all,) = _mm(
        hn0, dproj_ab, "tn", name="in_ab_dw", out_dest=True, out_dtype=BF16,
        hook=_gather_hook(pack_small(early_pieces)))
    p1_first, p1_first_bf = rs_front(dict(w_in_ab=gw_in_ab), FIRST_LAYOUT, "first")
    dhn0, (from_chips_first,) = _mm(dproj_ab, w_in_ab_g, "nt", name="in_ab_dx", b_dev=True,
                                    hook=_chips_hook(p1_first_bf))
    def norm_bwd_first(i, h, dhn, dres, w):
        (dh,), (dw,) = norm_bwd(i, h, dhn, dres, w)
        return (dh,), (dw, dh * (i == 0).astype(F32))

    (grad_x2d,), (g_norm_ab, dh0_first) = _rows(
        norm_bwd_first, [_win(h0), _win(dhn0), _win(dh1)], [norm_ab_w], [(D_MODEL, F32, 1)],
        [(1, D_MODEL), (CHUNK, D_MODEL)], name="norm_ab_bwd", nrow=lp)
    grad_x = grad_x2d[None]
    late_pieces = [("norm_ab_w", g_norm_ab), ("meta", dh0_first[PAD:CHUNK])]
    small = sum_small(early_all, early_pieces, "early")
    small.update(sum_small(_all_gather(pack_small(late_pieces), "gather_grads"), late_pieces, "late"))

    big_grads = {**rs_back(p1_c, from_chips_c, IN_C_LAYOUT, "in_c"), **rs_back(p1_o, from_chips_o, OTHER_LAYOUT, "other"),
                 **rs_back(p1_first, from_chips_first, FIRST_LAYOUT, "first")}
    big_grads["w_in_c"] = _lane_select(big_grads["w_in_c"], win_off, -1, 896, F32, True,
                                       "w_in_c_from_window")[:, :SHARD_C]
    small["final_norm_w"], small["norm_c_w"] = small["vec2048"][0:1], small["vec2048"][1:2]
    small["s5_d"], small["gla_b_gate"] = small["vec1024"][0:1], small["vec1024"][1:2]
    loss = small["vec1024"][2, 0]
    small["s5_lam_re"], small["s5_lam_im"] = small["lam3"][:, :S5_P], small["lam3"][:, 128:128 + S5_P]
    small["s5_log_dt"] = small["lam3"][:, 256:257]

    def my_cols(g, n):
        return lax.dynamic_slice_in_dim(g, dev * n, n, axis=g.ndim - 1)

    grads = dict(
        meta=my_cols(small["meta"], D_MODEL // N_DEV),
        norm_ab_w=small["norm_ab_w"], w_in_ab=big_grads["w_in_ab"][None], ret_norm_w=small["ret_norm_w"].reshape(1, RET_W),
        s5_lam_re=small["s5_lam_re"][None], s5_lam_im=small["s5_lam_im"][None],
        s5_log_dt=small["s5_log_dt"].reshape(1, S5_G),
        s5_b_re=small["s5_b_re"].reshape(1, S5_G, S5_P, S5_GH), s5_b_im=small["s5_b_im"].reshape(1, S5_G, S5_P, S5_GH),
        s5_c_re=small["s5_c_re"][None], s5_c_im=small["s5_c_im"][None], s5_d=small["s5_d"],
        s5_w_glu=big_grads["s5_w_glu"][None], w_out_ab=big_grads["w_out_ab"][None],
        norm_c_w=my_cols(small["norm_c_w"], D_MODEL // N_DEV), w_in_c=big_grads["w_in_c"][None],
        gla_w_gate=my_cols(small["gla_w_gate"], GLA_QK // N_DEV)[None],
        gla_b_gate=my_cols(small["gla_b_gate"], GLA_QK // N_DEV),
        gla_norm_w=my_cols(small["gla_norm_w"].reshape(1, GLA_W), GLA_W // N_DEV),
        w_out_c=big_grads["w_out_c"][None], final_norm_w=small["final_norm_w"].reshape(D_MODEL))

    deltas, new_m, new_v = {}, {}, {}
    for k in order:
        w = weights[k]
        d2, m2, v2 = _adamw(_as2d(w), _as2d(grads[k].reshape(w.shape)), _as2d(mom_m[k]), _as2d(mom_v[k]), "adamw_" + k)
        deltas[k], new_m[k], new_v[k] = d2.reshape(w.shape), m2.reshape(w.shape), v2.reshape(w.shape)
        grads[k] = grads[k].reshape(w.shape)

    return (loss, grad_x, *[grads[k] for k in order], *[deltas[k] for k in order],
            *[new_m[k] for k in order], *[new_v[k] for k in order])
```
